```python
import jax, jax.numpy as jnp
from jax import lax
import numpy as np

D_MODEL = 1024
BATCH = 2
SEQ = 8192
DEPTH = 1

GRID_W = 64
CTX_LEN = 256
D_MIX = 1024
D_FOURIER = 256
N_FOURIER_GROUPS = 4
FOURIER_GROUP = D_FOURIER // N_FOURIER_GROUPS
HEAD_DIM = 64
N_NA_HEADS = 12
D_NA = N_NA_HEADS * HEAD_DIM
NA_ROWS_MAX = 8
NA_COLS = 16
ROPE_THETA = 10000.0
N_GROUPS = 4
EXPERTS_PER_GROUP = 8
N_EXPERTS = N_GROUPS * EXPERTS_PER_GROUP
TOP_K = 2
D_EXPERT = 512
MOE_BLOCK = 128
N_MOD = 6
D_IN_PROJ = D_FOURIER + 3 * D_NA
EPS = 1e-6

kernel_name = "hymba_fnet_natten_hmoe_dit"


def rmsnorm(x, g):
    xf = x.astype(jnp.float32)
    r = lax.rsqrt(jnp.mean(xf * xf, axis=-1, keepdims=True) + EPS)
    return (xf * r * g.astype(jnp.float32)).astype(x.dtype)


def adaln(cond, w, b):
    mod = jax.nn.silu(cond) @ w + b
    return [m[:, None, :] for m in jnp.split(mod, N_MOD, axis=-1)]


def modulate(h, shift, scale):
    return h * (1.0 + scale) + shift


def fourier_mix(f, w_fmix):
    b, n, _ = f.shape
    fg = f.astype(jnp.float32).reshape(b, n, N_FOURIER_GROUPS, FOURIER_GROUP)
    fr = jnp.fft.fft2(fg, axes=(1, 3), norm="ortho").real
    return fr.reshape(b, n, D_FOURIER).astype(f.dtype) @ w_fmix


def _rotate(x, pos):
    half = x.shape[-1] // 2
    inv = ROPE_THETA ** (-jnp.arange(half, dtype=jnp.float32) / half)
    ang = pos.astype(jnp.float32)[:, None] * inv[None, :]
    cos = jnp.cos(ang)[None, :, None, :]
    sin = jnp.sin(ang)[None, :, None, :]
    x1, x2 = x[..., :half], x[..., half:]
    return jnp.concatenate([x1 * cos - x2 * sin, x1 * sin + x2 * cos], axis=-1)


def axial_rope(x):
    n = x.shape[1]
    t = jnp.arange(n)
    row, col = t // GRID_W, t % GRID_W
    xf = x.astype(jnp.float32)
    r = HEAD_DIM // 2
    out = jnp.concatenate([_rotate(xf[..., :r], row), _rotate(xf[..., r:], col)], axis=-1)
    return out.astype(x.dtype)


def neighbourhood_attention(q, k, v, k_ctx, v_ctx, rpb):
    b, n, h, hd = q.shape
    rows = n // GRID_W
    ks_r = min(NA_ROWS_MAX, rows)
    scale = HEAD_DIM ** -0.5
    qr = axial_rope(q).reshape(b, rows, GRID_W, h, hd)
    kg = axial_rope(k).reshape(b, rows, GRID_W, h, hd)
    vg = v.reshape(b, rows, GRID_W, h, hd)
    qp = q.reshape(b, rows, GRID_W, h, hd)
    r = jnp.arange(rows)
    row_start = jnp.clip(r - ks_r // 2, 0, rows - ks_r)
    row_idx = row_start[:, None] + jnp.arange(ks_r)[None, :]
    k_blk = kg[:, row_idx]
    v_blk = vg[:, row_idx]
    s_loc = jnp.einsum('brqhd,brkwhd->bhrqkw', qr, k_blk,
                       preferred_element_type=jnp.float32) * scale
    cidx = jnp.arange(GRID_W)
    col_start = jnp.clip(cidx - NA_COLS // 2, 0, GRID_W - NA_COLS)
    col_valid = (cidx[None, :] >= col_start[:, None]) & (cidx[None, :] < col_start[:, None] + NA_COLS)
    dr = row_idx - r[:, None] + (NA_ROWS_MAX - 1)
    dc = jnp.clip(cidx[None, :] - cidx[:, None] + (NA_COLS - 1), 0, 2 * NA_COLS - 2)
    bias = rpb[:, dr[:, None, :, None], dc[None, :, None, :]]
    s_loc = s_loc + bias[None].astype(jnp.float32)
    s_loc = jnp.where(col_valid[None, None, None, :, None, :], s_loc, jnp.finfo(jnp.float32).min)
    s_loc = s_loc.reshape(b, h, rows, GRID_W, ks_r * GRID_W)
    s_ctx = jnp.einsum('brqhd,blhd->bhrql', qp, k_ctx,
                       preferred_element_type=jnp.float32) * scale
    p = jax.nn.softmax(jnp.concatenate([s_loc, s_ctx], axis=-1), axis=-1)
    n_loc = ks_r * GRID_W
    p_loc = p[..., :n_loc].reshape(b, h, rows, GRID_W, ks_r, GRID_W).astype(v.dtype)
    p_ctx = p[..., n_loc:].astype(v.dtype)
    out = (jnp.einsum('bhrqkw,brkwhd->brqhd', p_loc, v_blk)
           + jnp.einsum('bhrql,blhd->brqhd', p_ctx, v_ctx))
    return out.reshape(b, n, h * hd)


def context_attention(q, k, v):
    b, l, h, hd = q.shape
    s = jnp.einsum('blhd,bmhd->bhlm', q, k, preferred_element_type=jnp.float32) * (HEAD_DIM ** -0.5)
    p = jax.nn.softmax(s, axis=-1).astype(v.dtype)
    return jnp.einsum('bhlm,bmhd->blhd', p, v).reshape(b, l, h * hd)


def hierarchical_moe(h2, w_rg, w_re, w_gate, w_up, w_down):
    n, d = h2.shape
    lg = (h2 @ w_rg).astype(jnp.float32)
    pg = jax.nn.softmax(lg, axis=-1)
    g_sel = jnp.argmax(lg, axis=-1)
    le = (h2 @ w_re).astype(jnp.float32).reshape(n, N_GROUPS, EXPERTS_PER_GROUP)
    le_sel = jnp.take_along_axis(le, g_sel[:, None, None], axis=1)[:, 0]
    pe = jax.nn.softmax(le_sel, axis=-1)
    top_w, top_i = lax.top_k(pe, TOP_K)
    top_w = top_w / jnp.sum(top_w, axis=-1, keepdims=True)
    gate = top_w * jnp.take_along_axis(pg, g_sel[:, None], axis=1)
    eid = g_sel[:, None] * EXPERTS_PER_GROUP + top_i
    m = n * TOP_K
    e_flat = eid.reshape(-1)
    tok_flat = jnp.repeat(jnp.arange(n, dtype=jnp.int32), TOP_K)
    gate_flat = gate.reshape(-1)
    order = jnp.argsort(e_flat)
    e_s, tok_s, gate_s = e_flat[order], tok_flat[order], gate_flat[order]
    counts = jnp.bincount(e_flat, length=N_EXPERTS)
    starts = jnp.cumsum(counts) - counts
    pcounts = ((counts + MOE_BLOCK - 1) // MOE_BLOCK) * MOE_BLOCK
    pends = jnp.cumsum(pcounts)
    pstarts = pends - pcounts
    dest = pstarts[e_s] + (jnp.arange(m) - starts[e_s])
    n_blocks = -(-m // MOE_BLOCK) + N_EXPERTS
    p_slots = n_blocks * MOE_BLOCK
    slot_tok = jnp.zeros((p_slots,), jnp.int32).at[dest].set(tok_s)
    slot_gate = jnp.zeros((p_slots,), jnp.float32).at[dest].set(gate_s)
    block_expert = jnp.minimum(
        jnp.searchsorted(pends, jnp.arange(n_blocks) * MOE_BLOCK, side='right'), N_EXPERTS - 1)
    xs = h2[slot_tok].reshape(n_blocks, MOE_BLOCK, d)

    def expert_block(args):
        xb, e = args
        return (jax.nn.silu(xb @ w_gate[e]) * (xb @ w_up[e])) @ w_down[e]

    ys = lax.map(expert_block, (xs, block_expert)).reshape(p_slots, d)
    ys = ys * slot_gate[:, None].astype(ys.dtype)
    return jnp.zeros((n, d), h2.dtype).at[slot_tok].add(ys)


def setup_inputs(seed: int = 0) -> dict:
    key = jax.random.key(seed)
    ks = jax.random.split(key, 20)
    f32 = jnp.float32
    nrm = lambda k, shape, s: jax.random.normal(k, shape, f32) * s
    return {
        "x": nrm(ks[0], (BATCH, SEQ, D_MODEL), 1.0),
        "c": nrm(ks[1], (BATCH, D_MODEL), 1.0),
        "ctx": nrm(ks[2], (BATCH, CTX_LEN, D_MODEL), 1.0),
        "c_ctx": nrm(ks[3], (D_MODEL,), 1.0),
        "w_ada": nrm(ks[4], (DEPTH, D_MODEL, N_MOD * D_MODEL), 0.5 * D_MODEL ** -0.5),
        "b_ada": nrm(ks[5], (DEPTH, N_MOD * D_MODEL), 0.02),
        "g_norm1": 1.0 + nrm(ks[6], (DEPTH, D_MODEL), 0.05),
        "w_in": nrm(ks[7], (DEPTH, D_MODEL, D_IN_PROJ), D_MODEL ** -0.5),
        "w_fmix": nrm(ks[8], (DEPTH, D_FOURIER, D_FOURIER), D_FOURIER ** -0.5),
        "rpb": nrm(ks[9], (DEPTH, N_NA_HEADS, 2 * NA_ROWS_MAX - 1, 2 * NA_COLS - 1), 0.2),
        "g_out": 1.0 + nrm(ks[10], (DEPTH, D_MIX), 0.05),
        "w_out": nrm(ks[11], (DEPTH, D_MIX, D_MODEL), D_MIX ** -0.5),
        "g_norm2": 1.0 + nrm(ks[12], (DEPTH, D_MODEL), 0.05),
        "w_router_group": nrm(ks[13], (DEPTH, D_MODEL, N_GROUPS), D_MODEL ** -0.5),
        "w_router_expert": nrm(ks[14], (DEPTH, D_MODEL, N_EXPERTS), D_MODEL ** -0.5),
        "w_gate": nrm(ks[15], (DEPTH, N_EXPERTS, D_MODEL, D_EXPERT), D_MODEL ** -0.5),
        "w_up": nrm(ks[16], (DEPTH, N_EXPERTS, D_MODEL, D_EXPERT), D_MODEL ** -0.5),
        "w_down": nrm(ks[17], (DEPTH, N_EXPERTS, D_EXPERT, D_MODEL), D_EXPERT ** -0.5),
        "g_final": 1.0 + nrm(ks[18], (D_MODEL,), 0.05),
    }


def reference(x, c, ctx, c_ctx, w_ada, b_ada, g_norm1, w_in, w_fmix, rpb, g_out, w_out, g_norm2,
              w_router_group, w_router_expert, w_gate, w_up, w_down, g_final):
    b, n, d = x.shape
    l = ctx.shape[1]
    xc = ctx
    q_end = D_FOURIER + D_NA
    k_end = D_FOURIER + 2 * D_NA
    for i in range(DEPTH):
        last = i == DEPTH - 1
        sh1, sc1, gt1, sh2, sc2, gt2 = adaln(c, w_ada[i], b_ada[i])
        csh1, csc1, cgt1, csh2, csc2, cgt2 = adaln(c_ctx[None, :], w_ada[i], b_ada[i])

        h = modulate(rmsnorm(x, g_norm1[i]), sh1, sc1)
        hc = modulate(rmsnorm(xc, g_norm1[i]), csh1, csc1)
        p = h @ w_in[i]
        pc = hc @ w_in[i]
        q = p[..., D_FOURIER:q_end].reshape(b, n, N_NA_HEADS, HEAD_DIM)
        k = p[..., q_end:k_end].reshape(b, n, N_NA_HEADS, HEAD_DIM)
        v = p[..., k_end:].reshape(b, n, N_NA_HEADS, HEAD_DIM)
        kc = pc[..., q_end:k_end].reshape(b, l, N_NA_HEADS, HEAD_DIM)
        vc = pc[..., k_end:].reshape(b, l, N_NA_HEADS, HEAD_DIM)
        f_out = fourier_mix(p[..., :D_FOURIER], w_fmix[i])
        na_out = neighbourhood_attention(q, k, v, kc, vc, rpb[i])
        y = jnp.concatenate([rmsnorm(f_out, g_out[i][:D_FOURIER]),
                             rmsnorm(na_out, g_out[i][D_FOURIER:])], axis=-1) @ w_out[i]
        x = x + gt1 * y
        if not last:
            qc = pc[..., D_FOURIER:q_end].reshape(b, l, N_NA_HEADS, HEAD_DIM)
            fc_out = fourier_mix(pc[..., :D_FOURIER], w_fmix[i])
            nac_out = context_attention(qc, kc, vc)
            yc = jnp.concatenate([rmsnorm(fc_out, g_out[i][:D_FOURIER]),
                                  rmsnorm(nac_out, g_out[i][D_FOURIER:])], axis=-1) @ w_out[i]
            xc = xc + cgt1 * yc

        h2 = modulate(rmsnorm(x, g_norm2[i]), sh2, sc2)
        moe = hierarchical_moe(h2.reshape(b * n, d), w_router_group[i], w_router_expert[i],
                               w_gate[i], w_up[i], w_down[i]).reshape(b, n, d)
        x = x + gt2 * moe
        if not last:
            h2c = modulate(rmsnorm(xc, g_norm2[i]), csh2, csc2)
            moec = hierarchical_moe(h2c.reshape(b * l, d), w_router_group[i], w_router_expert[i],
                                    w_gate[i], w_up[i], w_down[i]).reshape(b, l, d)
            xc = xc + cgt2 * moec
    return rmsnorm(x, g_final)
```

```python
import functools
import math

import numpy as np
import jax
import jax.numpy as jnp
from jax import lax
from jax.experimental import pallas as pl
from jax.experimental.pallas import tpu as pltpu

F32 = jnp.float32
BF16 = jnp.bfloat16

D_MODEL = 1024
GRID_W = 64
GRID_H = 128
SEQ = GRID_W * GRID_H
CTX_LEN = 256
D_FOURIER = 256
FOURIER_GROUP = 64
HEAD_DIM = 64
N_HEADS = 12
D_NA = N_HEADS * HEAD_DIM
N_PAIRS = N_HEADS // 2
NA_ROWS = 8
NA_COLS = 16
ROPE_THETA = 10000.0
N_GROUPS = 4
EXPERTS_PER_GROUP = 8
N_EXPERTS = N_GROUPS * EXPERTS_PER_GROUP
D_EXPERT = 512
N_MOD = 6
D_IN_PROJ = D_FOURIER + 3 * D_NA
EPS = 1e-6
LANES = 128
NEG = -1e30

TM_PROJ = 512
ATT_ROWS = 16
DFT_TW = 8
DFT_TN = 4096
MOE_BLK = 256
TC_COMB = 256
VMEM_LIMIT = 56 * 1024 * 1024


def _cparams(sem):
    return pltpu.CompilerParams(dimension_semantics=sem, vmem_limit_bytes=VMEM_LIMIT)


@functools.lru_cache(maxsize=None)
def _rope_tables():
    t = np.arange(SEQ)
    row, col = t // GRID_W, t % GRID_W
    lane = np.arange(LANES)
    d = lane % HEAD_DIM
    chunk = d // 32
    e = d % 32
    j = e % 16
    inv = ROPE_THETA ** (-(j.astype(np.float64)) / 16.0)
    pos = np.where(chunk[None, :] == 0, row[:, None], col[:, None]).astype(np.float64)
    ang = pos * inv[None, :]
    cos = np.cos(ang)
    sin = np.sin(ang)
    first = (e < 16)[None, :]
    s_first = np.where(first, -sin, 0.0)
    s_second = np.where(first, 0.0, sin)
    return (cos.astype(np.float32), s_first.astype(np.float32), s_second.astype(np.float32))


@functools.lru_cache(maxsize=None)
def _chan_dft():
    c = np.arange(FOURIER_GROUP)
    ang = 2.0 * np.pi * ((c[:, None] * c[None, :]) % FOURIER_GROUP) / FOURIER_GROUP
    eye = np.eye(D_FOURIER // FOURIER_GROUP)
    re = np.kron(eye, np.cos(ang))
    im = np.kron(eye, -np.sin(ang))
    return np.concatenate([re, im], axis=1).astype(np.float32)


@functools.lru_cache(maxsize=None)
def _row_dft():
    k1 = np.arange(GRID_H)[:, None]
    r = np.arange(GRID_H)[None, :]
    out = np.zeros((GRID_W, 2 * GRID_H, 2 * GRID_H), np.float32)
    for w in range(GRID_W):
        m = (k1 * (GRID_W * r + w)) % SEQ
        ang = 2.0 * np.pi * m / SEQ
        c, s = np.cos(ang), np.sin(ang)
        out[w] = np.block([[c, s], [-s, c]])
    return out


@functools.lru_cache(maxsize=None)
def _col_dft():
    k2 = np.arange(GRID_W)
    ang = 2.0 * np.pi * ((k2[:, None] * k2[None, :]) % GRID_W) / GRID_W
    scale = 1.0 / math.sqrt(SEQ * FOURIER_GROUP)
    return (np.concatenate([np.cos(ang), np.sin(ang)], axis=1) * scale).astype(np.float32)


@functools.lru_cache(maxsize=None)
def _bias_index():
    c = np.arange(GRID_W)
    start = np.clip(c - NA_COLS // 2, 0, GRID_W - NA_COLS)
    valid = (c[None, :] >= start[:, None]) & (c[None, :] < start[:, None] + NA_COLS)
    dc = np.clip(c[None, :] - c[:, None] + (NA_COLS - 1), 0, 2 * NA_COLS - 2)
    return dc.astype(np.int32), valid


@functools.lru_cache(maxsize=None)
def _strict_lower(n):
    return np.tril(np.ones((n, n), np.float32), k=-1)


def _adaln_kernel(c_ref, w_ref, b_ref, o_ref):
    c = c_ref[...]
    s = c * jax.nn.sigmoid(c)
    o_ref[...] = jnp.dot(s, w_ref[...], precision=lax.Precision.HIGHEST,
                         preferred_element_type=F32) + b_ref[...]


def adaln(cond8, w, b):
    n = w.shape[1]
    tn = 1536
    return pl.pallas_call(
        _adaln_kernel,
        grid=(n // tn,),
        in_specs=[pl.BlockSpec((8, D_MODEL), lambda j: (0, 0)),
                  pl.BlockSpec((D_MODEL, tn), lambda j: (0, j)),
                  pl.BlockSpec((1, tn), lambda j: (0, j))],
        out_specs=pl.BlockSpec((8, tn), lambda j: (0, j)),
        out_shape=jax.ShapeDtypeStruct((8, n), F32),
        compiler_params=_cparams(("arbitrary",)),
        name="adaln",
    )(cond8, w, b.reshape(1, n))


def _norm_mod(x, g, shift, scale):
    ms = jnp.mean(x * x, axis=-1, keepdims=True)
    return (x * lax.rsqrt(ms + EPS) * g) * (1.0 + scale) + shift


def _in_proj_kernel(x_ref, mod_ref, g_ref, w_ref, cs_ref, cos_ref, s1_ref, s2_ref,
                    qr_ref, qp_ref, kr_ref, v_ref, a_ref, h_scr):
    m = mod_ref[0]
    h_scr[...] = _norm_mod(x_ref[0], g_ref[...], m[0:1], m[1:2]).astype(BF16)
    cos, s1, s2 = cos_ref[...], s1_ref[...], s2_ref[...]

    def rope(t):
        return (t * cos + pltpu.roll(t, LANES - 16, axis=1) * s1 + pltpu.roll(t, 16, axis=1) * s2)

    f = jnp.dot(h_scr[...], w_ref[:, 0:D_FOURIER], preferred_element_type=F32)
    a = jnp.dot(f.astype(BF16), cs_ref[...], preferred_element_type=F32)
    a_ref[0, 0] = a[:, :D_FOURIER].astype(BF16)
    a_ref[0, 1] = a[:, D_FOURIER:].astype(BF16)

    scale = HEAD_DIM ** -0.5
    for c in range(D_NA // 256):
        lo = D_FOURIER + 256 * c
        q = jnp.dot(h_scr[...], w_ref[:, lo:lo + 256], preferred_element_type=F32)
        k = jnp.dot(h_scr[...], w_ref[:, lo + D_NA:lo + D_NA + 256], preferred_element_type=F32)
        v = jnp.dot(h_scr[...], w_ref[:, lo + 2 * D_NA:lo + 2 * D_NA + 256], preferred_element_type=F32)
        v_ref[0, :, 256 * c:256 * c + 256] = v.astype(BF16)
        for s in range(2):
            sl = slice(LANES * s, LANES * (s + 1))
            ol = slice(256 * c + LANES * s, 256 * c + LANES * (s + 1))
            qs, ks = q[:, sl], k[:, sl]
            qp_ref[0, :, ol] = (qs * scale).astype(BF16)
            qr_ref[0, :, ol] = (rope(qs) * scale).astype(BF16)
            kr_ref[0, :, ol] = rope(ks).astype(BF16)


def in_proj(x, mods, g1, w_in_bf16):
    b, n, d = x.shape
    tm = TM_PROJ
    cos, s1, s2 = _rope_tables()
    cs = jnp.asarray(_chan_dft(), BF16)
    tok = lambda bi, i: (bi, i, 0)
    const2 = lambda bi, i: (0, 0)
    tab = pl.BlockSpec((tm, LANES), lambda bi, i: (i, 0))
    qkv_shape = jax.ShapeDtypeStruct((b, n, D_NA), BF16)
    qkv_spec = pl.BlockSpec((1, tm, D_NA), tok)
    return pl.pallas_call(
        _in_proj_kernel,
        grid=(b, n // tm),
        in_specs=[pl.BlockSpec((1, tm, d), tok),
                  pl.BlockSpec((1, 8, d), lambda bi, i: (bi, 0, 0)),
                  pl.BlockSpec((1, d), const2),
                  pl.BlockSpec((d, D_IN_PROJ), const2),
                  pl.BlockSpec((D_FOURIER, 2 * D_FOURIER), const2),
                  tab, tab, tab],
        out_specs=[qkv_spec, qkv_spec, qkv_spec, qkv_spec,
                   pl.BlockSpec((1, 2, tm, D_FOURIER), lambda bi, i: (bi, 0, i, 0))],
        out_shape=[qkv_shape, qkv_shape, qkv_shape, qkv_shape,
                   jax.ShapeDtypeStruct((b, 2, n, D_FOURIER), BF16)],
        scratch_shapes=[pltpu.VMEM((tm, d), BF16)],
        compiler_params=_cparams(("arbitrary", "arbitrary")),
        name="in_proj",
    )(x, mods, g1.reshape(1, d), w_in_bf16, cs, jnp.asarray(cos), jnp.asarray(s1), jnp.asarray(s2))


def _ctx_proj_kernel(x_ref, mod_ref, g_ref, w_ref, k_ref, v_ref):
    m = mod_ref[...]
    h = _norm_mod(x_ref[0], g_ref[...], m[0:1], m[1:2]).astype(BF16)
    k_ref[0] = jnp.dot(h, w_ref[:, 0:D_NA], preferred_element_type=F32).astype(BF16)
    v_ref[0] = jnp.dot(h, w_ref[:, D_NA:2 * D_NA], preferred_element_type=F32).astype(BF16)


def ctx_proj(ctx, mod_ctx, g1, w_kv_bf16):
    b, l, d = ctx.shape
    shape = jax.ShapeDtypeStruct((b, l, D_NA), BF16)
    spec = pl.BlockSpec((1, l, D_NA), lambda bi: (bi, 0, 0))
    return pl.pallas_call(
        _ctx_proj_kernel,
        grid=(b,),
        in_specs=[pl.BlockSpec((1, l, d), lambda bi: (bi, 0, 0)),
                  pl.BlockSpec((8, d), lambda bi: (0, 0)),
                  pl.BlockSpec((1, d), lambda bi: (0, 0)),
                  pl.BlockSpec((d, 2 * D_NA), lambda bi: (0, 0))],
        out_specs=[spec, spec],
        out_shape=[shape, shape],
        compiler_params=_cparams(("arbitrary",)),
        name="ctx_proj",
    )(ctx, mod_ctx, g1.reshape(1, d), w_kv_bf16)


def _dft_rows_kernel(a_ref, g_ref, z_ref):
    for j in range(DFT_TW):
        sl = slice(D_FOURIER * j, D_FOURIER * (j + 1))
        rhs = jnp.concatenate([a_ref[0, 0, :, sl], a_ref[0, 1, :, sl]], axis=0)
        z = jnp.dot(g_ref[j], rhs, preferred_element_type=F32)
        z_ref[0, 0, j] = z[:GRID_H].astype(BF16)
        z_ref[0, 1, j] = z[GRID_H:].astype(BF16)


def dft_rows(a):
    b = a.shape[0]
    a4 = a.reshape(b, 2, GRID_H, GRID_W * D_FOURIER)
    g = jnp.asarray(_row_dft(), BF16)
    return pl.pallas_call(
        _dft_rows_kernel,
        grid=(GRID_W // DFT_TW, b),
        in_specs=[pl.BlockSpec((1, 2, GRID_H, DFT_TW * D_FOURIER), lambda j, bi: (bi, 0, 0, j)),
                  pl.BlockSpec((DFT_TW, 2 * GRID_H, 2 * GRID_H), lambda j, bi: (j, 0, 0))],
        out_specs=pl.BlockSpec((1, 2, DFT_TW, GRID_H, D_FOURIER), lambda j, bi: (bi, 0, j, 0, 0)),
        out_shape=jax.ShapeDtypeStruct((b, 2, GRID_W, GRID_H, D_FOURIER), BF16),
        compiler_params=_cparams(("arbitrary", "arbitrary")),
        name="dft_rows",
    )(a4, g)


def _dft_cols_kernel(z_ref, cs_ref, o_ref):
    rhs = jnp.concatenate([z_ref[0, 0], z_ref[0, 1]], axis=0)
    o_ref[0] = jnp.dot(cs_ref[...], rhs, preferred_element_type=F32).astype(BF16)


def dft_cols(z):
    b = z.shape[0]
    ncol = GRID_H * D_FOURIER
    z4 = z.reshape(b, 2, GRID_W, ncol)
    cs = jnp.asarray(_col_dft(), BF16)
    out = pl.pallas_call(
        _dft_cols_kernel,
        grid=(b, ncol // DFT_TN),
        in_specs=[pl.BlockSpec((1, 2, GRID_W, DFT_TN), lambda bi, j: (bi, 0, 0, j)),
                  pl.BlockSpec((GRID_W, 2 * GRID_W), lambda bi, j: (0, 0))],
        out_specs=pl.BlockSpec((1, GRID_W, DFT_TN), lambda bi, j: (bi, 0, j)),
        out_shape=jax.ShapeDtypeStruct((b, GRID_W, ncol), BF16),
        compiler_params=_cparams(("arbitrary", "arbitrary")),
        name="dft_cols",
    )(z4, cs)
    return out.reshape(b, SEQ, D_FOURIER)


def _attention_kernel(qr_ref, qp_ref, k_ref, v_ref, kc_ref, vc_ref, t2_ref, o_ref):
    rb = pl.program_id(2)
    lane = lax.broadcasted_iota(jnp.int32, (GRID_W, LANES), 1)
    first = lane < HEAD_DIM
    kc = kc_ref[0]
    vc = vc_ref[0]
    nt = (((1,), (1,)), ((), ()))

    def split(q):
        zero = jnp.zeros_like(q)
        return jnp.concatenate([jnp.where(first, q, zero), jnp.where(first, zero, q)], axis=0)

    def body(i, carry):
        r = rb * ATT_ROWS + i
        rs = jnp.clip(r - NA_ROWS // 2, 0, GRID_H - NA_ROWS)
        pat = r - rs
        qoff = pl.multiple_of(i * GRID_W, GRID_W)
        koff = pl.multiple_of(rs * GRID_W, GRID_W)
        qm = split(qr_ref[0, pl.ds(qoff, GRID_W), :])
        qpm = split(qp_ref[0, pl.ds(qoff, GRID_W), :])
        kw = k_ref[0, pl.ds(koff, NA_ROWS * GRID_W), :]
        vw = v_ref[0, pl.ds(koff, NA_ROWS * GRID_W), :]
        s = lax.dot_general(qm, kw, nt, preferred_element_type=F32)
        sc = lax.dot_general(qpm, kc, nt, preferred_element_type=F32)
        d0 = (NA_ROWS - 1) - pat
        bias = jnp.concatenate(
            [jnp.concatenate([t2_ref[0, hh, d0 + 2 * t] for t in range(NA_ROWS // 2)], axis=1)
             for hh in range(2)], axis=0)
        s = s + bias
        m = jnp.maximum(jnp.max(s, axis=1, keepdims=True), jnp.max(sc, axis=1, keepdims=True))
        p = jnp.exp(s - m)
        pc = jnp.exp(sc - m)
        l = jnp.sum(p, axis=1, keepdims=True) + jnp.sum(pc, axis=1, keepdims=True)
        o = (jnp.dot(p.astype(BF16), vw, preferred_element_type=F32)
             + jnp.dot(pc.astype(BF16), vc, preferred_element_type=F32))
        o = o / l
        o_ref[0, pl.ds(qoff, GRID_W), :] = jnp.where(first, o[:GRID_W], o[GRID_W:]).astype(BF16)
        return carry

    lax.fori_loop(0, ATT_ROWS, body, 0)


def bias_tables(rpb):
    dc, valid = _bias_index()
    t = jnp.where(valid[None, None], rpb[:, :, dc], NEG)
    t2 = jnp.concatenate([t[:, :-1], t[:, 1:]], axis=-1)
    return t2.reshape(N_PAIRS, 2, 2 * NA_ROWS - 2, GRID_W, LANES).astype(F32)


def attention(qr, qp, kr, v, kc, vc, t2):
    b, n, _ = qr.shape
    tq = ATT_ROWS * GRID_W
    qspec = pl.BlockSpec((1, tq, LANES), lambda bi, hp, i: (bi, i, hp))
    kspec = pl.BlockSpec((1, n, LANES), lambda bi, hp, i: (bi, 0, hp))
    cspec = pl.BlockSpec((1, CTX_LEN, LANES), lambda bi, hp, i: (bi, 0, hp))
    return pl.pallas_call(
        _attention_kernel,
        grid=(b, N_PAIRS, GRID_H // ATT_ROWS),
        in_specs=[qspec, qspec, kspec, kspec, cspec, cspec,
                  pl.BlockSpec((1, 2, 2 * NA_ROWS - 2, GRID_W, LANES), lambda bi, hp, i: (hp, 0, 0, 0, 0))],
        out_specs=qspec,
        out_shape=jax.ShapeDtypeStruct((b, n, D_NA), BF16),
        compiler_params=_cparams(("arbitrary", "arbitrary", "arbitrary")),
        name="attention",
    )(qr, qp, kr, v, kc, vc, t2)


def _rms(x, g):
    ms = jnp.mean(x * x, axis=-1, keepdims=True)
    return x * lax.rsqrt(ms + EPS) * g


def _out_proj_kernel(fr_ref, na_ref, x_ref, mod_ref, wf_ref, wo_ref, go_ref, g2_ref, wr_ref, tri_ref,
                     x1_ref, h2_ref, meta_ref, cnt_ref, run_scr):
    @pl.when((pl.program_id(0) == 0) & (pl.program_id(1) == 0))
    def _():
        run_scr[...] = jnp.zeros_like(run_scr)

    m = mod_ref[0]
    go = go_ref[...]
    fo = jnp.dot(fr_ref[0], wf_ref[...], preferred_element_type=F32)
    fn = _rms(fo, go[:, :D_FOURIER]).astype(BF16)
    nn = _rms(na_ref[0].astype(F32), go[:, D_FOURIER:]).astype(BF16)
    y = (jnp.dot(fn, wo_ref[0:D_FOURIER, :], preferred_element_type=F32)
         + jnp.dot(nn, wo_ref[D_FOURIER:, :], preferred_element_type=F32))
    x1 = x_ref[0] + m[2:3] * y
    x1_ref[0] = x1
    h2 = _norm_mod(x1, g2_ref[...], m[3:4], m[4:5])
    h2_ref[0] = h2
    logits = jnp.dot(h2.astype(BF16), wr_ref[...], preferred_element_type=F32)

    tm = logits.shape[0]
    lane = lax.broadcasted_iota(jnp.int32, (tm, LANES), 1).astype(F32)
    ninf = jnp.float32(-jnp.inf)

    def argmax_first(vals):
        mx = jnp.max(vals, axis=1, keepdims=True)
        idx = jnp.min(jnp.where(vals == mx, lane, float(LANES)), axis=1, keepdims=True)
        return mx, idx

    lg = jnp.where(lane < N_GROUPS, logits, ninf)
    gmax, gidx = argmax_first(lg)
    pg = 1.0 / jnp.sum(jnp.exp(lg - gmax), axis=1, keepdims=True)
    lo = N_GROUPS + EXPERTS_PER_GROUP * gidx
    le = jnp.where((lane >= lo) & (lane < lo + EXPERTS_PER_GROUP), logits, ninf)
    e1, i1 = argmax_first(le)
    e2, i2 = argmax_first(jnp.where(lane == i1, ninf, le))
    dd = jnp.exp(e2 - e1)
    gate1 = pg / (1.0 + dd)
    gate2 = pg * dd / (1.0 + dd)

    hot1 = lane == i1
    hot2 = lane == i2
    onehot = jnp.where(hot1 | hot2, 1.0, 0.0)
    cnt = jnp.dot(tri_ref[...], onehot.astype(BF16), preferred_element_type=F32) + run_scr[...]
    rank1 = jnp.sum(jnp.where(hot1, cnt, 0.0), axis=1, keepdims=True)
    rank2 = jnp.sum(jnp.where(hot2, cnt, 0.0), axis=1, keepdims=True)
    run_scr[...] = run_scr[...] + jnp.sum(onehot, axis=0, keepdims=True)

    meta = jnp.where(lane == 0, i1 - N_GROUPS,
           jnp.where(lane == 1, i2 - N_GROUPS,
           jnp.where(lane == 2, rank1,
           jnp.where(lane == 3, rank2,
           jnp.where(lane == 4, gate1,
           jnp.where(lane == 5, gate2, 0.0))))))
    meta_ref[...] = meta
    cnt_ref[...] = jnp.broadcast_to(run_scr[...], cnt_ref.shape)


def out_proj(fr, na, x, mods, w_fmix_bf16, w_out_bf16, g_out, g2, w_router_bf16):
    b, n, d = x.shape
    tm = TM_PROJ
    tok = lambda bi, i: (bi, i, 0)
    const2 = lambda bi, i: (0, 0)
    tri = jnp.asarray(_strict_lower(tm), BF16)
    return pl.pallas_call(
        _out_proj_kernel,
        grid=(b, n // tm),
        in_specs=[pl.BlockSpec((1, tm, D_FOURIER), tok),
                  pl.BlockSpec((1, tm, D_NA), tok),
                  pl.BlockSpec((1, tm, d), tok),
                  pl.BlockSpec((1, 8, d), lambda bi, i: (bi, 0, 0)),
                  pl.BlockSpec((D_FOURIER, D_FOURIER), const2),
                  pl.BlockSpec((d, d), const2),
                  pl.BlockSpec((1, d), const2),
                  pl.BlockSpec((1, d), const2),
                  pl.BlockSpec((d, LANES), const2),
                  pl.BlockSpec((tm, tm), const2)],
        out_specs=[pl.BlockSpec((1, tm, d), tok),
                   pl.BlockSpec((1, tm, d), tok),
                   pl.BlockSpec((tm, LANES), lambda bi, i: (bi * (n // tm) + i, 0)),
                   pl.BlockSpec((8, LANES), const2)],
        out_shape=[jax.ShapeDtypeStruct((b, n, d), F32),
                   jax.ShapeDtypeStruct((b, n, d), F32),
                   jax.ShapeDtypeStruct((b * n, LANES), F32),
                   jax.ShapeDtypeStruct((8, LANES), F32)],
        scratch_shapes=[pltpu.VMEM((1, LANES), F32)],
        compiler_params=_cparams(("arbitrary", "arbitrary")),
        name="out_proj",
    )(fr, na, x, mods, w_fmix_bf16, w_out_bf16, g_out.reshape(1, d), g2.reshape(1, d), w_router_bf16, tri)


def _experts_kernel(be_ref, nused_ref, tok_ref, h2_hbm, wg_ref, wu_ref, wd_ref, ys_ref,
                    xbuf, wg_scr, wu_scr, wd_scr, sem):
    i = pl.program_id(0)
    nblk = pl.num_programs(0)
    nused = nused_ref[0]

    def gather(blk, slot):
        base = blk * MOE_BLK

        def issue(s, carry):
            tok = tok_ref[base + s]
            pltpu.make_async_copy(h2_hbm.at[pl.ds(tok, 1)], xbuf.at[slot, pl.ds(s, 1)], sem.at[slot]).start()
            return carry

        lax.fori_loop(0, MOE_BLK, issue, 0, unroll=8)

    @pl.when((i == 0) & (nused > 0))
    def _():
        gather(0, 0)

    @pl.when((i + 1 < nblk) & (i + 1 < nused))
    def _():
        gather(i + 1, (i + 1) % 2)

    changed = (i == 0) | (be_ref[i] != be_ref[jnp.maximum(i - 1, 0)])

    @pl.when(changed & (i < nused))
    def _():
        wg_scr[...] = wg_ref[0].astype(BF16)
        wu_scr[...] = wu_ref[0].astype(BF16)
        wd_scr[...] = wd_ref[0].astype(BF16)

    @pl.when(i < nused)
    def _():
        slot = i % 2
        pltpu.make_async_copy(h2_hbm.at[pl.ds(0, MOE_BLK)], xbuf.at[slot], sem.at[slot]).wait()
        x = xbuf[slot].astype(BF16)
        g = jnp.dot(x, wg_scr[...], preferred_element_type=F32)
        u = jnp.dot(x, wu_scr[...], preferred_element_type=F32)
        hmid = (g * jax.nn.sigmoid(g) * u).astype(BF16)
        ys_ref[...] = jnp.dot(hmid, wd_scr[...], preferred_element_type=F32)

    @pl.when(i >= nused)
    def _():
        ys_ref[...] = jnp.zeros_like(ys_ref)


def experts(h2, block_expert, nused, slot_tok, w_gate, w_up, w_down):
    nt, d = h2.shape
    nblk = block_expert.shape[0]
    grid_spec = pltpu.PrefetchScalarGridSpec(
        num_scalar_prefetch=3,
        grid=(nblk,),
        in_specs=[pl.BlockSpec(memory_space=pl.ANY),
                  pl.BlockSpec((1, d, D_EXPERT), lambda i, be, nu, tk: (be[i], 0, 0)),
                  pl.BlockSpec((1, d, D_EXPERT), lambda i, be, nu, tk: (be[i], 0, 0)),
                  pl.BlockSpec((1, D_EXPERT, d), lambda i, be, nu, tk: (be[i], 0, 0))],
        out_specs=pl.BlockSpec((MOE_BLK, d), lambda i, be, nu, tk: (i, 0)),
        scratch_shapes=[pltpu.VMEM((2, MOE_BLK, d), F32),
                        pltpu.VMEM((d, D_EXPERT), BF16),
                        pltpu.VMEM((d, D_EXPERT), BF16),
                        pltpu.VMEM((D_EXPERT, d), BF16),
                        pltpu.SemaphoreType.DMA((2,))],
    )
    return pl.pallas_call(
        _experts_kernel,
        grid_spec=grid_spec,
        out_shape=jax.ShapeDtypeStruct((nblk * MOE_BLK, d), F32),
        compiler_params=_cparams(("arbitrary",)),
        name="experts",
    )(block_expert, nused, slot_tok, h2, w_gate, w_up, w_down)


def _combine_kernel(dest_ref, ys_hbm, x1_ref, meta_ref, mod_ref, gf_ref, o_ref, ybuf, sem):
    i = pl.program_id(0)
    nstep = pl.num_programs(0)
    tc = TC_COMB

    def gather(step, slot):
        base = step * (2 * tc)

        def issue(t, carry):
            d0 = dest_ref[base + 2 * t]
            d1 = dest_ref[base + 2 * t + 1]
            pltpu.make_async_copy(ys_hbm.at[pl.ds(d0, 1)], ybuf.at[slot, pl.ds(t, 1)], sem.at[slot]).start()
            pltpu.make_async_copy(ys_hbm.at[pl.ds(d1, 1)], ybuf.at[slot, pl.ds(tc + t, 1)], sem.at[slot]).start()
            return carry

        lax.fori_loop(0, tc, issue, 0, unroll=4)

    @pl.when(i == 0)
    def _():
        gather(0, 0)

    @pl.when(i + 1 < nstep)
    def _():
        gather(i + 1, (i + 1) % 2)

    slot = i % 2
    pltpu.make_async_copy(ys_hbm.at[pl.ds(0, 2 * tc)], ybuf.at[slot], sem.at[slot]).wait()
    meta = meta_ref[...]
    moe = ybuf[slot, 0:tc] * meta[:, 4:5] + ybuf[slot, tc:2 * tc] * meta[:, 5:6]
    x2 = x1_ref[...] + mod_ref[0][5:6] * moe
    o_ref[...] = _rms(x2, gf_ref[...])


def combine(dest_flat, ys, x1_flat, meta, mods, g_final, n_per_batch):
    nt, d = x1_flat.shape
    tc = TC_COMB
    per_b = n_per_batch // tc
    grid_spec = pltpu.PrefetchScalarGridSpec(
        num_scalar_prefetch=1,
        grid=(nt // tc,),
        in_specs=[pl.BlockSpec(memory_space=pl.ANY),
                  pl.BlockSpec((tc, d), lambda i, ds: (i, 0)),
                  pl.BlockSpec((tc, LANES), lambda i, ds: (i, 0)),
                  pl.BlockSpec((1, 8, d), lambda i, ds: (i // per_b, 0, 0)),
                  pl.BlockSpec((1, d), lambda i, ds: (0, 0))],
        out_specs=pl.BlockSpec((tc, d), lambda i, ds: (i, 0)),
        scratch_shapes=[pltpu.VMEM((2, 2 * tc, d), F32),
                        pltpu.SemaphoreType.DMA((2,))],
    )
    return pl.pallas_call(
        _combine_kernel,
        grid_spec=grid_spec,
        out_shape=jax.ShapeDtypeStruct((nt, d), F32),
        compiler_params=_cparams(("arbitrary",)),
        name="combine",
    )(dest_flat, ys, x1_flat, meta, mods, g_final.reshape(1, d))


def _dispatch_plan(meta, counts_row, nt):
    e = meta[:, 0:2].astype(jnp.int32)
    rank = meta[:, 2:4].astype(jnp.int32)
    counts = counts_row[N_GROUPS:N_GROUPS + N_EXPERTS].astype(jnp.int32)
    pcounts = ((counts + MOE_BLK - 1) // MOE_BLK) * MOE_BLK
    pends = jnp.cumsum(pcounts)
    pstarts = pends - pcounts
    dest = pstarts[e] + rank
    nblk = (nt * 2) // MOE_BLK + N_EXPERTS
    tok = jnp.repeat(jnp.arange(nt, dtype=jnp.int32), 2)
    slot_tok = jnp.zeros((nblk * MOE_BLK,), jnp.int32).at[dest.reshape(-1)].set(tok)
    block_expert = jnp.minimum(
        jnp.searchsorted(pends, jnp.arange(nblk, dtype=jnp.int32) * MOE_BLK, side="right"),
        N_EXPERTS - 1).astype(jnp.int32)
    nused = (pends[-1] // MOE_BLK).astype(jnp.int32).reshape(1)
    return dest.reshape(-1).astype(jnp.int32), slot_tok, block_expert, nused


def kernel(x, c, ctx, c_ctx, w_ada, b_ada, g_norm1, w_in, w_fmix, rpb, g_out, w_out, g_norm2,
           w_router_group, w_router_expert, w_gate, w_up, w_down, g_final):
    b, n, d = x.shape
    assert (b, n, d) == (c.shape[0], SEQ, D_MODEL) and w_ada.shape[0] == 1
    nt = b * n

    cond8 = jnp.zeros((8, d), F32).at[0:b].set(c).at[b].set(c_ctx)
    mod = adaln(cond8, w_ada[0], b_ada[0])
    mods = jnp.pad(mod[0:b].reshape(b, N_MOD, d), ((0, 0), (0, 2), (0, 0)))
    mod_ctx = jnp.pad(mod[b].reshape(N_MOD, d), ((0, 2), (0, 0)))

    w_in_b = w_in[0].astype(BF16)
    qr, qp, kr, v, a = in_proj(x, mods, g_norm1[0], w_in_b)
    kc, vc = ctx_proj(ctx, mod_ctx, g_norm1[0], w_in_b[:, D_FOURIER + D_NA:])

    fr = dft_cols(dft_rows(a))
    na = attention(qr, qp, kr, v, kc, vc, bias_tables(rpb[0]))

    w_router = jnp.concatenate(
        [w_router_group[0], w_router_expert[0],
         jnp.zeros((d, LANES - N_GROUPS - N_EXPERTS), F32)], axis=1).astype(BF16)
    x1, h2, meta, cnt = out_proj(fr, na, x, mods, w_fmix[0].astype(BF16), w_out[0].astype(BF16),
                                 g_out[0], g_norm2[0], w_router)

    dest, slot_tok, block_expert, nused = _dispatch_plan(meta, cnt[0], nt)
    ys = experts(h2.reshape(nt, d), block_expert, nused, slot_tok, w_gate[0], w_up[0], w_down[0])
    out = combine(dest, ys, x1.reshape(nt, d), meta, mods, g_final, n)
    return out.reshape(b, n, d)
```

```python
import functools
import math

import numpy as np
import jax
import jax.numpy as jnp
from jax import lax
from jax.experimental import pallas as pl
from jax.experimental.pallas import tpu as pltpu

F32 = jnp.float32
BF16 = jnp.bfloat16

D_MODEL = 1024
GRID_W = 64
GRID_H = 128
SEQ = GRID_W * GRID_H
CTX_LEN = 256
D_FOURIER = 256
FOURIER_GROUP = 64
HEAD_DIM = 64
N_HEADS = 12
D_NA = N_HEADS * HEAD_DIM
N_PAIRS = N_HEADS // 2
NA_ROWS = 8
NA_COLS = 16
ROPE_THETA = 10000.0
N_GROUPS = 4
EXPERTS_PER_GROUP = 8
N_EXPERTS = N_GROUPS * EXPERTS_PER_GROUP
D_EXPERT = 512
N_MOD = 6
D_IN_PROJ = D_FOURIER + 3 * D_NA
EPS = 1e-6
LANES = 128
NEG = -1e30

TM_PROJ = 512
ATT_ROWS = 16
ATT_UNROLL = 4
DFT_TW = 8
DFT_TN = 4096
MOE_BLK = 256
TC_COMB = 256
VMEM_LIMIT = 56 * 1024 * 1024


def _cparams(sem):
    return pltpu.CompilerParams(dimension_semantics=sem, vmem_limit_bytes=VMEM_LIMIT)


def _mxu_const(table):
    return jnp.asarray(table, F32).astype(BF16)


@functools.lru_cache(maxsize=None)
def _rope_tables():
    t = np.arange(SEQ)
    row, col = t // GRID_W, t % GRID_W
    lane = np.arange(LANES)
    d = lane % HEAD_DIM
    chunk = d // 32
    e = d % 32
    j = e % 16
    inv = ROPE_THETA ** (-(j.astype(np.float64)) / 16.0)
    pos = np.where(chunk[None, :] == 0, row[:, None], col[:, None]).astype(np.float64)
    ang = pos * inv[None, :]
    cos = np.cos(ang)
    sin = np.sin(ang)
    first = (e < 16)[None, :]
    s_first = np.where(first, -sin, 0.0)
    s_second = np.where(first, 0.0, sin)
    return (cos.astype(np.float32), s_first.astype(np.float32), s_second.astype(np.float32))


@functools.lru_cache(maxsize=None)
def _chan_dft():
    c = np.arange(FOURIER_GROUP)
    ang = 2.0 * np.pi * ((c[:, None] * c[None, :]) % FOURIER_GROUP) / FOURIER_GROUP
    eye = np.eye(D_FOURIER // FOURIER_GROUP)
    re = np.kron(eye, np.cos(ang))
    im = np.kron(eye, -np.sin(ang))
    return np.concatenate([re, im], axis=1).astype(np.float32)


@functools.lru_cache(maxsize=None)
def _row_dft():
    k1 = np.arange(GRID_H)[:, None]
    r = np.arange(GRID_H)[None, :]
    out = np.zeros((GRID_W, 2 * GRID_H, 2 * GRID_H), np.float32)
    for w in range(GRID_W):
        m = (k1 * (GRID_W * r + w)) % SEQ
        ang = 2.0 * np.pi * m / SEQ
        c, s = np.cos(ang), np.sin(ang)
        out[w] = np.block([[c, s], [-s, c]])
    return out


@functools.lru_cache(maxsize=None)
def _col_dft():
    k2 = np.arange(GRID_W)
    ang = 2.0 * np.pi * ((k2[:, None] * k2[None, :]) % GRID_W) / GRID_W
    scale = 1.0 / math.sqrt(SEQ * FOURIER_GROUP)
    return (np.concatenate([np.cos(ang), np.sin(ang)], axis=1) * scale).astype(np.float32)


@functools.lru_cache(maxsize=None)
def _bias_index():
    c = np.arange(GRID_W)
    start = np.clip(c - NA_COLS // 2, 0, GRID_W - NA_COLS)
    valid = (c[None, :] >= start[:, None]) & (c[None, :] < start[:, None] + NA_COLS)
    dc = np.clip(c[None, :] - c[:, None] + (NA_COLS - 1), 0, 2 * NA_COLS - 2)
    return dc.astype(np.int32), valid


@functools.lru_cache(maxsize=None)
def _strict_lower(n):
    return np.tril(np.ones((n, n), np.float32), k=-1)


def _adaln_kernel(c_ref, w_ref, b_ref, o_ref):
    c = c_ref[...]
    s = c * jax.nn.sigmoid(c)
    o_ref[...] = jnp.dot(s, w_ref[...], precision=lax.Precision.HIGHEST,
                         preferred_element_type=F32) + b_ref[...]


def adaln(cond8, w, b):
    n = w.shape[1]
    tn = 1536
    return pl.pallas_call(
        _adaln_kernel,
        grid=(n // tn,),
        in_specs=[pl.BlockSpec((8, D_MODEL), lambda j: (0, 0)),
                  pl.BlockSpec((D_MODEL, tn), lambda j: (0, j)),
                  pl.BlockSpec((1, tn), lambda j: (0, j))],
        out_specs=pl.BlockSpec((8, tn), lambda j: (0, j)),
        out_shape=jax.ShapeDtypeStruct((8, n), F32),
        compiler_params=_cparams(("arbitrary",)),
        name="adaln",
    )(cond8, w, b.reshape(1, n))


def _norm_mod(x, g, shift, scale):
    ms = jnp.mean(x * x, axis=-1, keepdims=True)
    return (x * lax.rsqrt(ms + EPS) * g) * (1.0 + scale) + shift


def _in_proj_kernel(x_ref, mod_ref, g_ref, w_ref, cs_ref, cos_ref, s1_ref, s2_ref,
                    qr_ref, qp_ref, kr_ref, v_ref, a_ref, h_scr):
    m = mod_ref[0]
    h_scr[...] = _norm_mod(x_ref[0], g_ref[...], m[0:1], m[1:2]).astype(BF16)
    cos, s1, s2 = cos_ref[...], s1_ref[...], s2_ref[...]

    def rope(t):
        return (t * cos + pltpu.roll(t, LANES - 16, axis=1) * s1 + pltpu.roll(t, 16, axis=1) * s2)

    f = jnp.dot(h_scr[...], w_ref[:, 0:D_FOURIER], preferred_element_type=F32)
    a = jnp.dot(f.astype(BF16), cs_ref[...], preferred_element_type=F32)
    a_ref[0, 0] = a[:, :D_FOURIER].astype(BF16)
    a_ref[0, 1] = a[:, D_FOURIER:].astype(BF16)

    scale = HEAD_DIM ** -0.5
    for c in range(D_NA // 256):
        lo = D_FOURIER + 256 * c
        q = jnp.dot(h_scr[...], w_ref[:, lo:lo + 256], preferred_element_type=F32)
        k = jnp.dot(h_scr[...], w_ref[:, lo + D_NA:lo + D_NA + 256], preferred_element_type=F32)
        v = jnp.dot(h_scr[...], w_ref[:, lo + 2 * D_NA:lo + 2 * D_NA + 256], preferred_element_type=F32)
        v_ref[0, :, 256 * c:256 * c + 256] = v.astype(BF16)
        for s in range(2):
            sl = slice(LANES * s, LANES * (s + 1))
            ol = slice(256 * c + LANES * s, 256 * c + LANES * (s + 1))
            qs, ks = q[:, sl], k[:, sl]
            qp_ref[0, :, ol] = (qs * scale).astype(BF16)
            qr_ref[0, :, ol] = (rope(qs) * scale).astype(BF16)
            kr_ref[0, :, ol] = rope(ks).astype(BF16)


def in_proj(x, mods, g1, w_in_bf16):
    b, n, d = x.shape
    tm = TM_PROJ
    cos, s1, s2 = _rope_tables()
    cs = _mxu_const(_chan_dft())
    tok = lambda bi, i: (bi, i, 0)
    const2 = lambda bi, i: (0, 0)
    tab = pl.BlockSpec((tm, LANES), lambda bi, i: (i, 0))
    qkv_shape = jax.ShapeDtypeStruct((b, n, D_NA), BF16)
    qkv_spec = pl.BlockSpec((1, tm, D_NA), tok)
    return pl.pallas_call(
        _in_proj_kernel,
        grid=(b, n // tm),
        in_specs=[pl.BlockSpec((1, tm, d), tok),
                  pl.BlockSpec((1, 8, d), lambda bi, i: (bi, 0, 0)),
                  pl.BlockSpec((1, d), const2),
                  pl.BlockSpec((d, D_IN_PROJ), const2),
                  pl.BlockSpec((D_FOURIER, 2 * D_FOURIER), const2),
                  tab, tab, tab],
        out_specs=[qkv_spec, qkv_spec, qkv_spec, qkv_spec,
                   pl.BlockSpec((1, 2, tm, D_FOURIER), lambda bi, i: (bi, 0, i, 0))],
        out_shape=[qkv_shape, qkv_shape, qkv_shape, qkv_shape,
                   jax.ShapeDtypeStruct((b, 2, n, D_FOURIER), BF16)],
        scratch_shapes=[pltpu.VMEM((tm, d), BF16)],
        compiler_params=_cparams(("arbitrary", "arbitrary")),
        name="in_proj",
    )(x, mods, g1.reshape(1, d), w_in_bf16, cs, jnp.asarray(cos), jnp.asarray(s1), jnp.asarray(s2))


def _ctx_proj_kernel(x_ref, mod_ref, g_ref, w_ref, k_ref, v_ref):
    m = mod_ref[...]
    h = _norm_mod(x_ref[0], g_ref[...], m[0:1], m[1:2]).astype(BF16)
    k_ref[0] = jnp.dot(h, w_ref[:, 0:D_NA], preferred_element_type=F32).astype(BF16)
    v_ref[0] = jnp.dot(h, w_ref[:, D_NA:2 * D_NA], preferred_element_type=F32).astype(BF16)


def ctx_proj(ctx, mod_ctx, g1, w_kv_bf16):
    b, l, d = ctx.shape
    shape = jax.ShapeDtypeStruct((b, l, D_NA), BF16)
    spec = pl.BlockSpec((1, l, D_NA), lambda bi: (bi, 0, 0))
    return pl.pallas_call(
        _ctx_proj_kernel,
        grid=(b,),
        in_specs=[pl.BlockSpec((1, l, d), lambda bi: (bi, 0, 0)),
                  pl.BlockSpec((8, d), lambda bi: (0, 0)),
                  pl.BlockSpec((1, d), lambda bi: (0, 0)),
                  pl.BlockSpec((d, 2 * D_NA), lambda bi: (0, 0))],
        out_specs=[spec, spec],
        out_shape=[shape, shape],
        compiler_params=_cparams(("arbitrary",)),
        name="ctx_proj",
    )(ctx, mod_ctx, g1.reshape(1, d), w_kv_bf16)


def _dft_rows_kernel(a_ref, g_ref, z_ref):
    for j in range(DFT_TW):
        sl = slice(D_FOURIER * j, D_FOURIER * (j + 1))
        rhs = jnp.concatenate([a_ref[0, 0, :, sl], a_ref[0, 1, :, sl]], axis=0)
        z = jnp.dot(g_ref[j], rhs, preferred_element_type=F32)
        z_ref[0, 0, j] = z[:GRID_H].astype(BF16)
        z_ref[0, 1, j] = z[GRID_H:].astype(BF16)


def dft_rows(a):
    b = a.shape[0]
    a4 = a.reshape(b, 2, GRID_H, GRID_W * D_FOURIER)
    g = _mxu_const(_row_dft())
    return pl.pallas_call(
        _dft_rows_kernel,
        grid=(GRID_W // DFT_TW, b),
        in_specs=[pl.BlockSpec((1, 2, GRID_H, DFT_TW * D_FOURIER), lambda j, bi: (bi, 0, 0, j)),
                  pl.BlockSpec((DFT_TW, 2 * GRID_H, 2 * GRID_H), lambda j, bi: (j, 0, 0))],
        out_specs=pl.BlockSpec((1, 2, DFT_TW, GRID_H, D_FOURIER), lambda j, bi: (bi, 0, j, 0, 0)),
        out_shape=jax.ShapeDtypeStruct((b, 2, GRID_W, GRID_H, D_FOURIER), BF16),
        compiler_params=_cparams(("arbitrary", "arbitrary")),
        name="dft_rows",
    )(a4, g)


def _dft_cols_kernel(z_ref, cs_ref, o_ref):
    rhs = jnp.concatenate([z_ref[0, 0], z_ref[0, 1]], axis=0)
    o_ref[0] = jnp.dot(cs_ref[...], rhs, preferred_element_type=F32).astype(BF16)


def dft_cols(z):
    b = z.shape[0]
    ncol = GRID_H * D_FOURIER
    z4 = z.reshape(b, 2, GRID_W, ncol)
    cs = _mxu_const(_col_dft())
    out = pl.pallas_call(
        _dft_cols_kernel,
        grid=(b, ncol // DFT_TN),
        in_specs=[pl.BlockSpec((1, 2, GRID_W, DFT_TN), lambda bi, j: (bi, 0, 0, j)),
                  pl.BlockSpec((GRID_W, 2 * GRID_W), lambda bi, j: (0, 0))],
        out_specs=pl.BlockSpec((1, GRID_W, DFT_TN), lambda bi, j: (bi, 0, j)),
        out_shape=jax.ShapeDtypeStruct((b, GRID_W, ncol), BF16),
        compiler_params=_cparams(("arbitrary", "arbitrary")),
        name="dft_cols",
    )(z4, cs)
    return out.reshape(b, SEQ, D_FOURIER)


def _attention_kernel(qr_ref, qp_ref, k_ref, v_ref, kc_ref, vc_ref, t2_ref, o_ref):
    rb = pl.program_id(2)
    lane = lax.broadcasted_iota(jnp.int32, (GRID_W, LANES), 1)
    first = lane < HEAD_DIM
    kc = kc_ref[0]
    vc = vc_ref[0]
    nt = (((1,), (1,)), ((), ()))

    def split(q):
        zero = jnp.zeros_like(q)
        return jnp.concatenate([jnp.where(first, q, zero), jnp.where(first, zero, q)], axis=0)

    def body(i, carry):
        r = rb * ATT_ROWS + i
        rs = jnp.clip(r - NA_ROWS // 2, 0, GRID_H - NA_ROWS)
        pat = r - rs
        qoff = pl.multiple_of(i * GRID_W, GRID_W)
        koff = pl.multiple_of(rs * GRID_W, GRID_W)
        qm = split(qr_ref[0, pl.ds(qoff, GRID_W), :])
        qpm = split(qp_ref[0, pl.ds(qoff, GRID_W), :])
        kw = k_ref[0, pl.ds(koff, NA_ROWS * GRID_W), :]
        vw = v_ref[0, pl.ds(koff, NA_ROWS * GRID_W), :]
        s = lax.dot_general(qm, kw, nt, preferred_element_type=F32)
        sc = lax.dot_general(qpm, kc, nt, preferred_element_type=F32)
        d0 = (NA_ROWS - 1) - pat
        bias = jnp.concatenate(
            [jnp.concatenate([t2_ref[0, hh, d0 + 2 * t] for t in range(NA_ROWS // 2)], axis=1)
             for hh in range(2)], axis=0)
        s = s + bias
        m = jnp.maximum(jnp.max(s, axis=1, keepdims=True), jnp.max(sc, axis=1, keepdims=True))
        p = jnp.exp(s - m)
        pc = jnp.exp(sc - m)
        l = jnp.sum(p, axis=1, keepdims=True) + jnp.sum(pc, axis=1, keepdims=True)
        o = (jnp.dot(p.astype(BF16), vw, preferred_element_type=F32)
             + jnp.dot(pc.astype(BF16), vc, preferred_element_type=F32))
        o = o / l
        o_ref[0, pl.ds(qoff, GRID_W), :] = jnp.where(first, o[:GRID_W], o[GRID_W:]).astype(BF16)
        return carry

    lax.fori_loop(0, ATT_ROWS, body, 0, unroll=ATT_UNROLL)


def bias_tables(rpb):
    dc, valid = _bias_index()
    t = jnp.where(valid[None, None], rpb[:, :, dc], NEG)
    t2 = jnp.concatenate([t[:, :-1], t[:, 1:]], axis=-1)
    return t2.reshape(N_PAIRS, 2, 2 * NA_ROWS - 2, GRID_W, LANES).astype(F32)


def attention(qr, qp, kr, v, kc, vc, t2):
    b, n, _ = qr.shape
    tq = ATT_ROWS * GRID_W
    qspec = pl.BlockSpec((1, tq, LANES), lambda bi, hp, i: (bi, i, hp))
    kspec = pl.BlockSpec((1, n, LANES), lambda bi, hp, i: (bi, 0, hp))
    cspec = pl.BlockSpec((1, CTX_LEN, LANES), lambda bi, hp, i: (bi, 0, hp))
    return pl.pallas_call(
        _attention_kernel,
        grid=(b, N_PAIRS, GRID_H // ATT_ROWS),
        in_specs=[qspec, qspec, kspec, kspec, cspec, cspec,
                  pl.BlockSpec((1, 2, 2 * NA_ROWS - 2, GRID_W, LANES), lambda bi, hp, i: (hp, 0, 0, 0, 0))],
        out_specs=qspec,
        out_shape=jax.ShapeDtypeStruct((b, n, D_NA), BF16),
        compiler_params=_cparams(("arbitrary", "arbitrary", "arbitrary")),
        name="attention",
    )(qr, qp, kr, v, kc, vc, t2)


def _rms(x, g):
    ms = jnp.mean(x * x, axis=-1, keepdims=True)
    return x * lax.rsqrt(ms + EPS) * g


TILE_ROWS = D_MODEL // LANES


def _store_token_tiles(ref, val):
    rows = val.shape[0]
    for j in range(TILE_ROWS):
        ref[pl.ds(j, rows, stride=TILE_ROWS), :] = val[:, LANES * j:LANES * (j + 1)]


def _load_token_tiles(ref, start, rows):
    return jnp.concatenate(
        [ref[pl.ds(start + j, rows, stride=TILE_ROWS), :] for j in range(TILE_ROWS)], axis=1)


def _out_proj_kernel(fr_ref, na_ref, x_ref, mod_ref, wf_ref, wo_ref, go_ref, g2_ref, wr_ref, tri_ref,
                     x1_ref, h2_ref, meta_ref, metat_ref, cnt_ref, run_scr):
    @pl.when((pl.program_id(0) == 0) & (pl.program_id(1) == 0))
    def _():
        run_scr[...] = jnp.zeros_like(run_scr)

    m = mod_ref[0]
    go = go_ref[...]
    fo = jnp.dot(fr_ref[0], wf_ref[...], preferred_element_type=F32)
    fn = _rms(fo, go[:, :D_FOURIER]).astype(BF16)
    nn = _rms(na_ref[0].astype(F32), go[:, D_FOURIER:]).astype(BF16)
    y = (jnp.dot(fn, wo_ref[0:D_FOURIER, :], preferred_element_type=F32)
         + jnp.dot(nn, wo_ref[D_FOURIER:, :], preferred_element_type=F32))
    x1 = x_ref[0] + m[2:3] * y
    x1_ref[0] = x1
    h2 = _norm_mod(x1, g2_ref[...], m[3:4], m[4:5])
    _store_token_tiles(h2_ref, h2)
    logits = jnp.dot(h2.astype(BF16), wr_ref[...], preferred_element_type=F32)

    tm = logits.shape[0]
    lane = lax.broadcasted_iota(jnp.int32, (tm, LANES), 1).astype(F32)
    ninf = jnp.float32(-jnp.inf)

    def argmax_first(vals):
        mx = jnp.max(vals, axis=1, keepdims=True)
        idx = jnp.min(jnp.where(vals == mx, lane, float(LANES)), axis=1, keepdims=True)
        return mx, idx

    lg = jnp.where(lane < N_GROUPS, logits, ninf)
    gmax, gidx = argmax_first(lg)
    pg = 1.0 / jnp.sum(jnp.exp(lg - gmax), axis=1, keepdims=True)
    lo = N_GROUPS + EXPERTS_PER_GROUP * gidx
    le = jnp.where((lane >= lo) & (lane < lo + EXPERTS_PER_GROUP), logits, ninf)
    e1, i1 = argmax_first(le)
    e2, i2 = argmax_first(jnp.where(lane == i1, ninf, le))
    dd = jnp.exp(e2 - e1)
    gate1 = pg / (1.0 + dd)
    gate2 = pg * dd / (1.0 + dd)

    hot1 = lane == i1
    hot2 = lane == i2
    onehot = jnp.where(hot1 | hot2, 1.0, 0.0)
    cnt = jnp.dot(tri_ref[...], onehot.astype(BF16), preferred_element_type=F32) + run_scr[...]
    rank1 = jnp.sum(jnp.where(hot1, cnt, 0.0), axis=1, keepdims=True)
    rank2 = jnp.sum(jnp.where(hot2, cnt, 0.0), axis=1, keepdims=True)
    run_scr[...] = run_scr[...] + jnp.sum(onehot, axis=0, keepdims=True)

    meta = jnp.where(lane == 0, i1 - N_GROUPS,
           jnp.where(lane == 1, i2 - N_GROUPS,
           jnp.where(lane == 2, rank1,
           jnp.where(lane == 3, rank2,
           jnp.where(lane == 4, gate1,
           jnp.where(lane == 5, gate2, 0.0))))))
    meta_ref[...] = meta
    metat_ref[...] = jnp.transpose(meta)[0:8, :]
    cnt_ref[...] = jnp.broadcast_to(run_scr[...], cnt_ref.shape)


def out_proj(fr, na, x, mods, w_fmix_bf16, w_out_bf16, g_out, g2, w_router_bf16):
    b, n, d = x.shape
    tm = TM_PROJ
    steps = n // tm
    tok = lambda bi, i: (bi, i, 0)
    const2 = lambda bi, i: (0, 0)
    flat = lambda bi, i: (bi * steps + i, 0)
    tri = _mxu_const(_strict_lower(tm))
    return pl.pallas_call(
        _out_proj_kernel,
        grid=(b, steps),
        in_specs=[pl.BlockSpec((1, tm, D_FOURIER), tok),
                  pl.BlockSpec((1, tm, D_NA), tok),
                  pl.BlockSpec((1, tm, d), tok),
                  pl.BlockSpec((1, 8, d), lambda bi, i: (bi, 0, 0)),
                  pl.BlockSpec((D_FOURIER, D_FOURIER), const2),
                  pl.BlockSpec((d, d), const2),
                  pl.BlockSpec((1, d), const2),
                  pl.BlockSpec((1, d), const2),
                  pl.BlockSpec((d, LANES), const2),
                  pl.BlockSpec((tm, tm), const2)],
        out_specs=[pl.BlockSpec((1, tm, d), tok),
                   pl.BlockSpec((tm * TILE_ROWS, LANES), flat),
                   pl.BlockSpec((tm, LANES), flat),
                   pl.BlockSpec((8, tm), lambda bi, i: (0, bi * steps + i)),
                   pl.BlockSpec((8, LANES), const2)],
        out_shape=[jax.ShapeDtypeStruct((b, n, d), F32),
                   jax.ShapeDtypeStruct((b * n * TILE_ROWS, LANES), F32),
                   jax.ShapeDtypeStruct((b * n, LANES), F32),
                   jax.ShapeDtypeStruct((8, b * n), F32),
                   jax.ShapeDtypeStruct((8, LANES), F32)],
        scratch_shapes=[pltpu.VMEM((1, LANES), F32)],
        compiler_params=_cparams(("arbitrary", "arbitrary")),
        name="out_proj",
    )(fr, na, x, mods, w_fmix_bf16, w_out_bf16, g_out.reshape(1, d), g2.reshape(1, d), w_router_bf16, tri)


DMA_CHUNK = 8


def _experts_kernel(be_ref, nused_ref, packed_ref, pstart_ref, h2_hbm, wg_ref, wu_ref, wd_ref,
                    ys_ref, dest_ref, xbuf, wg_scr, wu_scr, wd_scr, slot_tok, sem):
    i = pl.program_id(0)
    nblk = pl.num_programs(0)
    nused = nused_ref[0]
    n_assign = packed_ref.shape[0]
    nt = n_assign // 2
    blk_rows = MOE_BLK * TILE_ROWS

    @pl.when(i == 0)
    def _():
        ntrip = nused * (MOE_BLK // DMA_CHUNK)

        def clear(c, carry):
            for u in range(DMA_CHUNK):
                slot_tok[c * DMA_CHUNK + u] = 0
            return carry

        lax.fori_loop(0, ntrip, clear, 0)

        def place(c, carry):
            for u in range(DMA_CHUNK):
                a = c * DMA_CHUNK + u
                v = packed_ref[a]
                d = pstart_ref[v & (N_EXPERTS - 1)] + (v >> 5)
                slot_tok[d] = a & (nt - 1)
                dest_ref[a] = d
            return carry

        lax.fori_loop(0, jnp.minimum(ntrip, n_assign // DMA_CHUNK), place, 0)

    def gather(blk, slot):
        def issue(c, carry):
            s0 = c * DMA_CHUNK
            for u in range(DMA_CHUNK):
                tok = slot_tok[blk * MOE_BLK + s0 + u]
                pltpu.make_async_copy(
                    h2_hbm.at[pl.ds(tok * TILE_ROWS, TILE_ROWS)],
                    xbuf.at[pl.ds(slot * blk_rows + (s0 + u) * TILE_ROWS, TILE_ROWS)],
                    sem.at[slot]).start()
            return carry

        lax.fori_loop(0, MOE_BLK // DMA_CHUNK, issue, 0)

    @pl.when((i == 0) & (nused > 0))
    def _():
        gather(0, 0)

    @pl.when((i + 1 < nblk) & (i + 1 < nused))
    def _():
        gather(i + 1, (i + 1) % 2)

    changed = (i == 0) | (be_ref[i] != be_ref[jnp.maximum(i - 1, 0)])

    @pl.when(changed & (i < nused))
    def _():
        wg_scr[...] = wg_ref[0].astype(BF16)
        wu_scr[...] = wu_ref[0].astype(BF16)
        wd_scr[...] = wd_ref[0].astype(BF16)

    @pl.when(i < nused)
    def _():
        slot = i % 2
        start = pl.multiple_of(slot * blk_rows, blk_rows)
        pltpu.make_async_copy(h2_hbm.at[pl.ds(0, blk_rows)], xbuf.at[pl.ds(start, blk_rows)],
                              sem.at[slot]).wait()
        x = _load_token_tiles(xbuf, start, MOE_BLK).astype(BF16)
        g = jnp.dot(x, wg_scr[...], preferred_element_type=F32)
        u = jnp.dot(x, wu_scr[...], preferred_element_type=F32)
        hmid = (g * jax.nn.sigmoid(g) * u).astype(BF16)
        _store_token_tiles(ys_ref, jnp.dot(hmid, wd_scr[...], preferred_element_type=F32))

    @pl.when(i >= nused)
    def _():
        ys_ref[...] = jnp.zeros_like(ys_ref)


def experts(h2_tiles, block_expert, nused, packed, pstarts, w_gate, w_up, w_down):
    d = D_MODEL
    nblk = block_expert.shape[0]
    blk_rows = MOE_BLK * TILE_ROWS
    wmap = lambda i, be, nu, pk, ps: (be[i], 0, 0)
    grid_spec = pltpu.PrefetchScalarGridSpec(
        num_scalar_prefetch=4,
        grid=(nblk,),
        in_specs=[pl.BlockSpec(memory_space=pl.ANY),
                  pl.BlockSpec((1, d, D_EXPERT), wmap),
                  pl.BlockSpec((1, d, D_EXPERT), wmap),
                  pl.BlockSpec((1, D_EXPERT, d), wmap)],
        out_specs=[pl.BlockSpec((blk_rows, LANES), lambda i, be, nu, pk, ps: (i, 0)),
                   pl.BlockSpec(memory_space=pltpu.SMEM)],
        scratch_shapes=[pltpu.VMEM((2 * blk_rows, LANES), F32),
                        pltpu.VMEM((d, D_EXPERT), BF16),
                        pltpu.VMEM((d, D_EXPERT), BF16),
                        pltpu.VMEM((D_EXPERT, d), BF16),
                        pltpu.SMEM((nblk * MOE_BLK,), jnp.int32),
                        pltpu.SemaphoreType.DMA((2,))],
    )
    return pl.pallas_call(
        _experts_kernel,
        grid_spec=grid_spec,
        out_shape=[jax.ShapeDtypeStruct((nblk * blk_rows, LANES), F32),
                   jax.ShapeDtypeStruct(packed.shape, jnp.int32)],
        compiler_params=_cparams(("arbitrary",)),
        name="experts",
    )(block_expert, nused, packed, pstarts, h2_tiles, w_gate, w_up, w_down)


def _combine_kernel(dest_ref, ys_hbm, x1_ref, meta_ref, mod_ref, gf_ref, o_ref, ybuf, sem):
    i = pl.program_id(0)
    nstep = pl.num_programs(0)
    tc = TC_COMB
    nt = dest_ref.shape[0] // 2
    half_rows = tc * TILE_ROWS
    buf_rows = 2 * half_rows

    def gather(step, slot):
        def issue(c, carry):
            t0 = c * DMA_CHUNK
            for u in range(DMA_CHUNK):
                for k in range(2):
                    d = dest_ref[k * nt + step * tc + t0 + u]
                    pltpu.make_async_copy(
                        ys_hbm.at[pl.ds(d * TILE_ROWS, TILE_ROWS)],
                        ybuf.at[pl.ds(slot * buf_rows + k * half_rows + (t0 + u) * TILE_ROWS, TILE_ROWS)],
                        sem.at[slot]).start()
            return carry

        lax.fori_loop(0, tc // DMA_CHUNK, issue, 0)

    @pl.when(i == 0)
    def _():
        gather(0, 0)

    @pl.when(i + 1 < nstep)
    def _():
        gather(i + 1, (i + 1) % 2)

    slot = i % 2
    start = pl.multiple_of(slot * buf_rows, buf_rows)
    pltpu.make_async_copy(ys_hbm.at[pl.ds(0, buf_rows)], ybuf.at[pl.ds(start, buf_rows)], sem.at[slot]).wait()
    meta = meta_ref[...]
    y0 = _load_token_tiles(ybuf, start, tc)
    y1 = _load_token_tiles(ybuf, start + half_rows, tc)
    moe = y0 * meta[:, 4:5] + y1 * meta[:, 5:6]
    x2 = x1_ref[...] + mod_ref[0][5:6] * moe
    o_ref[...] = _rms(x2, gf_ref[...])


def combine(dest_flat, ys, x1_flat, meta, mods, g_final, n_per_batch):
    nt, d = x1_flat.shape
    tc = TC_COMB
    per_b = n_per_batch // tc
    grid_spec = pltpu.PrefetchScalarGridSpec(
        num_scalar_prefetch=1,
        grid=(nt // tc,),
        in_specs=[pl.BlockSpec(memory_space=pl.ANY),
                  pl.BlockSpec((tc, d), lambda i, ds: (i, 0)),
                  pl.BlockSpec((tc, LANES), lambda i, ds: (i, 0)),
                  pl.BlockSpec((1, 8, d), lambda i, ds: (i // per_b, 0, 0)),
                  pl.BlockSpec((1, d), lambda i, ds: (0, 0))],
        out_specs=pl.BlockSpec((tc, d), lambda i, ds: (i, 0)),
        scratch_shapes=[pltpu.VMEM((2 * 2 * tc * TILE_ROWS, LANES), F32),
                        pltpu.SemaphoreType.DMA((2,))],
    )
    return pl.pallas_call(
        _combine_kernel,
        grid_spec=grid_spec,
        out_shape=jax.ShapeDtypeStruct((nt, d), F32),
        compiler_params=_cparams(("arbitrary",)),
        name="combine",
    )(dest_flat, ys, x1_flat, meta, mods, g_final.reshape(1, d))


def _dispatch_plan(metat, counts_row, nt):
    packed = (metat[0:2] + float(N_EXPERTS) * metat[2:4]).astype(jnp.int32).reshape(-1)
    counts = counts_row[N_GROUPS:N_GROUPS + N_EXPERTS].astype(jnp.int32)
    pcounts = ((counts + MOE_BLK - 1) // MOE_BLK) * MOE_BLK
    pends = jnp.cumsum(pcounts)
    pstarts = pends - pcounts
    nblk = (nt * 2) // MOE_BLK + N_EXPERTS
    first_slot = jnp.arange(nblk, dtype=jnp.int32) * MOE_BLK
    block_expert = jnp.minimum(
        jnp.sum((pends[None, :] <= first_slot[:, None]).astype(jnp.int32), axis=1), N_EXPERTS - 1)
    nused = (pends[-1] // MOE_BLK).astype(jnp.int32).reshape(1)
    return packed, pstarts.astype(jnp.int32), block_expert.astype(jnp.int32), nused


def kernel(x, c, ctx, c_ctx, w_ada, b_ada, g_norm1, w_in, w_fmix, rpb, g_out, w_out, g_norm2,
           w_router_group, w_router_expert, w_gate, w_up, w_down, g_final):
    b, n, d = x.shape
    assert (b, n, d) == (c.shape[0], SEQ, D_MODEL) and w_ada.shape[0] == 1
    nt = b * n

    cond8 = jnp.zeros((8, d), F32).at[0:b].set(c).at[b].set(c_ctx)
    mod = adaln(cond8, w_ada[0], b_ada[0])
    mods = jnp.pad(mod[0:b].reshape(b, N_MOD, d), ((0, 0), (0, 2), (0, 0)))
    mod_ctx = jnp.pad(mod[b].reshape(N_MOD, d), ((0, 2), (0, 0)))

    w_in_b = w_in[0].astype(BF16)
    qr, qp, kr, v, a = in_proj(x, mods, g_norm1[0], w_in_b)
    kc, vc = ctx_proj(ctx, mod_ctx, g_norm1[0], w_in_b[:, D_FOURIER + D_NA:])

    fr = dft_cols(dft_rows(a))
    na = attention(qr, qp, kr, v, kc, vc, bias_tables(rpb[0]))

    w_router = jnp.concatenate(
        [w_router_group[0], w_router_expert[0],
         jnp.zeros((d, LANES - N_GROUPS - N_EXPERTS), F32)], axis=1).astype(BF16)
    x1, h2_tiles, meta, metat, cnt = out_proj(fr, na, x, mods, w_fmix[0].astype(BF16),
                                              w_out[0].astype(BF16), g_out[0], g_norm2[0], w_router)

    packed, pstarts, block_expert, nused = _dispatch_plan(metat, cnt[0], nt)
    ys_tiles, dest = experts(h2_tiles, block_expert, nused, packed, pstarts,
                             w_gate[0], w_up[0], w_down[0])
    out = combine(dest, ys_tiles, x1.reshape(nt, d), meta, mods, g_final, n)
    return out.reshape(b, n, d)
```

```python
import functools
import math

import numpy as np
import jax
import jax.numpy as jnp
from jax import lax
from jax.experimental import pallas as pl
from jax.experimental.pallas import tpu as pltpu

F32 = jnp.float32
BF16 = jnp.bfloat16

D_MODEL = 1024
GRID_W = 64
GRID_H = 128
SEQ = GRID_W * GRID_H
CTX_LEN = 256
D_FOURIER = 256
FOURIER_GROUP = 64
HEAD_DIM = 64
N_HEADS = 12
D_NA = N_HEADS * HEAD_DIM
N_PAIRS = N_HEADS // 2
NA_ROWS = 8
NA_COLS = 16
ROPE_THETA = 10000.0
N_GROUPS = 4
EXPERTS_PER_GROUP = 8
N_EXPERTS = N_GROUPS * EXPERTS_PER_GROUP
D_EXPERT = 512
N_MOD = 6
D_IN_PROJ = D_FOURIER + 3 * D_NA
EPS = 1e-6
LANES = 128
NEG = -1e30

TM_PROJ = 512
ATT_ROWS = 16
ATT_UNROLL = 4
DFT_TW = 8
DFT_TN = 4096
MOE_BLK = 256
TC_COMB = 256
VMEM_LIMIT = 56 * 1024 * 1024


def _cparams(sem):
    return pltpu.CompilerParams(dimension_semantics=sem, vmem_limit_bytes=VMEM_LIMIT)


def _mxu_const(table):
    return jnp.asarray(table, F32).astype(BF16)


@functools.lru_cache(maxsize=None)
def _rope_tables():
    t = np.arange(SEQ)
    row, col = t // GRID_W, t % GRID_W
    lane = np.arange(LANES)
    d = lane % HEAD_DIM
    chunk = d // 32
    e = d % 32
    j = e % 16
    inv = ROPE_THETA ** (-(j.astype(np.float64)) / 16.0)
    pos = np.where(chunk[None, :] == 0, row[:, None], col[:, None]).astype(np.float64)
    ang = pos * inv[None, :]
    cos = np.cos(ang)
    sin = np.sin(ang)
    first = (e < 16)[None, :]
    s_first = np.where(first, -sin, 0.0)
    s_second = np.where(first, 0.0, sin)
    return (cos.astype(np.float32), s_first.astype(np.float32), s_second.astype(np.float32))


@functools.lru_cache(maxsize=None)
def _chan_dft():
    c = np.arange(FOURIER_GROUP)
    ang = 2.0 * np.pi * ((c[:, None] * c[None, :]) % FOURIER_GROUP) / FOURIER_GROUP
    eye = np.eye(D_FOURIER // FOURIER_GROUP)
    re = np.kron(eye, np.cos(ang))
    im = np.kron(eye, -np.sin(ang))
    return np.concatenate([re, im], axis=1).astype(np.float32)


@functools.lru_cache(maxsize=None)
def _row_dft():
    k1 = np.arange(GRID_H)[:, None]
    r = np.arange(GRID_H)[None, :]
    out = np.zeros((GRID_W, 2 * GRID_H, 2 * GRID_H), np.float32)
    for w in range(GRID_W):
        m = (k1 * (GRID_W * r + w)) % SEQ
        ang = 2.0 * np.pi * m / SEQ
        c, s = np.cos(ang), np.sin(ang)
        out[w] = np.block([[c, s], [-s, c]])
    return out


@functools.lru_cache(maxsize=None)
def _col_dft():
    k2 = np.arange(GRID_W)
    ang = 2.0 * np.pi * ((k2[:, None] * k2[None, :]) % GRID_W) / GRID_W
    scale = 1.0 / math.sqrt(SEQ * FOURIER_GROUP)
    return (np.concatenate([np.cos(ang), np.sin(ang)], axis=1) * scale).astype(np.float32)


@functools.lru_cache(maxsize=None)
def _bias_index():
    c = np.arange(GRID_W)
    start = np.clip(c - NA_COLS // 2, 0, GRID_W - NA_COLS)
    valid = (c[None, :] >= start[:, None]) & (c[None, :] < start[:, None] + NA_COLS)
    dc = np.clip(c[None, :] - c[:, None] + (NA_COLS - 1), 0, 2 * NA_COLS - 2)
    return dc.astype(np.int32), valid


@functools.lru_cache(maxsize=None)
def _strict_lower(n):
    return np.tril(np.ones((n, n), np.float32), k=-1)


def _adaln_kernel(c_ref, w_ref, b_ref, o_ref):
    c = c_ref[...]
    s = c * jax.nn.sigmoid(c)
    o_ref[...] = jnp.dot(s, w_ref[...], precision=lax.Precision.HIGHEST,
                         preferred_element_type=F32) + b_ref[...]


def adaln(cond8, w, b):
    n = w.shape[1]
    tn = 1536
    return pl.pallas_call(
        _adaln_kernel,
        grid=(n // tn,),
        in_specs=[pl.BlockSpec((8, D_MODEL), lambda j: (0, 0)),
                  pl.BlockSpec((D_MODEL, tn), lambda j: (0, j)),
                  pl.BlockSpec((1, tn), lambda j: (0, j))],
        out_specs=pl.BlockSpec((8, tn), lambda j: (0, j)),
        out_shape=jax.ShapeDtypeStruct((8, n), F32),
        compiler_params=_cparams(("arbitrary",)),
        name="adaln",
    )(cond8, w, b.reshape(1, n))


def _norm_mod(x, g, shift, scale):
    ms = jnp.mean(x * x, axis=-1, keepdims=True)
    return (x * lax.rsqrt(ms + EPS) * g) * (1.0 + scale) + shift


def _in_proj_kernel(x_ref, mod_ref, g_ref, w_ref, cs_ref, cos_ref, s1_ref, s2_ref,
                    qr_ref, qp_ref, kr_ref, v_ref, a_ref, h_scr):
    m = mod_ref[0]
    h_scr[...] = _norm_mod(x_ref[0], g_ref[...], m[0:1], m[1:2]).astype(BF16)
    cos, s1, s2 = cos_ref[...], s1_ref[...], s2_ref[...]

    def rope(t):
        return (t * cos + pltpu.roll(t, LANES - 16, axis=1) * s1 + pltpu.roll(t, 16, axis=1) * s2)

    f = jnp.dot(h_scr[...], w_ref[:, 0:D_FOURIER], preferred_element_type=F32)
    a = jnp.dot(f.astype(BF16), cs_ref[...], preferred_element_type=F32)
    a_ref[0, 0] = a[:, :D_FOURIER].astype(BF16)
    a_ref[0, 1] = a[:, D_FOURIER:].astype(BF16)

    scale = HEAD_DIM ** -0.5
    for c in range(D_NA // 256):
        lo = D_FOURIER + 256 * c
        q = jnp.dot(h_scr[...], w_ref[:, lo:lo + 256], preferred_element_type=F32)
        k = jnp.dot(h_scr[...], w_ref[:, lo + D_NA:lo + D_NA + 256], preferred_element_type=F32)
        v = jnp.dot(h_scr[...], w_ref[:, lo + 2 * D_NA:lo + 2 * D_NA + 256], preferred_element_type=F32)
        v_ref[0, :, 256 * c:256 * c + 256] = v.astype(BF16)
        for s in range(2):
            sl = slice(LANES * s, LANES * (s + 1))
            ol = slice(256 * c + LANES * s, 256 * c + LANES * (s + 1))
            qs, ks = q[:, sl], k[:, sl]
            qp_ref[0, :, ol] = (qs * scale).astype(BF16)
            qr_ref[0, :, ol] = (rope(qs) * scale).astype(BF16)
            kr_ref[0, :, ol] = rope(ks).astype(BF16)


def in_proj(x, mods, g1, w_in_bf16):
    b, n, d = x.shape
    tm = TM_PROJ
    cos, s1, s2 = _rope_tables()
    cs = _mxu_const(_chan_dft())
    tok = lambda bi, i: (bi, i, 0)
    const2 = lambda bi, i: (0, 0)
    tab = pl.BlockSpec((tm, LANES), lambda bi, i: (i, 0))
    qkv_shape = jax.ShapeDtypeStruct((b, n, D_NA), BF16)
    qkv_spec = pl.BlockSpec((1, tm, D_NA), tok)
    return pl.pallas_call(
        _in_proj_kernel,
        grid=(b, n // tm),
        in_specs=[pl.BlockSpec((1, tm, d), tok),
                  pl.BlockSpec((1, 8, d), lambda bi, i: (bi, 0, 0)),
                  pl.BlockSpec((1, d), const2),
                  pl.BlockSpec((d, D_IN_PROJ), const2),
                  pl.BlockSpec((D_FOURIER, 2 * D_FOURIER), const2),
                  tab, tab, tab],
        out_specs=[qkv_spec, qkv_spec, qkv_spec, qkv_spec,
                   pl.BlockSpec((1, 2, tm, D_FOURIER), lambda bi, i: (bi, 0, i, 0))],
        out_shape=[qkv_shape, qkv_shape, qkv_shape, qkv_shape,
                   jax.ShapeDtypeStruct((b, 2, n, D_FOURIER), BF16)],
        scratch_shapes=[pltpu.VMEM((tm, d), BF16)],
        compiler_params=_cparams(("arbitrary", "arbitrary")),
        name="in_proj",
    )(x, mods, g1.reshape(1, d), w_in_bf16, cs, jnp.asarray(cos), jnp.asarray(s1), jnp.asarray(s2))


def _ctx_proj_kernel(x_ref, mod_ref, g_ref, w_ref, k_ref, v_ref):
    m = mod_ref[...]
    h = _norm_mod(x_ref[0], g_ref[...], m[0:1], m[1:2]).astype(BF16)
    k_ref[0] = jnp.dot(h, w_ref[:, 0:D_NA], preferred_element_type=F32).astype(BF16)
    v_ref[0] = jnp.dot(h, w_ref[:, D_NA:2 * D_NA], preferred_element_type=F32).astype(BF16)


def ctx_proj(ctx, mod_ctx, g1, w_kv_bf16):
    b, l, d = ctx.shape
    shape = jax.ShapeDtypeStruct((b, l, D_NA), BF16)
    spec = pl.BlockSpec((1, l, D_NA), lambda bi: (bi, 0, 0))
    return pl.pallas_call(
        _ctx_proj_kernel,
        grid=(b,),
        in_specs=[pl.BlockSpec((1, l, d), lambda bi: (bi, 0, 0)),
                  pl.BlockSpec((8, d), lambda bi: (0, 0)),
                  pl.BlockSpec((1, d), lambda bi: (0, 0)),
                  pl.BlockSpec((d, 2 * D_NA), lambda bi: (0, 0))],
        out_specs=[spec, spec],
        out_shape=[shape, shape],
        compiler_params=_cparams(("arbitrary",)),
        name="ctx_proj",
    )(ctx, mod_ctx, g1.reshape(1, d), w_kv_bf16)


def _dft_rows_kernel(a_ref, g_ref, z_ref):
    for j in range(DFT_TW):
        sl = slice(D_FOURIER * j, D_FOURIER * (j + 1))
        rhs = jnp.concatenate([a_ref[0, 0, :, sl], a_ref[0, 1, :, sl]], axis=0)
        z = jnp.dot(g_ref[j], rhs, preferred_element_type=F32)
        z_ref[0, 0, j] = z[:GRID_H].astype(BF16)
        z_ref[0, 1, j] = z[GRID_H:].astype(BF16)


def dft_rows(a):
    b = a.shape[0]
    a4 = a.reshape(b, 2, GRID_H, GRID_W * D_FOURIER)
    g = _mxu_const(_row_dft())
    return pl.pallas_call(
        _dft_rows_kernel,
        grid=(GRID_W // DFT_TW, b),
        in_specs=[pl.BlockSpec((1, 2, GRID_H, DFT_TW * D_FOURIER), lambda j, bi: (bi, 0, 0, j)),
                  pl.BlockSpec((DFT_TW, 2 * GRID_H, 2 * GRID_H), lambda j, bi: (j, 0, 0))],
        out_specs=pl.BlockSpec((1, 2, DFT_TW, GRID_H, D_FOURIER), lambda j, bi: (bi, 0, j, 0, 0)),
        out_shape=jax.ShapeDtypeStruct((b, 2, GRID_W, GRID_H, D_FOURIER), BF16),
        compiler_params=_cparams(("arbitrary", "arbitrary")),
        name="dft_rows",
    )(a4, g)


def _dft_cols_kernel(z_ref, cs_ref, o_ref):
    rhs = jnp.concatenate([z_ref[0, 0], z_ref[0, 1]], axis=0)
    o_ref[0] = jnp.dot(cs_ref[...], rhs, preferred_element_type=F32).astype(BF16)


def dft_cols(z):
    b = z.shape[0]
    ncol = GRID_H * D_FOURIER
    z4 = z.reshape(b, 2, GRID_W, ncol)
    cs = _mxu_const(_col_dft())
    out = pl.pallas_call(
        _dft_cols_kernel,
        grid=(b, ncol // DFT_TN),
        in_specs=[pl.BlockSpec((1, 2, GRID_W, DFT_TN), lambda bi, j: (bi, 0, 0, j)),
                  pl.BlockSpec((GRID_W, 2 * GRID_W), lambda bi, j: (0, 0))],
        out_specs=pl.BlockSpec((1, GRID_W, DFT_TN), lambda bi, j: (bi, 0, j)),
        out_shape=jax.ShapeDtypeStruct((b, GRID_W, ncol), BF16),
        compiler_params=_cparams(("arbitrary", "arbitrary")),
        name="dft_cols",
    )(z4, cs)
    return out.reshape(b, SEQ, D_FOURIER)


def _attention_kernel(qr_ref, qp_ref, k_ref, v_ref, kc_ref, vc_ref, t2_ref, o_ref):
    rb = pl.program_id(2)
    lane = lax.broadcasted_iota(jnp.int32, (GRID_W, LANES), 1)
    first = lane < HEAD_DIM
    kc = kc_ref[0]
    vc = vc_ref[0]
    nt = (((1,), (1,)), ((), ()))

    def split(q):
        zero = jnp.zeros_like(q)
        return jnp.concatenate([jnp.where(first, q, zero), jnp.where(first, zero, q)], axis=0)

    def body(i, carry):
        r = rb * ATT_ROWS + i
        rs = jnp.clip(r - NA_ROWS // 2, 0, GRID_H - NA_ROWS)
        pat = r - rs
        qoff = pl.multiple_of(i * GRID_W, GRID_W)
        koff = pl.multiple_of(rs * GRID_W, GRID_W)
        qm = split(qr_ref[0, pl.ds(qoff, GRID_W), :])
        qpm = split(qp_ref[0, pl.ds(qoff, GRID_W), :])
        kw = k_ref[0, pl.ds(koff, NA_ROWS * GRID_W), :]
        vw = v_ref[0, pl.ds(koff, NA_ROWS * GRID_W), :]
        s = lax.dot_general(qm, kw, nt, preferred_element_type=F32)
        sc = lax.dot_general(qpm, kc, nt, preferred_element_type=F32)
        d0 = (NA_ROWS - 1) - pat
        bias = jnp.concatenate(
            [jnp.concatenate([t2_ref[0, hh, d0 + 2 * t] for t in range(NA_ROWS // 2)], axis=1)
             for hh in range(2)], axis=0)
        s = s + bias
        m = jnp.maximum(jnp.max(s, axis=1, keepdims=True), jnp.max(sc, axis=1, keepdims=True))
        p = jnp.exp(s - m)
        pc = jnp.exp(sc - m)
        l = jnp.sum(p, axis=1, keepdims=True) + jnp.sum(pc, axis=1, keepdims=True)
        o = (jnp.dot(p.astype(BF16), vw, preferred_element_type=F32)
             + jnp.dot(pc.astype(BF16), vc, preferred_element_type=F32))
        o = o / l
        o_ref[0, pl.ds(qoff, GRID_W), :] = jnp.where(first, o[:GRID_W], o[GRID_W:]).astype(BF16)
        return carry

    lax.fori_loop(0, ATT_ROWS, body, 0, unroll=ATT_UNROLL)


def bias_tables(rpb):
    dc, valid = _bias_index()
    n_dc = 2 * NA_COLS - 1
    onehot = (dc.reshape(1, -1) == np.arange(n_dc).reshape(-1, 1)).astype(np.float32)
    t = jnp.dot(rpb.reshape(-1, n_dc), jnp.asarray(onehot), precision=lax.Precision.HIGHEST)
    t = jnp.where(valid[None, None], t.reshape(N_HEADS, 2 * NA_ROWS - 1, GRID_W, GRID_W), NEG)
    t2 = jnp.concatenate([t[:, :-1], t[:, 1:]], axis=-1)
    return t2.reshape(N_PAIRS, 2, 2 * NA_ROWS - 2, GRID_W, LANES).astype(F32)


def attention(qr, qp, kr, v, kc, vc, t2):
    b, n, _ = qr.shape
    tq = ATT_ROWS * GRID_W
    qspec = pl.BlockSpec((1, tq, LANES), lambda bi, hp, i: (bi, i, hp))
    kspec = pl.BlockSpec((1, n, LANES), lambda bi, hp, i: (bi, 0, hp))
    cspec = pl.BlockSpec((1, CTX_LEN, LANES), lambda bi, hp, i: (bi, 0, hp))
    return pl.pallas_call(
        _attention_kernel,
        grid=(b, N_PAIRS, GRID_H // ATT_ROWS),
        in_specs=[qspec, qspec, kspec, kspec, cspec, cspec,
                  pl.BlockSpec((1, 2, 2 * NA_ROWS - 2, GRID_W, LANES), lambda bi, hp, i: (hp, 0, 0, 0, 0))],
        out_specs=qspec,
        out_shape=jax.ShapeDtypeStruct((b, n, D_NA), BF16),
        compiler_params=_cparams(("arbitrary", "arbitrary", "arbitrary")),
        name="attention",
    )(qr, qp, kr, v, kc, vc, t2)


def _rms(x, g):
    ms = jnp.mean(x * x, axis=-1, keepdims=True)
    return x * lax.rsqrt(ms + EPS) * g


TILE_ROWS = D_MODEL // LANES


def _store_token_tiles(ref, val):
    rows = val.shape[0]
    for j in range(TILE_ROWS):
        ref[pl.ds(j, rows, stride=TILE_ROWS), :] = val[:, LANES * j:LANES * (j + 1)]


def _load_token_tiles(ref, start, rows):
    return jnp.concatenate(
        [ref[pl.ds(start + j, rows, stride=TILE_ROWS), :] for j in range(TILE_ROWS)], axis=1)


def _out_proj_kernel(fr_ref, na_ref, x_ref, mod_ref, wf_ref, wo_ref, go_ref, g2_ref, wr_ref, tri_ref,
                     x1_ref, h2_ref, meta_ref, metat_ref, cnt_ref, run_scr):
    @pl.when((pl.program_id(0) == 0) & (pl.program_id(1) == 0))
    def _():
        run_scr[...] = jnp.zeros_like(run_scr)

    m = mod_ref[0]
    go = go_ref[...]
    fo = jnp.dot(fr_ref[0], wf_ref[...], preferred_element_type=F32)
    fn = _rms(fo, go[:, :D_FOURIER]).astype(BF16)
    nn = _rms(na_ref[0].astype(F32), go[:, D_FOURIER:]).astype(BF16)
    y = (jnp.dot(fn, wo_ref[0:D_FOURIER, :], preferred_element_type=F32)
         + jnp.dot(nn, wo_ref[D_FOURIER:, :], preferred_element_type=F32))
    x1 = x_ref[0] + m[2:3] * y
    x1_ref[0] = x1
    h2 = _norm_mod(x1, g2_ref[...], m[3:4], m[4:5])
    _store_token_tiles(h2_ref, h2)
    logits = jnp.dot(h2.astype(BF16), wr_ref[...], preferred_element_type=F32)

    tm = logits.shape[0]
    lane = lax.broadcasted_iota(jnp.int32, (tm, LANES), 1).astype(F32)
    ninf = jnp.float32(-jnp.inf)

    def argmax_first(vals):
        mx = jnp.max(vals, axis=1, keepdims=True)
        idx = jnp.min(jnp.where(vals == mx, lane, float(LANES)), axis=1, keepdims=True)
        return mx, idx

    lg = jnp.where(lane < N_GROUPS, logits, ninf)
    gmax, gidx = argmax_first(lg)
    pg = 1.0 / jnp.sum(jnp.exp(lg - gmax), axis=1, keepdims=True)
    lo = N_GROUPS + EXPERTS_PER_GROUP * gidx
    le = jnp.where((lane >= lo) & (lane < lo + EXPERTS_PER_GROUP), logits, ninf)
    e1, i1 = argmax_first(le)
    e2, i2 = argmax_first(jnp.where(lane == i1, ninf, le))
    dd = jnp.exp(e2 - e1)
    gate1 = pg / (1.0 + dd)
    gate2 = pg * dd / (1.0 + dd)

    hot1 = lane == i1
    hot2 = lane == i2
    onehot = jnp.where(hot1 | hot2, 1.0, 0.0)
    cnt = jnp.dot(tri_ref[...], onehot.astype(BF16), preferred_element_type=F32) + run_scr[...]
    rank1 = jnp.sum(jnp.where(hot1, cnt, 0.0), axis=1, keepdims=True)
    rank2 = jnp.sum(jnp.where(hot2, cnt, 0.0), axis=1, keepdims=True)
    run_scr[...] = run_scr[...] + jnp.sum(onehot, axis=0, keepdims=True)

    meta = jnp.where(lane == 0, i1 - N_GROUPS,
           jnp.where(lane == 1, i2 - N_GROUPS,
           jnp.where(lane == 2, rank1,
           jnp.where(lane == 3, rank2,
           jnp.where(lane == 4, gate1,
           jnp.where(lane == 5, gate2, 0.0))))))
    meta_ref[...] = meta
    metat_ref[...] = jnp.transpose(meta)[0:8, :]
    cnt_ref[...] = jnp.broadcast_to(run_scr[...], cnt_ref.shape)


def out_proj(fr, na, x, mods, w_fmix_bf16, w_out_bf16, g_out, g2, w_router_bf16):
    b, n, d = x.shape
    tm = TM_PROJ
    steps = n // tm
    tok = lambda bi, i: (bi, i, 0)
    const2 = lambda bi, i: (0, 0)
    flat = lambda bi, i: (bi * steps + i, 0)
    tri = _mxu_const(_strict_lower(tm))
    return pl.pallas_call(
        _out_proj_kernel,
        grid=(b, steps),
        in_specs=[pl.BlockSpec((1, tm, D_FOURIER), tok),
                  pl.BlockSpec((1, tm, D_NA), tok),
                  pl.BlockSpec((1, tm, d), tok),
                  pl.BlockSpec((1, 8, d), lambda bi, i: (bi, 0, 0)),
                  pl.BlockSpec((D_FOURIER, D_FOURIER), const2),
                  pl.BlockSpec((d, d), const2),
                  pl.BlockSpec((1, d), const2),
                  pl.BlockSpec((1, d), const2),
                  pl.BlockSpec((d, LANES), const2),
                  pl.BlockSpec((tm, tm), const2)],
        out_specs=[pl.BlockSpec((1, tm, d), tok),
                   pl.BlockSpec((tm * TILE_ROWS, LANES), flat),
                   pl.BlockSpec((tm, LANES), flat),
                   pl.BlockSpec((8, tm), lambda bi, i: (0, bi * steps + i)),
                   pl.BlockSpec((8, LANES), const2)],
        out_shape=[jax.ShapeDtypeStruct((b, n, d), F32),
                   jax.ShapeDtypeStruct((b * n * TILE_ROWS, LANES), F32),
                   jax.ShapeDtypeStruct((b * n, LANES), F32),
                   jax.ShapeDtypeStruct((8, b * n), F32),
                   jax.ShapeDtypeStruct((8, LANES), F32)],
        scratch_shapes=[pltpu.VMEM((1, LANES), F32)],
        compiler_params=_cparams(("arbitrary", "arbitrary")),
        name="out_proj",
    )(fr, na, x, mods, w_fmix_bf16, w_out_bf16, g_out.reshape(1, d), g2.reshape(1, d), w_router_bf16, tri)


DMA_CHUNK = 8


TD_DISP = 512


def _dispatch_kernel(packed_ref, pstart_ref, count_ref, h2_ref, xs_hbm, dest_ref, zero_scr, sem, pad_sem):
    i = pl.program_id(0)
    n_assign = packed_ref.shape[0]
    nt = n_assign // 2

    @pl.when(i == 0)
    def _():
        zero_scr[...] = jnp.zeros_like(zero_scr)

        def per_expert(e, npad):
            lo = pstart_ref[e] + count_ref[e]
            hi = pstart_ref[e] + ((count_ref[e] + MOE_BLK - 1) // MOE_BLK) * MOE_BLK

            def fill(s, carry):
                pltpu.make_async_copy(zero_scr.at[pl.ds(0, TILE_ROWS)],
                                      xs_hbm.at[pl.ds(s * TILE_ROWS, TILE_ROWS)], pad_sem).start()
                return carry

            lax.fori_loop(lo, hi, fill, 0)
            return npad + (hi - lo)

        npad = lax.fori_loop(0, N_EXPERTS, per_expert, 0)

        blk_rows = MOE_BLK * TILE_ROWS
        first_free = (pstart_ref[N_EXPERTS - 1] + count_ref[N_EXPERTS - 1] + MOE_BLK - 1) // MOE_BLK
        n_blocks = xs_hbm.shape[0] // blk_rows

        def fill_block(bk, carry):
            pltpu.make_async_copy(zero_scr, xs_hbm.at[pl.ds(bk * blk_rows, blk_rows)], pad_sem).start()
            return carry

        lax.fori_loop(first_free, n_blocks, fill_block, 0)
        rows = npad * TILE_ROWS + (n_blocks - first_free) * blk_rows

        @pl.when(rows > 0)
        def _():
            pltpu.make_async_copy(xs_hbm.at[pl.ds(0, rows)], xs_hbm.at[pl.ds(0, rows)], pad_sem).wait()

    def issue(c, carry):
        t0 = c * DMA_CHUNK
        for u in range(DMA_CHUNK):
            for k in range(2):
                a = k * nt + i * TD_DISP + t0 + u
                v = packed_ref[a]
                d = pstart_ref[v & (N_EXPERTS - 1)] + (v >> 5)
                dest_ref[a] = d
                pltpu.make_async_copy(h2_ref.at[pl.ds((t0 + u) * TILE_ROWS, TILE_ROWS)],
                                      xs_hbm.at[pl.ds(d * TILE_ROWS, TILE_ROWS)], sem).start()
        return carry

    lax.fori_loop(0, TD_DISP // DMA_CHUNK, issue, 0)
    rows = 2 * TD_DISP * TILE_ROWS
    pltpu.make_async_copy(xs_hbm.at[pl.ds(0, rows)], xs_hbm.at[pl.ds(0, rows)], sem).wait()


def dispatch(h2_tiles, packed, pstarts, counts, n_slots):
    nt = packed.shape[0] // 2
    grid_spec = pltpu.PrefetchScalarGridSpec(
        num_scalar_prefetch=3,
        grid=(nt // TD_DISP,),
        in_specs=[pl.BlockSpec((TD_DISP * TILE_ROWS, LANES), lambda i, pk, ps, ct: (i, 0))],
        out_specs=[pl.BlockSpec(memory_space=pl.ANY),
                   pl.BlockSpec(memory_space=pltpu.SMEM)],
        scratch_shapes=[pltpu.VMEM((MOE_BLK * TILE_ROWS, LANES), F32),
                        pltpu.SemaphoreType.DMA(()),
                        pltpu.SemaphoreType.DMA(())],
    )
    return pl.pallas_call(
        _dispatch_kernel,
        grid_spec=grid_spec,
        out_shape=[jax.ShapeDtypeStruct((n_slots * TILE_ROWS, LANES), F32),
                   jax.ShapeDtypeStruct(packed.shape, jnp.int32)],
        compiler_params=_cparams(("arbitrary",)),
        name="dispatch",
    )(packed, pstarts, counts, h2_tiles)


def _experts_kernel(be_ref, nused_ref, xs_ref, wg_ref, wu_ref, wd_ref, ys_ref, wg_scr, wu_scr, wd_scr):
    i = pl.program_id(0)
    nused = nused_ref[0]
    changed = (i == 0) | (be_ref[i] != be_ref[jnp.maximum(i - 1, 0)])

    @pl.when(changed & (i < nused))
    def _():
        wg_scr[...] = wg_ref[0].astype(BF16)
        wu_scr[...] = wu_ref[0].astype(BF16)
        wd_scr[...] = wd_ref[0].astype(BF16)

    @pl.when(i < nused)
    def _():
        x = _load_token_tiles(xs_ref, 0, MOE_BLK).astype(BF16)
        g = jnp.dot(x, wg_scr[...], preferred_element_type=F32)
        u = jnp.dot(x, wu_scr[...], preferred_element_type=F32)
        hmid = (g * jax.nn.sigmoid(g) * u).astype(BF16)
        _store_token_tiles(ys_ref, jnp.dot(hmid, wd_scr[...], preferred_element_type=F32))

    @pl.when(i >= nused)
    def _():
        ys_ref[...] = jnp.zeros_like(ys_ref)


def experts(xs_tiles, block_expert, nused, w_gate, w_up, w_down):
    d = D_MODEL
    nblk = block_expert.shape[0]
    blk_rows = MOE_BLK * TILE_ROWS
    wmap = lambda i, be, nu: (be[i], 0, 0)
    grid_spec = pltpu.PrefetchScalarGridSpec(
        num_scalar_prefetch=2,
        grid=(nblk,),
        in_specs=[pl.BlockSpec((blk_rows, LANES), lambda i, be, nu: (jnp.minimum(i, nu[0] - 1), 0)),
                  pl.BlockSpec((1, d, D_EXPERT), wmap),
                  pl.BlockSpec((1, d, D_EXPERT), wmap),
                  pl.BlockSpec((1, D_EXPERT, d), wmap)],
        out_specs=pl.BlockSpec((blk_rows, LANES), lambda i, be, nu: (i, 0)),
        scratch_shapes=[pltpu.VMEM((d, D_EXPERT), BF16),
                        pltpu.VMEM((d, D_EXPERT), BF16),
                        pltpu.VMEM((D_EXPERT, d), BF16)],
    )
    return pl.pallas_call(
        _experts_kernel,
        grid_spec=grid_spec,
        out_shape=jax.ShapeDtypeStruct((nblk * blk_rows, LANES), F32),
        compiler_params=_cparams(("arbitrary",)),
        name="experts",
    )(block_expert, nused, xs_tiles, w_gate, w_up, w_down)


def _combine_kernel(dest_ref, ys_hbm, x1_ref, meta_ref, mod_ref, gf_ref, o_ref, ybuf, sem):
    i = pl.program_id(0)
    nstep = pl.num_programs(0)
    tc = TC_COMB
    nt = dest_ref.shape[0] // 2
    half_rows = tc * TILE_ROWS
    buf_rows = 2 * half_rows

    def gather(step, slot):
        def issue(c, carry):
            t0 = c * DMA_CHUNK
            for u in range(DMA_CHUNK):
                for k in range(2):
                    d = dest_ref[k * nt + step * tc + t0 + u]
                    pltpu.make_async_copy(
                        ys_hbm.at[pl.ds(d * TILE_ROWS, TILE_ROWS)],
                        ybuf.at[pl.ds(slot * buf_rows + k * half_rows + (t0 + u) * TILE_ROWS, TILE_ROWS)],
                        sem.at[slot]).start()
            return carry

        lax.fori_loop(0, tc // DMA_CHUNK, issue, 0)

    @pl.when(i == 0)
    def _():
        gather(0, 0)

    @pl.when(i + 1 < nstep)
    def _():
        gather(i + 1, (i + 1) % 2)

    slot = i % 2
    start = pl.multiple_of(slot * buf_rows, buf_rows)
    pltpu.make_async_copy(ys_hbm.at[pl.ds(0, buf_rows)], ybuf.at[pl.ds(start, buf_rows)], sem.at[slot]).wait()
    meta = meta_ref[...]
    y0 = _load_token_tiles(ybuf, start, tc)
    y1 = _load_token_tiles(ybuf, start + half_rows, tc)
    moe = y0 * meta[:, 4:5] + y1 * meta[:, 5:6]
    x2 = x1_ref[...] + mod_ref[0][5:6] * moe
    o_ref[...] = _rms(x2, gf_ref[...])


def combine(dest_flat, ys, x1_flat, meta, mods, g_final, n_per_batch):
    nt, d = x1_flat.shape
    tc = TC_COMB
    per_b = n_per_batch // tc
    grid_spec = pltpu.PrefetchScalarGridSpec(
        num_scalar_prefetch=1,
        grid=(nt // tc,),
        in_specs=[pl.BlockSpec(memory_space=pl.ANY),
                  pl.BlockSpec((tc, d), lambda i, ds: (i, 0)),
                  pl.BlockSpec((tc, LANES), lambda i, ds: (i, 0)),
                  pl.BlockSpec((1, 8, d), lambda i, ds: (i // per_b, 0, 0)),
                  pl.BlockSpec((1, d), lambda i, ds: (0, 0))],
        out_specs=pl.BlockSpec((tc, d), lambda i, ds: (i, 0)),
        scratch_shapes=[pltpu.VMEM((2 * 2 * tc * TILE_ROWS, LANES), F32),
                        pltpu.SemaphoreType.DMA((2,))],
    )
    return pl.pallas_call(
        _combine_kernel,
        grid_spec=grid_spec,
        out_shape=jax.ShapeDtypeStruct((nt, d), F32),
        compiler_params=_cparams(("arbitrary",)),
        name="combine",
    )(dest_flat, ys, x1_flat, meta, mods, g_final.reshape(1, d))


def _dispatch_plan(metat, counts_row, nt):
    packed = (metat[0:2] + float(N_EXPERTS) * metat[2:4]).astype(jnp.int32).reshape(-1)
    counts = counts_row[N_GROUPS:N_GROUPS + N_EXPERTS].astype(jnp.int32)
    pcounts = ((counts + MOE_BLK - 1) // MOE_BLK) * MOE_BLK
    pends = jnp.cumsum(pcounts)
    pstarts = pends - pcounts
    nblk = (nt * 2) // MOE_BLK + N_EXPERTS
    first_slot = jnp.arange(nblk, dtype=jnp.int32) * MOE_BLK
    block_expert = jnp.minimum(
        jnp.sum((pends[None, :] <= first_slot[:, None]).astype(jnp.int32), axis=1), N_EXPERTS - 1)
    nused = (pends[-1] // MOE_BLK).astype(jnp.int32).reshape(1)
    return packed, pstarts.astype(jnp.int32), counts, block_expert.astype(jnp.int32), nused


def kernel(x, c, ctx, c_ctx, w_ada, b_ada, g_norm1, w_in, w_fmix, rpb, g_out, w_out, g_norm2,
           w_router_group, w_router_expert, w_gate, w_up, w_down, g_final):
    b, n, d = x.shape
    assert (b, n, d) == (c.shape[0], SEQ, D_MODEL) and w_ada.shape[0] == 1
    nt = b * n

    cond8 = jnp.zeros((8, d), F32).at[0:b].set(c).at[b].set(c_ctx)
    mod = adaln(cond8, w_ada[0], b_ada[0])
    mods = jnp.pad(mod[0:b].reshape(b, N_MOD, d), ((0, 0), (0, 2), (0, 0)))
    mod_ctx = jnp.pad(mod[b].reshape(N_MOD, d), ((0, 2), (0, 0)))

    w_in_b = w_in[0].astype(BF16)
    qr, qp, kr, v, a = in_proj(x, mods, g_norm1[0], w_in_b)
    kc, vc = ctx_proj(ctx, mod_ctx, g_norm1[0], w_in_b[:, D_FOURIER + D_NA:])

    fr = dft_cols(dft_rows(a))
    na = attention(qr, qp, kr, v, kc, vc, bias_tables(rpb[0]))

    w_router = jnp.concatenate(
        [w_router_group[0], w_router_expert[0],
         jnp.zeros((d, LANES - N_GROUPS - N_EXPERTS), F32)], axis=1).astype(BF16)
    x1, h2_tiles, meta, metat, cnt = out_proj(fr, na, x, mods, w_fmix[0].astype(BF16),
                                              w_out[0].astype(BF16), g_out[0], g_norm2[0], w_router)

    packed, pstarts, counts, block_expert, nused = _dispatch_plan(metat, cnt[0], nt)
    xs_tiles, dest = dispatch(h2_tiles, packed, pstarts, counts, block_expert.shape[0] * MOE_BLK)
    ys_tiles = experts(xs_tiles, block_expert, nused, w_gate[0], w_up[0], w_down[0])
    out = combine(dest, ys_tiles, x1.reshape(nt, d), meta, mods, g_final, n)
    return out.reshape(b, n, d)
```

```python
import functools
import math

import numpy as np
import jax
import jax.numpy as jnp
from jax import lax
from jax.experimental import pallas as pl
from jax.experimental.pallas import tpu as pltpu

F32 = jnp.float32
BF16 = jnp.bfloat16

D_MODEL = 1024
GRID_W = 64
GRID_H = 128
SEQ = GRID_W * GRID_H
CTX_LEN = 256
D_FOURIER = 256
FOURIER_GROUP = 64
HEAD_DIM = 64
N_HEADS = 12
D_NA = N_HEADS * HEAD_DIM
N_PAIRS = N_HEADS // 2
NA_ROWS = 8
NA_COLS = 16
ROPE_THETA = 10000.0
N_GROUPS = 4
EXPERTS_PER_GROUP = 8
N_EXPERTS = N_GROUPS * EXPERTS_PER_GROUP
D_EXPERT = 512
N_MOD = 6
D_IN_PROJ = D_FOURIER + 3 * D_NA
EPS = 1e-6
LANES = 128
NEG = -1e30
LOG2E = math.log2(math.e)

TM_PROJ = 512
ATT_ROWS = 16
DFT_TW = 8
DFT_TN = 4096
MOE_BLK = 256
TC_COMB = 256
VMEM_LIMIT = 56 * 1024 * 1024


def _cparams(sem):
    return pltpu.CompilerParams(dimension_semantics=sem, vmem_limit_bytes=VMEM_LIMIT)


def _mxu_const(table):
    return jnp.asarray(table, F32).astype(BF16)


@functools.lru_cache(maxsize=None)
def _rope_tables():
    t = np.arange(SEQ)
    row, col = t // GRID_W, t % GRID_W
    lane = np.arange(LANES)
    d = lane % HEAD_DIM
    chunk = d // 32
    e = d % 32
    j = e % 16
    inv = ROPE_THETA ** (-(j.astype(np.float64)) / 16.0)
    pos = np.where(chunk[None, :] == 0, row[:, None], col[:, None]).astype(np.float64)
    ang = pos * inv[None, :]
    cos = np.cos(ang)
    sin = np.sin(ang)
    first = (e < 16)[None, :]
    s_first = np.where(first, -sin, 0.0)
    s_second = np.where(first, 0.0, sin)
    return (cos.astype(np.float32), s_first.astype(np.float32), s_second.astype(np.float32))


@functools.lru_cache(maxsize=None)
def _chan_dft():
    c = np.arange(FOURIER_GROUP)
    ang = 2.0 * np.pi * ((c[:, None] * c[None, :]) % FOURIER_GROUP) / FOURIER_GROUP
    eye = np.eye(D_FOURIER // FOURIER_GROUP)
    re = np.kron(eye, np.cos(ang))
    im = np.kron(eye, -np.sin(ang))
    return np.concatenate([re, im], axis=1).astype(np.float32)


@functools.lru_cache(maxsize=None)
def _row_dft():
    k1 = np.arange(GRID_H)[:, None]
    r = np.arange(GRID_H)[None, :]
    out = np.zeros((GRID_W, 2 * GRID_H, 2 * GRID_H), np.float32)
    for w in range(GRID_W):
        m = (k1 * (GRID_W * r + w)) % SEQ
        ang = 2.0 * np.pi * m / SEQ
        c, s = np.cos(ang), np.sin(ang)
        out[w] = np.block([[c, s], [-s, c]])
    return out


@functools.lru_cache(maxsize=None)
def _col_dft():
    k2 = np.arange(GRID_W)
    ang = 2.0 * np.pi * ((k2[:, None] * k2[None, :]) % GRID_W) / GRID_W
    scale = 1.0 / math.sqrt(SEQ * FOURIER_GROUP)
    return (np.concatenate([np.cos(ang), np.sin(ang)], axis=1) * scale).astype(np.float32)


@functools.lru_cache(maxsize=None)
def _bias_index():
    c = np.arange(GRID_W)
    start = np.clip(c - NA_COLS // 2, 0, GRID_W - NA_COLS)
    valid = (c[None, :] >= start[:, None]) & (c[None, :] < start[:, None] + NA_COLS)
    dc = np.clip(c[None, :] - c[:, None] + (NA_COLS - 1), 0, 2 * NA_COLS - 2)
    return dc.astype(np.int32), valid


@functools.lru_cache(maxsize=None)
def _strict_lower(n):
    return np.tril(np.ones((n, n), np.float32), k=-1)


def _adaln_kernel(c_ref, w_ref, b_ref, o_ref):
    c = c_ref[...]
    s = c * jax.nn.sigmoid(c)
    o_ref[...] = jnp.dot(s, w_ref[...], precision=lax.Precision.HIGHEST,
                         preferred_element_type=F32) + b_ref[...]


def adaln(cond8, w, b):
    n = w.shape[1]
    tn = 1536
    return pl.pallas_call(
        _adaln_kernel,
        grid=(n // tn,),
        in_specs=[pl.BlockSpec((8, D_MODEL), lambda j: (0, 0)),
                  pl.BlockSpec((D_MODEL, tn), lambda j: (0, j)),
                  pl.BlockSpec((1, tn), lambda j: (0, j))],
        out_specs=pl.BlockSpec((8, tn), lambda j: (0, j)),
        out_shape=jax.ShapeDtypeStruct((8, n), F32),
        compiler_params=_cparams(("arbitrary",)),
        name="adaln",
    )(cond8, w, b.reshape(1, n))


def _norm_mod(x, g, shift, scale):
    ms = jnp.mean(x * x, axis=-1, keepdims=True)
    return (x * lax.rsqrt(ms + EPS) * g) * (1.0 + scale) + shift


def _in_proj_kernel(x_ref, mod_ref, g_ref, w_ref, cs_ref, cos_ref, s1_ref, s2_ref,
                    qr_ref, qp_ref, kr_ref, v_ref, a_ref, h_scr):
    m = mod_ref[0]
    h_scr[...] = _norm_mod(x_ref[0], g_ref[...], m[0:1], m[1:2]).astype(BF16)
    cos, s1, s2 = cos_ref[...], s1_ref[...], s2_ref[...]

    def rope(t):
        return (t * cos + pltpu.roll(t, LANES - 16, axis=1) * s1 + pltpu.roll(t, 16, axis=1) * s2)

    f = jnp.dot(h_scr[...], w_ref[:, 0:D_FOURIER], preferred_element_type=F32)
    a = jnp.dot(f.astype(BF16), cs_ref[...], preferred_element_type=F32)
    a_ref[0, 0] = a[:, :D_FOURIER].astype(BF16)
    a_ref[0, 1] = a[:, D_FOURIER:].astype(BF16)

    scale = HEAD_DIM ** -0.5 * LOG2E
    for c in range(D_NA // 256):
        lo = D_FOURIER + 256 * c
        q = jnp.dot(h_scr[...], w_ref[:, lo:lo + 256], preferred_element_type=F32)
        k = jnp.dot(h_scr[...], w_ref[:, lo + D_NA:lo + D_NA + 256], preferred_element_type=F32)
        v = jnp.dot(h_scr[...], w_ref[:, lo + 2 * D_NA:lo + 2 * D_NA + 256], preferred_element_type=F32)
        v_ref[0, :, 256 * c:256 * c + 256] = v.astype(BF16)
        for s in range(2):
            sl = slice(LANES * s, LANES * (s + 1))
            ol = slice(256 * c + LANES * s, 256 * c + LANES * (s + 1))
            qs, ks = q[:, sl], k[:, sl]
            qp_ref[0, :, ol] = (qs * scale).astype(BF16)
            qr_ref[0, :, ol] = (rope(qs) * scale).astype(BF16)
            kr_ref[0, :, ol] = rope(ks).astype(BF16)


def in_proj(x, mods, g1, w_in_bf16):
    b, n, d = x.shape
    tm = TM_PROJ
    cos, s1, s2 = _rope_tables()
    cs = _mxu_const(_chan_dft())
    tok = lambda bi, i: (bi, i, 0)
    const2 = lambda bi, i: (0, 0)
    tab = pl.BlockSpec((tm, LANES), lambda bi, i: (i, 0))
    qkv_shape = jax.ShapeDtypeStruct((b, n, D_NA), BF16)
    qkv_spec = pl.BlockSpec((1, tm, D_NA), tok)
    return pl.pallas_call(
        _in_proj_kernel,
        grid=(b, n // tm),
        in_specs=[pl.BlockSpec((1, tm, d), tok),
                  pl.BlockSpec((1, 8, d), lambda bi, i: (bi, 0, 0)),
                  pl.BlockSpec((1, d), const2),
                  pl.BlockSpec((d, D_IN_PROJ), const2),
                  pl.BlockSpec((D_FOURIER, 2 * D_FOURIER), const2),
                  tab, tab, tab],
        out_specs=[qkv_spec, qkv_spec, qkv_spec, qkv_spec,
                   pl.BlockSpec((1, 2, tm, D_FOURIER), lambda bi, i: (bi, 0, i, 0))],
        out_shape=[qkv_shape, qkv_shape, qkv_shape, qkv_shape,
                   jax.ShapeDtypeStruct((b, 2, n, D_FOURIER), BF16)],
        scratch_shapes=[pltpu.VMEM((tm, d), BF16)],
        compiler_params=_cparams(("arbitrary", "arbitrary")),
        name="in_proj",
    )(x, mods, g1.reshape(1, d), w_in_bf16, cs, jnp.asarray(cos), jnp.asarray(s1), jnp.asarray(s2))


def _ctx_proj_kernel(x_ref, mod_ref, g_ref, w_ref, k_ref, v_ref):
    m = mod_ref[...]
    h = _norm_mod(x_ref[0], g_ref[...], m[0:1], m[1:2]).astype(BF16)
    k_ref[0] = jnp.dot(h, w_ref[:, 0:D_NA], preferred_element_type=F32).astype(BF16)
    v_ref[0] = jnp.dot(h, w_ref[:, D_NA:2 * D_NA], preferred_element_type=F32).astype(BF16)


def ctx_proj(ctx, mod_ctx, g1, w_kv_bf16):
    b, l, d = ctx.shape
    shape = jax.ShapeDtypeStruct((b, l, D_NA), BF16)
    spec = pl.BlockSpec((1, l, D_NA), lambda bi: (bi, 0, 0))
    return pl.pallas_call(
        _ctx_proj_kernel,
        grid=(b,),
        in_specs=[pl.BlockSpec((1, l, d), lambda bi: (bi, 0, 0)),
                  pl.BlockSpec((8, d), lambda bi: (0, 0)),
                  pl.BlockSpec((1, d), lambda bi: (0, 0)),
                  pl.BlockSpec((d, 2 * D_NA), lambda bi: (0, 0))],
        out_specs=[spec, spec],
        out_shape=[shape, shape],
        compiler_params=_cparams(("arbitrary",)),
        name="ctx_proj",
    )(ctx, mod_ctx, g1.reshape(1, d), w_kv_bf16)


def _dft_rows_kernel(a_ref, g_ref, z_ref):
    for j in range(DFT_TW):
        sl = slice(D_FOURIER * j, D_FOURIER * (j + 1))
        rhs = jnp.concatenate([a_ref[0, 0, :, sl], a_ref[0, 1, :, sl]], axis=0)
        z = jnp.dot(g_ref[j], rhs, preferred_element_type=F32)
        z_ref[0, 0, j] = z[:GRID_H].astype(BF16)
        z_ref[0, 1, j] = z[GRID_H:].astype(BF16)


def dft_rows(a):
    b = a.shape[0]
    a4 = a.reshape(b, 2, GRID_H, GRID_W * D_FOURIER)
    g = _mxu_const(_row_dft())
    return pl.pallas_call(
        _dft_rows_kernel,
        grid=(GRID_W // DFT_TW, b),
        in_specs=[pl.BlockSpec((1, 2, GRID_H, DFT_TW * D_FOURIER), lambda j, bi: (bi, 0, 0, j)),
                  pl.BlockSpec((DFT_TW, 2 * GRID_H, 2 * GRID_H), lambda j, bi: (j, 0, 0))],
        out_specs=pl.BlockSpec((1, 2, DFT_TW, GRID_H, D_FOURIER), lambda j, bi: (bi, 0, j, 0, 0)),
        out_shape=jax.ShapeDtypeStruct((b, 2, GRID_W, GRID_H, D_FOURIER), BF16),
        compiler_params=_cparams(("arbitrary", "arbitrary")),
        name="dft_rows",
    )(a4, g)


def _dft_cols_kernel(z_ref, cs_ref, o_ref):
    rhs = jnp.concatenate([z_ref[0, 0], z_ref[0, 1]], axis=0)
    o_ref[0] = jnp.dot(cs_ref[...], rhs, preferred_element_type=F32).astype(BF16)


def dft_cols(z):
    b = z.shape[0]
    ncol = GRID_H * D_FOURIER
    z4 = z.reshape(b, 2, GRID_W, ncol)
    cs = _mxu_const(_col_dft())
    out = pl.pallas_call(
        _dft_cols_kernel,
        grid=(b, ncol // DFT_TN),
        in_specs=[pl.BlockSpec((1, 2, GRID_W, DFT_TN), lambda bi, j: (bi, 0, 0, j)),
                  pl.BlockSpec((GRID_W, 2 * GRID_W), lambda bi, j: (0, 0))],
        out_specs=pl.BlockSpec((1, GRID_W, DFT_TN), lambda bi, j: (bi, 0, j)),
        out_shape=jax.ShapeDtypeStruct((b, GRID_W, ncol), BF16),
        compiler_params=_cparams(("arbitrary", "arbitrary")),
        name="dft_cols",
    )(z4, cs)
    return out.reshape(b, SEQ, D_FOURIER)


WIN_ROWS = NA_ROWS + 2
WIN_KEYS = WIN_ROWS * GRID_W
WIN_TILES = WIN_KEYS // LANES
N_DR = 2 * NA_ROWS + 1
PAIRS_PER_TRIP = 4


def _attention_kernel(qr_ref, qp_ref, k_ref, v_ref, kc_ref, vc_ref, tb_ref, o_ref,
                      sc_scr, pc_scr, s_scr, p_scr, o_scr):
    rb = pl.program_id(2)
    nq = ATT_ROWS * GRID_W
    lane = lax.broadcasted_iota(jnp.int32, (nq, LANES), 1)
    first = lane < HEAD_DIM
    first2 = lax.broadcasted_iota(jnp.int32, (2 * GRID_W, LANES), 1) < HEAD_DIM
    nt = (((1,), (1,)), ((), ()))

    qp = qp_ref[0]
    zero = jnp.zeros_like(qp)
    sc_scr[0] = lax.dot_general(jnp.where(first, qp, zero), kc_ref[0], nt, preferred_element_type=F32)
    sc_scr[1] = lax.dot_general(jnp.where(first, zero, qp), kc_ref[0], nt, preferred_element_type=F32)

    def row_pair(jp, u):
        r0 = rb * ATT_ROWS + 2 * jp
        ws = jnp.clip(r0 - NA_ROWS // 2, 0, GRID_H - WIN_ROWS)
        qoff = pl.multiple_of(jp * 2 * GRID_W, 2 * GRID_W)
        koff = pl.multiple_of(ws * GRID_W, GRID_W)
        q2 = qr_ref[0, pl.ds(qoff, 2 * GRID_W), :]
        z2 = jnp.zeros_like(q2)
        qm = jnp.concatenate([jnp.where(first2, q2, z2), jnp.where(first2, z2, q2)], axis=0)
        kw = k_ref[0, pl.ds(koff, WIN_KEYS), :]
        s_scr[u] = lax.dot_general(qm, kw, nt, preferred_element_type=F32)
        for c in range(4):
            hh, i = c // 2, c % 2
            r = r0 + i
            off = jnp.clip(r - NA_ROWS // 2, 0, GRID_H - NA_ROWS) - ws
            rows = slice(c * GRID_W, (c + 1) * GRID_W)
            tiles = []
            for t in range(WIN_TILES):
                v0 = (2 * t >= off) & (2 * t < off + NA_ROWS)
                v1 = (2 * t + 1 >= off) & (2 * t + 1 < off + NA_ROWS)
                kind = jnp.where(v0 & v1, 0, jnp.where(v0, 2, 1))
                didx = jnp.where(v0 | v1, ws + 2 * t - r + NA_ROWS, N_DR - 1)
                tiles.append(s_scr[u, rows, t * LANES:(t + 1) * LANES] + tb_ref[0, hh, kind, didx])
            qrow = pl.ds(qoff + i * GRID_W, GRID_W)
            sc = sc_scr[hh, qrow, :]
            mt = jnp.maximum(sc[:, :LANES], sc[:, LANES:])
            for tl in tiles:
                mt = jnp.maximum(mt, tl)
            m = jnp.max(mt, axis=1, keepdims=True)
            pc_scr[hh, qrow, :] = jnp.exp2(sc - m).astype(BF16)
            for t, tl in enumerate(tiles):
                p_scr[u, rows, t * LANES:(t + 1) * LANES] = jnp.exp2(tl - m).astype(BF16)
        vw = jnp.concatenate([v_ref[0, pl.ds(koff, WIN_KEYS), :], jnp.ones((WIN_KEYS, LANES), BF16)], axis=1)
        o = jnp.dot(p_scr[u], vw, preferred_element_type=F32)
        o_scr[0, pl.ds(qoff, 2 * GRID_W), :] = o[:2 * GRID_W]
        o_scr[1, pl.ds(qoff, 2 * GRID_W), :] = o[2 * GRID_W:]

    def trip(it, carry):
        for u in range(PAIRS_PER_TRIP):
            row_pair(it * PAIRS_PER_TRIP + u, u)
        return carry

    lax.fori_loop(0, ATT_ROWS // (2 * PAIRS_PER_TRIP), trip, 0)

    vc = jnp.concatenate([vc_ref[0], jnp.ones((CTX_LEN, LANES), BF16)], axis=1)
    oa = o_scr[0] + jnp.dot(pc_scr[0], vc, preferred_element_type=F32)
    ob = o_scr[1] + jnp.dot(pc_scr[1], vc, preferred_element_type=F32)
    o_ref[0] = jnp.where(first, oa[:, :LANES] / oa[:, LANES:], ob[:, :LANES] / ob[:, LANES:]).astype(BF16)


def bias_tables(rpb):
    dc, valid = _bias_index()
    n_dc = 2 * NA_COLS - 1
    onehot = (dc.reshape(1, -1) == np.arange(n_dc).reshape(-1, 1)).astype(np.float32)
    t = jnp.dot(rpb.reshape(-1, n_dc), jnp.asarray(onehot), precision=lax.Precision.HIGHEST)
    t = jnp.where(valid[None, None], LOG2E * t.reshape(N_HEADS, 2 * NA_ROWS - 1, GRID_W, GRID_W), NEG)
    t = jnp.pad(t, ((0, 0), (1, 2), (0, 0), (0, 0)), constant_values=NEG)
    left, right = t[:, :N_DR], t[:, 1:N_DR + 1]
    neg = jnp.full_like(left, NEG)
    tb = jnp.stack([jnp.concatenate([left, right], axis=-1),
                    jnp.concatenate([neg, right], axis=-1),
                    jnp.concatenate([left, neg], axis=-1)], axis=1)
    return tb.reshape(N_PAIRS, 2, 3, N_DR, GRID_W, LANES).astype(F32)


def attention(qr, qp, kr, v, kc, vc, tb):
    b, n, _ = qr.shape
    tq = ATT_ROWS * GRID_W
    qspec = pl.BlockSpec((1, tq, LANES), lambda bi, hp, i: (bi, i, hp))
    kspec = pl.BlockSpec((1, n, LANES), lambda bi, hp, i: (bi, 0, hp))
    cspec = pl.BlockSpec((1, CTX_LEN, LANES), lambda bi, hp, i: (bi, 0, hp))
    return pl.pallas_call(
        _attention_kernel,
        grid=(b, N_PAIRS, GRID_H // ATT_ROWS),
        in_specs=[qspec, qspec, kspec, kspec, cspec, cspec,
                  pl.BlockSpec((1, 2, 3, N_DR, GRID_W, LANES), lambda bi, hp, i: (hp, 0, 0, 0, 0, 0))],
        out_specs=qspec,
        out_shape=jax.ShapeDtypeStruct((b, n, D_NA), BF16),
        scratch_shapes=[pltpu.VMEM((2, tq, CTX_LEN), F32),
                        pltpu.VMEM((2, tq, CTX_LEN), BF16),
                        pltpu.VMEM((PAIRS_PER_TRIP, 4 * GRID_W, WIN_KEYS), F32),
                        pltpu.VMEM((PAIRS_PER_TRIP, 4 * GRID_W, WIN_KEYS), BF16),
                        pltpu.VMEM((2, tq, 2 * LANES), F32)],
        compiler_params=_cparams(("arbitrary", "arbitrary", "arbitrary")),
        name="attention",
    )(qr, qp, kr, v, kc, vc, tb)


def _rms(x, g):
    ms = jnp.mean(x * x, axis=-1, keepdims=True)
    return x * lax.rsqrt(ms + EPS) * g


TILE_ROWS = D_MODEL // LANES


def _store_token_tiles(ref, val):
    rows = val.shape[0]
    for j in range(TILE_ROWS):
        ref[pl.ds(j, rows, stride=TILE_ROWS), :] = val[:, LANES * j:LANES * (j + 1)]


def _load_token_tiles(ref, start, rows):
    return jnp.concatenate(
        [ref[pl.ds(start + j, rows, stride=TILE_ROWS), :] for j in range(TILE_ROWS)], axis=1)


def _out_proj_kernel(fr_ref, na_ref, x_ref, mod_ref, wf_ref, wo_ref, go_ref, g2_ref, wr_ref, tri_ref,
                     x1_ref, h2_ref, meta_ref, metat_ref, cnt_ref, run_scr):
    @pl.when((pl.program_id(0) == 0) & (pl.program_id(1) == 0))
    def _():
        run_scr[...] = jnp.zeros_like(run_scr)

    m = mod_ref[0]
    go = go_ref[...]
    fo = jnp.dot(fr_ref[0], wf_ref[...], preferred_element_type=F32)
    fn = _rms(fo, go[:, :D_FOURIER]).astype(BF16)
    nn = _rms(na_ref[0].astype(F32), go[:, D_FOURIER:]).astype(BF16)
    y = (jnp.dot(fn, wo_ref[0:D_FOURIER, :], preferred_element_type=F32)
         + jnp.dot(nn, wo_ref[D_FOURIER:, :], preferred_element_type=F32))
    x1 = x_ref[0] + m[2:3] * y
    x1_ref[0] = x1
    h2 = _norm_mod(x1, g2_ref[...], m[3:4], m[4:5])
    _store_token_tiles(h2_ref, h2)
    logits = jnp.dot(h2.astype(BF16), wr_ref[...], preferred_element_type=F32)

    tm = logits.shape[0]
    lane = lax.broadcasted_iota(jnp.int32, (tm, LANES), 1).astype(F32)
    ninf = jnp.float32(-jnp.inf)

    def argmax_first(vals):
        mx = jnp.max(vals, axis=1, keepdims=True)
        idx = jnp.min(jnp.where(vals == mx, lane, float(LANES)), axis=1, keepdims=True)
        return mx, idx

    lg = jnp.where(lane < N_GROUPS, logits, ninf)
    gmax, gidx = argmax_first(lg)
    pg = 1.0 / jnp.sum(jnp.exp(lg - gmax), axis=1, keepdims=True)
    lo = N_GROUPS + EXPERTS_PER_GROUP * gidx
    le = jnp.where((lane >= lo) & (lane < lo + EXPERTS_PER_GROUP), logits, ninf)
    e1, i1 = argmax_first(le)
    e2, i2 = argmax_first(jnp.where(lane == i1, ninf, le))
    dd = jnp.exp(e2 - e1)
    gate1 = pg / (1.0 + dd)
    gate2 = pg * dd / (1.0 + dd)

    hot1 = lane == i1
    hot2 = lane == i2
    onehot = jnp.where(hot1 | hot2, 1.0, 0.0)
    cnt = jnp.dot(tri_ref[...], onehot.astype(BF16), preferred_element_type=F32) + run_scr[...]
    rank1 = jnp.sum(jnp.where(hot1, cnt, 0.0), axis=1, keepdims=True)
    rank2 = jnp.sum(jnp.where(hot2, cnt, 0.0), axis=1, keepdims=True)
    run_scr[...] = run_scr[...] + jnp.sum(onehot, axis=0, keepdims=True)

    meta = jnp.where(lane == 0, i1 - N_GROUPS,
           jnp.where(lane == 1, i2 - N_GROUPS,
           jnp.where(lane == 2, rank1,
           jnp.where(lane == 3, rank2,
           jnp.where(lane == 4, gate1,
           jnp.where(lane == 5, gate2, 0.0))))))
    meta_ref[...] = meta
    metat_ref[...] = jnp.transpose(meta)[0:8, :]
    cnt_ref[...] = jnp.broadcast_to(run_scr[...], cnt_ref.shape)


def out_proj(fr, na, x, mods, w_fmix_bf16, w_out_bf16, g_out, g2, w_router_bf16):
    b, n, d = x.shape
    tm = TM_PROJ
    steps = n // tm
    tok = lambda bi, i: (bi, i, 0)
    const2 = lambda bi, i: (0, 0)
    flat = lambda bi, i: (bi * steps + i, 0)
    tri = _mxu_const(_strict_lower(tm))
    return pl.pallas_call(
        _out_proj_kernel,
        grid=(b, steps),
        in_specs=[pl.BlockSpec((1, tm, D_FOURIER), tok),
                  pl.BlockSpec((1, tm, D_NA), tok),
                  pl.BlockSpec((1, tm, d), tok),
                  pl.BlockSpec((1, 8, d), lambda bi, i: (bi, 0, 0)),
                  pl.BlockSpec((D_FOURIER, D_FOURIER), const2),
                  pl.BlockSpec((d, d), const2),
                  pl.BlockSpec((1, d), const2),
                  pl.BlockSpec((1, d), const2),
                  pl.BlockSpec((d, LANES), const2),
                  pl.BlockSpec((tm, tm), const2)],
        out_specs=[pl.BlockSpec((1, tm, d), tok),
                   pl.BlockSpec((tm * TILE_ROWS, LANES), flat),
                   pl.BlockSpec((tm, LANES), flat),
                   pl.BlockSpec((8, tm), lambda bi, i: (0, bi * steps + i)),
                   pl.BlockSpec((8, LANES), const2)],
        out_shape=[jax.ShapeDtypeStruct((b, n, d), F32),
                   jax.ShapeDtypeStruct((b * n * TILE_ROWS, LANES), F32),
                   jax.ShapeDtypeStruct((b * n, LANES), F32),
                   jax.ShapeDtypeStruct((8, b * n), F32),
                   jax.ShapeDtypeStruct((8, LANES), F32)],
        scratch_shapes=[pltpu.VMEM((1, LANES), F32)],
        compiler_params=_cparams(("arbitrary", "arbitrary")),
        name="out_proj",
    )(fr, na, x, mods, w_fmix_bf16, w_out_bf16, g_out.reshape(1, d), g2.reshape(1, d), w_router_bf16, tri)


DMA_CHUNK = 8


TD_DISP = 512


def _dispatch_kernel(packed_ref, pstart_ref, count_ref, h2_ref, xs_hbm, dest_ref, zero_scr, sem, pad_sem):
    i = pl.program_id(0)
    n_assign = packed_ref.shape[0]
    nt = n_assign // 2

    @pl.when(i == 0)
    def _():
        zero_scr[...] = jnp.zeros_like(zero_scr)

        def per_expert(e, npad):
            lo = pstart_ref[e] + count_ref[e]
            hi = pstart_ref[e] + ((count_ref[e] + MOE_BLK - 1) // MOE_BLK) * MOE_BLK

            def fill(s, carry):
                pltpu.make_async_copy(zero_scr.at[pl.ds(0, TILE_ROWS)],
                                      xs_hbm.at[pl.ds(s * TILE_ROWS, TILE_ROWS)], pad_sem).start()
                return carry

            lax.fori_loop(lo, hi, fill, 0)
            return npad + (hi - lo)

        npad = lax.fori_loop(0, N_EXPERTS, per_expert, 0)

        blk_rows = MOE_BLK * TILE_ROWS
        first_free = (pstart_ref[N_EXPERTS - 1] + count_ref[N_EXPERTS - 1] + MOE_BLK - 1) // MOE_BLK
        n_blocks = xs_hbm.shape[0] // blk_rows

        def fill_block(bk, carry):
            pltpu.make_async_copy(zero_scr, xs_hbm.at[pl.ds(bk * blk_rows, blk_rows)], pad_sem).start()
            return carry

        lax.fori_loop(first_free, n_blocks, fill_block, 0)
        rows = npad * TILE_ROWS + (n_blocks - first_free) * blk_rows

        @pl.when(rows > 0)
        def _():
            pltpu.make_async_copy(xs_hbm.at[pl.ds(0, rows)], xs_hbm.at[pl.ds(0, rows)], pad_sem).wait()

    def issue(c, carry):
        t0 = c * DMA_CHUNK
        for u in range(DMA_CHUNK):
            for k in range(2):
                a = k * nt + i * TD_DISP + t0 + u
                v = packed_ref[a]
                d = pstart_ref[v & (N_EXPERTS - 1)] + (v >> 5)
                dest_ref[a] = d
                pltpu.make_async_copy(h2_ref.at[pl.ds((t0 + u) * TILE_ROWS, TILE_ROWS)],
                                      xs_hbm.at[pl.ds(d * TILE_ROWS, TILE_ROWS)], sem).start()
        return carry

    lax.fori_loop(0, TD_DISP // DMA_CHUNK, issue, 0)
    rows = 2 * TD_DISP * TILE_ROWS
    pltpu.make_async_copy(xs_hbm.at[pl.ds(0, rows)], xs_hbm.at[pl.ds(0, rows)], sem).wait()


def dispatch(h2_tiles, packed, pstarts, counts, n_slots):
    nt = packed.shape[0] // 2
    grid_spec = pltpu.PrefetchScalarGridSpec(
        num_scalar_prefetch=3,
        grid=(nt // TD_DISP,),
        in_specs=[pl.BlockSpec((TD_DISP * TILE_ROWS, LANES), lambda i, pk, ps, ct: (i, 0))],
        out_specs=[pl.BlockSpec(memory_space=pl.ANY),
                   pl.BlockSpec(memory_space=pltpu.SMEM)],
        scratch_shapes=[pltpu.VMEM((MOE_BLK * TILE_ROWS, LANES), F32),
                        pltpu.SemaphoreType.DMA(()),
                        pltpu.SemaphoreType.DMA(())],
    )
    return pl.pallas_call(
        _dispatch_kernel,
        grid_spec=grid_spec,
        out_shape=[jax.ShapeDtypeStruct((n_slots * TILE_ROWS, LANES), F32),
                   jax.ShapeDtypeStruct(packed.shape, jnp.int32)],
        compiler_params=_cparams(("arbitrary",)),
        name="dispatch",
    )(packed, pstarts, counts, h2_tiles)


def _experts_kernel(be_ref, nused_ref, xs_ref, wg_ref, wu_ref, wd_ref, ys_ref, wg_scr, wu_scr, wd_scr):
    i = pl.program_id(0)
    nused = nused_ref[0]
    changed = (i == 0) | (be_ref[i] != be_ref[jnp.maximum(i - 1, 0)])

    @pl.when(changed & (i < nused))
    def _():
        wg_scr[...] = wg_ref[0].astype(BF16)
        wu_scr[...] = wu_ref[0].astype(BF16)
        wd_scr[...] = wd_ref[0].astype(BF16)

    @pl.when(i < nused)
    def _():
        x = _load_token_tiles(xs_ref, 0, MOE_BLK).astype(BF16)
        g = jnp.dot(x, wg_scr[...], preferred_element_type=F32)
        u = jnp.dot(x, wu_scr[...], preferred_element_type=F32)
        hmid = (g * jax.nn.sigmoid(g) * u).astype(BF16)
        _store_token_tiles(ys_ref, jnp.dot(hmid, wd_scr[...], preferred_element_type=F32))

    @pl.when(i >= nused)
    def _():
        ys_ref[...] = jnp.zeros_like(ys_ref)


def experts(xs_tiles, block_expert, nused, w_gate, w_up, w_down):
    d = D_MODEL
    nblk = block_expert.shape[0]
    blk_rows = MOE_BLK * TILE_ROWS
    wmap = lambda i, be, nu: (be[i], 0, 0)
    grid_spec = pltpu.PrefetchScalarGridSpec(
        num_scalar_prefetch=2,
        grid=(nblk,),
        in_specs=[pl.BlockSpec((blk_rows, LANES), lambda i, be, nu: (jnp.minimum(i, nu[0] - 1), 0)),
                  pl.BlockSpec((1, d, D_EXPERT), wmap),
                  pl.BlockSpec((1, d, D_EXPERT), wmap),
                  pl.BlockSpec((1, D_EXPERT, d), wmap)],
        out_specs=pl.BlockSpec((blk_rows, LANES), lambda i, be, nu: (i, 0)),
        scratch_shapes=[pltpu.VMEM((d, D_EXPERT), BF16),
                        pltpu.VMEM((d, D_EXPERT), BF16),
                        pltpu.VMEM((D_EXPERT, d), BF16)],
    )
    return pl.pallas_call(
        _experts_kernel,
        grid_spec=grid_spec,
        out_shape=jax.ShapeDtypeStruct((nblk * blk_rows, LANES), F32),
        compiler_params=_cparams(("arbitrary",)),
        name="experts",
    )(block_expert, nused, xs_tiles, w_gate, w_up, w_down)


def _combine_kernel(dest_ref, ys_hbm, x1_ref, meta_ref, mod_ref, gf_ref, o_ref, ybuf, sem):
    i = pl.program_id(0)
    nstep = pl.num_programs(0)
    tc = TC_COMB
    nt = dest_ref.shape[0] // 2
    half_rows = tc * TILE_ROWS
    buf_rows = 2 * half_rows

    def gather(step, slot):
        def issue(c, carry):
            t0 = c * DMA_CHUNK
            for u in range(DMA_CHUNK):
                for k in range(2):
                    d = dest_ref[k * nt + step * tc + t0 + u]
                    pltpu.make_async_copy(
                        ys_hbm.at[pl.ds(d * TILE_ROWS, TILE_ROWS)],
                        ybuf.at[pl.ds(slot * buf_rows + k * half_rows + (t0 + u) * TILE_ROWS, TILE_ROWS)],
                        sem.at[slot]).start()
            return carry

        lax.fori_loop(0, tc // DMA_CHUNK, issue, 0)

    @pl.when(i == 0)
    def _():
        gather(0, 0)

    @pl.when(i + 1 < nstep)
    def _():
        gather(i + 1, (i + 1) % 2)

    slot = i % 2
    start = pl.multiple_of(slot * buf_rows, buf_rows)
    pltpu.make_async_copy(ys_hbm.at[pl.ds(0, buf_rows)], ybuf.at[pl.ds(start, buf_rows)], sem.at[slot]).wait()
    meta = meta_ref[...]
    y0 = _load_token_tiles(ybuf, start, tc)
    y1 = _load_token_tiles(ybuf, start + half_rows, tc)
    moe = y0 * meta[:, 4:5] + y1 * meta[:, 5:6]
    x2 = x1_ref[...] + mod_ref[0][5:6] * moe
    o_ref[...] = _rms(x2, gf_ref[...])


def combine(dest_flat, ys, x1_flat, meta, mods, g_final, n_per_batch):
    nt, d = x1_flat.shape
    tc = TC_COMB
    per_b = n_per_batch // tc
    grid_spec = pltpu.PrefetchScalarGridSpec(
        num_scalar_prefetch=1,
        grid=(nt // tc,),
        in_specs=[pl.BlockSpec(memory_space=pl.ANY),
                  pl.BlockSpec((tc, d), lambda i, ds: (i, 0)),
                  pl.BlockSpec((tc, LANES), lambda i, ds: (i, 0)),
                  pl.BlockSpec((1, 8, d), lambda i, ds: (i // per_b, 0, 0)),
                  pl.BlockSpec((1, d), lambda i, ds: (0, 0))],
        out_specs=pl.BlockSpec((tc, d), lambda i, ds: (i, 0)),
        scratch_shapes=[pltpu.VMEM((2 * 2 * tc * TILE_ROWS, LANES), F32),
                        pltpu.SemaphoreType.DMA((2,))],
    )
    return pl.pallas_call(
        _combine_kernel,
        grid_spec=grid_spec,
        out_shape=jax.ShapeDtypeStruct((nt, d), F32),
        compiler_params=_cparams(("arbitrary",)),
        name="combine",
    )(dest_flat, ys, x1_flat, meta, mods, g_final.reshape(1, d))


def _dispatch_plan(metat, counts_row, nt):
    packed = (metat[0:2] + float(N_EXPERTS) * metat[2:4]).astype(jnp.int32).reshape(-1)
    counts = counts_row[N_GROUPS:N_GROUPS + N_EXPERTS].astype(jnp.int32)
    pcounts = ((counts + MOE_BLK - 1) // MOE_BLK) * MOE_BLK
    pends = jnp.cumsum(pcounts)
    pstarts = pends - pcounts
    nblk = (nt * 2) // MOE_BLK + N_EXPERTS
    first_slot = jnp.arange(nblk, dtype=jnp.int32) * MOE_BLK
    block_expert = jnp.minimum(
        jnp.sum((pends[None, :] <= first_slot[:, None]).astype(jnp.int32), axis=1), N_EXPERTS - 1)
    nused = (pends[-1] // MOE_BLK).astype(jnp.int32).reshape(1)
    return packed, pstarts.astype(jnp.int32), counts, block_expert.astype(jnp.int32), nused


def kernel(x, c, ctx, c_ctx, w_ada, b_ada, g_norm1, w_in, w_fmix, rpb, g_out, w_out, g_norm2,
           w_router_group, w_router_expert, w_gate, w_up, w_down, g_final):
    b, n, d = x.shape
    assert (b, n, d) == (c.shape[0], SEQ, D_MODEL) and w_ada.shape[0] == 1
    nt = b * n

    cond8 = jnp.zeros((8, d), F32).at[0:b].set(c).at[b].set(c_ctx)
    mod = adaln(cond8, w_ada[0], b_ada[0])
    mods = jnp.pad(mod[0:b].reshape(b, N_MOD, d), ((0, 0), (0, 2), (0, 0)))
    mod_ctx = jnp.pad(mod[b].reshape(N_MOD, d), ((0, 2), (0, 0)))

    w_in_b = w_in[0].astype(BF16)
    qr, qp, kr, v, a = in_proj(x, mods, g_norm1[0], w_in_b)
    kc, vc = ctx_proj(ctx, mod_ctx, g_norm1[0], w_in_b[:, D_FOURIER + D_NA:])

    fr = dft_cols(dft_rows(a))
    na = attention(qr, qp, kr, v, kc, vc, bias_tables(rpb[0]))

    w_router = jnp.concatenate(
        [w_router_group[0], w_router_expert[0],
         jnp.zeros((d, LANES - N_GROUPS - N_EXPERTS), F32)], axis=1).astype(BF16)
    x1, h2_tiles, meta, metat, cnt = out_proj(fr, na, x, mods, w_fmix[0].astype(BF16),
                                              w_out[0].astype(BF16), g_out[0], g_norm2[0], w_router)

    packed, pstarts, counts, block_expert, nused = _dispatch_plan(metat, cnt[0], nt)
    xs_tiles, dest = dispatch(h2_tiles, packed, pstarts, counts, block_expert.shape[0] * MOE_BLK)
    ys_tiles = experts(xs_tiles, block_expert, nused, w_gate[0], w_up[0], w_down[0])
    out = combine(dest, ys_tiles, x1.reshape(nt, d), meta, mods, g_final, n)
    return out.reshape(b, n, d)
```

```python
import functools
import math

import numpy as np
import jax
import jax.numpy as jnp
from jax import lax
from jax.experimental import pallas as pl
from jax.experimental.pallas import tpu as pltpu

F32 = jnp.float32
BF16 = jnp.bfloat16

D_MODEL = 1024
GRID_W = 64
GRID_H = 128
SEQ = GRID_W * GRID_H
CTX_LEN = 256
D_FOURIER = 256
FOURIER_GROUP = 64
HEAD_DIM = 64
N_HEADS = 12
D_NA = N_HEADS * HEAD_DIM
N_PAIRS = N_HEADS // 2
NA_ROWS = 8
NA_COLS = 16
ROPE_THETA = 10000.0
N_GROUPS = 4
EXPERTS_PER_GROUP = 8
N_EXPERTS = N_GROUPS * EXPERTS_PER_GROUP
D_EXPERT = 512
N_MOD = 6
D_IN_PROJ = D_FOURIER + 3 * D_NA
EPS = 1e-6
LANES = 128
NEG = -1e30
LOG2E = math.log2(math.e)

TM_PROJ = 512
ATT_ROWS = 16
DFT_TW = 8
DFT_TN = 4096
MOE_BLK = 256
TC_COMB = 256
VMEM_LIMIT = 56 * 1024 * 1024


def _cparams(sem):
    return pltpu.CompilerParams(dimension_semantics=sem, vmem_limit_bytes=VMEM_LIMIT)


def _mxu_const(table):
    return jnp.asarray(table, F32).astype(BF16)


@functools.lru_cache(maxsize=None)
def _rope_tables():
    t = np.arange(SEQ)
    row, col = t // GRID_W, t % GRID_W
    lane = np.arange(LANES)
    d = lane % HEAD_DIM
    chunk = d // 32
    e = d % 32
    j = e % 16
    inv = ROPE_THETA ** (-(j.astype(np.float64)) / 16.0)
    pos = np.where(chunk[None, :] == 0, row[:, None], col[:, None]).astype(np.float64)
    ang = pos * inv[None, :]
    cos = np.cos(ang)
    sin = np.sin(ang)
    first = (e < 16)[None, :]
    s_first = np.where(first, -sin, 0.0)
    s_second = np.where(first, 0.0, sin)
    return (cos.astype(np.float32), s_first.astype(np.float32), s_second.astype(np.float32))


@functools.lru_cache(maxsize=None)
def _chan_dft():
    c = np.arange(FOURIER_GROUP)
    ang = 2.0 * np.pi * ((c[:, None] * c[None, :]) % FOURIER_GROUP) / FOURIER_GROUP
    eye = np.eye(D_FOURIER // FOURIER_GROUP)
    re = np.kron(eye, np.cos(ang))
    im = np.kron(eye, -np.sin(ang))
    return np.concatenate([re, im], axis=1).astype(np.float32)


@functools.lru_cache(maxsize=None)
def _row_dft():
    k1 = np.arange(GRID_H)[:, None]
    r = np.arange(GRID_H)[None, :]
    out = np.zeros((GRID_W, 2 * GRID_H, 2 * GRID_H), np.float32)
    for w in range(GRID_W):
        m = (k1 * (GRID_W * r + w)) % SEQ
        ang = 2.0 * np.pi * m / SEQ
        c, s = np.cos(ang), np.sin(ang)
        out[w] = np.block([[c, s], [-s, c]])
    return out


@functools.lru_cache(maxsize=None)
def _col_dft():
    k2 = np.arange(GRID_W)
    ang = 2.0 * np.pi * ((k2[:, None] * k2[None, :]) % GRID_W) / GRID_W
    scale = 1.0 / math.sqrt(SEQ * FOURIER_GROUP)
    return (np.concatenate([np.cos(ang), np.sin(ang)], axis=1) * scale).astype(np.float32)


@functools.lru_cache(maxsize=None)
def _bias_index():
    c = np.arange(GRID_W)
    start = np.clip(c - NA_COLS // 2, 0, GRID_W - NA_COLS)
    valid = (c[None, :] >= start[:, None]) & (c[None, :] < start[:, None] + NA_COLS)
    dc = np.clip(c[None, :] - c[:, None] + (NA_COLS - 1), 0, 2 * NA_COLS - 2)
    return dc.astype(np.int32), valid


@functools.lru_cache(maxsize=None)
def _strict_lower(n):
    return np.tril(np.ones((n, n), np.float32), k=-1)


def _adaln_kernel(c_ref, w_ref, b_ref, o_ref):
    c = c_ref[...]
    s = c * jax.nn.sigmoid(c)
    o_ref[...] = jnp.dot(s, w_ref[...], precision=lax.Precision.HIGHEST,
                         preferred_element_type=F32) + b_ref[...]


def adaln(cond8, w, b):
    n = w.shape[1]
    tn = 1536
    return pl.pallas_call(
        _adaln_kernel,
        grid=(n // tn,),
        in_specs=[pl.BlockSpec((8, D_MODEL), lambda j: (0, 0)),
                  pl.BlockSpec((D_MODEL, tn), lambda j: (0, j)),
                  pl.BlockSpec((1, tn), lambda j: (0, j))],
        out_specs=pl.BlockSpec((8, tn), lambda j: (0, j)),
        out_shape=jax.ShapeDtypeStruct((8, n), F32),
        compiler_params=_cparams(("arbitrary",)),
        name="adaln",
    )(cond8, w, b.reshape(1, n))


def _norm_mod(x, g, shift, scale):
    ms = jnp.mean(x * x, axis=-1, keepdims=True)
    return (x * lax.rsqrt(ms + EPS) * g) * (1.0 + scale) + shift


def _in_proj_kernel(x_ref, mod_ref, g_ref, w_ref, cs_ref, cos_ref, s1_ref, s2_ref,
                    qr_ref, qp_ref, kr_ref, v_ref, a_ref, h_scr):
    m = mod_ref[0]
    h_scr[...] = _norm_mod(x_ref[0], g_ref[...], m[0:1], m[1:2]).astype(BF16)
    cos, s1, s2 = cos_ref[...], s1_ref[...], s2_ref[...]

    def rope(t):
        return (t * cos + pltpu.roll(t, LANES - 16, axis=1) * s1 + pltpu.roll(t, 16, axis=1) * s2)

    f = jnp.dot(h_scr[...], w_ref[:, 0:D_FOURIER], preferred_element_type=F32)
    a = jnp.dot(f.astype(BF16), cs_ref[...], preferred_element_type=F32)
    a_ref[0, 0] = a[:, :D_FOURIER].astype(BF16)
    a_ref[0, 1] = a[:, D_FOURIER:].astype(BF16)

    scale = HEAD_DIM ** -0.5 * LOG2E
    for c in range(D_NA // 256):
        lo = D_FOURIER + 256 * c
        q = jnp.dot(h_scr[...], w_ref[:, lo:lo + 256], preferred_element_type=F32)
        k = jnp.dot(h_scr[...], w_ref[:, lo + D_NA:lo + D_NA + 256], preferred_element_type=F32)
        v = jnp.dot(h_scr[...], w_ref[:, lo + 2 * D_NA:lo + 2 * D_NA + 256], preferred_element_type=F32)
        v_ref[0, :, 256 * c:256 * c + 256] = v.astype(BF16)
        for s in range(2):
            sl = slice(LANES * s, LANES * (s + 1))
            ol = slice(256 * c + LANES * s, 256 * c + LANES * (s + 1))
            qs, ks = q[:, sl], k[:, sl]
            qp_ref[0, :, ol] = (qs * scale).astype(BF16)
            qr_ref[0, :, ol] = (rope(qs) * scale).astype(BF16)
            kr_ref[0, :, ol] = rope(ks).astype(BF16)


def in_proj(x, mods, g1, w_in_bf16):
    b, n, d = x.shape
    tm = TM_PROJ
    cos, s1, s2 = _rope_tables()
    cs = _mxu_const(_chan_dft())
    tok = lambda bi, i: (bi, i, 0)
    const2 = lambda bi, i: (0, 0)
    tab = pl.BlockSpec((tm, LANES), lambda bi, i: (i, 0))
    qkv_shape = jax.ShapeDtypeStruct((b, n, D_NA), BF16)
    qkv_spec = pl.BlockSpec((1, tm, D_NA), tok)
    return pl.pallas_call(
        _in_proj_kernel,
        grid=(b, n // tm),
        in_specs=[pl.BlockSpec((1, tm, d), tok),
                  pl.BlockSpec((1, 8, d), lambda bi, i: (bi, 0, 0)),
                  pl.BlockSpec((1, d), const2),
                  pl.BlockSpec((d, D_IN_PROJ), const2),
                  pl.BlockSpec((D_FOURIER, 2 * D_FOURIER), const2),
                  tab, tab, tab],
        out_specs=[qkv_spec, qkv_spec, qkv_spec, qkv_spec,
                   pl.BlockSpec((1, 2, tm, D_FOURIER), lambda bi, i: (bi, 0, i, 0))],
        out_shape=[qkv_shape, qkv_shape, qkv_shape, qkv_shape,
                   jax.ShapeDtypeStruct((b, 2, n, D_FOURIER), BF16)],
        scratch_shapes=[pltpu.VMEM((tm, d), BF16)],
        compiler_params=_cparams(("arbitrary", "arbitrary")),
        name="in_proj",
    )(x, mods, g1.reshape(1, d), w_in_bf16, cs, jnp.asarray(cos), jnp.asarray(s1), jnp.asarray(s2))


def _ctx_proj_kernel(x_ref, mod_ref, g_ref, w_ref, k_ref, v_ref):
    m = mod_ref[...]
    h = _norm_mod(x_ref[0], g_ref[...], m[0:1], m[1:2]).astype(BF16)
    k_ref[0] = jnp.dot(h, w_ref[:, 0:D_NA], preferred_element_type=F32).astype(BF16)
    v_ref[0] = jnp.dot(h, w_ref[:, D_NA:2 * D_NA], preferred_element_type=F32).astype(BF16)


def ctx_proj(ctx, mod_ctx, g1, w_kv_bf16):
    b, l, d = ctx.shape
    shape = jax.ShapeDtypeStruct((b, l, D_NA), BF16)
    spec = pl.BlockSpec((1, l, D_NA), lambda bi: (bi, 0, 0))
    return pl.pallas_call(
        _ctx_proj_kernel,
        grid=(b,),
        in_specs=[pl.BlockSpec((1, l, d), lambda bi: (bi, 0, 0)),
                  pl.BlockSpec((8, d), lambda bi: (0, 0)),
                  pl.BlockSpec((1, d), lambda bi: (0, 0)),
                  pl.BlockSpec((d, 2 * D_NA), lambda bi: (0, 0))],
        out_specs=[spec, spec],
        out_shape=[shape, shape],
        compiler_params=_cparams(("arbitrary",)),
        name="ctx_proj",
    )(ctx, mod_ctx, g1.reshape(1, d), w_kv_bf16)


def _dft_rows_kernel(a_ref, g_ref, z_ref):
    for j in range(DFT_TW):
        sl = slice(D_FOURIER * j, D_FOURIER * (j + 1))
        rhs = jnp.concatenate([a_ref[0, 0, :, sl], a_ref[0, 1, :, sl]], axis=0)
        z = jnp.dot(g_ref[j], rhs, preferred_element_type=F32)
        z_ref[0, 0, j] = z[:GRID_H].astype(BF16)
        z_ref[0, 1, j] = z[GRID_H:].astype(BF16)


def dft_rows(a):
    b = a.shape[0]
    a4 = a.reshape(b, 2, GRID_H, GRID_W * D_FOURIER)
    g = _mxu_const(_row_dft())
    return pl.pallas_call(
        _dft_rows_kernel,
        grid=(GRID_W // DFT_TW, b),
        in_specs=[pl.BlockSpec((1, 2, GRID_H, DFT_TW * D_FOURIER), lambda j, bi: (bi, 0, 0, j)),
                  pl.BlockSpec((DFT_TW, 2 * GRID_H, 2 * GRID_H), lambda j, bi: (j, 0, 0))],
        out_specs=pl.BlockSpec((1, 2, DFT_TW, GRID_H, D_FOURIER), lambda j, bi: (bi, 0, j, 0, 0)),
        out_shape=jax.ShapeDtypeStruct((b, 2, GRID_W, GRID_H, D_FOURIER), BF16),
        compiler_params=_cparams(("arbitrary", "arbitrary")),
        name="dft_rows",
    )(a4, g)


def _dft_cols_kernel(z_ref, cs_ref, o_ref):
    rhs = jnp.concatenate([z_ref[0, 0], z_ref[0, 1]], axis=0)
    o_ref[0] = jnp.dot(cs_ref[...], rhs, preferred_element_type=F32).astype(BF16)


def dft_cols(z):
    b = z.shape[0]
    ncol = GRID_H * D_FOURIER
    z4 = z.reshape(b, 2, GRID_W, ncol)
    cs = _mxu_const(_col_dft())
    out = pl.pallas_call(
        _dft_cols_kernel,
        grid=(b, ncol // DFT_TN),
        in_specs=[pl.BlockSpec((1, 2, GRID_W, DFT_TN), lambda bi, j: (bi, 0, 0, j)),
                  pl.BlockSpec((GRID_W, 2 * GRID_W), lambda bi, j: (0, 0))],
        out_specs=pl.BlockSpec((1, GRID_W, DFT_TN), lambda bi, j: (bi, 0, j)),
        out_shape=jax.ShapeDtypeStruct((b, GRID_W, ncol), BF16),
        compiler_params=_cparams(("arbitrary", "arbitrary")),
        name="dft_cols",
    )(z4, cs)
    return out.reshape(b, SEQ, D_FOURIER)


WIN_ROWS = NA_ROWS + 2
WIN_KEYS = WIN_ROWS * GRID_W
WIN_TILES = WIN_KEYS // LANES
N_DR = 2 * NA_ROWS + 1
SOFT_ROWS = 32
ATT_PAIRS = ATT_ROWS // 2


def _attention_kernel(qr_ref, qp_ref, k_ref, v_ref, kc_ref, vc_ref, tb_ref, o_ref,
                      sc_scr, pc_scr, s_scr, p_scr, o_scr):
    rb = pl.program_id(2)
    nq = ATT_ROWS * GRID_W
    lane = lax.broadcasted_iota(jnp.int32, (nq, LANES), 1)
    first = lane < HEAD_DIM
    first2 = lax.broadcasted_iota(jnp.int32, (2 * GRID_W, LANES), 1) < HEAD_DIM
    nt = (((1,), (1,)), ((), ()))

    qp = qp_ref[0]
    zero = jnp.zeros_like(qp)
    sc_scr[0] = lax.dot_general(jnp.where(first, qp, zero), kc_ref[0], nt, preferred_element_type=F32)
    sc_scr[1] = lax.dot_general(jnp.where(first, zero, qp), kc_ref[0], nt, preferred_element_type=F32)

    def window(jp):
        r0 = rb * ATT_ROWS + 2 * jp
        ws = jnp.clip(r0 - NA_ROWS // 2, 0, GRID_H - WIN_ROWS)
        return r0, ws, pl.multiple_of(ws * GRID_W, GRID_W)

    def scores(jp):
        _, _, koff = window(jp)
        q2 = qr_ref[0, jp * 2 * GRID_W:(jp + 1) * 2 * GRID_W, :]
        z2 = jnp.zeros_like(q2)
        qm = jnp.concatenate([jnp.where(first2, q2, z2), jnp.where(first2, z2, q2)], axis=0)
        kw = k_ref[0, pl.ds(koff, WIN_KEYS), :]
        s_scr[jp] = lax.dot_general(qm, kw, nt, preferred_element_type=F32)

    def softmax(jp):
        r0, ws, _ = window(jp)
        u, qoff = jp, jp * 2 * GRID_W
        for c in range(4):
            hh, i = c // 2, c % 2
            r = r0 + i
            off = jnp.clip(r - NA_ROWS // 2, 0, GRID_H - NA_ROWS) - ws
            sel = []
            for t in range(WIN_TILES):
                v0 = (2 * t >= off) & (2 * t < off + NA_ROWS)
                v1 = (2 * t + 1 >= off) & (2 * t + 1 < off + NA_ROWS)
                kind = jnp.where(v0 & v1, 0, jnp.where(v0, 2, 1))
                sel.append((kind, jnp.where(v0 | v1, ws + 2 * t - r + NA_ROWS, N_DR - 1)))
            for h in range(GRID_W // SOFT_ROWS):
                lo = h * SOFT_ROWS
                rows = slice(c * GRID_W + lo, c * GRID_W + lo + SOFT_ROWS)
                qrow = slice(qoff + i * GRID_W + lo, qoff + i * GRID_W + lo + SOFT_ROWS)
                tiles = [s_scr[u, rows, t * LANES:(t + 1) * LANES]
                         + tb_ref[0, hh, sel[t][0], sel[t][1], lo:lo + SOFT_ROWS, :] for t in range(WIN_TILES)]
                sc = sc_scr[hh, qrow, :]
                mt = jnp.maximum(sc[:, :LANES], sc[:, LANES:])
                for tl in tiles:
                    mt = jnp.maximum(mt, tl)
                m = jnp.max(mt, axis=1, keepdims=True)
                pc_scr[hh, qrow, :] = jnp.exp2(sc - m).astype(BF16)
                for t, tl in enumerate(tiles):
                    p_scr[u, rows, t * LANES:(t + 1) * LANES] = jnp.exp2(tl - m).astype(BF16)


    def weighted_values(jp):
        _, _, koff = window(jp)
        vw = jnp.concatenate([v_ref[0, pl.ds(koff, WIN_KEYS), :], jnp.ones((WIN_KEYS, LANES), BF16)], axis=1)
        o = jnp.dot(p_scr[jp], vw, preferred_element_type=F32)
        qs = slice(jp * 2 * GRID_W, (jp + 1) * 2 * GRID_W)
        o_scr[0, qs, :] = o[:2 * GRID_W]
        o_scr[1, qs, :] = o[2 * GRID_W:]

    npair = ATT_PAIRS
    scores(0)
    scores(1)
    softmax(0)
    for jp in range(npair):
        if jp + 2 < npair:
            scores(jp + 2)
        if jp + 1 < npair:
            softmax(jp + 1)
        weighted_values(jp)

    vc = jnp.concatenate([vc_ref[0], jnp.ones((CTX_LEN, LANES), BF16)], axis=1)
    oa = o_scr[0] + jnp.dot(pc_scr[0], vc, preferred_element_type=F32)
    ob = o_scr[1] + jnp.dot(pc_scr[1], vc, preferred_element_type=F32)
    o_ref[0] = jnp.where(first, oa[:, :LANES] / oa[:, LANES:], ob[:, :LANES] / ob[:, LANES:]).astype(BF16)


def bias_tables(rpb):
    dc, valid = _bias_index()
    n_dc = 2 * NA_COLS - 1
    onehot = (dc.reshape(1, -1) == np.arange(n_dc).reshape(-1, 1)).astype(np.float32)
    t = jnp.dot(rpb.reshape(-1, n_dc), jnp.asarray(onehot), precision=lax.Precision.HIGHEST)
    t = jnp.where(valid[None, None], LOG2E * t.reshape(N_HEADS, 2 * NA_ROWS - 1, GRID_W, GRID_W), NEG)
    t = jnp.pad(t, ((0, 0), (1, 2), (0, 0), (0, 0)), constant_values=NEG)
    left, right = t[:, :N_DR], t[:, 1:N_DR + 1]
    neg = jnp.full_like(left, NEG)
    tb = jnp.stack([jnp.concatenate([left, right], axis=-1),
                    jnp.concatenate([neg, right], axis=-1),
                    jnp.concatenate([left, neg], axis=-1)], axis=1)
    return tb.reshape(N_PAIRS, 2, 3, N_DR, GRID_W, LANES).astype(F32)


def attention(qr, qp, kr, v, kc, vc, tb):
    b, n, _ = qr.shape
    tq = ATT_ROWS * GRID_W
    qspec = pl.BlockSpec((1, tq, LANES), lambda bi, hp, i: (bi, i, hp))
    kspec = pl.BlockSpec((1, n, LANES), lambda bi, hp, i: (bi, 0, hp))
    cspec = pl.BlockSpec((1, CTX_LEN, LANES), lambda bi, hp, i: (bi, 0, hp))
    return pl.pallas_call(
        _attention_kernel,
        grid=(b, N_PAIRS, GRID_H // ATT_ROWS),
        in_specs=[qspec, qspec, kspec, kspec, cspec, cspec,
                  pl.BlockSpec((1, 2, 3, N_DR, GRID_W, LANES), lambda bi, hp, i: (hp, 0, 0, 0, 0, 0))],
        out_specs=qspec,
        out_shape=jax.ShapeDtypeStruct((b, n, D_NA), BF16),
        scratch_shapes=[pltpu.VMEM((2, tq, CTX_LEN), F32),
                        pltpu.VMEM((2, tq, CTX_LEN), BF16),
                        pltpu.VMEM((ATT_PAIRS, 4 * GRID_W, WIN_KEYS), F32),
                        pltpu.VMEM((ATT_PAIRS, 4 * GRID_W, WIN_KEYS), BF16),
                        pltpu.VMEM((2, tq, 2 * LANES), F32)],
        compiler_params=_cparams(("arbitrary", "arbitrary", "arbitrary")),
        name="attention",
    )(qr, qp, kr, v, kc, vc, tb)


def _rms(x, g):
    ms = jnp.mean(x * x, axis=-1, keepdims=True)
    return x * lax.rsqrt(ms + EPS) * g


TILE_ROWS = D_MODEL // LANES


def _store_token_tiles(ref, val):
    rows = val.shape[0]
    for j in range(TILE_ROWS):
        ref[pl.ds(j, rows, stride=TILE_ROWS), :] = val[:, LANES * j:LANES * (j + 1)]


def _load_token_tiles(ref, start, rows):
    return jnp.concatenate(
        [ref[pl.ds(start + j, rows, stride=TILE_ROWS), :] for j in range(TILE_ROWS)], axis=1)


def _out_proj_kernel(fr_ref, na_ref, x_ref, mod_ref, wf_ref, wo_ref, go_ref, g2_ref, wr_ref, tri_ref,
                     x1_ref, h2_ref, meta_ref, metat_ref, cnt_ref, run_scr):
    @pl.when((pl.program_id(0) == 0) & (pl.program_id(1) == 0))
    def _():
        run_scr[...] = jnp.zeros_like(run_scr)

    m = mod_ref[0]
    go = go_ref[...]
    fo = jnp.dot(fr_ref[0], wf_ref[...], preferred_element_type=F32)
    fn = _rms(fo, go[:, :D_FOURIER]).astype(BF16)
    nn = _rms(na_ref[0].astype(F32), go[:, D_FOURIER:]).astype(BF16)
    y = (jnp.dot(fn, wo_ref[0:D_FOURIER, :], preferred_element_type=F32)
         + jnp.dot(nn, wo_ref[D_FOURIER:, :], preferred_element_type=F32))
    x1 = x_ref[0] + m[2:3] * y
    x1_ref[0] = x1
    h2 = _norm_mod(x1, g2_ref[...], m[3:4], m[4:5])
    _store_token_tiles(h2_ref, h2)
    logits = jnp.dot(h2.astype(BF16), wr_ref[...], preferred_element_type=F32)

    tm = logits.shape[0]
    lane = lax.broadcasted_iota(jnp.int32, (tm, LANES), 1).astype(F32)
    ninf = jnp.float32(-jnp.inf)

    def argmax_first(vals):
        mx = jnp.max(vals, axis=1, keepdims=True)
        idx = jnp.min(jnp.where(vals == mx, lane, float(LANES)), axis=1, keepdims=True)
        return mx, idx

    lg = jnp.where(lane < N_GROUPS, logits, ninf)
    gmax, gidx = argmax_first(lg)
    pg = 1.0 / jnp.sum(jnp.exp(lg - gmax), axis=1, keepdims=True)
    lo = N_GROUPS + EXPERTS_PER_GROUP * gidx
    le = jnp.where((lane >= lo) & (lane < lo + EXPERTS_PER_GROUP), logits, ninf)
    e1, i1 = argmax_first(le)
    e2, i2 = argmax_first(jnp.where(lane == i1, ninf, le))
    dd = jnp.exp(e2 - e1)
    gate1 = pg / (1.0 + dd)
    gate2 = pg * dd / (1.0 + dd)

    hot1 = lane == i1
    hot2 = lane == i2
    onehot = jnp.where(hot1 | hot2, 1.0, 0.0)
    cnt = jnp.dot(tri_ref[...], onehot.astype(BF16), preferred_element_type=F32) + run_scr[...]
    rank1 = jnp.sum(jnp.where(hot1, cnt, 0.0), axis=1, keepdims=True)
    rank2 = jnp.sum(jnp.where(hot2, cnt, 0.0), axis=1, keepdims=True)
    run_scr[...] = run_scr[...] + jnp.sum(onehot, axis=0, keepdims=True)

    meta = jnp.where(lane == 0, i1 - N_GROUPS,
           jnp.where(lane == 1, i2 - N_GROUPS,
           jnp.where(lane == 2, rank1,
           jnp.where(lane == 3, rank2,
           jnp.where(lane == 4, gate1,
           jnp.where(lane == 5, gate2, 0.0))))))
    meta_ref[...] = meta
    metat_ref[...] = jnp.transpose(meta)[0:8, :]
    cnt_ref[...] = jnp.broadcast_to(run_scr[...], cnt_ref.shape)


def out_proj(fr, na, x, mods, w_fmix_bf16, w_out_bf16, g_out, g2, w_router_bf16):
    b, n, d = x.shape
    tm = TM_PROJ
    steps = n // tm
    tok = lambda bi, i: (bi, i, 0)
    const2 = lambda bi, i: (0, 0)
    flat = lambda bi, i: (bi * steps + i, 0)
    tri = _mxu_const(_strict_lower(tm))
    return pl.pallas_call(
        _out_proj_kernel,
        grid=(b, steps),
        in_specs=[pl.BlockSpec((1, tm, D_FOURIER), tok),
                  pl.BlockSpec((1, tm, D_NA), tok),
                  pl.BlockSpec((1, tm, d), tok),
                  pl.BlockSpec((1, 8, d), lambda bi, i: (bi, 0, 0)),
                  pl.BlockSpec((D_FOURIER, D_FOURIER), const2),
                  pl.BlockSpec((d, d), const2),
                  pl.BlockSpec((1, d), const2),
                  pl.BlockSpec((1, d), const2),
                  pl.BlockSpec((d, LANES), const2),
                  pl.BlockSpec((tm, tm), const2)],
        out_specs=[pl.BlockSpec((1, tm, d), tok),
                   pl.BlockSpec((tm * TILE_ROWS, LANES), flat),
                   pl.BlockSpec((tm, LANES), flat),
                   pl.BlockSpec((8, tm), lambda bi, i: (0, bi * steps + i)),
                   pl.BlockSpec((8, LANES), const2)],
        out_shape=[jax.ShapeDtypeStruct((b, n, d), F32),
                   jax.ShapeDtypeStruct((b * n * TILE_ROWS, LANES), F32),
                   jax.ShapeDtypeStruct((b * n, LANES), F32),
                   jax.ShapeDtypeStruct((8, b * n), F32),
                   jax.ShapeDtypeStruct((8, LANES), F32)],
        scratch_shapes=[pltpu.VMEM((1, LANES), F32)],
        compiler_params=_cparams(("arbitrary", "arbitrary")),
        name="out_proj",
    )(fr, na, x, mods, w_fmix_bf16, w_out_bf16, g_out.reshape(1, d), g2.reshape(1, d), w_router_bf16, tri)


DMA_CHUNK = 8


TD_DISP = 512


def _dispatch_kernel(dest_ref, pstart_ref, count_ref, h2_ref, xs_hbm, zero_scr, sem, pad_sem):
    i = pl.program_id(0)
    nt = dest_ref.shape[0] // 2

    @pl.when(i == 0)
    def _():
        zero_scr[...] = jnp.zeros_like(zero_scr)

        def per_expert(e, npad):
            lo = pstart_ref[e] + count_ref[e]
            hi = pstart_ref[e] + ((count_ref[e] + MOE_BLK - 1) // MOE_BLK) * MOE_BLK

            def fill(s, carry):
                pltpu.make_async_copy(zero_scr.at[pl.ds(0, TILE_ROWS)],
                                      xs_hbm.at[pl.ds(s * TILE_ROWS, TILE_ROWS)], pad_sem).start()
                return carry

            lax.fori_loop(lo, hi, fill, 0)
            return npad + (hi - lo)

        npad = lax.fori_loop(0, N_EXPERTS, per_expert, 0)

        blk_rows = MOE_BLK * TILE_ROWS
        first_free = (pstart_ref[N_EXPERTS - 1] + count_ref[N_EXPERTS - 1] + MOE_BLK - 1) // MOE_BLK
        n_blocks = xs_hbm.shape[0] // blk_rows

        def fill_block(bk, carry):
            pltpu.make_async_copy(zero_scr, xs_hbm.at[pl.ds(bk * blk_rows, blk_rows)], pad_sem).start()
            return carry

        lax.fori_loop(first_free, n_blocks, fill_block, 0)
        rows = npad * TILE_ROWS + (n_blocks - first_free) * blk_rows

        @pl.when(rows > 0)
        def _():
            pltpu.make_async_copy(xs_hbm.at[pl.ds(0, rows)], xs_hbm.at[pl.ds(0, rows)], pad_sem).wait()

    def issue(c, carry):
        t0 = c * DMA_CHUNK
        for u in range(DMA_CHUNK):
            for k in range(2):
                d = dest_ref[k * nt + i * TD_DISP + t0 + u]
                pltpu.make_async_copy(h2_ref.at[pl.ds((t0 + u) * TILE_ROWS, TILE_ROWS)],
                                      xs_hbm.at[pl.ds(d * TILE_ROWS, TILE_ROWS)], sem).start(priority=k)
        return carry

    lax.fori_loop(0, TD_DISP // DMA_CHUNK, issue, 0)
    rows = 2 * TD_DISP * TILE_ROWS
    pltpu.make_async_copy(xs_hbm.at[pl.ds(0, rows)], xs_hbm.at[pl.ds(0, rows)], sem).wait()


def dispatch(h2_tiles, dest, pstarts, counts, n_slots):
    nt = dest.shape[0] // 2
    grid_spec = pltpu.PrefetchScalarGridSpec(
        num_scalar_prefetch=3,
        grid=(nt // TD_DISP,),
        in_specs=[pl.BlockSpec((TD_DISP * TILE_ROWS, LANES), lambda i, ds, ps, ct: (i, 0))],
        out_specs=pl.BlockSpec(memory_space=pl.ANY),
        scratch_shapes=[pltpu.VMEM((MOE_BLK * TILE_ROWS, LANES), F32),
                        pltpu.SemaphoreType.DMA(()),
                        pltpu.SemaphoreType.DMA(())],
    )
    return pl.pallas_call(
        _dispatch_kernel,
        grid_spec=grid_spec,
        out_shape=jax.ShapeDtypeStruct((n_slots * TILE_ROWS, LANES), F32),
        compiler_params=_cparams(("arbitrary",)),
        name="dispatch",
    )(dest, pstarts, counts, h2_tiles)


def _experts_kernel(be_ref, nused_ref, xs_ref, wg_ref, wu_ref, wd_ref, ys_ref, wg_scr, wu_scr, wd_scr):
    i = pl.program_id(0)
    nused = nused_ref[0]
    changed = (i == 0) | (be_ref[i] != be_ref[jnp.maximum(i - 1, 0)])

    @pl.when(changed & (i < nused))
    def _():
        wg_scr[...] = wg_ref[0].astype(BF16)
        wu_scr[...] = wu_ref[0].astype(BF16)
        wd_scr[...] = wd_ref[0].astype(BF16)

    @pl.when(i < nused)
    def _():
        x = _load_token_tiles(xs_ref, 0, MOE_BLK).astype(BF16)
        g = jnp.dot(x, wg_scr[...], preferred_element_type=F32)
        u = jnp.dot(x, wu_scr[...], preferred_element_type=F32)
        hmid = (g * jax.nn.sigmoid(g) * u).astype(BF16)
        _store_token_tiles(ys_ref, jnp.dot(hmid, wd_scr[...], preferred_element_type=F32))

    @pl.when(i >= nused)
    def _():
        ys_ref[...] = jnp.zeros_like(ys_ref)


def experts(xs_tiles, block_expert, nused, w_gate, w_up, w_down):
    d = D_MODEL
    nblk = block_expert.shape[0]
    blk_rows = MOE_BLK * TILE_ROWS
    wmap = lambda i, be, nu: (be[i], 0, 0)
    grid_spec = pltpu.PrefetchScalarGridSpec(
        num_scalar_prefetch=2,
        grid=(nblk,),
        in_specs=[pl.BlockSpec((blk_rows, LANES), lambda i, be, nu: (jnp.minimum(i, nu[0] - 1), 0)),
                  pl.BlockSpec((1, d, D_EXPERT), wmap),
                  pl.BlockSpec((1, d, D_EXPERT), wmap),
                  pl.BlockSpec((1, D_EXPERT, d), wmap)],
        out_specs=pl.BlockSpec((blk_rows, LANES), lambda i, be, nu: (i, 0)),
        scratch_shapes=[pltpu.VMEM((d, D_EXPERT), BF16),
                        pltpu.VMEM((d, D_EXPERT), BF16),
                        pltpu.VMEM((D_EXPERT, d), BF16)],
    )
    return pl.pallas_call(
        _experts_kernel,
        grid_spec=grid_spec,
        out_shape=jax.ShapeDtypeStruct((nblk * blk_rows, LANES), F32),
        compiler_params=_cparams(("arbitrary",)),
        name="experts",
    )(block_expert, nused, xs_tiles, w_gate, w_up, w_down)


def _combine_kernel(dest_ref, ys_hbm, x1_ref, meta_ref, mod_ref, gf_ref, o_ref, ybuf, sem):
    i = pl.program_id(0)
    nstep = pl.num_programs(0)
    tc = TC_COMB
    nt = dest_ref.shape[0] // 2
    half_rows = tc * TILE_ROWS
    buf_rows = 2 * half_rows

    def gather(step, slot):
        def issue(c, carry):
            t0 = c * DMA_CHUNK
            for u in range(DMA_CHUNK):
                for k in range(2):
                    d = dest_ref[k * nt + step * tc + t0 + u]
                    pltpu.make_async_copy(
                        ys_hbm.at[pl.ds(d * TILE_ROWS, TILE_ROWS)],
                        ybuf.at[pl.ds(slot * buf_rows + k * half_rows + (t0 + u) * TILE_ROWS, TILE_ROWS)],
                        sem.at[slot]).start(priority=k)
            return carry

        lax.fori_loop(0, tc // DMA_CHUNK, issue, 0)

    @pl.when(i == 0)
    def _():
        gather(0, 0)

    @pl.when(i + 1 < nstep)
    def _():
        gather(i + 1, (i + 1) % 2)

    slot = i % 2
    start = pl.multiple_of(slot * buf_rows, buf_rows)
    pltpu.make_async_copy(ys_hbm.at[pl.ds(0, buf_rows)], ybuf.at[pl.ds(start, buf_rows)], sem.at[slot]).wait()
    meta = meta_ref[...]
    y0 = _load_token_tiles(ybuf, start, tc)
    y1 = _load_token_tiles(ybuf, start + half_rows, tc)
    moe = y0 * meta[:, 4:5] + y1 * meta[:, 5:6]
    x2 = x1_ref[...] + mod_ref[0][5:6] * moe
    o_ref[...] = _rms(x2, gf_ref[...])


def combine(dest_flat, ys, x1_flat, meta, mods, g_final, n_per_batch):
    nt, d = x1_flat.shape
    tc = TC_COMB
    per_b = n_per_batch // tc
    grid_spec = pltpu.PrefetchScalarGridSpec(
        num_scalar_prefetch=1,
        grid=(nt // tc,),
        in_specs=[pl.BlockSpec(memory_space=pl.ANY),
                  pl.BlockSpec((tc, d), lambda i, ds: (i, 0)),
                  pl.BlockSpec((tc, LANES), lambda i, ds: (i, 0)),
                  pl.BlockSpec((1, 8, d), lambda i, ds: (i // per_b, 0, 0)),
                  pl.BlockSpec((1, d), lambda i, ds: (0, 0))],
        out_specs=pl.BlockSpec((tc, d), lambda i, ds: (i, 0)),
        scratch_shapes=[pltpu.VMEM((2 * 2 * tc * TILE_ROWS, LANES), F32),
                        pltpu.SemaphoreType.DMA((2,))],
    )
    return pl.pallas_call(
        _combine_kernel,
        grid_spec=grid_spec,
        out_shape=jax.ShapeDtypeStruct((nt, d), F32),
        compiler_params=_cparams(("arbitrary",)),
        name="combine",
    )(dest_flat, ys, x1_flat, meta, mods, g_final.reshape(1, d))


def _dispatch_plan(metat, counts_row, nt):
    counts = counts_row[N_GROUPS:N_GROUPS + N_EXPERTS].astype(jnp.int32)
    pcounts = ((counts + MOE_BLK - 1) // MOE_BLK) * MOE_BLK
    pends = jnp.cumsum(pcounts)
    pstarts = pends - pcounts
    eid = metat[0:2].astype(jnp.int32)
    dest = metat[2:4].astype(jnp.int32)
    for e in range(N_EXPERTS):
        dest = dest + jnp.where(eid == e, pstarts[e], 0)
    dest = dest.reshape(-1)
    nblk = (nt * 2) // MOE_BLK + N_EXPERTS
    first_slot = jnp.arange(nblk, dtype=jnp.int32) * MOE_BLK
    block_expert = jnp.minimum(
        jnp.sum((pends[None, :] <= first_slot[:, None]).astype(jnp.int32), axis=1), N_EXPERTS - 1)
    nused = (pends[-1] // MOE_BLK).astype(jnp.int32).reshape(1)
    return dest, pstarts.astype(jnp.int32), counts, block_expert.astype(jnp.int32), nused


def kernel(x, c, ctx, c_ctx, w_ada, b_ada, g_norm1, w_in, w_fmix, rpb, g_out, w_out, g_norm2,
           w_router_group, w_router_expert, w_gate, w_up, w_down, g_final):
    b, n, d = x.shape
    assert (b, n, d) == (c.shape[0], SEQ, D_MODEL) and w_ada.shape[0] == 1
    nt = b * n

    cond8 = jnp.zeros((8, d), F32).at[0:b].set(c).at[b].set(c_ctx)
    mod = adaln(cond8, w_ada[0], b_ada[0])
    mods = jnp.pad(mod[0:b].reshape(b, N_MOD, d), ((0, 0), (0, 2), (0, 0)))
    mod_ctx = jnp.pad(mod[b].reshape(N_MOD, d), ((0, 2), (0, 0)))

    w_in_b = w_in[0].astype(BF16)
    qr, qp, kr, v, a = in_proj(x, mods, g_norm1[0], w_in_b)
    kc, vc = ctx_proj(ctx, mod_ctx, g_norm1[0], w_in_b[:, D_FOURIER + D_NA:])

    fr = dft_cols(dft_rows(a))
    na = attention(qr, qp, kr, v, kc, vc, bias_tables(rpb[0]))

    w_router = jnp.concatenate(
        [w_router_group[0], w_router_expert[0],
         jnp.zeros((d, LANES - N_GROUPS - N_EXPERTS), F32)], axis=1).astype(BF16)
    x1, h2_tiles, meta, metat, cnt = out_proj(fr, na, x, mods, w_fmix[0].astype(BF16),
                                              w_out[0].astype(BF16), g_out[0], g_norm2[0], w_router)

    dest, pstarts, counts, block_expert, nused = _dispatch_plan(metat, cnt[0], nt)
    xs_tiles = dispatch(h2_tiles, dest, pstarts, counts, block_expert.shape[0] * MOE_BLK)
    ys_tiles = experts(xs_tiles, block_expert, nused, w_gate[0], w_up[0], w_down[0])
    out = combine(dest, ys_tiles, x1.reshape(nt, d), meta, mods, g_final, n)
    return out.reshape(b, n, d)
```

```python
import functools
import math

import numpy as np
import jax
import jax.numpy as jnp
from jax import lax
from jax.experimental import pallas as pl
from jax.experimental.pallas import tpu as pltpu

F32 = jnp.float32
BF16 = jnp.bfloat16

D_MODEL = 1024
GRID_W = 64
GRID_H = 128
SEQ = GRID_W * GRID_H
CTX_LEN = 256
D_FOURIER = 256
FOURIER_GROUP = 64
HEAD_DIM = 64
N_HEADS = 12
D_NA = N_HEADS * HEAD_DIM
N_PAIRS = N_HEADS // 2
NA_ROWS = 8
NA_COLS = 16
ROPE_THETA = 10000.0
N_GROUPS = 4
EXPERTS_PER_GROUP = 8
N_EXPERTS = N_GROUPS * EXPERTS_PER_GROUP
D_EXPERT = 512
N_MOD = 6
D_IN_PROJ = D_FOURIER + 3 * D_NA
EPS = 1e-6
LANES = 128
NEG = -1e30
LOG2E = math.log2(math.e)

TM_PROJ = 512
ATT_ROWS = 16
DFT_TW = 8
DFT_TN = 4096
MOE_BLK = 512
MOE_CHUNK = 256
TC_COMB = 256
VMEM_LIMIT = 56 * 1024 * 1024


def _cparams(sem):
    return pltpu.CompilerParams(dimension_semantics=sem, vmem_limit_bytes=VMEM_LIMIT)


def _mxu_const(table):
    return jnp.asarray(table, F32).astype(BF16)


@functools.lru_cache(maxsize=None)
def _rope_tables():
    t = np.arange(SEQ)
    row, col = t // GRID_W, t % GRID_W
    lane = np.arange(LANES)
    d = lane % HEAD_DIM
    chunk = d // 32
    e = d % 32
    j = e % 16
    inv = ROPE_THETA ** (-(j.astype(np.float64)) / 16.0)
    pos = np.where(chunk[None, :] == 0, row[:, None], col[:, None]).astype(np.float64)
    ang = pos * inv[None, :]
    cos = np.cos(ang)
    sin = np.sin(ang)
    first = (e < 16)[None, :]
    s_first = np.where(first, -sin, 0.0)
    s_second = np.where(first, 0.0, sin)
    return (cos.astype(np.float32), s_first.astype(np.float32), s_second.astype(np.float32))


@functools.lru_cache(maxsize=None)
def _chan_dft():
    c = np.arange(FOURIER_GROUP)
    ang = 2.0 * np.pi * ((c[:, None] * c[None, :]) % FOURIER_GROUP) / FOURIER_GROUP
    eye = np.eye(D_FOURIER // FOURIER_GROUP)
    re = np.kron(eye, np.cos(ang))
    im = np.kron(eye, -np.sin(ang))
    return np.concatenate([re, im], axis=1).astype(np.float32)


@functools.lru_cache(maxsize=None)
def _row_dft():
    k1 = np.arange(GRID_H)[:, None]
    r = np.arange(GRID_H)[None, :]
    out = np.zeros((GRID_W, 2 * GRID_H, 2 * GRID_H), np.float32)
    for w in range(GRID_W):
        m = (k1 * (GRID_W * r + w)) % SEQ
        ang = 2.0 * np.pi * m / SEQ
        c, s = np.cos(ang), np.sin(ang)
        out[w] = np.block([[c, s], [-s, c]])
    return out


@functools.lru_cache(maxsize=None)
def _col_dft():
    k2 = np.arange(GRID_W)
    ang = 2.0 * np.pi * ((k2[:, None] * k2[None, :]) % GRID_W) / GRID_W
    scale = 1.0 / math.sqrt(SEQ * FOURIER_GROUP)
    return (np.concatenate([np.cos(ang), np.sin(ang)], axis=1) * scale).astype(np.float32)


@functools.lru_cache(maxsize=None)
def _bias_index():
    c = np.arange(GRID_W)
    start = np.clip(c - NA_COLS // 2, 0, GRID_W - NA_COLS)
    valid = (c[None, :] >= start[:, None]) & (c[None, :] < start[:, None] + NA_COLS)
    dc = np.clip(c[None, :] - c[:, None] + (NA_COLS - 1), 0, 2 * NA_COLS - 2)
    return dc.astype(np.int32), valid


@functools.lru_cache(maxsize=None)
def _strict_lower(n):
    return np.tril(np.ones((n, n), np.float32), k=-1)


def _adaln_kernel(c_ref, w_ref, b_ref, o_ref):
    c = c_ref[...]
    s = c * jax.nn.sigmoid(c)
    o_ref[...] = jnp.dot(s, w_ref[...], precision=lax.Precision.HIGHEST,
                         preferred_element_type=F32) + b_ref[...]


def adaln(cond8, w, b):
    n = w.shape[1]
    tn = 1536
    return pl.pallas_call(
        _adaln_kernel,
        grid=(n // tn,),
        in_specs=[pl.BlockSpec((8, D_MODEL), lambda j: (0, 0)),
                  pl.BlockSpec((D_MODEL, tn), lambda j: (0, j)),
                  pl.BlockSpec((1, tn), lambda j: (0, j))],
        out_specs=pl.BlockSpec((8, tn), lambda j: (0, j)),
        out_shape=jax.ShapeDtypeStruct((8, n), F32),
        compiler_params=_cparams(("arbitrary",)),
        name="adaln",
    )(cond8, w, b.reshape(1, n))


def _norm_mod(x, g, shift, scale):
    ms = jnp.mean(x * x, axis=-1, keepdims=True)
    return (x * lax.rsqrt(ms + EPS) * g) * (1.0 + scale) + shift


def _in_proj_kernel(x_ref, mod_ref, g_ref, w_ref, cs_ref, cos_ref, s1_ref, s2_ref,
                    qr_ref, qp_ref, kr_ref, v_ref, a_ref, h_scr):
    m = mod_ref[0]
    h_scr[...] = _norm_mod(x_ref[0], g_ref[...], m[0:1], m[1:2]).astype(BF16)
    cos, s1, s2 = cos_ref[...], s1_ref[...], s2_ref[...]

    def rope(t):
        return (t * cos + pltpu.roll(t, LANES - 16, axis=1) * s1 + pltpu.roll(t, 16, axis=1) * s2)

    f = jnp.dot(h_scr[...], w_ref[:, 0:D_FOURIER], preferred_element_type=F32)
    a = jnp.dot(f.astype(BF16), cs_ref[...], preferred_element_type=F32)
    a_ref[0, 0] = a[:, :D_FOURIER].astype(BF16)
    a_ref[0, 1] = a[:, D_FOURIER:].astype(BF16)

    scale = HEAD_DIM ** -0.5 * LOG2E
    for c in range(D_NA // 256):
        lo = D_FOURIER + 256 * c
        q = jnp.dot(h_scr[...], w_ref[:, lo:lo + 256], preferred_element_type=F32)
        k = jnp.dot(h_scr[...], w_ref[:, lo + D_NA:lo + D_NA + 256], preferred_element_type=F32)
        v = jnp.dot(h_scr[...], w_ref[:, lo + 2 * D_NA:lo + 2 * D_NA + 256], preferred_element_type=F32)
        v_ref[0, :, 256 * c:256 * c + 256] = v.astype(BF16)
        for s in range(2):
            sl = slice(LANES * s, LANES * (s + 1))
            ol = slice(256 * c + LANES * s, 256 * c + LANES * (s + 1))
            qs, ks = q[:, sl], k[:, sl]
            qp_ref[0, :, ol] = (qs * scale).astype(BF16)
            qr_ref[0, :, ol] = (rope(qs) * scale).astype(BF16)
            kr_ref[0, :, ol] = rope(ks).astype(BF16)


def in_proj(x, mods, g1, w_in_bf16):
    b, n, d = x.shape
    tm = TM_PROJ
    cos, s1, s2 = _rope_tables()
    cs = _mxu_const(_chan_dft())
    tok = lambda bi, i: (bi, i, 0)
    const2 = lambda bi, i: (0, 0)
    tab = pl.BlockSpec((tm, LANES), lambda bi, i: (i, 0))
    qkv_shape = jax.ShapeDtypeStruct((b, n, D_NA), BF16)
    qkv_spec = pl.BlockSpec((1, tm, D_NA), tok)
    return pl.pallas_call(
        _in_proj_kernel,
        grid=(b, n // tm),
        in_specs=[pl.BlockSpec((1, tm, d), tok),
                  pl.BlockSpec((1, 8, d), lambda bi, i: (bi, 0, 0)),
                  pl.BlockSpec((1, d), const2),
                  pl.BlockSpec((d, D_IN_PROJ), const2),
                  pl.BlockSpec((D_FOURIER, 2 * D_FOURIER), const2),
                  tab, tab, tab],
        out_specs=[qkv_spec, qkv_spec, qkv_spec, qkv_spec,
                   pl.BlockSpec((1, 2, tm, D_FOURIER), lambda bi, i: (bi, 0, i, 0))],
        out_shape=[qkv_shape, qkv_shape, qkv_shape, qkv_shape,
                   jax.ShapeDtypeStruct((b, 2, n, D_FOURIER), BF16)],
        scratch_shapes=[pltpu.VMEM((tm, d), BF16)],
        compiler_params=_cparams(("arbitrary", "arbitrary")),
        name="in_proj",
    )(x, mods, g1.reshape(1, d), w_in_bf16, cs, jnp.asarray(cos), jnp.asarray(s1), jnp.asarray(s2))


def _ctx_proj_kernel(x_ref, mod_ref, g_ref, w_ref, k_ref, v_ref):
    m = mod_ref[...]
    h = _norm_mod(x_ref[0], g_ref[...], m[0:1], m[1:2]).astype(BF16)
    k_ref[0] = jnp.dot(h, w_ref[:, 0:D_NA], preferred_element_type=F32).astype(BF16)
    v_ref[0] = jnp.dot(h, w_ref[:, D_NA:2 * D_NA], preferred_element_type=F32).astype(BF16)


def ctx_proj(ctx, mod_ctx, g1, w_kv_bf16):
    b, l, d = ctx.shape
    shape = jax.ShapeDtypeStruct((b, l, D_NA), BF16)
    spec = pl.BlockSpec((1, l, D_NA), lambda bi: (bi, 0, 0))
    return pl.pallas_call(
        _ctx_proj_kernel,
        grid=(b,),
        in_specs=[pl.BlockSpec((1, l, d), lambda bi: (bi, 0, 0)),
                  pl.BlockSpec((8, d), lambda bi: (0, 0)),
                  pl.BlockSpec((1, d), lambda bi: (0, 0)),
                  pl.BlockSpec((d, 2 * D_NA), lambda bi: (0, 0))],
        out_specs=[spec, spec],
        out_shape=[shape, shape],
        compiler_params=_cparams(("arbitrary",)),
        name="ctx_proj",
    )(ctx, mod_ctx, g1.reshape(1, d), w_kv_bf16)


def _dft_rows_kernel(a_ref, g_ref, z_ref):
    for j in range(DFT_TW):
        sl = slice(D_FOURIER * j, D_FOURIER * (j + 1))
        rhs = jnp.concatenate([a_ref[0, 0, :, sl], a_ref[0, 1, :, sl]], axis=0)
        z = jnp.dot(g_ref[j], rhs, preferred_element_type=F32)
        z_ref[0, 0, j] = z[:GRID_H].astype(BF16)
        z_ref[0, 1, j] = z[GRID_H:].astype(BF16)


def dft_rows(a):
    b = a.shape[0]
    a4 = a.reshape(b, 2, GRID_H, GRID_W * D_FOURIER)
    g = _mxu_const(_row_dft())
    return pl.pallas_call(
        _dft_rows_kernel,
        grid=(GRID_W // DFT_TW, b),
        in_specs=[pl.BlockSpec((1, 2, GRID_H, DFT_TW * D_FOURIER), lambda j, bi: (bi, 0, 0, j)),
                  pl.BlockSpec((DFT_TW, 2 * GRID_H, 2 * GRID_H), lambda j, bi: (j, 0, 0))],
        out_specs=pl.BlockSpec((1, 2, DFT_TW, GRID_H, D_FOURIER), lambda j, bi: (bi, 0, j, 0, 0)),
        out_shape=jax.ShapeDtypeStruct((b, 2, GRID_W, GRID_H, D_FOURIER), BF16),
        compiler_params=_cparams(("arbitrary", "arbitrary")),
        name="dft_rows",
    )(a4, g)


def _dft_cols_kernel(z_ref, cs_ref, o_ref):
    rhs = jnp.concatenate([z_ref[0, 0], z_ref[0, 1]], axis=0)
    o_ref[0] = jnp.dot(cs_ref[...], rhs, preferred_element_type=F32).astype(BF16)


def dft_cols(z):
    b = z.shape[0]
    ncol = GRID_H * D_FOURIER
    z4 = z.reshape(b, 2, GRID_W, ncol)
    cs = _mxu_const(_col_dft())
    out = pl.pallas_call(
        _dft_cols_kernel,
        grid=(b, ncol // DFT_TN),
        in_specs=[pl.BlockSpec((1, 2, GRID_W, DFT_TN), lambda bi, j: (bi, 0, 0, j)),
                  pl.BlockSpec((GRID_W, 2 * GRID_W), lambda bi, j: (0, 0))],
        out_specs=pl.BlockSpec((1, GRID_W, DFT_TN), lambda bi, j: (bi, 0, j)),
        out_shape=jax.ShapeDtypeStruct((b, GRID_W, ncol), BF16),
        compiler_params=_cparams(("arbitrary", "arbitrary")),
        name="dft_cols",
    )(z4, cs)
    return out.reshape(b, SEQ, D_FOURIER)


WIN_ROWS = NA_ROWS + 2
WIN_KEYS = WIN_ROWS * GRID_W
WIN_TILES = WIN_KEYS // LANES
N_DR = 2 * NA_ROWS + 1
SOFT_ROWS = 32
ATT_PAIRS = ATT_ROWS // 2


def _attention_kernel(qr_ref, qp_ref, k_ref, v_ref, kc_ref, vc_ref, tb_ref, o_ref,
                      sc_scr, pc_scr, s_scr, p_scr, o_scr):
    rb = pl.program_id(2)
    nq = ATT_ROWS * GRID_W
    lane = lax.broadcasted_iota(jnp.int32, (nq, LANES), 1)
    first = lane < HEAD_DIM
    first2 = lax.broadcasted_iota(jnp.int32, (2 * GRID_W, LANES), 1) < HEAD_DIM
    nt = (((1,), (1,)), ((), ()))

    qp = qp_ref[0]
    zero = jnp.zeros_like(qp)
    sc_scr[0] = lax.dot_general(jnp.where(first, qp, zero), kc_ref[0], nt, preferred_element_type=F32)
    sc_scr[1] = lax.dot_general(jnp.where(first, zero, qp), kc_ref[0], nt, preferred_element_type=F32)

    def window(jp):
        r0 = rb * ATT_ROWS + 2 * jp
        ws = jnp.clip(r0 - NA_ROWS // 2, 0, GRID_H - WIN_ROWS)
        return r0, ws, pl.multiple_of(ws * GRID_W, GRID_W)

    def scores(jp):
        _, _, koff = window(jp)
        q2 = qr_ref[0, jp * 2 * GRID_W:(jp + 1) * 2 * GRID_W, :]
        z2 = jnp.zeros_like(q2)
        qm = jnp.concatenate([jnp.where(first2, q2, z2), jnp.where(first2, z2, q2)], axis=0)
        kw = k_ref[0, pl.ds(koff, WIN_KEYS), :]
        s_scr[jp] = lax.dot_general(qm, kw, nt, preferred_element_type=F32)

    def softmax(jp):
        r0, ws, _ = window(jp)
        u, qoff = jp, jp * 2 * GRID_W
        for c in range(4):
            hh, i = c // 2, c % 2
            r = r0 + i
            off = jnp.clip(r - NA_ROWS // 2, 0, GRID_H - NA_ROWS) - ws
            sel = []
            for t in range(WIN_TILES):
                v0 = (2 * t >= off) & (2 * t < off + NA_ROWS)
                v1 = (2 * t + 1 >= off) & (2 * t + 1 < off + NA_ROWS)
                kind = jnp.where(v0 & v1, 0, jnp.where(v0, 2, 1))
                sel.append((kind, jnp.where(v0 | v1, ws + 2 * t - r + NA_ROWS, N_DR - 1)))
            for h in range(GRID_W // SOFT_ROWS):
                lo = h * SOFT_ROWS
                rows = slice(c * GRID_W + lo, c * GRID_W + lo + SOFT_ROWS)
                qrow = slice(qoff + i * GRID_W + lo, qoff + i * GRID_W + lo + SOFT_ROWS)
                tiles = [s_scr[u, rows, t * LANES:(t + 1) * LANES]
                         + tb_ref[0, hh, sel[t][0], sel[t][1], lo:lo + SOFT_ROWS, :] for t in range(WIN_TILES)]
                sc = sc_scr[hh, qrow, :]
                mt = jnp.maximum(sc[:, :LANES], sc[:, LANES:])
                for tl in tiles:
                    mt = jnp.maximum(mt, tl)
                m = jnp.max(mt, axis=1, keepdims=True)
                pc_scr[hh, qrow, :] = jnp.exp2(sc - m).astype(BF16)
                for t, tl in enumerate(tiles):
                    p_scr[u, rows, t * LANES:(t + 1) * LANES] = jnp.exp2(tl - m).astype(BF16)


    def weighted_values(jp):
        _, _, koff = window(jp)
        vw = jnp.concatenate([v_ref[0, pl.ds(koff, WIN_KEYS), :], jnp.ones((WIN_KEYS, LANES), BF16)], axis=1)
        o = jnp.dot(p_scr[jp], vw, preferred_element_type=F32)
        qs = slice(jp * 2 * GRID_W, (jp + 1) * 2 * GRID_W)
        o_scr[0, qs, :] = o[:2 * GRID_W]
        o_scr[1, qs, :] = o[2 * GRID_W:]

    npair = ATT_PAIRS
    scores(0)
    scores(1)
    softmax(0)
    for jp in range(npair):
        if jp + 2 < npair:
            scores(jp + 2)
        if jp + 1 < npair:
            softmax(jp + 1)
        weighted_values(jp)

    vc = jnp.concatenate([vc_ref[0], jnp.ones((CTX_LEN, LANES), BF16)], axis=1)
    oa = o_scr[0] + jnp.dot(pc_scr[0], vc, preferred_element_type=F32)
    ob = o_scr[1] + jnp.dot(pc_scr[1], vc, preferred_element_type=F32)
    o_ref[0] = jnp.where(first, oa[:, :LANES] / oa[:, LANES:], ob[:, :LANES] / ob[:, LANES:]).astype(BF16)


def bias_tables(rpb):
    dc, valid = _bias_index()
    n_dc = 2 * NA_COLS - 1
    onehot = (dc.reshape(1, -1) == np.arange(n_dc).reshape(-1, 1)).astype(np.float32)
    t = jnp.dot(rpb.reshape(-1, n_dc), jnp.asarray(onehot), precision=lax.Precision.HIGHEST)
    t = jnp.where(valid[None, None], LOG2E * t.reshape(N_HEADS, 2 * NA_ROWS - 1, GRID_W, GRID_W), NEG)
    t = jnp.pad(t, ((0, 0), (1, 2), (0, 0), (0, 0)), constant_values=NEG)
    left, right = t[:, :N_DR], t[:, 1:N_DR + 1]
    neg = jnp.full_like(left, NEG)
    tb = jnp.stack([jnp.concatenate([left, right], axis=-1),
                    jnp.concatenate([neg, right], axis=-1),
                    jnp.concatenate([left, neg], axis=-1)], axis=1)
    return tb.reshape(N_PAIRS, 2, 3, N_DR, GRID_W, LANES).astype(F32)


def attention(qr, qp, kr, v, kc, vc, tb):
    b, n, _ = qr.shape
    tq = ATT_ROWS * GRID_W
    qspec = pl.BlockSpec((1, tq, LANES), lambda bi, hp, i: (bi, i, hp))
    kspec = pl.BlockSpec((1, n, LANES), lambda bi, hp, i: (bi, 0, hp))
    cspec = pl.BlockSpec((1, CTX_LEN, LANES), lambda bi, hp, i: (bi, 0, hp))
    return pl.pallas_call(
        _attention_kernel,
        grid=(b, N_PAIRS, GRID_H // ATT_ROWS),
        in_specs=[qspec, qspec, kspec, kspec, cspec, cspec,
                  pl.BlockSpec((1, 2, 3, N_DR, GRID_W, LANES), lambda bi, hp, i: (hp, 0, 0, 0, 0, 0))],
        out_specs=qspec,
        out_shape=jax.ShapeDtypeStruct((b, n, D_NA), BF16),
        scratch_shapes=[pltpu.VMEM((2, tq, CTX_LEN), F32),
                        pltpu.VMEM((2, tq, CTX_LEN), BF16),
                        pltpu.VMEM((ATT_PAIRS, 4 * GRID_W, WIN_KEYS), F32),
                        pltpu.VMEM((ATT_PAIRS, 4 * GRID_W, WIN_KEYS), BF16),
                        pltpu.VMEM((2, tq, 2 * LANES), F32)],
        compiler_params=_cparams(("arbitrary", "arbitrary", "arbitrary")),
        name="attention",
    )(qr, qp, kr, v, kc, vc, tb)


def _rms(x, g):
    ms = jnp.mean(x * x, axis=-1, keepdims=True)
    return x * lax.rsqrt(ms + EPS) * g


TILE_ROWS = D_MODEL // LANES


def _store_token_tiles(ref, val, start=0):
    rows = val.shape[0]
    for j in range(TILE_ROWS):
        ref[pl.ds(start + j, rows, stride=TILE_ROWS), :] = val[:, LANES * j:LANES * (j + 1)]


def _load_token_tiles(ref, start, rows):
    return jnp.concatenate(
        [ref[pl.ds(start + j, rows, stride=TILE_ROWS), :] for j in range(TILE_ROWS)], axis=1)


def _out_proj_kernel(fr_ref, na_ref, x_ref, mod_ref, wf_ref, wo_ref, go_ref, g2_ref, wr_ref, tri_ref,
                     x1_ref, h2_ref, meta_ref, metat_ref, cnt_ref, run_scr):
    @pl.when((pl.program_id(0) == 0) & (pl.program_id(1) == 0))
    def _():
        run_scr[...] = jnp.zeros_like(run_scr)

    m = mod_ref[0]
    go = go_ref[...]
    fo = jnp.dot(fr_ref[0], wf_ref[...], preferred_element_type=F32)
    fn = _rms(fo, go[:, :D_FOURIER]).astype(BF16)
    nn = _rms(na_ref[0].astype(F32), go[:, D_FOURIER:]).astype(BF16)
    y = (jnp.dot(fn, wo_ref[0:D_FOURIER, :], preferred_element_type=F32)
         + jnp.dot(nn, wo_ref[D_FOURIER:, :], preferred_element_type=F32))
    x1 = x_ref[0] + m[2:3] * y
    x1_ref[0] = x1
    h2 = _norm_mod(x1, g2_ref[...], m[3:4], m[4:5])
    _store_token_tiles(h2_ref, h2)
    logits = jnp.dot(h2.astype(BF16), wr_ref[...], preferred_element_type=F32)

    tm = logits.shape[0]
    lane = lax.broadcasted_iota(jnp.int32, (tm, LANES), 1).astype(F32)
    ninf = jnp.float32(-jnp.inf)

    def argmax_first(vals):
        mx = jnp.max(vals, axis=1, keepdims=True)
        idx = jnp.min(jnp.where(vals == mx, lane, float(LANES)), axis=1, keepdims=True)
        return mx, idx

    lg = jnp.where(lane < N_GROUPS, logits, ninf)
    gmax, gidx = argmax_first(lg)
    pg = 1.0 / jnp.sum(jnp.exp(lg - gmax), axis=1, keepdims=True)
    lo = N_GROUPS + EXPERTS_PER_GROUP * gidx
    le = jnp.where((lane >= lo) & (lane < lo + EXPERTS_PER_GROUP), logits, ninf)
    e1, i1 = argmax_first(le)
    e2, i2 = argmax_first(jnp.where(lane == i1, ninf, le))
    dd = jnp.exp(e2 - e1)
    gate1 = pg / (1.0 + dd)
    gate2 = pg * dd / (1.0 + dd)

    hot1 = lane == i1
    hot2 = lane == i2
    onehot = jnp.where(hot1 | hot2, 1.0, 0.0)
    cnt = jnp.dot(tri_ref[...], onehot.astype(BF16), preferred_element_type=F32) + run_scr[...]
    rank1 = jnp.sum(jnp.where(hot1, cnt, 0.0), axis=1, keepdims=True)
    rank2 = jnp.sum(jnp.where(hot2, cnt, 0.0), axis=1, keepdims=True)
    run_scr[...] = run_scr[...] + jnp.sum(onehot, axis=0, keepdims=True)

    meta = jnp.where(lane == 0, i1 - N_GROUPS,
           jnp.where(lane == 1, i2 - N_GROUPS,
           jnp.where(lane == 2, rank1,
           jnp.where(lane == 3, rank2,
           jnp.where(lane == 4, gate1,
           jnp.where(lane == 5, gate2, 0.0))))))
    meta_ref[...] = meta
    metat_ref[...] = jnp.transpose(meta)[0:8, :]
    cnt_ref[...] = jnp.broadcast_to(run_scr[...], cnt_ref.shape)


def out_proj(fr, na, x, mods, w_fmix_bf16, w_out_bf16, g_out, g2, w_router_bf16):
    b, n, d = x.shape
    tm = TM_PROJ
    steps = n // tm
    tok = lambda bi, i: (bi, i, 0)
    const2 = lambda bi, i: (0, 0)
    flat = lambda bi, i: (bi * steps + i, 0)
    tri = _mxu_const(_strict_lower(tm))
    return pl.pallas_call(
        _out_proj_kernel,
        grid=(b, steps),
        in_specs=[pl.BlockSpec((1, tm, D_FOURIER), tok),
                  pl.BlockSpec((1, tm, D_NA), tok),
                  pl.BlockSpec((1, tm, d), tok),
                  pl.BlockSpec((1, 8, d), lambda bi, i: (bi, 0, 0)),
                  pl.BlockSpec((D_FOURIER, D_FOURIER), const2),
                  pl.BlockSpec((d, d), const2),
                  pl.BlockSpec((1, d), const2),
                  pl.BlockSpec((1, d), const2),
                  pl.BlockSpec((d, LANES), const2),
                  pl.BlockSpec((tm, tm), const2)],
        out_specs=[pl.BlockSpec((1, tm, d), tok),
                   pl.BlockSpec((tm * TILE_ROWS, LANES), flat),
                   pl.BlockSpec((tm, LANES), flat),
                   pl.BlockSpec((8, tm), lambda bi, i: (0, bi * steps + i)),
                   pl.BlockSpec((8, LANES), const2)],
        out_shape=[jax.ShapeDtypeStruct((b, n, d), F32),
                   jax.ShapeDtypeStruct((b * n * TILE_ROWS, LANES), F32),
                   jax.ShapeDtypeStruct((b * n, LANES), F32),
                   jax.ShapeDtypeStruct((8, b * n), F32),
                   jax.ShapeDtypeStruct((8, LANES), F32)],
        scratch_shapes=[pltpu.VMEM((1, LANES), F32)],
        compiler_params=_cparams(("arbitrary", "arbitrary")),
        name="out_proj",
    )(fr, na, x, mods, w_fmix_bf16, w_out_bf16, g_out.reshape(1, d), g2.reshape(1, d), w_router_bf16, tri)


DMA_CHUNK = 8


TD_DISP = 512


def _dispatch_kernel(dest_ref, pstart_ref, count_ref, h2_ref, xs_hbm, zero_scr, sem, pad_sem):
    i = pl.program_id(0)
    nt = dest_ref.shape[0] // 2

    @pl.when(i == 0)
    def _():
        zero_scr[...] = jnp.zeros_like(zero_scr)

        def per_expert(e, npad):
            lo = pstart_ref[e] + count_ref[e]
            mid = pstart_ref[e] + ((count_ref[e] + MOE_CHUNK - 1) // MOE_CHUNK) * MOE_CHUNK
            hi = pstart_ref[e] + ((count_ref[e] + MOE_BLK - 1) // MOE_BLK) * MOE_BLK

            def fill(s, carry):
                pltpu.make_async_copy(zero_scr.at[pl.ds(0, TILE_ROWS)],
                                      xs_hbm.at[pl.ds(s * TILE_ROWS, TILE_ROWS)], pad_sem).start()
                return carry

            def fill_chunk(s, carry):
                pltpu.make_async_copy(zero_scr.at[pl.ds(0, MOE_CHUNK * TILE_ROWS)],
                                      xs_hbm.at[pl.ds(mid * TILE_ROWS + s * MOE_CHUNK * TILE_ROWS,
                                                      MOE_CHUNK * TILE_ROWS)], pad_sem).start()
                return carry

            lax.fori_loop(lo, mid, fill, 0)
            lax.fori_loop(0, (hi - mid) // MOE_CHUNK, fill_chunk, 0)
            return npad + (hi - lo)

        npad = lax.fori_loop(0, N_EXPERTS, per_expert, 0)

        blk_rows = MOE_BLK * TILE_ROWS
        first_free = (pstart_ref[N_EXPERTS - 1] + count_ref[N_EXPERTS - 1] + MOE_BLK - 1) // MOE_BLK
        n_blocks = xs_hbm.shape[0] // blk_rows

        def fill_block(bk, carry):
            pltpu.make_async_copy(zero_scr, xs_hbm.at[pl.ds(bk * blk_rows, blk_rows)], pad_sem).start()
            return carry

        lax.fori_loop(first_free, n_blocks, fill_block, 0)
        rows = npad * TILE_ROWS + (n_blocks - first_free) * blk_rows

        @pl.when(rows > 0)
        def _():
            pltpu.make_async_copy(xs_hbm.at[pl.ds(0, rows)], xs_hbm.at[pl.ds(0, rows)], pad_sem).wait()

    def issue(c, carry):
        t0 = c * DMA_CHUNK
        for u in range(DMA_CHUNK):
            for k in range(2):
                d = dest_ref[k * nt + i * TD_DISP + t0 + u]
                pltpu.make_async_copy(h2_ref.at[pl.ds((t0 + u) * TILE_ROWS, TILE_ROWS)],
                                      xs_hbm.at[pl.ds(d * TILE_ROWS, TILE_ROWS)], sem).start(priority=k)
        return carry

    lax.fori_loop(0, TD_DISP // DMA_CHUNK, issue, 0)
    rows = 2 * TD_DISP * TILE_ROWS
    pltpu.make_async_copy(xs_hbm.at[pl.ds(0, rows)], xs_hbm.at[pl.ds(0, rows)], sem).wait()


def dispatch(h2_tiles, dest, pstarts, counts, n_slots):
    nt = dest.shape[0] // 2
    grid_spec = pltpu.PrefetchScalarGridSpec(
        num_scalar_prefetch=3,
        grid=(nt // TD_DISP,),
        in_specs=[pl.BlockSpec((TD_DISP * TILE_ROWS, LANES), lambda i, ds, ps, ct: (i, 0))],
        out_specs=pl.BlockSpec(memory_space=pl.ANY),
        scratch_shapes=[pltpu.VMEM((MOE_BLK * TILE_ROWS, LANES), F32),
                        pltpu.SemaphoreType.DMA(()),
                        pltpu.SemaphoreType.DMA(())],
    )
    return pl.pallas_call(
        _dispatch_kernel,
        grid_spec=grid_spec,
        out_shape=jax.ShapeDtypeStruct((n_slots * TILE_ROWS, LANES), F32),
        compiler_params=_cparams(("arbitrary",)),
        name="dispatch",
    )(dest, pstarts, counts, h2_tiles)


def _experts_kernel(be_ref, nused_ref, valid_ref, xs_ref, wg_ref, wu_ref, wd_ref, ys_ref,
                    wg_scr, wu_scr, wd_scr):
    i = pl.program_id(0)
    valid = valid_ref[i]
    changed = (i == 0) | (be_ref[i] != be_ref[jnp.maximum(i - 1, 0)])
    chunk_rows = MOE_CHUNK * TILE_ROWS

    @pl.when(changed & (valid > 0))
    def _():
        wg_scr[...] = wg_ref[0].astype(BF16)
        wu_scr[...] = wu_ref[0].astype(BF16)
        wd_scr[...] = wd_ref[0].astype(BF16)

    def run(n_chunks):
        hmids = []
        for h in range(n_chunks):
            x = _load_token_tiles(xs_ref, h * chunk_rows, MOE_CHUNK).astype(BF16)
            g = jnp.dot(x, wg_scr[...], preferred_element_type=F32)
            u = jnp.dot(x, wu_scr[...], preferred_element_type=F32)
            hmids.append((g * jax.nn.sigmoid(g) * u).astype(BF16))
        for h, hmid in enumerate(hmids):
            _store_token_tiles(ys_ref, jnp.dot(hmid, wd_scr[...], preferred_element_type=F32), h * chunk_rows)
        if n_chunks * chunk_rows < ys_ref.shape[0]:
            ys_ref[n_chunks * chunk_rows:, :] = jnp.zeros((ys_ref.shape[0] - n_chunks * chunk_rows, LANES), F32)

    for n_chunks in range(MOE_BLK // MOE_CHUNK + 1):
        lo, hi = (n_chunks - 1) * MOE_CHUNK, n_chunks * MOE_CHUNK
        pl.when((valid > lo) & (valid <= hi))(functools.partial(run, n_chunks))


def experts(xs_tiles, block_expert, nused, block_valid, w_gate, w_up, w_down):
    d = D_MODEL
    nblk = block_expert.shape[0]
    blk_rows = MOE_BLK * TILE_ROWS
    wmap = lambda i, be, nu, bv: (be[i], 0, 0)
    grid_spec = pltpu.PrefetchScalarGridSpec(
        num_scalar_prefetch=3,
        grid=(nblk,),
        in_specs=[pl.BlockSpec((blk_rows, LANES), lambda i, be, nu, bv: (jnp.minimum(i, nu[0] - 1), 0)),
                  pl.BlockSpec((1, d, D_EXPERT), wmap),
                  pl.BlockSpec((1, d, D_EXPERT), wmap),
                  pl.BlockSpec((1, D_EXPERT, d), wmap)],
        out_specs=pl.BlockSpec((blk_rows, LANES), lambda i, be, nu, bv: (i, 0)),
        scratch_shapes=[pltpu.VMEM((d, D_EXPERT), BF16),
                        pltpu.VMEM((d, D_EXPERT), BF16),
                        pltpu.VMEM((D_EXPERT, d), BF16)],
    )
    return pl.pallas_call(
        _experts_kernel,
        grid_spec=grid_spec,
        out_shape=jax.ShapeDtypeStruct((nblk * blk_rows, LANES), F32),
        compiler_params=_cparams(("arbitrary",)),
        name="experts",
    )(block_expert, nused, block_valid, xs_tiles, w_gate, w_up, w_down)


def _combine_kernel(dest_ref, ys_hbm, x1_ref, meta_ref, mod_ref, gf_ref, o_ref, ybuf, sem):
    i = pl.program_id(0)
    nstep = pl.num_programs(0)
    tc = TC_COMB
    nt = dest_ref.shape[0] // 2
    half_rows = tc * TILE_ROWS
    buf_rows = 2 * half_rows

    def gather(step, slot):
        def issue(c, carry):
            t0 = c * DMA_CHUNK
            for u in range(DMA_CHUNK):
                for k in range(2):
                    d = dest_ref[k * nt + step * tc + t0 + u]
                    pltpu.make_async_copy(
                        ys_hbm.at[pl.ds(d * TILE_ROWS, TILE_ROWS)],
                        ybuf.at[pl.ds(slot * buf_rows + k * half_rows + (t0 + u) * TILE_ROWS, TILE_ROWS)],
                        sem.at[slot]).start(priority=k)
            return carry

        lax.fori_loop(0, tc // DMA_CHUNK, issue, 0)

    @pl.when(i == 0)
    def _():
        gather(0, 0)

    @pl.when(i + 1 < nstep)
    def _():
        gather(i + 1, (i + 1) % 2)

    slot = i % 2
    start = pl.multiple_of(slot * buf_rows, buf_rows)
    pltpu.make_async_copy(ys_hbm.at[pl.ds(0, buf_rows)], ybuf.at[pl.ds(start, buf_rows)], sem.at[slot]).wait()
    meta = meta_ref[...]
    y0 = _load_token_tiles(ybuf, start, tc)
    y1 = _load_token_tiles(ybuf, start + half_rows, tc)
    moe = y0 * meta[:, 4:5] + y1 * meta[:, 5:6]
    x2 = x1_ref[...] + mod_ref[0][5:6] * moe
    o_ref[...] = _rms(x2, gf_ref[...])


def combine(dest_flat, ys, x1_flat, meta, mods, g_final, n_per_batch):
    nt, d = x1_flat.shape
    tc = TC_COMB
    per_b = n_per_batch // tc
    grid_spec = pltpu.PrefetchScalarGridSpec(
        num_scalar_prefetch=1,
        grid=(nt // tc,),
        in_specs=[pl.BlockSpec(memory_space=pl.ANY),
                  pl.BlockSpec((tc, d), lambda i, ds: (i, 0)),
                  pl.BlockSpec((tc, LANES), lambda i, ds: (i, 0)),
                  pl.BlockSpec((1, 8, d), lambda i, ds: (i // per_b, 0, 0)),
                  pl.BlockSpec((1, d), lambda i, ds: (0, 0))],
        out_specs=pl.BlockSpec((tc, d), lambda i, ds: (i, 0)),
        scratch_shapes=[pltpu.VMEM((2 * 2 * tc * TILE_ROWS, LANES), F32),
                        pltpu.SemaphoreType.DMA((2,))],
    )
    return pl.pallas_call(
        _combine_kernel,
        grid_spec=grid_spec,
        out_shape=jax.ShapeDtypeStruct((nt, d), F32),
        compiler_params=_cparams(("arbitrary",)),
        name="combine",
    )(dest_flat, ys, x1_flat, meta, mods, g_final.reshape(1, d))


def _dispatch_plan(metat, counts_row, nt):
    counts = counts_row[N_GROUPS:N_GROUPS + N_EXPERTS].astype(jnp.int32)
    pcounts = ((counts + MOE_BLK - 1) // MOE_BLK) * MOE_BLK
    pends = jnp.cumsum(pcounts)
    pstarts = pends - pcounts
    eid = metat[0:2].astype(jnp.int32)
    dest = metat[2:4].astype(jnp.int32)
    for e in range(N_EXPERTS):
        dest = dest + jnp.where(eid == e, pstarts[e], 0)
    dest = dest.reshape(-1)
    nblk = (nt * 2) // MOE_BLK + N_EXPERTS
    first_slot = jnp.arange(nblk, dtype=jnp.int32) * MOE_BLK
    block_expert = jnp.minimum(
        jnp.sum((pends[None, :] <= first_slot[:, None]).astype(jnp.int32), axis=1), N_EXPERTS - 1)
    nused = (pends[-1] // MOE_BLK).astype(jnp.int32).reshape(1)
    seg_end = jnp.sum(jnp.where(block_expert[:, None] == jnp.arange(N_EXPERTS)[None, :],
                                (pstarts + counts)[None, :], 0), axis=1)
    block_valid = jnp.where(first_slot < pends[-1], jnp.clip(seg_end - first_slot, 0, MOE_BLK), 0)
    return (dest, pstarts.astype(jnp.int32), counts, block_expert.astype(jnp.int32), nused,
            block_valid.astype(jnp.int32))


def kernel(x, c, ctx, c_ctx, w_ada, b_ada, g_norm1, w_in, w_fmix, rpb, g_out, w_out, g_norm2,
           w_router_group, w_router_expert, w_gate, w_up, w_down, g_final):
    b, n, d = x.shape
    assert (b, n, d) == (c.shape[0], SEQ, D_MODEL) and w_ada.shape[0] == 1
    nt = b * n

    cond8 = jnp.zeros((8, d), F32).at[0:b].set(c).at[b].set(c_ctx)
    mod = adaln(cond8, w_ada[0], b_ada[0])
    mods = jnp.pad(mod[0:b].reshape(b, N_MOD, d), ((0, 0), (0, 2), (0, 0)))
    mod_ctx = jnp.pad(mod[b].reshape(N_MOD, d), ((0, 2), (0, 0)))

    w_in_b = w_in[0].astype(BF16)
    qr, qp, kr, v, a = in_proj(x, mods, g_norm1[0], w_in_b)
    kc, vc = ctx_proj(ctx, mod_ctx, g_norm1[0], w_in_b[:, D_FOURIER + D_NA:])

    fr = dft_cols(dft_rows(a))
    na = attention(qr, qp, kr, v, kc, vc, bias_tables(rpb[0]))

    w_router = jnp.concatenate(
        [w_router_group[0], w_router_expert[0],
         jnp.zeros((d, LANES - N_GROUPS - N_EXPERTS), F32)], axis=1).astype(BF16)
    x1, h2_tiles, meta, metat, cnt = out_proj(fr, na, x, mods, w_fmix[0].astype(BF16),
                                              w_out[0].astype(BF16), g_out[0], g_norm2[0], w_router)

    dest, pstarts, counts, block_expert, nused, block_valid = _dispatch_plan(metat, cnt[0], nt)
    xs_tiles = dispatch(h2_tiles, dest, pstarts, counts, block_expert.shape[0] * MOE_BLK)
    ys_tiles = experts(xs_tiles, block_expert, nused, block_valid, w_gate[0], w_up[0], w_down[0])
    out = combine(dest, ys_tiles, x1.reshape(nt, d), meta, mods, g_final, n)
    return out.reshape(b, n, d)
```

```python
import functools
import math

import numpy as np
import jax
import jax.numpy as jnp
from jax import lax
from jax.experimental import pallas as pl
from jax.experimental.pallas import tpu as pltpu

F32 = jnp.float32
BF16 = jnp.bfloat16

D_MODEL = 1024
GRID_W = 64
GRID_H = 128
SEQ = GRID_W * GRID_H
CTX_LEN = 256
D_FOURIER = 256
FOURIER_GROUP = 64
HEAD_DIM = 64
N_HEADS = 12
D_NA = N_HEADS * HEAD_DIM
N_PAIRS = N_HEADS // 2
NA_ROWS = 8
NA_COLS = 16
ROPE_THETA = 10000.0
N_GROUPS = 4
EXPERTS_PER_GROUP = 8
N_EXPERTS = N_GROUPS * EXPERTS_PER_GROUP
D_EXPERT = 512
N_MOD = 6
D_IN_PROJ = D_FOURIER + 3 * D_NA
EPS = 1e-6
LANES = 128
NEG = -1e30
LOG2E = math.log2(math.e)

TM_PROJ = 512
ATT_ROWS = 16
DFT_TW = 8
DFT_TN = 4096
MOE_BLK = 512
MOE_CHUNK = 256
TC_COMB = 256
VMEM_LIMIT = 56 * 1024 * 1024


def _cparams(sem):
    return pltpu.CompilerParams(dimension_semantics=sem, vmem_limit_bytes=VMEM_LIMIT)


def _mxu_const(table):
    return jnp.asarray(table, F32).astype(BF16)


@functools.lru_cache(maxsize=None)
def _rope_tables():
    t = np.arange(SEQ)
    row, col = t // GRID_W, t % GRID_W
    lane = np.arange(LANES)
    d = lane % HEAD_DIM
    chunk = d // 32
    e = d % 32
    j = e % 16
    inv = ROPE_THETA ** (-(j.astype(np.float64)) / 16.0)
    pos = np.where(chunk[None, :] == 0, row[:, None], col[:, None]).astype(np.float64)
    ang = pos * inv[None, :]
    cos = np.cos(ang)
    sin = np.sin(ang)
    first = (e < 16)[None, :]
    s_first = np.where(first, -sin, 0.0)
    s_second = np.where(first, 0.0, sin)
    return (cos.astype(np.float32), s_first.astype(np.float32), s_second.astype(np.float32))


@functools.lru_cache(maxsize=None)
def _chan_dft():
    c = np.arange(FOURIER_GROUP)
    ang = 2.0 * np.pi * ((c[:, None] * c[None, :]) % FOURIER_GROUP) / FOURIER_GROUP
    eye = np.eye(D_FOURIER // FOURIER_GROUP)
    re = np.kron(eye, np.cos(ang))
    im = np.kron(eye, -np.sin(ang))
    return np.concatenate([re, im], axis=1).astype(np.float32)


@functools.lru_cache(maxsize=None)
def _row_dft():
    k1 = np.arange(GRID_H)[:, None]
    r = np.arange(GRID_H)[None, :]
    out = np.zeros((GRID_W, 2 * GRID_H, 2 * GRID_H), np.float32)
    for w in range(GRID_W):
        m = (k1 * (GRID_W * r + w)) % SEQ
        ang = 2.0 * np.pi * m / SEQ
        c, s = np.cos(ang), np.sin(ang)
        out[w] = np.block([[c, s], [-s, c]])
    return out


@functools.lru_cache(maxsize=None)
def _col_dft():
    k2 = np.arange(GRID_W)
    ang = 2.0 * np.pi * ((k2[:, None] * k2[None, :]) % GRID_W) / GRID_W
    scale = 1.0 / math.sqrt(SEQ * FOURIER_GROUP)
    return (np.concatenate([np.cos(ang), np.sin(ang)], axis=1) * scale).astype(np.float32)


@functools.lru_cache(maxsize=None)
def _bias_index():
    c = np.arange(GRID_W)
    start = np.clip(c - NA_COLS // 2, 0, GRID_W - NA_COLS)
    valid = (c[None, :] >= start[:, None]) & (c[None, :] < start[:, None] + NA_COLS)
    dc = np.clip(c[None, :] - c[:, None] + (NA_COLS - 1), 0, 2 * NA_COLS - 2)
    return dc.astype(np.int32), valid


@functools.lru_cache(maxsize=None)
def _strict_lower(n):
    return np.tril(np.ones((n, n), np.float32), k=-1)


def _adaln_kernel(c_ref, w_ref, b_ref, o_ref):
    c = c_ref[...]
    s = c * jax.nn.sigmoid(c)
    o_ref[...] = jnp.dot(s, w_ref[...], precision=lax.Precision.HIGHEST,
                         preferred_element_type=F32) + b_ref[...]


def adaln(cond8, w, b):
    n = w.shape[1]
    tn = 1536
    return pl.pallas_call(
        _adaln_kernel,
        grid=(n // tn,),
        in_specs=[pl.BlockSpec((8, D_MODEL), lambda j: (0, 0)),
                  pl.BlockSpec((D_MODEL, tn), lambda j: (0, j)),
                  pl.BlockSpec((1, tn), lambda j: (0, j))],
        out_specs=pl.BlockSpec((8, tn), lambda j: (0, j)),
        out_shape=jax.ShapeDtypeStruct((8, n), F32),
        compiler_params=_cparams(("arbitrary",)),
        name="adaln",
    )(cond8, w, b.reshape(1, n))


def _norm_mod(x, g, shift, scale):
    ms = jnp.mean(x * x, axis=-1, keepdims=True)
    return (x * lax.rsqrt(ms + EPS) * g) * (1.0 + scale) + shift


def _in_proj_kernel(x_ref, mod_ref, g_ref, w_ref, cs_ref, cos_ref, s1_ref, s2_ref,
                    qr_ref, qp_ref, kr_ref, v_ref, a_ref, h_scr):
    m = mod_ref[0]
    h_scr[...] = _norm_mod(x_ref[0], g_ref[...], m[0:1], m[1:2]).astype(BF16)
    cos, s1, s2 = cos_ref[...], s1_ref[...], s2_ref[...]

    def rope(t):
        return (t * cos + pltpu.roll(t, LANES - 16, axis=1) * s1 + pltpu.roll(t, 16, axis=1) * s2)

    f = jnp.dot(h_scr[...], w_ref[:, 0:D_FOURIER], preferred_element_type=F32)
    a = jnp.dot(f.astype(BF16), cs_ref[...], preferred_element_type=F32)
    a_ref[0, 0] = a[:, :D_FOURIER].astype(BF16)
    a_ref[0, 1] = a[:, D_FOURIER:].astype(BF16)

    scale = HEAD_DIM ** -0.5 * LOG2E
    for c in range(D_NA // 256):
        lo = D_FOURIER + 256 * c
        q = jnp.dot(h_scr[...], w_ref[:, lo:lo + 256], preferred_element_type=F32)
        k = jnp.dot(h_scr[...], w_ref[:, lo + D_NA:lo + D_NA + 256], preferred_element_type=F32)
        v = jnp.dot(h_scr[...], w_ref[:, lo + 2 * D_NA:lo + 2 * D_NA + 256], preferred_element_type=F32)
        v_ref[0, :, 256 * c:256 * c + 256] = v.astype(BF16)
        for s in range(2):
            sl = slice(LANES * s, LANES * (s + 1))
            ol = slice(256 * c + LANES * s, 256 * c + LANES * (s + 1))
            qs, ks = q[:, sl], k[:, sl]
            qp_ref[0, :, ol] = (qs * scale).astype(BF16)
            qr_ref[0, :, ol] = (rope(qs) * scale).astype(BF16)
            kr_ref[0, :, ol] = rope(ks).astype(BF16)


def in_proj(x, mods, g1, w_in_bf16):
    b, n, d = x.shape
    tm = TM_PROJ
    cos, s1, s2 = _rope_tables()
    cs = _mxu_const(_chan_dft())
    tok = lambda bi, i: (bi, i, 0)
    const2 = lambda bi, i: (0, 0)
    tab = pl.BlockSpec((tm, LANES), lambda bi, i: (i, 0))
    qkv_shape = jax.ShapeDtypeStruct((b, n, D_NA), BF16)
    qkv_spec = pl.BlockSpec((1, tm, D_NA), tok)
    return pl.pallas_call(
        _in_proj_kernel,
        grid=(b, n // tm),
        in_specs=[pl.BlockSpec((1, tm, d), tok),
                  pl.BlockSpec((1, 8, d), lambda bi, i: (bi, 0, 0)),
                  pl.BlockSpec((1, d), const2),
                  pl.BlockSpec((d, D_IN_PROJ), const2),
                  pl.BlockSpec((D_FOURIER, 2 * D_FOURIER), const2),
                  tab, tab, tab],
        out_specs=[qkv_spec, qkv_spec, qkv_spec, qkv_spec,
                   pl.BlockSpec((1, 2, tm, D_FOURIER), lambda bi, i: (bi, 0, i, 0))],
        out_shape=[qkv_shape, qkv_shape, qkv_shape, qkv_shape,
                   jax.ShapeDtypeStruct((b, 2, n, D_FOURIER), BF16)],
        scratch_shapes=[pltpu.VMEM((tm, d), BF16)],
        compiler_params=_cparams(("arbitrary", "arbitrary")),
        name="in_proj",
    )(x, mods, g1.reshape(1, d), w_in_bf16, cs, jnp.asarray(cos), jnp.asarray(s1), jnp.asarray(s2))


def _ctx_proj_kernel(x_ref, mod_ref, g_ref, w_ref, k_ref, v_ref):
    m = mod_ref[...]
    h = _norm_mod(x_ref[0], g_ref[...], m[0:1], m[1:2]).astype(BF16)
    k_ref[0] = jnp.dot(h, w_ref[:, 0:D_NA], preferred_element_type=F32).astype(BF16)
    v_ref[0] = jnp.dot(h, w_ref[:, D_NA:2 * D_NA], preferred_element_type=F32).astype(BF16)


def ctx_proj(ctx, mod_ctx, g1, w_kv_bf16):
    b, l, d = ctx.shape
    shape = jax.ShapeDtypeStruct((b, l, D_NA), BF16)
    spec = pl.BlockSpec((1, l, D_NA), lambda bi: (bi, 0, 0))
    return pl.pallas_call(
        _ctx_proj_kernel,
        grid=(b,),
        in_specs=[pl.BlockSpec((1, l, d), lambda bi: (bi, 0, 0)),
                  pl.BlockSpec((8, d), lambda bi: (0, 0)),
                  pl.BlockSpec((1, d), lambda bi: (0, 0)),
                  pl.BlockSpec((d, 2 * D_NA), lambda bi: (0, 0))],
        out_specs=[spec, spec],
        out_shape=[shape, shape],
        compiler_params=_cparams(("arbitrary",)),
        name="ctx_proj",
    )(ctx, mod_ctx, g1.reshape(1, d), w_kv_bf16)


def _dft_rows_kernel(a_ref, g_ref, z_ref):
    for j in range(DFT_TW):
        sl = slice(D_FOURIER * j, D_FOURIER * (j + 1))
        rhs = jnp.concatenate([a_ref[0, 0, :, sl], a_ref[0, 1, :, sl]], axis=0)
        z = jnp.dot(g_ref[j], rhs, preferred_element_type=F32)
        z_ref[0, 0, j] = z[:GRID_H].astype(BF16)
        z_ref[0, 1, j] = z[GRID_H:].astype(BF16)


def dft_rows(a):
    b = a.shape[0]
    a4 = a.reshape(b, 2, GRID_H, GRID_W * D_FOURIER)
    g = _mxu_const(_row_dft())
    return pl.pallas_call(
        _dft_rows_kernel,
        grid=(GRID_W // DFT_TW, b),
        in_specs=[pl.BlockSpec((1, 2, GRID_H, DFT_TW * D_FOURIER), lambda j, bi: (bi, 0, 0, j)),
                  pl.BlockSpec((DFT_TW, 2 * GRID_H, 2 * GRID_H), lambda j, bi: (j, 0, 0))],
        out_specs=pl.BlockSpec((1, 2, DFT_TW, GRID_H, D_FOURIER), lambda j, bi: (bi, 0, j, 0, 0)),
        out_shape=jax.ShapeDtypeStruct((b, 2, GRID_W, GRID_H, D_FOURIER), BF16),
        compiler_params=_cparams(("arbitrary", "arbitrary")),
        name="dft_rows",
    )(a4, g)


def _dft_cols_kernel(z_ref, cs_ref, o_ref):
    rhs = jnp.concatenate([z_ref[0, 0], z_ref[0, 1]], axis=0)
    o_ref[0] = jnp.dot(cs_ref[...], rhs, preferred_element_type=F32).astype(BF16)


def dft_cols(z):
    b = z.shape[0]
    ncol = GRID_H * D_FOURIER
    z4 = z.reshape(b, 2, GRID_W, ncol)
    cs = _mxu_const(_col_dft())
    out = pl.pallas_call(
        _dft_cols_kernel,
        grid=(b, ncol // DFT_TN),
        in_specs=[pl.BlockSpec((1, 2, GRID_W, DFT_TN), lambda bi, j: (bi, 0, 0, j)),
                  pl.BlockSpec((GRID_W, 2 * GRID_W), lambda bi, j: (0, 0))],
        out_specs=pl.BlockSpec((1, GRID_W, DFT_TN), lambda bi, j: (bi, 0, j)),
        out_shape=jax.ShapeDtypeStruct((b, GRID_W, ncol), BF16),
        compiler_params=_cparams(("arbitrary", "arbitrary")),
        name="dft_cols",
    )(z4, cs)
    return out.reshape(b, SEQ, D_FOURIER)


WIN_ROWS = NA_ROWS + 2
WIN_KEYS = WIN_ROWS * GRID_W
WIN_TILES = WIN_KEYS // LANES
N_DR = 2 * NA_ROWS + 1
SOFT_ROWS = 32
ATT_PAIRS = ATT_ROWS // 2


def _attention_kernel(qr_ref, qp_ref, k_ref, v_ref, kc_ref, vc_ref, tb_ref, o_ref,
                      sc_scr, pc_scr, s_scr, p_scr):
    rb = pl.program_id(2)
    first2 = lax.broadcasted_iota(jnp.int32, (2 * GRID_W, LANES), 1) < HEAD_DIM
    nt = (((1,), (1,)), ((), ()))

    def window(jp):
        r0 = rb * ATT_ROWS + 2 * jp
        ws = jnp.clip(r0 - NA_ROWS // 2, 0, GRID_H - WIN_ROWS)
        return r0, ws, pl.multiple_of(ws * GRID_W, GRID_W)

    def split_heads(q2):
        z2 = jnp.zeros_like(q2)
        return jnp.concatenate([jnp.where(first2, q2, z2), jnp.where(first2, z2, q2)], axis=0)

    def scores(jp):
        _, _, koff = window(jp)
        qs = slice(jp * 2 * GRID_W, (jp + 1) * 2 * GRID_W)
        kw = k_ref[0, pl.ds(koff, WIN_KEYS), :]
        s_scr[jp] = lax.dot_general(split_heads(qr_ref[0, qs, :]), kw, nt, preferred_element_type=F32)
        sc_scr[jp] = lax.dot_general(split_heads(qp_ref[0, qs, :]), kc_ref[0], nt, preferred_element_type=F32)

    def softmax(jp):
        r0, ws, _ = window(jp)
        for c in range(4):
            hh, i = c // 2, c % 2
            r = r0 + i
            off = jnp.clip(r - NA_ROWS // 2, 0, GRID_H - NA_ROWS) - ws
            sel = []
            for t in range(WIN_TILES):
                v0 = (2 * t >= off) & (2 * t < off + NA_ROWS)
                v1 = (2 * t + 1 >= off) & (2 * t + 1 < off + NA_ROWS)
                kind = jnp.where(v0 & v1, 0, jnp.where(v0, 2, 1))
                sel.append((kind, jnp.where(v0 | v1, ws + 2 * t - r + NA_ROWS, N_DR - 1)))
            for h in range(GRID_W // SOFT_ROWS):
                lo = h * SOFT_ROWS
                rows = slice(c * GRID_W + lo, c * GRID_W + lo + SOFT_ROWS)
                tiles = [s_scr[jp, rows, t * LANES:(t + 1) * LANES]
                         + tb_ref[0, hh, sel[t][0], sel[t][1], lo:lo + SOFT_ROWS, :] for t in range(WIN_TILES)]
                sc = sc_scr[jp, rows, :]
                mt = jnp.maximum(sc[:, :LANES], sc[:, LANES:])
                for tl in tiles:
                    mt = jnp.maximum(mt, tl)
                m = jnp.max(mt, axis=1, keepdims=True)
                pc_scr[jp, rows, :] = jnp.exp2(sc - m).astype(BF16)
                for t, tl in enumerate(tiles):
                    p_scr[jp, rows, t * LANES:(t + 1) * LANES] = jnp.exp2(tl - m).astype(BF16)

    def weighted_values(jp):
        _, _, koff = window(jp)
        vw = jnp.concatenate([v_ref[0, pl.ds(koff, WIN_KEYS), :], jnp.ones((WIN_KEYS, LANES), BF16)], axis=1)
        vc = jnp.concatenate([vc_ref[0], jnp.ones((CTX_LEN, LANES), BF16)], axis=1)
        o = (jnp.dot(p_scr[jp], vw, preferred_element_type=F32)
             + jnp.dot(pc_scr[jp], vc, preferred_element_type=F32))
        oa, ob = o[:2 * GRID_W], o[2 * GRID_W:]
        out = jnp.where(first2, oa[:, :LANES] / oa[:, LANES:], ob[:, :LANES] / ob[:, LANES:])
        o_ref[0, jp * 2 * GRID_W:(jp + 1) * 2 * GRID_W, :] = out.astype(BF16)

    scores(0)
    scores(1)
    softmax(0)
    for jp in range(ATT_PAIRS):
        if jp + 2 < ATT_PAIRS:
            scores(jp + 2)
        if jp + 1 < ATT_PAIRS:
            softmax(jp + 1)
        weighted_values(jp)


def bias_tables(rpb):
    dc, valid = _bias_index()
    n_dc = 2 * NA_COLS - 1
    onehot = (dc.reshape(1, -1) == np.arange(n_dc).reshape(-1, 1)).astype(np.float32)
    t = jnp.dot(rpb.reshape(-1, n_dc), jnp.asarray(onehot), precision=lax.Precision.HIGHEST)
    t = jnp.where(valid[None, None], LOG2E * t.reshape(N_HEADS, 2 * NA_ROWS - 1, GRID_W, GRID_W), NEG)
    t = jnp.pad(t, ((0, 0), (1, 2), (0, 0), (0, 0)), constant_values=NEG)
    left, right = t[:, :N_DR], t[:, 1:N_DR + 1]
    neg = jnp.full_like(left, NEG)
    tb = jnp.stack([jnp.concatenate([left, right], axis=-1),
                    jnp.concatenate([neg, right], axis=-1),
                    jnp.concatenate([left, neg], axis=-1)], axis=1)
    return tb.reshape(N_PAIRS, 2, 3, N_DR, GRID_W, LANES).astype(F32)


def attention(qr, qp, kr, v, kc, vc, tb):
    b, n, _ = qr.shape
    tq = ATT_ROWS * GRID_W
    qspec = pl.BlockSpec((1, tq, LANES), lambda bi, hp, i: (bi, i, hp))
    kspec = pl.BlockSpec((1, n, LANES), lambda bi, hp, i: (bi, 0, hp))
    cspec = pl.BlockSpec((1, CTX_LEN, LANES), lambda bi, hp, i: (bi, 0, hp))
    return pl.pallas_call(
        _attention_kernel,
        grid=(b, N_PAIRS, GRID_H // ATT_ROWS),
        in_specs=[qspec, qspec, kspec, kspec, cspec, cspec,
                  pl.BlockSpec((1, 2, 3, N_DR, GRID_W, LANES), lambda bi, hp, i: (hp, 0, 0, 0, 0, 0))],
        out_specs=qspec,
        out_shape=jax.ShapeDtypeStruct((b, n, D_NA), BF16),
        scratch_shapes=[pltpu.VMEM((ATT_PAIRS, 4 * GRID_W, CTX_LEN), F32),
                        pltpu.VMEM((ATT_PAIRS, 4 * GRID_W, CTX_LEN), BF16),
                        pltpu.VMEM((ATT_PAIRS, 4 * GRID_W, WIN_KEYS), F32),
                        pltpu.VMEM((ATT_PAIRS, 4 * GRID_W, WIN_KEYS), BF16)],
        compiler_params=_cparams(("arbitrary", "arbitrary", "arbitrary")),
        name="attention",
    )(qr, qp, kr, v, kc, vc, tb)


def _rms(x, g):
    ms = jnp.mean(x * x, axis=-1, keepdims=True)
    return x * lax.rsqrt(ms + EPS) * g


TILE_ROWS = D_MODEL // LANES


def _store_token_tiles(ref, val, start=0):
    rows = val.shape[0]
    for j in range(TILE_ROWS):
        ref[pl.ds(start + j, rows, stride=TILE_ROWS), :] = val[:, LANES * j:LANES * (j + 1)]


def _load_token_tiles(ref, start, rows):
    return jnp.concatenate(
        [ref[pl.ds(start + j, rows, stride=TILE_ROWS), :] for j in range(TILE_ROWS)], axis=1)


def _out_proj_kernel(fr_ref, na_ref, x_ref, mod_ref, wf_ref, wo_ref, go_ref, g2_ref, wr_ref, tri_ref,
                     x1_ref, h2_ref, meta_ref, metat_ref, cnt_ref, run_scr):
    @pl.when((pl.program_id(0) == 0) & (pl.program_id(1) == 0))
    def _():
        run_scr[...] = jnp.zeros_like(run_scr)

    m = mod_ref[0]
    go = go_ref[...]
    fo = jnp.dot(fr_ref[0], wf_ref[...], preferred_element_type=F32)
    fn = _rms(fo, go[:, :D_FOURIER]).astype(BF16)
    nn = _rms(na_ref[0].astype(F32), go[:, D_FOURIER:]).astype(BF16)
    y = (jnp.dot(fn, wo_ref[0:D_FOURIER, :], preferred_element_type=F32)
         + jnp.dot(nn, wo_ref[D_FOURIER:, :], preferred_element_type=F32))
    x1 = x_ref[0] + m[2:3] * y
    x1_ref[0] = x1
    h2 = _norm_mod(x1, g2_ref[...], m[3:4], m[4:5])
    _store_token_tiles(h2_ref, h2)
    logits = jnp.dot(h2.astype(BF16), wr_ref[...], preferred_element_type=F32)

    tm = logits.shape[0]
    lane = lax.broadcasted_iota(jnp.int32, (tm, LANES), 1).astype(F32)
    ninf = jnp.float32(-jnp.inf)

    def argmax_first(vals):
        mx = jnp.max(vals, axis=1, keepdims=True)
        idx = jnp.min(jnp.where(vals == mx, lane, float(LANES)), axis=1, keepdims=True)
        return mx, idx

    lg = jnp.where(lane < N_GROUPS, logits, ninf)
    gmax, gidx = argmax_first(lg)
    pg = 1.0 / jnp.sum(jnp.exp(lg - gmax), axis=1, keepdims=True)
    lo = N_GROUPS + EXPERTS_PER_GROUP * gidx
    le = jnp.where((lane >= lo) & (lane < lo + EXPERTS_PER_GROUP), logits, ninf)
    e1, i1 = argmax_first(le)
    e2, i2 = argmax_first(jnp.where(lane == i1, ninf, le))
    dd = jnp.exp(e2 - e1)
    gate1 = pg / (1.0 + dd)
    gate2 = pg * dd / (1.0 + dd)

    hot1 = lane == i1
    hot2 = lane == i2
    onehot = jnp.where(hot1 | hot2, 1.0, 0.0)
    cnt = jnp.dot(tri_ref[...], onehot.astype(BF16), preferred_element_type=F32) + run_scr[...]
    rank1 = jnp.sum(jnp.where(hot1, cnt, 0.0), axis=1, keepdims=True)
    rank2 = jnp.sum(jnp.where(hot2, cnt, 0.0), axis=1, keepdims=True)
    run_scr[...] = run_scr[...] + jnp.sum(onehot, axis=0, keepdims=True)

    meta = jnp.where(lane == 0, i1 - N_GROUPS,
           jnp.where(lane == 1, i2 - N_GROUPS,
           jnp.where(lane == 2, rank1,
           jnp.where(lane == 3, rank2,
           jnp.where(lane == 4, gate1,
           jnp.where(lane == 5, gate2, 0.0))))))
    meta_ref[...] = meta
    metat_ref[...] = jnp.transpose(meta)[0:8, :]
    cnt_ref[...] = jnp.broadcast_to(run_scr[...], cnt_ref.shape)


def out_proj(fr, na, x, mods, w_fmix_bf16, w_out_bf16, g_out, g2, w_router_bf16):
    b, n, d = x.shape
    tm = TM_PROJ
    steps = n // tm
    tok = lambda bi, i: (bi, i, 0)
    const2 = lambda bi, i: (0, 0)
    flat = lambda bi, i: (bi * steps + i, 0)
    tri = _mxu_const(_strict_lower(tm))
    return pl.pallas_call(
        _out_proj_kernel,
        grid=(b, steps),
        in_specs=[pl.BlockSpec((1, tm, D_FOURIER), tok),
                  pl.BlockSpec((1, tm, D_NA), tok),
                  pl.BlockSpec((1, tm, d), tok),
                  pl.BlockSpec((1, 8, d), lambda bi, i: (bi, 0, 0)),
                  pl.BlockSpec((D_FOURIER, D_FOURIER), const2),
                  pl.BlockSpec((d, d), const2),
                  pl.BlockSpec((1, d), const2),
                  pl.BlockSpec((1, d), const2),
                  pl.BlockSpec((d, LANES), const2),
                  pl.BlockSpec((tm, tm), const2)],
        out_specs=[pl.BlockSpec((1, tm, d), tok),
                   pl.BlockSpec((tm * TILE_ROWS, LANES), flat),
                   pl.BlockSpec((tm, LANES), flat),
                   pl.BlockSpec((8, tm), lambda bi, i: (0, bi * steps + i)),
                   pl.BlockSpec((8, LANES), const2)],
        out_shape=[jax.ShapeDtypeStruct((b, n, d), F32),
                   jax.ShapeDtypeStruct((b * n * TILE_ROWS, LANES), F32),
                   jax.ShapeDtypeStruct((b * n, LANES), F32),
                   jax.ShapeDtypeStruct((8, b * n), F32),
                   jax.ShapeDtypeStruct((8, LANES), F32)],
        scratch_shapes=[pltpu.VMEM((1, LANES), F32)],
        compiler_params=_cparams(("arbitrary", "arbitrary")),
        name="out_proj",
    )(fr, na, x, mods, w_fmix_bf16, w_out_bf16, g_out.reshape(1, d), g2.reshape(1, d), w_router_bf16, tri)


DMA_CHUNK = 8


TD_DISP = 512


def _dispatch_kernel(dest_ref, pstart_ref, count_ref, h2_ref, xs_hbm, zero_scr, sem, pad_sem):
    i = pl.program_id(0)
    nt = dest_ref.shape[0] // 2

    @pl.when(i == 0)
    def _():
        zero_scr[...] = jnp.zeros_like(zero_scr)

        def per_expert(e, npad):
            lo = pstart_ref[e] + count_ref[e]
            mid = pstart_ref[e] + ((count_ref[e] + MOE_CHUNK - 1) // MOE_CHUNK) * MOE_CHUNK
            hi = pstart_ref[e] + ((count_ref[e] + MOE_BLK - 1) // MOE_BLK) * MOE_BLK

            def fill(s, carry):
                pltpu.make_async_copy(zero_scr.at[pl.ds(0, TILE_ROWS)],
                                      xs_hbm.at[pl.ds(s * TILE_ROWS, TILE_ROWS)], pad_sem).start()
                return carry

            def fill_chunk(s, carry):
                pltpu.make_async_copy(zero_scr.at[pl.ds(0, MOE_CHUNK * TILE_ROWS)],
                                      xs_hbm.at[pl.ds(mid * TILE_ROWS + s * MOE_CHUNK * TILE_ROWS,
                                                      MOE_CHUNK * TILE_ROWS)], pad_sem).start()
                return carry

            lax.fori_loop(lo, mid, fill, 0)
            lax.fori_loop(0, (hi - mid) // MOE_CHUNK, fill_chunk, 0)
            return npad + (hi - lo)

        npad = lax.fori_loop(0, N_EXPERTS, per_expert, 0)

        blk_rows = MOE_BLK * TILE_ROWS
        first_free = (pstart_ref[N_EXPERTS - 1] + count_ref[N_EXPERTS - 1] + MOE_BLK - 1) // MOE_BLK
        n_blocks = xs_hbm.shape[0] // blk_rows

        def fill_block(bk, carry):
            pltpu.make_async_copy(zero_scr, xs_hbm.at[pl.ds(bk * blk_rows, blk_rows)], pad_sem).start()
            return carry

        lax.fori_loop(first_free, n_blocks, fill_block, 0)
        rows = npad * TILE_ROWS + (n_blocks - first_free) * blk_rows

        @pl.when(rows > 0)
        def _():
            pltpu.make_async_copy(xs_hbm.at[pl.ds(0, rows)], xs_hbm.at[pl.ds(0, rows)], pad_sem).wait()

    def issue(c, carry):
        t0 = c * DMA_CHUNK
        for u in range(DMA_CHUNK):
            for k in range(2):
                d = dest_ref[k * nt + i * TD_DISP + t0 + u]
                pltpu.make_async_copy(h2_ref.at[pl.ds((t0 + u) * TILE_ROWS, TILE_ROWS)],
                                      xs_hbm.at[pl.ds(d * TILE_ROWS, TILE_ROWS)], sem).start(priority=k)
        return carry

    lax.fori_loop(0, TD_DISP // DMA_CHUNK, issue, 0)
    rows = 2 * TD_DISP * TILE_ROWS
    pltpu.make_async_copy(xs_hbm.at[pl.ds(0, rows)], xs_hbm.at[pl.ds(0, rows)], sem).wait()


def dispatch(h2_tiles, dest, pstarts, counts, n_slots):
    nt = dest.shape[0] // 2
    grid_spec = pltpu.PrefetchScalarGridSpec(
        num_scalar_prefetch=3,
        grid=(nt // TD_DISP,),
        in_specs=[pl.BlockSpec((TD_DISP * TILE_ROWS, LANES), lambda i, ds, ps, ct: (i, 0))],
        out_specs=pl.BlockSpec(memory_space=pl.ANY),
        scratch_shapes=[pltpu.VMEM((MOE_BLK * TILE_ROWS, LANES), F32),
                        pltpu.SemaphoreType.DMA(()),
                        pltpu.SemaphoreType.DMA(())],
    )
    return pl.pallas_call(
        _dispatch_kernel,
        grid_spec=grid_spec,
        out_shape=jax.ShapeDtypeStruct((n_slots * TILE_ROWS, LANES), F32),
        compiler_params=_cparams(("arbitrary",)),
        name="dispatch",
    )(dest, pstarts, counts, h2_tiles)


def _experts_kernel(be_ref, nused_ref, valid_ref, xs_ref, wg_ref, wu_ref, wd_ref, ys_ref,
                    wg_scr, wu_scr, wd_scr):
    i = pl.program_id(0)
    valid = valid_ref[i]
    changed = (i == 0) | (be_ref[i] != be_ref[jnp.maximum(i - 1, 0)])
    chunk_rows = MOE_CHUNK * TILE_ROWS

    @pl.when(changed & (valid > 0))
    def _():
        wg_scr[...] = wg_ref[0].astype(BF16)
        wu_scr[...] = wu_ref[0].astype(BF16)
        wd_scr[...] = wd_ref[0].astype(BF16)

    def run(n_chunks):
        hmids = []
        for h in range(n_chunks):
            x = _load_token_tiles(xs_ref, h * chunk_rows, MOE_CHUNK).astype(BF16)
            g = jnp.dot(x, wg_scr[...], preferred_element_type=F32)
            u = jnp.dot(x, wu_scr[...], preferred_element_type=F32)
            hmids.append((g * jax.nn.sigmoid(g) * u).astype(BF16))
        for h, hmid in enumerate(hmids):
            _store_token_tiles(ys_ref, jnp.dot(hmid, wd_scr[...], preferred_element_type=F32), h * chunk_rows)
        if n_chunks * chunk_rows < ys_ref.shape[0]:
            ys_ref[n_chunks * chunk_rows:, :] = jnp.zeros((ys_ref.shape[0] - n_chunks * chunk_rows, LANES), F32)

    for n_chunks in range(MOE_BLK // MOE_CHUNK + 1):
        lo, hi = (n_chunks - 1) * MOE_CHUNK, n_chunks * MOE_CHUNK
        pl.when((valid > lo) & (valid <= hi))(functools.partial(run, n_chunks))


def experts(xs_tiles, block_expert, nused, block_valid, w_gate, w_up, w_down):
    d = D_MODEL
    nblk = block_expert.shape[0]
    blk_rows = MOE_BLK * TILE_ROWS
    wmap = lambda i, be, nu, bv: (be[i], 0, 0)
    grid_spec = pltpu.PrefetchScalarGridSpec(
        num_scalar_prefetch=3,
        grid=(nblk,),
        in_specs=[pl.BlockSpec((blk_rows, LANES), lambda i, be, nu, bv: (jnp.minimum(i, nu[0] - 1), 0)),
                  pl.BlockSpec((1, d, D_EXPERT), wmap),
                  pl.BlockSpec((1, d, D_EXPERT), wmap),
                  pl.BlockSpec((1, D_EXPERT, d), wmap)],
        out_specs=pl.BlockSpec((blk_rows, LANES), lambda i, be, nu, bv: (i, 0)),
        scratch_shapes=[pltpu.VMEM((d, D_EXPERT), BF16),
                        pltpu.VMEM((d, D_EXPERT), BF16),
                        pltpu.VMEM((D_EXPERT, d), BF16)],
    )
    return pl.pallas_call(
        _experts_kernel,
        grid_spec=grid_spec,
        out_shape=jax.ShapeDtypeStruct((nblk * blk_rows, LANES), F32),
        compiler_params=_cparams(("arbitrary",)),
        name="experts",
    )(block_expert, nused, block_valid, xs_tiles, w_gate, w_up, w_down)


def _combine_kernel(dest_ref, ys_hbm, x1_ref, meta_ref, mod_ref, gf_ref, o_ref, ybuf, sem):
    i = pl.program_id(0)
    nstep = pl.num_programs(0)
    tc = TC_COMB
    nt = dest_ref.shape[0] // 2
    half_rows = tc * TILE_ROWS
    buf_rows = 2 * half_rows

    def gather(step, slot):
        def issue(c, carry):
            t0 = c * DMA_CHUNK
            for u in range(DMA_CHUNK):
                for k in range(2):
                    d = dest_ref[k * nt + step * tc + t0 + u]
                    pltpu.make_async_copy(
                        ys_hbm.at[pl.ds(d * TILE_ROWS, TILE_ROWS)],
                        ybuf.at[pl.ds(slot * buf_rows + k * half_rows + (t0 + u) * TILE_ROWS, TILE_ROWS)],
                        sem.at[slot]).start(priority=k)
            return carry

        lax.fori_loop(0, tc // DMA_CHUNK, issue, 0)

    @pl.when(i == 0)
    def _():
        gather(0, 0)

    @pl.when(i + 1 < nstep)
    def _():
        gather(i + 1, (i + 1) % 2)

    slot = i % 2
    start = pl.multiple_of(slot * buf_rows, buf_rows)
    pltpu.make_async_copy(ys_hbm.at[pl.ds(0, buf_rows)], ybuf.at[pl.ds(start, buf_rows)], sem.at[slot]).wait()
    meta = meta_ref[...]
    y0 = _load_token_tiles(ybuf, start, tc)
    y1 = _load_token_tiles(ybuf, start + half_rows, tc)
    moe = y0 * meta[:, 4:5] + y1 * meta[:, 5:6]
    x2 = x1_ref[...] + mod_ref[0][5:6] * moe
    o_ref[...] = _rms(x2, gf_ref[...])


def combine(dest_flat, ys, x1_flat, meta, mods, g_final, n_per_batch):
    nt, d = x1_flat.shape
    tc = TC_COMB
    per_b = n_per_batch // tc
    grid_spec = pltpu.PrefetchScalarGridSpec(
        num_scalar_prefetch=1,
        grid=(nt // tc,),
        in_specs=[pl.BlockSpec(memory_space=pl.ANY),
                  pl.BlockSpec((tc, d), lambda i, ds: (i, 0)),
                  pl.BlockSpec((tc, LANES), lambda i, ds: (i, 0)),
                  pl.BlockSpec((1, 8, d), lambda i, ds: (i // per_b, 0, 0)),
                  pl.BlockSpec((1, d), lambda i, ds: (0, 0))],
        out_specs=pl.BlockSpec((tc, d), lambda i, ds: (i, 0)),
        scratch_shapes=[pltpu.VMEM((2 * 2 * tc * TILE_ROWS, LANES), F32),
                        pltpu.SemaphoreType.DMA((2,))],
    )
    return pl.pallas_call(
        _combine_kernel,
        grid_spec=grid_spec,
        out_shape=jax.ShapeDtypeStruct((nt, d), F32),
        compiler_params=_cparams(("arbitrary",)),
        name="combine",
    )(dest_flat, ys, x1_flat, meta, mods, g_final.reshape(1, d))


def _dispatch_plan(metat, counts_row, nt):
    counts = counts_row[N_GROUPS:N_GROUPS + N_EXPERTS].astype(jnp.int32)
    pcounts = ((counts + MOE_BLK - 1) // MOE_BLK) * MOE_BLK
    pends = jnp.cumsum(pcounts)
    pstarts = pends - pcounts
    eid = metat[0:2].astype(jnp.int32)
    dest = metat[2:4].astype(jnp.int32)
    for e in range(N_EXPERTS):
        dest = dest + jnp.where(eid == e, pstarts[e], 0)
    dest = dest.reshape(-1)
    nblk = (nt * 2) // MOE_BLK + N_EXPERTS
    first_slot = jnp.arange(nblk, dtype=jnp.int32) * MOE_BLK
    block_expert = jnp.minimum(
        jnp.sum((pends[None, :] <= first_slot[:, None]).astype(jnp.int32), axis=1), N_EXPERTS - 1)
    nused = (pends[-1] // MOE_BLK).astype(jnp.int32).reshape(1)
    seg_end = jnp.sum(jnp.where(block_expert[:, None] == jnp.arange(N_EXPERTS)[None, :],
                                (pstarts + counts)[None, :], 0), axis=1)
    block_valid = jnp.where(first_slot < pends[-1], jnp.clip(seg_end - first_slot, 0, MOE_BLK), 0)
    return (dest, pstarts.astype(jnp.int32), counts, block_expert.astype(jnp.int32), nused,
            block_valid.astype(jnp.int32))


def kernel(x, c, ctx, c_ctx, w_ada, b_ada, g_norm1, w_in, w_fmix, rpb, g_out, w_out, g_norm2,
           w_router_group, w_router_expert, w_gate, w_up, w_down, g_final):
    b, n, d = x.shape
    assert (b, n, d) == (c.shape[0], SEQ, D_MODEL) and w_ada.shape[0] == 1
    nt = b * n

    cond8 = jnp.zeros((8, d), F32).at[0:b].set(c).at[b].set(c_ctx)
    mod = adaln(cond8, w_ada[0], b_ada[0])
    mods = jnp.pad(mod[0:b].reshape(b, N_MOD, d), ((0, 0), (0, 2), (0, 0)))
    mod_ctx = jnp.pad(mod[b].reshape(N_MOD, d), ((0, 2), (0, 0)))

    w_in_b = w_in[0].astype(BF16)
    qr, qp, kr, v, a = in_proj(x, mods, g_norm1[0], w_in_b)
    kc, vc = ctx_proj(ctx, mod_ctx, g_norm1[0], w_in_b[:, D_FOURIER + D_NA:])

    fr = dft_cols(dft_rows(a))
    na = attention(qr, qp, kr, v, kc, vc, bias_tables(rpb[0]))

    w_router = jnp.concatenate(
        [w_router_group[0], w_router_expert[0],
         jnp.zeros((d, LANES - N_GROUPS - N_EXPERTS), F32)], axis=1).astype(BF16)
    x1, h2_tiles, meta, metat, cnt = out_proj(fr, na, x, mods, w_fmix[0].astype(BF16),
                                              w_out[0].astype(BF16), g_out[0], g_norm2[0], w_router)

    dest, pstarts, counts, block_expert, nused, block_valid = _dispatch_plan(metat, cnt[0], nt)
    xs_tiles = dispatch(h2_tiles, dest, pstarts, counts, block_expert.shape[0] * MOE_BLK)
    ys_tiles = experts(xs_tiles, block_expert, nused, block_valid, w_gate[0], w_up[0], w_down[0])
    out = combine(dest, ys_tiles, x1.reshape(nt, d), meta, mods, g_final, n)
    return out.reshape(b, n, d)
```

```python
import functools
import math

import numpy as np
import jax
import jax.numpy as jnp
from jax import lax
from jax.experimental import pallas as pl
from jax.experimental.pallas import tpu as pltpu

F32 = jnp.float32
BF16 = jnp.bfloat16

D_MODEL = 1024
GRID_W = 64
GRID_H = 128
SEQ = GRID_W * GRID_H
CTX_LEN = 256
D_FOURIER = 256
FOURIER_GROUP = 64
HEAD_DIM = 64
N_HEADS = 12
D_NA = N_HEADS * HEAD_DIM
N_PAIRS = N_HEADS // 2
NA_ROWS = 8
NA_COLS = 16
ROPE_THETA = 10000.0
N_GROUPS = 4
EXPERTS_PER_GROUP = 8
N_EXPERTS = N_GROUPS * EXPERTS_PER_GROUP
D_EXPERT = 512
N_MOD = 6
D_IN_PROJ = D_FOURIER + 3 * D_NA
EPS = 1e-6
LANES = 128
NEG = -1e30
LOG2E = math.log2(math.e)

TM_PROJ = 512
ATT_ROWS = 32
DFT_TW = 8
DFT_TN = 4096
MOE_BLK = 512
MOE_CHUNK = 256
TC_COMB = 256
VMEM_LIMIT = 56 * 1024 * 1024


def _cparams(sem):
    return pltpu.CompilerParams(dimension_semantics=sem, vmem_limit_bytes=VMEM_LIMIT)


def _mxu_const(table):
    return jnp.asarray(table, F32).astype(BF16)


@functools.lru_cache(maxsize=None)
def _rope_tables():
    t = np.arange(SEQ)
    row, col = t // GRID_W, t % GRID_W
    lane = np.arange(LANES)
    d = lane % HEAD_DIM
    chunk = d // 32
    e = d % 32
    j = e % 16
    inv = ROPE_THETA ** (-(j.astype(np.float64)) / 16.0)
    pos = np.where(chunk[None, :] == 0, row[:, None], col[:, None]).astype(np.float64)
    ang = pos * inv[None, :]
    cos = np.cos(ang)
    sin = np.sin(ang)
    first = (e < 16)[None, :]
    s_first = np.where(first, -sin, 0.0)
    s_second = np.where(first, 0.0, sin)
    return (cos.astype(np.float32), s_first.astype(np.float32), s_second.astype(np.float32))


@functools.lru_cache(maxsize=None)
def _chan_dft():
    c = np.arange(FOURIER_GROUP)
    ang = 2.0 * np.pi * ((c[:, None] * c[None, :]) % FOURIER_GROUP) / FOURIER_GROUP
    eye = np.eye(D_FOURIER // FOURIER_GROUP)
    re = np.kron(eye, np.cos(ang))
    im = np.kron(eye, -np.sin(ang))
    return np.concatenate([re, im], axis=1).astype(np.float32)


@functools.lru_cache(maxsize=None)
def _row_dft():
    k1 = np.arange(GRID_H)[:, None]
    r = np.arange(GRID_H)[None, :]
    out = np.zeros((GRID_W, 2 * GRID_H, 2 * GRID_H), np.float32)
    for w in range(GRID_W):
        m = (k1 * (GRID_W * r + w)) % SEQ
        ang = 2.0 * np.pi * m / SEQ
        c, s = np.cos(ang), np.sin(ang)
        out[w] = np.block([[c, s], [-s, c]])
    return out


@functools.lru_cache(maxsize=None)
def _col_dft():
    k2 = np.arange(GRID_W)
    ang = 2.0 * np.pi * ((k2[:, None] * k2[None, :]) % GRID_W) / GRID_W
    scale = 1.0 / math.sqrt(SEQ * FOURIER_GROUP)
    return (np.concatenate([np.cos(ang), np.sin(ang)], axis=1) * scale).astype(np.float32)


@functools.lru_cache(maxsize=None)
def _bias_index():
    c = np.arange(GRID_W)
    start = np.clip(c - NA_COLS // 2, 0, GRID_W - NA_COLS)
    valid = (c[None, :] >= start[:, None]) & (c[None, :] < start[:, None] + NA_COLS)
    dc = np.clip(c[None, :] - c[:, None] + (NA_COLS - 1), 0, 2 * NA_COLS - 2)
    return dc.astype(np.int32), valid


@functools.lru_cache(maxsize=None)
def _strict_lower(n):
    return np.tril(np.ones((n, n), np.float32), k=-1)


def _adaln_kernel(c_ref, w_ref, b_ref, o_ref):
    c = c_ref[...]
    s = c * jax.nn.sigmoid(c)
    o_ref[...] = jnp.dot(s, w_ref[...], precision=lax.Precision.HIGHEST,
                         preferred_element_type=F32) + b_ref[...]


def adaln(cond8, w, b):
    n = w.shape[1]
    tn = 1536
    return pl.pallas_call(
        _adaln_kernel,
        grid=(n // tn,),
        in_specs=[pl.BlockSpec((8, D_MODEL), lambda j: (0, 0)),
                  pl.BlockSpec((D_MODEL, tn), lambda j: (0, j)),
                  pl.BlockSpec((1, tn), lambda j: (0, j))],
        out_specs=pl.BlockSpec((8, tn), lambda j: (0, j)),
        out_shape=jax.ShapeDtypeStruct((8, n), F32),
        compiler_params=_cparams(("arbitrary",)),
        name="adaln",
    )(cond8, w, b.reshape(1, n))


def _norm_mod(x, g, shift, scale):
    ms = jnp.mean(x * x, axis=-1, keepdims=True)
    return (x * lax.rsqrt(ms + EPS) * g) * (1.0 + scale) + shift


def _in_proj_kernel(x_ref, mod_ref, g_ref, w_ref, cs_ref, cos_ref, s1_ref, s2_ref,
                    qr_ref, qp_ref, kr_ref, v_ref, a_ref, h_scr):
    m = mod_ref[0]
    h_scr[...] = _norm_mod(x_ref[0], g_ref[...], m[0:1], m[1:2]).astype(BF16)
    cos, s1, s2 = cos_ref[...], s1_ref[...], s2_ref[...]

    def rope(t):
        return (t * cos + pltpu.roll(t, LANES - 16, axis=1) * s1 + pltpu.roll(t, 16, axis=1) * s2)

    f = jnp.dot(h_scr[...], w_ref[:, 0:D_FOURIER], preferred_element_type=F32)
    a = jnp.dot(f.astype(BF16), cs_ref[...], preferred_element_type=F32)
    a_ref[0, 0] = a[:, :D_FOURIER].astype(BF16)
    a_ref[0, 1] = a[:, D_FOURIER:].astype(BF16)

    scale = HEAD_DIM ** -0.5 * LOG2E
    for c in range(D_NA // 256):
        lo = D_FOURIER + 256 * c
        q = jnp.dot(h_scr[...], w_ref[:, lo:lo + 256], preferred_element_type=F32)
        k = jnp.dot(h_scr[...], w_ref[:, lo + D_NA:lo + D_NA + 256], preferred_element_type=F32)
        v = jnp.dot(h_scr[...], w_ref[:, lo + 2 * D_NA:lo + 2 * D_NA + 256], preferred_element_type=F32)
        v_ref[0, :, 256 * c:256 * c + 256] = v.astype(BF16)
        for s in range(2):
            sl = slice(LANES * s, LANES * (s + 1))
            ol = slice(256 * c + LANES * s, 256 * c + LANES * (s + 1))
            qs, ks = q[:, sl], k[:, sl]
            qp_ref[0, :, ol] = (qs * scale).astype(BF16)
            qr_ref[0, :, ol] = (rope(qs) * scale).astype(BF16)
            kr_ref[0, :, ol] = rope(ks).astype(BF16)


def in_proj(x, mods, g1, w_in_bf16):
    b, n, d = x.shape
    tm = TM_PROJ
    cos, s1, s2 = _rope_tables()
    cs = _mxu_const(_chan_dft())
    tok = lambda bi, i: (bi, i, 0)
    const2 = lambda bi, i: (0, 0)
    tab = pl.BlockSpec((tm, LANES), lambda bi, i: (i, 0))
    qkv_shape = jax.ShapeDtypeStruct((b, n, D_NA), BF16)
    qkv_spec = pl.BlockSpec((1, tm, D_NA), tok)
    return pl.pallas_call(
        _in_proj_kernel,
        grid=(b, n // tm),
        in_specs=[pl.BlockSpec((1, tm, d), tok),
                  pl.BlockSpec((1, 8, d), lambda bi, i: (bi, 0, 0)),
                  pl.BlockSpec((1, d), const2),
                  pl.BlockSpec((d, D_IN_PROJ), const2),
                  pl.BlockSpec((D_FOURIER, 2 * D_FOURIER), const2),
                  tab, tab, tab],
        out_specs=[qkv_spec, qkv_spec, qkv_spec, qkv_spec,
                   pl.BlockSpec((1, 2, tm, D_FOURIER), lambda bi, i: (bi, 0, i, 0))],
        out_shape=[qkv_shape, qkv_shape, qkv_shape, qkv_shape,
                   jax.ShapeDtypeStruct((b, 2, n, D_FOURIER), BF16)],
        scratch_shapes=[pltpu.VMEM((tm, d), BF16)],
        compiler_params=_cparams(("arbitrary", "arbitrary")),
        name="in_proj",
    )(x, mods, g1.reshape(1, d), w_in_bf16, cs, jnp.asarray(cos), jnp.asarray(s1), jnp.asarray(s2))


def _ctx_proj_kernel(x_ref, mod_ref, g_ref, w_ref, k_ref, v_ref):
    m = mod_ref[...]
    h = _norm_mod(x_ref[0], g_ref[...], m[0:1], m[1:2]).astype(BF16)
    k_ref[0] = jnp.dot(h, w_ref[:, 0:D_NA], preferred_element_type=F32).astype(BF16)
    v_ref[0] = jnp.dot(h, w_ref[:, D_NA:2 * D_NA], preferred_element_type=F32).astype(BF16)


def ctx_proj(ctx, mod_ctx, g1, w_kv_bf16):
    b, l, d = ctx.shape
    shape = jax.ShapeDtypeStruct((b, l, D_NA), BF16)
    spec = pl.BlockSpec((1, l, D_NA), lambda bi: (bi, 0, 0))
    return pl.pallas_call(
        _ctx_proj_kernel,
        grid=(b,),
        in_specs=[pl.BlockSpec((1, l, d), lambda bi: (bi, 0, 0)),
                  pl.BlockSpec((8, d), lambda bi: (0, 0)),
                  pl.BlockSpec((1, d), lambda bi: (0, 0)),
                  pl.BlockSpec((d, 2 * D_NA), lambda bi: (0, 0))],
        out_specs=[spec, spec],
        out_shape=[shape, shape],
        compiler_params=_cparams(("arbitrary",)),
        name="ctx_proj",
    )(ctx, mod_ctx, g1.reshape(1, d), w_kv_bf16)


def _dft_rows_kernel(a_ref, g_ref, z_ref):
    for j in range(DFT_TW):
        sl = slice(D_FOURIER * j, D_FOURIER * (j + 1))
        rhs = jnp.concatenate([a_ref[0, 0, :, sl], a_ref[0, 1, :, sl]], axis=0)
        z = jnp.dot(g_ref[j], rhs, preferred_element_type=F32)
        z_ref[0, 0, j] = z[:GRID_H].astype(BF16)
        z_ref[0, 1, j] = z[GRID_H:].astype(BF16)


def dft_rows(a):
    b = a.shape[0]
    a4 = a.reshape(b, 2, GRID_H, GRID_W * D_FOURIER)
    g = _mxu_const(_row_dft())
    return pl.pallas_call(
        _dft_rows_kernel,
        grid=(GRID_W // DFT_TW, b),
        in_specs=[pl.BlockSpec((1, 2, GRID_H, DFT_TW * D_FOURIER), lambda j, bi: (bi, 0, 0, j)),
                  pl.BlockSpec((DFT_TW, 2 * GRID_H, 2 * GRID_H), lambda j, bi: (j, 0, 0))],
        out_specs=pl.BlockSpec((1, 2, DFT_TW, GRID_H, D_FOURIER), lambda j, bi: (bi, 0, j, 0, 0)),
        out_shape=jax.ShapeDtypeStruct((b, 2, GRID_W, GRID_H, D_FOURIER), BF16),
        compiler_params=_cparams(("arbitrary", "arbitrary")),
        name="dft_rows",
    )(a4, g)


def _dft_cols_kernel(z_ref, cs_ref, o_ref):
    rhs = jnp.concatenate([z_ref[0, 0], z_ref[0, 1]], axis=0)
    o_ref[0] = jnp.dot(cs_ref[...], rhs, preferred_element_type=F32).astype(BF16)


def dft_cols(z):
    b = z.shape[0]
    ncol = GRID_H * D_FOURIER
    z4 = z.reshape(b, 2, GRID_W, ncol)
    cs = _mxu_const(_col_dft())
    out = pl.pallas_call(
        _dft_cols_kernel,
        grid=(b, ncol // DFT_TN),
        in_specs=[pl.BlockSpec((1, 2, GRID_W, DFT_TN), lambda bi, j: (bi, 0, 0, j)),
                  pl.BlockSpec((GRID_W, 2 * GRID_W), lambda bi, j: (0, 0))],
        out_specs=pl.BlockSpec((1, GRID_W, DFT_TN), lambda bi, j: (bi, 0, j)),
        out_shape=jax.ShapeDtypeStruct((b, GRID_W, ncol), BF16),
        compiler_params=_cparams(("arbitrary", "arbitrary")),
        name="dft_cols",
    )(z4, cs)
    return out.reshape(b, SEQ, D_FOURIER)


WIN_ROWS = NA_ROWS + 2
WIN_KEYS = WIN_ROWS * GRID_W
WIN_TILES = WIN_KEYS // LANES
N_DR = 2 * NA_ROWS + 1
SOFT_ROWS = 32
ATT_PAIRS = ATT_ROWS // 2


def _attention_kernel(qr_ref, qp_ref, k_ref, v_ref, kc_ref, vc_ref, tb_ref, o_ref,
                      sc_scr, pc_scr, s_scr, p_scr):
    rb = pl.program_id(2)
    first2 = lax.broadcasted_iota(jnp.int32, (2 * GRID_W, LANES), 1) < HEAD_DIM
    nt = (((1,), (1,)), ((), ()))

    def window(jp):
        r0 = rb * ATT_ROWS + 2 * jp
        ws = jnp.clip(r0 - NA_ROWS // 2, 0, GRID_H - WIN_ROWS)
        return r0, ws, pl.multiple_of(ws * GRID_W, GRID_W)

    def split_heads(q2):
        z2 = jnp.zeros_like(q2)
        return jnp.concatenate([jnp.where(first2, q2, z2), jnp.where(first2, z2, q2)], axis=0)

    def scores(jp):
        _, _, koff = window(jp)
        qs = slice(jp * 2 * GRID_W, (jp + 1) * 2 * GRID_W)
        kw = k_ref[0, pl.ds(koff, WIN_KEYS), :]
        s_scr[jp] = lax.dot_general(split_heads(qr_ref[0, qs, :]), kw, nt, preferred_element_type=F32)
        sc_scr[jp] = lax.dot_general(split_heads(qp_ref[0, qs, :]), kc_ref[0], nt, preferred_element_type=F32)

    def softmax(jp):
        r0, ws, _ = window(jp)
        for c in range(4):
            hh, i = c // 2, c % 2
            r = r0 + i
            off = jnp.clip(r - NA_ROWS // 2, 0, GRID_H - NA_ROWS) - ws
            sel = []
            for t in range(WIN_TILES):
                v0 = (2 * t >= off) & (2 * t < off + NA_ROWS)
                v1 = (2 * t + 1 >= off) & (2 * t + 1 < off + NA_ROWS)
                kind = jnp.where(v0 & v1, 0, jnp.where(v0, 2, 1))
                sel.append((kind, jnp.where(v0 | v1, ws + 2 * t - r + NA_ROWS, N_DR - 1)))
            for h in range(GRID_W // SOFT_ROWS):
                lo = h * SOFT_ROWS
                rows = slice(c * GRID_W + lo, c * GRID_W + lo + SOFT_ROWS)
                tiles = [s_scr[jp, rows, t * LANES:(t + 1) * LANES]
                         + tb_ref[0, hh, sel[t][0], sel[t][1], lo:lo + SOFT_ROWS, :] for t in range(WIN_TILES)]
                sc = sc_scr[jp, rows, :]
                mt = jnp.maximum(sc[:, :LANES], sc[:, LANES:])
                for tl in tiles:
                    mt = jnp.maximum(mt, tl)
                m = jnp.max(mt, axis=1, keepdims=True)
                pc_scr[jp, rows, :] = jnp.exp2(sc - m).astype(BF16)
                for t, tl in enumerate(tiles):
                    p_scr[jp, rows, t * LANES:(t + 1) * LANES] = jnp.exp2(tl - m).astype(BF16)

    def weighted_values(jp):
        _, _, koff = window(jp)
        vw = jnp.concatenate([v_ref[0, pl.ds(koff, WIN_KEYS), :], jnp.ones((WIN_KEYS, LANES), BF16)], axis=1)
        vc = jnp.concatenate([vc_ref[0], jnp.ones((CTX_LEN, LANES), BF16)], axis=1)
        o = (jnp.dot(p_scr[jp], vw, preferred_element_type=F32)
             + jnp.dot(pc_scr[jp], vc, preferred_element_type=F32))
        oa, ob = o[:2 * GRID_W], o[2 * GRID_W:]
        out = jnp.where(first2, oa[:, :LANES] / oa[:, LANES:], ob[:, :LANES] / ob[:, LANES:])
        o_ref[0, jp * 2 * GRID_W:(jp + 1) * 2 * GRID_W, :] = out.astype(BF16)

    scores(0)
    scores(1)
    softmax(0)
    for jp in range(ATT_PAIRS):
        if jp + 2 < ATT_PAIRS:
            scores(jp + 2)
        if jp + 1 < ATT_PAIRS:
            softmax(jp + 1)
        weighted_values(jp)


def bias_tables(rpb):
    dc, valid = _bias_index()
    n_dc = 2 * NA_COLS - 1
    onehot = (dc.reshape(1, -1) == np.arange(n_dc).reshape(-1, 1)).astype(np.float32)
    t = jnp.dot(rpb.reshape(-1, n_dc), jnp.asarray(onehot), precision=lax.Precision.HIGHEST)
    t = jnp.where(valid[None, None], LOG2E * t.reshape(N_HEADS, 2 * NA_ROWS - 1, GRID_W, GRID_W), NEG)
    t = jnp.pad(t, ((0, 0), (1, 2), (0, 0), (0, 0)), constant_values=NEG)
    left, right = t[:, :N_DR], t[:, 1:N_DR + 1]
    neg = jnp.full_like(left, NEG)
    tb = jnp.stack([jnp.concatenate([left, right], axis=-1),
                    jnp.concatenate([neg, right], axis=-1),
                    jnp.concatenate([left, neg], axis=-1)], axis=1)
    return tb.reshape(N_PAIRS, 2, 3, N_DR, GRID_W, LANES).astype(F32)


def attention(qr, qp, kr, v, kc, vc, tb):
    b, n, _ = qr.shape
    tq = ATT_ROWS * GRID_W
    qspec = pl.BlockSpec((1, tq, LANES), lambda bi, hp, i: (bi, i, hp))
    kspec = pl.BlockSpec((1, n, LANES), lambda bi, hp, i: (bi, 0, hp))
    cspec = pl.BlockSpec((1, CTX_LEN, LANES), lambda bi, hp, i: (bi, 0, hp))
    return pl.pallas_call(
        _attention_kernel,
        grid=(b, N_PAIRS, GRID_H // ATT_ROWS),
        in_specs=[qspec, qspec, kspec, kspec, cspec, cspec,
                  pl.BlockSpec((1, 2, 3, N_DR, GRID_W, LANES), lambda bi, hp, i: (hp, 0, 0, 0, 0, 0))],
        out_specs=qspec,
        out_shape=jax.ShapeDtypeStruct((b, n, D_NA), BF16),
        scratch_shapes=[pltpu.VMEM((ATT_PAIRS, 4 * GRID_W, CTX_LEN), F32),
                        pltpu.VMEM((ATT_PAIRS, 4 * GRID_W, CTX_LEN), BF16),
                        pltpu.VMEM((ATT_PAIRS, 4 * GRID_W, WIN_KEYS), F32),
                        pltpu.VMEM((ATT_PAIRS, 4 * GRID_W, WIN_KEYS), BF16)],
        compiler_params=_cparams(("arbitrary", "arbitrary", "arbitrary")),
        name="attention",
    )(qr, qp, kr, v, kc, vc, tb)


def _rms(x, g):
    ms = jnp.mean(x * x, axis=-1, keepdims=True)
    return x * lax.rsqrt(ms + EPS) * g


TILE_ROWS = D_MODEL // LANES


def _store_token_tiles(ref, val, start=0):
    rows = val.shape[0]
    for j in range(TILE_ROWS):
        ref[pl.ds(start + j, rows, stride=TILE_ROWS), :] = val[:, LANES * j:LANES * (j + 1)]


def _load_token_tiles(ref, start, rows):
    return jnp.concatenate(
        [ref[pl.ds(start + j, rows, stride=TILE_ROWS), :] for j in range(TILE_ROWS)], axis=1)


def _out_proj_kernel(fr_ref, na_ref, x_ref, mod_ref, wf_ref, wo_ref, go_ref, g2_ref, wr_ref, tri_ref,
                     x1_ref, h2_ref, meta_ref, metat_ref, cnt_ref, run_scr):
    @pl.when((pl.program_id(0) == 0) & (pl.program_id(1) == 0))
    def _():
        run_scr[...] = jnp.zeros_like(run_scr)

    m = mod_ref[0]
    go = go_ref[...]
    fo = jnp.dot(fr_ref[0], wf_ref[...], preferred_element_type=F32)
    fn = _rms(fo, go[:, :D_FOURIER]).astype(BF16)
    nn = _rms(na_ref[0].astype(F32), go[:, D_FOURIER:]).astype(BF16)
    y = (jnp.dot(fn, wo_ref[0:D_FOURIER, :], preferred_element_type=F32)
         + jnp.dot(nn, wo_ref[D_FOURIER:, :], preferred_element_type=F32))
    x1 = x_ref[0] + m[2:3] * y
    x1_ref[0] = x1
    h2 = _norm_mod(x1, g2_ref[...], m[3:4], m[4:5])
    _store_token_tiles(h2_ref, h2)
    logits = jnp.dot(h2.astype(BF16), wr_ref[...], preferred_element_type=F32)

    tm = logits.shape[0]
    lane = lax.broadcasted_iota(jnp.int32, (tm, LANES), 1).astype(F32)
    ninf = jnp.float32(-jnp.inf)

    def argmax_first(vals):
        mx = jnp.max(vals, axis=1, keepdims=True)
        idx = jnp.min(jnp.where(vals == mx, lane, float(LANES)), axis=1, keepdims=True)
        return mx, idx

    lg = jnp.where(lane < N_GROUPS, logits, ninf)
    gmax, gidx = argmax_first(lg)
    pg = 1.0 / jnp.sum(jnp.exp(lg - gmax), axis=1, keepdims=True)
    lo = N_GROUPS + EXPERTS_PER_GROUP * gidx
    le = jnp.where((lane >= lo) & (lane < lo + EXPERTS_PER_GROUP), logits, ninf)
    e1, i1 = argmax_first(le)
    e2, i2 = argmax_first(jnp.where(lane == i1, ninf, le))
    dd = jnp.exp(e2 - e1)
    gate1 = pg / (1.0 + dd)
    gate2 = pg * dd / (1.0 + dd)

    hot1 = lane == i1
    hot2 = lane == i2
    onehot = jnp.where(hot1 | hot2, 1.0, 0.0)
    cnt = jnp.dot(tri_ref[...], onehot.astype(BF16), preferred_element_type=F32) + run_scr[...]
    rank1 = jnp.sum(jnp.where(hot1, cnt, 0.0), axis=1, keepdims=True)
    rank2 = jnp.sum(jnp.where(hot2, cnt, 0.0), axis=1, keepdims=True)
    run_scr[...] = run_scr[...] + jnp.sum(onehot, axis=0, keepdims=True)

    meta = jnp.where(lane == 0, i1 - N_GROUPS,
           jnp.where(lane == 1, i2 - N_GROUPS,
           jnp.where(lane == 2, rank1,
           jnp.where(lane == 3, rank2,
           jnp.where(lane == 4, gate1,
           jnp.where(lane == 5, gate2, 0.0))))))
    meta_ref[...] = meta
    metat_ref[...] = jnp.transpose(meta)[0:8, :]
    cnt_ref[...] = jnp.broadcast_to(run_scr[...], cnt_ref.shape)


def out_proj(fr, na, x, mods, w_fmix_bf16, w_out_bf16, g_out, g2, w_router_bf16):
    b, n, d = x.shape
    tm = TM_PROJ
    steps = n // tm
    tok = lambda bi, i: (bi, i, 0)
    const2 = lambda bi, i: (0, 0)
    flat = lambda bi, i: (bi * steps + i, 0)
    tri = _mxu_const(_strict_lower(tm))
    return pl.pallas_call(
        _out_proj_kernel,
        grid=(b, steps),
        in_specs=[pl.BlockSpec((1, tm, D_FOURIER), tok),
                  pl.BlockSpec((1, tm, D_NA), tok),
                  pl.BlockSpec((1, tm, d), tok),
                  pl.BlockSpec((1, 8, d), lambda bi, i: (bi, 0, 0)),
                  pl.BlockSpec((D_FOURIER, D_FOURIER), const2),
                  pl.BlockSpec((d, d), const2),
                  pl.BlockSpec((1, d), const2),
                  pl.BlockSpec((1, d), const2),
                  pl.BlockSpec((d, LANES), const2),
                  pl.BlockSpec((tm, tm), const2)],
        out_specs=[pl.BlockSpec((1, tm, d), tok),
                   pl.BlockSpec((tm * TILE_ROWS, LANES), flat),
                   pl.BlockSpec((tm, LANES), flat),
                   pl.BlockSpec((8, tm), lambda bi, i: (0, bi * steps + i)),
                   pl.BlockSpec((8, LANES), const2)],
        out_shape=[jax.ShapeDtypeStruct((b, n, d), F32),
                   jax.ShapeDtypeStruct((b * n * TILE_ROWS, LANES), F32),
                   jax.ShapeDtypeStruct((b * n, LANES), F32),
                   jax.ShapeDtypeStruct((8, b * n), F32),
                   jax.ShapeDtypeStruct((8, LANES), F32)],
        scratch_shapes=[pltpu.VMEM((1, LANES), F32)],
        compiler_params=_cparams(("arbitrary", "arbitrary")),
        name="out_proj",
    )(fr, na, x, mods, w_fmix_bf16, w_out_bf16, g_out.reshape(1, d), g2.reshape(1, d), w_router_bf16, tri)


DMA_CHUNK = 8


TD_DISP = 1024


def _dispatch_kernel(dest_ref, pstart_ref, count_ref, h2_ref, xs_hbm, zero_scr, sem, pad_sem):
    i = pl.program_id(0)
    nt = dest_ref.shape[0] // 2

    @pl.when(i == 0)
    def _():
        zero_scr[...] = jnp.zeros_like(zero_scr)

        def per_expert(e, npad):
            lo = pstart_ref[e] + count_ref[e]
            mid = pstart_ref[e] + ((count_ref[e] + MOE_CHUNK - 1) // MOE_CHUNK) * MOE_CHUNK
            hi = pstart_ref[e] + ((count_ref[e] + MOE_BLK - 1) // MOE_BLK) * MOE_BLK

            def fill(s, carry):
                pltpu.make_async_copy(zero_scr.at[pl.ds(0, TILE_ROWS)],
                                      xs_hbm.at[pl.ds(s * TILE_ROWS, TILE_ROWS)], pad_sem).start()
                return carry

            def fill_chunk(s, carry):
                pltpu.make_async_copy(zero_scr.at[pl.ds(0, MOE_CHUNK * TILE_ROWS)],
                                      xs_hbm.at[pl.ds(mid * TILE_ROWS + s * MOE_CHUNK * TILE_ROWS,
                                                      MOE_CHUNK * TILE_ROWS)], pad_sem).start()
                return carry

            lax.fori_loop(lo, mid, fill, 0)
            lax.fori_loop(0, (hi - mid) // MOE_CHUNK, fill_chunk, 0)
            return npad + (hi - lo)

        npad = lax.fori_loop(0, N_EXPERTS, per_expert, 0)

        blk_rows = MOE_BLK * TILE_ROWS
        first_free = (pstart_ref[N_EXPERTS - 1] + count_ref[N_EXPERTS - 1] + MOE_BLK - 1) // MOE_BLK
        n_blocks = xs_hbm.shape[0] // blk_rows

        def fill_block(bk, carry):
            pltpu.make_async_copy(zero_scr, xs_hbm.at[pl.ds(bk * blk_rows, blk_rows)], pad_sem).start()
            return carry

        lax.fori_loop(first_free, n_blocks, fill_block, 0)
        rows = npad * TILE_ROWS + (n_blocks - first_free) * blk_rows

        @pl.when(rows > 0)
        def _():
            pltpu.make_async_copy(xs_hbm.at[pl.ds(0, rows)], xs_hbm.at[pl.ds(0, rows)], pad_sem).wait()

    def issue(c, carry):
        t0 = c * DMA_CHUNK
        for u in range(DMA_CHUNK):
            for k in range(2):
                d = dest_ref[k * nt + i * TD_DISP + t0 + u]
                pltpu.make_async_copy(h2_ref.at[pl.ds((t0 + u) * TILE_ROWS, TILE_ROWS)],
                                      xs_hbm.at[pl.ds(d * TILE_ROWS, TILE_ROWS)], sem).start(priority=k)
        return carry

    lax.fori_loop(0, TD_DISP // DMA_CHUNK, issue, 0)
    rows = 2 * TD_DISP * TILE_ROWS
    pltpu.make_async_copy(xs_hbm.at[pl.ds(0, rows)], xs_hbm.at[pl.ds(0, rows)], sem).wait()


def dispatch(h2_tiles, dest, pstarts, counts, n_slots):
    nt = dest.shape[0] // 2
    grid_spec = pltpu.PrefetchScalarGridSpec(
        num_scalar_prefetch=3,
        grid=(nt // TD_DISP,),
        in_specs=[pl.BlockSpec((TD_DISP * TILE_ROWS, LANES), lambda i, ds, ps, ct: (i, 0))],
        out_specs=pl.BlockSpec(memory_space=pl.ANY),
        scratch_shapes=[pltpu.VMEM((MOE_BLK * TILE_ROWS, LANES), F32),
                        pltpu.SemaphoreType.DMA(()),
                        pltpu.SemaphoreType.DMA(())],
    )
    return pl.pallas_call(
        _dispatch_kernel,
        grid_spec=grid_spec,
        out_shape=jax.ShapeDtypeStruct((n_slots * TILE_ROWS, LANES), F32),
        compiler_params=_cparams(("arbitrary",)),
        name="dispatch",
    )(dest, pstarts, counts, h2_tiles)


def _experts_kernel(be_ref, nused_ref, valid_ref, xs_ref, wg_ref, wu_ref, wd_ref, ys_ref,
                    wg_scr, wu_scr, wd_scr):
    i = pl.program_id(0)
    valid = valid_ref[i]
    changed = (i == 0) | (be_ref[i] != be_ref[jnp.maximum(i - 1, 0)])
    chunk_rows = MOE_CHUNK * TILE_ROWS

    @pl.when(changed & (valid > 0))
    def _():
        wg_scr[...] = wg_ref[0].astype(BF16)
        wu_scr[...] = wu_ref[0].astype(BF16)
        wd_scr[...] = wd_ref[0].astype(BF16)

    def run(n_chunks):
        hmids = []
        for h in range(n_chunks):
            x = _load_token_tiles(xs_ref, h * chunk_rows, MOE_CHUNK).astype(BF16)
            g = jnp.dot(x, wg_scr[...], preferred_element_type=F32)
            u = jnp.dot(x, wu_scr[...], preferred_element_type=F32)
            hmids.append((g * jax.nn.sigmoid(g) * u).astype(BF16))
        for h, hmid in enumerate(hmids):
            _store_token_tiles(ys_ref, jnp.dot(hmid, wd_scr[...], preferred_element_type=F32), h * chunk_rows)
        if n_chunks * chunk_rows < ys_ref.shape[0]:
            ys_ref[n_chunks * chunk_rows:, :] = jnp.zeros((ys_ref.shape[0] - n_chunks * chunk_rows, LANES), F32)

    for n_chunks in range(MOE_BLK // MOE_CHUNK + 1):
        lo, hi = (n_chunks - 1) * MOE_CHUNK, n_chunks * MOE_CHUNK
        pl.when((valid > lo) & (valid <= hi))(functools.partial(run, n_chunks))


def experts(xs_tiles, block_expert, nused, block_valid, w_gate, w_up, w_down):
    d = D_MODEL
    nblk = block_expert.shape[0]
    blk_rows = MOE_BLK * TILE_ROWS
    wmap = lambda i, be, nu, bv: (be[i], 0, 0)
    grid_spec = pltpu.PrefetchScalarGridSpec(
        num_scalar_prefetch=3,
        grid=(nblk,),
        in_specs=[pl.BlockSpec((blk_rows, LANES), lambda i, be, nu, bv: (jnp.minimum(i, nu[0] - 1), 0)),
                  pl.BlockSpec((1, d, D_EXPERT), wmap),
                  pl.BlockSpec((1, d, D_EXPERT), wmap),
                  pl.BlockSpec((1, D_EXPERT, d), wmap)],
        out_specs=pl.BlockSpec((blk_rows, LANES), lambda i, be, nu, bv: (i, 0)),
        scratch_shapes=[pltpu.VMEM((d, D_EXPERT), BF16),
                        pltpu.VMEM((d, D_EXPERT), BF16),
                        pltpu.VMEM((D_EXPERT, d), BF16)],
    )
    return pl.pallas_call(
        _experts_kernel,
        grid_spec=grid_spec,
        out_shape=jax.ShapeDtypeStruct((nblk * blk_rows, LANES), F32),
        compiler_params=_cparams(("arbitrary",)),
        name="experts",
    )(block_expert, nused, block_valid, xs_tiles, w_gate, w_up, w_down)


def _combine_kernel(dest_ref, ys_hbm, x1_ref, meta_ref, mod_ref, gf_ref, o_ref, ybuf, sem):
    i = pl.program_id(0)
    nstep = pl.num_programs(0)
    tc = TC_COMB
    nt = dest_ref.shape[0] // 2
    half_rows = tc * TILE_ROWS
    buf_rows = 2 * half_rows

    def gather(step, slot):
        def issue(c, carry):
            t0 = c * DMA_CHUNK
            for u in range(DMA_CHUNK):
                for k in range(2):
                    d = dest_ref[k * nt + step * tc + t0 + u]
                    pltpu.make_async_copy(
                        ys_hbm.at[pl.ds(d * TILE_ROWS, TILE_ROWS)],
                        ybuf.at[pl.ds(slot * buf_rows + k * half_rows + (t0 + u) * TILE_ROWS, TILE_ROWS)],
                        sem.at[slot]).start(priority=k)
            return carry

        lax.fori_loop(0, tc // DMA_CHUNK, issue, 0)

    @pl.when(i == 0)
    def _():
        gather(0, 0)

    @pl.when(i + 1 < nstep)
    def _():
        gather(i + 1, (i + 1) % 2)

    slot = i % 2
    start = pl.multiple_of(slot * buf_rows, buf_rows)
    pltpu.make_async_copy(ys_hbm.at[pl.ds(0, buf_rows)], ybuf.at[pl.ds(start, buf_rows)], sem.at[slot]).wait()
    meta = meta_ref[...]
    y0 = _load_token_tiles(ybuf, start, tc)
    y1 = _load_token_tiles(ybuf, start + half_rows, tc)
    moe = y0 * meta[:, 4:5] + y1 * meta[:, 5:6]
    x2 = x1_ref[...] + mod_ref[0][5:6] * moe
    o_ref[...] = _rms(x2, gf_ref[...])


def combine(dest_flat, ys, x1_flat, meta, mods, g_final, n_per_batch):
    nt, d = x1_flat.shape
    tc = TC_COMB
    per_b = n_per_batch // tc
    grid_spec = pltpu.PrefetchScalarGridSpec(
        num_scalar_prefetch=1,
        grid=(nt // tc,),
        in_specs=[pl.BlockSpec(memory_space=pl.ANY),
                  pl.BlockSpec((tc, d), lambda i, ds: (i, 0)),
                  pl.BlockSpec((tc, LANES), lambda i, ds: (i, 0)),
                  pl.BlockSpec((1, 8, d), lambda i, ds: (i // per_b, 0, 0)),
                  pl.BlockSpec((1, d), lambda i, ds: (0, 0))],
        out_specs=pl.BlockSpec((tc, d), lambda i, ds: (i, 0)),
        scratch_shapes=[pltpu.VMEM((2 * 2 * tc * TILE_ROWS, LANES), F32),
                        pltpu.SemaphoreType.DMA((2,))],
    )
    return pl.pallas_call(
        _combine_kernel,
        grid_spec=grid_spec,
        out_shape=jax.ShapeDtypeStruct((nt, d), F32),
        compiler_params=_cparams(("arbitrary",)),
        name="combine",
    )(dest_flat, ys, x1_flat, meta, mods, g_final.reshape(1, d))


def _dispatch_plan(metat, counts_row, nt):
    counts = counts_row[N_GROUPS:N_GROUPS + N_EXPERTS].astype(jnp.int32)
    pcounts = ((counts + MOE_BLK - 1) // MOE_BLK) * MOE_BLK
    pends = jnp.cumsum(pcounts)
    pstarts = pends - pcounts
    eid = metat[0:2].astype(jnp.int32)
    dest = metat[2:4].astype(jnp.int32)
    for e in range(N_EXPERTS):
        dest = dest + jnp.where(eid == e, pstarts[e], 0)
    dest = dest.reshape(-1)
    nblk = (nt * 2) // MOE_BLK + N_EXPERTS
    first_slot = jnp.arange(nblk, dtype=jnp.int32) * MOE_BLK
    block_expert = jnp.minimum(
        jnp.sum((pends[None, :] <= first_slot[:, None]).astype(jnp.int32), axis=1), N_EXPERTS - 1)
    nused = (pends[-1] // MOE_BLK).astype(jnp.int32).reshape(1)
    seg_end = jnp.sum(jnp.where(block_expert[:, None] == jnp.arange(N_EXPERTS)[None, :],
                                (pstarts + counts)[None, :], 0), axis=1)
    block_valid = jnp.where(first_slot < pends[-1], jnp.clip(seg_end - first_slot, 0, MOE_BLK), 0)
    return (dest, pstarts.astype(jnp.int32), counts, block_expert.astype(jnp.int32), nused,
            block_valid.astype(jnp.int32))


def kernel(x, c, ctx, c_ctx, w_ada, b_ada, g_norm1, w_in, w_fmix, rpb, g_out, w_out, g_norm2,
           w_router_group, w_router_expert, w_gate, w_up, w_down, g_final):
    b, n, d = x.shape
    assert (b, n, d) == (c.shape[0], SEQ, D_MODEL) and w_ada.shape[0] == 1
    nt = b * n

    cond8 = jnp.zeros((8, d), F32).at[0:b].set(c).at[b].set(c_ctx)
    mod = adaln(cond8, w_ada[0], b_ada[0])
    mods = jnp.pad(mod[0:b].reshape(b, N_MOD, d), ((0, 0), (0, 2), (0, 0)))
    mod_ctx = jnp.pad(mod[b].reshape(N_MOD, d), ((0, 2), (0, 0)))

    w_in_b = w_in[0].astype(BF16)
    qr, qp, kr, v, a = in_proj(x, mods, g_norm1[0], w_in_b)
    kc, vc = ctx_proj(ctx, mod_ctx, g_norm1[0], w_in_b[:, D_FOURIER + D_NA:])

    fr = dft_cols(dft_rows(a))
    na = attention(qr, qp, kr, v, kc, vc, bias_tables(rpb[0]))

    w_router = jnp.concatenate(
        [w_router_group[0], w_router_expert[0],
         jnp.zeros((d, LANES - N_GROUPS - N_EXPERTS), F32)], axis=1).astype(BF16)
    x1, h2_tiles, meta, metat, cnt = out_proj(fr, na, x, mods, w_fmix[0].astype(BF16),
                                              w_out[0].astype(BF16), g_out[0], g_norm2[0], w_router)

    dest, pstarts, counts, block_expert, nused, block_valid = _dispatch_plan(metat, cnt[0], nt)
    xs_tiles = dispatch(h2_tiles, dest, pstarts, counts, block_expert.shape[0] * MOE_BLK)
    ys_tiles = experts(xs_tiles, block_expert, nused, block_valid, w_gate[0], w_up[0], w_down[0])
    out = combine(dest, ys_tiles, x1.reshape(nt, d), meta, mods, g_final, n)
    return out.reshape(b, n, d)
```

```python
import functools
import math

import numpy as np
import jax
import jax.numpy as jnp
from jax import lax
from jax.experimental import pallas as pl
from jax.experimental.pallas import tpu as pltpu

F32 = jnp.float32
BF16 = jnp.bfloat16

D_MODEL = 1024
GRID_W = 64
GRID_H = 128
SEQ = GRID_W * GRID_H
CTX_LEN = 256
D_FOURIER = 256
FOURIER_GROUP = 64
HEAD_DIM = 64
N_HEADS = 12
D_NA = N_HEADS * HEAD_DIM
N_PAIRS = N_HEADS // 2
NA_ROWS = 8
NA_COLS = 16
ROPE_THETA = 10000.0
N_GROUPS = 4
EXPERTS_PER_GROUP = 8
N_EXPERTS = N_GROUPS * EXPERTS_PER_GROUP
D_EXPERT = 512
N_MOD = 6
D_IN_PROJ = D_FOURIER + 3 * D_NA
EPS = 1e-6
LANES = 128
NEG = -1e30
LOG2E = math.log2(math.e)

TM_IN = 1024
TM_PROJ = 512
ATT_ROWS = 32
DFT_TW = 8
DFT_TN = 4096
MOE_BLK = 512
MOE_CHUNK = 256
TC_COMB = 256
VMEM_LIMIT = 56 * 1024 * 1024


def _cparams(sem):
    return pltpu.CompilerParams(dimension_semantics=sem, vmem_limit_bytes=VMEM_LIMIT)


def _mxu_const(table):
    return jnp.asarray(table, F32).astype(BF16)


@functools.lru_cache(maxsize=None)
def _rope_tables():
    t = np.arange(SEQ)
    row, col = t // GRID_W, t % GRID_W
    lane = np.arange(LANES)
    d = lane % HEAD_DIM
    chunk = d // 32
    e = d % 32
    j = e % 16
    inv = ROPE_THETA ** (-(j.astype(np.float64)) / 16.0)
    pos = np.where(chunk[None, :] == 0, row[:, None], col[:, None]).astype(np.float64)
    ang = pos * inv[None, :]
    cos = np.cos(ang)
    sin = np.sin(ang)
    first = (e < 16)[None, :]
    s_first = np.where(first, -sin, 0.0)
    s_second = np.where(first, 0.0, sin)
    return (cos.astype(np.float32), s_first.astype(np.float32), s_second.astype(np.float32))


@functools.lru_cache(maxsize=None)
def _chan_dft():
    c = np.arange(FOURIER_GROUP)
    ang = 2.0 * np.pi * ((c[:, None] * c[None, :]) % FOURIER_GROUP) / FOURIER_GROUP
    eye = np.eye(D_FOURIER // FOURIER_GROUP)
    re = np.kron(eye, np.cos(ang))
    im = np.kron(eye, -np.sin(ang))
    return np.concatenate([re, im], axis=1).astype(np.float32)


@functools.lru_cache(maxsize=None)
def _row_dft():
    k1 = np.arange(GRID_H)[:, None]
    r = np.arange(GRID_H)[None, :]
    out = np.zeros((GRID_W, 2 * GRID_H, 2 * GRID_H), np.float32)
    for w in range(GRID_W):
        m = (k1 * (GRID_W * r + w)) % SEQ
        ang = 2.0 * np.pi * m / SEQ
        c, s = np.cos(ang), np.sin(ang)
        out[w] = np.block([[c, s], [-s, c]])
    return out


@functools.lru_cache(maxsize=None)
def _col_dft():
    k2 = np.arange(GRID_W)
    ang = 2.0 * np.pi * ((k2[:, None] * k2[None, :]) % GRID_W) / GRID_W
    scale = 1.0 / math.sqrt(SEQ * FOURIER_GROUP)
    return (np.concatenate([np.cos(ang), np.sin(ang)], axis=1) * scale).astype(np.float32)


@functools.lru_cache(maxsize=None)
def _bias_index():
    c = np.arange(GRID_W)
    start = np.clip(c - NA_COLS // 2, 0, GRID_W - NA_COLS)
    valid = (c[None, :] >= start[:, None]) & (c[None, :] < start[:, None] + NA_COLS)
    dc = np.clip(c[None, :] - c[:, None] + (NA_COLS - 1), 0, 2 * NA_COLS - 2)
    return dc.astype(np.int32), valid


@functools.lru_cache(maxsize=None)
def _strict_lower(n):
    return np.tril(np.ones((n, n), np.float32), k=-1)


def _adaln_kernel(c_ref, w_ref, b_ref, o_ref):
    c = c_ref[...]
    s = c * jax.nn.sigmoid(c)
    o_ref[...] = jnp.dot(s, w_ref[...], precision=lax.Precision.HIGHEST,
                         preferred_element_type=F32) + b_ref[...]


def adaln(cond8, w, b):
    n = w.shape[1]
    tn = 1536
    return pl.pallas_call(
        _adaln_kernel,
        grid=(n // tn,),
        in_specs=[pl.BlockSpec((8, D_MODEL), lambda j: (0, 0)),
                  pl.BlockSpec((D_MODEL, tn), lambda j: (0, j)),
                  pl.BlockSpec((1, tn), lambda j: (0, j))],
        out_specs=pl.BlockSpec((8, tn), lambda j: (0, j)),
        out_shape=jax.ShapeDtypeStruct((8, n), F32),
        compiler_params=_cparams(("arbitrary",)),
        name="adaln",
    )(cond8, w, b.reshape(1, n))


def _norm_mod(x, g, shift, scale):
    ms = jnp.mean(x * x, axis=-1, keepdims=True)
    return (x * lax.rsqrt(ms + EPS) * g) * (1.0 + scale) + shift


def _in_proj_kernel(x_ref, mod_ref, g_ref, w_ref, cs_ref, cos_ref, s1_ref, s2_ref,
                    qr_ref, qp_ref, kr_ref, v_ref, a_ref, h_scr):
    m = mod_ref[0]
    h_scr[...] = _norm_mod(x_ref[0], g_ref[...], m[0:1], m[1:2]).astype(BF16)
    cos, s1, s2 = cos_ref[...], s1_ref[...], s2_ref[...]

    def rope(t):
        return (t * cos + pltpu.roll(t, LANES - 16, axis=1) * s1 + pltpu.roll(t, 16, axis=1) * s2)

    f = jnp.dot(h_scr[...], w_ref[:, 0:D_FOURIER], preferred_element_type=F32)
    a = jnp.dot(f.astype(BF16), cs_ref[...], preferred_element_type=F32)
    a_ref[0, 0] = a[:, :D_FOURIER].astype(BF16)
    a_ref[0, 1] = a[:, D_FOURIER:].astype(BF16)

    scale = HEAD_DIM ** -0.5 * LOG2E
    for c in range(D_NA // 256):
        lo = D_FOURIER + 256 * c
        q = jnp.dot(h_scr[...], w_ref[:, lo:lo + 256], preferred_element_type=F32)
        k = jnp.dot(h_scr[...], w_ref[:, lo + D_NA:lo + D_NA + 256], preferred_element_type=F32)
        v = jnp.dot(h_scr[...], w_ref[:, lo + 2 * D_NA:lo + 2 * D_NA + 256], preferred_element_type=F32)
        v_ref[0, :, 256 * c:256 * c + 256] = v.astype(BF16)
        for s in range(2):
            sl = slice(LANES * s, LANES * (s + 1))
            ol = slice(256 * c + LANES * s, 256 * c + LANES * (s + 1))
            qs, ks = q[:, sl], k[:, sl]
            qp_ref[0, :, ol] = (qs * scale).astype(BF16)
            qr_ref[0, :, ol] = (rope(qs) * scale).astype(BF16)
            kr_ref[0, :, ol] = rope(ks).astype(BF16)


def in_proj(x, mods, g1, w_in_bf16):
    b, n, d = x.shape
    tm = TM_IN
    cos, s1, s2 = _rope_tables()
    cs = _mxu_const(_chan_dft())
    tok = lambda bi, i: (bi, i, 0)
    const2 = lambda bi, i: (0, 0)
    tab = pl.BlockSpec((tm, LANES), lambda bi, i: (i, 0))
    qkv_shape = jax.ShapeDtypeStruct((b, n, D_NA), BF16)
    qkv_spec = pl.BlockSpec((1, tm, D_NA), tok)
    return pl.pallas_call(
        _in_proj_kernel,
        grid=(b, n // tm),
        in_specs=[pl.BlockSpec((1, tm, d), tok),
                  pl.BlockSpec((1, 8, d), lambda bi, i: (bi, 0, 0)),
                  pl.BlockSpec((1, d), const2),
                  pl.BlockSpec((d, D_IN_PROJ), const2),
                  pl.BlockSpec((D_FOURIER, 2 * D_FOURIER), const2),
                  tab, tab, tab],
        out_specs=[qkv_spec, qkv_spec, qkv_spec, qkv_spec,
                   pl.BlockSpec((1, 2, tm, D_FOURIER), lambda bi, i: (bi, 0, i, 0))],
        out_shape=[qkv_shape, qkv_shape, qkv_shape, qkv_shape,
                   jax.ShapeDtypeStruct((b, 2, n, D_FOURIER), BF16)],
        scratch_shapes=[pltpu.VMEM((tm, d), BF16)],
        compiler_params=_cparams(("arbitrary", "arbitrary")),
        name="in_proj",
    )(x, mods, g1.reshape(1, d), w_in_bf16, cs, jnp.asarray(cos), jnp.asarray(s1), jnp.asarray(s2))


def _ctx_proj_kernel(x_ref, mod_ref, g_ref, w_ref, k_ref, v_ref):
    m = mod_ref[...]
    h = _norm_mod(x_ref[0], g_ref[...], m[0:1], m[1:2]).astype(BF16)
    k_ref[0] = jnp.dot(h, w_ref[:, 0:D_NA], preferred_element_type=F32).astype(BF16)
    v_ref[0] = jnp.dot(h, w_ref[:, D_NA:2 * D_NA], preferred_element_type=F32).astype(BF16)


def ctx_proj(ctx, mod_ctx, g1, w_kv_bf16):
    b, l, d = ctx.shape
    shape = jax.ShapeDtypeStruct((b, l, D_NA), BF16)
    spec = pl.BlockSpec((1, l, D_NA), lambda bi: (bi, 0, 0))
    return pl.pallas_call(
        _ctx_proj_kernel,
        grid=(b,),
        in_specs=[pl.BlockSpec((1, l, d), lambda bi: (bi, 0, 0)),
                  pl.BlockSpec((8, d), lambda bi: (0, 0)),
                  pl.BlockSpec((1, d), lambda bi: (0, 0)),
                  pl.BlockSpec((d, 2 * D_NA), lambda bi: (0, 0))],
        out_specs=[spec, spec],
        out_shape=[shape, shape],
        compiler_params=_cparams(("arbitrary",)),
        name="ctx_proj",
    )(ctx, mod_ctx, g1.reshape(1, d), w_kv_bf16)


def _dft_rows_kernel(a_ref, g_ref, z_ref):
    for j in range(DFT_TW):
        sl = slice(D_FOURIER * j, D_FOURIER * (j + 1))
        rhs = jnp.concatenate([a_ref[0, 0, :, sl], a_ref[0, 1, :, sl]], axis=0)
        z = jnp.dot(g_ref[j], rhs, preferred_element_type=F32)
        z_ref[0, 0, j] = z[:GRID_H].astype(BF16)
        z_ref[0, 1, j] = z[GRID_H:].astype(BF16)


def dft_rows(a):
    b = a.shape[0]
    a4 = a.reshape(b, 2, GRID_H, GRID_W * D_FOURIER)
    g = _mxu_const(_row_dft())
    return pl.pallas_call(
        _dft_rows_kernel,
        grid=(GRID_W // DFT_TW, b),
        in_specs=[pl.BlockSpec((1, 2, GRID_H, DFT_TW * D_FOURIER), lambda j, bi: (bi, 0, 0, j)),
                  pl.BlockSpec((DFT_TW, 2 * GRID_H, 2 * GRID_H), lambda j, bi: (j, 0, 0))],
        out_specs=pl.BlockSpec((1, 2, DFT_TW, GRID_H, D_FOURIER), lambda j, bi: (bi, 0, j, 0, 0)),
        out_shape=jax.ShapeDtypeStruct((b, 2, GRID_W, GRID_H, D_FOURIER), BF16),
        compiler_params=_cparams(("arbitrary", "arbitrary")),
        name="dft_rows",
    )(a4, g)


def _dft_cols_kernel(z_ref, cs_ref, o_ref):
    rhs = jnp.concatenate([z_ref[0, 0], z_ref[0, 1]], axis=0)
    o_ref[0] = jnp.dot(cs_ref[...], rhs, preferred_element_type=F32).astype(BF16)


def dft_cols(z):
    b = z.shape[0]
    ncol = GRID_H * D_FOURIER
    z4 = z.reshape(b, 2, GRID_W, ncol)
    cs = _mxu_const(_col_dft())
    out = pl.pallas_call(
        _dft_cols_kernel,
        grid=(b, ncol // DFT_TN),
        in_specs=[pl.BlockSpec((1, 2, GRID_W, DFT_TN), lambda bi, j: (bi, 0, 0, j)),
                  pl.BlockSpec((GRID_W, 2 * GRID_W), lambda bi, j: (0, 0))],
        out_specs=pl.BlockSpec((1, GRID_W, DFT_TN), lambda bi, j: (bi, 0, j)),
        out_shape=jax.ShapeDtypeStruct((b, GRID_W, ncol), BF16),
        compiler_params=_cparams(("arbitrary", "arbitrary")),
        name="dft_cols",
    )(z4, cs)
    return out.reshape(b, SEQ, D_FOURIER)


WIN_ROWS = NA_ROWS + 2
WIN_KEYS = WIN_ROWS * GRID_W
WIN_TILES = WIN_KEYS // LANES
N_DR = 2 * NA_ROWS + 1
SOFT_ROWS = 32
ATT_PAIRS = ATT_ROWS // 2


def _attention_kernel(qr_ref, qp_ref, k_ref, v_ref, kc_ref, vc_ref, tb_ref, o_ref,
                      sc_scr, pc_scr, s_scr, p_scr):
    rb = pl.program_id(2)
    first2 = lax.broadcasted_iota(jnp.int32, (2 * GRID_W, LANES), 1) < HEAD_DIM
    nt = (((1,), (1,)), ((), ()))

    def window(jp):
        r0 = rb * ATT_ROWS + 2 * jp
        ws = jnp.clip(r0 - NA_ROWS // 2, 0, GRID_H - WIN_ROWS)
        return r0, ws, pl.multiple_of(ws * GRID_W, GRID_W)

    def split_heads(q2):
        z2 = jnp.zeros_like(q2)
        return jnp.concatenate([jnp.where(first2, q2, z2), jnp.where(first2, z2, q2)], axis=0)

    def scores(jp):
        _, _, koff = window(jp)
        qs = slice(jp * 2 * GRID_W, (jp + 1) * 2 * GRID_W)
        kw = k_ref[0, pl.ds(koff, WIN_KEYS), :]
        s_scr[jp] = lax.dot_general(split_heads(qr_ref[0, qs, :]), kw, nt, preferred_element_type=F32)
        sc_scr[jp] = lax.dot_general(split_heads(qp_ref[0, qs, :]), kc_ref[0], nt, preferred_element_type=F32)

    def softmax(jp):
        r0, ws, _ = window(jp)
        for c in range(4):
            hh, i = c // 2, c % 2
            r = r0 + i
            off = jnp.clip(r - NA_ROWS // 2, 0, GRID_H - NA_ROWS) - ws
            sel = []
            for t in range(WIN_TILES):
                v0 = (2 * t >= off) & (2 * t < off + NA_ROWS)
                v1 = (2 * t + 1 >= off) & (2 * t + 1 < off + NA_ROWS)
                kind = jnp.where(v0 & v1, 0, jnp.where(v0, 2, 1))
                sel.append((kind, jnp.where(v0 | v1, ws + 2 * t - r + NA_ROWS, N_DR - 1)))
            for h in range(GRID_W // SOFT_ROWS):
                lo = h * SOFT_ROWS
                rows = slice(c * GRID_W + lo, c * GRID_W + lo + SOFT_ROWS)
                tiles = [s_scr[jp, rows, t * LANES:(t + 1) * LANES]
                         + tb_ref[0, hh, sel[t][0], sel[t][1], lo:lo + SOFT_ROWS, :] for t in range(WIN_TILES)]
                sc = sc_scr[jp, rows, :]
                mt = jnp.maximum(sc[:, :LANES], sc[:, LANES:])
                for tl in tiles:
                    mt = jnp.maximum(mt, tl)
                m = jnp.max(mt, axis=1, keepdims=True)
                pc_scr[jp, rows, :] = jnp.exp2(sc - m).astype(BF16)
                for t, tl in enumerate(tiles):
                    p_scr[jp, rows, t * LANES:(t + 1) * LANES] = jnp.exp2(tl - m).astype(BF16)

    def weighted_values(jp):
        _, _, koff = window(jp)
        vw = jnp.concatenate([v_ref[0, pl.ds(koff, WIN_KEYS), :], jnp.ones((WIN_KEYS, LANES), BF16)], axis=1)
        vc = jnp.concatenate([vc_ref[0], jnp.ones((CTX_LEN, LANES), BF16)], axis=1)
        o = (jnp.dot(p_scr[jp], vw, preferred_element_type=F32)
             + jnp.dot(pc_scr[jp], vc, preferred_element_type=F32))
        oa, ob = o[:2 * GRID_W], o[2 * GRID_W:]
        out = jnp.where(first2, oa[:, :LANES] / oa[:, LANES:], ob[:, :LANES] / ob[:, LANES:])
        o_ref[0, jp * 2 * GRID_W:(jp + 1) * 2 * GRID_W, :] = out.astype(BF16)

    scores(0)
    scores(1)
    softmax(0)
    for jp in range(ATT_PAIRS):
        if jp + 2 < ATT_PAIRS:
            scores(jp + 2)
        if jp + 1 < ATT_PAIRS:
            softmax(jp + 1)
        weighted_values(jp)


def bias_tables(rpb):
    dc, valid = _bias_index()
    n_dc = 2 * NA_COLS - 1
    onehot = (dc.reshape(1, -1) == np.arange(n_dc).reshape(-1, 1)).astype(np.float32)
    t = jnp.dot(rpb.reshape(-1, n_dc), jnp.asarray(onehot), precision=lax.Precision.HIGHEST)
    t = jnp.where(valid[None, None], LOG2E * t.reshape(N_HEADS, 2 * NA_ROWS - 1, GRID_W, GRID_W), NEG)
    t = jnp.pad(t, ((0, 0), (1, 2), (0, 0), (0, 0)), constant_values=NEG)
    tb = pl.pallas_call(
        _bias_pairs_kernel,
        grid=(N_HEADS,),
        in_specs=[pl.BlockSpec((1, N_DR + 1, GRID_W, GRID_W), lambda h: (h, 0, 0, 0))],
        out_specs=pl.BlockSpec((1, 3, N_DR, GRID_W, LANES), lambda h: (h, 0, 0, 0, 0)),
        out_shape=jax.ShapeDtypeStruct((N_HEADS, 3, N_DR, GRID_W, LANES), F32),
        compiler_params=_cparams(("arbitrary",)),
        name="bias_pairs",
    )(t)
    return tb.reshape(N_PAIRS, 2, 3, N_DR, GRID_W, LANES)


def _bias_pairs_kernel(t_ref, o_ref):
    neg = jnp.full((GRID_W, GRID_W), NEG, F32)
    for d in range(N_DR):
        left, right = t_ref[0, d], t_ref[0, d + 1]
        o_ref[0, 0, d] = jnp.concatenate([left, right], axis=1)
        o_ref[0, 1, d] = jnp.concatenate([neg, right], axis=1)
        o_ref[0, 2, d] = jnp.concatenate([left, neg], axis=1)


def attention(qr, qp, kr, v, kc, vc, tb):
    b, n, _ = qr.shape
    tq = ATT_ROWS * GRID_W
    qspec = pl.BlockSpec((1, tq, LANES), lambda bi, hp, i: (bi, i, hp))
    kspec = pl.BlockSpec((1, n, LANES), lambda bi, hp, i: (bi, 0, hp))
    cspec = pl.BlockSpec((1, CTX_LEN, LANES), lambda bi, hp, i: (bi, 0, hp))
    return pl.pallas_call(
        _attention_kernel,
        grid=(b, N_PAIRS, GRID_H // ATT_ROWS),
        in_specs=[qspec, qspec, kspec, kspec, cspec, cspec,
                  pl.BlockSpec((1, 2, 3, N_DR, GRID_W, LANES), lambda bi, hp, i: (hp, 0, 0, 0, 0, 0))],
        out_specs=qspec,
        out_shape=jax.ShapeDtypeStruct((b, n, D_NA), BF16),
        scratch_shapes=[pltpu.VMEM((ATT_PAIRS, 4 * GRID_W, CTX_LEN), F32),
                        pltpu.VMEM((ATT_PAIRS, 4 * GRID_W, CTX_LEN), BF16),
                        pltpu.VMEM((ATT_PAIRS, 4 * GRID_W, WIN_KEYS), F32),
                        pltpu.VMEM((ATT_PAIRS, 4 * GRID_W, WIN_KEYS), BF16)],
        compiler_params=_cparams(("arbitrary", "arbitrary", "arbitrary")),
        name="attention",
    )(qr, qp, kr, v, kc, vc, tb)


def _rms(x, g):
    ms = jnp.mean(x * x, axis=-1, keepdims=True)
    return x * lax.rsqrt(ms + EPS) * g


TILE_ROWS = D_MODEL // LANES


def _store_token_tiles(ref, val, start=0):
    rows = val.shape[0]
    for j in range(TILE_ROWS):
        ref[pl.ds(start + j, rows, stride=TILE_ROWS), :] = val[:, LANES * j:LANES * (j + 1)]


def _load_token_tiles(ref, start, rows):
    return jnp.concatenate(
        [ref[pl.ds(start + j, rows, stride=TILE_ROWS), :] for j in range(TILE_ROWS)], axis=1)


def _out_proj_kernel(fr_ref, na_ref, x_ref, mod_ref, wf_ref, wo_ref, go_ref, g2_ref, wr_ref, tri_ref,
                     x1_ref, h2_ref, meta_ref, metat_ref, cnt_ref, run_scr):
    @pl.when((pl.program_id(0) == 0) & (pl.program_id(1) == 0))
    def _():
        run_scr[...] = jnp.zeros_like(run_scr)

    m = mod_ref[0]
    go = go_ref[...]
    fo = jnp.dot(fr_ref[0], wf_ref[...], preferred_element_type=F32)
    fn = _rms(fo, go[:, :D_FOURIER]).astype(BF16)
    nn = _rms(na_ref[0].astype(F32), go[:, D_FOURIER:]).astype(BF16)
    y = (jnp.dot(fn, wo_ref[0:D_FOURIER, :], preferred_element_type=F32)
         + jnp.dot(nn, wo_ref[D_FOURIER:, :], preferred_element_type=F32))
    x1 = x_ref[0] + m[2:3] * y
    x1_ref[0] = x1
    h2 = _norm_mod(x1, g2_ref[...], m[3:4], m[4:5])
    _store_token_tiles(h2_ref, h2)
    logits = jnp.dot(h2.astype(BF16), wr_ref[...], preferred_element_type=F32)

    tm = logits.shape[0]
    lane = lax.broadcasted_iota(jnp.int32, (tm, LANES), 1).astype(F32)
    ninf = jnp.float32(-jnp.inf)

    def argmax_first(vals):
        mx = jnp.max(vals, axis=1, keepdims=True)
        idx = jnp.min(jnp.where(vals == mx, lane, float(LANES)), axis=1, keepdims=True)
        return mx, idx

    lg = jnp.where(lane < N_GROUPS, logits, ninf)
    gmax, gidx = argmax_first(lg)
    pg = 1.0 / jnp.sum(jnp.exp(lg - gmax), axis=1, keepdims=True)
    lo = N_GROUPS + EXPERTS_PER_GROUP * gidx
    le = jnp.where((lane >= lo) & (lane < lo + EXPERTS_PER_GROUP), logits, ninf)
    e1, i1 = argmax_first(le)
    e2, i2 = argmax_first(jnp.where(lane == i1, ninf, le))
    dd = jnp.exp(e2 - e1)
    gate1 = pg / (1.0 + dd)
    gate2 = pg * dd / (1.0 + dd)

    hot1 = lane == i1
    hot2 = lane == i2
    onehot = jnp.where(hot1 | hot2, 1.0, 0.0)
    cnt = jnp.dot(tri_ref[...], onehot.astype(BF16), preferred_element_type=F32) + run_scr[...]
    rank1 = jnp.sum(jnp.where(hot1, cnt, 0.0), axis=1, keepdims=True)
    rank2 = jnp.sum(jnp.where(hot2, cnt, 0.0), axis=1, keepdims=True)
    run_scr[...] = run_scr[...] + jnp.sum(onehot, axis=0, keepdims=True)

    meta = jnp.where(lane == 0, i1 - N_GROUPS,
           jnp.where(lane == 1, i2 - N_GROUPS,
           jnp.where(lane == 2, rank1,
           jnp.where(lane == 3, rank2,
           jnp.where(lane == 4, gate1,
           jnp.where(lane == 5, gate2, 0.0))))))
    meta_ref[...] = meta
    metat_ref[...] = jnp.transpose(meta)[0:8, :]
    cnt_ref[...] = jnp.broadcast_to(run_scr[...], cnt_ref.shape)


def out_proj(fr, na, x, mods, w_fmix_bf16, w_out_bf16, g_out, g2, w_router_bf16):
    b, n, d = x.shape
    tm = TM_PROJ
    steps = n // tm
    tok = lambda bi, i: (bi, i, 0)
    const2 = lambda bi, i: (0, 0)
    flat = lambda bi, i: (bi * steps + i, 0)
    tri = _mxu_const(_strict_lower(tm))
    return pl.pallas_call(
        _out_proj_kernel,
        grid=(b, steps),
        in_specs=[pl.BlockSpec((1, tm, D_FOURIER), tok),
                  pl.BlockSpec((1, tm, D_NA), tok),
                  pl.BlockSpec((1, tm, d), tok),
                  pl.BlockSpec((1, 8, d), lambda bi, i: (bi, 0, 0)),
                  pl.BlockSpec((D_FOURIER, D_FOURIER), const2),
                  pl.BlockSpec((d, d), const2),
                  pl.BlockSpec((1, d), const2),
                  pl.BlockSpec((1, d), const2),
                  pl.BlockSpec((d, LANES), const2),
                  pl.BlockSpec((tm, tm), const2)],
        out_specs=[pl.BlockSpec((1, tm, d), tok),
                   pl.BlockSpec((tm * TILE_ROWS, LANES), flat),
                   pl.BlockSpec((tm, LANES), flat),
                   pl.BlockSpec((8, tm), lambda bi, i: (0, bi * steps + i)),
                   pl.BlockSpec((8, LANES), const2)],
        out_shape=[jax.ShapeDtypeStruct((b, n, d), F32),
                   jax.ShapeDtypeStruct((b * n * TILE_ROWS, LANES), F32),
                   jax.ShapeDtypeStruct((b * n, LANES), F32),
                   jax.ShapeDtypeStruct((8, b * n), F32),
                   jax.ShapeDtypeStruct((8, LANES), F32)],
        scratch_shapes=[pltpu.VMEM((1, LANES), F32)],
        compiler_params=_cparams(("arbitrary", "arbitrary")),
        name="out_proj",
    )(fr, na, x, mods, w_fmix_bf16, w_out_bf16, g_out.reshape(1, d), g2.reshape(1, d), w_router_bf16, tri)


DMA_CHUNK = 8


TD_DISP = 1024


def _dispatch_kernel(dest_ref, pstart_ref, count_ref, h2_ref, xs_hbm, zero_scr, sem, pad_sem):
    i = pl.program_id(0)
    nt = dest_ref.shape[0] // 2

    @pl.when(i == 0)
    def _():
        zero_scr[...] = jnp.zeros_like(zero_scr)

        def per_expert(e, npad):
            lo = pstart_ref[e] + count_ref[e]
            mid = pstart_ref[e] + ((count_ref[e] + MOE_CHUNK - 1) // MOE_CHUNK) * MOE_CHUNK
            hi = pstart_ref[e] + ((count_ref[e] + MOE_BLK - 1) // MOE_BLK) * MOE_BLK

            def fill(s, carry):
                pltpu.make_async_copy(zero_scr.at[pl.ds(0, TILE_ROWS)],
                                      xs_hbm.at[pl.ds(s * TILE_ROWS, TILE_ROWS)], pad_sem).start()
                return carry

            def fill_chunk(s, carry):
                pltpu.make_async_copy(zero_scr.at[pl.ds(0, MOE_CHUNK * TILE_ROWS)],
                                      xs_hbm.at[pl.ds(mid * TILE_ROWS + s * MOE_CHUNK * TILE_ROWS,
                                                      MOE_CHUNK * TILE_ROWS)], pad_sem).start()
                return carry

            lax.fori_loop(lo, mid, fill, 0)
            lax.fori_loop(0, (hi - mid) // MOE_CHUNK, fill_chunk, 0)
            return npad + (hi - lo)

        npad = lax.fori_loop(0, N_EXPERTS, per_expert, 0)

        blk_rows = MOE_BLK * TILE_ROWS
        first_free = (pstart_ref[N_EXPERTS - 1] + count_ref[N_EXPERTS - 1] + MOE_BLK - 1) // MOE_BLK
        n_blocks = xs_hbm.shape[0] // blk_rows

        def fill_block(bk, carry):
            pltpu.make_async_copy(zero_scr, xs_hbm.at[pl.ds(bk * blk_rows, blk_rows)], pad_sem).start()
            return carry

        lax.fori_loop(first_free, n_blocks, fill_block, 0)
        rows = npad * TILE_ROWS + (n_blocks - first_free) * blk_rows

        @pl.when(rows > 0)
        def _():
            pltpu.make_async_copy(xs_hbm.at[pl.ds(0, rows)], xs_hbm.at[pl.ds(0, rows)], pad_sem).wait()

    def issue(c, carry):
        t0 = c * DMA_CHUNK
        for u in range(DMA_CHUNK):
            for k in range(2):
                d = dest_ref[k * nt + i * TD_DISP + t0 + u]
                pltpu.make_async_copy(h2_ref.at[pl.ds((t0 + u) * TILE_ROWS, TILE_ROWS)],
                                      xs_hbm.at[pl.ds(d * TILE_ROWS, TILE_ROWS)], sem).start(priority=k)
        return carry

    lax.fori_loop(0, TD_DISP // DMA_CHUNK, issue, 0)
    rows = 2 * TD_DISP * TILE_ROWS
    pltpu.make_async_copy(xs_hbm.at[pl.ds(0, rows)], xs_hbm.at[pl.ds(0, rows)], sem).wait()


def dispatch(h2_tiles, dest, pstarts, counts, n_slots):
    nt = dest.shape[0] // 2
    grid_spec = pltpu.PrefetchScalarGridSpec(
        num_scalar_prefetch=3,
        grid=(nt // TD_DISP,),
        in_specs=[pl.BlockSpec((TD_DISP * TILE_ROWS, LANES), lambda i, ds, ps, ct: (i, 0))],
        out_specs=pl.BlockSpec(memory_space=pl.ANY),
        scratch_shapes=[pltpu.VMEM((MOE_BLK * TILE_ROWS, LANES), F32),
                        pltpu.SemaphoreType.DMA(()),
                        pltpu.SemaphoreType.DMA(())],
    )
    return pl.pallas_call(
        _dispatch_kernel,
        grid_spec=grid_spec,
        out_shape=jax.ShapeDtypeStruct((n_slots * TILE_ROWS, LANES), F32),
        compiler_params=_cparams(("arbitrary",)),
        name="dispatch",
    )(dest, pstarts, counts, h2_tiles)


def _experts_kernel(be_ref, nused_ref, valid_ref, xs_ref, wg_ref, wu_ref, wd_ref, ys_ref,
                    wg_scr, wu_scr, wd_scr):
    i = pl.program_id(0)
    valid = valid_ref[i]
    changed = (i == 0) | (be_ref[i] != be_ref[jnp.maximum(i - 1, 0)])
    chunk_rows = MOE_CHUNK * TILE_ROWS

    @pl.when(changed & (valid > 0))
    def _():
        wg_scr[...] = wg_ref[0].astype(BF16)
        wu_scr[...] = wu_ref[0].astype(BF16)
        wd_scr[...] = wd_ref[0].astype(BF16)

    def run(n_chunks):
        hmids = []
        for h in range(n_chunks):
            x = _load_token_tiles(xs_ref, h * chunk_rows, MOE_CHUNK).astype(BF16)
            g = jnp.dot(x, wg_scr[...], preferred_element_type=F32)
            u = jnp.dot(x, wu_scr[...], preferred_element_type=F32)
            hmids.append((g * jax.nn.sigmoid(g) * u).astype(BF16))
        for h, hmid in enumerate(hmids):
            _store_token_tiles(ys_ref, jnp.dot(hmid, wd_scr[...], preferred_element_type=F32), h * chunk_rows)
        if n_chunks * chunk_rows < ys_ref.shape[0]:
            ys_ref[n_chunks * chunk_rows:, :] = jnp.zeros((ys_ref.shape[0] - n_chunks * chunk_rows, LANES), F32)

    for n_chunks in range(MOE_BLK // MOE_CHUNK + 1):
        lo, hi = (n_chunks - 1) * MOE_CHUNK, n_chunks * MOE_CHUNK
        pl.when((valid > lo) & (valid <= hi))(functools.partial(run, n_chunks))


def experts(xs_tiles, block_expert, nused, block_valid, w_gate, w_up, w_down):
    d = D_MODEL
    nblk = block_expert.shape[0]
    blk_rows = MOE_BLK * TILE_ROWS
    wmap = lambda i, be, nu, bv: (be[i], 0, 0)
    grid_spec = pltpu.PrefetchScalarGridSpec(
        num_scalar_prefetch=3,
        grid=(nblk,),
        in_specs=[pl.BlockSpec((blk_rows, LANES), lambda i, be, nu, bv: (jnp.minimum(i, nu[0] - 1), 0)),
                  pl.BlockSpec((1, d, D_EXPERT), wmap),
                  pl.BlockSpec((1, d, D_EXPERT), wmap),
                  pl.BlockSpec((1, D_EXPERT, d), wmap)],
        out_specs=pl.BlockSpec((blk_rows, LANES), lambda i, be, nu, bv: (i, 0)),
        scratch_shapes=[pltpu.VMEM((d, D_EXPERT), BF16),
                        pltpu.VMEM((d, D_EXPERT), BF16),
                        pltpu.VMEM((D_EXPERT, d), BF16)],
    )
    return pl.pallas_call(
        _experts_kernel,
        grid_spec=grid_spec,
        out_shape=jax.ShapeDtypeStruct((nblk * blk_rows, LANES), F32),
        compiler_params=_cparams(("arbitrary",)),
        name="experts",
    )(block_expert, nused, block_valid, xs_tiles, w_gate, w_up, w_down)


def _combine_kernel(dest_ref, ys_hbm, x1_ref, meta_ref, mod_ref, gf_ref, o_ref, ybuf, sem):
    i = pl.program_id(0)
    nstep = pl.num_programs(0)
    tc = TC_COMB
    nt = dest_ref.shape[0] // 2
    half_rows = tc * TILE_ROWS
    buf_rows = 2 * half_rows

    def gather(step, slot):
        def issue(c, carry):
            t0 = c * DMA_CHUNK
            for u in range(DMA_CHUNK):
                for k in range(2):
                    d = dest_ref[k * nt + step * tc + t0 + u]
                    pltpu.make_async_copy(
                        ys_hbm.at[pl.ds(d * TILE_ROWS, TILE_ROWS)],
                        ybuf.at[pl.ds(slot * buf_rows + k * half_rows + (t0 + u) * TILE_ROWS, TILE_ROWS)],
                        sem.at[slot]).start(priority=k)
            return carry

        lax.fori_loop(0, tc // DMA_CHUNK, issue, 0)

    @pl.when(i == 0)
    def _():
        gather(0, 0)

    @pl.when(i + 1 < nstep)
    def _():
        gather(i + 1, (i + 1) % 2)

    slot = i % 2
    start = pl.multiple_of(slot * buf_rows, buf_rows)
    pltpu.make_async_copy(ys_hbm.at[pl.ds(0, buf_rows)], ybuf.at[pl.ds(start, buf_rows)], sem.at[slot]).wait()
    meta = meta_ref[...]
    y0 = _load_token_tiles(ybuf, start, tc)
    y1 = _load_token_tiles(ybuf, start + half_rows, tc)
    moe = y0 * meta[:, 4:5] + y1 * meta[:, 5:6]
    x2 = x1_ref[...] + mod_ref[0][5:6] * moe
    o_ref[...] = _rms(x2, gf_ref[...])


def combine(dest_flat, ys, x1_flat, meta, mods, g_final, n_per_batch):
    nt, d = x1_flat.shape
    tc = TC_COMB
    per_b = n_per_batch // tc
    grid_spec = pltpu.PrefetchScalarGridSpec(
        num_scalar_prefetch=1,
        grid=(nt // tc,),
        in_specs=[pl.BlockSpec(memory_space=pl.ANY),
                  pl.BlockSpec((tc, d), lambda i, ds: (i, 0)),
                  pl.BlockSpec((tc, LANES), lambda i, ds: (i, 0)),
                  pl.BlockSpec((1, 8, d), lambda i, ds: (i // per_b, 0, 0)),
                  pl.BlockSpec((1, d), lambda i, ds: (0, 0))],
        out_specs=pl.BlockSpec((tc, d), lambda i, ds: (i, 0)),
        scratch_shapes=[pltpu.VMEM((2 * 2 * tc * TILE_ROWS, LANES), F32),
                        pltpu.SemaphoreType.DMA((2,))],
    )
    return pl.pallas_call(
        _combine_kernel,
        grid_spec=grid_spec,
        out_shape=jax.ShapeDtypeStruct((nt, d), F32),
        compiler_params=_cparams(("arbitrary",)),
        name="combine",
    )(dest_flat, ys, x1_flat, meta, mods, g_final.reshape(1, d))


def _dispatch_plan(metat, counts_row, nt):
    counts = counts_row[N_GROUPS:N_GROUPS + N_EXPERTS].astype(jnp.int32)
    pcounts = ((counts + MOE_BLK - 1) // MOE_BLK) * MOE_BLK
    pends = jnp.cumsum(pcounts)
    pstarts = pends - pcounts
    eid = metat[0:2].astype(jnp.int32).reshape(-1, LANES)
    dest = metat[2:4].astype(jnp.int32).reshape(-1, LANES)
    for e in range(N_EXPERTS):
        dest = dest + jnp.where(eid == e, pstarts[e], 0)
    dest = dest.reshape(-1)
    nblk = (nt * 2) // MOE_BLK + N_EXPERTS
    first_slot = jnp.arange(nblk, dtype=jnp.int32) * MOE_BLK
    block_expert = jnp.minimum(
        jnp.sum((pends[None, :] <= first_slot[:, None]).astype(jnp.int32), axis=1), N_EXPERTS - 1)
    nused = (pends[-1] // MOE_BLK).astype(jnp.int32).reshape(1)
    seg_end = jnp.sum(jnp.where(block_expert[:, None] == jnp.arange(N_EXPERTS)[None, :],
                                (pstarts + counts)[None, :], 0), axis=1)
    block_valid = jnp.where(first_slot < pends[-1], jnp.clip(seg_end - first_slot, 0, MOE_BLK), 0)
    return (dest, pstarts.astype(jnp.int32), counts, block_expert.astype(jnp.int32), nused,
            block_valid.astype(jnp.int32))


def kernel(x, c, ctx, c_ctx, w_ada, b_ada, g_norm1, w_in, w_fmix, rpb, g_out, w_out, g_norm2,
           w_router_group, w_router_expert, w_gate, w_up, w_down, g_final):
    b, n, d = x.shape
    assert (b, n, d) == (c.shape[0], SEQ, D_MODEL) and w_ada.shape[0] == 1
    nt = b * n

    cond8 = jnp.zeros((8, d), F32).at[0:b].set(c).at[b].set(c_ctx)
    mod = adaln(cond8, w_ada[0], b_ada[0])
    mods = jnp.pad(mod[0:b].reshape(b, N_MOD, d), ((0, 0), (0, 2), (0, 0)))
    mod_ctx = jnp.pad(mod[b].reshape(N_MOD, d), ((0, 2), (0, 0)))

    w_in_b = w_in[0].astype(BF16)
    qr, qp, kr, v, a = in_proj(x, mods, g_norm1[0], w_in_b)
    kc, vc = ctx_proj(ctx, mod_ctx, g_norm1[0], w_in_b[:, D_FOURIER + D_NA:])

    fr = dft_cols(dft_rows(a))
    na = attention(qr, qp, kr, v, kc, vc, bias_tables(rpb[0]))

    w_router = jnp.concatenate(
        [w_router_group[0], w_router_expert[0],
         jnp.zeros((d, LANES - N_GROUPS - N_EXPERTS), F32)], axis=1).astype(BF16)
    x1, h2_tiles, meta, metat, cnt = out_proj(fr, na, x, mods, w_fmix[0].astype(BF16),
                                              w_out[0].astype(BF16), g_out[0], g_norm2[0], w_router)

    dest, pstarts, counts, block_expert, nused, block_valid = _dispatch_plan(metat, cnt[0], nt)
    xs_tiles = dispatch(h2_tiles, dest, pstarts, counts, block_expert.shape[0] * MOE_BLK)
    ys_tiles = experts(xs_tiles, block_expert, nused, block_valid, w_gate[0], w_up[0], w_down[0])
    out = combine(dest, ys_tiles, x1.reshape(nt, d), meta, mods, g_final, n)
    return out.reshape(b, n, d)
```

```python
import functools
import math

import numpy as np
import jax
import jax.numpy as jnp
from jax import lax
from jax.experimental import pallas as pl
from jax.experimental.pallas import tpu as pltpu

F32 = jnp.float32
BF16 = jnp.bfloat16

D_MODEL = 1024
GRID_W = 64
GRID_H = 128
SEQ = GRID_W * GRID_H
CTX_LEN = 256
D_FOURIER = 256
FOURIER_GROUP = 64
HEAD_DIM = 64
N_HEADS = 12
D_NA = N_HEADS * HEAD_DIM
N_PAIRS = N_HEADS // 2
NA_ROWS = 8
NA_COLS = 16
ROPE_THETA = 10000.0
N_GROUPS = 4
EXPERTS_PER_GROUP = 8
N_EXPERTS = N_GROUPS * EXPERTS_PER_GROUP
D_EXPERT = 512
N_MOD = 6
D_IN_PROJ = D_FOURIER + 3 * D_NA
EPS = 1e-6
LANES = 128
NEG = -1e30
LOG2E = math.log2(math.e)

TM_IN = 1024
ROW_PITCH = 72
TM_PROJ = 512
ATT_ROWS = 32
DFT_TW = 16
DFT_TK1 = 16
MOE_BLK = 512
MOE_CHUNK = 256
TC_COMB = 256
VMEM_LIMIT = 56 * 1024 * 1024


def _cparams(sem):
    return pltpu.CompilerParams(dimension_semantics=sem, vmem_limit_bytes=VMEM_LIMIT)


def _mxu_const(table):
    return jnp.asarray(table, F32).astype(BF16)


@functools.lru_cache(maxsize=None)
def _rope_tables():
    t = np.arange(SEQ)
    row, col = t // GRID_W, t % GRID_W
    lane = np.arange(LANES)
    d = lane % HEAD_DIM
    chunk = d // 32
    e = d % 32
    j = e % 16
    inv = ROPE_THETA ** (-(j.astype(np.float64)) / 16.0)
    pos = np.where(chunk[None, :] == 0, row[:, None], col[:, None]).astype(np.float64)
    ang = pos * inv[None, :]
    cos = np.cos(ang)
    sin = np.sin(ang)
    first = (e < 16)[None, :]
    s_first = np.where(first, -sin, 0.0)
    s_second = np.where(first, 0.0, sin)
    return (cos.astype(np.float32), s_first.astype(np.float32), s_second.astype(np.float32))


@functools.lru_cache(maxsize=None)
def _chan_dft():
    c = np.arange(FOURIER_GROUP)
    ang = 2.0 * np.pi * ((c[:, None] * c[None, :]) % FOURIER_GROUP) / FOURIER_GROUP
    eye = np.eye(D_FOURIER // FOURIER_GROUP)
    re = np.kron(eye, np.cos(ang))
    im = np.kron(eye, -np.sin(ang))
    return np.concatenate([re, im], axis=1).astype(np.float32)


@functools.lru_cache(maxsize=None)
def _row_dft():
    k1 = np.arange(GRID_H)[:, None]
    r = np.arange(GRID_H)[None, :]
    out = np.zeros((GRID_W, 2 * GRID_H, 2 * GRID_H), np.float32)
    for w in range(GRID_W):
        m = (k1 * (GRID_W * r + w)) % SEQ
        ang = 2.0 * np.pi * m / SEQ
        c, s = np.cos(ang), np.sin(ang)
        out[w] = np.block([[c, s], [-s, c]])
    return out


@functools.lru_cache(maxsize=None)
def _col_dft():
    k2 = np.arange(GRID_W)
    ang = 2.0 * np.pi * ((k2[:, None] * k2[None, :]) % GRID_W) / GRID_W
    scale = 1.0 / math.sqrt(SEQ * FOURIER_GROUP)
    return (np.concatenate([np.cos(ang), np.sin(ang)], axis=1) * scale).astype(np.float32)


@functools.lru_cache(maxsize=None)
def _bias_index():
    c = np.arange(GRID_W)
    start = np.clip(c - NA_COLS // 2, 0, GRID_W - NA_COLS)
    valid = (c[None, :] >= start[:, None]) & (c[None, :] < start[:, None] + NA_COLS)
    dc = np.clip(c[None, :] - c[:, None] + (NA_COLS - 1), 0, 2 * NA_COLS - 2)
    return dc.astype(np.int32), valid


@functools.lru_cache(maxsize=None)
def _strict_lower(n):
    return np.tril(np.ones((n, n), np.float32), k=-1)


def _adaln_kernel(c_ref, w_ref, b_ref, o_ref):
    c = c_ref[...]
    s = c * jax.nn.sigmoid(c)
    o_ref[...] = jnp.dot(s, w_ref[...], precision=lax.Precision.HIGHEST,
                         preferred_element_type=F32) + b_ref[...]


def adaln(cond8, w, b):
    n = w.shape[1]
    tn = 1536
    return pl.pallas_call(
        _adaln_kernel,
        grid=(n // tn,),
        in_specs=[pl.BlockSpec((8, D_MODEL), lambda j: (0, 0)),
                  pl.BlockSpec((D_MODEL, tn), lambda j: (0, j)),
                  pl.BlockSpec((1, tn), lambda j: (0, j))],
        out_specs=pl.BlockSpec((8, tn), lambda j: (0, j)),
        out_shape=jax.ShapeDtypeStruct((8, n), F32),
        compiler_params=_cparams(("arbitrary",)),
        name="adaln",
    )(cond8, w, b.reshape(1, n))


def _norm_mod(x, g, shift, scale):
    ms = jnp.mean(x * x, axis=-1, keepdims=True)
    return (x * lax.rsqrt(ms + EPS) * g) * (1.0 + scale) + shift


def _in_proj_kernel(x_ref, mod_ref, g_ref, w_ref, cs_ref, cos_ref, s1_ref, s2_ref,
                    qr_ref, qp_ref, kr_ref, v_ref, a_ref, h_scr, a_scr):
    m = mod_ref[0]
    h_scr[...] = _norm_mod(x_ref[0], g_ref[...], m[0:1], m[1:2]).astype(BF16)
    cos, s1, s2 = cos_ref[...], s1_ref[...], s2_ref[...]

    def rope(t):
        return (t * cos + pltpu.roll(t, LANES - 16, axis=1) * s1 + pltpu.roll(t, 16, axis=1) * s2)

    f = jnp.dot(h_scr[...], w_ref[:, 0:D_FOURIER], preferred_element_type=F32)
    a = jnp.dot(f.astype(BF16), cs_ref[...], preferred_element_type=F32)
    n_planes = 2 * D_FOURIER // LANES
    tile_rows = a.shape[0] // GRID_W
    for p in range(n_planes):
        for r in range(tile_rows):
            a_scr[p, r * ROW_PITCH:r * ROW_PITCH + GRID_W, :] = a[r * GRID_W:(r + 1) * GRID_W,
                                                                LANES * p:LANES * (p + 1)]
    for w in range(GRID_W):
        for p in range(n_planes):
            slab = a_scr[p, pl.ds(w, tile_rows, stride=ROW_PITCH), :]
            lo = w * D_FOURIER + (p % 2) * LANES
            a_ref[0, p // 2, :, lo:lo + LANES] = slab.astype(BF16)

    scale = HEAD_DIM ** -0.5 * LOG2E
    for c in range(D_NA // 256):
        lo = D_FOURIER + 256 * c
        q = jnp.dot(h_scr[...], w_ref[:, lo:lo + 256], preferred_element_type=F32)
        k = jnp.dot(h_scr[...], w_ref[:, lo + D_NA:lo + D_NA + 256], preferred_element_type=F32)
        v = jnp.dot(h_scr[...], w_ref[:, lo + 2 * D_NA:lo + 2 * D_NA + 256], preferred_element_type=F32)
        v_ref[0, :, 256 * c:256 * c + 256] = v.astype(BF16)
        for s in range(2):
            sl = slice(LANES * s, LANES * (s + 1))
            ol = slice(256 * c + LANES * s, 256 * c + LANES * (s + 1))
            qs, ks = q[:, sl], k[:, sl]
            qp_ref[0, :, ol] = (qs * scale).astype(BF16)
            qr_ref[0, :, ol] = (rope(qs) * scale).astype(BF16)
            kr_ref[0, :, ol] = rope(ks).astype(BF16)


def in_proj(x, mods, g1, w_in_bf16):
    b, n, d = x.shape
    tm = TM_IN
    cos, s1, s2 = _rope_tables()
    cs = _mxu_const(_chan_dft())
    tok = lambda bi, i: (bi, i, 0)
    const2 = lambda bi, i: (0, 0)
    tab = pl.BlockSpec((tm, LANES), lambda bi, i: (i, 0))
    qkv_shape = jax.ShapeDtypeStruct((b, n, D_NA), BF16)
    qkv_spec = pl.BlockSpec((1, tm, D_NA), tok)
    return pl.pallas_call(
        _in_proj_kernel,
        grid=(b, n // tm),
        in_specs=[pl.BlockSpec((1, tm, d), tok),
                  pl.BlockSpec((1, 8, d), lambda bi, i: (bi, 0, 0)),
                  pl.BlockSpec((1, d), const2),
                  pl.BlockSpec((d, D_IN_PROJ), const2),
                  pl.BlockSpec((D_FOURIER, 2 * D_FOURIER), const2),
                  tab, tab, tab],
        out_specs=[qkv_spec, qkv_spec, qkv_spec, qkv_spec,
                   pl.BlockSpec((1, 2, tm // GRID_W, GRID_W * D_FOURIER), lambda bi, i: (bi, 0, i, 0))],
        out_shape=[qkv_shape, qkv_shape, qkv_shape, qkv_shape,
                   jax.ShapeDtypeStruct((b, 2, n // GRID_W, GRID_W * D_FOURIER), BF16)],
        scratch_shapes=[pltpu.VMEM((tm, d), BF16),
                        pltpu.VMEM((2 * D_FOURIER // LANES, (tm // GRID_W) * ROW_PITCH, LANES), F32)],
        compiler_params=_cparams(("arbitrary", "arbitrary")),
        name="in_proj",
    )(x, mods, g1.reshape(1, d), w_in_bf16, cs, jnp.asarray(cos), jnp.asarray(s1), jnp.asarray(s2))


def _ctx_proj_kernel(x_ref, mod_ref, g_ref, w_ref, k_ref, v_ref):
    m = mod_ref[...]
    h = _norm_mod(x_ref[0], g_ref[...], m[0:1], m[1:2]).astype(BF16)
    k_ref[0] = jnp.dot(h, w_ref[:, 0:D_NA], preferred_element_type=F32).astype(BF16)
    v_ref[0] = jnp.dot(h, w_ref[:, D_NA:2 * D_NA], preferred_element_type=F32).astype(BF16)


def ctx_proj(ctx, mod_ctx, g1, w_kv_bf16):
    b, l, d = ctx.shape
    shape = jax.ShapeDtypeStruct((b, l, D_NA), BF16)
    spec = pl.BlockSpec((1, l, D_NA), lambda bi: (bi, 0, 0))
    return pl.pallas_call(
        _ctx_proj_kernel,
        grid=(b,),
        in_specs=[pl.BlockSpec((1, l, d), lambda bi: (bi, 0, 0)),
                  pl.BlockSpec((8, d), lambda bi: (0, 0)),
                  pl.BlockSpec((1, d), lambda bi: (0, 0)),
                  pl.BlockSpec((d, 2 * D_NA), lambda bi: (0, 0))],
        out_specs=[spec, spec],
        out_shape=[shape, shape],
        compiler_params=_cparams(("arbitrary",)),
        name="ctx_proj",
    )(ctx, mod_ctx, g1.reshape(1, d), w_kv_bf16)


N_PLANES = D_FOURIER // LANES
K1_PITCH = GRID_H + 8
K2_PITCH = GRID_W + 8


def _dft_rows_kernel(a_ref, g_ref, z_ref, z_scr):
    for j in range(DFT_TW):
        sl = slice(D_FOURIER * j, D_FOURIER * (j + 1))
        rhs = jnp.concatenate([a_ref[0, 0, :, sl], a_ref[0, 1, :, sl]], axis=0)
        z = jnp.dot(g_ref[j], rhs, preferred_element_type=F32)
        for c in range(2):
            for p in range(N_PLANES):
                z_scr[c * N_PLANES + p, j * K1_PITCH:j * K1_PITCH + GRID_H, :] = (
                    z[c * GRID_H:(c + 1) * GRID_H, LANES * p:LANES * (p + 1)])
    for k1 in range(GRID_H):
        for c in range(2):
            for p in range(N_PLANES):
                slab = z_scr[c * N_PLANES + p, pl.ds(k1, DFT_TW, stride=K1_PITCH), :]
                lo = k1 * D_FOURIER + p * LANES
                z_ref[0, c, :, lo:lo + LANES] = slab.astype(BF16)


def dft_rows(a):
    b = a.shape[0]
    g = _mxu_const(_row_dft())
    return pl.pallas_call(
        _dft_rows_kernel,
        grid=(GRID_W // DFT_TW, b),
        in_specs=[pl.BlockSpec((1, 2, GRID_H, DFT_TW * D_FOURIER), lambda j, bi: (bi, 0, 0, j)),
                  pl.BlockSpec((DFT_TW, 2 * GRID_H, 2 * GRID_H), lambda j, bi: (j, 0, 0))],
        out_specs=pl.BlockSpec((1, 2, DFT_TW, GRID_H * D_FOURIER), lambda j, bi: (bi, 0, j, 0)),
        out_shape=jax.ShapeDtypeStruct((b, 2, GRID_W, GRID_H * D_FOURIER), BF16),
        scratch_shapes=[pltpu.VMEM((2 * N_PLANES, DFT_TW * K1_PITCH, LANES), F32)],
        compiler_params=_cparams(("arbitrary", "arbitrary")),
        name="dft_rows",
    )(a, g)


def _dft_cols_kernel(z_ref, cs_ref, o_ref, o_scr):
    for j in range(DFT_TK1):
        sl = slice(D_FOURIER * j, D_FOURIER * (j + 1))
        rhs = jnp.concatenate([z_ref[0, 0, :, sl], z_ref[0, 1, :, sl]], axis=0)
        out = jnp.dot(cs_ref[...], rhs, preferred_element_type=F32)
        for p in range(N_PLANES):
            o_scr[p, j * K2_PITCH:j * K2_PITCH + GRID_W, :] = out[:, LANES * p:LANES * (p + 1)]
    for k2 in range(GRID_W):
        for p in range(N_PLANES):
            slab = o_scr[p, pl.ds(k2, DFT_TK1, stride=K2_PITCH), :]
            o_ref[0, k2, :, LANES * p:LANES * (p + 1)] = slab.astype(BF16)


def dft_cols(z):
    b = z.shape[0]
    cs = _mxu_const(_col_dft())
    out = pl.pallas_call(
        _dft_cols_kernel,
        grid=(b, GRID_H // DFT_TK1),
        in_specs=[pl.BlockSpec((1, 2, GRID_W, DFT_TK1 * D_FOURIER), lambda bi, j: (bi, 0, 0, j)),
                  pl.BlockSpec((GRID_W, 2 * GRID_W), lambda bi, j: (0, 0))],
        out_specs=pl.BlockSpec((1, GRID_W, DFT_TK1, D_FOURIER), lambda bi, j: (bi, 0, j, 0)),
        out_shape=jax.ShapeDtypeStruct((b, GRID_W, GRID_H, D_FOURIER), BF16),
        scratch_shapes=[pltpu.VMEM((N_PLANES, DFT_TK1 * K2_PITCH, LANES), F32)],
        compiler_params=_cparams(("arbitrary", "arbitrary")),
        name="dft_cols",
    )(z, cs)
    return out.reshape(b, SEQ, D_FOURIER)


WIN_ROWS = NA_ROWS + 2
WIN_KEYS = WIN_ROWS * GRID_W
WIN_TILES = WIN_KEYS // LANES
N_DR = 2 * NA_ROWS + 1
SOFT_ROWS = 32
ATT_PAIRS = ATT_ROWS // 2


def _attention_kernel(qr_ref, qp_ref, k_ref, v_ref, kc_ref, vc_ref, tb_ref, o_ref,
                      sc_scr, pc_scr, s_scr, p_scr):
    rb = pl.program_id(2)
    first2 = lax.broadcasted_iota(jnp.int32, (2 * GRID_W, LANES), 1) < HEAD_DIM
    nt = (((1,), (1,)), ((), ()))

    def window(jp):
        r0 = rb * ATT_ROWS + 2 * jp
        ws = jnp.clip(r0 - NA_ROWS // 2, 0, GRID_H - WIN_ROWS)
        return r0, ws, pl.multiple_of(ws * GRID_W, GRID_W)

    def split_heads(q2):
        z2 = jnp.zeros_like(q2)
        return jnp.concatenate([jnp.where(first2, q2, z2), jnp.where(first2, z2, q2)], axis=0)

    def scores(jp):
        _, _, koff = window(jp)
        qs = slice(jp * 2 * GRID_W, (jp + 1) * 2 * GRID_W)
        kw = k_ref[0, pl.ds(koff, WIN_KEYS), :]
        s_scr[jp] = lax.dot_general(split_heads(qr_ref[0, qs, :]), kw, nt, preferred_element_type=F32)
        sc_scr[jp] = lax.dot_general(split_heads(qp_ref[0, qs, :]), kc_ref[0], nt, preferred_element_type=F32)

    def softmax(jp):
        r0, ws, _ = window(jp)
        for c in range(4):
            hh, i = c // 2, c % 2
            r = r0 + i
            off = jnp.clip(r - NA_ROWS // 2, 0, GRID_H - NA_ROWS) - ws
            sel = []
            for t in range(WIN_TILES):
                v0 = (2 * t >= off) & (2 * t < off + NA_ROWS)
                v1 = (2 * t + 1 >= off) & (2 * t + 1 < off + NA_ROWS)
                kind = jnp.where(v0 & v1, 0, jnp.where(v0, 2, 1))
                sel.append((kind, jnp.where(v0 | v1, ws + 2 * t - r + NA_ROWS, N_DR - 1)))
            for h in range(GRID_W // SOFT_ROWS):
                lo = h * SOFT_ROWS
                rows = slice(c * GRID_W + lo, c * GRID_W + lo + SOFT_ROWS)
                tiles = [s_scr[jp, rows, t * LANES:(t + 1) * LANES]
                         + tb_ref[0, hh, sel[t][0], sel[t][1], lo:lo + SOFT_ROWS, :] for t in range(WIN_TILES)]
                sc = sc_scr[jp, rows, :]
                mt = jnp.maximum(sc[:, :LANES], sc[:, LANES:])
                for tl in tiles:
                    mt = jnp.maximum(mt, tl)
                m = jnp.max(mt, axis=1, keepdims=True)
                pc_scr[jp, rows, :] = jnp.exp2(sc - m).astype(BF16)
                for t, tl in enumerate(tiles):
                    p_scr[jp, rows, t * LANES:(t + 1) * LANES] = jnp.exp2(tl - m).astype(BF16)

    def weighted_values(jp):
        _, _, koff = window(jp)
        vw = jnp.concatenate([v_ref[0, pl.ds(koff, WIN_KEYS), :], jnp.ones((WIN_KEYS, LANES), BF16)], axis=1)
        vc = jnp.concatenate([vc_ref[0], jnp.ones((CTX_LEN, LANES), BF16)], axis=1)
        o = (jnp.dot(p_scr[jp], vw, preferred_element_type=F32)
             + jnp.dot(pc_scr[jp], vc, preferred_element_type=F32))
        oa, ob = o[:2 * GRID_W], o[2 * GRID_W:]
        out = jnp.where(first2, oa[:, :LANES] / oa[:, LANES:], ob[:, :LANES] / ob[:, LANES:])
        o_ref[0, jp * 2 * GRID_W:(jp + 1) * 2 * GRID_W, :] = out.astype(BF16)

    scores(0)
    scores(1)
    softmax(0)
    for jp in range(ATT_PAIRS):
        if jp + 2 < ATT_PAIRS:
            scores(jp + 2)
        if jp + 1 < ATT_PAIRS:
            softmax(jp + 1)
        weighted_values(jp)


def bias_tables(rpb):
    dc, valid = _bias_index()
    n_dc = 2 * NA_COLS - 1
    onehot = (dc.reshape(1, -1) == np.arange(n_dc).reshape(-1, 1)).astype(np.float32)
    t = jnp.dot(rpb.reshape(-1, n_dc), jnp.asarray(onehot), precision=lax.Precision.HIGHEST)
    t = jnp.where(valid[None, None], LOG2E * t.reshape(N_HEADS, 2 * NA_ROWS - 1, GRID_W, GRID_W), NEG)
    t = jnp.pad(t, ((0, 0), (1, 2), (0, 0), (0, 0)), constant_values=NEG)
    tb = pl.pallas_call(
        _bias_pairs_kernel,
        grid=(N_HEADS,),
        in_specs=[pl.BlockSpec((1, N_DR + 1, GRID_W, GRID_W), lambda h: (h, 0, 0, 0))],
        out_specs=pl.BlockSpec((1, 3, N_DR, GRID_W, LANES), lambda h: (h, 0, 0, 0, 0)),
        out_shape=jax.ShapeDtypeStruct((N_HEADS, 3, N_DR, GRID_W, LANES), F32),
        compiler_params=_cparams(("arbitrary",)),
        name="bias_pairs",
    )(t)
    return tb.reshape(N_PAIRS, 2, 3, N_DR, GRID_W, LANES)


def _bias_pairs_kernel(t_ref, o_ref):
    neg = jnp.full((GRID_W, GRID_W), NEG, F32)
    for d in range(N_DR):
        left, right = t_ref[0, d], t_ref[0, d + 1]
        o_ref[0, 0, d] = jnp.concatenate([left, right], axis=1)
        o_ref[0, 1, d] = jnp.concatenate([neg, right], axis=1)
        o_ref[0, 2, d] = jnp.concatenate([left, neg], axis=1)


def attention(qr, qp, kr, v, kc, vc, tb):
    b, n, _ = qr.shape
    tq = ATT_ROWS * GRID_W
    qspec = pl.BlockSpec((1, tq, LANES), lambda bi, hp, i: (bi, i, hp))
    kspec = pl.BlockSpec((1, n, LANES), lambda bi, hp, i: (bi, 0, hp))
    cspec = pl.BlockSpec((1, CTX_LEN, LANES), lambda bi, hp, i: (bi, 0, hp))
    return pl.pallas_call(
        _attention_kernel,
        grid=(b, N_PAIRS, GRID_H // ATT_ROWS),
        in_specs=[qspec, qspec, kspec, kspec, cspec, cspec,
                  pl.BlockSpec((1, 2, 3, N_DR, GRID_W, LANES), lambda bi, hp, i: (hp, 0, 0, 0, 0, 0))],
        out_specs=qspec,
        out_shape=jax.ShapeDtypeStruct((b, n, D_NA), BF16),
        scratch_shapes=[pltpu.VMEM((ATT_PAIRS, 4 * GRID_W, CTX_LEN), F32),
                        pltpu.VMEM((ATT_PAIRS, 4 * GRID_W, CTX_LEN), BF16),
                        pltpu.VMEM((ATT_PAIRS, 4 * GRID_W, WIN_KEYS), F32),
                        pltpu.VMEM((ATT_PAIRS, 4 * GRID_W, WIN_KEYS), BF16)],
        compiler_params=_cparams(("arbitrary", "arbitrary", "arbitrary")),
        name="attention",
    )(qr, qp, kr, v, kc, vc, tb)


def _rms(x, g):
    ms = jnp.mean(x * x, axis=-1, keepdims=True)
    return x * lax.rsqrt(ms + EPS) * g


TILE_ROWS = D_MODEL // LANES


def _store_token_tiles(ref, val, start=0):
    rows = val.shape[0]
    for j in range(TILE_ROWS):
        ref[pl.ds(start + j, rows, stride=TILE_ROWS), :] = val[:, LANES * j:LANES * (j + 1)]


def _load_token_tiles(ref, start, rows):
    return jnp.concatenate(
        [ref[pl.ds(start + j, rows, stride=TILE_ROWS), :] for j in range(TILE_ROWS)], axis=1)


def _out_proj_kernel(fr_ref, na_ref, x_ref, mod_ref, wf_ref, wo_ref, go_ref, g2_ref, wr_ref, tri_ref,
                     x1_ref, h2_ref, meta_ref, metat_ref, cnt_ref, run_scr):
    @pl.when((pl.program_id(0) == 0) & (pl.program_id(1) == 0))
    def _():
        run_scr[...] = jnp.zeros_like(run_scr)

    m = mod_ref[0]
    go = go_ref[...]
    fo = jnp.dot(fr_ref[0], wf_ref[...], preferred_element_type=F32)
    fn = _rms(fo, go[:, :D_FOURIER]).astype(BF16)
    nn = _rms(na_ref[0].astype(F32), go[:, D_FOURIER:]).astype(BF16)
    y = (jnp.dot(fn, wo_ref[0:D_FOURIER, :], preferred_element_type=F32)
         + jnp.dot(nn, wo_ref[D_FOURIER:, :], preferred_element_type=F32))
    x1 = x_ref[0] + m[2:3] * y
    x1_ref[0] = x1
    h2 = _norm_mod(x1, g2_ref[...], m[3:4], m[4:5])
    _store_token_tiles(h2_ref, h2)
    logits = jnp.dot(h2.astype(BF16), wr_ref[...], preferred_element_type=F32)

    tm = logits.shape[0]
    lane = lax.broadcasted_iota(jnp.int32, (tm, LANES), 1).astype(F32)
    ninf = jnp.float32(-jnp.inf)

    def argmax_first(vals):
        mx = jnp.max(vals, axis=1, keepdims=True)
        idx = jnp.min(jnp.where(vals == mx, lane, float(LANES)), axis=1, keepdims=True)
        return mx, idx

    lg = jnp.where(lane < N_GROUPS, logits, ninf)
    gmax, gidx = argmax_first(lg)
    pg = 1.0 / jnp.sum(jnp.exp(lg - gmax), axis=1, keepdims=True)
    lo = N_GROUPS + EXPERTS_PER_GROUP * gidx
    le = jnp.where((lane >= lo) & (lane < lo + EXPERTS_PER_GROUP), logits, ninf)
    e1, i1 = argmax_first(le)
    e2, i2 = argmax_first(jnp.where(lane == i1, ninf, le))
    dd = jnp.exp(e2 - e1)
    gate1 = pg / (1.0 + dd)
    gate2 = pg * dd / (1.0 + dd)

    hot1 = lane == i1
    hot2 = lane == i2
    onehot = jnp.where(hot1 | hot2, 1.0, 0.0)
    cnt = jnp.dot(tri_ref[...], onehot.astype(BF16), preferred_element_type=F32) + run_scr[...]
    rank1 = jnp.sum(jnp.where(hot1, cnt, 0.0), axis=1, keepdims=True)
    rank2 = jnp.sum(jnp.where(hot2, cnt, 0.0), axis=1, keepdims=True)
    run_scr[...] = run_scr[...] + jnp.sum(onehot, axis=0, keepdims=True)

    meta = jnp.where(lane == 0, i1 - N_GROUPS,
           jnp.where(lane == 1, i2 - N_GROUPS,
           jnp.where(lane == 2, rank1,
           jnp.where(lane == 3, rank2,
           jnp.where(lane == 4, gate1,
           jnp.where(lane == 5, gate2, 0.0))))))
    meta_ref[...] = meta
    metat_ref[...] = jnp.transpose(meta)[0:8, :]
    cnt_ref[...] = jnp.broadcast_to(run_scr[...], cnt_ref.shape)


def out_proj(fr, na, x, mods, w_fmix_bf16, w_out_bf16, g_out, g2, w_router_bf16):
    b, n, d = x.shape
    tm = TM_PROJ
    steps = n // tm
    tok = lambda bi, i: (bi, i, 0)
    const2 = lambda bi, i: (0, 0)
    flat = lambda bi, i: (bi * steps + i, 0)
    tri = _mxu_const(_strict_lower(tm))
    return pl.pallas_call(
        _out_proj_kernel,
        grid=(b, steps),
        in_specs=[pl.BlockSpec((1, tm, D_FOURIER), tok),
                  pl.BlockSpec((1, tm, D_NA), tok),
                  pl.BlockSpec((1, tm, d), tok),
                  pl.BlockSpec((1, 8, d), lambda bi, i: (bi, 0, 0)),
                  pl.BlockSpec((D_FOURIER, D_FOURIER), const2),
                  pl.BlockSpec((d, d), const2),
                  pl.BlockSpec((1, d), const2),
                  pl.BlockSpec((1, d), const2),
                  pl.BlockSpec((d, LANES), const2),
                  pl.BlockSpec((tm, tm), const2)],
        out_specs=[pl.BlockSpec((1, tm, d), tok),
                   pl.BlockSpec((tm * TILE_ROWS, LANES), flat),
                   pl.BlockSpec((tm, LANES), flat),
                   pl.BlockSpec((8, tm), lambda bi, i: (0, bi * steps + i)),
                   pl.BlockSpec((8, LANES), const2)],
        out_shape=[jax.ShapeDtypeStruct((b, n, d), F32),
                   jax.ShapeDtypeStruct((b * n * TILE_ROWS, LANES), F32),
                   jax.ShapeDtypeStruct((b * n, LANES), F32),
                   jax.ShapeDtypeStruct((8, b * n), F32),
                   jax.ShapeDtypeStruct((8, LANES), F32)],
        scratch_shapes=[pltpu.VMEM((1, LANES), F32)],
        compiler_params=_cparams(("arbitrary", "arbitrary")),
        name="out_proj",
    )(fr, na, x, mods, w_fmix_bf16, w_out_bf16, g_out.reshape(1, d), g2.reshape(1, d), w_router_bf16, tri)


DMA_CHUNK = 8


TD_DISP = 1024


def _dispatch_kernel(dest_ref, pstart_ref, count_ref, h2_ref, xs_hbm, zero_scr, sem, pad_sem):
    i = pl.program_id(0)
    nt = dest_ref.shape[0] // 2

    @pl.when(i == 0)
    def _():
        zero_scr[...] = jnp.zeros_like(zero_scr)

        def per_expert(e, npad):
            lo = pstart_ref[e] + count_ref[e]
            mid = pstart_ref[e] + ((count_ref[e] + MOE_CHUNK - 1) // MOE_CHUNK) * MOE_CHUNK
            hi = pstart_ref[e] + ((count_ref[e] + MOE_BLK - 1) // MOE_BLK) * MOE_BLK

            def fill(s, carry):
                pltpu.make_async_copy(zero_scr.at[pl.ds(0, TILE_ROWS)],
                                      xs_hbm.at[pl.ds(s * TILE_ROWS, TILE_ROWS)], pad_sem).start()
                return carry

            def fill_chunk(s, carry):
                pltpu.make_async_copy(zero_scr.at[pl.ds(0, MOE_CHUNK * TILE_ROWS)],
                                      xs_hbm.at[pl.ds(mid * TILE_ROWS + s * MOE_CHUNK * TILE_ROWS,
                                                      MOE_CHUNK * TILE_ROWS)], pad_sem).start()
                return carry

            lax.fori_loop(lo, mid, fill, 0)
            lax.fori_loop(0, (hi - mid) // MOE_CHUNK, fill_chunk, 0)
            return npad + (hi - lo)

        npad = lax.fori_loop(0, N_EXPERTS, per_expert, 0)

        blk_rows = MOE_BLK * TILE_ROWS
        first_free = (pstart_ref[N_EXPERTS - 1] + count_ref[N_EXPERTS - 1] + MOE_BLK - 1) // MOE_BLK
        n_blocks = xs_hbm.shape[0] // blk_rows

        def fill_block(bk, carry):
            pltpu.make_async_copy(zero_scr, xs_hbm.at[pl.ds(bk * blk_rows, blk_rows)], pad_sem).start()
            return carry

        lax.fori_loop(first_free, n_blocks, fill_block, 0)
        rows = npad * TILE_ROWS + (n_blocks - first_free) * blk_rows

        @pl.when(rows > 0)
        def _():
            pltpu.make_async_copy(xs_hbm.at[pl.ds(0, rows)], xs_hbm.at[pl.ds(0, rows)], pad_sem).wait()

    def issue(c, carry):
        t0 = c * DMA_CHUNK
        for u in range(DMA_CHUNK):
            for k in range(2):
                d = dest_ref[k * nt + i * TD_DISP + t0 + u]
                pltpu.make_async_copy(h2_ref.at[pl.ds((t0 + u) * TILE_ROWS, TILE_ROWS)],
                                      xs_hbm.at[pl.ds(d * TILE_ROWS, TILE_ROWS)], sem).start(priority=k)
        return carry

    lax.fori_loop(0, TD_DISP // DMA_CHUNK, issue, 0)
    rows = 2 * TD_DISP * TILE_ROWS
    pltpu.make_async_copy(xs_hbm.at[pl.ds(0, rows)], xs_hbm.at[pl.ds(0, rows)], sem).wait()


def dispatch(h2_tiles, dest, pstarts, counts, n_slots):
    nt = dest.shape[0] // 2
    grid_spec = pltpu.PrefetchScalarGridSpec(
        num_scalar_prefetch=3,
        grid=(nt // TD_DISP,),
        in_specs=[pl.BlockSpec((TD_DISP * TILE_ROWS, LANES), lambda i, ds, ps, ct: (i, 0))],
        out_specs=pl.BlockSpec(memory_space=pl.ANY),
        scratch_shapes=[pltpu.VMEM((MOE_BLK * TILE_ROWS, LANES), F32),
                        pltpu.SemaphoreType.DMA(()),
                        pltpu.SemaphoreType.DMA(())],
    )
    return pl.pallas_call(
        _dispatch_kernel,
        grid_spec=grid_spec,
        out_shape=jax.ShapeDtypeStruct((n_slots * TILE_ROWS, LANES), F32),
        compiler_params=_cparams(("arbitrary",)),
        name="dispatch",
    )(dest, pstarts, counts, h2_tiles)


def _experts_kernel(be_ref, nused_ref, valid_ref, xs_ref, wg_ref, wu_ref, wd_ref, ys_ref,
                    wg_scr, wu_scr, wd_scr):
    i = pl.program_id(0)
    valid = valid_ref[i]
    changed = (i == 0) | (be_ref[i] != be_ref[jnp.maximum(i - 1, 0)])
    chunk_rows = MOE_CHUNK * TILE_ROWS

    @pl.when(changed & (valid > 0))
    def _():
        wg_scr[...] = wg_ref[0].astype(BF16)
        wu_scr[...] = wu_ref[0].astype(BF16)
        wd_scr[...] = wd_ref[0].astype(BF16)

    def run(n_chunks):
        hmids = []
        for h in range(n_chunks):
            x = _load_token_tiles(xs_ref, h * chunk_rows, MOE_CHUNK).astype(BF16)
            g = jnp.dot(x, wg_scr[...], preferred_element_type=F32)
            u = jnp.dot(x, wu_scr[...], preferred_element_type=F32)
            hmids.append((g * jax.nn.sigmoid(g) * u).astype(BF16))
        for h, hmid in enumerate(hmids):
            _store_token_tiles(ys_ref, jnp.dot(hmid, wd_scr[...], preferred_element_type=F32), h * chunk_rows)
        if n_chunks * chunk_rows < ys_ref.shape[0]:
            ys_ref[n_chunks * chunk_rows:, :] = jnp.zeros((ys_ref.shape[0] - n_chunks * chunk_rows, LANES), F32)

    for n_chunks in range(MOE_BLK // MOE_CHUNK + 1):
        lo, hi = (n_chunks - 1) * MOE_CHUNK, n_chunks * MOE_CHUNK
        pl.when((valid > lo) & (valid <= hi))(functools.partial(run, n_chunks))


def experts(xs_tiles, block_expert, nused, block_valid, w_gate, w_up, w_down):
    d = D_MODEL
    nblk = block_expert.shape[0]
    blk_rows = MOE_BLK * TILE_ROWS
    wmap = lambda i, be, nu, bv: (be[i], 0, 0)
    grid_spec = pltpu.PrefetchScalarGridSpec(
        num_scalar_prefetch=3,
        grid=(nblk,),
        in_specs=[pl.BlockSpec((blk_rows, LANES), lambda i, be, nu, bv: (jnp.minimum(i, nu[0] - 1), 0)),
                  pl.BlockSpec((1, d, D_EXPERT), wmap),
                  pl.BlockSpec((1, d, D_EXPERT), wmap),
                  pl.BlockSpec((1, D_EXPERT, d), wmap)],
        out_specs=pl.BlockSpec((blk_rows, LANES), lambda i, be, nu, bv: (i, 0)),
        scratch_shapes=[pltpu.VMEM((d, D_EXPERT), BF16),
                        pltpu.VMEM((d, D_EXPERT), BF16),
                        pltpu.VMEM((D_EXPERT, d), BF16)],
    )
    return pl.pallas_call(
        _experts_kernel,
        grid_spec=grid_spec,
        out_shape=jax.ShapeDtypeStruct((nblk * blk_rows, LANES), F32),
        compiler_params=_cparams(("arbitrary",)),
        name="experts",
    )(block_expert, nused, block_valid, xs_tiles, w_gate, w_up, w_down)


def _combine_kernel(dest_ref, ys_hbm, x1_ref, meta_ref, mod_ref, gf_ref, o_ref, ybuf, sem):
    i = pl.program_id(0)
    nstep = pl.num_programs(0)
    tc = TC_COMB
    nt = dest_ref.shape[0] // 2
    half_rows = tc * TILE_ROWS
    buf_rows = 2 * half_rows

    def gather(step, slot):
        def issue(c, carry):
            t0 = c * DMA_CHUNK
            for u in range(DMA_CHUNK):
                for k in range(2):
                    d = dest_ref[k * nt + step * tc + t0 + u]
                    pltpu.make_async_copy(
                        ys_hbm.at[pl.ds(d * TILE_ROWS, TILE_ROWS)],
                        ybuf.at[pl.ds(slot * buf_rows + k * half_rows + (t0 + u) * TILE_ROWS, TILE_ROWS)],
                        sem.at[slot]).start(priority=k)
            return carry

        lax.fori_loop(0, tc // DMA_CHUNK, issue, 0)

    @pl.when(i == 0)
    def _():
        gather(0, 0)

    @pl.when(i + 1 < nstep)
    def _():
        gather(i + 1, (i + 1) % 2)

    slot = i % 2
    start = pl.multiple_of(slot * buf_rows, buf_rows)
    pltpu.make_async_copy(ys_hbm.at[pl.ds(0, buf_rows)], ybuf.at[pl.ds(start, buf_rows)], sem.at[slot]).wait()
    meta = meta_ref[...]
    y0 = _load_token_tiles(ybuf, start, tc)
    y1 = _load_token_tiles(ybuf, start + half_rows, tc)
    moe = y0 * meta[:, 4:5] + y1 * meta[:, 5:6]
    x2 = x1_ref[...] + mod_ref[0][5:6] * moe
    o_ref[...] = _rms(x2, gf_ref[...])


def combine(dest_flat, ys, x1_flat, meta, mods, g_final, n_per_batch):
    nt, d = x1_flat.shape
    tc = TC_COMB
    per_b = n_per_batch // tc
    grid_spec = pltpu.PrefetchScalarGridSpec(
        num_scalar_prefetch=1,
        grid=(nt // tc,),
        in_specs=[pl.BlockSpec(memory_space=pl.ANY),
                  pl.BlockSpec((tc, d), lambda i, ds: (i, 0)),
                  pl.BlockSpec((tc, LANES), lambda i, ds: (i, 0)),
                  pl.BlockSpec((1, 8, d), lambda i, ds: (i // per_b, 0, 0)),
                  pl.BlockSpec((1, d), lambda i, ds: (0, 0))],
        out_specs=pl.BlockSpec((tc, d), lambda i, ds: (i, 0)),
        scratch_shapes=[pltpu.VMEM((2 * 2 * tc * TILE_ROWS, LANES), F32),
                        pltpu.SemaphoreType.DMA((2,))],
    )
    return pl.pallas_call(
        _combine_kernel,
        grid_spec=grid_spec,
        out_shape=jax.ShapeDtypeStruct((nt, d), F32),
        compiler_params=_cparams(("arbitrary",)),
        name="combine",
    )(dest_flat, ys, x1_flat, meta, mods, g_final.reshape(1, d))


def _dispatch_plan(metat, counts_row, nt):
    counts = counts_row[N_GROUPS:N_GROUPS + N_EXPERTS].astype(jnp.int32)
    pcounts = ((counts + MOE_BLK - 1) // MOE_BLK) * MOE_BLK
    pends = jnp.cumsum(pcounts)
    pstarts = pends - pcounts
    eid = metat[0:2].astype(jnp.int32).reshape(-1, LANES)
    dest = metat[2:4].astype(jnp.int32).reshape(-1, LANES)
    for e in range(N_EXPERTS):
        dest = dest + jnp.where(eid == e, pstarts[e], 0)
    dest = dest.reshape(-1)
    nblk = (nt * 2) // MOE_BLK + N_EXPERTS
    first_slot = jnp.arange(nblk, dtype=jnp.int32) * MOE_BLK
    block_expert = jnp.minimum(
        jnp.sum((pends[None, :] <= first_slot[:, None]).astype(jnp.int32), axis=1), N_EXPERTS - 1)
    nused = (pends[-1] // MOE_BLK).astype(jnp.int32).reshape(1)
    seg_end = jnp.sum(jnp.where(block_expert[:, None] == jnp.arange(N_EXPERTS)[None, :],
                                (pstarts + counts)[None, :], 0), axis=1)
    block_valid = jnp.where(first_slot < pends[-1], jnp.clip(seg_end - first_slot, 0, MOE_BLK), 0)
    return (dest, pstarts.astype(jnp.int32), counts, block_expert.astype(jnp.int32), nused,
            block_valid.astype(jnp.int32))


def kernel(x, c, ctx, c_ctx, w_ada, b_ada, g_norm1, w_in, w_fmix, rpb, g_out, w_out, g_norm2,
           w_router_group, w_router_expert, w_gate, w_up, w_down, g_final):
    b, n, d = x.shape
    assert (b, n, d) == (c.shape[0], SEQ, D_MODEL) and w_ada.shape[0] == 1
    nt = b * n

    cond8 = jnp.zeros((8, d), F32).at[0:b].set(c).at[b].set(c_ctx)
    mod = adaln(cond8, w_ada[0], b_ada[0])
    mods = jnp.pad(mod[0:b].reshape(b, N_MOD, d), ((0, 0), (0, 2), (0, 0)))
    mod_ctx = jnp.pad(mod[b].reshape(N_MOD, d), ((0, 2), (0, 0)))

    w_in_b = w_in[0].astype(BF16)
    qr, qp, kr, v, a = in_proj(x, mods, g_norm1[0], w_in_b)
    kc, vc = ctx_proj(ctx, mod_ctx, g_norm1[0], w_in_b[:, D_FOURIER + D_NA:])

    fr = dft_cols(dft_rows(a))
    na = attention(qr, qp, kr, v, kc, vc, bias_tables(rpb[0]))

    w_router = jnp.concatenate(
        [w_router_group[0], w_router_expert[0],
         jnp.zeros((d, LANES - N_GROUPS - N_EXPERTS), F32)], axis=1).astype(BF16)
    x1, h2_tiles, meta, metat, cnt = out_proj(fr, na, x, mods, w_fmix[0].astype(BF16),
                                              w_out[0].astype(BF16), g_out[0], g_norm2[0], w_router)

    dest, pstarts, counts, block_expert, nused, block_valid = _dispatch_plan(metat, cnt[0], nt)
    xs_tiles = dispatch(h2_tiles, dest, pstarts, counts, block_expert.shape[0] * MOE_BLK)
    ys_tiles = experts(xs_tiles, block_expert, nused, block_valid, w_gate[0], w_up[0], w_down[0])
    out = combine(dest, ys_tiles, x1.reshape(nt, d), meta, mods, g_final, n)
    return out.reshape(b, n, d)
```

```python
import functools
import math

import numpy as np
import jax
import jax.numpy as jnp
from jax import lax
from jax.experimental import pallas as pl
from jax.experimental.pallas import tpu as pltpu

F32 = jnp.float32
BF16 = jnp.bfloat16

D_MODEL = 1024
GRID_W = 64
GRID_H = 128
SEQ = GRID_W * GRID_H
CTX_LEN = 256
D_FOURIER = 256
FOURIER_GROUP = 64
HEAD_DIM = 64
N_HEADS = 12
D_NA = N_HEADS * HEAD_DIM
N_PAIRS = N_HEADS // 2
NA_ROWS = 8
NA_COLS = 16
ROPE_THETA = 10000.0
N_GROUPS = 4
EXPERTS_PER_GROUP = 8
N_EXPERTS = N_GROUPS * EXPERTS_PER_GROUP
D_EXPERT = 512
N_MOD = 6
D_IN_PROJ = D_FOURIER + 3 * D_NA
EPS = 1e-6
LANES = 128
NEG = -1e30
LOG2E = math.log2(math.e)

TM_IN = 1024
ROW_PITCH = 72
TM_PROJ = 1024
ATT_ROWS = 32
DFT_TW = 16
DFT_TK1 = 16
MOE_BLK = 512
MOE_CHUNK = 256
TC_COMB = 256
VMEM_LIMIT = 56 * 1024 * 1024


def _cparams(sem):
    return pltpu.CompilerParams(dimension_semantics=sem, vmem_limit_bytes=VMEM_LIMIT)


def _mxu_const(table):
    return jnp.asarray(table, F32).astype(BF16)


@functools.lru_cache(maxsize=None)
def _rope_tables():
    t = np.arange(SEQ)
    row, col = t // GRID_W, t % GRID_W
    lane = np.arange(LANES)
    d = lane % HEAD_DIM
    chunk = d // 32
    e = d % 32
    j = e % 16
    inv = ROPE_THETA ** (-(j.astype(np.float64)) / 16.0)
    pos = np.where(chunk[None, :] == 0, row[:, None], col[:, None]).astype(np.float64)
    ang = pos * inv[None, :]
    cos = np.cos(ang)
    sin = np.sin(ang)
    first = (e < 16)[None, :]
    s_first = np.where(first, -sin, 0.0)
    s_second = np.where(first, 0.0, sin)
    return (cos.astype(np.float32), s_first.astype(np.float32), s_second.astype(np.float32))


@functools.lru_cache(maxsize=None)
def _chan_dft():
    c = np.arange(FOURIER_GROUP)
    ang = 2.0 * np.pi * ((c[:, None] * c[None, :]) % FOURIER_GROUP) / FOURIER_GROUP
    eye = np.eye(D_FOURIER // FOURIER_GROUP)
    re = np.kron(eye, np.cos(ang))
    im = np.kron(eye, -np.sin(ang))
    return np.concatenate([re, im], axis=1).astype(np.float32)


@functools.lru_cache(maxsize=None)
def _row_dft():
    k1 = np.arange(GRID_H)[:, None]
    r = np.arange(GRID_H)[None, :]
    out = np.zeros((GRID_W, 2 * GRID_H, 2 * GRID_H), np.float32)
    for w in range(GRID_W):
        m = (k1 * (GRID_W * r + w)) % SEQ
        ang = 2.0 * np.pi * m / SEQ
        c, s = np.cos(ang), np.sin(ang)
        out[w] = np.block([[c, s], [-s, c]])
    return out


@functools.lru_cache(maxsize=None)
def _col_dft():
    k2 = np.arange(GRID_W)
    ang = 2.0 * np.pi * ((k2[:, None] * k2[None, :]) % GRID_W) / GRID_W
    scale = 1.0 / math.sqrt(SEQ * FOURIER_GROUP)
    return (np.concatenate([np.cos(ang), np.sin(ang)], axis=1) * scale).astype(np.float32)


@functools.lru_cache(maxsize=None)
def _bias_index():
    c = np.arange(GRID_W)
    start = np.clip(c - NA_COLS // 2, 0, GRID_W - NA_COLS)
    valid = (c[None, :] >= start[:, None]) & (c[None, :] < start[:, None] + NA_COLS)
    dc = np.clip(c[None, :] - c[:, None] + (NA_COLS - 1), 0, 2 * NA_COLS - 2)
    return dc.astype(np.int32), valid


@functools.lru_cache(maxsize=None)
def _strict_lower(n):
    return np.tril(np.ones((n, n), np.float32), k=-1)


def _adaln_kernel(c_ref, w_ref, b_ref, o_ref):
    c = c_ref[...]
    s = c * jax.nn.sigmoid(c)
    o_ref[...] = jnp.dot(s, w_ref[...], precision=lax.Precision.HIGHEST,
                         preferred_element_type=F32) + b_ref[...]


def adaln(cond8, w, b):
    n = w.shape[1]
    tn = 1536
    return pl.pallas_call(
        _adaln_kernel,
        grid=(n // tn,),
        in_specs=[pl.BlockSpec((8, D_MODEL), lambda j: (0, 0)),
                  pl.BlockSpec((D_MODEL, tn), lambda j: (0, j)),
                  pl.BlockSpec((1, tn), lambda j: (0, j))],
        out_specs=pl.BlockSpec((8, tn), lambda j: (0, j)),
        out_shape=jax.ShapeDtypeStruct((8, n), F32),
        compiler_params=_cparams(("arbitrary",)),
        name="adaln",
    )(cond8, w, b.reshape(1, n))


def _norm_mod(x, g, shift, scale):
    ms = jnp.mean(x * x, axis=-1, keepdims=True)
    return (x * lax.rsqrt(ms + EPS) * g) * (1.0 + scale) + shift


def _in_proj_kernel(x_ref, mod_ref, g_ref, w_ref, cs_ref, cos_ref, s1_ref, s2_ref,
                    qr_ref, qp_ref, kr_ref, v_ref, a_ref, h_scr, a_scr):
    m = mod_ref[0]
    h_scr[...] = _norm_mod(x_ref[0], g_ref[...], m[0:1], m[1:2]).astype(BF16)
    cos, s1, s2 = cos_ref[...], s1_ref[...], s2_ref[...]

    def rope(t):
        return (t * cos + pltpu.roll(t, LANES - 16, axis=1) * s1 + pltpu.roll(t, 16, axis=1) * s2)

    f = jnp.dot(h_scr[...], w_ref[:, 0:D_FOURIER], preferred_element_type=F32)
    a = jnp.dot(f.astype(BF16), cs_ref[...], preferred_element_type=F32)
    n_planes = 2 * D_FOURIER // LANES
    tile_rows = a.shape[0] // GRID_W
    for p in range(n_planes):
        for r in range(tile_rows):
            a_scr[p, r * ROW_PITCH:r * ROW_PITCH + GRID_W, :] = a[r * GRID_W:(r + 1) * GRID_W,
                                                                LANES * p:LANES * (p + 1)]
    for w in range(GRID_W):
        for p in range(n_planes):
            slab = a_scr[p, pl.ds(w, tile_rows, stride=ROW_PITCH), :]
            lo = w * D_FOURIER + (p % 2) * LANES
            a_ref[0, p // 2, :, lo:lo + LANES] = slab.astype(BF16)

    scale = HEAD_DIM ** -0.5 * LOG2E
    for c in range(D_NA // 256):
        lo = D_FOURIER + 256 * c
        q = jnp.dot(h_scr[...], w_ref[:, lo:lo + 256], preferred_element_type=F32)
        k = jnp.dot(h_scr[...], w_ref[:, lo + D_NA:lo + D_NA + 256], preferred_element_type=F32)
        v = jnp.dot(h_scr[...], w_ref[:, lo + 2 * D_NA:lo + 2 * D_NA + 256], preferred_element_type=F32)
        v_ref[0, :, 256 * c:256 * c + 256] = v.astype(BF16)
        for s in range(2):
            sl = slice(LANES * s, LANES * (s + 1))
            ol = slice(256 * c + LANES * s, 256 * c + LANES * (s + 1))
            qs, ks = q[:, sl], k[:, sl]
            qp_ref[0, :, ol] = (qs * scale).astype(BF16)
            qr_ref[0, :, ol] = (rope(qs) * scale).astype(BF16)
            kr_ref[0, :, ol] = rope(ks).astype(BF16)


def in_proj(x, mods, g1, w_in_bf16):
    b, n, d = x.shape
    tm = TM_IN
    cos, s1, s2 = _rope_tables()
    cs = _mxu_const(_chan_dft())
    tok = lambda bi, i: (bi, i, 0)
    const2 = lambda bi, i: (0, 0)
    tab = pl.BlockSpec((tm, LANES), lambda bi, i: (i, 0))
    qkv_shape = jax.ShapeDtypeStruct((b, n, D_NA), BF16)
    qkv_spec = pl.BlockSpec((1, tm, D_NA), tok)
    return pl.pallas_call(
        _in_proj_kernel,
        grid=(b, n // tm),
        in_specs=[pl.BlockSpec((1, tm, d), tok),
                  pl.BlockSpec((1, 8, d), lambda bi, i: (bi, 0, 0)),
                  pl.BlockSpec((1, d), const2),
                  pl.BlockSpec((d, D_IN_PROJ), const2),
                  pl.BlockSpec((D_FOURIER, 2 * D_FOURIER), const2),
                  tab, tab, tab],
        out_specs=[qkv_spec, qkv_spec, qkv_spec, qkv_spec,
                   pl.BlockSpec((1, 2, tm // GRID_W, GRID_W * D_FOURIER), lambda bi, i: (bi, 0, i, 0))],
        out_shape=[qkv_shape, qkv_shape, qkv_shape, qkv_shape,
                   jax.ShapeDtypeStruct((b, 2, n // GRID_W, GRID_W * D_FOURIER), BF16)],
        scratch_shapes=[pltpu.VMEM((tm, d), BF16),
                        pltpu.VMEM((2 * D_FOURIER // LANES, (tm // GRID_W) * ROW_PITCH, LANES), F32)],
        compiler_params=_cparams(("arbitrary", "arbitrary")),
        name="in_proj",
    )(x, mods, g1.reshape(1, d), w_in_bf16, cs, jnp.asarray(cos), jnp.asarray(s1), jnp.asarray(s2))


def _ctx_proj_kernel(x_ref, mod_ref, g_ref, w_ref, k_ref, v_ref):
    m = mod_ref[...]
    h = _norm_mod(x_ref[0], g_ref[...], m[0:1], m[1:2]).astype(BF16)
    k_ref[0] = jnp.dot(h, w_ref[:, 0:D_NA], preferred_element_type=F32).astype(BF16)
    v_ref[0] = jnp.dot(h, w_ref[:, D_NA:2 * D_NA], preferred_element_type=F32).astype(BF16)


def ctx_proj(ctx, mod_ctx, g1, w_kv_bf16):
    b, l, d = ctx.shape
    shape = jax.ShapeDtypeStruct((b, l, D_NA), BF16)
    spec = pl.BlockSpec((1, l, D_NA), lambda bi: (bi, 0, 0))
    return pl.pallas_call(
        _ctx_proj_kernel,
        grid=(b,),
        in_specs=[pl.BlockSpec((1, l, d), lambda bi: (bi, 0, 0)),
                  pl.BlockSpec((8, d), lambda bi: (0, 0)),
                  pl.BlockSpec((1, d), lambda bi: (0, 0)),
                  pl.BlockSpec((d, 2 * D_NA), lambda bi: (0, 0))],
        out_specs=[spec, spec],
        out_shape=[shape, shape],
        compiler_params=_cparams(("arbitrary",)),
        name="ctx_proj",
    )(ctx, mod_ctx, g1.reshape(1, d), w_kv_bf16)


N_PLANES = D_FOURIER // LANES
K1_PITCH = GRID_H + 8
K2_PITCH = GRID_W + 8


def _dft_rows_kernel(a_ref, g_ref, z_ref, z_scr):
    for j in range(DFT_TW):
        sl = slice(D_FOURIER * j, D_FOURIER * (j + 1))
        rhs = jnp.concatenate([a_ref[0, 0, :, sl], a_ref[0, 1, :, sl]], axis=0)
        z = jnp.dot(g_ref[j], rhs, preferred_element_type=F32)
        for c in range(2):
            for p in range(N_PLANES):
                z_scr[c * N_PLANES + p, j * K1_PITCH:j * K1_PITCH + GRID_H, :] = (
                    z[c * GRID_H:(c + 1) * GRID_H, LANES * p:LANES * (p + 1)])
    for k1 in range(GRID_H):
        for c in range(2):
            for p in range(N_PLANES):
                slab = z_scr[c * N_PLANES + p, pl.ds(k1, DFT_TW, stride=K1_PITCH), :]
                lo = k1 * D_FOURIER + p * LANES
                z_ref[0, c, :, lo:lo + LANES] = slab.astype(BF16)


def dft_rows(a):
    b = a.shape[0]
    g = _mxu_const(_row_dft())
    return pl.pallas_call(
        _dft_rows_kernel,
        grid=(GRID_W // DFT_TW, b),
        in_specs=[pl.BlockSpec((1, 2, GRID_H, DFT_TW * D_FOURIER), lambda j, bi: (bi, 0, 0, j)),
                  pl.BlockSpec((DFT_TW, 2 * GRID_H, 2 * GRID_H), lambda j, bi: (j, 0, 0))],
        out_specs=pl.BlockSpec((1, 2, DFT_TW, GRID_H * D_FOURIER), lambda j, bi: (bi, 0, j, 0)),
        out_shape=jax.ShapeDtypeStruct((b, 2, GRID_W, GRID_H * D_FOURIER), BF16),
        scratch_shapes=[pltpu.VMEM((2 * N_PLANES, DFT_TW * K1_PITCH, LANES), F32)],
        compiler_params=_cparams(("arbitrary", "arbitrary")),
        name="dft_rows",
    )(a, g)


def _dft_cols_kernel(z_ref, cs_ref, o_ref, o_scr):
    for j in range(DFT_TK1):
        sl = slice(D_FOURIER * j, D_FOURIER * (j + 1))
        rhs = jnp.concatenate([z_ref[0, 0, :, sl], z_ref[0, 1, :, sl]], axis=0)
        out = jnp.dot(cs_ref[...], rhs, preferred_element_type=F32)
        for p in range(N_PLANES):
            o_scr[p, j * K2_PITCH:j * K2_PITCH + GRID_W, :] = out[:, LANES * p:LANES * (p + 1)]
    for k2 in range(GRID_W):
        for p in range(N_PLANES):
            slab = o_scr[p, pl.ds(k2, DFT_TK1, stride=K2_PITCH), :]
            o_ref[0, k2, :, LANES * p:LANES * (p + 1)] = slab.astype(BF16)


def dft_cols(z):
    b = z.shape[0]
    cs = _mxu_const(_col_dft())
    out = pl.pallas_call(
        _dft_cols_kernel,
        grid=(b, GRID_H // DFT_TK1),
        in_specs=[pl.BlockSpec((1, 2, GRID_W, DFT_TK1 * D_FOURIER), lambda bi, j: (bi, 0, 0, j)),
                  pl.BlockSpec((GRID_W, 2 * GRID_W), lambda bi, j: (0, 0))],
        out_specs=pl.BlockSpec((1, GRID_W, DFT_TK1, D_FOURIER), lambda bi, j: (bi, 0, j, 0)),
        out_shape=jax.ShapeDtypeStruct((b, GRID_W, GRID_H, D_FOURIER), BF16),
        scratch_shapes=[pltpu.VMEM((N_PLANES, DFT_TK1 * K2_PITCH, LANES), F32)],
        compiler_params=_cparams(("arbitrary", "arbitrary")),
        name="dft_cols",
    )(z, cs)
    return out.reshape(b, SEQ, D_FOURIER)


ROW_KEYS = NA_ROWS * GRID_W
ROW_TILES = ROW_KEYS // LANES
N_DR = 2 * NA_ROWS - 2
SOFT_ROWS = 32
ATT_PAIRS = ATT_ROWS // 2


def _attention_kernel(qr_ref, qp_ref, k_ref, v_ref, kc_ref, vc_ref, tb_ref, o_ref,
                      sc_scr, pc_scr, s_scr, p_scr):
    rb = pl.program_id(2)
    first1 = lax.broadcasted_iota(jnp.int32, (GRID_W, LANES), 1) < HEAD_DIM
    nt = (((1,), (1,)), ((), ()))

    def window(jp, i):
        r = rb * ATT_ROWS + 2 * jp + i
        rs = jnp.clip(r - NA_ROWS // 2, 0, GRID_H - NA_ROWS)
        return r, rs, pl.multiple_of(rs * GRID_W, GRID_W)

    def split_heads(q1):
        z1 = jnp.zeros_like(q1)
        return jnp.concatenate([jnp.where(first1, q1, z1), jnp.where(first1, z1, q1)], axis=0)

    def scores(jp):
        qp_rows = []
        for i in range(2):
            _, _, koff = window(jp, i)
            qs = slice((2 * jp + i) * GRID_W, (2 * jp + i + 1) * GRID_W)
            kw = k_ref[0, pl.ds(koff, ROW_KEYS), :]
            s_scr[jp, i * 2 * GRID_W:(i + 1) * 2 * GRID_W, :] = lax.dot_general(
                split_heads(qr_ref[0, qs, :]), kw, nt, preferred_element_type=F32)
            qp_rows.append(split_heads(qp_ref[0, qs, :]))
        sc_scr[jp] = lax.dot_general(jnp.concatenate(qp_rows, axis=0), kc_ref[0], nt,
                                     preferred_element_type=F32)

    def softmax(jp):
        for c in range(4):
            i, hh = c // 2, c % 2
            r, rs, _ = window(jp, i)
            d0 = rs - r + NA_ROWS - 1
            for h in range(GRID_W // SOFT_ROWS):
                lo = h * SOFT_ROWS
                rows = slice(c * GRID_W + lo, c * GRID_W + lo + SOFT_ROWS)
                tiles = [s_scr[jp, rows, t * LANES:(t + 1) * LANES]
                         + tb_ref[0, hh, d0 + 2 * t, lo:lo + SOFT_ROWS, :] for t in range(ROW_TILES)]
                sc = sc_scr[jp, rows, :]
                mt = jnp.maximum(sc[:, :LANES], sc[:, LANES:])
                for tl in tiles:
                    mt = jnp.maximum(mt, tl)
                m = jnp.max(mt, axis=1, keepdims=True)
                pc_scr[jp, rows, :] = jnp.exp2(sc - m).astype(BF16)
                for t, tl in enumerate(tiles):
                    p_scr[jp, rows, t * LANES:(t + 1) * LANES] = jnp.exp2(tl - m).astype(BF16)

    def weighted_values(jp):
        vc = jnp.concatenate([vc_ref[0], jnp.ones((CTX_LEN, LANES), BF16)], axis=1)
        oc = jnp.dot(pc_scr[jp], vc, preferred_element_type=F32)
        for i in range(2):
            _, _, koff = window(jp, i)
            rows = slice(i * 2 * GRID_W, (i + 1) * 2 * GRID_W)
            vw = jnp.concatenate([v_ref[0, pl.ds(koff, ROW_KEYS), :], jnp.ones((ROW_KEYS, LANES), BF16)],
                                 axis=1)
            o = jnp.dot(p_scr[jp, rows, :], vw, preferred_element_type=F32) + oc[rows]
            oa, ob = o[:GRID_W], o[GRID_W:]
            out = jnp.where(first1, oa[:, :LANES] / oa[:, LANES:], ob[:, :LANES] / ob[:, LANES:])
            o_ref[0, (2 * jp + i) * GRID_W:(2 * jp + i + 1) * GRID_W, :] = out.astype(BF16)

    scores(0)
    scores(1)
    softmax(0)
    for jp in range(ATT_PAIRS):
        if jp + 2 < ATT_PAIRS:
            scores(jp + 2)
        if jp + 1 < ATT_PAIRS:
            softmax(jp + 1)
        weighted_values(jp)


def bias_tables(rpb):
    dc, valid = _bias_index()
    n_dc = 2 * NA_COLS - 1
    onehot = (dc.reshape(1, -1) == np.arange(n_dc).reshape(-1, 1)).astype(np.float32)
    t = jnp.dot(rpb.reshape(-1, n_dc), jnp.asarray(onehot), precision=lax.Precision.HIGHEST)
    t = jnp.where(valid[None, None], LOG2E * t.reshape(N_HEADS, 2 * NA_ROWS - 1, GRID_W, GRID_W), NEG)
    tb = pl.pallas_call(
        _bias_pairs_kernel,
        grid=(N_HEADS,),
        in_specs=[pl.BlockSpec((1, N_DR + 1, GRID_W, GRID_W), lambda h: (h, 0, 0, 0))],
        out_specs=pl.BlockSpec((1, N_DR, GRID_W, LANES), lambda h: (h, 0, 0, 0)),
        out_shape=jax.ShapeDtypeStruct((N_HEADS, N_DR, GRID_W, LANES), F32),
        compiler_params=_cparams(("arbitrary",)),
        name="bias_pairs",
    )(t)
    return tb.reshape(N_PAIRS, 2, N_DR, GRID_W, LANES)


def _bias_pairs_kernel(t_ref, o_ref):
    for d in range(N_DR):
        o_ref[0, d] = jnp.concatenate([t_ref[0, d], t_ref[0, d + 1]], axis=1)


def attention(qr, qp, kr, v, kc, vc, tb):
    b, n, _ = qr.shape
    tq = ATT_ROWS * GRID_W
    qspec = pl.BlockSpec((1, tq, LANES), lambda bi, hp, i: (bi, i, hp))
    kspec = pl.BlockSpec((1, n, LANES), lambda bi, hp, i: (bi, 0, hp))
    cspec = pl.BlockSpec((1, CTX_LEN, LANES), lambda bi, hp, i: (bi, 0, hp))
    return pl.pallas_call(
        _attention_kernel,
        grid=(b, N_PAIRS, GRID_H // ATT_ROWS),
        in_specs=[qspec, qspec, kspec, kspec, cspec, cspec,
                  pl.BlockSpec((1, 2, N_DR, GRID_W, LANES), lambda bi, hp, i: (hp, 0, 0, 0, 0))],
        out_specs=qspec,
        out_shape=jax.ShapeDtypeStruct((b, n, D_NA), BF16),
        scratch_shapes=[pltpu.VMEM((ATT_PAIRS, 4 * GRID_W, CTX_LEN), F32),
                        pltpu.VMEM((ATT_PAIRS, 4 * GRID_W, CTX_LEN), BF16),
                        pltpu.VMEM((ATT_PAIRS, 4 * GRID_W, ROW_KEYS), F32),
                        pltpu.VMEM((ATT_PAIRS, 4 * GRID_W, ROW_KEYS), BF16)],
        compiler_params=_cparams(("arbitrary", "arbitrary", "arbitrary")),
        name="attention",
    )(qr, qp, kr, v, kc, vc, tb)


def _rms(x, g):
    ms = jnp.mean(x * x, axis=-1, keepdims=True)
    return x * lax.rsqrt(ms + EPS) * g


TILE_ROWS = D_MODEL // LANES


def _store_token_tiles(ref, val, start=0):
    rows = val.shape[0]
    for j in range(TILE_ROWS):
        ref[pl.ds(start + j, rows, stride=TILE_ROWS), :] = val[:, LANES * j:LANES * (j + 1)]


def _load_token_tiles(ref, start, rows):
    return jnp.concatenate(
        [ref[pl.ds(start + j, rows, stride=TILE_ROWS), :] for j in range(TILE_ROWS)], axis=1)


def _out_proj_kernel(fr_ref, na_ref, x_ref, mod_ref, wf_ref, wo_ref, go_ref, g2_ref, wr_ref, tri_ref,
                     x1_ref, h2_ref, meta_ref, metat_ref, cnt_ref, run_scr):
    @pl.when((pl.program_id(0) == 0) & (pl.program_id(1) == 0))
    def _():
        run_scr[...] = jnp.zeros_like(run_scr)

    m = mod_ref[0]
    go = go_ref[...]
    fo = jnp.dot(fr_ref[0], wf_ref[...], preferred_element_type=F32)
    fn = _rms(fo, go[:, :D_FOURIER]).astype(BF16)
    nn = _rms(na_ref[0].astype(F32), go[:, D_FOURIER:]).astype(BF16)
    y = (jnp.dot(fn, wo_ref[0:D_FOURIER, :], preferred_element_type=F32)
         + jnp.dot(nn, wo_ref[D_FOURIER:, :], preferred_element_type=F32))
    x1 = x_ref[0] + m[2:3] * y
    x1_ref[0] = x1
    h2 = _norm_mod(x1, g2_ref[...], m[3:4], m[4:5])
    _store_token_tiles(h2_ref, h2)
    logits = jnp.dot(h2.astype(BF16), wr_ref[...], preferred_element_type=F32)

    tm = logits.shape[0]
    lane = lax.broadcasted_iota(jnp.int32, (tm, LANES), 1).astype(F32)
    ninf = jnp.float32(-jnp.inf)

    def argmax_first(vals):
        mx = jnp.max(vals, axis=1, keepdims=True)
        idx = jnp.min(jnp.where(vals == mx, lane, float(LANES)), axis=1, keepdims=True)
        return mx, idx

    lg = jnp.where(lane < N_GROUPS, logits, ninf)
    gmax, gidx = argmax_first(lg)
    pg = 1.0 / jnp.sum(jnp.exp(lg - gmax), axis=1, keepdims=True)
    lo = N_GROUPS + EXPERTS_PER_GROUP * gidx
    le = jnp.where((lane >= lo) & (lane < lo + EXPERTS_PER_GROUP), logits, ninf)
    e1, i1 = argmax_first(le)
    e2, i2 = argmax_first(jnp.where(lane == i1, ninf, le))
    dd = jnp.exp(e2 - e1)
    gate1 = pg / (1.0 + dd)
    gate2 = pg * dd / (1.0 + dd)

    hot1 = lane == i1
    hot2 = lane == i2
    onehot = jnp.where(hot1 | hot2, 1.0, 0.0)
    cnt = jnp.dot(tri_ref[...], onehot.astype(BF16), preferred_element_type=F32) + run_scr[...]
    rank1 = jnp.sum(jnp.where(hot1, cnt, 0.0), axis=1, keepdims=True)
    rank2 = jnp.sum(jnp.where(hot2, cnt, 0.0), axis=1, keepdims=True)
    run_scr[...] = run_scr[...] + jnp.sum(onehot, axis=0, keepdims=True)

    meta = jnp.where(lane == 0, i1 - N_GROUPS,
           jnp.where(lane == 1, i2 - N_GROUPS,
           jnp.where(lane == 2, rank1,
           jnp.where(lane == 3, rank2,
           jnp.where(lane == 4, gate1,
           jnp.where(lane == 5, gate2, 0.0))))))
    meta_ref[...] = meta
    metat_ref[...] = jnp.transpose(meta)[0:8, :]
    cnt_ref[...] = jnp.broadcast_to(run_scr[...], cnt_ref.shape)


def out_proj(fr, na, x, mods, w_fmix_bf16, w_out_bf16, g_out, g2, w_router_bf16):
    b, n, d = x.shape
    tm = TM_PROJ
    steps = n // tm
    tok = lambda bi, i: (bi, i, 0)
    const2 = lambda bi, i: (0, 0)
    flat = lambda bi, i: (bi * steps + i, 0)
    tri = _mxu_const(_strict_lower(tm))
    return pl.pallas_call(
        _out_proj_kernel,
        grid=(b, steps),
        in_specs=[pl.BlockSpec((1, tm, D_FOURIER), tok),
                  pl.BlockSpec((1, tm, D_NA), tok),
                  pl.BlockSpec((1, tm, d), tok),
                  pl.BlockSpec((1, 8, d), lambda bi, i: (bi, 0, 0)),
                  pl.BlockSpec((D_FOURIER, D_FOURIER), const2),
                  pl.BlockSpec((d, d), const2),
                  pl.BlockSpec((1, d), const2),
                  pl.BlockSpec((1, d), const2),
                  pl.BlockSpec((d, LANES), const2),
                  pl.BlockSpec((tm, tm), const2)],
        out_specs=[pl.BlockSpec((1, tm, d), tok),
                   pl.BlockSpec((tm * TILE_ROWS, LANES), flat),
                   pl.BlockSpec((tm, LANES), flat),
                   pl.BlockSpec((8, tm), lambda bi, i: (0, bi * steps + i)),
                   pl.BlockSpec((8, LANES), const2)],
        out_shape=[jax.ShapeDtypeStruct((b, n, d), F32),
                   jax.ShapeDtypeStruct((b * n * TILE_ROWS, LANES), F32),
                   jax.ShapeDtypeStruct((b * n, LANES), F32),
                   jax.ShapeDtypeStruct((8, b * n), F32),
                   jax.ShapeDtypeStruct((8, LANES), F32)],
        scratch_shapes=[pltpu.VMEM((1, LANES), F32)],
        compiler_params=_cparams(("arbitrary", "arbitrary")),
        name="out_proj",
    )(fr, na, x, mods, w_fmix_bf16, w_out_bf16, g_out.reshape(1, d), g2.reshape(1, d), w_router_bf16, tri)


DMA_CHUNK = 8


TD_DISP = 1024


def _dispatch_kernel(dest_ref, pstart_ref, count_ref, h2_ref, xs_hbm, zero_scr, sem, pad_sem):
    i = pl.program_id(0)
    nt = dest_ref.shape[0] // 2

    @pl.when(i == 0)
    def _():
        zero_scr[...] = jnp.zeros_like(zero_scr)

        def per_expert(e, npad):
            lo = pstart_ref[e] + count_ref[e]
            mid = pstart_ref[e] + ((count_ref[e] + MOE_CHUNK - 1) // MOE_CHUNK) * MOE_CHUNK
            hi = pstart_ref[e] + ((count_ref[e] + MOE_BLK - 1) // MOE_BLK) * MOE_BLK

            def fill(s, carry):
                pltpu.make_async_copy(zero_scr.at[pl.ds(0, TILE_ROWS)],
                                      xs_hbm.at[pl.ds(s * TILE_ROWS, TILE_ROWS)], pad_sem).start()
                return carry

            def fill_chunk(s, carry):
                pltpu.make_async_copy(zero_scr.at[pl.ds(0, MOE_CHUNK * TILE_ROWS)],
                                      xs_hbm.at[pl.ds(mid * TILE_ROWS + s * MOE_CHUNK * TILE_ROWS,
                                                      MOE_CHUNK * TILE_ROWS)], pad_sem).start()
                return carry

            lax.fori_loop(lo, mid, fill, 0)
            lax.fori_loop(0, (hi - mid) // MOE_CHUNK, fill_chunk, 0)
            return npad + (hi - lo)

        npad = lax.fori_loop(0, N_EXPERTS, per_expert, 0)

        blk_rows = MOE_BLK * TILE_ROWS
        first_free = (pstart_ref[N_EXPERTS - 1] + count_ref[N_EXPERTS - 1] + MOE_BLK - 1) // MOE_BLK
        n_blocks = xs_hbm.shape[0] // blk_rows

        def fill_block(bk, carry):
            pltpu.make_async_copy(zero_scr, xs_hbm.at[pl.ds(bk * blk_rows, blk_rows)], pad_sem).start()
            return carry

        lax.fori_loop(first_free, n_blocks, fill_block, 0)
        rows = npad * TILE_ROWS + (n_blocks - first_free) * blk_rows

        @pl.when(rows > 0)
        def _():
            pltpu.make_async_copy(xs_hbm.at[pl.ds(0, rows)], xs_hbm.at[pl.ds(0, rows)], pad_sem).wait()

    def issue(c, carry):
        t0 = c * DMA_CHUNK
        for u in range(DMA_CHUNK):
            for k in range(2):
                d = dest_ref[k * nt + i * TD_DISP + t0 + u]
                pltpu.make_async_copy(h2_ref.at[pl.ds((t0 + u) * TILE_ROWS, TILE_ROWS)],
                                      xs_hbm.at[pl.ds(d * TILE_ROWS, TILE_ROWS)], sem).start(priority=k)
        return carry

    lax.fori_loop(0, TD_DISP // DMA_CHUNK, issue, 0)
    rows = 2 * TD_DISP * TILE_ROWS
    pltpu.make_async_copy(xs_hbm.at[pl.ds(0, rows)], xs_hbm.at[pl.ds(0, rows)], sem).wait()


def dispatch(h2_tiles, dest, pstarts, counts, n_slots):
    nt = dest.shape[0] // 2
    grid_spec = pltpu.PrefetchScalarGridSpec(
        num_scalar_prefetch=3,
        grid=(nt // TD_DISP,),
        in_specs=[pl.BlockSpec((TD_DISP * TILE_ROWS, LANES), lambda i, ds, ps, ct: (i, 0))],
        out_specs=pl.BlockSpec(memory_space=pl.ANY),
        scratch_shapes=[pltpu.VMEM((MOE_BLK * TILE_ROWS, LANES), F32),
                        pltpu.SemaphoreType.DMA(()),
                        pltpu.SemaphoreType.DMA(())],
    )
    return pl.pallas_call(
        _dispatch_kernel,
        grid_spec=grid_spec,
        out_shape=jax.ShapeDtypeStruct((n_slots * TILE_ROWS, LANES), F32),
        compiler_params=_cparams(("arbitrary",)),
        name="dispatch",
    )(dest, pstarts, counts, h2_tiles)


def _experts_kernel(be_ref, nused_ref, valid_ref, xs_ref, wg_ref, wu_ref, wd_ref, ys_ref,
                    wg_scr, wu_scr, wd_scr):
    i = pl.program_id(0)
    valid = valid_ref[i]
    changed = (i == 0) | (be_ref[i] != be_ref[jnp.maximum(i - 1, 0)])
    chunk_rows = MOE_CHUNK * TILE_ROWS

    @pl.when(changed & (valid > 0))
    def _():
        wg_scr[...] = wg_ref[0].astype(BF16)
        wu_scr[...] = wu_ref[0].astype(BF16)
        wd_scr[...] = wd_ref[0].astype(BF16)

    def run(n_chunks):
        hmids = []
        for h in range(n_chunks):
            x = _load_token_tiles(xs_ref, h * chunk_rows, MOE_CHUNK).astype(BF16)
            g = jnp.dot(x, wg_scr[...], preferred_element_type=F32)
            u = jnp.dot(x, wu_scr[...], preferred_element_type=F32)
            hmids.append((g * jax.nn.sigmoid(g) * u).astype(BF16))
        for h, hmid in enumerate(hmids):
            _store_token_tiles(ys_ref, jnp.dot(hmid, wd_scr[...], preferred_element_type=F32), h * chunk_rows)
        if n_chunks * chunk_rows < ys_ref.shape[0]:
            ys_ref[n_chunks * chunk_rows:, :] = jnp.zeros((ys_ref.shape[0] - n_chunks * chunk_rows, LANES), F32)

    for n_chunks in range(MOE_BLK // MOE_CHUNK + 1):
        lo, hi = (n_chunks - 1) * MOE_CHUNK, n_chunks * MOE_CHUNK
        pl.when((valid > lo) & (valid <= hi))(functools.partial(run, n_chunks))


def experts(xs_tiles, block_expert, nused, block_valid, w_gate, w_up, w_down):
    d = D_MODEL
    nblk = block_expert.shape[0]
    blk_rows = MOE_BLK * TILE_ROWS
    wmap = lambda i, be, nu, bv: (be[i], 0, 0)
    grid_spec = pltpu.PrefetchScalarGridSpec(
        num_scalar_prefetch=3,
        grid=(nblk,),
        in_specs=[pl.BlockSpec((blk_rows, LANES), lambda i, be, nu, bv: (jnp.minimum(i, nu[0] - 1), 0)),
                  pl.BlockSpec((1, d, D_EXPERT), wmap),
                  pl.BlockSpec((1, d, D_EXPERT), wmap),
                  pl.BlockSpec((1, D_EXPERT, d), wmap)],
        out_specs=pl.BlockSpec((blk_rows, LANES), lambda i, be, nu, bv: (i, 0)),
        scratch_shapes=[pltpu.VMEM((d, D_EXPERT), BF16),
                        pltpu.VMEM((d, D_EXPERT), BF16),
                        pltpu.VMEM((D_EXPERT, d), BF16)],
    )
    return pl.pallas_call(
        _experts_kernel,
        grid_spec=grid_spec,
        out_shape=jax.ShapeDtypeStruct((nblk * blk_rows, LANES), F32),
        compiler_params=_cparams(("arbitrary",)),
        name="experts",
    )(block_expert, nused, block_valid, xs_tiles, w_gate, w_up, w_down)


def _combine_kernel(dest_ref, ys_hbm, x1_ref, meta_ref, mod_ref, gf_ref, o_ref, ybuf, sem):
    i = pl.program_id(0)
    nstep = pl.num_programs(0)
    tc = TC_COMB
    nt = dest_ref.shape[0] // 2
    half_rows = tc * TILE_ROWS
    buf_rows = 2 * half_rows

    def gather(step, slot):
        def issue(c, carry):
            t0 = c * DMA_CHUNK
            for u in range(DMA_CHUNK):
                for k in range(2):
                    d = dest_ref[k * nt + step * tc + t0 + u]
                    pltpu.make_async_copy(
                        ys_hbm.at[pl.ds(d * TILE_ROWS, TILE_ROWS)],
                        ybuf.at[pl.ds(slot * buf_rows + k * half_rows + (t0 + u) * TILE_ROWS, TILE_ROWS)],
                        sem.at[slot]).start(priority=k)
            return carry

        lax.fori_loop(0, tc // DMA_CHUNK, issue, 0)

    @pl.when(i == 0)
    def _():
        gather(0, 0)

    @pl.when(i + 1 < nstep)
    def _():
        gather(i + 1, (i + 1) % 2)

    slot = i % 2
    start = pl.multiple_of(slot * buf_rows, buf_rows)
    pltpu.make_async_copy(ys_hbm.at[pl.ds(0, buf_rows)], ybuf.at[pl.ds(start, buf_rows)], sem.at[slot]).wait()
    meta = meta_ref[...]
    y0 = _load_token_tiles(ybuf, start, tc)
    y1 = _load_token_tiles(ybuf, start + half_rows, tc)
    moe = y0 * meta[:, 4:5] + y1 * meta[:, 5:6]
    x2 = x1_ref[...] + mod_ref[0][5:6] * moe
    o_ref[...] = _rms(x2, gf_ref[...])


def combine(dest_flat, ys, x1_flat, meta, mods, g_final, n_per_batch):
    nt, d = x1_flat.shape
    tc = TC_COMB
    per_b = n_per_batch // tc
    grid_spec = pltpu.PrefetchScalarGridSpec(
        num_scalar_prefetch=1,
        grid=(nt // tc,),
        in_specs=[pl.BlockSpec(memory_space=pl.ANY),
                  pl.BlockSpec((tc, d), lambda i, ds: (i, 0)),
                  pl.BlockSpec((tc, LANES), lambda i, ds: (i, 0)),
                  pl.BlockSpec((1, 8, d), lambda i, ds: (i // per_b, 0, 0)),
                  pl.BlockSpec((1, d), lambda i, ds: (0, 0))],
        out_specs=pl.BlockSpec((tc, d), lambda i, ds: (i, 0)),
        scratch_shapes=[pltpu.VMEM((2 * 2 * tc * TILE_ROWS, LANES), F32),
                        pltpu.SemaphoreType.DMA((2,))],
    )
    return pl.pallas_call(
        _combine_kernel,
        grid_spec=grid_spec,
        out_shape=jax.ShapeDtypeStruct((nt, d), F32),
        compiler_params=_cparams(("arbitrary",)),
        name="combine",
    )(dest_flat, ys, x1_flat, meta, mods, g_final.reshape(1, d))


def _dispatch_plan(metat, counts_row, nt):
    counts = counts_row[N_GROUPS:N_GROUPS + N_EXPERTS].astype(jnp.int32)
    pcounts = ((counts + MOE_BLK - 1) // MOE_BLK) * MOE_BLK
    pends = jnp.cumsum(pcounts)
    pstarts = pends - pcounts
    eid = metat[0:2].astype(jnp.int32).reshape(-1, LANES)
    dest = metat[2:4].astype(jnp.int32).reshape(-1, LANES)
    for e in range(N_EXPERTS):
        dest = dest + jnp.where(eid == e, pstarts[e], 0)
    dest = dest.reshape(-1)
    nblk = (nt * 2) // MOE_BLK + N_EXPERTS
    first_slot = jnp.arange(nblk, dtype=jnp.int32) * MOE_BLK
    block_expert = jnp.minimum(
        jnp.sum((pends[None, :] <= first_slot[:, None]).astype(jnp.int32), axis=1), N_EXPERTS - 1)
    nused = (pends[-1] // MOE_BLK).astype(jnp.int32).reshape(1)
    seg_end = jnp.sum(jnp.where(block_expert[:, None] == jnp.arange(N_EXPERTS)[None, :],
                                (pstarts + counts)[None, :], 0), axis=1)
    block_valid = jnp.where(first_slot < pends[-1], jnp.clip(seg_end - first_slot, 0, MOE_BLK), 0)
    return (dest, pstarts.astype(jnp.int32), counts, block_expert.astype(jnp.int32), nused,
            block_valid.astype(jnp.int32))


def kernel(x, c, ctx, c_ctx, w_ada, b_ada, g_norm1, w_in, w_fmix, rpb, g_out, w_out, g_norm2,
           w_router_group, w_router_expert, w_gate, w_up, w_down, g_final):
    b, n, d = x.shape
    assert (b, n, d) == (c.shape[0], SEQ, D_MODEL) and w_ada.shape[0] == 1
    nt = b * n

    cond8 = jnp.zeros((8, d), F32).at[0:b].set(c).at[b].set(c_ctx)
    mod = adaln(cond8, w_ada[0], b_ada[0])
    mods = jnp.pad(mod[0:b].reshape(b, N_MOD, d), ((0, 0), (0, 2), (0, 0)))
    mod_ctx = jnp.pad(mod[b].reshape(N_MOD, d), ((0, 2), (0, 0)))

    w_in_b = w_in[0].astype(BF16)
    qr, qp, kr, v, a = in_proj(x, mods, g_norm1[0], w_in_b)
    kc, vc = ctx_proj(ctx, mod_ctx, g_norm1[0], w_in_b[:, D_FOURIER + D_NA:])

    fr = dft_cols(dft_rows(a))
    na = attention(qr, qp, kr, v, kc, vc, bias_tables(rpb[0]))

    w_router = jnp.concatenate(
        [w_router_group[0], w_router_expert[0],
         jnp.zeros((d, LANES - N_GROUPS - N_EXPERTS), F32)], axis=1).astype(BF16)
    x1, h2_tiles, meta, metat, cnt = out_proj(fr, na, x, mods, w_fmix[0].astype(BF16),
                                              w_out[0].astype(BF16), g_out[0], g_norm2[0], w_router)

    dest, pstarts, counts, block_expert, nused, block_valid = _dispatch_plan(metat, cnt[0], nt)
    xs_tiles = dispatch(h2_tiles, dest, pstarts, counts, block_expert.shape[0] * MOE_BLK)
    ys_tiles = experts(xs_tiles, block_expert, nused, block_valid, w_gate[0], w_up[0], w_down[0])
    out = combine(dest, ys_tiles, x1.reshape(nt, d), meta, mods, g_final, n)
    return out.reshape(b, n, d)
```

```python
import functools
import math

import numpy as np
import jax
import jax.numpy as jnp
from jax import lax
from jax.experimental import pallas as pl
from jax.experimental.pallas import tpu as pltpu

F32 = jnp.float32
BF16 = jnp.bfloat16

D_MODEL = 1024
GRID_W = 64
GRID_H = 128
SEQ = GRID_W * GRID_H
CTX_LEN = 256
D_FOURIER = 256
FOURIER_GROUP = 64
HEAD_DIM = 64
N_HEADS = 12
D_NA = N_HEADS * HEAD_DIM
N_PAIRS = N_HEADS // 2
NA_ROWS = 8
NA_COLS = 16
ROPE_THETA = 10000.0
N_GROUPS = 4
EXPERTS_PER_GROUP = 8
N_EXPERTS = N_GROUPS * EXPERTS_PER_GROUP
D_EXPERT = 512
N_MOD = 6
D_IN_PROJ = D_FOURIER + 3 * D_NA
EPS = 1e-6
LANES = 128
NEG = -1e30
LOG2E = math.log2(math.e)

TM_IN = 1024
ROW_PITCH = 72
TM_PROJ = 1024
ATT_ROWS = 32
DFT_TW = 16
DFT_TK1 = 16
MOE_BLK = 512
MOE_CHUNK = 256
TC_COMB = 256
VMEM_LIMIT = 56 * 1024 * 1024


def _cparams(sem):
    return pltpu.CompilerParams(dimension_semantics=sem, vmem_limit_bytes=VMEM_LIMIT)


def _mxu_const(table):
    return jnp.asarray(table, F32).astype(BF16)


@functools.lru_cache(maxsize=None)
def _rope_tables():
    t = np.arange(SEQ)
    row, col = t // GRID_W, t % GRID_W
    lane = np.arange(LANES)
    d = lane % HEAD_DIM
    chunk = d // 32
    e = d % 32
    j = e % 16
    inv = ROPE_THETA ** (-(j.astype(np.float64)) / 16.0)
    pos = np.where(chunk[None, :] == 0, row[:, None], col[:, None]).astype(np.float64)
    ang = pos * inv[None, :]
    cos = np.cos(ang)
    sin = np.sin(ang)
    first = (e < 16)[None, :]
    s_first = np.where(first, -sin, 0.0)
    s_second = np.where(first, 0.0, sin)
    return (cos.astype(np.float32), s_first.astype(np.float32), s_second.astype(np.float32))


@functools.lru_cache(maxsize=None)
def _chan_dft():
    c = np.arange(FOURIER_GROUP)
    ang = 2.0 * np.pi * ((c[:, None] * c[None, :]) % FOURIER_GROUP) / FOURIER_GROUP
    eye = np.eye(D_FOURIER // FOURIER_GROUP)
    re = np.kron(eye, np.cos(ang))
    im = np.kron(eye, -np.sin(ang))
    return np.concatenate([re, im], axis=1).astype(np.float32)


@functools.lru_cache(maxsize=None)
def _row_dft():
    k1 = np.arange(GRID_H)[:, None]
    r = np.arange(GRID_H)[None, :]
    out = np.zeros((GRID_W, 2 * GRID_H, 2 * GRID_H), np.float32)
    for w in range(GRID_W):
        m = (k1 * (GRID_W * r + w)) % SEQ
        ang = 2.0 * np.pi * m / SEQ
        c, s = np.cos(ang), np.sin(ang)
        out[w] = np.block([[c, s], [-s, c]])
    return out


@functools.lru_cache(maxsize=None)
def _col_dft():
    k2 = np.arange(GRID_W)
    ang = 2.0 * np.pi * ((k2[:, None] * k2[None, :]) % GRID_W) / GRID_W
    scale = 1.0 / math.sqrt(SEQ * FOURIER_GROUP)
    return (np.concatenate([np.cos(ang), np.sin(ang)], axis=1) * scale).astype(np.float32)


@functools.lru_cache(maxsize=None)
def _bias_index():
    c = np.arange(GRID_W)
    start = np.clip(c - NA_COLS // 2, 0, GRID_W - NA_COLS)
    valid = (c[None, :] >= start[:, None]) & (c[None, :] < start[:, None] + NA_COLS)
    dc = np.clip(c[None, :] - c[:, None] + (NA_COLS - 1), 0, 2 * NA_COLS - 2)
    return dc.astype(np.int32), valid


@functools.lru_cache(maxsize=None)
def _strict_lower(n):
    return np.tril(np.ones((n, n), np.float32), k=-1)


def _adaln_kernel(c_ref, w_ref, b_ref, o_ref):
    c = c_ref[...]
    s = c * jax.nn.sigmoid(c)
    o_ref[...] = jnp.dot(s.astype(BF16), w_ref[...].astype(BF16), preferred_element_type=F32) + b_ref[...]


def adaln(cond8, w, b):
    n = w.shape[1]
    tn = 1536
    return pl.pallas_call(
        _adaln_kernel,
        grid=(n // tn,),
        in_specs=[pl.BlockSpec((8, D_MODEL), lambda j: (0, 0)),
                  pl.BlockSpec((D_MODEL, tn), lambda j: (0, j)),
                  pl.BlockSpec((1, tn), lambda j: (0, j))],
        out_specs=pl.BlockSpec((8, tn), lambda j: (0, j)),
        out_shape=jax.ShapeDtypeStruct((8, n), F32),
        compiler_params=_cparams(("arbitrary",)),
        name="adaln",
    )(cond8, w, b.reshape(1, n))


def _norm_mod(x, g, shift, scale):
    ms = jnp.mean(x * x, axis=-1, keepdims=True)
    return (x * lax.rsqrt(ms + EPS) * g) * (1.0 + scale) + shift


def _in_proj_kernel(x_ref, mod_ref, g_ref, w_ref, cs_ref, cos_ref, s1_ref, s2_ref,
                    qr_ref, qp_ref, kr_ref, v_ref, a_ref, h_scr, a_scr):
    m = mod_ref[0]
    h_scr[...] = _norm_mod(x_ref[0], g_ref[...], m[0:1], m[1:2]).astype(BF16)
    cos, s1, s2 = cos_ref[...], s1_ref[...], s2_ref[...]

    def rope(t):
        return (t * cos + pltpu.roll(t, LANES - 16, axis=1) * s1 + pltpu.roll(t, 16, axis=1) * s2)

    f = jnp.dot(h_scr[...], w_ref[:, 0:D_FOURIER], preferred_element_type=F32)
    a = jnp.dot(f.astype(BF16), cs_ref[...], preferred_element_type=F32)
    n_planes = 2 * D_FOURIER // LANES
    tile_rows = a.shape[0] // GRID_W
    for p in range(n_planes):
        for r in range(tile_rows):
            a_scr[p, r * ROW_PITCH:r * ROW_PITCH + GRID_W, :] = a[r * GRID_W:(r + 1) * GRID_W,
                                                                LANES * p:LANES * (p + 1)]
    for w in range(GRID_W):
        for p in range(n_planes):
            slab = a_scr[p, pl.ds(w, tile_rows, stride=ROW_PITCH), :]
            lo = w * D_FOURIER + (p % 2) * LANES
            a_ref[0, p // 2, :, lo:lo + LANES] = slab.astype(BF16)

    scale = HEAD_DIM ** -0.5 * LOG2E
    for c in range(D_NA // 256):
        lo = D_FOURIER + 256 * c
        q = jnp.dot(h_scr[...], w_ref[:, lo:lo + 256], preferred_element_type=F32)
        k = jnp.dot(h_scr[...], w_ref[:, lo + D_NA:lo + D_NA + 256], preferred_element_type=F32)
        v = jnp.dot(h_scr[...], w_ref[:, lo + 2 * D_NA:lo + 2 * D_NA + 256], preferred_element_type=F32)
        v_ref[0, :, 256 * c:256 * c + 256] = v.astype(BF16)
        for s in range(2):
            sl = slice(LANES * s, LANES * (s + 1))
            ol = slice(256 * c + LANES * s, 256 * c + LANES * (s + 1))
            qs, ks = q[:, sl], k[:, sl]
            qp_ref[0, :, ol] = (qs * scale).astype(BF16)
            qr_ref[0, :, ol] = (rope(qs) * scale).astype(BF16)
            kr_ref[0, :, ol] = rope(ks).astype(BF16)


def in_proj(x, mods, g1, w_in_bf16):
    b, n, d = x.shape
    tm = TM_IN
    cos, s1, s2 = _rope_tables()
    cs = _mxu_const(_chan_dft())
    tok = lambda bi, i: (bi, i, 0)
    const2 = lambda bi, i: (0, 0)
    tab = pl.BlockSpec((tm, LANES), lambda bi, i: (i, 0))
    qkv_shape = jax.ShapeDtypeStruct((b, n, D_NA), BF16)
    qkv_spec = pl.BlockSpec((1, tm, D_NA), tok)
    return pl.pallas_call(
        _in_proj_kernel,
        grid=(b, n // tm),
        in_specs=[pl.BlockSpec((1, tm, d), tok),
                  pl.BlockSpec((1, 8, d), lambda bi, i: (bi, 0, 0)),
                  pl.BlockSpec((1, d), const2),
                  pl.BlockSpec((d, D_IN_PROJ), const2),
                  pl.BlockSpec((D_FOURIER, 2 * D_FOURIER), const2),
                  tab, tab, tab],
        out_specs=[qkv_spec, qkv_spec, qkv_spec, qkv_spec,
                   pl.BlockSpec((1, 2, tm // GRID_W, GRID_W * D_FOURIER), lambda bi, i: (bi, 0, i, 0))],
        out_shape=[qkv_shape, qkv_shape, qkv_shape, qkv_shape,
                   jax.ShapeDtypeStruct((b, 2, n // GRID_W, GRID_W * D_FOURIER), BF16)],
        scratch_shapes=[pltpu.VMEM((tm, d), BF16),
                        pltpu.VMEM((2 * D_FOURIER // LANES, (tm // GRID_W) * ROW_PITCH, LANES), F32)],
        compiler_params=_cparams(("arbitrary", "arbitrary")),
        name="in_proj",
    )(x, mods, g1.reshape(1, d), w_in_bf16, cs, jnp.asarray(cos), jnp.asarray(s1), jnp.asarray(s2))


def _ctx_proj_kernel(x_ref, mod_ref, g_ref, w_ref, k_ref, v_ref):
    m = mod_ref[...]
    h = _norm_mod(x_ref[0], g_ref[...], m[0:1], m[1:2]).astype(BF16)
    k_ref[0] = jnp.dot(h, w_ref[:, 0:D_NA], preferred_element_type=F32).astype(BF16)
    v_ref[0] = jnp.dot(h, w_ref[:, D_NA:2 * D_NA], preferred_element_type=F32).astype(BF16)


def ctx_proj(ctx, mod_ctx, g1, w_kv_bf16):
    b, l, d = ctx.shape
    shape = jax.ShapeDtypeStruct((b, l, D_NA), BF16)
    spec = pl.BlockSpec((1, l, D_NA), lambda bi: (bi, 0, 0))
    return pl.pallas_call(
        _ctx_proj_kernel,
        grid=(b,),
        in_specs=[pl.BlockSpec((1, l, d), lambda bi: (bi, 0, 0)),
                  pl.BlockSpec((8, d), lambda bi: (0, 0)),
                  pl.BlockSpec((1, d), lambda bi: (0, 0)),
                  pl.BlockSpec((d, 2 * D_NA), lambda bi: (0, 0))],
        out_specs=[spec, spec],
        out_shape=[shape, shape],
        compiler_params=_cparams(("arbitrary",)),
        name="ctx_proj",
    )(ctx, mod_ctx, g1.reshape(1, d), w_kv_bf16)


N_PLANES = D_FOURIER // LANES
K1_PITCH = GRID_H + 8
K2_PITCH = GRID_W + 8


def _dft_rows_kernel(a_ref, g_ref, z_ref, z_scr):
    for j in range(DFT_TW):
        sl = slice(D_FOURIER * j, D_FOURIER * (j + 1))
        rhs = jnp.concatenate([a_ref[0, 0, :, sl], a_ref[0, 1, :, sl]], axis=0)
        z = jnp.dot(g_ref[j], rhs, preferred_element_type=F32)
        for c in range(2):
            for p in range(N_PLANES):
                z_scr[c * N_PLANES + p, j * K1_PITCH:j * K1_PITCH + GRID_H, :] = (
                    z[c * GRID_H:(c + 1) * GRID_H, LANES * p:LANES * (p + 1)])
    for k1 in range(GRID_H):
        for c in range(2):
            for p in range(N_PLANES):
                slab = z_scr[c * N_PLANES + p, pl.ds(k1, DFT_TW, stride=K1_PITCH), :]
                lo = k1 * D_FOURIER + p * LANES
                z_ref[0, c, :, lo:lo + LANES] = slab.astype(BF16)


def dft_rows(a):
    b = a.shape[0]
    g = _mxu_const(_row_dft())
    return pl.pallas_call(
        _dft_rows_kernel,
        grid=(GRID_W // DFT_TW, b),
        in_specs=[pl.BlockSpec((1, 2, GRID_H, DFT_TW * D_FOURIER), lambda j, bi: (bi, 0, 0, j)),
                  pl.BlockSpec((DFT_TW, 2 * GRID_H, 2 * GRID_H), lambda j, bi: (j, 0, 0))],
        out_specs=pl.BlockSpec((1, 2, DFT_TW, GRID_H * D_FOURIER), lambda j, bi: (bi, 0, j, 0)),
        out_shape=jax.ShapeDtypeStruct((b, 2, GRID_W, GRID_H * D_FOURIER), BF16),
        scratch_shapes=[pltpu.VMEM((2 * N_PLANES, DFT_TW * K1_PITCH, LANES), F32)],
        compiler_params=_cparams(("arbitrary", "arbitrary")),
        name="dft_rows",
    )(a, g)


def _dft_cols_kernel(z_ref, cs_ref, o_ref, o_scr):
    for j in range(DFT_TK1):
        sl = slice(D_FOURIER * j, D_FOURIER * (j + 1))
        rhs = jnp.concatenate([z_ref[0, 0, :, sl], z_ref[0, 1, :, sl]], axis=0)
        out = jnp.dot(cs_ref[...], rhs, preferred_element_type=F32)
        for p in range(N_PLANES):
            o_scr[p, j * K2_PITCH:j * K2_PITCH + GRID_W, :] = out[:, LANES * p:LANES * (p + 1)]
    for k2 in range(GRID_W):
        for p in range(N_PLANES):
            slab = o_scr[p, pl.ds(k2, DFT_TK1, stride=K2_PITCH), :]
            o_ref[0, k2, :, LANES * p:LANES * (p + 1)] = slab.astype(BF16)


def dft_cols(z):
    b = z.shape[0]
    cs = _mxu_const(_col_dft())
    out = pl.pallas_call(
        _dft_cols_kernel,
        grid=(b, GRID_H // DFT_TK1),
        in_specs=[pl.BlockSpec((1, 2, GRID_W, DFT_TK1 * D_FOURIER), lambda bi, j: (bi, 0, 0, j)),
                  pl.BlockSpec((GRID_W, 2 * GRID_W), lambda bi, j: (0, 0))],
        out_specs=pl.BlockSpec((1, GRID_W, DFT_TK1, D_FOURIER), lambda bi, j: (bi, 0, j, 0)),
        out_shape=jax.ShapeDtypeStruct((b, GRID_W, GRID_H, D_FOURIER), BF16),
        scratch_shapes=[pltpu.VMEM((N_PLANES, DFT_TK1 * K2_PITCH, LANES), F32)],
        compiler_params=_cparams(("arbitrary", "arbitrary")),
        name="dft_cols",
    )(z, cs)
    return out.reshape(b, SEQ, D_FOURIER)


ROW_KEYS = NA_ROWS * GRID_W
ROW_TILES = ROW_KEYS // LANES
N_DR = 2 * NA_ROWS - 2
SOFT_ROWS = 32
ATT_PAIRS = ATT_ROWS // 2


def _attention_kernel(qr_ref, qp_ref, k_ref, v_ref, kc_ref, vc_ref, tb_ref, o_ref,
                      sc_scr, pc_scr, s_scr, p_scr):
    rb = pl.program_id(2)
    first1 = lax.broadcasted_iota(jnp.int32, (GRID_W, LANES), 1) < HEAD_DIM
    nt = (((1,), (1,)), ((), ()))

    def window(jp, i):
        r = rb * ATT_ROWS + 2 * jp + i
        rs = jnp.clip(r - NA_ROWS // 2, 0, GRID_H - NA_ROWS)
        return r, rs, pl.multiple_of(rs * GRID_W, GRID_W)

    def split_heads(q1):
        z1 = jnp.zeros_like(q1)
        return jnp.concatenate([jnp.where(first1, q1, z1), jnp.where(first1, z1, q1)], axis=0)

    def scores(jp):
        qp_rows = []
        for i in range(2):
            _, _, koff = window(jp, i)
            qs = slice((2 * jp + i) * GRID_W, (2 * jp + i + 1) * GRID_W)
            kw = k_ref[0, pl.ds(koff, ROW_KEYS), :]
            s_scr[jp, i * 2 * GRID_W:(i + 1) * 2 * GRID_W, :] = lax.dot_general(
                split_heads(qr_ref[0, qs, :]), kw, nt, preferred_element_type=F32)
            qp_rows.append(split_heads(qp_ref[0, qs, :]))
        sc_scr[jp] = lax.dot_general(jnp.concatenate(qp_rows, axis=0), kc_ref[0], nt,
                                     preferred_element_type=F32)

    def softmax(jp):
        for c in range(4):
            i, hh = c // 2, c % 2
            r, rs, _ = window(jp, i)
            d0 = rs - r + NA_ROWS - 1
            for h in range(GRID_W // SOFT_ROWS):
                lo = h * SOFT_ROWS
                rows = slice(c * GRID_W + lo, c * GRID_W + lo + SOFT_ROWS)
                tiles = [s_scr[jp, rows, t * LANES:(t + 1) * LANES]
                         + tb_ref[0, hh, d0 + 2 * t, lo:lo + SOFT_ROWS, :] for t in range(ROW_TILES)]
                sc = sc_scr[jp, rows, :]
                mt = jnp.maximum(sc[:, :LANES], sc[:, LANES:])
                for tl in tiles:
                    mt = jnp.maximum(mt, tl)
                m = jnp.max(mt, axis=1, keepdims=True)
                pc_scr[jp, rows, :] = jnp.exp2(sc - m).astype(BF16)
                for t, tl in enumerate(tiles):
                    p_scr[jp, rows, t * LANES:(t + 1) * LANES] = jnp.exp2(tl - m).astype(BF16)

    def weighted_values(jp):
        vc = jnp.concatenate([vc_ref[0], jnp.ones((CTX_LEN, LANES), BF16)], axis=1)
        oc = jnp.dot(pc_scr[jp], vc, preferred_element_type=F32)
        for i in range(2):
            _, _, koff = window(jp, i)
            rows = slice(i * 2 * GRID_W, (i + 1) * 2 * GRID_W)
            vw = jnp.concatenate([v_ref[0, pl.ds(koff, ROW_KEYS), :], jnp.ones((ROW_KEYS, LANES), BF16)],
                                 axis=1)
            o = jnp.dot(p_scr[jp, rows, :], vw, preferred_element_type=F32) + oc[rows]
            oa, ob = o[:GRID_W], o[GRID_W:]
            out = jnp.where(first1, oa[:, :LANES] / oa[:, LANES:], ob[:, :LANES] / ob[:, LANES:])
            o_ref[0, (2 * jp + i) * GRID_W:(2 * jp + i + 1) * GRID_W, :] = out.astype(BF16)

    scores(0)
    scores(1)
    softmax(0)
    for jp in range(ATT_PAIRS):
        if jp + 2 < ATT_PAIRS:
            scores(jp + 2)
        if jp + 1 < ATT_PAIRS:
            softmax(jp + 1)
        weighted_values(jp)


def bias_tables(rpb):
    dc, valid = _bias_index()
    n_dc = 2 * NA_COLS - 1
    onehot = (dc.reshape(1, -1) == np.arange(n_dc).reshape(-1, 1)).astype(np.float32)
    t = jnp.dot(rpb.reshape(-1, n_dc), jnp.asarray(onehot), precision=lax.Precision.HIGHEST)
    t = jnp.where(valid[None, None], LOG2E * t.reshape(N_HEADS, 2 * NA_ROWS - 1, GRID_W, GRID_W), NEG)
    tb = pl.pallas_call(
        _bias_pairs_kernel,
        grid=(N_HEADS,),
        in_specs=[pl.BlockSpec((1, N_DR + 1, GRID_W, GRID_W), lambda h: (h, 0, 0, 0))],
        out_specs=pl.BlockSpec((1, N_DR, GRID_W, LANES), lambda h: (h, 0, 0, 0)),
        out_shape=jax.ShapeDtypeStruct((N_HEADS, N_DR, GRID_W, LANES), F32),
        compiler_params=_cparams(("arbitrary",)),
        name="bias_pairs",
    )(t)
    return tb.reshape(N_PAIRS, 2, N_DR, GRID_W, LANES)


def _bias_pairs_kernel(t_ref, o_ref):
    for d in range(N_DR):
        o_ref[0, d] = jnp.concatenate([t_ref[0, d], t_ref[0, d + 1]], axis=1)


def attention(qr, qp, kr, v, kc, vc, tb):
    b, n, _ = qr.shape
    tq = ATT_ROWS * GRID_W
    qspec = pl.BlockSpec((1, tq, LANES), lambda bi, hp, i: (bi, i, hp))
    kspec = pl.BlockSpec((1, n, LANES), lambda bi, hp, i: (bi, 0, hp))
    cspec = pl.BlockSpec((1, CTX_LEN, LANES), lambda bi, hp, i: (bi, 0, hp))
    return pl.pallas_call(
        _attention_kernel,
        grid=(b, N_PAIRS, GRID_H // ATT_ROWS),
        in_specs=[qspec, qspec, kspec, kspec, cspec, cspec,
                  pl.BlockSpec((1, 2, N_DR, GRID_W, LANES), lambda bi, hp, i: (hp, 0, 0, 0, 0))],
        out_specs=qspec,
        out_shape=jax.ShapeDtypeStruct((b, n, D_NA), BF16),
        scratch_shapes=[pltpu.VMEM((ATT_PAIRS, 4 * GRID_W, CTX_LEN), F32),
                        pltpu.VMEM((ATT_PAIRS, 4 * GRID_W, CTX_LEN), BF16),
                        pltpu.VMEM((ATT_PAIRS, 4 * GRID_W, ROW_KEYS), F32),
                        pltpu.VMEM((ATT_PAIRS, 4 * GRID_W, ROW_KEYS), BF16)],
        compiler_params=_cparams(("arbitrary", "arbitrary", "arbitrary")),
        name="attention",
    )(qr, qp, kr, v, kc, vc, tb)


def _rms(x, g):
    ms = jnp.mean(x * x, axis=-1, keepdims=True)
    return x * lax.rsqrt(ms + EPS) * g


TILE_ROWS = D_MODEL // LANES


def _store_token_tiles(ref, val, start=0):
    rows = val.shape[0]
    for j in range(TILE_ROWS):
        ref[pl.ds(start + j, rows, stride=TILE_ROWS), :] = val[:, LANES * j:LANES * (j + 1)]


def _load_token_tiles(ref, start, rows):
    return jnp.concatenate(
        [ref[pl.ds(start + j, rows, stride=TILE_ROWS), :] for j in range(TILE_ROWS)], axis=1)


def _out_proj_kernel(fr_ref, na_ref, x_ref, mod_ref, wf_ref, wo_ref, go_ref, g2_ref, wr_ref, tri_ref,
                     x1_ref, h2_ref, meta_ref, metat_ref, cnt_ref, run_scr):
    @pl.when((pl.program_id(0) == 0) & (pl.program_id(1) == 0))
    def _():
        run_scr[...] = jnp.zeros_like(run_scr)

    m = mod_ref[0]
    go = go_ref[...]
    fo = jnp.dot(fr_ref[0], wf_ref[...], preferred_element_type=F32)
    fn = _rms(fo, go[:, :D_FOURIER]).astype(BF16)
    nn = _rms(na_ref[0].astype(F32), go[:, D_FOURIER:]).astype(BF16)
    y = (jnp.dot(fn, wo_ref[0:D_FOURIER, :], preferred_element_type=F32)
         + jnp.dot(nn, wo_ref[D_FOURIER:, :], preferred_element_type=F32))
    x1 = x_ref[0] + m[2:3] * y
    x1_ref[0] = x1
    h2 = _norm_mod(x1, g2_ref[...], m[3:4], m[4:5])
    _store_token_tiles(h2_ref, h2)
    logits = jnp.dot(h2.astype(BF16), wr_ref[...], preferred_element_type=F32)

    tm = logits.shape[0]
    lane = lax.broadcasted_iota(jnp.int32, (tm, LANES), 1).astype(F32)
    ninf = jnp.float32(-jnp.inf)

    def argmax_first(vals):
        mx = jnp.max(vals, axis=1, keepdims=True)
        idx = jnp.min(jnp.where(vals == mx, lane, float(LANES)), axis=1, keepdims=True)
        return mx, idx

    lg = jnp.where(lane < N_GROUPS, logits, ninf)
    gmax, gidx = argmax_first(lg)
    pg = 1.0 / jnp.sum(jnp.exp(lg - gmax), axis=1, keepdims=True)
    lo = N_GROUPS + EXPERTS_PER_GROUP * gidx
    le = jnp.where((lane >= lo) & (lane < lo + EXPERTS_PER_GROUP), logits, ninf)
    e1, i1 = argmax_first(le)
    e2, i2 = argmax_first(jnp.where(lane == i1, ninf, le))
    dd = jnp.exp(e2 - e1)
    gate1 = pg / (1.0 + dd)
    gate2 = pg * dd / (1.0 + dd)

    hot1 = lane == i1
    hot2 = lane == i2
    onehot = jnp.where(hot1 | hot2, 1.0, 0.0)
    cnt = jnp.dot(tri_ref[...], onehot.astype(BF16), preferred_element_type=F32) + run_scr[...]
    rank1 = jnp.sum(jnp.where(hot1, cnt, 0.0), axis=1, keepdims=True)
    rank2 = jnp.sum(jnp.where(hot2, cnt, 0.0), axis=1, keepdims=True)
    run_scr[...] = run_scr[...] + jnp.sum(onehot, axis=0, keepdims=True)

    meta = jnp.where(lane == 0, i1 - N_GROUPS,
           jnp.where(lane == 1, i2 - N_GROUPS,
           jnp.where(lane == 2, rank1,
           jnp.where(lane == 3, rank2,
           jnp.where(lane == 4, gate1,
           jnp.where(lane == 5, gate2, 0.0))))))
    meta_ref[...] = meta
    metat_ref[...] = jnp.transpose(meta)[0:8, :]
    cnt_ref[...] = jnp.broadcast_to(run_scr[...], cnt_ref.shape)


def out_proj(fr, na, x, mods, w_fmix_bf16, w_out_bf16, g_out, g2, w_router_bf16):
    b, n, d = x.shape
    tm = TM_PROJ
    steps = n // tm
    tok = lambda bi, i: (bi, i, 0)
    const2 = lambda bi, i: (0, 0)
    flat = lambda bi, i: (bi * steps + i, 0)
    tri = _mxu_const(_strict_lower(tm))
    return pl.pallas_call(
        _out_proj_kernel,
        grid=(b, steps),
        in_specs=[pl.BlockSpec((1, tm, D_FOURIER), tok),
                  pl.BlockSpec((1, tm, D_NA), tok),
                  pl.BlockSpec((1, tm, d), tok),
                  pl.BlockSpec((1, 8, d), lambda bi, i: (bi, 0, 0)),
                  pl.BlockSpec((D_FOURIER, D_FOURIER), const2),
                  pl.BlockSpec((d, d), const2),
                  pl.BlockSpec((1, d), const2),
                  pl.BlockSpec((1, d), const2),
                  pl.BlockSpec((d, LANES), const2),
                  pl.BlockSpec((tm, tm), const2)],
        out_specs=[pl.BlockSpec((1, tm, d), tok),
                   pl.BlockSpec((tm * TILE_ROWS, LANES), flat),
                   pl.BlockSpec((tm, LANES), flat),
                   pl.BlockSpec((8, tm), lambda bi, i: (0, bi * steps + i)),
                   pl.BlockSpec((8, LANES), const2)],
        out_shape=[jax.ShapeDtypeStruct((b, n, d), F32),
                   jax.ShapeDtypeStruct((b * n * TILE_ROWS, LANES), F32),
                   jax.ShapeDtypeStruct((b * n, LANES), F32),
                   jax.ShapeDtypeStruct((8, b * n), F32),
                   jax.ShapeDtypeStruct((8, LANES), F32)],
        scratch_shapes=[pltpu.VMEM((1, LANES), F32)],
        compiler_params=_cparams(("arbitrary", "arbitrary")),
        name="out_proj",
    )(fr, na, x, mods, w_fmix_bf16, w_out_bf16, g_out.reshape(1, d), g2.reshape(1, d), w_router_bf16, tri)


DMA_CHUNK = 8


TD_DISP = 1024


def _dispatch_kernel(packed_ref, pstart_ref, count_ref, h2_ref, xs_hbm, dest_ref, zero_scr, sem, pad_sem):
    i = pl.program_id(0)
    nt = packed_ref.shape[0] // 2

    @pl.when(i == 0)
    def _():
        zero_scr[...] = jnp.zeros_like(zero_scr)

        def per_expert(e, npad):
            lo = pstart_ref[e] + count_ref[e]
            mid = pstart_ref[e] + ((count_ref[e] + MOE_CHUNK - 1) // MOE_CHUNK) * MOE_CHUNK
            hi = pstart_ref[e] + ((count_ref[e] + MOE_BLK - 1) // MOE_BLK) * MOE_BLK

            def fill(s, carry):
                pltpu.make_async_copy(zero_scr.at[pl.ds(0, TILE_ROWS)],
                                      xs_hbm.at[pl.ds(s * TILE_ROWS, TILE_ROWS)], pad_sem).start()
                return carry

            def fill_chunk(s, carry):
                pltpu.make_async_copy(zero_scr.at[pl.ds(0, MOE_CHUNK * TILE_ROWS)],
                                      xs_hbm.at[pl.ds(mid * TILE_ROWS + s * MOE_CHUNK * TILE_ROWS,
                                                      MOE_CHUNK * TILE_ROWS)], pad_sem).start()
                return carry

            lax.fori_loop(lo, mid, fill, 0)
            lax.fori_loop(0, (hi - mid) // MOE_CHUNK, fill_chunk, 0)
            return npad + (hi - lo)

        npad = lax.fori_loop(0, N_EXPERTS, per_expert, 0)

        blk_rows = MOE_BLK * TILE_ROWS
        first_free = (pstart_ref[N_EXPERTS - 1] + count_ref[N_EXPERTS - 1] + MOE_BLK - 1) // MOE_BLK
        n_blocks = xs_hbm.shape[0] // blk_rows

        def fill_block(bk, carry):
            pltpu.make_async_copy(zero_scr, xs_hbm.at[pl.ds(bk * blk_rows, blk_rows)], pad_sem).start()
            return carry

        lax.fori_loop(first_free, n_blocks, fill_block, 0)
        rows = npad * TILE_ROWS + (n_blocks - first_free) * blk_rows

        @pl.when(rows > 0)
        def _():
            pltpu.make_async_copy(xs_hbm.at[pl.ds(0, rows)], xs_hbm.at[pl.ds(0, rows)], pad_sem).wait()

    def issue(c, carry):
        t0 = c * DMA_CHUNK
        for u in range(DMA_CHUNK):
            for k in range(2):
                a = k * nt + i * TD_DISP + t0 + u
                v = packed_ref[a]
                d = pstart_ref[v & (N_EXPERTS - 1)] + (v >> 5)
                dest_ref[a] = d
                pltpu.make_async_copy(h2_ref.at[pl.ds((t0 + u) * TILE_ROWS, TILE_ROWS)],
                                      xs_hbm.at[pl.ds(d * TILE_ROWS, TILE_ROWS)], sem).start(priority=k)
        return carry

    lax.fori_loop(0, TD_DISP // DMA_CHUNK, issue, 0)
    rows = 2 * TD_DISP * TILE_ROWS
    pltpu.make_async_copy(xs_hbm.at[pl.ds(0, rows)], xs_hbm.at[pl.ds(0, rows)], sem).wait()


def dispatch(h2_tiles, packed, pstarts, counts, n_slots):
    nt = packed.shape[0] // 2
    grid_spec = pltpu.PrefetchScalarGridSpec(
        num_scalar_prefetch=3,
        grid=(nt // TD_DISP,),
        in_specs=[pl.BlockSpec((TD_DISP * TILE_ROWS, LANES), lambda i, pk, ps, ct: (i, 0))],
        out_specs=[pl.BlockSpec(memory_space=pl.ANY),
                   pl.BlockSpec(memory_space=pltpu.SMEM)],
        scratch_shapes=[pltpu.VMEM((MOE_BLK * TILE_ROWS, LANES), F32),
                        pltpu.SemaphoreType.DMA(()),
                        pltpu.SemaphoreType.DMA(())],
    )
    return pl.pallas_call(
        _dispatch_kernel,
        grid_spec=grid_spec,
        out_shape=[jax.ShapeDtypeStruct((n_slots * TILE_ROWS, LANES), F32),
                   jax.ShapeDtypeStruct(packed.shape, jnp.int32)],
        compiler_params=_cparams(("arbitrary",)),
        name="dispatch",
    )(packed, pstarts, counts, h2_tiles)


def _experts_kernel(be_ref, nused_ref, valid_ref, xs_ref, wg_ref, wu_ref, wd_ref, ys_ref,
                    wg_scr, wu_scr, wd_scr):
    i = pl.program_id(0)
    valid = valid_ref[i]
    changed = (i == 0) | (be_ref[i] != be_ref[jnp.maximum(i - 1, 0)])
    chunk_rows = MOE_CHUNK * TILE_ROWS

    @pl.when(changed & (valid > 0))
    def _():
        wg_scr[...] = wg_ref[0].astype(BF16)
        wu_scr[...] = wu_ref[0].astype(BF16)
        wd_scr[...] = wd_ref[0].astype(BF16)

    def run(n_chunks):
        hmids = []
        for h in range(n_chunks):
            x = _load_token_tiles(xs_ref, h * chunk_rows, MOE_CHUNK).astype(BF16)
            g = jnp.dot(x, wg_scr[...], preferred_element_type=F32)
            u = jnp.dot(x, wu_scr[...], preferred_element_type=F32)
            hmids.append((g * jax.nn.sigmoid(g) * u).astype(BF16))
        for h, hmid in enumerate(hmids):
            _store_token_tiles(ys_ref, jnp.dot(hmid, wd_scr[...], preferred_element_type=F32), h * chunk_rows)
        if n_chunks * chunk_rows < ys_ref.shape[0]:
            ys_ref[n_chunks * chunk_rows:, :] = jnp.zeros((ys_ref.shape[0] - n_chunks * chunk_rows, LANES), F32)

    for n_chunks in range(MOE_BLK // MOE_CHUNK + 1):
        lo, hi = (n_chunks - 1) * MOE_CHUNK, n_chunks * MOE_CHUNK
        pl.when((valid > lo) & (valid <= hi))(functools.partial(run, n_chunks))


def experts(xs_tiles, block_expert, nused, block_valid, w_gate, w_up, w_down):
    d = D_MODEL
    nblk = block_expert.shape[0]
    blk_rows = MOE_BLK * TILE_ROWS
    wmap = lambda i, be, nu, bv: (be[i], 0, 0)
    grid_spec = pltpu.PrefetchScalarGridSpec(
        num_scalar_prefetch=3,
        grid=(nblk,),
        in_specs=[pl.BlockSpec((blk_rows, LANES), lambda i, be, nu, bv: (jnp.minimum(i, nu[0] - 1), 0)),
                  pl.BlockSpec((1, d, D_EXPERT), wmap),
                  pl.BlockSpec((1, d, D_EXPERT), wmap),
                  pl.BlockSpec((1, D_EXPERT, d), wmap)],
        out_specs=pl.BlockSpec((blk_rows, LANES), lambda i, be, nu, bv: (i, 0)),
        scratch_shapes=[pltpu.VMEM((d, D_EXPERT), BF16),
                        pltpu.VMEM((d, D_EXPERT), BF16),
                        pltpu.VMEM((D_EXPERT, d), BF16)],
    )
    return pl.pallas_call(
        _experts_kernel,
        grid_spec=grid_spec,
        out_shape=jax.ShapeDtypeStruct((nblk * blk_rows, LANES), F32),
        compiler_params=_cparams(("arbitrary",)),
        name="experts",
    )(block_expert, nused, block_valid, xs_tiles, w_gate, w_up, w_down)


def _combine_kernel(dest_ref, ys_hbm, x1_ref, meta_ref, mod_ref, gf_ref, o_ref, ybuf, sem):
    i = pl.program_id(0)
    nstep = pl.num_programs(0)
    tc = TC_COMB
    nt = dest_ref.shape[0] // 2
    half_rows = tc * TILE_ROWS
    buf_rows = 2 * half_rows

    def gather(step, slot):
        def issue(c, carry):
            t0 = c * DMA_CHUNK
            for u in range(DMA_CHUNK):
                for k in range(2):
                    d = dest_ref[k * nt + step * tc + t0 + u]
                    pltpu.make_async_copy(
                        ys_hbm.at[pl.ds(d * TILE_ROWS, TILE_ROWS)],
                        ybuf.at[pl.ds(slot * buf_rows + k * half_rows + (t0 + u) * TILE_ROWS, TILE_ROWS)],
                        sem.at[slot]).start(priority=k)
            return carry

        lax.fori_loop(0, tc // DMA_CHUNK, issue, 0)

    @pl.when(i == 0)
    def _():
        gather(0, 0)

    @pl.when(i + 1 < nstep)
    def _():
        gather(i + 1, (i + 1) % 2)

    slot = i % 2
    start = pl.multiple_of(slot * buf_rows, buf_rows)
    pltpu.make_async_copy(ys_hbm.at[pl.ds(0, buf_rows)], ybuf.at[pl.ds(start, buf_rows)], sem.at[slot]).wait()
    meta = meta_ref[...]
    y0 = _load_token_tiles(ybuf, start, tc)
    y1 = _load_token_tiles(ybuf, start + half_rows, tc)
    moe = y0 * meta[:, 4:5] + y1 * meta[:, 5:6]
    x2 = x1_ref[...] + mod_ref[0][5:6] * moe
    o_ref[...] = _rms(x2, gf_ref[...])


def combine(dest_flat, ys, x1_flat, meta, mods, g_final, n_per_batch):
    nt, d = x1_flat.shape
    tc = TC_COMB
    per_b = n_per_batch // tc
    grid_spec = pltpu.PrefetchScalarGridSpec(
        num_scalar_prefetch=1,
        grid=(nt // tc,),
        in_specs=[pl.BlockSpec(memory_space=pl.ANY),
                  pl.BlockSpec((tc, d), lambda i, ds: (i, 0)),
                  pl.BlockSpec((tc, LANES), lambda i, ds: (i, 0)),
                  pl.BlockSpec((1, 8, d), lambda i, ds: (i // per_b, 0, 0)),
                  pl.BlockSpec((1, d), lambda i, ds: (0, 0))],
        out_specs=pl.BlockSpec((tc, d), lambda i, ds: (i, 0)),
        scratch_shapes=[pltpu.VMEM((2 * 2 * tc * TILE_ROWS, LANES), F32),
                        pltpu.SemaphoreType.DMA((2,))],
    )
    return pl.pallas_call(
        _combine_kernel,
        grid_spec=grid_spec,
        out_shape=jax.ShapeDtypeStruct((nt, d), F32),
        compiler_params=_cparams(("arbitrary",)),
        name="combine",
    )(dest_flat, ys, x1_flat, meta, mods, g_final.reshape(1, d))


def _dispatch_plan(metat, counts_row, nt):
    counts = counts_row[N_GROUPS:N_GROUPS + N_EXPERTS].astype(jnp.int32)
    pcounts = ((counts + MOE_BLK - 1) // MOE_BLK) * MOE_BLK
    pends = jnp.cumsum(pcounts)
    pstarts = pends - pcounts
    packed = (metat[0:2] + float(N_EXPERTS) * metat[2:4]).astype(jnp.int32).reshape(-1)
    nblk = (nt * 2) // MOE_BLK + N_EXPERTS
    first_slot = jnp.arange(nblk, dtype=jnp.int32) * MOE_BLK
    block_expert = jnp.minimum(
        jnp.sum((pends[None, :] <= first_slot[:, None]).astype(jnp.int32), axis=1), N_EXPERTS - 1)
    nused = (pends[-1] // MOE_BLK).astype(jnp.int32).reshape(1)
    seg_end = jnp.sum(jnp.where(block_expert[:, None] == jnp.arange(N_EXPERTS)[None, :],
                                (pstarts + counts)[None, :], 0), axis=1)
    block_valid = jnp.where(first_slot < pends[-1], jnp.clip(seg_end - first_slot, 0, MOE_BLK), 0)
    return (packed, pstarts.astype(jnp.int32), counts, block_expert.astype(jnp.int32), nused,
            block_valid.astype(jnp.int32))


def kernel(x, c, ctx, c_ctx, w_ada, b_ada, g_norm1, w_in, w_fmix, rpb, g_out, w_out, g_norm2,
           w_router_group, w_router_expert, w_gate, w_up, w_down, g_final):
    b, n, d = x.shape
    assert (b, n, d) == (c.shape[0], SEQ, D_MODEL) and w_ada.shape[0] == 1
    nt = b * n

    cond8 = jnp.zeros((8, d), F32).at[0:b].set(c).at[b].set(c_ctx)
    mod = adaln(cond8, w_ada[0], b_ada[0])
    mods = jnp.pad(mod[0:b].reshape(b, N_MOD, d), ((0, 0), (0, 2), (0, 0)))
    mod_ctx = jnp.pad(mod[b].reshape(N_MOD, d), ((0, 2), (0, 0)))

    w_in_b = w_in[0].astype(BF16)
    qr, qp, kr, v, a = in_proj(x, mods, g_norm1[0], w_in_b)
    kc, vc = ctx_proj(ctx, mod_ctx, g_norm1[0], w_in_b[:, D_FOURIER + D_NA:])

    fr = dft_cols(dft_rows(a))
    na = attention(qr, qp, kr, v, kc, vc, bias_tables(rpb[0]))

    w_router = jnp.concatenate(
        [w_router_group[0], w_router_expert[0],
         jnp.zeros((d, LANES - N_GROUPS - N_EXPERTS), F32)], axis=1).astype(BF16)
    x1, h2_tiles, meta, metat, cnt = out_proj(fr, na, x, mods, w_fmix[0].astype(BF16),
                                              w_out[0].astype(BF16), g_out[0], g_norm2[0], w_router)

    packed, pstarts, counts, block_expert, nused, block_valid = _dispatch_plan(metat, cnt[0], nt)
    xs_tiles, dest = dispatch(h2_tiles, packed, pstarts, counts, block_expert.shape[0] * MOE_BLK)
    ys_tiles = experts(xs_tiles, block_expert, nused, block_valid, w_gate[0], w_up[0], w_down[0])
    out = combine(dest, ys_tiles, x1.reshape(nt, d), meta, mods, g_final, n)
    return out.reshape(b, n, d)
```

```python
import functools
import math

import numpy as np
import jax
import jax.numpy as jnp
from jax import lax
from jax.experimental import pallas as pl
from jax.experimental.pallas import tpu as pltpu

F32 = jnp.float32
BF16 = jnp.bfloat16

D_MODEL = 1024
GRID_W = 64
GRID_H = 128
SEQ = GRID_W * GRID_H
CTX_LEN = 256
D_FOURIER = 256
FOURIER_GROUP = 64
HEAD_DIM = 64
N_HEADS = 12
D_NA = N_HEADS * HEAD_DIM
N_PAIRS = N_HEADS // 2
NA_ROWS = 8
NA_COLS = 16
ROPE_THETA = 10000.0
N_GROUPS = 4
EXPERTS_PER_GROUP = 8
N_EXPERTS = N_GROUPS * EXPERTS_PER_GROUP
D_EXPERT = 512
N_MOD = 6
D_IN_PROJ = D_FOURIER + 3 * D_NA
EPS = 1e-6
LANES = 128
NEG = -1e30
LOG2E = math.log2(math.e)

TM_IN = 1024
ROW_PITCH = 72
TM_PROJ = 1024
ATT_ROWS = 32
DFT_TW = 16
DFT_TK1 = 16
MOE_BLK = 512
MOE_CHUNK = 256
TC_COMB = 256
VMEM_LIMIT = 56 * 1024 * 1024


def _cparams(sem):
    return pltpu.CompilerParams(dimension_semantics=sem, vmem_limit_bytes=VMEM_LIMIT)


def _mxu_const(table):
    return jnp.asarray(table, F32).astype(BF16)


@functools.lru_cache(maxsize=None)
def _rope_tables():
    t = np.arange(SEQ)
    row, col = t // GRID_W, t % GRID_W
    lane = np.arange(LANES)
    d = lane % HEAD_DIM
    chunk = d // 32
    e = d % 32
    j = e % 16
    inv = ROPE_THETA ** (-(j.astype(np.float64)) / 16.0)
    pos = np.where(chunk[None, :] == 0, row[:, None], col[:, None]).astype(np.float64)
    ang = pos * inv[None, :]
    cos = np.cos(ang)
    sin = np.sin(ang)
    first = (e < 16)[None, :]
    s_first = np.where(first, -sin, 0.0)
    s_second = np.where(first, 0.0, sin)
    return (cos.astype(np.float32), s_first.astype(np.float32), s_second.astype(np.float32))


@functools.lru_cache(maxsize=None)
def _chan_dft():
    c = np.arange(FOURIER_GROUP)
    ang = 2.0 * np.pi * ((c[:, None] * c[None, :]) % FOURIER_GROUP) / FOURIER_GROUP
    eye = np.eye(D_FOURIER // FOURIER_GROUP)
    re = np.kron(eye, np.cos(ang))
    im = np.kron(eye, -np.sin(ang))
    return np.concatenate([re, im], axis=1).astype(np.float32)


@functools.lru_cache(maxsize=None)
def _row_dft():
    k1 = np.arange(GRID_H)[:, None]
    r = np.arange(GRID_H)[None, :]
    out = np.zeros((GRID_W, 2 * GRID_H, 2 * GRID_H), np.float32)
    for w in range(GRID_W):
        m = (k1 * (GRID_W * r + w)) % SEQ
        ang = 2.0 * np.pi * m / SEQ
        c, s = np.cos(ang), np.sin(ang)
        out[w] = np.block([[c, s], [-s, c]])
    return out


@functools.lru_cache(maxsize=None)
def _col_dft():
    k2 = np.arange(GRID_W)
    ang = 2.0 * np.pi * ((k2[:, None] * k2[None, :]) % GRID_W) / GRID_W
    scale = 1.0 / math.sqrt(SEQ * FOURIER_GROUP)
    return (np.concatenate([np.cos(ang), np.sin(ang)], axis=1) * scale).astype(np.float32)


@functools.lru_cache(maxsize=None)
def _bias_index():
    c = np.arange(GRID_W)
    start = np.clip(c - NA_COLS // 2, 0, GRID_W - NA_COLS)
    valid = (c[None, :] >= start[:, None]) & (c[None, :] < start[:, None] + NA_COLS)
    dc = np.clip(c[None, :] - c[:, None] + (NA_COLS - 1), 0, 2 * NA_COLS - 2)
    return dc.astype(np.int32), valid


@functools.lru_cache(maxsize=None)
def _strict_lower(n):
    return np.tril(np.ones((n, n), np.float32), k=-1)


def _adaln_kernel(c_ref, w_ref, b_ref, o_ref):
    c = c_ref[...]
    s = c * jax.nn.sigmoid(c)
    o_ref[...] = jnp.dot(s.astype(BF16), w_ref[...].astype(BF16), preferred_element_type=F32) + b_ref[...]


def adaln(cond8, w, b):
    n = w.shape[1]
    tn = 1536
    return pl.pallas_call(
        _adaln_kernel,
        grid=(n // tn,),
        in_specs=[pl.BlockSpec((8, D_MODEL), lambda j: (0, 0)),
                  pl.BlockSpec((D_MODEL, tn), lambda j: (0, j)),
                  pl.BlockSpec((1, tn), lambda j: (0, j))],
        out_specs=pl.BlockSpec((8, tn), lambda j: (0, j)),
        out_shape=jax.ShapeDtypeStruct((8, n), F32),
        compiler_params=_cparams(("arbitrary",)),
        name="adaln",
    )(cond8, w, b.reshape(1, n))


def _norm_mod(x, g, shift, scale):
    ms = jnp.mean(x * x, axis=-1, keepdims=True)
    return (x * lax.rsqrt(ms + EPS) * g) * (1.0 + scale) + shift


def _in_proj_kernel(x_ref, mod_ref, g_ref, w_ref, cs_ref, cos_ref, s1_ref, s2_ref,
                    qr_ref, qp_ref, kr_ref, v_ref, a_ref, h_scr, a_scr):
    m = mod_ref[0]
    h_scr[...] = _norm_mod(x_ref[0], g_ref[...], m[0:1], m[1:2]).astype(BF16)
    cos, s1, s2 = cos_ref[...], s1_ref[...], s2_ref[...]

    def rope(t):
        return (t * cos + pltpu.roll(t, LANES - 16, axis=1) * s1 + pltpu.roll(t, 16, axis=1) * s2)

    f = jnp.dot(h_scr[...], w_ref[:, 0:D_FOURIER], preferred_element_type=F32)
    a = jnp.dot(f.astype(BF16), cs_ref[...], preferred_element_type=F32)
    n_planes = 2 * D_FOURIER // LANES
    tile_rows = a.shape[0] // GRID_W
    for p in range(n_planes):
        for r in range(tile_rows):
            a_scr[p, r * ROW_PITCH:r * ROW_PITCH + GRID_W, :] = a[r * GRID_W:(r + 1) * GRID_W,
                                                                LANES * p:LANES * (p + 1)]
    for w in range(GRID_W):
        for p in range(n_planes):
            slab = a_scr[p, pl.ds(w, tile_rows, stride=ROW_PITCH), :]
            lo = w * D_FOURIER + (p % 2) * LANES
            a_ref[0, p // 2, :, lo:lo + LANES] = slab.astype(BF16)

    scale = HEAD_DIM ** -0.5 * LOG2E
    for c in range(D_NA // 256):
        lo = D_FOURIER + 256 * c
        q = jnp.dot(h_scr[...], w_ref[:, lo:lo + 256], preferred_element_type=F32)
        k = jnp.dot(h_scr[...], w_ref[:, lo + D_NA:lo + D_NA + 256], preferred_element_type=F32)
        v = jnp.dot(h_scr[...], w_ref[:, lo + 2 * D_NA:lo + 2 * D_NA + 256], preferred_element_type=F32)
        v_ref[0, :, 256 * c:256 * c + 256] = v.astype(BF16)
        for s in range(2):
            sl = slice(LANES * s, LANES * (s + 1))
            ol = slice(256 * c + LANES * s, 256 * c + LANES * (s + 1))
            qs, ks = q[:, sl], k[:, sl]
            qp_ref[0, :, ol] = (qs * scale).astype(BF16)
            qr_ref[0, :, ol] = (rope(qs) * scale).astype(BF16)
            kr_ref[0, :, ol] = rope(ks).astype(BF16)


def in_proj(x, mods, g1, w_in_bf16):
    b, n, d = x.shape
    tm = TM_IN
    cos, s1, s2 = _rope_tables()
    cs = _mxu_const(_chan_dft())
    tok = lambda bi, i: (bi, i, 0)
    const2 = lambda bi, i: (0, 0)
    tab = pl.BlockSpec((tm, LANES), lambda bi, i: (i, 0))
    qkv_shape = jax.ShapeDtypeStruct((b, n, D_NA), BF16)
    qkv_spec = pl.BlockSpec((1, tm, D_NA), tok)
    return pl.pallas_call(
        _in_proj_kernel,
        grid=(b, n // tm),
        in_specs=[pl.BlockSpec((1, tm, d), tok),
                  pl.BlockSpec((1, 8, d), lambda bi, i: (bi, 0, 0)),
                  pl.BlockSpec((1, d), const2),
                  pl.BlockSpec((d, D_IN_PROJ), const2),
                  pl.BlockSpec((D_FOURIER, 2 * D_FOURIER), const2),
                  tab, tab, tab],
        out_specs=[qkv_spec, qkv_spec, qkv_spec, qkv_spec,
                   pl.BlockSpec((1, 2, tm // GRID_W, GRID_W * D_FOURIER), lambda bi, i: (bi, 0, i, 0))],
        out_shape=[qkv_shape, qkv_shape, qkv_shape, qkv_shape,
                   jax.ShapeDtypeStruct((b, 2, n // GRID_W, GRID_W * D_FOURIER), BF16)],
        scratch_shapes=[pltpu.VMEM((tm, d), BF16),
                        pltpu.VMEM((2 * D_FOURIER // LANES, (tm // GRID_W) * ROW_PITCH, LANES), F32)],
        compiler_params=_cparams(("arbitrary", "arbitrary")),
        name="in_proj",
    )(x, mods, g1.reshape(1, d), w_in_bf16, cs, jnp.asarray(cos), jnp.asarray(s1), jnp.asarray(s2))


def _ctx_proj_kernel(x_ref, mod_ref, g_ref, w_ref, k_ref, v_ref):
    m = mod_ref[...]
    h = _norm_mod(x_ref[0], g_ref[...], m[0:1], m[1:2]).astype(BF16)
    k_ref[0] = jnp.dot(h, w_ref[:, 0:D_NA], preferred_element_type=F32).astype(BF16)
    v_ref[0] = jnp.dot(h, w_ref[:, D_NA:2 * D_NA], preferred_element_type=F32).astype(BF16)


def ctx_proj(ctx, mod_ctx, g1, w_kv_bf16):
    b, l, d = ctx.shape
    shape = jax.ShapeDtypeStruct((b, l, D_NA), BF16)
    spec = pl.BlockSpec((1, l, D_NA), lambda bi: (bi, 0, 0))
    return pl.pallas_call(
        _ctx_proj_kernel,
        grid=(b,),
        in_specs=[pl.BlockSpec((1, l, d), lambda bi: (bi, 0, 0)),
                  pl.BlockSpec((8, d), lambda bi: (0, 0)),
                  pl.BlockSpec((1, d), lambda bi: (0, 0)),
                  pl.BlockSpec((d, 2 * D_NA), lambda bi: (0, 0))],
        out_specs=[spec, spec],
        out_shape=[shape, shape],
        compiler_params=_cparams(("arbitrary",)),
        name="ctx_proj",
    )(ctx, mod_ctx, g1.reshape(1, d), w_kv_bf16)


N_PLANES = D_FOURIER // LANES
K1_PITCH = GRID_H + 8
K2_PITCH = GRID_W + 8


def _dft_rows_kernel(a_ref, g_ref, z_ref, z_scr):
    for j in range(DFT_TW):
        sl = slice(D_FOURIER * j, D_FOURIER * (j + 1))
        rhs = jnp.concatenate([a_ref[0, 0, :, sl], a_ref[0, 1, :, sl]], axis=0)
        z = jnp.dot(g_ref[j], rhs, preferred_element_type=F32)
        for c in range(2):
            for p in range(N_PLANES):
                z_scr[c * N_PLANES + p, j * K1_PITCH:j * K1_PITCH + GRID_H, :] = (
                    z[c * GRID_H:(c + 1) * GRID_H, LANES * p:LANES * (p + 1)])
    for k1 in range(GRID_H):
        for c in range(2):
            for p in range(N_PLANES):
                slab = z_scr[c * N_PLANES + p, pl.ds(k1, DFT_TW, stride=K1_PITCH), :]
                lo = k1 * D_FOURIER + p * LANES
                z_ref[0, c, :, lo:lo + LANES] = slab.astype(BF16)


def dft_rows(a):
    b = a.shape[0]
    g = _mxu_const(_row_dft())
    return pl.pallas_call(
        _dft_rows_kernel,
        grid=(GRID_W // DFT_TW, b),
        in_specs=[pl.BlockSpec((1, 2, GRID_H, DFT_TW * D_FOURIER), lambda j, bi: (bi, 0, 0, j)),
                  pl.BlockSpec((DFT_TW, 2 * GRID_H, 2 * GRID_H), lambda j, bi: (j, 0, 0))],
        out_specs=pl.BlockSpec((1, 2, DFT_TW, GRID_H * D_FOURIER), lambda j, bi: (bi, 0, j, 0)),
        out_shape=jax.ShapeDtypeStruct((b, 2, GRID_W, GRID_H * D_FOURIER), BF16),
        scratch_shapes=[pltpu.VMEM((2 * N_PLANES, DFT_TW * K1_PITCH, LANES), F32)],
        compiler_params=_cparams(("arbitrary", "arbitrary")),
        name="dft_rows",
    )(a, g)


def _dft_cols_kernel(z_ref, cs_ref, o_ref, o_scr):
    for j in range(DFT_TK1):
        sl = slice(D_FOURIER * j, D_FOURIER * (j + 1))
        rhs = jnp.concatenate([z_ref[0, 0, :, sl], z_ref[0, 1, :, sl]], axis=0)
        out = jnp.dot(cs_ref[...], rhs, preferred_element_type=F32)
        for p in range(N_PLANES):
            o_scr[p, j * K2_PITCH:j * K2_PITCH + GRID_W, :] = out[:, LANES * p:LANES * (p + 1)]
    for k2 in range(GRID_W):
        for p in range(N_PLANES):
            slab = o_scr[p, pl.ds(k2, DFT_TK1, stride=K2_PITCH), :]
            o_ref[0, k2, :, LANES * p:LANES * (p + 1)] = slab.astype(BF16)


def dft_cols(z):
    b = z.shape[0]
    cs = _mxu_const(_col_dft())
    out = pl.pallas_call(
        _dft_cols_kernel,
        grid=(b, GRID_H // DFT_TK1),
        in_specs=[pl.BlockSpec((1, 2, GRID_W, DFT_TK1 * D_FOURIER), lambda bi, j: (bi, 0, 0, j)),
                  pl.BlockSpec((GRID_W, 2 * GRID_W), lambda bi, j: (0, 0))],
        out_specs=pl.BlockSpec((1, GRID_W, DFT_TK1, D_FOURIER), lambda bi, j: (bi, 0, j, 0)),
        out_shape=jax.ShapeDtypeStruct((b, GRID_W, GRID_H, D_FOURIER), BF16),
        scratch_shapes=[pltpu.VMEM((N_PLANES, DFT_TK1 * K2_PITCH, LANES), F32)],
        compiler_params=_cparams(("arbitrary", "arbitrary")),
        name="dft_cols",
    )(z, cs)
    return out.reshape(b, SEQ, D_FOURIER)


ROW_KEYS = NA_ROWS * GRID_W
ROW_TILES = ROW_KEYS // LANES
N_DR = 2 * NA_ROWS - 2
SOFT_ROWS = 32
ATT_PAIRS = ATT_ROWS // 2


def _attention_kernel(qr_ref, qp_ref, k_ref, v_ref, kc_ref, vc_ref, tb_ref, o_ref,
                      sc_scr, pc_scr, s_scr, p_scr):
    rb = pl.program_id(2)
    first1 = lax.broadcasted_iota(jnp.int32, (GRID_W, LANES), 1) < HEAD_DIM
    nt = (((1,), (1,)), ((), ()))

    def window(jp, i):
        r = rb * ATT_ROWS + 2 * jp + i
        rs = jnp.clip(r - NA_ROWS // 2, 0, GRID_H - NA_ROWS)
        return r, rs, pl.multiple_of(rs * GRID_W, GRID_W)

    def split_heads(q1):
        z1 = jnp.zeros_like(q1)
        return jnp.concatenate([jnp.where(first1, q1, z1), jnp.where(first1, z1, q1)], axis=0)

    def scores(jp):
        qp_rows = []
        for i in range(2):
            _, _, koff = window(jp, i)
            qs = slice((2 * jp + i) * GRID_W, (2 * jp + i + 1) * GRID_W)
            kw = k_ref[0, pl.ds(koff, ROW_KEYS), :]
            s_scr[jp, i * 2 * GRID_W:(i + 1) * 2 * GRID_W, :] = lax.dot_general(
                split_heads(qr_ref[0, qs, :]), kw, nt, preferred_element_type=F32)
            qp_rows.append(split_heads(qp_ref[0, qs, :]))
        sc_scr[jp] = lax.dot_general(jnp.concatenate(qp_rows, axis=0), kc_ref[0], nt,
                                     preferred_element_type=F32)

    def softmax(jp):
        for c in range(4):
            i, hh = c // 2, c % 2
            r, rs, _ = window(jp, i)
            d0 = rs - r + NA_ROWS - 1
            for h in range(GRID_W // SOFT_ROWS):
                lo = h * SOFT_ROWS
                rows = slice(c * GRID_W + lo, c * GRID_W + lo + SOFT_ROWS)
                tiles = [s_scr[jp, rows, t * LANES:(t + 1) * LANES]
                         + tb_ref[0, hh, d0 + 2 * t, lo:lo + SOFT_ROWS, :] for t in range(ROW_TILES)]
                sc = sc_scr[jp, rows, :]
                mt = jnp.maximum(sc[:, :LANES], sc[:, LANES:])
                for tl in tiles:
                    mt = jnp.maximum(mt, tl)
                m = jnp.max(mt, axis=1, keepdims=True)
                pc_scr[jp, rows, :] = jnp.exp2(sc - m).astype(BF16)
                for t, tl in enumerate(tiles):
                    p_scr[jp, rows, t * LANES:(t + 1) * LANES] = jnp.exp2(tl - m).astype(BF16)

    def weighted_values(jp):
        vc = jnp.concatenate([vc_ref[0], jnp.ones((CTX_LEN, LANES), BF16)], axis=1)
        oc = jnp.dot(pc_scr[jp], vc, preferred_element_type=F32)
        for i in range(2):
            _, _, koff = window(jp, i)
            rows = slice(i * 2 * GRID_W, (i + 1) * 2 * GRID_W)
            vw = jnp.concatenate([v_ref[0, pl.ds(koff, ROW_KEYS), :], jnp.ones((ROW_KEYS, LANES), BF16)],
                                 axis=1)
            o = jnp.dot(p_scr[jp, rows, :], vw, preferred_element_type=F32) + oc[rows]
            oa, ob = o[:GRID_W], o[GRID_W:]
            out = jnp.where(first1, oa[:, :LANES] / oa[:, LANES:], ob[:, :LANES] / ob[:, LANES:])
            o_ref[0, (2 * jp + i) * GRID_W:(2 * jp + i + 1) * GRID_W, :] = out.astype(BF16)

    scores(0)
    scores(1)
    softmax(0)
    for jp in range(ATT_PAIRS):
        if jp + 2 < ATT_PAIRS:
            scores(jp + 2)
        if jp + 1 < ATT_PAIRS:
            softmax(jp + 1)
        weighted_values(jp)


def bias_tables(rpb):
    dc, valid = _bias_index()
    n_dc = 2 * NA_COLS - 1
    onehot = (dc.reshape(1, -1) == np.arange(n_dc).reshape(-1, 1)).astype(np.float32)
    t = jnp.dot(rpb.reshape(-1, n_dc), jnp.asarray(onehot), precision=lax.Precision.HIGHEST)
    t = jnp.where(valid[None, None], LOG2E * t.reshape(N_HEADS, 2 * NA_ROWS - 1, GRID_W, GRID_W), NEG)
    tb = pl.pallas_call(
        _bias_pairs_kernel,
        grid=(N_HEADS,),
        in_specs=[pl.BlockSpec((1, N_DR + 1, GRID_W, GRID_W), lambda h: (h, 0, 0, 0))],
        out_specs=pl.BlockSpec((1, N_DR, GRID_W, LANES), lambda h: (h, 0, 0, 0)),
        out_shape=jax.ShapeDtypeStruct((N_HEADS, N_DR, GRID_W, LANES), F32),
        compiler_params=_cparams(("arbitrary",)),
        name="bias_pairs",
    )(t)
    return tb.reshape(N_PAIRS, 2, N_DR, GRID_W, LANES)


def _bias_pairs_kernel(t_ref, o_ref):
    for d in range(N_DR):
        o_ref[0, d] = jnp.concatenate([t_ref[0, d], t_ref[0, d + 1]], axis=1)


def attention(qr, qp, kr, v, kc, vc, tb):
    b, n, _ = qr.shape
    tq = ATT_ROWS * GRID_W
    qspec = pl.BlockSpec((1, tq, LANES), lambda bi, hp, i: (bi, i, hp))
    kspec = pl.BlockSpec((1, n, LANES), lambda bi, hp, i: (bi, 0, hp))
    cspec = pl.BlockSpec((1, CTX_LEN, LANES), lambda bi, hp, i: (bi, 0, hp))
    return pl.pallas_call(
        _attention_kernel,
        grid=(b, N_PAIRS, GRID_H // ATT_ROWS),
        in_specs=[qspec, qspec, kspec, kspec, cspec, cspec,
                  pl.BlockSpec((1, 2, N_DR, GRID_W, LANES), lambda bi, hp, i: (hp, 0, 0, 0, 0))],
        out_specs=qspec,
        out_shape=jax.ShapeDtypeStruct((b, n, D_NA), BF16),
        scratch_shapes=[pltpu.VMEM((ATT_PAIRS, 4 * GRID_W, CTX_LEN), F32),
                        pltpu.VMEM((ATT_PAIRS, 4 * GRID_W, CTX_LEN), BF16),
                        pltpu.VMEM((ATT_PAIRS, 4 * GRID_W, ROW_KEYS), F32),
                        pltpu.VMEM((ATT_PAIRS, 4 * GRID_W, ROW_KEYS), BF16)],
        compiler_params=_cparams(("arbitrary", "arbitrary", "arbitrary")),
        name="attention",
    )(qr, qp, kr, v, kc, vc, tb)


def _rms(x, g):
    ms = jnp.mean(x * x, axis=-1, keepdims=True)
    return x * lax.rsqrt(ms + EPS) * g


TILE_ROWS = D_MODEL // LANES


def _store_token_tiles(ref, val, start=0):
    rows = val.shape[0]
    for j in range(TILE_ROWS):
        ref[pl.ds(start + j, rows, stride=TILE_ROWS), :] = val[:, LANES * j:LANES * (j + 1)]


def _load_token_tiles(ref, start, rows):
    return jnp.concatenate(
        [ref[pl.ds(start + j, rows, stride=TILE_ROWS), :] for j in range(TILE_ROWS)], axis=1)


def _out_proj_kernel(fr_ref, na_ref, x_ref, mod_ref, wf_ref, wo_ref, go_ref, g2_ref, wr_ref, tri_ref,
                     x1_ref, h2_ref, meta_ref, metat_ref, cnt_ref, run_scr):
    @pl.when((pl.program_id(0) == 0) & (pl.program_id(1) == 0))
    def _():
        run_scr[...] = jnp.zeros_like(run_scr)

    m = mod_ref[0]
    go = go_ref[...]
    fo = jnp.dot(fr_ref[0], wf_ref[...], preferred_element_type=F32)
    fn = _rms(fo, go[:, :D_FOURIER]).astype(BF16)
    nn = _rms(na_ref[0].astype(F32), go[:, D_FOURIER:]).astype(BF16)
    y = (jnp.dot(fn, wo_ref[0:D_FOURIER, :], preferred_element_type=F32)
         + jnp.dot(nn, wo_ref[D_FOURIER:, :], preferred_element_type=F32))
    x1 = x_ref[0] + m[2:3] * y
    x1_ref[0] = x1
    h2 = _norm_mod(x1, g2_ref[...], m[3:4], m[4:5])
    _store_token_tiles(h2_ref, h2)
    logits = jnp.dot(h2.astype(BF16), wr_ref[...], preferred_element_type=F32)

    tm = logits.shape[0]
    lane = lax.broadcasted_iota(jnp.int32, (tm, LANES), 1).astype(F32)
    ninf = jnp.float32(-jnp.inf)

    def argmax_first(vals):
        mx = jnp.max(vals, axis=1, keepdims=True)
        idx = jnp.min(jnp.where(vals == mx, lane, float(LANES)), axis=1, keepdims=True)
        return mx, idx

    lg = jnp.where(lane < N_GROUPS, logits, ninf)
    gmax, gidx = argmax_first(lg)
    pg = 1.0 / jnp.sum(jnp.exp(lg - gmax), axis=1, keepdims=True)
    lo = N_GROUPS + EXPERTS_PER_GROUP * gidx
    le = jnp.where((lane >= lo) & (lane < lo + EXPERTS_PER_GROUP), logits, ninf)
    e1, i1 = argmax_first(le)
    e2, i2 = argmax_first(jnp.where(lane == i1, ninf, le))
    dd = jnp.exp(e2 - e1)
    gate1 = pg / (1.0 + dd)
    gate2 = pg * dd / (1.0 + dd)

    hot1 = lane == i1
    hot2 = lane == i2
    onehot = jnp.where(hot1 | hot2, 1.0, 0.0)
    cnt = jnp.dot(tri_ref[...], onehot.astype(BF16), preferred_element_type=F32) + run_scr[...]
    rank1 = jnp.sum(jnp.where(hot1, cnt, 0.0), axis=1, keepdims=True)
    rank2 = jnp.sum(jnp.where(hot2, cnt, 0.0), axis=1, keepdims=True)
    run_scr[...] = run_scr[...] + jnp.sum(onehot, axis=0, keepdims=True)

    meta = jnp.where(lane == 0, i1 - N_GROUPS,
           jnp.where(lane == 1, i2 - N_GROUPS,
           jnp.where(lane == 2, rank1,
           jnp.where(lane == 3, rank2,
           jnp.where(lane == 4, gate1,
           jnp.where(lane == 5, gate2, 0.0))))))
    meta_ref[...] = meta
    metat_ref[...] = jnp.transpose(meta)[0:8, :]
    cnt_ref[...] = jnp.broadcast_to(run_scr[...], cnt_ref.shape)


def out_proj(fr, na, x, mods, w_fmix_bf16, w_out_bf16, g_out, g2, w_router_bf16):
    b, n, d = x.shape
    tm = TM_PROJ
    steps = n // tm
    tok = lambda bi, i: (bi, i, 0)
    const2 = lambda bi, i: (0, 0)
    flat = lambda bi, i: (bi * steps + i, 0)
    tri = _mxu_const(_strict_lower(tm))
    return pl.pallas_call(
        _out_proj_kernel,
        grid=(b, steps),
        in_specs=[pl.BlockSpec((1, tm, D_FOURIER), tok),
                  pl.BlockSpec((1, tm, D_NA), tok),
                  pl.BlockSpec((1, tm, d), tok),
                  pl.BlockSpec((1, 8, d), lambda bi, i: (bi, 0, 0)),
                  pl.BlockSpec((D_FOURIER, D_FOURIER), const2),
                  pl.BlockSpec((d, d), const2),
                  pl.BlockSpec((1, d), const2),
                  pl.BlockSpec((1, d), const2),
                  pl.BlockSpec((d, LANES), const2),
                  pl.BlockSpec((tm, tm), const2)],
        out_specs=[pl.BlockSpec((1, tm, d), tok),
                   pl.BlockSpec((tm * TILE_ROWS, LANES), flat),
                   pl.BlockSpec((tm, LANES), flat),
                   pl.BlockSpec((8, tm), lambda bi, i: (0, bi * steps + i)),
                   pl.BlockSpec((8, LANES), const2)],
        out_shape=[jax.ShapeDtypeStruct((b, n, d), F32),
                   jax.ShapeDtypeStruct((b * n * TILE_ROWS, LANES), F32),
                   jax.ShapeDtypeStruct((b * n, LANES), F32),
                   jax.ShapeDtypeStruct((8, b * n), F32),
                   jax.ShapeDtypeStruct((8, LANES), F32)],
        scratch_shapes=[pltpu.VMEM((1, LANES), F32)],
        compiler_params=_cparams(("arbitrary", "arbitrary")),
        name="out_proj",
    )(fr, na, x, mods, w_fmix_bf16, w_out_bf16, g_out.reshape(1, d), g2.reshape(1, d), w_router_bf16, tri)


DMA_CHUNK = 8


TD_DISP = 1024


def _dispatch_kernel(dest_ref, pstart_ref, count_ref, h2_ref, xs_hbm, zero_scr, sem, pad_sem):
    i = pl.program_id(0)
    nt = dest_ref.shape[0] // 2

    @pl.when(i == 0)
    def _():
        zero_scr[...] = jnp.zeros_like(zero_scr)

        def per_expert(e, npad):
            lo = pstart_ref[e] + count_ref[e]
            mid = pstart_ref[e] + ((count_ref[e] + MOE_CHUNK - 1) // MOE_CHUNK) * MOE_CHUNK
            hi = pstart_ref[e] + ((count_ref[e] + MOE_BLK - 1) // MOE_BLK) * MOE_BLK

            def fill(s, carry):
                pltpu.make_async_copy(zero_scr.at[pl.ds(0, TILE_ROWS)],
                                      xs_hbm.at[pl.ds(s * TILE_ROWS, TILE_ROWS)], pad_sem).start()
                return carry

            def fill_chunk(s, carry):
                pltpu.make_async_copy(zero_scr.at[pl.ds(0, MOE_CHUNK * TILE_ROWS)],
                                      xs_hbm.at[pl.ds(mid * TILE_ROWS + s * MOE_CHUNK * TILE_ROWS,
                                                      MOE_CHUNK * TILE_ROWS)], pad_sem).start()
                return carry

            lax.fori_loop(lo, mid, fill, 0)
            lax.fori_loop(0, (hi - mid) // MOE_CHUNK, fill_chunk, 0)
            return npad + (hi - lo)

        npad = lax.fori_loop(0, N_EXPERTS, per_expert, 0)

        blk_rows = MOE_BLK * TILE_ROWS
        first_free = (pstart_ref[N_EXPERTS - 1] + count_ref[N_EXPERTS - 1] + MOE_BLK - 1) // MOE_BLK
        n_blocks = xs_hbm.shape[0] // blk_rows

        def fill_block(bk, carry):
            pltpu.make_async_copy(zero_scr, xs_hbm.at[pl.ds(bk * blk_rows, blk_rows)], pad_sem).start()
            return carry

        lax.fori_loop(first_free, n_blocks, fill_block, 0)
        rows = npad * TILE_ROWS + (n_blocks - first_free) * blk_rows

        @pl.when(rows > 0)
        def _():
            pltpu.make_async_copy(xs_hbm.at[pl.ds(0, rows)], xs_hbm.at[pl.ds(0, rows)], pad_sem).wait()

    def issue(c, carry):
        t0 = c * DMA_CHUNK
        for u in range(DMA_CHUNK):
            for k in range(2):
                d = dest_ref[k * nt + i * TD_DISP + t0 + u]
                pltpu.make_async_copy(h2_ref.at[pl.ds((t0 + u) * TILE_ROWS, TILE_ROWS)],
                                      xs_hbm.at[pl.ds(d * TILE_ROWS, TILE_ROWS)], sem).start(priority=k)
        return carry

    lax.fori_loop(0, TD_DISP // DMA_CHUNK, issue, 0)
    rows = 2 * TD_DISP * TILE_ROWS
    pltpu.make_async_copy(xs_hbm.at[pl.ds(0, rows)], xs_hbm.at[pl.ds(0, rows)], sem).wait()


def dispatch(h2_tiles, dest, pstarts, counts, n_slots):
    nt = dest.shape[0] // 2
    grid_spec = pltpu.PrefetchScalarGridSpec(
        num_scalar_prefetch=3,
        grid=(nt // TD_DISP,),
        in_specs=[pl.BlockSpec((TD_DISP * TILE_ROWS, LANES), lambda i, ds, ps, ct: (i, 0))],
        out_specs=pl.BlockSpec(memory_space=pl.ANY),
        scratch_shapes=[pltpu.VMEM((MOE_BLK * TILE_ROWS, LANES), F32),
                        pltpu.SemaphoreType.DMA(()),
                        pltpu.SemaphoreType.DMA(())],
    )
    return pl.pallas_call(
        _dispatch_kernel,
        grid_spec=grid_spec,
        out_shape=jax.ShapeDtypeStruct((n_slots * TILE_ROWS, LANES), F32),
        compiler_params=_cparams(("arbitrary",)),
        name="dispatch",
    )(dest, pstarts, counts, h2_tiles)


def _experts_kernel(be_ref, nused_ref, valid_ref, xs_ref, wg_ref, wu_ref, wd_ref, ys_ref,
                    wg_scr, wu_scr, wd_scr):
    i = pl.program_id(0)
    valid = valid_ref[i]
    changed = (i == 0) | (be_ref[i] != be_ref[jnp.maximum(i - 1, 0)])
    chunk_rows = MOE_CHUNK * TILE_ROWS

    @pl.when(changed & (valid > 0))
    def _():
        wg_scr[...] = wg_ref[0].astype(BF16)
        wu_scr[...] = wu_ref[0].astype(BF16)
        wd_scr[...] = wd_ref[0].astype(BF16)

    def run(n_chunks):
        hmids = []
        for h in range(n_chunks):
            x = _load_token_tiles(xs_ref, h * chunk_rows, MOE_CHUNK).astype(BF16)
            g = jnp.dot(x, wg_scr[...], preferred_element_type=F32)
            u = jnp.dot(x, wu_scr[...], preferred_element_type=F32)
            hmids.append((g * jax.nn.sigmoid(g) * u).astype(BF16))
        for h, hmid in enumerate(hmids):
            _store_token_tiles(ys_ref, jnp.dot(hmid, wd_scr[...], preferred_element_type=F32), h * chunk_rows)
        if n_chunks * chunk_rows < ys_ref.shape[0]:
            ys_ref[n_chunks * chunk_rows:, :] = jnp.zeros((ys_ref.shape[0] - n_chunks * chunk_rows, LANES), F32)

    for n_chunks in range(MOE_BLK // MOE_CHUNK + 1):
        lo, hi = (n_chunks - 1) * MOE_CHUNK, n_chunks * MOE_CHUNK
        pl.when((valid > lo) & (valid <= hi))(functools.partial(run, n_chunks))


def experts(xs_tiles, block_expert, nused, block_valid, w_gate, w_up, w_down):
    d = D_MODEL
    nblk = block_expert.shape[0]
    blk_rows = MOE_BLK * TILE_ROWS
    wmap = lambda i, be, nu, bv: (be[i], 0, 0)
    grid_spec = pltpu.PrefetchScalarGridSpec(
        num_scalar_prefetch=3,
        grid=(nblk,),
        in_specs=[pl.BlockSpec((blk_rows, LANES), lambda i, be, nu, bv: (jnp.minimum(i, nu[0] - 1), 0)),
                  pl.BlockSpec((1, d, D_EXPERT), wmap),
                  pl.BlockSpec((1, d, D_EXPERT), wmap),
                  pl.BlockSpec((1, D_EXPERT, d), wmap)],
        out_specs=pl.BlockSpec((blk_rows, LANES), lambda i, be, nu, bv: (i, 0)),
        scratch_shapes=[pltpu.VMEM((d, D_EXPERT), BF16),
                        pltpu.VMEM((d, D_EXPERT), BF16),
                        pltpu.VMEM((D_EXPERT, d), BF16)],
    )
    return pl.pallas_call(
        _experts_kernel,
        grid_spec=grid_spec,
        out_shape=jax.ShapeDtypeStruct((nblk * blk_rows, LANES), F32),
        compiler_params=_cparams(("arbitrary",)),
        name="experts",
    )(block_expert, nused, block_valid, xs_tiles, w_gate, w_up, w_down)


def _combine_kernel(dest_ref, ys_hbm, x1_ref, meta_ref, mod_ref, gf_ref, o_ref, ybuf, sem):
    i = pl.program_id(0)
    nstep = pl.num_programs(0)
    tc = TC_COMB
    nt = dest_ref.shape[0] // 2
    half_rows = tc * TILE_ROWS
    buf_rows = 2 * half_rows

    def gather(step, slot):
        def issue(c, carry):
            t0 = c * DMA_CHUNK
            for u in range(DMA_CHUNK):
                for k in range(2):
                    d = dest_ref[k * nt + step * tc + t0 + u]
                    pltpu.make_async_copy(
                        ys_hbm.at[pl.ds(d * TILE_ROWS, TILE_ROWS)],
                        ybuf.at[pl.ds(slot * buf_rows + k * half_rows + (t0 + u) * TILE_ROWS, TILE_ROWS)],
                        sem.at[slot]).start(priority=k)
            return carry

        lax.fori_loop(0, tc // DMA_CHUNK, issue, 0)

    @pl.when(i == 0)
    def _():
        gather(0, 0)

    @pl.when(i + 1 < nstep)
    def _():
        gather(i + 1, (i + 1) % 2)

    slot = i % 2
    start = pl.multiple_of(slot * buf_rows, buf_rows)
    pltpu.make_async_copy(ys_hbm.at[pl.ds(0, buf_rows)], ybuf.at[pl.ds(start, buf_rows)], sem.at[slot]).wait()
    meta = meta_ref[...]
    y0 = _load_token_tiles(ybuf, start, tc)
    y1 = _load_token_tiles(ybuf, start + half_rows, tc)
    moe = y0 * meta[:, 4:5] + y1 * meta[:, 5:6]
    x2 = x1_ref[...] + mod_ref[0][5:6] * moe
    o_ref[...] = _rms(x2, gf_ref[...])


def combine(dest_flat, ys, x1_flat, meta, mods, g_final, n_per_batch):
    nt, d = x1_flat.shape
    tc = TC_COMB
    per_b = n_per_batch // tc
    grid_spec = pltpu.PrefetchScalarGridSpec(
        num_scalar_prefetch=1,
        grid=(nt // tc,),
        in_specs=[pl.BlockSpec(memory_space=pl.ANY),
                  pl.BlockSpec((tc, d), lambda i, ds: (i, 0)),
                  pl.BlockSpec((tc, LANES), lambda i, ds: (i, 0)),
                  pl.BlockSpec((1, 8, d), lambda i, ds: (i // per_b, 0, 0)),
                  pl.BlockSpec((1, d), lambda i, ds: (0, 0))],
        out_specs=pl.BlockSpec((tc, d), lambda i, ds: (i, 0)),
        scratch_shapes=[pltpu.VMEM((2 * 2 * tc * TILE_ROWS, LANES), F32),
                        pltpu.SemaphoreType.DMA((2,))],
    )
    return pl.pallas_call(
        _combine_kernel,
        grid_spec=grid_spec,
        out_shape=jax.ShapeDtypeStruct((nt, d), F32),
        compiler_params=_cparams(("arbitrary",)),
        name="combine",
    )(dest_flat, ys, x1_flat, meta, mods, g_final.reshape(1, d))


def _slots_kernel(pstart_ref, metat_ref, dest_ref):
    eid = metat_ref[0:2, :]
    slot = metat_ref[2:4, :]
    for e in range(N_EXPERTS):
        slot = slot + jnp.where(eid == float(e), pstart_ref[e].astype(F32), 0.0)
    dest_ref[...] = slot.astype(jnp.int32)


def _dispatch_plan(metat, counts_row, nt):
    counts = counts_row[N_GROUPS:N_GROUPS + N_EXPERTS].astype(jnp.int32)
    pcounts = ((counts + MOE_BLK - 1) // MOE_BLK) * MOE_BLK
    pends = jnp.cumsum(pcounts)
    pstarts = pends - pcounts
    dest = pl.pallas_call(
        _slots_kernel,
        grid_spec=pltpu.PrefetchScalarGridSpec(
            num_scalar_prefetch=1, grid=(1,),
            in_specs=[pl.BlockSpec(metat.shape, lambda i, ps: (0, 0))],
            out_specs=pl.BlockSpec((2, nt), lambda i, ps: (0, 0))),
        out_shape=jax.ShapeDtypeStruct((2, nt), jnp.int32),
        compiler_params=_cparams(("arbitrary",)),
        name="slots",
    )(pstarts.astype(jnp.int32), metat).reshape(-1)
    nblk = (nt * 2) // MOE_BLK + N_EXPERTS
    first_slot = jnp.arange(nblk, dtype=jnp.int32) * MOE_BLK
    block_expert = jnp.minimum(
        jnp.sum((pends[None, :] <= first_slot[:, None]).astype(jnp.int32), axis=1), N_EXPERTS - 1)
    nused = (pends[-1] // MOE_BLK).astype(jnp.int32).reshape(1)
    seg_end = jnp.sum(jnp.where(block_expert[:, None] == jnp.arange(N_EXPERTS)[None, :],
                                (pstarts + counts)[None, :], 0), axis=1)
    block_valid = jnp.where(first_slot < pends[-1], jnp.clip(seg_end - first_slot, 0, MOE_BLK), 0)
    return (dest, pstarts.astype(jnp.int32), counts, block_expert.astype(jnp.int32), nused,
            block_valid.astype(jnp.int32))


def kernel(x, c, ctx, c_ctx, w_ada, b_ada, g_norm1, w_in, w_fmix, rpb, g_out, w_out, g_norm2,
           w_router_group, w_router_expert, w_gate, w_up, w_down, g_final):
    b, n, d = x.shape
    assert (b, n, d) == (c.shape[0], SEQ, D_MODEL) and w_ada.shape[0] == 1
    nt = b * n

    cond8 = jnp.zeros((8, d), F32).at[0:b].set(c).at[b].set(c_ctx)
    mod = adaln(cond8, w_ada[0], b_ada[0])
    mods = jnp.pad(mod[0:b].reshape(b, N_MOD, d), ((0, 0), (0, 2), (0, 0)))
    mod_ctx = jnp.pad(mod[b].reshape(N_MOD, d), ((0, 2), (0, 0)))

    w_in_b = w_in[0].astype(BF16)
    qr, qp, kr, v, a = in_proj(x, mods, g_norm1[0], w_in_b)
    kc, vc = ctx_proj(ctx, mod_ctx, g_norm1[0], w_in_b[:, D_FOURIER + D_NA:])

    fr = dft_cols(dft_rows(a))
    na = attention(qr, qp, kr, v, kc, vc, bias_tables(rpb[0]))

    w_router = jnp.concatenate(
        [w_router_group[0], w_router_expert[0],
         jnp.zeros((d, LANES - N_GROUPS - N_EXPERTS), F32)], axis=1).astype(BF16)
    x1, h2_tiles, meta, metat, cnt = out_proj(fr, na, x, mods, w_fmix[0].astype(BF16),
                                              w_out[0].astype(BF16), g_out[0], g_norm2[0], w_router)

    dest, pstarts, counts, block_expert, nused, block_valid = _dispatch_plan(metat, cnt[0], nt)
    xs_tiles = dispatch(h2_tiles, dest, pstarts, counts, block_expert.shape[0] * MOE_BLK)
    ys_tiles = experts(xs_tiles, block_expert, nused, block_valid, w_gate[0], w_up[0], w_down[0])
    out = combine(dest, ys_tiles, x1.reshape(nt, d), meta, mods, g_final, n)
    return out.reshape(b, n, d)
```

```python
import functools
import math

import numpy as np
import jax
import jax.numpy as jnp
from jax import lax
from jax.experimental import pallas as pl
from jax.experimental.pallas import tpu as pltpu

F32 = jnp.float32
BF16 = jnp.bfloat16

D_MODEL = 1024
GRID_W = 64
GRID_H = 128
SEQ = GRID_W * GRID_H
CTX_LEN = 256
D_FOURIER = 256
FOURIER_GROUP = 64
HEAD_DIM = 64
N_HEADS = 12
D_NA = N_HEADS * HEAD_DIM
N_PAIRS = N_HEADS // 2
NA_ROWS = 8
NA_COLS = 16
ROPE_THETA = 10000.0
N_GROUPS = 4
EXPERTS_PER_GROUP = 8
N_EXPERTS = N_GROUPS * EXPERTS_PER_GROUP
D_EXPERT = 512
N_MOD = 6
D_IN_PROJ = D_FOURIER + 3 * D_NA
EPS = 1e-6
LANES = 128
NEG = -1e30
LOG2E = math.log2(math.e)

TM_IN = 1024
ROW_PITCH = 72
TM_PROJ = 1024
OUT_CHUNK = 512
ATT_ROWS = 64
DFT_TW = 16
DFT_TK1 = 16
MOE_BLK = 512
MOE_CHUNK = 256
TC_COMB = 256
VMEM_LIMIT = 56 * 1024 * 1024


def _cparams(sem):
    return pltpu.CompilerParams(dimension_semantics=sem, vmem_limit_bytes=VMEM_LIMIT)


def _mxu_const(table):
    return jnp.asarray(table, F32).astype(BF16)


@functools.lru_cache(maxsize=None)
def _rope_tables():
    t = np.arange(SEQ)
    row, col = t // GRID_W, t % GRID_W
    lane = np.arange(LANES)
    d = lane % HEAD_DIM
    chunk = d // 32
    e = d % 32
    j = e % 16
    inv = ROPE_THETA ** (-(j.astype(np.float64)) / 16.0)
    pos = np.where(chunk[None, :] == 0, row[:, None], col[:, None]).astype(np.float64)
    ang = pos * inv[None, :]
    cos = np.cos(ang)
    sin = np.sin(ang)
    first = (e < 16)[None, :]
    s_first = np.where(first, -sin, 0.0)
    s_second = np.where(first, 0.0, sin)
    return (cos.astype(np.float32), s_first.astype(np.float32), s_second.astype(np.float32))


@functools.lru_cache(maxsize=None)
def _chan_dft():
    c = np.arange(FOURIER_GROUP)
    ang = 2.0 * np.pi * ((c[:, None] * c[None, :]) % FOURIER_GROUP) / FOURIER_GROUP
    eye = np.eye(D_FOURIER // FOURIER_GROUP)
    re = np.kron(eye, np.cos(ang))
    im = np.kron(eye, -np.sin(ang))
    return np.concatenate([re, im], axis=1).astype(np.float32)


@functools.lru_cache(maxsize=None)
def _row_dft():
    k1 = np.arange(GRID_H)[:, None]
    r = np.arange(GRID_H)[None, :]
    out = np.zeros((GRID_W, 2 * GRID_H, 2 * GRID_H), np.float32)
    for w in range(GRID_W):
        m = (k1 * (GRID_W * r + w)) % SEQ
        ang = 2.0 * np.pi * m / SEQ
        c, s = np.cos(ang), np.sin(ang)
        out[w] = np.block([[c, s], [-s, c]])
    return out


@functools.lru_cache(maxsize=None)
def _col_dft():
    k2 = np.arange(GRID_W)
    ang = 2.0 * np.pi * ((k2[:, None] * k2[None, :]) % GRID_W) / GRID_W
    scale = 1.0 / math.sqrt(SEQ * FOURIER_GROUP)
    return (np.concatenate([np.cos(ang), np.sin(ang)], axis=1) * scale).astype(np.float32)


@functools.lru_cache(maxsize=None)
def _bias_index():
    c = np.arange(GRID_W)
    start = np.clip(c - NA_COLS // 2, 0, GRID_W - NA_COLS)
    valid = (c[None, :] >= start[:, None]) & (c[None, :] < start[:, None] + NA_COLS)
    dc = np.clip(c[None, :] - c[:, None] + (NA_COLS - 1), 0, 2 * NA_COLS - 2)
    return dc.astype(np.int32), valid


@functools.lru_cache(maxsize=None)
def _strict_lower(n):
    return np.tril(np.ones((n, n), np.float32), k=-1)


def _adaln_kernel(c_ref, w_ref, b_ref, o_ref):
    c = c_ref[...]
    s = c * jax.nn.sigmoid(c)
    o_ref[...] = jnp.dot(s.astype(BF16), w_ref[...].astype(BF16), preferred_element_type=F32) + b_ref[...]


def adaln(cond8, w, b):
    n = w.shape[1]
    tn = 1536
    return pl.pallas_call(
        _adaln_kernel,
        grid=(n // tn,),
        in_specs=[pl.BlockSpec((8, D_MODEL), lambda j: (0, 0)),
                  pl.BlockSpec((D_MODEL, tn), lambda j: (0, j)),
                  pl.BlockSpec((1, tn), lambda j: (0, j))],
        out_specs=pl.BlockSpec((8, tn), lambda j: (0, j)),
        out_shape=jax.ShapeDtypeStruct((8, n), F32),
        compiler_params=_cparams(("arbitrary",)),
        name="adaln",
    )(cond8, w, b.reshape(1, n))


def _norm_mod(x, g, shift, scale):
    ms = jnp.mean(x * x, axis=-1, keepdims=True)
    return (x * lax.rsqrt(ms + EPS) * g) * (1.0 + scale) + shift


def _in_proj_kernel(x_ref, mod_ref, g_ref, w_ref, cs_ref, cos_ref, s1_ref, s2_ref,
                    qr_ref, qp_ref, kr_ref, v_ref, a_ref, h_scr, a_scr):
    m = mod_ref[0]
    h_scr[...] = _norm_mod(x_ref[0], g_ref[...], m[0:1], m[1:2]).astype(BF16)
    cos, s1, s2 = cos_ref[...], s1_ref[...], s2_ref[...]

    def rope(t):
        return (t * cos + pltpu.roll(t, LANES - 16, axis=1) * s1 + pltpu.roll(t, 16, axis=1) * s2)

    f = jnp.dot(h_scr[...], w_ref[:, 0:D_FOURIER], preferred_element_type=F32)
    a = jnp.dot(f.astype(BF16), cs_ref[...], preferred_element_type=F32)
    n_planes = 2 * D_FOURIER // LANES
    tile_rows = a.shape[0] // GRID_W
    for p in range(n_planes):
        for r in range(tile_rows):
            a_scr[p, r * ROW_PITCH:r * ROW_PITCH + GRID_W, :] = a[r * GRID_W:(r + 1) * GRID_W,
                                                                LANES * p:LANES * (p + 1)]
    for w in range(GRID_W):
        for p in range(n_planes):
            slab = a_scr[p, pl.ds(w, tile_rows, stride=ROW_PITCH), :]
            lo = w * D_FOURIER + (p % 2) * LANES
            a_ref[0, p // 2, :, lo:lo + LANES] = slab.astype(BF16)

    scale = HEAD_DIM ** -0.5 * LOG2E
    for c in range(D_NA // 256):
        lo = D_FOURIER + 256 * c
        q = jnp.dot(h_scr[...], w_ref[:, lo:lo + 256], preferred_element_type=F32)
        k = jnp.dot(h_scr[...], w_ref[:, lo + D_NA:lo + D_NA + 256], preferred_element_type=F32)
        v = jnp.dot(h_scr[...], w_ref[:, lo + 2 * D_NA:lo + 2 * D_NA + 256], preferred_element_type=F32)
        v_ref[0, :, 256 * c:256 * c + 256] = v.astype(BF16)
        for s in range(2):
            sl = slice(LANES * s, LANES * (s + 1))
            ol = slice(256 * c + LANES * s, 256 * c + LANES * (s + 1))
            qs, ks = q[:, sl], k[:, sl]
            qp_ref[0, :, ol] = (qs * scale).astype(BF16)
            qr_ref[0, :, ol] = (rope(qs) * scale).astype(BF16)
            kr_ref[0, :, ol] = rope(ks).astype(BF16)


def in_proj(x, mods, g1, w_in_bf16):
    b, n, d = x.shape
    tm = TM_IN
    cos, s1, s2 = _rope_tables()
    cs = _mxu_const(_chan_dft())
    tok = lambda bi, i: (bi, i, 0)
    const2 = lambda bi, i: (0, 0)
    tab = pl.BlockSpec((tm, LANES), lambda bi, i: (i, 0))
    qkv_shape = jax.ShapeDtypeStruct((b, n, D_NA), BF16)
    qkv_spec = pl.BlockSpec((1, tm, D_NA), tok)
    return pl.pallas_call(
        _in_proj_kernel,
        grid=(b, n // tm),
        in_specs=[pl.BlockSpec((1, tm, d), tok),
                  pl.BlockSpec((1, 8, d), lambda bi, i: (bi, 0, 0)),
                  pl.BlockSpec((1, d), const2),
                  pl.BlockSpec((d, D_IN_PROJ), const2),
                  pl.BlockSpec((D_FOURIER, 2 * D_FOURIER), const2),
                  tab, tab, tab],
        out_specs=[qkv_spec, qkv_spec, qkv_spec, qkv_spec,
                   pl.BlockSpec((1, 2, tm // GRID_W, GRID_W * D_FOURIER), lambda bi, i: (bi, 0, i, 0))],
        out_shape=[qkv_shape, qkv_shape, qkv_shape, qkv_shape,
                   jax.ShapeDtypeStruct((b, 2, n // GRID_W, GRID_W * D_FOURIER), BF16)],
        scratch_shapes=[pltpu.VMEM((tm, d), BF16),
                        pltpu.VMEM((2 * D_FOURIER // LANES, (tm // GRID_W) * ROW_PITCH, LANES), F32)],
        compiler_params=_cparams(("arbitrary", "arbitrary")),
        name="in_proj",
    )(x, mods, g1.reshape(1, d), w_in_bf16, cs, jnp.asarray(cos), jnp.asarray(s1), jnp.asarray(s2))


def _ctx_proj_kernel(x_ref, mod_ref, g_ref, w_ref, k_ref, v_ref):
    m = mod_ref[...]
    h = _norm_mod(x_ref[0], g_ref[...], m[0:1], m[1:2]).astype(BF16)
    k_ref[0] = jnp.dot(h, w_ref[:, 0:D_NA], preferred_element_type=F32).astype(BF16)
    v_ref[0] = jnp.dot(h, w_ref[:, D_NA:2 * D_NA], preferred_element_type=F32).astype(BF16)


def ctx_proj(ctx, mod_ctx, g1, w_kv_bf16):
    b, l, d = ctx.shape
    shape = jax.ShapeDtypeStruct((b, l, D_NA), BF16)
    spec = pl.BlockSpec((1, l, D_NA), lambda bi: (bi, 0, 0))
    return pl.pallas_call(
        _ctx_proj_kernel,
        grid=(b,),
        in_specs=[pl.BlockSpec((1, l, d), lambda bi: (bi, 0, 0)),
                  pl.BlockSpec((8, d), lambda bi: (0, 0)),
                  pl.BlockSpec((1, d), lambda bi: (0, 0)),
                  pl.BlockSpec((d, 2 * D_NA), lambda bi: (0, 0))],
        out_specs=[spec, spec],
        out_shape=[shape, shape],
        compiler_params=_cparams(("arbitrary",)),
        name="ctx_proj",
    )(ctx, mod_ctx, g1.reshape(1, d), w_kv_bf16)


N_PLANES = D_FOURIER // LANES
K1_PITCH = GRID_H + 8
K2_PITCH = GRID_W + 8


def _dft_rows_kernel(a_ref, g_ref, z_ref, z_scr):
    for j in range(DFT_TW):
        sl = slice(D_FOURIER * j, D_FOURIER * (j + 1))
        rhs = jnp.concatenate([a_ref[0, 0, :, sl], a_ref[0, 1, :, sl]], axis=0)
        z = jnp.dot(g_ref[j], rhs, preferred_element_type=F32)
        for c in range(2):
            for p in range(N_PLANES):
                z_scr[c * N_PLANES + p, j * K1_PITCH:j * K1_PITCH + GRID_H, :] = (
                    z[c * GRID_H:(c + 1) * GRID_H, LANES * p:LANES * (p + 1)])
    for k1 in range(GRID_H):
        for c in range(2):
            for p in range(N_PLANES):
                slab = z_scr[c * N_PLANES + p, pl.ds(k1, DFT_TW, stride=K1_PITCH), :]
                lo = k1 * D_FOURIER + p * LANES
                z_ref[0, c, :, lo:lo + LANES] = slab.astype(BF16)


def dft_rows(a):
    b = a.shape[0]
    g = _mxu_const(_row_dft())
    return pl.pallas_call(
        _dft_rows_kernel,
        grid=(GRID_W // DFT_TW, b),
        in_specs=[pl.BlockSpec((1, 2, GRID_H, DFT_TW * D_FOURIER), lambda j, bi: (bi, 0, 0, j)),
                  pl.BlockSpec((DFT_TW, 2 * GRID_H, 2 * GRID_H), lambda j, bi: (j, 0, 0))],
        out_specs=pl.BlockSpec((1, 2, DFT_TW, GRID_H * D_FOURIER), lambda j, bi: (bi, 0, j, 0)),
        out_shape=jax.ShapeDtypeStruct((b, 2, GRID_W, GRID_H * D_FOURIER), BF16),
        scratch_shapes=[pltpu.VMEM((2 * N_PLANES, DFT_TW * K1_PITCH, LANES), F32)],
        compiler_params=_cparams(("arbitrary", "arbitrary")),
        name="dft_rows",
    )(a, g)


def _dft_cols_kernel(z_ref, cs_ref, o_ref, o_scr):
    for j in range(DFT_TK1):
        sl = slice(D_FOURIER * j, D_FOURIER * (j + 1))
        rhs = jnp.concatenate([z_ref[0, 0, :, sl], z_ref[0, 1, :, sl]], axis=0)
        out = jnp.dot(cs_ref[...], rhs, preferred_element_type=F32)
        for p in range(N_PLANES):
            o_scr[p, j * K2_PITCH:j * K2_PITCH + GRID_W, :] = out[:, LANES * p:LANES * (p + 1)]
    for k2 in range(GRID_W):
        for p in range(N_PLANES):
            slab = o_scr[p, pl.ds(k2, DFT_TK1, stride=K2_PITCH), :]
            o_ref[0, k2, :, LANES * p:LANES * (p + 1)] = slab.astype(BF16)


def dft_cols(z):
    b = z.shape[0]
    cs = _mxu_const(_col_dft())
    out = pl.pallas_call(
        _dft_cols_kernel,
        grid=(b, GRID_H // DFT_TK1),
        in_specs=[pl.BlockSpec((1, 2, GRID_W, DFT_TK1 * D_FOURIER), lambda bi, j: (bi, 0, 0, j)),
                  pl.BlockSpec((GRID_W, 2 * GRID_W), lambda bi, j: (0, 0))],
        out_specs=pl.BlockSpec((1, GRID_W, DFT_TK1, D_FOURIER), lambda bi, j: (bi, 0, j, 0)),
        out_shape=jax.ShapeDtypeStruct((b, GRID_W, GRID_H, D_FOURIER), BF16),
        scratch_shapes=[pltpu.VMEM((N_PLANES, DFT_TK1 * K2_PITCH, LANES), F32)],
        compiler_params=_cparams(("arbitrary", "arbitrary")),
        name="dft_cols",
    )(z, cs)
    return out.reshape(b, SEQ, D_FOURIER)


ROW_KEYS = NA_ROWS * GRID_W
ROW_TILES = ROW_KEYS // LANES
N_DR = 2 * NA_ROWS - 2
SOFT_ROWS = 32
ATT_PAIRS = ATT_ROWS // 2


def _attention_kernel(qr_ref, qp_ref, k_ref, v_ref, kc_ref, vc_ref, tb_ref, o_ref,
                      sc_scr, pc_scr, s_scr, p_scr):
    rb = pl.program_id(2)
    first1 = lax.broadcasted_iota(jnp.int32, (GRID_W, LANES), 1) < HEAD_DIM
    nt = (((1,), (1,)), ((), ()))

    def window(jp, i):
        r = rb * ATT_ROWS + 2 * jp + i
        rs = jnp.clip(r - NA_ROWS // 2, 0, GRID_H - NA_ROWS)
        return r, rs, pl.multiple_of(rs * GRID_W, GRID_W)

    def split_heads(q1):
        z1 = jnp.zeros_like(q1)
        return jnp.concatenate([jnp.where(first1, q1, z1), jnp.where(first1, z1, q1)], axis=0)

    def scores(jp):
        qp_rows = []
        for i in range(2):
            _, _, koff = window(jp, i)
            qs = slice((2 * jp + i) * GRID_W, (2 * jp + i + 1) * GRID_W)
            kw = k_ref[0, pl.ds(koff, ROW_KEYS), :]
            s_scr[jp, i * 2 * GRID_W:(i + 1) * 2 * GRID_W, :] = lax.dot_general(
                split_heads(qr_ref[0, qs, :]), kw, nt, preferred_element_type=F32)
            qp_rows.append(split_heads(qp_ref[0, qs, :]))
        sc_scr[jp] = lax.dot_general(jnp.concatenate(qp_rows, axis=0), kc_ref[0], nt,
                                     preferred_element_type=F32)

    def softmax(jp):
        for c in range(4):
            i, hh = c // 2, c % 2
            r, rs, _ = window(jp, i)
            d0 = rs - r + NA_ROWS - 1
            for h in range(GRID_W // SOFT_ROWS):
                lo = h * SOFT_ROWS
                rows = slice(c * GRID_W + lo, c * GRID_W + lo + SOFT_ROWS)
                tiles = [s_scr[jp, rows, t * LANES:(t + 1) * LANES]
                         + tb_ref[0, hh, d0 + 2 * t, lo:lo + SOFT_ROWS, :] for t in range(ROW_TILES)]
                sc = sc_scr[jp, rows, :]
                mt = jnp.maximum(sc[:, :LANES], sc[:, LANES:])
                for tl in tiles:
                    mt = jnp.maximum(mt, tl)
                m = jnp.max(mt, axis=1, keepdims=True)
                pc_scr[jp, rows, :] = jnp.exp2(sc - m).astype(BF16)
                for t, tl in enumerate(tiles):
                    p_scr[jp, rows, t * LANES:(t + 1) * LANES] = jnp.exp2(tl - m).astype(BF16)

    def weighted_values(jp):
        vc = jnp.concatenate([vc_ref[0], jnp.ones((CTX_LEN, LANES), BF16)], axis=1)
        oc = jnp.dot(pc_scr[jp], vc, preferred_element_type=F32)
        for i in range(2):
            _, _, koff = window(jp, i)
            rows = slice(i * 2 * GRID_W, (i + 1) * 2 * GRID_W)
            vw = jnp.concatenate([v_ref[0, pl.ds(koff, ROW_KEYS), :], jnp.ones((ROW_KEYS, LANES), BF16)],
                                 axis=1)
            o = jnp.dot(p_scr[jp, rows, :], vw, preferred_element_type=F32) + oc[rows]
            oa, ob = o[:GRID_W], o[GRID_W:]
            out = jnp.where(first1, oa[:, :LANES] / oa[:, LANES:], ob[:, :LANES] / ob[:, LANES:])
            o_ref[0, (2 * jp + i) * GRID_W:(2 * jp + i + 1) * GRID_W, :] = out.astype(BF16)

    scores(0)
    scores(1)
    softmax(0)
    for jp in range(ATT_PAIRS):
        if jp + 2 < ATT_PAIRS:
            scores(jp + 2)
        if jp + 1 < ATT_PAIRS:
            softmax(jp + 1)
        weighted_values(jp)


def bias_tables(rpb):
    dc, valid = _bias_index()
    n_dc = 2 * NA_COLS - 1
    onehot = (dc.reshape(1, -1) == np.arange(n_dc).reshape(-1, 1)).astype(np.float32)
    t = jnp.dot(rpb.reshape(-1, n_dc), jnp.asarray(onehot), precision=lax.Precision.HIGHEST)
    t = jnp.where(valid[None, None], LOG2E * t.reshape(N_HEADS, 2 * NA_ROWS - 1, GRID_W, GRID_W), NEG)
    tb = pl.pallas_call(
        _bias_pairs_kernel,
        grid=(N_HEADS,),
        in_specs=[pl.BlockSpec((1, N_DR + 1, GRID_W, GRID_W), lambda h: (h, 0, 0, 0))],
        out_specs=pl.BlockSpec((1, N_DR, GRID_W, LANES), lambda h: (h, 0, 0, 0)),
        out_shape=jax.ShapeDtypeStruct((N_HEADS, N_DR, GRID_W, LANES), F32),
        compiler_params=_cparams(("arbitrary",)),
        name="bias_pairs",
    )(t)
    return tb.reshape(N_PAIRS, 2, N_DR, GRID_W, LANES)


def _bias_pairs_kernel(t_ref, o_ref):
    for d in range(N_DR):
        o_ref[0, d] = jnp.concatenate([t_ref[0, d], t_ref[0, d + 1]], axis=1)


def attention(qr, qp, kr, v, kc, vc, tb):
    b, n, _ = qr.shape
    tq = ATT_ROWS * GRID_W
    qspec = pl.BlockSpec((1, tq, LANES), lambda bi, hp, i: (bi, i, hp))
    kspec = pl.BlockSpec((1, n, LANES), lambda bi, hp, i: (bi, 0, hp))
    cspec = pl.BlockSpec((1, CTX_LEN, LANES), lambda bi, hp, i: (bi, 0, hp))
    return pl.pallas_call(
        _attention_kernel,
        grid=(b, N_PAIRS, GRID_H // ATT_ROWS),
        in_specs=[qspec, qspec, kspec, kspec, cspec, cspec,
                  pl.BlockSpec((1, 2, N_DR, GRID_W, LANES), lambda bi, hp, i: (hp, 0, 0, 0, 0))],
        out_specs=qspec,
        out_shape=jax.ShapeDtypeStruct((b, n, D_NA), BF16),
        scratch_shapes=[pltpu.VMEM((ATT_PAIRS, 4 * GRID_W, CTX_LEN), F32),
                        pltpu.VMEM((ATT_PAIRS, 4 * GRID_W, CTX_LEN), BF16),
                        pltpu.VMEM((ATT_PAIRS, 4 * GRID_W, ROW_KEYS), F32),
                        pltpu.VMEM((ATT_PAIRS, 4 * GRID_W, ROW_KEYS), BF16)],
        compiler_params=_cparams(("arbitrary", "arbitrary", "arbitrary")),
        name="attention",
    )(qr, qp, kr, v, kc, vc, tb)


def _rms(x, g):
    ms = jnp.mean(x * x, axis=-1, keepdims=True)
    return x * lax.rsqrt(ms + EPS) * g


TILE_ROWS = D_MODEL // LANES


def _store_token_tiles(ref, val, start=0):
    rows = val.shape[0]
    for j in range(TILE_ROWS):
        ref[pl.ds(start + j, rows, stride=TILE_ROWS), :] = val[:, LANES * j:LANES * (j + 1)]


def _load_token_tiles(ref, start, rows):
    return jnp.concatenate(
        [ref[pl.ds(start + j, rows, stride=TILE_ROWS), :] for j in range(TILE_ROWS)], axis=1)


def _out_proj_kernel(fr_ref, na_ref, x_ref, mod_ref, wf_ref, wo_ref, go_ref, g2_ref, wr_ref, tri_ref,
                     x1_ref, h2_ref, meta_ref, metat_ref, cnt_ref, run_scr):
    @pl.when((pl.program_id(0) == 0) & (pl.program_id(1) == 0))
    def _():
        run_scr[...] = jnp.zeros_like(run_scr)

    m = mod_ref[0]
    go = go_ref[...]
    rows_c = OUT_CHUNK
    lane = lax.broadcasted_iota(jnp.int32, (rows_c, LANES), 1).astype(F32)
    ninf = jnp.float32(-jnp.inf)

    def argmax_first(vals):
        mx = jnp.max(vals, axis=1, keepdims=True)
        idx = jnp.min(jnp.where(vals == mx, lane, float(LANES)), axis=1, keepdims=True)
        return mx, idx

    run = run_scr[...]
    for c in range(x_ref.shape[1] // rows_c):
        rows = slice(c * rows_c, (c + 1) * rows_c)
        fo = jnp.dot(fr_ref[0, rows, :], wf_ref[...], preferred_element_type=F32)
        fn = _rms(fo, go[:, :D_FOURIER]).astype(BF16)
        nn = _rms(na_ref[0, rows, :].astype(F32), go[:, D_FOURIER:]).astype(BF16)
        y = (jnp.dot(fn, wo_ref[0:D_FOURIER, :], preferred_element_type=F32)
             + jnp.dot(nn, wo_ref[D_FOURIER:, :], preferred_element_type=F32))
        x1 = x_ref[0, rows, :] + m[2:3] * y
        x1_ref[0, rows, :] = x1
        h2 = _norm_mod(x1, g2_ref[...], m[3:4], m[4:5])
        _store_token_tiles(h2_ref, h2, c * rows_c * TILE_ROWS)
        logits = jnp.dot(h2.astype(BF16), wr_ref[...], preferred_element_type=F32)

        lg = jnp.where(lane < N_GROUPS, logits, ninf)
        gmax, gidx = argmax_first(lg)
        pg = 1.0 / jnp.sum(jnp.exp(lg - gmax), axis=1, keepdims=True)
        lo = N_GROUPS + EXPERTS_PER_GROUP * gidx
        le = jnp.where((lane >= lo) & (lane < lo + EXPERTS_PER_GROUP), logits, ninf)
        e1, i1 = argmax_first(le)
        e2, i2 = argmax_first(jnp.where(lane == i1, ninf, le))
        dd = jnp.exp(e2 - e1)
        gate1 = pg / (1.0 + dd)
        gate2 = pg * dd / (1.0 + dd)

        hot1 = lane == i1
        hot2 = lane == i2
        onehot = jnp.where(hot1 | hot2, 1.0, 0.0)
        cnt = jnp.dot(tri_ref[...], onehot.astype(BF16), preferred_element_type=F32) + run
        rank1 = jnp.sum(jnp.where(hot1, cnt, 0.0), axis=1, keepdims=True)
        rank2 = jnp.sum(jnp.where(hot2, cnt, 0.0), axis=1, keepdims=True)
        run = run + jnp.sum(onehot, axis=0, keepdims=True)

        meta = jnp.where(lane == 0, i1 - N_GROUPS,
               jnp.where(lane == 1, i2 - N_GROUPS,
               jnp.where(lane == 2, rank1,
               jnp.where(lane == 3, rank2,
               jnp.where(lane == 4, gate1,
               jnp.where(lane == 5, gate2, 0.0))))))
        meta_ref[rows, :] = meta
        metat_ref[:, rows] = jnp.transpose(meta)[0:8, :]
    run_scr[...] = run
    cnt_ref[...] = jnp.broadcast_to(run, cnt_ref.shape)


def out_proj(fr, na, x, mods, w_fmix_bf16, w_out_bf16, g_out, g2, w_router_bf16):
    b, n, d = x.shape
    tm = TM_PROJ
    steps = n // tm
    tok = lambda bi, i: (bi, i, 0)
    const2 = lambda bi, i: (0, 0)
    flat = lambda bi, i: (bi * steps + i, 0)
    tri = _mxu_const(_strict_lower(OUT_CHUNK))
    return pl.pallas_call(
        _out_proj_kernel,
        grid=(b, steps),
        in_specs=[pl.BlockSpec((1, tm, D_FOURIER), tok),
                  pl.BlockSpec((1, tm, D_NA), tok),
                  pl.BlockSpec((1, tm, d), tok),
                  pl.BlockSpec((1, 8, d), lambda bi, i: (bi, 0, 0)),
                  pl.BlockSpec((D_FOURIER, D_FOURIER), const2),
                  pl.BlockSpec((d, d), const2),
                  pl.BlockSpec((1, d), const2),
                  pl.BlockSpec((1, d), const2),
                  pl.BlockSpec((d, LANES), const2),
                  pl.BlockSpec((OUT_CHUNK, OUT_CHUNK), const2)],
        out_specs=[pl.BlockSpec((1, tm, d), tok),
                   pl.BlockSpec((tm * TILE_ROWS, LANES), flat),
                   pl.BlockSpec((tm, LANES), flat),
                   pl.BlockSpec((8, tm), lambda bi, i: (0, bi * steps + i)),
                   pl.BlockSpec((8, LANES), const2)],
        out_shape=[jax.ShapeDtypeStruct((b, n, d), F32),
                   jax.ShapeDtypeStruct((b * n * TILE_ROWS, LANES), F32),
                   jax.ShapeDtypeStruct((b * n, LANES), F32),
                   jax.ShapeDtypeStruct((8, b * n), F32),
                   jax.ShapeDtypeStruct((8, LANES), F32)],
        scratch_shapes=[pltpu.VMEM((1, LANES), F32)],
        compiler_params=_cparams(("arbitrary", "arbitrary")),
        name="out_proj",
    )(fr, na, x, mods, w_fmix_bf16, w_out_bf16, g_out.reshape(1, d), g2.reshape(1, d), w_router_bf16, tri)


DMA_CHUNK = 8


TD_DISP = 1024


def _dispatch_kernel(dest_ref, pstart_ref, count_ref, h2_ref, xs_hbm, zero_scr, sem, pad_sem):
    i = pl.program_id(0)
    nt = dest_ref.shape[0] // 2

    @pl.when(i == 0)
    def _():
        zero_scr[...] = jnp.zeros_like(zero_scr)

        def per_expert(e, npad):
            lo = pstart_ref[e] + count_ref[e]
            mid = pstart_ref[e] + ((count_ref[e] + MOE_CHUNK - 1) // MOE_CHUNK) * MOE_CHUNK
            hi = pstart_ref[e] + ((count_ref[e] + MOE_BLK - 1) // MOE_BLK) * MOE_BLK

            def fill(s, carry):
                pltpu.make_async_copy(zero_scr.at[pl.ds(0, TILE_ROWS)],
                                      xs_hbm.at[pl.ds(s * TILE_ROWS, TILE_ROWS)], pad_sem).start()
                return carry

            def fill_chunk(s, carry):
                pltpu.make_async_copy(zero_scr.at[pl.ds(0, MOE_CHUNK * TILE_ROWS)],
                                      xs_hbm.at[pl.ds(mid * TILE_ROWS + s * MOE_CHUNK * TILE_ROWS,
                                                      MOE_CHUNK * TILE_ROWS)], pad_sem).start()
                return carry

            lax.fori_loop(lo, mid, fill, 0)
            lax.fori_loop(0, (hi - mid) // MOE_CHUNK, fill_chunk, 0)
            return npad + (hi - lo)

        npad = lax.fori_loop(0, N_EXPERTS, per_expert, 0)

        blk_rows = MOE_BLK * TILE_ROWS
        first_free = (pstart_ref[N_EXPERTS - 1] + count_ref[N_EXPERTS - 1] + MOE_BLK - 1) // MOE_BLK
        n_blocks = xs_hbm.shape[0] // blk_rows

        def fill_block(bk, carry):
            pltpu.make_async_copy(zero_scr, xs_hbm.at[pl.ds(bk * blk_rows, blk_rows)], pad_sem).start()
            return carry

        lax.fori_loop(first_free, n_blocks, fill_block, 0)
        rows = npad * TILE_ROWS + (n_blocks - first_free) * blk_rows

        @pl.when(rows > 0)
        def _():
            pltpu.make_async_copy(xs_hbm.at[pl.ds(0, rows)], xs_hbm.at[pl.ds(0, rows)], pad_sem).wait()

    def issue(c, carry):
        t0 = c * DMA_CHUNK
        for u in range(DMA_CHUNK):
            for k in range(2):
                d = dest_ref[k * nt + i * TD_DISP + t0 + u]
                pltpu.make_async_copy(h2_ref.at[pl.ds((t0 + u) * TILE_ROWS, TILE_ROWS)],
                                      xs_hbm.at[pl.ds(d * TILE_ROWS, TILE_ROWS)], sem).start(priority=k)
        return carry

    lax.fori_loop(0, TD_DISP // DMA_CHUNK, issue, 0)
    rows = 2 * TD_DISP * TILE_ROWS
    pltpu.make_async_copy(xs_hbm.at[pl.ds(0, rows)], xs_hbm.at[pl.ds(0, rows)], sem).wait()


def dispatch(h2_tiles, dest, pstarts, counts, n_slots):
    nt = dest.shape[0] // 2
    grid_spec = pltpu.PrefetchScalarGridSpec(
        num_scalar_prefetch=3,
        grid=(nt // TD_DISP,),
        in_specs=[pl.BlockSpec((TD_DISP * TILE_ROWS, LANES), lambda i, ds, ps, ct: (i, 0))],
        out_specs=pl.BlockSpec(memory_space=pl.ANY),
        scratch_shapes=[pltpu.VMEM((MOE_BLK * TILE_ROWS, LANES), F32),
                        pltpu.SemaphoreType.DMA(()),
                        pltpu.SemaphoreType.DMA(())],
    )
    return pl.pallas_call(
        _dispatch_kernel,
        grid_spec=grid_spec,
        out_shape=jax.ShapeDtypeStruct((n_slots * TILE_ROWS, LANES), F32),
        compiler_params=_cparams(("arbitrary",)),
        name="dispatch",
    )(dest, pstarts, counts, h2_tiles)


def _experts_kernel(be_ref, nused_ref, valid_ref, xs_ref, wg_ref, wu_ref, wd_ref, ys_ref,
                    wg_scr, wu_scr, wd_scr):
    i = pl.program_id(0)
    valid = valid_ref[i]
    changed = (i == 0) | (be_ref[i] != be_ref[jnp.maximum(i - 1, 0)])
    chunk_rows = MOE_CHUNK * TILE_ROWS

    @pl.when(changed & (valid > 0))
    def _():
        wg_scr[...] = wg_ref[0].astype(BF16)
        wu_scr[...] = wu_ref[0].astype(BF16)
        wd_scr[...] = wd_ref[0].astype(BF16)

    def run(n_chunks):
        hmids = []
        for h in range(n_chunks):
            x = _load_token_tiles(xs_ref, h * chunk_rows, MOE_CHUNK).astype(BF16)
            g = jnp.dot(x, wg_scr[...], preferred_element_type=F32)
            u = jnp.dot(x, wu_scr[...], preferred_element_type=F32)
            hmids.append((g * jax.nn.sigmoid(g) * u).astype(BF16))
        for h, hmid in enumerate(hmids):
            _store_token_tiles(ys_ref, jnp.dot(hmid, wd_scr[...], preferred_element_type=F32), h * chunk_rows)
        if n_chunks * chunk_rows < ys_ref.shape[0]:
            ys_ref[n_chunks * chunk_rows:, :] = jnp.zeros((ys_ref.shape[0] - n_chunks * chunk_rows, LANES), F32)

    for n_chunks in range(MOE_BLK // MOE_CHUNK + 1):
        lo, hi = (n_chunks - 1) * MOE_CHUNK, n_chunks * MOE_CHUNK
        pl.when((valid > lo) & (valid <= hi))(functools.partial(run, n_chunks))


def experts(xs_tiles, block_expert, nused, block_valid, w_gate, w_up, w_down):
    d = D_MODEL
    nblk = block_expert.shape[0]
    blk_rows = MOE_BLK * TILE_ROWS
    wmap = lambda i, be, nu, bv: (be[i], 0, 0)
    grid_spec = pltpu.PrefetchScalarGridSpec(
        num_scalar_prefetch=3,
        grid=(nblk,),
        in_specs=[pl.BlockSpec((blk_rows, LANES), lambda i, be, nu, bv: (jnp.minimum(i, nu[0] - 1), 0)),
                  pl.BlockSpec((1, d, D_EXPERT), wmap),
                  pl.BlockSpec((1, d, D_EXPERT), wmap),
                  pl.BlockSpec((1, D_EXPERT, d), wmap)],
        out_specs=pl.BlockSpec((blk_rows, LANES), lambda i, be, nu, bv: (i, 0)),
        scratch_shapes=[pltpu.VMEM((d, D_EXPERT), BF16),
                        pltpu.VMEM((d, D_EXPERT), BF16),
                        pltpu.VMEM((D_EXPERT, d), BF16)],
    )
    return pl.pallas_call(
        _experts_kernel,
        grid_spec=grid_spec,
        out_shape=jax.ShapeDtypeStruct((nblk * blk_rows, LANES), F32),
        compiler_params=_cparams(("arbitrary",)),
        name="experts",
    )(block_expert, nused, block_valid, xs_tiles, w_gate, w_up, w_down)


def _combine_kernel(dest_ref, ys_hbm, x1_ref, meta_ref, mod_ref, gf_ref, o_ref, ybuf, sem):
    i = pl.program_id(0)
    nstep = pl.num_programs(0)
    tc = TC_COMB
    nt = dest_ref.shape[0] // 2
    half_rows = tc * TILE_ROWS
    buf_rows = 2 * half_rows

    def gather(step, slot):
        def issue(c, carry):
            t0 = c * DMA_CHUNK
            for u in range(DMA_CHUNK):
                for k in range(2):
                    d = dest_ref[k * nt + step * tc + t0 + u]
                    pltpu.make_async_copy(
                        ys_hbm.at[pl.ds(d * TILE_ROWS, TILE_ROWS)],
                        ybuf.at[pl.ds(slot * buf_rows + k * half_rows + (t0 + u) * TILE_ROWS, TILE_ROWS)],
                        sem.at[slot]).start(priority=k)
            return carry

        lax.fori_loop(0, tc // DMA_CHUNK, issue, 0)

    @pl.when(i == 0)
    def _():
        gather(0, 0)

    @pl.when(i + 1 < nstep)
    def _():
        gather(i + 1, (i + 1) % 2)

    slot = i % 2
    start = pl.multiple_of(slot * buf_rows, buf_rows)
    pltpu.make_async_copy(ys_hbm.at[pl.ds(0, buf_rows)], ybuf.at[pl.ds(start, buf_rows)], sem.at[slot]).wait()
    meta = meta_ref[...]
    y0 = _load_token_tiles(ybuf, start, tc)
    y1 = _load_token_tiles(ybuf, start + half_rows, tc)
    moe = y0 * meta[:, 4:5] + y1 * meta[:, 5:6]
    x2 = x1_ref[...] + mod_ref[0][5:6] * moe
    o_ref[...] = _rms(x2, gf_ref[...])


def combine(dest_flat, ys, x1_flat, meta, mods, g_final, n_per_batch):
    nt, d = x1_flat.shape
    tc = TC_COMB
    per_b = n_per_batch // tc
    grid_spec = pltpu.PrefetchScalarGridSpec(
        num_scalar_prefetch=1,
        grid=(nt // tc,),
        in_specs=[pl.BlockSpec(memory_space=pl.ANY),
                  pl.BlockSpec((tc, d), lambda i, ds: (i, 0)),
                  pl.BlockSpec((tc, LANES), lambda i, ds: (i, 0)),
                  pl.BlockSpec((1, 8, d), lambda i, ds: (i // per_b, 0, 0)),
                  pl.BlockSpec((1, d), lambda i, ds: (0, 0))],
        out_specs=pl.BlockSpec((tc, d), lambda i, ds: (i, 0)),
        scratch_shapes=[pltpu.VMEM((2 * 2 * tc * TILE_ROWS, LANES), F32),
                        pltpu.SemaphoreType.DMA((2,))],
    )
    return pl.pallas_call(
        _combine_kernel,
        grid_spec=grid_spec,
        out_shape=jax.ShapeDtypeStruct((nt, d), F32),
        compiler_params=_cparams(("arbitrary",)),
        name="combine",
    )(dest_flat, ys, x1_flat, meta, mods, g_final.reshape(1, d))


def _slots_kernel(pstart_ref, metat_ref, dest_ref):
    eid = metat_ref[0:2, :]
    slot = metat_ref[2:4, :]
    for e in range(N_EXPERTS):
        slot = slot + jnp.where(eid == float(e), pstart_ref[e].astype(F32), 0.0)
    dest_ref[...] = slot.astype(jnp.int32)


def _dispatch_plan(metat, counts_row, nt):
    counts = counts_row[N_GROUPS:N_GROUPS + N_EXPERTS].astype(jnp.int32)
    pcounts = ((counts + MOE_BLK - 1) // MOE_BLK) * MOE_BLK
    pends = jnp.cumsum(pcounts)
    pstarts = pends - pcounts
    dest = pl.pallas_call(
        _slots_kernel,
        grid_spec=pltpu.PrefetchScalarGridSpec(
            num_scalar_prefetch=1, grid=(1,),
            in_specs=[pl.BlockSpec(metat.shape, lambda i, ps: (0, 0))],
            out_specs=pl.BlockSpec((2, nt), lambda i, ps: (0, 0))),
        out_shape=jax.ShapeDtypeStruct((2, nt), jnp.int32),
        compiler_params=_cparams(("arbitrary",)),
        name="slots",
    )(pstarts.astype(jnp.int32), metat).reshape(-1)
    nblk = (nt * 2) // MOE_BLK + N_EXPERTS
    first_slot = jnp.arange(nblk, dtype=jnp.int32) * MOE_BLK
    block_expert = jnp.minimum(
        jnp.sum((pends[None, :] <= first_slot[:, None]).astype(jnp.int32), axis=1), N_EXPERTS - 1)
    nused = (pends[-1] // MOE_BLK).astype(jnp.int32).reshape(1)
    seg_end = jnp.sum(jnp.where(block_expert[:, None] == jnp.arange(N_EXPERTS)[None, :],
                                (pstarts + counts)[None, :], 0), axis=1)
    block_valid = jnp.where(first_slot < pends[-1], jnp.clip(seg_end - first_slot, 0, MOE_BLK), 0)
    return (dest, pstarts.astype(jnp.int32), counts, block_expert.astype(jnp.int32), nused,
            block_valid.astype(jnp.int32))


def kernel(x, c, ctx, c_ctx, w_ada, b_ada, g_norm1, w_in, w_fmix, rpb, g_out, w_out, g_norm2,
           w_router_group, w_router_expert, w_gate, w_up, w_down, g_final):
    b, n, d = x.shape
    assert (b, n, d) == (c.shape[0], SEQ, D_MODEL) and w_ada.shape[0] == 1
    nt = b * n

    cond8 = jnp.zeros((8, d), F32).at[0:b].set(c).at[b].set(c_ctx)
    mod = adaln(cond8, w_ada[0], b_ada[0])
    mods = jnp.pad(mod[0:b].reshape(b, N_MOD, d), ((0, 0), (0, 2), (0, 0)))
    mod_ctx = jnp.pad(mod[b].reshape(N_MOD, d), ((0, 2), (0, 0)))

    w_in_b = w_in[0].astype(BF16)
    qr, qp, kr, v, a = in_proj(x, mods, g_norm1[0], w_in_b)
    kc, vc = ctx_proj(ctx, mod_ctx, g_norm1[0], w_in_b[:, D_FOURIER + D_NA:])

    fr = dft_cols(dft_rows(a))
    na = attention(qr, qp, kr, v, kc, vc, bias_tables(rpb[0]))

    w_router = jnp.concatenate(
        [w_router_group[0], w_router_expert[0],
         jnp.zeros((d, LANES - N_GROUPS - N_EXPERTS), F32)], axis=1).astype(BF16)
    x1, h2_tiles, meta, metat, cnt = out_proj(fr, na, x, mods, w_fmix[0].astype(BF16),
                                              w_out[0].astype(BF16), g_out[0], g_norm2[0], w_router)

    dest, pstarts, counts, block_expert, nused, block_valid = _dispatch_plan(metat, cnt[0], nt)
    xs_tiles = dispatch(h2_tiles, dest, pstarts, counts, block_expert.shape[0] * MOE_BLK)
    ys_tiles = experts(xs_tiles, block_expert, nused, block_valid, w_gate[0], w_up[0], w_down[0])
    out = combine(dest, ys_tiles, x1.reshape(nt, d), meta, mods, g_final, n)
    return out.reshape(b, n, d)
```

```python
import functools
import math

import numpy as np
import jax
import jax.numpy as jnp
from jax import lax
from jax.experimental import pallas as pl
from jax.experimental.pallas import tpu as pltpu

F32 = jnp.float32
BF16 = jnp.bfloat16

D_MODEL = 1024
GRID_W = 64
GRID_H = 128
SEQ = GRID_W * GRID_H
CTX_LEN = 256
D_FOURIER = 256
FOURIER_GROUP = 64
HEAD_DIM = 64
N_HEADS = 12
D_NA = N_HEADS * HEAD_DIM
N_PAIRS = N_HEADS // 2
NA_ROWS = 8
NA_COLS = 16
ROPE_THETA = 10000.0
ROPE_CHUNK = HEAD_DIM // 2
ROPE_HALF = ROPE_CHUNK // 2
N_GROUPS = 4
EXPERTS_PER_GROUP = 8
N_EXPERTS = N_GROUPS * EXPERTS_PER_GROUP
D_EXPERT = 512
N_MOD = 6
D_IN_PROJ = D_FOURIER + 3 * D_NA
EPS = 1e-6
LANES = 128
NEG = -1e30
LOG2E = math.log2(math.e)

ADALN_TN = 1536
TM_IN = 1024
ROW_PITCH = 72
TM_PROJ = 1024
OUT_CHUNK = 512
ATT_ROWS = 64
DFT_TW = 16
DFT_TK1 = 16
MOE_BLK = 512
MOE_CHUNK = 256
TD_DISP = 1024
TC_COMB = 256
DMA_CHUNK = 8
VMEM_LIMIT = 56 * 1024 * 1024


def _cparams(sem):
    return pltpu.CompilerParams(dimension_semantics=sem, vmem_limit_bytes=VMEM_LIMIT)


def _mxu_const(table):
    return jnp.asarray(table, F32).astype(BF16)


@functools.lru_cache(maxsize=None)
def _rope_tables():
    t = np.arange(SEQ)
    row, col = t // GRID_W, t % GRID_W
    lane = np.arange(LANES)
    d = lane % HEAD_DIM
    chunk = d // ROPE_CHUNK
    e = d % ROPE_CHUNK
    j = e % ROPE_HALF
    inv = ROPE_THETA ** (-(j.astype(np.float64)) / ROPE_HALF)
    pos = np.where(chunk[None, :] == 0, row[:, None], col[:, None]).astype(np.float64)
    ang = pos * inv[None, :]
    cos = np.cos(ang)
    sin = np.sin(ang)
    first = (e < ROPE_HALF)[None, :]
    s_first = np.where(first, -sin, 0.0)
    s_second = np.where(first, 0.0, sin)
    return (cos.astype(np.float32), s_first.astype(np.float32), s_second.astype(np.float32))


@functools.lru_cache(maxsize=None)
def _chan_dft():
    c = np.arange(FOURIER_GROUP)
    ang = 2.0 * np.pi * ((c[:, None] * c[None, :]) % FOURIER_GROUP) / FOURIER_GROUP
    eye = np.eye(D_FOURIER // FOURIER_GROUP)
    re = np.kron(eye, np.cos(ang))
    im = np.kron(eye, -np.sin(ang))
    return np.concatenate([re, im], axis=1).astype(np.float32)


@functools.lru_cache(maxsize=None)
def _row_dft():
    k1 = np.arange(GRID_H)[:, None]
    r = np.arange(GRID_H)[None, :]
    out = np.zeros((GRID_W, 2 * GRID_H, 2 * GRID_H), np.float32)
    for w in range(GRID_W):
        m = (k1 * (GRID_W * r + w)) % SEQ
        ang = 2.0 * np.pi * m / SEQ
        c, s = np.cos(ang), np.sin(ang)
        out[w] = np.block([[c, s], [-s, c]])
    return out


@functools.lru_cache(maxsize=None)
def _col_dft():
    k2 = np.arange(GRID_W)
    ang = 2.0 * np.pi * ((k2[:, None] * k2[None, :]) % GRID_W) / GRID_W
    scale = 1.0 / math.sqrt(SEQ * FOURIER_GROUP)
    return (np.concatenate([np.cos(ang), np.sin(ang)], axis=1) * scale).astype(np.float32)


@functools.lru_cache(maxsize=None)
def _bias_index():
    c = np.arange(GRID_W)
    start = np.clip(c - NA_COLS // 2, 0, GRID_W - NA_COLS)
    valid = (c[None, :] >= start[:, None]) & (c[None, :] < start[:, None] + NA_COLS)
    dc = np.clip(c[None, :] - c[:, None] + (NA_COLS - 1), 0, 2 * NA_COLS - 2)
    return dc.astype(np.int32), valid


@functools.lru_cache(maxsize=None)
def _strict_lower(n):
    return np.tril(np.ones((n, n), np.float32), k=-1)


def _adaln_kernel(c_ref, w_ref, b_ref, o_ref):
    c = c_ref[...]
    s = c * jax.nn.sigmoid(c)
    o_ref[...] = jnp.dot(s.astype(BF16), w_ref[...].astype(BF16), preferred_element_type=F32) + b_ref[...]


def adaln(cond8, w, b):
    n = w.shape[1]
    tn = ADALN_TN
    return pl.pallas_call(
        _adaln_kernel,
        grid=(n // tn,),
        in_specs=[pl.BlockSpec((8, D_MODEL), lambda j: (0, 0)),
                  pl.BlockSpec((D_MODEL, tn), lambda j: (0, j)),
                  pl.BlockSpec((1, tn), lambda j: (0, j))],
        out_specs=pl.BlockSpec((8, tn), lambda j: (0, j)),
        out_shape=jax.ShapeDtypeStruct((8, n), F32),
        compiler_params=_cparams(("arbitrary",)),
        name="adaln",
    )(cond8, w, b.reshape(1, n))


def _norm_mod(x, g, shift, scale):
    ms = jnp.mean(x * x, axis=-1, keepdims=True)
    return (x * lax.rsqrt(ms + EPS) * g) * (1.0 + scale) + shift


def _in_proj_kernel(x_ref, mod_ref, g_ref, w_ref, cs_ref, cos_ref, s1_ref, s2_ref,
                    qr_ref, qp_ref, kr_ref, v_ref, a_ref, h_scr, a_scr):
    m = mod_ref[0]
    h_scr[...] = _norm_mod(x_ref[0], g_ref[...], m[0:1], m[1:2]).astype(BF16)
    cos, s1, s2 = cos_ref[...], s1_ref[...], s2_ref[...]

    def rope(t):
        return (t * cos + pltpu.roll(t, LANES - ROPE_HALF, axis=1) * s1
                + pltpu.roll(t, ROPE_HALF, axis=1) * s2)

    f = jnp.dot(h_scr[...], w_ref[:, 0:D_FOURIER], preferred_element_type=F32)
    a = jnp.dot(f.astype(BF16), cs_ref[...], preferred_element_type=F32)
    n_planes = 2 * D_FOURIER // LANES
    tile_rows = a.shape[0] // GRID_W
    for p in range(n_planes):
        for r in range(tile_rows):
            a_scr[p, r * ROW_PITCH:r * ROW_PITCH + GRID_W, :] = a[r * GRID_W:(r + 1) * GRID_W,
                                                                LANES * p:LANES * (p + 1)]
    for w in range(GRID_W):
        for p in range(n_planes):
            slab = a_scr[p, pl.ds(w, tile_rows, stride=ROW_PITCH), :]
            lo = w * D_FOURIER + (p % 2) * LANES
            a_ref[0, p // 2, :, lo:lo + LANES] = slab.astype(BF16)

    scale = HEAD_DIM ** -0.5 * LOG2E
    wide = 2 * LANES
    for c in range(D_NA // wide):
        lo = D_FOURIER + wide * c
        q = jnp.dot(h_scr[...], w_ref[:, lo:lo + wide], preferred_element_type=F32)
        k = jnp.dot(h_scr[...], w_ref[:, lo + D_NA:lo + D_NA + wide], preferred_element_type=F32)
        v = jnp.dot(h_scr[...], w_ref[:, lo + 2 * D_NA:lo + 2 * D_NA + wide], preferred_element_type=F32)
        v_ref[0, :, wide * c:wide * (c + 1)] = v.astype(BF16)
        for s in range(2):
            sl = slice(LANES * s, LANES * (s + 1))
            ol = slice(wide * c + LANES * s, wide * c + LANES * (s + 1))
            qs, ks = q[:, sl], k[:, sl]
            qp_ref[0, :, ol] = (qs * scale).astype(BF16)
            qr_ref[0, :, ol] = (rope(qs) * scale).astype(BF16)
            kr_ref[0, :, ol] = rope(ks).astype(BF16)


def in_proj(x, mods, g1, w_in_bf16):
    b, n, d = x.shape
    tm = TM_IN
    cos, s1, s2 = _rope_tables()
    cs = _mxu_const(_chan_dft())
    tok = lambda bi, i: (bi, i, 0)
    const2 = lambda bi, i: (0, 0)
    tab = pl.BlockSpec((tm, LANES), lambda bi, i: (i, 0))
    qkv_shape = jax.ShapeDtypeStruct((b, n, D_NA), BF16)
    qkv_spec = pl.BlockSpec((1, tm, D_NA), tok)
    return pl.pallas_call(
        _in_proj_kernel,
        grid=(b, n // tm),
        in_specs=[pl.BlockSpec((1, tm, d), tok),
                  pl.BlockSpec((1, 8, d), lambda bi, i: (bi, 0, 0)),
                  pl.BlockSpec((1, d), const2),
                  pl.BlockSpec((d, D_IN_PROJ), const2),
                  pl.BlockSpec((D_FOURIER, 2 * D_FOURIER), const2),
                  tab, tab, tab],
        out_specs=[qkv_spec, qkv_spec, qkv_spec, qkv_spec,
                   pl.BlockSpec((1, 2, tm // GRID_W, GRID_W * D_FOURIER), lambda bi, i: (bi, 0, i, 0))],
        out_shape=[qkv_shape, qkv_shape, qkv_shape, qkv_shape,
                   jax.ShapeDtypeStruct((b, 2, n // GRID_W, GRID_W * D_FOURIER), BF16)],
        scratch_shapes=[pltpu.VMEM((tm, d), BF16),
                        pltpu.VMEM((2 * D_FOURIER // LANES, (tm // GRID_W) * ROW_PITCH, LANES), F32)],
        compiler_params=_cparams(("arbitrary", "arbitrary")),
        name="in_proj",
    )(x, mods, g1.reshape(1, d), w_in_bf16, cs, jnp.asarray(cos), jnp.asarray(s1), jnp.asarray(s2))


def _ctx_proj_kernel(x_ref, mod_ref, g_ref, w_ref, k_ref, v_ref):
    m = mod_ref[...]
    h = _norm_mod(x_ref[0], g_ref[...], m[0:1], m[1:2]).astype(BF16)
    k_ref[0] = jnp.dot(h, w_ref[:, 0:D_NA], preferred_element_type=F32).astype(BF16)
    v_ref[0] = jnp.dot(h, w_ref[:, D_NA:2 * D_NA], preferred_element_type=F32).astype(BF16)


def ctx_proj(ctx, mod_ctx, g1, w_kv_bf16):
    b, l, d = ctx.shape
    shape = jax.ShapeDtypeStruct((b, l, D_NA), BF16)
    spec = pl.BlockSpec((1, l, D_NA), lambda bi: (bi, 0, 0))
    return pl.pallas_call(
        _ctx_proj_kernel,
        grid=(b,),
        in_specs=[pl.BlockSpec((1, l, d), lambda bi: (bi, 0, 0)),
                  pl.BlockSpec((8, d), lambda bi: (0, 0)),
                  pl.BlockSpec((1, d), lambda bi: (0, 0)),
                  pl.BlockSpec((d, 2 * D_NA), lambda bi: (0, 0))],
        out_specs=[spec, spec],
        out_shape=[shape, shape],
        compiler_params=_cparams(("arbitrary",)),
        name="ctx_proj",
    )(ctx, mod_ctx, g1.reshape(1, d), w_kv_bf16)


N_PLANES = D_FOURIER // LANES
K1_PITCH = GRID_H + 8
K2_PITCH = GRID_W + 8


def _dft_rows_kernel(a_ref, g_ref, z_ref, z_scr):
    for j in range(DFT_TW):
        sl = slice(D_FOURIER * j, D_FOURIER * (j + 1))
        rhs = jnp.concatenate([a_ref[0, 0, :, sl], a_ref[0, 1, :, sl]], axis=0)
        z = jnp.dot(g_ref[j], rhs, preferred_element_type=F32)
        for c in range(2):
            for p in range(N_PLANES):
                z_scr[c * N_PLANES + p, j * K1_PITCH:j * K1_PITCH + GRID_H, :] = (
                    z[c * GRID_H:(c + 1) * GRID_H, LANES * p:LANES * (p + 1)])
    for k1 in range(GRID_H):
        for c in range(2):
            for p in range(N_PLANES):
                slab = z_scr[c * N_PLANES + p, pl.ds(k1, DFT_TW, stride=K1_PITCH), :]
                lo = k1 * D_FOURIER + p * LANES
                z_ref[0, c, :, lo:lo + LANES] = slab.astype(BF16)


def dft_rows(a):
    b = a.shape[0]
    g = _mxu_const(_row_dft())
    return pl.pallas_call(
        _dft_rows_kernel,
        grid=(GRID_W // DFT_TW, b),
        in_specs=[pl.BlockSpec((1, 2, GRID_H, DFT_TW * D_FOURIER), lambda j, bi: (bi, 0, 0, j)),
                  pl.BlockSpec((DFT_TW, 2 * GRID_H, 2 * GRID_H), lambda j, bi: (j, 0, 0))],
        out_specs=pl.BlockSpec((1, 2, DFT_TW, GRID_H * D_FOURIER), lambda j, bi: (bi, 0, j, 0)),
        out_shape=jax.ShapeDtypeStruct((b, 2, GRID_W, GRID_H * D_FOURIER), BF16),
        scratch_shapes=[pltpu.VMEM((2 * N_PLANES, DFT_TW * K1_PITCH, LANES), F32)],
        compiler_params=_cparams(("arbitrary", "arbitrary")),
        name="dft_rows",
    )(a, g)


def _dft_cols_kernel(z_ref, cs_ref, o_ref, o_scr):
    for j in range(DFT_TK1):
        sl = slice(D_FOURIER * j, D_FOURIER * (j + 1))
        rhs = jnp.concatenate([z_ref[0, 0, :, sl], z_ref[0, 1, :, sl]], axis=0)
        out = jnp.dot(cs_ref[...], rhs, preferred_element_type=F32)
        for p in range(N_PLANES):
            o_scr[p, j * K2_PITCH:j * K2_PITCH + GRID_W, :] = out[:, LANES * p:LANES * (p + 1)]
    for k2 in range(GRID_W):
        for p in range(N_PLANES):
            slab = o_scr[p, pl.ds(k2, DFT_TK1, stride=K2_PITCH), :]
            o_ref[0, k2, :, LANES * p:LANES * (p + 1)] = slab.astype(BF16)


def dft_cols(z):
    b = z.shape[0]
    cs = _mxu_const(_col_dft())
    out = pl.pallas_call(
        _dft_cols_kernel,
        grid=(b, GRID_H // DFT_TK1),
        in_specs=[pl.BlockSpec((1, 2, GRID_W, DFT_TK1 * D_FOURIER), lambda bi, j: (bi, 0, 0, j)),
                  pl.BlockSpec((GRID_W, 2 * GRID_W), lambda bi, j: (0, 0))],
        out_specs=pl.BlockSpec((1, GRID_W, DFT_TK1, D_FOURIER), lambda bi, j: (bi, 0, j, 0)),
        out_shape=jax.ShapeDtypeStruct((b, GRID_W, GRID_H, D_FOURIER), BF16),
        scratch_shapes=[pltpu.VMEM((N_PLANES, DFT_TK1 * K2_PITCH, LANES), F32)],
        compiler_params=_cparams(("arbitrary", "arbitrary")),
        name="dft_cols",
    )(z, cs)
    return out.reshape(b, SEQ, D_FOURIER)


ROW_KEYS = NA_ROWS * GRID_W
ROW_TILES = ROW_KEYS // LANES
N_DR = 2 * NA_ROWS - 2
SOFT_ROWS = 32
ATT_PAIRS = ATT_ROWS // 2


def _attention_kernel(qr_ref, qp_ref, k_ref, v_ref, kc_ref, vc_ref, tb_ref, o_ref,
                      sc_scr, pc_scr, s_scr, p_scr):
    rb = pl.program_id(2)
    first1 = lax.broadcasted_iota(jnp.int32, (GRID_W, LANES), 1) < HEAD_DIM
    nt = (((1,), (1,)), ((), ()))

    def window(jp, i):
        r = rb * ATT_ROWS + 2 * jp + i
        rs = jnp.clip(r - NA_ROWS // 2, 0, GRID_H - NA_ROWS)
        return r, rs, pl.multiple_of(rs * GRID_W, GRID_W)

    def split_heads(q1):
        z1 = jnp.zeros_like(q1)
        return jnp.concatenate([jnp.where(first1, q1, z1), jnp.where(first1, z1, q1)], axis=0)

    def scores(jp):
        qp_rows = []
        for i in range(2):
            _, _, koff = window(jp, i)
            qs = slice((2 * jp + i) * GRID_W, (2 * jp + i + 1) * GRID_W)
            kw = k_ref[0, pl.ds(koff, ROW_KEYS), :]
            s_scr[jp, i * 2 * GRID_W:(i + 1) * 2 * GRID_W, :] = lax.dot_general(
                split_heads(qr_ref[0, qs, :]), kw, nt, preferred_element_type=F32)
            qp_rows.append(split_heads(qp_ref[0, qs, :]))
        sc_scr[jp] = lax.dot_general(jnp.concatenate(qp_rows, axis=0), kc_ref[0], nt,
                                     preferred_element_type=F32)

    def softmax(jp):
        for c in range(4):
            i, hh = c // 2, c % 2
            r, rs, _ = window(jp, i)
            d0 = rs - r + NA_ROWS - 1
            for h in range(GRID_W // SOFT_ROWS):
                lo = h * SOFT_ROWS
                rows = slice(c * GRID_W + lo, c * GRID_W + lo + SOFT_ROWS)
                tiles = [s_scr[jp, rows, t * LANES:(t + 1) * LANES]
                         + tb_ref[0, hh, d0 + 2 * t, lo:lo + SOFT_ROWS, :] for t in range(ROW_TILES)]
                sc = sc_scr[jp, rows, :]
                mt = jnp.maximum(sc[:, :LANES], sc[:, LANES:])
                for tl in tiles:
                    mt = jnp.maximum(mt, tl)
                m = jnp.max(mt, axis=1, keepdims=True)
                pc_scr[jp, rows, :] = jnp.exp2(sc - m).astype(BF16)
                for t, tl in enumerate(tiles):
                    p_scr[jp, rows, t * LANES:(t + 1) * LANES] = jnp.exp2(tl - m).astype(BF16)

    def weighted_values(jp):
        vc = jnp.concatenate([vc_ref[0], jnp.ones((CTX_LEN, LANES), BF16)], axis=1)
        oc = jnp.dot(pc_scr[jp], vc, preferred_element_type=F32)
        for i in range(2):
            _, _, koff = window(jp, i)
            rows = slice(i * 2 * GRID_W, (i + 1) * 2 * GRID_W)
            vw = jnp.concatenate([v_ref[0, pl.ds(koff, ROW_KEYS), :], jnp.ones((ROW_KEYS, LANES), BF16)],
                                 axis=1)
            o = jnp.dot(p_scr[jp, rows, :], vw, preferred_element_type=F32) + oc[rows]
            oa, ob = o[:GRID_W], o[GRID_W:]
            out = jnp.where(first1, oa[:, :LANES] / oa[:, LANES:], ob[:, :LANES] / ob[:, LANES:])
            o_ref[0, (2 * jp + i) * GRID_W:(2 * jp + i + 1) * GRID_W, :] = out.astype(BF16)

    scores(0)
    scores(1)
    softmax(0)
    for jp in range(ATT_PAIRS):
        if jp + 2 < ATT_PAIRS:
            scores(jp + 2)
        if jp + 1 < ATT_PAIRS:
            softmax(jp + 1)
        weighted_values(jp)


def bias_tables(rpb):
    dc, valid = _bias_index()
    n_dc = 2 * NA_COLS - 1
    onehot = (dc.reshape(1, -1) == np.arange(n_dc).reshape(-1, 1)).astype(np.float32)
    t = jnp.dot(rpb.reshape(-1, n_dc), jnp.asarray(onehot), precision=lax.Precision.HIGHEST)
    t = jnp.where(valid[None, None], LOG2E * t.reshape(N_HEADS, 2 * NA_ROWS - 1, GRID_W, GRID_W), NEG)
    tb = pl.pallas_call(
        _bias_pairs_kernel,
        grid=(N_HEADS,),
        in_specs=[pl.BlockSpec((1, N_DR + 1, GRID_W, GRID_W), lambda h: (h, 0, 0, 0))],
        out_specs=pl.BlockSpec((1, N_DR, GRID_W, LANES), lambda h: (h, 0, 0, 0)),
        out_shape=jax.ShapeDtypeStruct((N_HEADS, N_DR, GRID_W, LANES), F32),
        compiler_params=_cparams(("arbitrary",)),
        name="bias_pairs",
    )(t)
    return tb.reshape(N_PAIRS, 2, N_DR, GRID_W, LANES)


def _bias_pairs_kernel(t_ref, o_ref):
    for d in range(N_DR):
        o_ref[0, d] = jnp.concatenate([t_ref[0, d], t_ref[0, d + 1]], axis=1)


def attention(qr, qp, kr, v, kc, vc, tb):
    b, n, _ = qr.shape
    tq = ATT_ROWS * GRID_W
    qspec = pl.BlockSpec((1, tq, LANES), lambda bi, hp, i: (bi, i, hp))
    kspec = pl.BlockSpec((1, n, LANES), lambda bi, hp, i: (bi, 0, hp))
    cspec = pl.BlockSpec((1, CTX_LEN, LANES), lambda bi, hp, i: (bi, 0, hp))
    return pl.pallas_call(
        _attention_kernel,
        grid=(b, N_PAIRS, GRID_H // ATT_ROWS),
        in_specs=[qspec, qspec, kspec, kspec, cspec, cspec,
                  pl.BlockSpec((1, 2, N_DR, GRID_W, LANES), lambda bi, hp, i: (hp, 0, 0, 0, 0))],
        out_specs=qspec,
        out_shape=jax.ShapeDtypeStruct((b, n, D_NA), BF16),
        scratch_shapes=[pltpu.VMEM((ATT_PAIRS, 4 * GRID_W, CTX_LEN), F32),
                        pltpu.VMEM((ATT_PAIRS, 4 * GRID_W, CTX_LEN), BF16),
                        pltpu.VMEM((ATT_PAIRS, 4 * GRID_W, ROW_KEYS), F32),
                        pltpu.VMEM((ATT_PAIRS, 4 * GRID_W, ROW_KEYS), BF16)],
        compiler_params=_cparams(("arbitrary", "arbitrary", "arbitrary")),
        name="attention",
    )(qr, qp, kr, v, kc, vc, tb)


def _rms(x, g):
    ms = jnp.mean(x * x, axis=-1, keepdims=True)
    return x * lax.rsqrt(ms + EPS) * g


TILE_ROWS = D_MODEL // LANES


def _store_token_tiles(ref, val, start=0):
    rows = val.shape[0]
    for j in range(TILE_ROWS):
        ref[pl.ds(start + j, rows, stride=TILE_ROWS), :] = val[:, LANES * j:LANES * (j + 1)]


def _load_token_tiles(ref, start, rows):
    return jnp.concatenate(
        [ref[pl.ds(start + j, rows, stride=TILE_ROWS), :] for j in range(TILE_ROWS)], axis=1)


def _out_proj_kernel(fr_ref, na_ref, x_ref, mod_ref, wf_ref, wo_ref, go_ref, g2_ref, wr_ref, tri_ref,
                     x1_ref, h2_ref, meta_ref, metat_ref, cnt_ref, run_scr):
    @pl.when((pl.program_id(0) == 0) & (pl.program_id(1) == 0))
    def _():
        run_scr[...] = jnp.zeros_like(run_scr)

    m = mod_ref[0]
    go = go_ref[...]
    rows_c = OUT_CHUNK
    lane = lax.broadcasted_iota(jnp.int32, (rows_c, LANES), 1).astype(F32)
    ninf = jnp.float32(-jnp.inf)

    def argmax_first(vals):
        mx = jnp.max(vals, axis=1, keepdims=True)
        idx = jnp.min(jnp.where(vals == mx, lane, float(LANES)), axis=1, keepdims=True)
        return mx, idx

    run = run_scr[...]
    for c in range(x_ref.shape[1] // rows_c):
        rows = slice(c * rows_c, (c + 1) * rows_c)
        fo = jnp.dot(fr_ref[0, rows, :], wf_ref[...], preferred_element_type=F32)
        fn = _rms(fo, go[:, :D_FOURIER]).astype(BF16)
        nn = _rms(na_ref[0, rows, :].astype(F32), go[:, D_FOURIER:]).astype(BF16)
        y = (jnp.dot(fn, wo_ref[0:D_FOURIER, :], preferred_element_type=F32)
             + jnp.dot(nn, wo_ref[D_FOURIER:, :], preferred_element_type=F32))
        x1 = x_ref[0, rows, :] + m[2:3] * y
        x1_ref[0, rows, :] = x1
        h2 = _norm_mod(x1, g2_ref[...], m[3:4], m[4:5])
        _store_token_tiles(h2_ref, h2, c * rows_c * TILE_ROWS)
        logits = jnp.dot(h2.astype(BF16), wr_ref[...], preferred_element_type=F32)

        lg = jnp.where(lane < N_GROUPS, logits, ninf)
        gmax, gidx = argmax_first(lg)
        pg = 1.0 / jnp.sum(jnp.exp(lg - gmax), axis=1, keepdims=True)
        lo = N_GROUPS + EXPERTS_PER_GROUP * gidx
        le = jnp.where((lane >= lo) & (lane < lo + EXPERTS_PER_GROUP), logits, ninf)
        e1, i1 = argmax_first(le)
        e2, i2 = argmax_first(jnp.where(lane == i1, ninf, le))
        dd = jnp.exp(e2 - e1)
        gate1 = pg / (1.0 + dd)
        gate2 = pg * dd / (1.0 + dd)

        hot1 = lane == i1
        hot2 = lane == i2
        onehot = jnp.where(hot1 | hot2, 1.0, 0.0)
        cnt = jnp.dot(tri_ref[...], onehot.astype(BF16), preferred_element_type=F32) + run
        rank1 = jnp.sum(jnp.where(hot1, cnt, 0.0), axis=1, keepdims=True)
        rank2 = jnp.sum(jnp.where(hot2, cnt, 0.0), axis=1, keepdims=True)
        run = run + jnp.sum(onehot, axis=0, keepdims=True)

        meta = jnp.where(lane == 0, i1 - N_GROUPS,
               jnp.where(lane == 1, i2 - N_GROUPS,
               jnp.where(lane == 2, rank1,
               jnp.where(lane == 3, rank2,
               jnp.where(lane == 4, gate1,
               jnp.where(lane == 5, gate2, 0.0))))))
        meta_ref[rows, :] = meta
        metat_ref[:, rows] = jnp.transpose(meta)[0:8, :]
    run_scr[...] = run
    cnt_ref[...] = jnp.broadcast_to(run, cnt_ref.shape)


def out_proj(fr, na, x, mods, w_fmix_bf16, w_out_bf16, g_out, g2, w_router_bf16):
    b, n, d = x.shape
    tm = TM_PROJ
    steps = n // tm
    tok = lambda bi, i: (bi, i, 0)
    const2 = lambda bi, i: (0, 0)
    flat = lambda bi, i: (bi * steps + i, 0)
    tri = _mxu_const(_strict_lower(OUT_CHUNK))
    return pl.pallas_call(
        _out_proj_kernel,
        grid=(b, steps),
        in_specs=[pl.BlockSpec((1, tm, D_FOURIER), tok),
                  pl.BlockSpec((1, tm, D_NA), tok),
                  pl.BlockSpec((1, tm, d), tok),
                  pl.BlockSpec((1, 8, d), lambda bi, i: (bi, 0, 0)),
                  pl.BlockSpec((D_FOURIER, D_FOURIER), const2),
                  pl.BlockSpec((d, d), const2),
                  pl.BlockSpec((1, d), const2),
                  pl.BlockSpec((1, d), const2),
                  pl.BlockSpec((d, LANES), const2),
                  pl.BlockSpec((OUT_CHUNK, OUT_CHUNK), const2)],
        out_specs=[pl.BlockSpec((1, tm, d), tok),
                   pl.BlockSpec((tm * TILE_ROWS, LANES), flat),
                   pl.BlockSpec((tm, LANES), flat),
                   pl.BlockSpec((8, tm), lambda bi, i: (0, bi * steps + i)),
                   pl.BlockSpec((8, LANES), const2)],
        out_shape=[jax.ShapeDtypeStruct((b, n, d), F32),
                   jax.ShapeDtypeStruct((b * n * TILE_ROWS, LANES), F32),
                   jax.ShapeDtypeStruct((b * n, LANES), F32),
                   jax.ShapeDtypeStruct((8, b * n), F32),
                   jax.ShapeDtypeStruct((8, LANES), F32)],
        scratch_shapes=[pltpu.VMEM((1, LANES), F32)],
        compiler_params=_cparams(("arbitrary", "arbitrary")),
        name="out_proj",
    )(fr, na, x, mods, w_fmix_bf16, w_out_bf16, g_out.reshape(1, d), g2.reshape(1, d), w_router_bf16, tri)


def _dispatch_kernel(dest_ref, pstart_ref, count_ref, h2_ref, xs_hbm, zero_scr, sem, pad_sem):
    i = pl.program_id(0)
    nt = dest_ref.shape[0] // 2

    @pl.when(i == 0)
    def _():
        zero_scr[...] = jnp.zeros_like(zero_scr)

        def per_expert(e, npad):
            lo = pstart_ref[e] + count_ref[e]
            mid = pstart_ref[e] + ((count_ref[e] + MOE_CHUNK - 1) // MOE_CHUNK) * MOE_CHUNK
            hi = pstart_ref[e] + ((count_ref[e] + MOE_BLK - 1) // MOE_BLK) * MOE_BLK

            def fill(s, carry):
                pltpu.make_async_copy(zero_scr.at[pl.ds(0, TILE_ROWS)],
                                      xs_hbm.at[pl.ds(s * TILE_ROWS, TILE_ROWS)], pad_sem).start()
                return carry

            def fill_chunk(s, carry):
                pltpu.make_async_copy(zero_scr.at[pl.ds(0, MOE_CHUNK * TILE_ROWS)],
                                      xs_hbm.at[pl.ds(mid * TILE_ROWS + s * MOE_CHUNK * TILE_ROWS,
                                                      MOE_CHUNK * TILE_ROWS)], pad_sem).start()
                return carry

            lax.fori_loop(lo, mid, fill, 0)
            lax.fori_loop(0, (hi - mid) // MOE_CHUNK, fill_chunk, 0)
            return npad + (hi - lo)

        npad = lax.fori_loop(0, N_EXPERTS, per_expert, 0)

        blk_rows = MOE_BLK * TILE_ROWS
        first_free = (pstart_ref[N_EXPERTS - 1] + count_ref[N_EXPERTS - 1] + MOE_BLK - 1) // MOE_BLK
        n_blocks = xs_hbm.shape[0] // blk_rows

        def fill_block(bk, carry):
            pltpu.make_async_copy(zero_scr, xs_hbm.at[pl.ds(bk * blk_rows, blk_rows)], pad_sem).start()
            return carry

        lax.fori_loop(first_free, n_blocks, fill_block, 0)
        rows = npad * TILE_ROWS + (n_blocks - first_free) * blk_rows

        @pl.when(rows > 0)
        def _():
            pltpu.make_async_copy(xs_hbm.at[pl.ds(0, rows)], xs_hbm.at[pl.ds(0, rows)], pad_sem).wait()

    def issue(c, carry):
        t0 = c * DMA_CHUNK
        for u in range(DMA_CHUNK):
            for k in range(2):
                d = dest_ref[k * nt + i * TD_DISP + t0 + u]
                pltpu.make_async_copy(h2_ref.at[pl.ds((t0 + u) * TILE_ROWS, TILE_ROWS)],
                                      xs_hbm.at[pl.ds(d * TILE_ROWS, TILE_ROWS)], sem).start(priority=k)
        return carry

    lax.fori_loop(0, TD_DISP // DMA_CHUNK, issue, 0)
    rows = 2 * TD_DISP * TILE_ROWS
    pltpu.make_async_copy(xs_hbm.at[pl.ds(0, rows)], xs_hbm.at[pl.ds(0, rows)], sem).wait()


def dispatch(h2_tiles, dest, pstarts, counts, n_slots):
    nt = dest.shape[0] // 2
    grid_spec = pltpu.PrefetchScalarGridSpec(
        num_scalar_prefetch=3,
        grid=(nt // TD_DISP,),
        in_specs=[pl.BlockSpec((TD_DISP * TILE_ROWS, LANES), lambda i, ds, ps, ct: (i, 0))],
        out_specs=pl.BlockSpec(memory_space=pl.ANY),
        scratch_shapes=[pltpu.VMEM((MOE_BLK * TILE_ROWS, LANES), F32),
                        pltpu.SemaphoreType.DMA(()),
                        pltpu.SemaphoreType.DMA(())],
    )
    return pl.pallas_call(
        _dispatch_kernel,
        grid_spec=grid_spec,
        out_shape=jax.ShapeDtypeStruct((n_slots * TILE_ROWS, LANES), F32),
        compiler_params=_cparams(("arbitrary",)),
        name="dispatch",
    )(dest, pstarts, counts, h2_tiles)


def _experts_kernel(be_ref, nused_ref, valid_ref, xs_ref, wg_ref, wu_ref, wd_ref, ys_ref,
                    wg_scr, wu_scr, wd_scr):
    i = pl.program_id(0)
    valid = valid_ref[i]
    changed = (i == 0) | (be_ref[i] != be_ref[jnp.maximum(i - 1, 0)])
    chunk_rows = MOE_CHUNK * TILE_ROWS

    @pl.when(changed & (valid > 0))
    def _():
        wg_scr[...] = wg_ref[0].astype(BF16)
        wu_scr[...] = wu_ref[0].astype(BF16)
        wd_scr[...] = wd_ref[0].astype(BF16)

    def run(n_chunks):
        hmids = []
        for h in range(n_chunks):
            x = _load_token_tiles(xs_ref, h * chunk_rows, MOE_CHUNK).astype(BF16)
            g = jnp.dot(x, wg_scr[...], preferred_element_type=F32)
            u = jnp.dot(x, wu_scr[...], preferred_element_type=F32)
            hmids.append((g * jax.nn.sigmoid(g) * u).astype(BF16))
        for h, hmid in enumerate(hmids):
            _store_token_tiles(ys_ref, jnp.dot(hmid, wd_scr[...], preferred_element_type=F32), h * chunk_rows)
        if n_chunks * chunk_rows < ys_ref.shape[0]:
            ys_ref[n_chunks * chunk_rows:, :] = jnp.zeros((ys_ref.shape[0] - n_chunks * chunk_rows, LANES), F32)

    for n_chunks in range(MOE_BLK // MOE_CHUNK + 1):
        lo, hi = (n_chunks - 1) * MOE_CHUNK, n_chunks * MOE_CHUNK
        pl.when((valid > lo) & (valid <= hi))(functools.partial(run, n_chunks))


def experts(xs_tiles, block_expert, nused, block_valid, w_gate, w_up, w_down):
    d = D_MODEL
    nblk = block_expert.shape[0]
    blk_rows = MOE_BLK * TILE_ROWS
    wmap = lambda i, be, nu, bv: (be[i], 0, 0)
    grid_spec = pltpu.PrefetchScalarGridSpec(
        num_scalar_prefetch=3,
        grid=(nblk,),
        in_specs=[pl.BlockSpec((blk_rows, LANES), lambda i, be, nu, bv: (jnp.minimum(i, nu[0] - 1), 0)),
                  pl.BlockSpec((1, d, D_EXPERT), wmap),
                  pl.BlockSpec((1, d, D_EXPERT), wmap),
                  pl.BlockSpec((1, D_EXPERT, d), wmap)],
        out_specs=pl.BlockSpec((blk_rows, LANES), lambda i, be, nu, bv: (i, 0)),
        scratch_shapes=[pltpu.VMEM((d, D_EXPERT), BF16),
                        pltpu.VMEM((d, D_EXPERT), BF16),
                        pltpu.VMEM((D_EXPERT, d), BF16)],
    )
    return pl.pallas_call(
        _experts_kernel,
        grid_spec=grid_spec,
        out_shape=jax.ShapeDtypeStruct((nblk * blk_rows, LANES), F32),
        compiler_params=_cparams(("arbitrary",)),
        name="experts",
    )(block_expert, nused, block_valid, xs_tiles, w_gate, w_up, w_down)


def _combine_kernel(dest_ref, ys_hbm, x1_ref, meta_ref, mod_ref, gf_ref, o_ref, ybuf, sem):
    i = pl.program_id(0)
    nstep = pl.num_programs(0)
    tc = TC_COMB
    nt = dest_ref.shape[0] // 2
    half_rows = tc * TILE_ROWS
    buf_rows = 2 * half_rows

    def gather(step, slot):
        def issue(c, carry):
            t0 = c * DMA_CHUNK
            for u in range(DMA_CHUNK):
                for k in range(2):
                    d = dest_ref[k * nt + step * tc + t0 + u]
                    pltpu.make_async_copy(
                        ys_hbm.at[pl.ds(d * TILE_ROWS, TILE_ROWS)],
                        ybuf.at[pl.ds(slot * buf_rows + k * half_rows + (t0 + u) * TILE_ROWS, TILE_ROWS)],
                        sem.at[slot]).start(priority=k)
            return carry

        lax.fori_loop(0, tc // DMA_CHUNK, issue, 0)

    @pl.when(i == 0)
    def _():
        gather(0, 0)

    @pl.when(i + 1 < nstep)
    def _():
        gather(i + 1, (i + 1) % 2)

    slot = i % 2
    start = pl.multiple_of(slot * buf_rows, buf_rows)
    pltpu.make_async_copy(ys_hbm.at[pl.ds(0, buf_rows)], ybuf.at[pl.ds(start, buf_rows)], sem.at[slot]).wait()
    meta = meta_ref[...]
    y0 = _load_token_tiles(ybuf, start, tc)
    y1 = _load_token_tiles(ybuf, start + half_rows, tc)
    moe = y0 * meta[:, 4:5] + y1 * meta[:, 5:6]
    x2 = x1_ref[...] + mod_ref[0][5:6] * moe
    o_ref[...] = _rms(x2, gf_ref[...])


def combine(dest_flat, ys, x1_flat, meta, mods, g_final, n_per_batch):
    nt, d = x1_flat.shape
    tc = TC_COMB
    per_b = n_per_batch // tc
    grid_spec = pltpu.PrefetchScalarGridSpec(
        num_scalar_prefetch=1,
        grid=(nt // tc,),
        in_specs=[pl.BlockSpec(memory_space=pl.ANY),
                  pl.BlockSpec((tc, d), lambda i, ds: (i, 0)),
                  pl.BlockSpec((tc, LANES), lambda i, ds: (i, 0)),
                  pl.BlockSpec((1, 8, d), lambda i, ds: (i // per_b, 0, 0)),
                  pl.BlockSpec((1, d), lambda i, ds: (0, 0))],
        out_specs=pl.BlockSpec((tc, d), lambda i, ds: (i, 0)),
        scratch_shapes=[pltpu.VMEM((2 * 2 * tc * TILE_ROWS, LANES), F32),
                        pltpu.SemaphoreType.DMA((2,))],
    )
    return pl.pallas_call(
        _combine_kernel,
        grid_spec=grid_spec,
        out_shape=jax.ShapeDtypeStruct((nt, d), F32),
        compiler_params=_cparams(("arbitrary",)),
        name="combine",
    )(dest_flat, ys, x1_flat, meta, mods, g_final.reshape(1, d))


def _slots_kernel(pstart_ref, metat_ref, dest_ref):
    eid = metat_ref[0:2, :]
    slot = metat_ref[2:4, :]
    for e in range(N_EXPERTS):
        slot = slot + jnp.where(eid == float(e), pstart_ref[e].astype(F32), 0.0)
    dest_ref[...] = slot.astype(jnp.int32)


def _dispatch_plan(metat, counts_row, nt):
    counts = counts_row[N_GROUPS:N_GROUPS + N_EXPERTS].astype(jnp.int32)
    pcounts = ((counts + MOE_BLK - 1) // MOE_BLK) * MOE_BLK
    pends = jnp.cumsum(pcounts)
    pstarts = pends - pcounts
    dest = pl.pallas_call(
        _slots_kernel,
        grid_spec=pltpu.PrefetchScalarGridSpec(
            num_scalar_prefetch=1, grid=(1,),
            in_specs=[pl.BlockSpec(metat.shape, lambda i, ps: (0, 0))],
            out_specs=pl.BlockSpec((2, nt), lambda i, ps: (0, 0))),
        out_shape=jax.ShapeDtypeStruct((2, nt), jnp.int32),
        compiler_params=_cparams(("arbitrary",)),
        name="slots",
    )(pstarts.astype(jnp.int32), metat).reshape(-1)
    nblk = (nt * 2) // MOE_BLK + N_EXPERTS
    first_slot = jnp.arange(nblk, dtype=jnp.int32) * MOE_BLK
    block_expert = jnp.minimum(
        jnp.sum((pends[None, :] <= first_slot[:, None]).astype(jnp.int32), axis=1), N_EXPERTS - 1)
    nused = (pends[-1] // MOE_BLK).astype(jnp.int32).reshape(1)
    seg_end = jnp.sum(jnp.where(block_expert[:, None] == jnp.arange(N_EXPERTS)[None, :],
                                (pstarts + counts)[None, :], 0), axis=1)
    block_valid = jnp.where(first_slot < pends[-1], jnp.clip(seg_end - first_slot, 0, MOE_BLK), 0)
    return (dest, pstarts.astype(jnp.int32), counts, block_expert.astype(jnp.int32), nused,
            block_valid.astype(jnp.int32))


def kernel(x, c, ctx, c_ctx, w_ada, b_ada, g_norm1, w_in, w_fmix, rpb, g_out, w_out, g_norm2,
           w_router_group, w_router_expert, w_gate, w_up, w_down, g_final):
    b, n, d = x.shape
    assert (b, n, d) == (c.shape[0], SEQ, D_MODEL) and w_ada.shape[0] == 1
    nt = b * n

    cond8 = jnp.zeros((8, d), F32).at[0:b].set(c).at[b].set(c_ctx)
    mod = adaln(cond8, w_ada[0], b_ada[0])
    mods = jnp.pad(mod[0:b].reshape(b, N_MOD, d), ((0, 0), (0, 2), (0, 0)))
    mod_ctx = jnp.pad(mod[b].reshape(N_MOD, d), ((0, 2), (0, 0)))

    w_in_b = w_in[0].astype(BF16)
    qr, qp, kr, v, a = in_proj(x, mods, g_norm1[0], w_in_b)
    kc, vc = ctx_proj(ctx, mod_ctx, g_norm1[0], w_in_b[:, D_FOURIER + D_NA:])

    fr = dft_cols(dft_rows(a))
    na = attention(qr, qp, kr, v, kc, vc, bias_tables(rpb[0]))

    w_router = jnp.concatenate(
        [w_router_group[0], w_router_expert[0],
         jnp.zeros((d, LANES - N_GROUPS - N_EXPERTS), F32)], axis=1).astype(BF16)
    x1, h2_tiles, meta, metat, cnt = out_proj(fr, na, x, mods, w_fmix[0].astype(BF16),
                                              w_out[0].astype(BF16), g_out[0], g_norm2[0], w_router)

    dest, pstarts, counts, block_expert, nused, block_valid = _dispatch_plan(metat, cnt[0], nt)
    xs_tiles = dispatch(h2_tiles, dest, pstarts, counts, block_expert.shape[0] * MOE_BLK)
    ys_tiles = experts(xs_tiles, block_expert, nused, block_valid, w_gate[0], w_up[0], w_down[0])
    out = combine(dest, ys_tiles, x1.reshape(nt, d), meta, mods, g_final, n)
    return out.reshape(b, n, d)
```

```python
import functools
import math

import numpy as np
import jax
import jax.numpy as jnp
from jax import lax
from jax.experimental import pallas as pl
from jax.experimental.pallas import tpu as pltpu

F32 = jnp.float32
BF16 = jnp.bfloat16

D_MODEL = 1024
GRID_W = 64
GRID_H = 128
SEQ = GRID_W * GRID_H
CTX_LEN = 256
D_FOURIER = 256
FOURIER_GROUP = 64
HEAD_DIM = 64
N_HEADS = 12
D_NA = N_HEADS * HEAD_DIM
N_PAIRS = N_HEADS // 2
NA_ROWS = 8
NA_COLS = 16
ROPE_THETA = 10000.0
ROPE_CHUNK = HEAD_DIM // 2
ROPE_HALF = ROPE_CHUNK // 2
N_GROUPS = 4
EXPERTS_PER_GROUP = 8
N_EXPERTS = N_GROUPS * EXPERTS_PER_GROUP
D_EXPERT = 512
N_MOD = 6
D_IN_PROJ = D_FOURIER + 3 * D_NA
EPS = 1e-6
LANES = 128
NEG = -1e30
LOG2E = math.log2(math.e)

ADALN_TN = 1536
TM_IN = 1024
ROW_PITCH = 72
TM_PROJ = 1024
OUT_CHUNK = 512
ATT_ROWS = 64
DFT_TW = 16
DFT_TK1 = 16
MOE_BLK = 512
MOE_CHUNK = 256
TD_DISP = 2048
TC_COMB = 512
DMA_CHUNK = 8
VMEM_LIMIT = 56 * 1024 * 1024


def _cparams(sem):
    return pltpu.CompilerParams(dimension_semantics=sem, vmem_limit_bytes=VMEM_LIMIT)


def _mxu_const(table):
    return jnp.asarray(table, F32).astype(BF16)


@functools.lru_cache(maxsize=None)
def _rope_tables():
    t = np.arange(SEQ)
    row, col = t // GRID_W, t % GRID_W
    lane = np.arange(LANES)
    d = lane % HEAD_DIM
    chunk = d // ROPE_CHUNK
    e = d % ROPE_CHUNK
    j = e % ROPE_HALF
    inv = ROPE_THETA ** (-(j.astype(np.float64)) / ROPE_HALF)
    pos = np.where(chunk[None, :] == 0, row[:, None], col[:, None]).astype(np.float64)
    ang = pos * inv[None, :]
    cos = np.cos(ang)
    sin = np.sin(ang)
    first = (e < ROPE_HALF)[None, :]
    s_first = np.where(first, -sin, 0.0)
    s_second = np.where(first, 0.0, sin)
    return (cos.astype(np.float32), s_first.astype(np.float32), s_second.astype(np.float32))


@functools.lru_cache(maxsize=None)
def _chan_dft():
    c = np.arange(FOURIER_GROUP)
    ang = 2.0 * np.pi * ((c[:, None] * c[None, :]) % FOURIER_GROUP) / FOURIER_GROUP
    eye = np.eye(D_FOURIER // FOURIER_GROUP)
    re = np.kron(eye, np.cos(ang))
    im = np.kron(eye, -np.sin(ang))
    return np.concatenate([re, im], axis=1).astype(np.float32)


@functools.lru_cache(maxsize=None)
def _row_dft():
    k1 = np.arange(GRID_H)[:, None]
    r = np.arange(GRID_H)[None, :]
    out = np.zeros((GRID_W, 2 * GRID_H, 2 * GRID_H), np.float32)
    for w in range(GRID_W):
        m = (k1 * (GRID_W * r + w)) % SEQ
        ang = 2.0 * np.pi * m / SEQ
        c, s = np.cos(ang), np.sin(ang)
        out[w] = np.block([[c, s], [-s, c]])
    return out


@functools.lru_cache(maxsize=None)
def _col_dft():
    k2 = np.arange(GRID_W)
    ang = 2.0 * np.pi * ((k2[:, None] * k2[None, :]) % GRID_W) / GRID_W
    scale = 1.0 / math.sqrt(SEQ * FOURIER_GROUP)
    return (np.concatenate([np.cos(ang), np.sin(ang)], axis=1) * scale).astype(np.float32)


@functools.lru_cache(maxsize=None)
def _bias_index():
    c = np.arange(GRID_W)
    start = np.clip(c - NA_COLS // 2, 0, GRID_W - NA_COLS)
    valid = (c[None, :] >= start[:, None]) & (c[None, :] < start[:, None] + NA_COLS)
    dc = np.clip(c[None, :] - c[:, None] + (NA_COLS - 1), 0, 2 * NA_COLS - 2)
    return dc.astype(np.int32), valid


@functools.lru_cache(maxsize=None)
def _strict_lower(n):
    return np.tril(np.ones((n, n), np.float32), k=-1)


def _adaln_kernel(c_ref, w_ref, b_ref, o_ref):
    c = c_ref[...]
    s = c * jax.nn.sigmoid(c)
    o_ref[...] = jnp.dot(s.astype(BF16), w_ref[...].astype(BF16), preferred_element_type=F32) + b_ref[...]


def adaln(cond8, w, b):
    n = w.shape[1]
    tn = ADALN_TN
    return pl.pallas_call(
        _adaln_kernel,
        grid=(n // tn,),
        in_specs=[pl.BlockSpec((8, D_MODEL), lambda j: (0, 0)),
                  pl.BlockSpec((D_MODEL, tn), lambda j: (0, j)),
                  pl.BlockSpec((1, tn), lambda j: (0, j))],
        out_specs=pl.BlockSpec((8, tn), lambda j: (0, j)),
        out_shape=jax.ShapeDtypeStruct((8, n), F32),
        compiler_params=_cparams(("arbitrary",)),
        name="adaln",
    )(cond8, w, b.reshape(1, n))


def _norm_mod(x, g, shift, scale):
    ms = jnp.mean(x * x, axis=-1, keepdims=True)
    return (x * lax.rsqrt(ms + EPS) * g) * (1.0 + scale) + shift


def _in_proj_kernel(x_ref, mod_ref, g_ref, w_ref, cs_ref, cos_ref, s1_ref, s2_ref,
                    qr_ref, qp_ref, kr_ref, v_ref, a_ref, h_scr, a_scr):
    m = mod_ref[0]
    h_scr[...] = _norm_mod(x_ref[0], g_ref[...], m[0:1], m[1:2]).astype(BF16)
    cos, s1, s2 = cos_ref[...], s1_ref[...], s2_ref[...]

    def rope(t):
        return (t * cos + pltpu.roll(t, LANES - ROPE_HALF, axis=1) * s1
                + pltpu.roll(t, ROPE_HALF, axis=1) * s2)

    f = jnp.dot(h_scr[...], w_ref[:, 0:D_FOURIER], preferred_element_type=F32)
    a = jnp.dot(f.astype(BF16), cs_ref[...], preferred_element_type=F32)
    n_planes = 2 * D_FOURIER // LANES
    tile_rows = a.shape[0] // GRID_W
    for p in range(n_planes):
        for r in range(tile_rows):
            a_scr[p, r * ROW_PITCH:r * ROW_PITCH + GRID_W, :] = a[r * GRID_W:(r + 1) * GRID_W,
                                                                LANES * p:LANES * (p + 1)]
    for w in range(GRID_W):
        for p in range(n_planes):
            slab = a_scr[p, pl.ds(w, tile_rows, stride=ROW_PITCH), :]
            lo = w * D_FOURIER + (p % 2) * LANES
            a_ref[0, p // 2, :, lo:lo + LANES] = slab.astype(BF16)

    scale = HEAD_DIM ** -0.5 * LOG2E
    wide = 2 * LANES
    for c in range(D_NA // wide):
        lo = D_FOURIER + wide * c
        q = jnp.dot(h_scr[...], w_ref[:, lo:lo + wide], preferred_element_type=F32)
        k = jnp.dot(h_scr[...], w_ref[:, lo + D_NA:lo + D_NA + wide], preferred_element_type=F32)
        v = jnp.dot(h_scr[...], w_ref[:, lo + 2 * D_NA:lo + 2 * D_NA + wide], preferred_element_type=F32)
        v_ref[0, :, wide * c:wide * (c + 1)] = v.astype(BF16)
        for s in range(2):
            sl = slice(LANES * s, LANES * (s + 1))
            ol = slice(wide * c + LANES * s, wide * c + LANES * (s + 1))
            qs, ks = q[:, sl], k[:, sl]
            qp_ref[0, :, ol] = (qs * scale).astype(BF16)
            qr_ref[0, :, ol] = (rope(qs) * scale).astype(BF16)
            kr_ref[0, :, ol] = rope(ks).astype(BF16)


def in_proj(x, mods, g1, w_in_bf16):
    b, n, d = x.shape
    tm = TM_IN
    cos, s1, s2 = _rope_tables()
    cs = _mxu_const(_chan_dft())
    tok = lambda bi, i: (bi, i, 0)
    const2 = lambda bi, i: (0, 0)
    tab = pl.BlockSpec((tm, LANES), lambda bi, i: (i, 0))
    qkv_shape = jax.ShapeDtypeStruct((b, n, D_NA), BF16)
    qkv_spec = pl.BlockSpec((1, tm, D_NA), tok)
    return pl.pallas_call(
        _in_proj_kernel,
        grid=(b, n // tm),
        in_specs=[pl.BlockSpec((1, tm, d), tok),
                  pl.BlockSpec((1, 8, d), lambda bi, i: (bi, 0, 0)),
                  pl.BlockSpec((1, d), const2),
                  pl.BlockSpec((d, D_IN_PROJ), const2),
                  pl.BlockSpec((D_FOURIER, 2 * D_FOURIER), const2),
                  tab, tab, tab],
        out_specs=[qkv_spec, qkv_spec, qkv_spec, qkv_spec,
                   pl.BlockSpec((1, 2, tm // GRID_W, GRID_W * D_FOURIER), lambda bi, i: (bi, 0, i, 0))],
        out_shape=[qkv_shape, qkv_shape, qkv_shape, qkv_shape,
                   jax.ShapeDtypeStruct((b, 2, n // GRID_W, GRID_W * D_FOURIER), BF16)],
        scratch_shapes=[pltpu.VMEM((tm, d), BF16),
                        pltpu.VMEM((2 * D_FOURIER // LANES, (tm // GRID_W) * ROW_PITCH, LANES), F32)],
        compiler_params=_cparams(("arbitrary", "arbitrary")),
        name="in_proj",
    )(x, mods, g1.reshape(1, d), w_in_bf16, cs, jnp.asarray(cos), jnp.asarray(s1), jnp.asarray(s2))


def _ctx_proj_kernel(x_ref, mod_ref, g_ref, w_ref, k_ref, v_ref):
    m = mod_ref[...]
    h = _norm_mod(x_ref[0], g_ref[...], m[0:1], m[1:2]).astype(BF16)
    k_ref[0] = jnp.dot(h, w_ref[:, 0:D_NA], preferred_element_type=F32).astype(BF16)
    v_ref[0] = jnp.dot(h, w_ref[:, D_NA:2 * D_NA], preferred_element_type=F32).astype(BF16)


def ctx_proj(ctx, mod_ctx, g1, w_kv_bf16):
    b, l, d = ctx.shape
    shape = jax.ShapeDtypeStruct((b, l, D_NA), BF16)
    spec = pl.BlockSpec((1, l, D_NA), lambda bi: (bi, 0, 0))
    return pl.pallas_call(
        _ctx_proj_kernel,
        grid=(b,),
        in_specs=[pl.BlockSpec((1, l, d), lambda bi: (bi, 0, 0)),
                  pl.BlockSpec((8, d), lambda bi: (0, 0)),
                  pl.BlockSpec((1, d), lambda bi: (0, 0)),
                  pl.BlockSpec((d, 2 * D_NA), lambda bi: (0, 0))],
        out_specs=[spec, spec],
        out_shape=[shape, shape],
        compiler_params=_cparams(("arbitrary",)),
        name="ctx_proj",
    )(ctx, mod_ctx, g1.reshape(1, d), w_kv_bf16)


N_PLANES = D_FOURIER // LANES
K1_PITCH = GRID_H + 8
K2_PITCH = GRID_W + 8


def _dft_rows_kernel(a_ref, g_ref, z_ref, z_scr):
    for j in range(DFT_TW):
        sl = slice(D_FOURIER * j, D_FOURIER * (j + 1))
        rhs = jnp.concatenate([a_ref[0, 0, :, sl], a_ref[0, 1, :, sl]], axis=0)
        z = jnp.dot(g_ref[j], rhs, preferred_element_type=F32)
        for c in range(2):
            for p in range(N_PLANES):
                z_scr[c * N_PLANES + p, j * K1_PITCH:j * K1_PITCH + GRID_H, :] = (
                    z[c * GRID_H:(c + 1) * GRID_H, LANES * p:LANES * (p + 1)])
    for k1 in range(GRID_H):
        for c in range(2):
            for p in range(N_PLANES):
                slab = z_scr[c * N_PLANES + p, pl.ds(k1, DFT_TW, stride=K1_PITCH), :]
                lo = k1 * D_FOURIER + p * LANES
                z_ref[0, c, :, lo:lo + LANES] = slab.astype(BF16)


def dft_rows(a):
    b = a.shape[0]
    g = _mxu_const(_row_dft())
    return pl.pallas_call(
        _dft_rows_kernel,
        grid=(GRID_W // DFT_TW, b),
        in_specs=[pl.BlockSpec((1, 2, GRID_H, DFT_TW * D_FOURIER), lambda j, bi: (bi, 0, 0, j)),
                  pl.BlockSpec((DFT_TW, 2 * GRID_H, 2 * GRID_H), lambda j, bi: (j, 0, 0))],
        out_specs=pl.BlockSpec((1, 2, DFT_TW, GRID_H * D_FOURIER), lambda j, bi: (bi, 0, j, 0)),
        out_shape=jax.ShapeDtypeStruct((b, 2, GRID_W, GRID_H * D_FOURIER), BF16),
        scratch_shapes=[pltpu.VMEM((2 * N_PLANES, DFT_TW * K1_PITCH, LANES), F32)],
        compiler_params=_cparams(("arbitrary", "arbitrary")),
        name="dft_rows",
    )(a, g)


def _dft_cols_kernel(z_ref, cs_ref, o_ref, o_scr):
    for j in range(DFT_TK1):
        sl = slice(D_FOURIER * j, D_FOURIER * (j + 1))
        rhs = jnp.concatenate([z_ref[0, 0, :, sl], z_ref[0, 1, :, sl]], axis=0)
        out = jnp.dot(cs_ref[...], rhs, preferred_element_type=F32)
        for p in range(N_PLANES):
            o_scr[p, j * K2_PITCH:j * K2_PITCH + GRID_W, :] = out[:, LANES * p:LANES * (p + 1)]
    for k2 in range(GRID_W):
        for p in range(N_PLANES):
            slab = o_scr[p, pl.ds(k2, DFT_TK1, stride=K2_PITCH), :]
            o_ref[0, k2, :, LANES * p:LANES * (p + 1)] = slab.astype(BF16)


def dft_cols(z):
    b = z.shape[0]
    cs = _mxu_const(_col_dft())
    out = pl.pallas_call(
        _dft_cols_kernel,
        grid=(b, GRID_H // DFT_TK1),
        in_specs=[pl.BlockSpec((1, 2, GRID_W, DFT_TK1 * D_FOURIER), lambda bi, j: (bi, 0, 0, j)),
                  pl.BlockSpec((GRID_W, 2 * GRID_W), lambda bi, j: (0, 0))],
        out_specs=pl.BlockSpec((1, GRID_W, DFT_TK1, D_FOURIER), lambda bi, j: (bi, 0, j, 0)),
        out_shape=jax.ShapeDtypeStruct((b, GRID_W, GRID_H, D_FOURIER), BF16),
        scratch_shapes=[pltpu.VMEM((N_PLANES, DFT_TK1 * K2_PITCH, LANES), F32)],
        compiler_params=_cparams(("arbitrary", "arbitrary")),
        name="dft_cols",
    )(z, cs)
    return out.reshape(b, SEQ, D_FOURIER)


ROW_KEYS = NA_ROWS * GRID_W
ROW_TILES = ROW_KEYS // LANES
N_DR = 2 * NA_ROWS - 2
SOFT_ROWS = 32
ATT_PAIRS = ATT_ROWS // 2


def _attention_kernel(qr_ref, qp_ref, k_ref, v_ref, kc_ref, vc_ref, tb_ref, o_ref,
                      sc_scr, pc_scr, s_scr, p_scr):
    rb = pl.program_id(2)
    first1 = lax.broadcasted_iota(jnp.int32, (GRID_W, LANES), 1) < HEAD_DIM
    nt = (((1,), (1,)), ((), ()))

    def window(jp, i):
        r = rb * ATT_ROWS + 2 * jp + i
        rs = jnp.clip(r - NA_ROWS // 2, 0, GRID_H - NA_ROWS)
        return r, rs, pl.multiple_of(rs * GRID_W, GRID_W)

    def split_heads(q1):
        z1 = jnp.zeros_like(q1)
        return jnp.concatenate([jnp.where(first1, q1, z1), jnp.where(first1, z1, q1)], axis=0)

    def scores(jp):
        qp_rows = []
        for i in range(2):
            _, _, koff = window(jp, i)
            qs = slice((2 * jp + i) * GRID_W, (2 * jp + i + 1) * GRID_W)
            kw = k_ref[0, pl.ds(koff, ROW_KEYS), :]
            s_scr[jp, i * 2 * GRID_W:(i + 1) * 2 * GRID_W, :] = lax.dot_general(
                split_heads(qr_ref[0, qs, :]), kw, nt, preferred_element_type=F32)
            qp_rows.append(split_heads(qp_ref[0, qs, :]))
        sc_scr[jp] = lax.dot_general(jnp.concatenate(qp_rows, axis=0), kc_ref[0], nt,
                                     preferred_element_type=F32)

    def softmax(jp):
        for c in range(4):
            i, hh = c // 2, c % 2
            r, rs, _ = window(jp, i)
            d0 = rs - r + NA_ROWS - 1
            for h in range(GRID_W // SOFT_ROWS):
                lo = h * SOFT_ROWS
                rows = slice(c * GRID_W + lo, c * GRID_W + lo + SOFT_ROWS)
                tiles = [s_scr[jp, rows, t * LANES:(t + 1) * LANES]
                         + tb_ref[0, hh, d0 + 2 * t, lo:lo + SOFT_ROWS, :] for t in range(ROW_TILES)]
                sc = sc_scr[jp, rows, :]
                mt = jnp.maximum(sc[:, :LANES], sc[:, LANES:])
                for tl in tiles:
                    mt = jnp.maximum(mt, tl)
                m = jnp.max(mt, axis=1, keepdims=True)
                pc_scr[jp, rows, :] = jnp.exp2(sc - m).astype(BF16)
                for t, tl in enumerate(tiles):
                    p_scr[jp, rows, t * LANES:(t + 1) * LANES] = jnp.exp2(tl - m).astype(BF16)

    def weighted_values(jp):
        vc = jnp.concatenate([vc_ref[0], jnp.ones((CTX_LEN, LANES), BF16)], axis=1)
        oc = jnp.dot(pc_scr[jp], vc, preferred_element_type=F32)
        for i in range(2):
            _, _, koff = window(jp, i)
            rows = slice(i * 2 * GRID_W, (i + 1) * 2 * GRID_W)
            vw = jnp.concatenate([v_ref[0, pl.ds(koff, ROW_KEYS), :], jnp.ones((ROW_KEYS, LANES), BF16)],
                                 axis=1)
            o = jnp.dot(p_scr[jp, rows, :], vw, preferred_element_type=F32) + oc[rows]
            oa, ob = o[:GRID_W], o[GRID_W:]
            out = jnp.where(first1, oa[:, :LANES] / oa[:, LANES:], ob[:, :LANES] / ob[:, LANES:])
            o_ref[0, (2 * jp + i) * GRID_W:(2 * jp + i + 1) * GRID_W, :] = out.astype(BF16)

    scores(0)
    scores(1)
    softmax(0)
    for jp in range(ATT_PAIRS):
        if jp + 2 < ATT_PAIRS:
            scores(jp + 2)
        if jp + 1 < ATT_PAIRS:
            softmax(jp + 1)
        weighted_values(jp)


def bias_tables(rpb):
    dc, valid = _bias_index()
    n_dc = 2 * NA_COLS - 1
    onehot = (dc.reshape(1, -1) == np.arange(n_dc).reshape(-1, 1)).astype(np.float32)
    t = jnp.dot(rpb.reshape(-1, n_dc), jnp.asarray(onehot), precision=lax.Precision.HIGHEST)
    t = jnp.where(valid[None, None], LOG2E * t.reshape(N_HEADS, 2 * NA_ROWS - 1, GRID_W, GRID_W), NEG)
    tb = pl.pallas_call(
        _bias_pairs_kernel,
        grid=(N_HEADS,),
        in_specs=[pl.BlockSpec((1, N_DR + 1, GRID_W, GRID_W), lambda h: (h, 0, 0, 0))],
        out_specs=pl.BlockSpec((1, N_DR, GRID_W, LANES), lambda h: (h, 0, 0, 0)),
        out_shape=jax.ShapeDtypeStruct((N_HEADS, N_DR, GRID_W, LANES), F32),
        compiler_params=_cparams(("arbitrary",)),
        name="bias_pairs",
    )(t)
    return tb.reshape(N_PAIRS, 2, N_DR, GRID_W, LANES)


def _bias_pairs_kernel(t_ref, o_ref):
    for d in range(N_DR):
        o_ref[0, d] = jnp.concatenate([t_ref[0, d], t_ref[0, d + 1]], axis=1)


def attention(qr, qp, kr, v, kc, vc, tb):
    b, n, _ = qr.shape
    tq = ATT_ROWS * GRID_W
    qspec = pl.BlockSpec((1, tq, LANES), lambda bi, hp, i: (bi, i, hp))
    kspec = pl.BlockSpec((1, n, LANES), lambda bi, hp, i: (bi, 0, hp))
    cspec = pl.BlockSpec((1, CTX_LEN, LANES), lambda bi, hp, i: (bi, 0, hp))
    return pl.pallas_call(
        _attention_kernel,
        grid=(b, N_PAIRS, GRID_H // ATT_ROWS),
        in_specs=[qspec, qspec, kspec, kspec, cspec, cspec,
                  pl.BlockSpec((1, 2, N_DR, GRID_W, LANES), lambda bi, hp, i: (hp, 0, 0, 0, 0))],
        out_specs=qspec,
        out_shape=jax.ShapeDtypeStruct((b, n, D_NA), BF16),
        scratch_shapes=[pltpu.VMEM((ATT_PAIRS, 4 * GRID_W, CTX_LEN), F32),
                        pltpu.VMEM((ATT_PAIRS, 4 * GRID_W, CTX_LEN), BF16),
                        pltpu.VMEM((ATT_PAIRS, 4 * GRID_W, ROW_KEYS), F32),
                        pltpu.VMEM((ATT_PAIRS, 4 * GRID_W, ROW_KEYS), BF16)],
        compiler_params=_cparams(("arbitrary", "arbitrary", "arbitrary")),
        name="attention",
    )(qr, qp, kr, v, kc, vc, tb)


def _rms(x, g):
    ms = jnp.mean(x * x, axis=-1, keepdims=True)
    return x * lax.rsqrt(ms + EPS) * g


TILE_ROWS = D_MODEL // LANES


def _store_token_tiles(ref, val, start=0):
    rows = val.shape[0]
    for j in range(TILE_ROWS):
        ref[pl.ds(start + j, rows, stride=TILE_ROWS), :] = val[:, LANES * j:LANES * (j + 1)]


def _load_token_tiles(ref, start, rows):
    return jnp.concatenate(
        [ref[pl.ds(start + j, rows, stride=TILE_ROWS), :] for j in range(TILE_ROWS)], axis=1)


def _out_proj_kernel(fr_ref, na_ref, x_ref, mod_ref, wf_ref, wo_ref, go_ref, g2_ref, wr_ref, tri_ref,
                     x1_ref, h2_ref, meta_ref, metat_ref, cnt_ref, run_scr):
    @pl.when((pl.program_id(0) == 0) & (pl.program_id(1) == 0))
    def _():
        run_scr[...] = jnp.zeros_like(run_scr)

    m = mod_ref[0]
    go = go_ref[...]
    rows_c = OUT_CHUNK
    lane = lax.broadcasted_iota(jnp.int32, (rows_c, LANES), 1).astype(F32)
    ninf = jnp.float32(-jnp.inf)

    def argmax_first(vals):
        mx = jnp.max(vals, axis=1, keepdims=True)
        idx = jnp.min(jnp.where(vals == mx, lane, float(LANES)), axis=1, keepdims=True)
        return mx, idx

    run = run_scr[...]
    for c in range(x_ref.shape[1] // rows_c):
        rows = slice(c * rows_c, (c + 1) * rows_c)
        fo = jnp.dot(fr_ref[0, rows, :], wf_ref[...], preferred_element_type=F32)
        fn = _rms(fo, go[:, :D_FOURIER]).astype(BF16)
        nn = _rms(na_ref[0, rows, :].astype(F32), go[:, D_FOURIER:]).astype(BF16)
        y = (jnp.dot(fn, wo_ref[0:D_FOURIER, :], preferred_element_type=F32)
             + jnp.dot(nn, wo_ref[D_FOURIER:, :], preferred_element_type=F32))
        x1 = x_ref[0, rows, :] + m[2:3] * y
        x1_ref[0, rows, :] = x1
        h2 = _norm_mod(x1, g2_ref[...], m[3:4], m[4:5])
        _store_token_tiles(h2_ref, h2, c * rows_c * TILE_ROWS)
        logits = jnp.dot(h2.astype(BF16), wr_ref[...], preferred_element_type=F32)

        lg = jnp.where(lane < N_GROUPS, logits, ninf)
        gmax, gidx = argmax_first(lg)
        pg = 1.0 / jnp.sum(jnp.exp(lg - gmax), axis=1, keepdims=True)
        lo = N_GROUPS + EXPERTS_PER_GROUP * gidx
        le = jnp.where((lane >= lo) & (lane < lo + EXPERTS_PER_GROUP), logits, ninf)
        e1, i1 = argmax_first(le)
        e2, i2 = argmax_first(jnp.where(lane == i1, ninf, le))
        dd = jnp.exp(e2 - e1)
        gate1 = pg / (1.0 + dd)
        gate2 = pg * dd / (1.0 + dd)

        hot1 = lane == i1
        hot2 = lane == i2
        onehot = jnp.where(hot1 | hot2, 1.0, 0.0)
        cnt = jnp.dot(tri_ref[...], onehot.astype(BF16), preferred_element_type=F32) + run
        rank1 = jnp.sum(jnp.where(hot1, cnt, 0.0), axis=1, keepdims=True)
        rank2 = jnp.sum(jnp.where(hot2, cnt, 0.0), axis=1, keepdims=True)
        run = run + jnp.sum(onehot, axis=0, keepdims=True)

        meta = jnp.where(lane == 0, i1 - N_GROUPS,
               jnp.where(lane == 1, i2 - N_GROUPS,
               jnp.where(lane == 2, rank1,
               jnp.where(lane == 3, rank2,
               jnp.where(lane == 4, gate1,
               jnp.where(lane == 5, gate2, 0.0))))))
        meta_ref[rows, :] = meta
        metat_ref[:, rows] = jnp.transpose(meta)[0:8, :]
    run_scr[...] = run
    cnt_ref[...] = jnp.broadcast_to(run, cnt_ref.shape)


def out_proj(fr, na, x, mods, w_fmix_bf16, w_out_bf16, g_out, g2, w_router_bf16):
    b, n, d = x.shape
    tm = TM_PROJ
    steps = n // tm
    tok = lambda bi, i: (bi, i, 0)
    const2 = lambda bi, i: (0, 0)
    flat = lambda bi, i: (bi * steps + i, 0)
    tri = _mxu_const(_strict_lower(OUT_CHUNK))
    return pl.pallas_call(
        _out_proj_kernel,
        grid=(b, steps),
        in_specs=[pl.BlockSpec((1, tm, D_FOURIER), tok),
                  pl.BlockSpec((1, tm, D_NA), tok),
                  pl.BlockSpec((1, tm, d), tok),
                  pl.BlockSpec((1, 8, d), lambda bi, i: (bi, 0, 0)),
                  pl.BlockSpec((D_FOURIER, D_FOURIER), const2),
                  pl.BlockSpec((d, d), const2),
                  pl.BlockSpec((1, d), const2),
                  pl.BlockSpec((1, d), const2),
                  pl.BlockSpec((d, LANES), const2),
                  pl.BlockSpec((OUT_CHUNK, OUT_CHUNK), const2)],
        out_specs=[pl.BlockSpec((1, tm, d), tok),
                   pl.BlockSpec((tm * TILE_ROWS, LANES), flat),
                   pl.BlockSpec((tm, LANES), flat),
                   pl.BlockSpec((8, tm), lambda bi, i: (0, bi * steps + i)),
                   pl.BlockSpec((8, LANES), const2)],
        out_shape=[jax.ShapeDtypeStruct((b, n, d), F32),
                   jax.ShapeDtypeStruct((b * n * TILE_ROWS, LANES), F32),
                   jax.ShapeDtypeStruct((b * n, LANES), F32),
                   jax.ShapeDtypeStruct((8, b * n), F32),
                   jax.ShapeDtypeStruct((8, LANES), F32)],
        scratch_shapes=[pltpu.VMEM((1, LANES), F32)],
        compiler_params=_cparams(("arbitrary", "arbitrary")),
        name="out_proj",
    )(fr, na, x, mods, w_fmix_bf16, w_out_bf16, g_out.reshape(1, d), g2.reshape(1, d), w_router_bf16, tri)


def _dispatch_kernel(dest_ref, pstart_ref, count_ref, h2_ref, xs_hbm, zero_scr, sem, pad_sem):
    i = pl.program_id(0)
    nt = dest_ref.shape[0] // 2

    @pl.when(i == 0)
    def _():
        zero_scr[...] = jnp.zeros_like(zero_scr)

        def per_expert(e, npad):
            lo = pstart_ref[e] + count_ref[e]
            mid = pstart_ref[e] + ((count_ref[e] + MOE_CHUNK - 1) // MOE_CHUNK) * MOE_CHUNK
            hi = pstart_ref[e] + ((count_ref[e] + MOE_BLK - 1) // MOE_BLK) * MOE_BLK

            def fill(s, carry):
                pltpu.make_async_copy(zero_scr.at[pl.ds(0, TILE_ROWS)],
                                      xs_hbm.at[pl.ds(s * TILE_ROWS, TILE_ROWS)], pad_sem).start()
                return carry

            def fill_chunk(s, carry):
                pltpu.make_async_copy(zero_scr.at[pl.ds(0, MOE_CHUNK * TILE_ROWS)],
                                      xs_hbm.at[pl.ds(mid * TILE_ROWS + s * MOE_CHUNK * TILE_ROWS,
                                                      MOE_CHUNK * TILE_ROWS)], pad_sem).start()
                return carry

            lax.fori_loop(lo, mid, fill, 0)
            lax.fori_loop(0, (hi - mid) // MOE_CHUNK, fill_chunk, 0)
            return npad + (hi - lo)

        npad = lax.fori_loop(0, N_EXPERTS, per_expert, 0)

        blk_rows = MOE_BLK * TILE_ROWS
        first_free = (pstart_ref[N_EXPERTS - 1] + count_ref[N_EXPERTS - 1] + MOE_BLK - 1) // MOE_BLK
        n_blocks = xs_hbm.shape[0] // blk_rows

        def fill_block(bk, carry):
            pltpu.make_async_copy(zero_scr, xs_hbm.at[pl.ds(bk * blk_rows, blk_rows)], pad_sem).start()
            return carry

        lax.fori_loop(first_free, n_blocks, fill_block, 0)
        rows = npad * TILE_ROWS + (n_blocks - first_free) * blk_rows

        @pl.when(rows > 0)
        def _():
            pltpu.make_async_copy(xs_hbm.at[pl.ds(0, rows)], xs_hbm.at[pl.ds(0, rows)], pad_sem).wait()

    def issue(c, carry):
        t0 = c * DMA_CHUNK
        for u in range(DMA_CHUNK):
            for k in range(2):
                d = dest_ref[k * nt + i * TD_DISP + t0 + u]
                pltpu.make_async_copy(h2_ref.at[pl.ds((t0 + u) * TILE_ROWS, TILE_ROWS)],
                                      xs_hbm.at[pl.ds(d * TILE_ROWS, TILE_ROWS)], sem).start(priority=k)
        return carry

    lax.fori_loop(0, TD_DISP // DMA_CHUNK, issue, 0)
    rows = 2 * TD_DISP * TILE_ROWS
    pltpu.make_async_copy(xs_hbm.at[pl.ds(0, rows)], xs_hbm.at[pl.ds(0, rows)], sem).wait()


def dispatch(h2_tiles, dest, pstarts, counts, n_slots):
    nt = dest.shape[0] // 2
    grid_spec = pltpu.PrefetchScalarGridSpec(
        num_scalar_prefetch=3,
        grid=(nt // TD_DISP,),
        in_specs=[pl.BlockSpec((TD_DISP * TILE_ROWS, LANES), lambda i, ds, ps, ct: (i, 0))],
        out_specs=pl.BlockSpec(memory_space=pl.ANY),
        scratch_shapes=[pltpu.VMEM((MOE_BLK * TILE_ROWS, LANES), F32),
                        pltpu.SemaphoreType.DMA(()),
                        pltpu.SemaphoreType.DMA(())],
    )
    return pl.pallas_call(
        _dispatch_kernel,
        grid_spec=grid_spec,
        out_shape=jax.ShapeDtypeStruct((n_slots * TILE_ROWS, LANES), F32),
        compiler_params=_cparams(("arbitrary",)),
        name="dispatch",
    )(dest, pstarts, counts, h2_tiles)


def _experts_kernel(be_ref, nused_ref, valid_ref, xs_ref, wg_ref, wu_ref, wd_ref, ys_ref,
                    wg_scr, wu_scr, wd_scr):
    i = pl.program_id(0)
    valid = valid_ref[i]
    changed = (i == 0) | (be_ref[i] != be_ref[jnp.maximum(i - 1, 0)])
    chunk_rows = MOE_CHUNK * TILE_ROWS

    @pl.when(changed & (valid > 0))
    def _():
        wg_scr[...] = wg_ref[0].astype(BF16)
        wu_scr[...] = wu_ref[0].astype(BF16)
        wd_scr[...] = wd_ref[0].astype(BF16)

    def run(n_chunks):
        hmids = []
        for h in range(n_chunks):
            x = _load_token_tiles(xs_ref, h * chunk_rows, MOE_CHUNK).astype(BF16)
            g = jnp.dot(x, wg_scr[...], preferred_element_type=F32)
            u = jnp.dot(x, wu_scr[...], preferred_element_type=F32)
            hmids.append((g * jax.nn.sigmoid(g) * u).astype(BF16))
        for h, hmid in enumerate(hmids):
            _store_token_tiles(ys_ref, jnp.dot(hmid, wd_scr[...], preferred_element_type=F32), h * chunk_rows)
        if n_chunks * chunk_rows < ys_ref.shape[0]:
            ys_ref[n_chunks * chunk_rows:, :] = jnp.zeros((ys_ref.shape[0] - n_chunks * chunk_rows, LANES), F32)

    for n_chunks in range(MOE_BLK // MOE_CHUNK + 1):
        lo, hi = (n_chunks - 1) * MOE_CHUNK, n_chunks * MOE_CHUNK
        pl.when((valid > lo) & (valid <= hi))(functools.partial(run, n_chunks))


def experts(xs_tiles, block_expert, nused, block_valid, w_gate, w_up, w_down):
    d = D_MODEL
    nblk = block_expert.shape[0]
    blk_rows = MOE_BLK * TILE_ROWS
    wmap = lambda i, be, nu, bv: (be[i], 0, 0)
    grid_spec = pltpu.PrefetchScalarGridSpec(
        num_scalar_prefetch=3,
        grid=(nblk,),
        in_specs=[pl.BlockSpec((blk_rows, LANES), lambda i, be, nu, bv: (jnp.minimum(i, nu[0] - 1), 0)),
                  pl.BlockSpec((1, d, D_EXPERT), wmap),
                  pl.BlockSpec((1, d, D_EXPERT), wmap),
                  pl.BlockSpec((1, D_EXPERT, d), wmap)],
        out_specs=pl.BlockSpec((blk_rows, LANES), lambda i, be, nu, bv: (i, 0)),
        scratch_shapes=[pltpu.VMEM((d, D_EXPERT), BF16),
                        pltpu.VMEM((d, D_EXPERT), BF16),
                        pltpu.VMEM((D_EXPERT, d), BF16)],
    )
    return pl.pallas_call(
        _experts_kernel,
        grid_spec=grid_spec,
        out_shape=jax.ShapeDtypeStruct((nblk * blk_rows, LANES), F32),
        compiler_params=_cparams(("arbitrary",)),
        name="experts",
    )(block_expert, nused, block_valid, xs_tiles, w_gate, w_up, w_down)


def _combine_kernel(dest_ref, ys_hbm, x1_ref, meta_ref, mod_ref, gf_ref, o_ref, ybuf, sem):
    i = pl.program_id(0)
    nstep = pl.num_programs(0)
    tc = TC_COMB
    nt = dest_ref.shape[0] // 2
    half_rows = tc * TILE_ROWS
    buf_rows = 2 * half_rows

    def gather(step, slot):
        def issue(c, carry):
            t0 = c * DMA_CHUNK
            for u in range(DMA_CHUNK):
                for k in range(2):
                    d = dest_ref[k * nt + step * tc + t0 + u]
                    pltpu.make_async_copy(
                        ys_hbm.at[pl.ds(d * TILE_ROWS, TILE_ROWS)],
                        ybuf.at[pl.ds(slot * buf_rows + k * half_rows + (t0 + u) * TILE_ROWS, TILE_ROWS)],
                        sem.at[slot]).start(priority=k)
            return carry

        lax.fori_loop(0, tc // DMA_CHUNK, issue, 0)

    @pl.when(i == 0)
    def _():
        gather(0, 0)

    @pl.when(i + 1 < nstep)
    def _():
        gather(i + 1, (i + 1) % 2)

    slot = i % 2
    start = pl.multiple_of(slot * buf_rows, buf_rows)
    pltpu.make_async_copy(ys_hbm.at[pl.ds(0, buf_rows)], ybuf.at[pl.ds(start, buf_rows)], sem.at[slot]).wait()
    meta = meta_ref[...]
    y0 = _load_token_tiles(ybuf, start, tc)
    y1 = _load_token_tiles(ybuf, start + half_rows, tc)
    moe = y0 * meta[:, 4:5] + y1 * meta[:, 5:6]
    x2 = x1_ref[...] + mod_ref[0][5:6] * moe
    o_ref[...] = _rms(x2, gf_ref[...])


def combine(dest_flat, ys, x1_flat, meta, mods, g_final, n_per_batch):
    nt, d = x1_flat.shape
    tc = TC_COMB
    per_b = n_per_batch // tc
    grid_spec = pltpu.PrefetchScalarGridSpec(
        num_scalar_prefetch=1,
        grid=(nt // tc,),
        in_specs=[pl.BlockSpec(memory_space=pl.ANY),
                  pl.BlockSpec((tc, d), lambda i, ds: (i, 0)),
                  pl.BlockSpec((tc, LANES), lambda i, ds: (i, 0)),
                  pl.BlockSpec((1, 8, d), lambda i, ds: (i // per_b, 0, 0)),
                  pl.BlockSpec((1, d), lambda i, ds: (0, 0))],
        out_specs=pl.BlockSpec((tc, d), lambda i, ds: (i, 0)),
        scratch_shapes=[pltpu.VMEM((2 * 2 * tc * TILE_ROWS, LANES), F32),
                        pltpu.SemaphoreType.DMA((2,))],
    )
    return pl.pallas_call(
        _combine_kernel,
        grid_spec=grid_spec,
        out_shape=jax.ShapeDtypeStruct((nt, d), F32),
        compiler_params=_cparams(("arbitrary",)),
        name="combine",
    )(dest_flat, ys, x1_flat, meta, mods, g_final.reshape(1, d))


def _slots_kernel(pstart_ref, metat_ref, dest_ref):
    eid = metat_ref[0:2, :]
    slot = metat_ref[2:4, :]
    for e in range(N_EXPERTS):
        slot = slot + jnp.where(eid == float(e), pstart_ref[e].astype(F32), 0.0)
    dest_ref[...] = slot.astype(jnp.int32)


def _dispatch_plan(metat, counts_row, nt):
    counts = counts_row[N_GROUPS:N_GROUPS + N_EXPERTS].astype(jnp.int32)
    pcounts = ((counts + MOE_BLK - 1) // MOE_BLK) * MOE_BLK
    pends = jnp.cumsum(pcounts)
    pstarts = pends - pcounts
    dest = pl.pallas_call(
        _slots_kernel,
        grid_spec=pltpu.PrefetchScalarGridSpec(
            num_scalar_prefetch=1, grid=(1,),
            in_specs=[pl.BlockSpec(metat.shape, lambda i, ps: (0, 0))],
            out_specs=pl.BlockSpec((2, nt), lambda i, ps: (0, 0))),
        out_shape=jax.ShapeDtypeStruct((2, nt), jnp.int32),
        compiler_params=_cparams(("arbitrary",)),
        name="slots",
    )(pstarts.astype(jnp.int32), metat).reshape(-1)
    nblk = (nt * 2) // MOE_BLK + N_EXPERTS
    first_slot = jnp.arange(nblk, dtype=jnp.int32) * MOE_BLK
    block_expert = jnp.minimum(
        jnp.sum((pends[None, :] <= first_slot[:, None]).astype(jnp.int32), axis=1), N_EXPERTS - 1)
    nused = (pends[-1] // MOE_BLK).astype(jnp.int32).reshape(1)
    seg_end = jnp.sum(jnp.where(block_expert[:, None] == jnp.arange(N_EXPERTS)[None, :],
                                (pstarts + counts)[None, :], 0), axis=1)
    block_valid = jnp.where(first_slot < pends[-1], jnp.clip(seg_end - first_slot, 0, MOE_BLK), 0)
    return (dest, pstarts.astype(jnp.int32), counts, block_expert.astype(jnp.int32), nused,
            block_valid.astype(jnp.int32))


def kernel(x, c, ctx, c_ctx, w_ada, b_ada, g_norm1, w_in, w_fmix, rpb, g_out, w_out, g_norm2,
           w_router_group, w_router_expert, w_gate, w_up, w_down, g_final):
    b, n, d = x.shape
    assert (b, n, d) == (c.shape[0], SEQ, D_MODEL) and w_ada.shape[0] == 1
    nt = b * n

    cond8 = jnp.zeros((8, d), F32).at[0:b].set(c).at[b].set(c_ctx)
    mod = adaln(cond8, w_ada[0], b_ada[0])
    mods = jnp.pad(mod[0:b].reshape(b, N_MOD, d), ((0, 0), (0, 2), (0, 0)))
    mod_ctx = jnp.pad(mod[b].reshape(N_MOD, d), ((0, 2), (0, 0)))

    w_in_b = w_in[0].astype(BF16)
    qr, qp, kr, v, a = in_proj(x, mods, g_norm1[0], w_in_b)
    kc, vc = ctx_proj(ctx, mod_ctx, g_norm1[0], w_in_b[:, D_FOURIER + D_NA:])

    fr = dft_cols(dft_rows(a))
    na = attention(qr, qp, kr, v, kc, vc, bias_tables(rpb[0]))

    w_router = jnp.concatenate(
        [w_router_group[0], w_router_expert[0],
         jnp.zeros((d, LANES - N_GROUPS - N_EXPERTS), F32)], axis=1).astype(BF16)
    x1, h2_tiles, meta, metat, cnt = out_proj(fr, na, x, mods, w_fmix[0].astype(BF16),
                                              w_out[0].astype(BF16), g_out[0], g_norm2[0], w_router)

    dest, pstarts, counts, block_expert, nused, block_valid = _dispatch_plan(metat, cnt[0], nt)
    xs_tiles = dispatch(h2_tiles, dest, pstarts, counts, block_expert.shape[0] * MOE_BLK)
    ys_tiles = experts(xs_tiles, block_expert, nused, block_valid, w_gate[0], w_up[0], w_down[0])
    out = combine(dest, ys_tiles, x1.reshape(nt, d), meta, mods, g_final, n)
    return out.reshape(b, n, d)
```

```python
import functools
import math

import numpy as np
import jax
import jax.numpy as jnp
from jax import lax
from jax.experimental import pallas as pl
from jax.experimental.pallas import tpu as pltpu

F32 = jnp.float32
BF16 = jnp.bfloat16

D_MODEL = 1024
GRID_W = 64
GRID_H = 128
SEQ = GRID_W * GRID_H
CTX_LEN = 256
D_FOURIER = 256
FOURIER_GROUP = 64
HEAD_DIM = 64
N_HEADS = 12
D_NA = N_HEADS * HEAD_DIM
N_PAIRS = N_HEADS // 2
NA_ROWS = 8
NA_COLS = 16
ROPE_THETA = 10000.0
ROPE_CHUNK = HEAD_DIM // 2
ROPE_HALF = ROPE_CHUNK // 2
N_GROUPS = 4
EXPERTS_PER_GROUP = 8
N_EXPERTS = N_GROUPS * EXPERTS_PER_GROUP
D_EXPERT = 512
N_MOD = 6
D_IN_PROJ = D_FOURIER + 3 * D_NA
EPS = 1e-6
LANES = 128
NEG = -1e30
LOG2E = math.log2(math.e)

ADALN_TN = 1536
TM_IN = 1024
ROW_PITCH = 72
TM_PROJ = 1024
OUT_CHUNK = 512
ATT_ROWS = 64
DFT_TW = 16
DFT_TK1 = 16
MOE_BLK = 512
MOE_CHUNK = 256
TD_DISP = 2048
TC_COMB = 256
DMA_CHUNK = 8
VMEM_LIMIT = 56 * 1024 * 1024


def _cparams(sem):
    return pltpu.CompilerParams(dimension_semantics=sem, vmem_limit_bytes=VMEM_LIMIT)


def _mxu_const(table):
    return jnp.asarray(table, F32).astype(BF16)


@functools.lru_cache(maxsize=None)
def _rope_tables():
    t = np.arange(SEQ)
    row, col = t // GRID_W, t % GRID_W
    lane = np.arange(LANES)
    d = lane % HEAD_DIM
    chunk = d // ROPE_CHUNK
    e = d % ROPE_CHUNK
    j = e % ROPE_HALF
    inv = ROPE_THETA ** (-(j.astype(np.float64)) / ROPE_HALF)
    pos = np.where(chunk[None, :] == 0, row[:, None], col[:, None]).astype(np.float64)
    ang = pos * inv[None, :]
    cos = np.cos(ang)
    sin = np.sin(ang)
    first = (e < ROPE_HALF)[None, :]
    s_first = np.where(first, -sin, 0.0)
    s_second = np.where(first, 0.0, sin)
    return (cos.astype(np.float32), s_first.astype(np.float32), s_second.astype(np.float32))


@functools.lru_cache(maxsize=None)
def _chan_dft():
    c = np.arange(FOURIER_GROUP)
    ang = 2.0 * np.pi * ((c[:, None] * c[None, :]) % FOURIER_GROUP) / FOURIER_GROUP
    eye = np.eye(D_FOURIER // FOURIER_GROUP)
    re = np.kron(eye, np.cos(ang))
    im = np.kron(eye, -np.sin(ang))
    return np.concatenate([re, im], axis=1).astype(np.float32)


@functools.lru_cache(maxsize=None)
def _row_dft():
    k1 = np.arange(GRID_H)[:, None]
    r = np.arange(GRID_H)[None, :]
    out = np.zeros((GRID_W, 2 * GRID_H, 2 * GRID_H), np.float32)
    for w in range(GRID_W):
        m = (k1 * (GRID_W * r + w)) % SEQ
        ang = 2.0 * np.pi * m / SEQ
        c, s = np.cos(ang), np.sin(ang)
        out[w] = np.block([[c, s], [-s, c]])
    return out


@functools.lru_cache(maxsize=None)
def _col_dft():
    k2 = np.arange(GRID_W)
    ang = 2.0 * np.pi * ((k2[:, None] * k2[None, :]) % GRID_W) / GRID_W
    scale = 1.0 / math.sqrt(SEQ * FOURIER_GROUP)
    return (np.concatenate([np.cos(ang), np.sin(ang)], axis=1) * scale).astype(np.float32)


@functools.lru_cache(maxsize=None)
def _bias_index():
    c = np.arange(GRID_W)
    start = np.clip(c - NA_COLS // 2, 0, GRID_W - NA_COLS)
    valid = (c[None, :] >= start[:, None]) & (c[None, :] < start[:, None] + NA_COLS)
    dc = np.clip(c[None, :] - c[:, None] + (NA_COLS - 1), 0, 2 * NA_COLS - 2)
    return dc.astype(np.int32), valid


@functools.lru_cache(maxsize=None)
def _strict_lower(n):
    return np.tril(np.ones((n, n), np.float32), k=-1)


def _adaln_kernel(c_ref, w_ref, b_ref, o_ref):
    c = c_ref[...]
    s = c * jax.nn.sigmoid(c)
    o_ref[...] = jnp.dot(s.astype(BF16), w_ref[...].astype(BF16), preferred_element_type=F32) + b_ref[...]


def adaln(cond8, w, b):
    n = w.shape[1]
    tn = ADALN_TN
    return pl.pallas_call(
        _adaln_kernel,
        grid=(n // tn,),
        in_specs=[pl.BlockSpec((8, D_MODEL), lambda j: (0, 0)),
                  pl.BlockSpec((D_MODEL, tn), lambda j: (0, j)),
                  pl.BlockSpec((1, tn), lambda j: (0, j))],
        out_specs=pl.BlockSpec((8, tn), lambda j: (0, j)),
        out_shape=jax.ShapeDtypeStruct((8, n), F32),
        compiler_params=_cparams(("arbitrary",)),
        name="adaln",
    )(cond8, w, b.reshape(1, n))


def _norm_mod(x, g, shift, scale):
    ms = jnp.mean(x * x, axis=-1, keepdims=True)
    return (x * lax.rsqrt(ms + EPS) * g) * (1.0 + scale) + shift


def _in_proj_kernel(x_ref, mod_ref, g_ref, w_ref, cs_ref, cos_ref, s1_ref, s2_ref,
                    qr_ref, qp_ref, kr_ref, v_ref, a_ref, h_scr, a_scr):
    m = mod_ref[0]
    h_scr[...] = _norm_mod(x_ref[0], g_ref[...], m[0:1], m[1:2]).astype(BF16)
    cos, s1, s2 = cos_ref[...], s1_ref[...], s2_ref[...]

    def rope(t):
        return (t * cos + pltpu.roll(t, LANES - ROPE_HALF, axis=1) * s1
                + pltpu.roll(t, ROPE_HALF, axis=1) * s2)

    f = jnp.dot(h_scr[...], w_ref[:, 0:D_FOURIER], preferred_element_type=F32)
    a = jnp.dot(f.astype(BF16), cs_ref[...], preferred_element_type=F32)
    n_planes = 2 * D_FOURIER // LANES
    tile_rows = a.shape[0] // GRID_W
    for p in range(n_planes):
        for r in range(tile_rows):
            a_scr[p, r * ROW_PITCH:r * ROW_PITCH + GRID_W, :] = a[r * GRID_W:(r + 1) * GRID_W,
                                                                LANES * p:LANES * (p + 1)]
    for w in range(GRID_W):
        for p in range(n_planes):
            slab = a_scr[p, pl.ds(w, tile_rows, stride=ROW_PITCH), :]
            lo = w * D_FOURIER + (p % 2) * LANES
            a_ref[0, p // 2, :, lo:lo + LANES] = slab.astype(BF16)

    scale = HEAD_DIM ** -0.5 * LOG2E
    wide = 2 * LANES
    for c in range(D_NA // wide):
        lo = D_FOURIER + wide * c
        q = jnp.dot(h_scr[...], w_ref[:, lo:lo + wide], preferred_element_type=F32)
        k = jnp.dot(h_scr[...], w_ref[:, lo + D_NA:lo + D_NA + wide], preferred_element_type=F32)
        v = jnp.dot(h_scr[...], w_ref[:, lo + 2 * D_NA:lo + 2 * D_NA + wide], preferred_element_type=F32)
        v_ref[0, :, wide * c:wide * (c + 1)] = v.astype(BF16)
        for s in range(2):
            sl = slice(LANES * s, LANES * (s + 1))
            ol = slice(wide * c + LANES * s, wide * c + LANES * (s + 1))
            qs, ks = q[:, sl], k[:, sl]
            qp_ref[0, :, ol] = (qs * scale).astype(BF16)
            qr_ref[0, :, ol] = (rope(qs) * scale).astype(BF16)
            kr_ref[0, :, ol] = rope(ks).astype(BF16)


def in_proj(x, mods, g1, w_in_bf16):
    b, n, d = x.shape
    tm = TM_IN
    cos, s1, s2 = _rope_tables()
    cs = _mxu_const(_chan_dft())
    tok = lambda bi, i: (bi, i, 0)
    const2 = lambda bi, i: (0, 0)
    tab = pl.BlockSpec((tm, LANES), lambda bi, i: (i, 0))
    qkv_shape = jax.ShapeDtypeStruct((b, n, D_NA), BF16)
    qkv_spec = pl.BlockSpec((1, tm, D_NA), tok)
    return pl.pallas_call(
        _in_proj_kernel,
        grid=(b, n // tm),
        in_specs=[pl.BlockSpec((1, tm, d), tok),
                  pl.BlockSpec((1, 8, d), lambda bi, i: (bi, 0, 0)),
                  pl.BlockSpec((1, d), const2),
                  pl.BlockSpec((d, D_IN_PROJ), const2),
                  pl.BlockSpec((D_FOURIER, 2 * D_FOURIER), const2),
                  tab, tab, tab],
        out_specs=[qkv_spec, qkv_spec, qkv_spec, qkv_spec,
                   pl.BlockSpec((1, 2, tm // GRID_W, GRID_W * D_FOURIER), lambda bi, i: (bi, 0, i, 0))],
        out_shape=[qkv_shape, qkv_shape, qkv_shape, qkv_shape,
                   jax.ShapeDtypeStruct((b, 2, n // GRID_W, GRID_W * D_FOURIER), BF16)],
        scratch_shapes=[pltpu.VMEM((tm, d), BF16),
                        pltpu.VMEM((2 * D_FOURIER // LANES, (tm // GRID_W) * ROW_PITCH, LANES), F32)],
        compiler_params=_cparams(("arbitrary", "arbitrary")),
        name="in_proj",
    )(x, mods, g1.reshape(1, d), w_in_bf16, cs, jnp.asarray(cos), jnp.asarray(s1), jnp.asarray(s2))


def _ctx_proj_kernel(x_ref, mod_ref, g_ref, w_ref, k_ref, v_ref):
    m = mod_ref[...]
    h = _norm_mod(x_ref[0], g_ref[...], m[0:1], m[1:2]).astype(BF16)
    k_ref[0] = jnp.dot(h, w_ref[:, 0:D_NA], preferred_element_type=F32).astype(BF16)
    v_ref[0] = jnp.dot(h, w_ref[:, D_NA:2 * D_NA], preferred_element_type=F32).astype(BF16)


def ctx_proj(ctx, mod_ctx, g1, w_kv_bf16):
    b, l, d = ctx.shape
    shape = jax.ShapeDtypeStruct((b, l, D_NA), BF16)
    spec = pl.BlockSpec((1, l, D_NA), lambda bi: (bi, 0, 0))
    return pl.pallas_call(
        _ctx_proj_kernel,
        grid=(b,),
        in_specs=[pl.BlockSpec((1, l, d), lambda bi: (bi, 0, 0)),
                  pl.BlockSpec((8, d), lambda bi: (0, 0)),
                  pl.BlockSpec((1, d), lambda bi: (0, 0)),
                  pl.BlockSpec((d, 2 * D_NA), lambda bi: (0, 0))],
        out_specs=[spec, spec],
        out_shape=[shape, shape],
        compiler_params=_cparams(("arbitrary",)),
        name="ctx_proj",
    )(ctx, mod_ctx, g1.reshape(1, d), w_kv_bf16)


N_PLANES = D_FOURIER // LANES
K1_PITCH = GRID_H + 8
K2_PITCH = GRID_W + 8


def _dft_rows_kernel(a_ref, g_ref, z_ref, z_scr):
    for j in range(DFT_TW):
        sl = slice(D_FOURIER * j, D_FOURIER * (j + 1))
        rhs = jnp.concatenate([a_ref[0, 0, :, sl], a_ref[0, 1, :, sl]], axis=0)
        z = jnp.dot(g_ref[j], rhs, preferred_element_type=F32)
        for c in range(2):
            for p in range(N_PLANES):
                z_scr[c * N_PLANES + p, j * K1_PITCH:j * K1_PITCH + GRID_H, :] = (
                    z[c * GRID_H:(c + 1) * GRID_H, LANES * p:LANES * (p + 1)])
    for k1 in range(GRID_H):
        for c in range(2):
            for p in range(N_PLANES):
                slab = z_scr[c * N_PLANES + p, pl.ds(k1, DFT_TW, stride=K1_PITCH), :]
                lo = k1 * D_FOURIER + p * LANES
                z_ref[0, c, :, lo:lo + LANES] = slab.astype(BF16)


def dft_rows(a):
    b = a.shape[0]
    g = _mxu_const(_row_dft())
    return pl.pallas_call(
        _dft_rows_kernel,
        grid=(GRID_W // DFT_TW, b),
        in_specs=[pl.BlockSpec((1, 2, GRID_H, DFT_TW * D_FOURIER), lambda j, bi: (bi, 0, 0, j)),
                  pl.BlockSpec((DFT_TW, 2 * GRID_H, 2 * GRID_H), lambda j, bi: (j, 0, 0))],
        out_specs=pl.BlockSpec((1, 2, DFT_TW, GRID_H * D_FOURIER), lambda j, bi: (bi, 0, j, 0)),
        out_shape=jax.ShapeDtypeStruct((b, 2, GRID_W, GRID_H * D_FOURIER), BF16),
        scratch_shapes=[pltpu.VMEM((2 * N_PLANES, DFT_TW * K1_PITCH, LANES), F32)],
        compiler_params=_cparams(("arbitrary", "arbitrary")),
        name="dft_rows",
    )(a, g)


def _dft_cols_kernel(z_ref, cs_ref, o_ref, o_scr):
    for j in range(DFT_TK1):
        sl = slice(D_FOURIER * j, D_FOURIER * (j + 1))
        rhs = jnp.concatenate([z_ref[0, 0, :, sl], z_ref[0, 1, :, sl]], axis=0)
        out = jnp.dot(cs_ref[...], rhs, preferred_element_type=F32)
        for p in range(N_PLANES):
            o_scr[p, j * K2_PITCH:j * K2_PITCH + GRID_W, :] = out[:, LANES * p:LANES * (p + 1)]
    for k2 in range(GRID_W):
        for p in range(N_PLANES):
            slab = o_scr[p, pl.ds(k2, DFT_TK1, stride=K2_PITCH), :]
            o_ref[0, k2, :, LANES * p:LANES * (p + 1)] = slab.astype(BF16)


def dft_cols(z):
    b = z.shape[0]
    cs = _mxu_const(_col_dft())
    out = pl.pallas_call(
        _dft_cols_kernel,
        grid=(b, GRID_H // DFT_TK1),
        in_specs=[pl.BlockSpec((1, 2, GRID_W, DFT_TK1 * D_FOURIER), lambda bi, j: (bi, 0, 0, j)),
                  pl.BlockSpec((GRID_W, 2 * GRID_W), lambda bi, j: (0, 0))],
        out_specs=pl.BlockSpec((1, GRID_W, DFT_TK1, D_FOURIER), lambda bi, j: (bi, 0, j, 0)),
        out_shape=jax.ShapeDtypeStruct((b, GRID_W, GRID_H, D_FOURIER), BF16),
        scratch_shapes=[pltpu.VMEM((N_PLANES, DFT_TK1 * K2_PITCH, LANES), F32)],
        compiler_params=_cparams(("arbitrary", "arbitrary")),
        name="dft_cols",
    )(z, cs)
    return out.reshape(b, SEQ, D_FOURIER)


ROW_KEYS = NA_ROWS * GRID_W
ROW_TILES = ROW_KEYS // LANES
N_DR = 2 * NA_ROWS - 2
SOFT_ROWS = 32
ATT_PAIRS = ATT_ROWS // 2


def _attention_kernel(qr_ref, qp_ref, k_ref, v_ref, kc_ref, vc_ref, tb_ref, o_ref,
                      sc_scr, pc_scr, s_scr, p_scr):
    rb = pl.program_id(2)
    first1 = lax.broadcasted_iota(jnp.int32, (GRID_W, LANES), 1) < HEAD_DIM
    nt = (((1,), (1,)), ((), ()))

    def window(jp, i):
        r = rb * ATT_ROWS + 2 * jp + i
        rs = jnp.clip(r - NA_ROWS // 2, 0, GRID_H - NA_ROWS)
        return r, rs, pl.multiple_of(rs * GRID_W, GRID_W)

    def split_heads(q1):
        z1 = jnp.zeros_like(q1)
        return jnp.concatenate([jnp.where(first1, q1, z1), jnp.where(first1, z1, q1)], axis=0)

    def scores(jp):
        qp_rows = []
        for i in range(2):
            _, _, koff = window(jp, i)
            qs = slice((2 * jp + i) * GRID_W, (2 * jp + i + 1) * GRID_W)
            kw = k_ref[0, pl.ds(koff, ROW_KEYS), :]
            s_scr[jp, i * 2 * GRID_W:(i + 1) * 2 * GRID_W, :] = lax.dot_general(
                split_heads(qr_ref[0, qs, :]), kw, nt, preferred_element_type=F32)
            qp_rows.append(split_heads(qp_ref[0, qs, :]))
        sc_scr[jp] = lax.dot_general(jnp.concatenate(qp_rows, axis=0), kc_ref[0], nt,
                                     preferred_element_type=F32)

    def softmax(jp):
        for c in range(4):
            i, hh = c // 2, c % 2
            r, rs, _ = window(jp, i)
            d0 = rs - r + NA_ROWS - 1
            for h in range(GRID_W // SOFT_ROWS):
                lo = h * SOFT_ROWS
                rows = slice(c * GRID_W + lo, c * GRID_W + lo + SOFT_ROWS)
                tiles = [s_scr[jp, rows, t * LANES:(t + 1) * LANES]
                         + tb_ref[0, hh, d0 + 2 * t, lo:lo + SOFT_ROWS, :] for t in range(ROW_TILES)]
                sc = sc_scr[jp, rows, :]
                mt = jnp.maximum(sc[:, :LANES], sc[:, LANES:])
                for tl in tiles:
                    mt = jnp.maximum(mt, tl)
                m = jnp.max(mt, axis=1, keepdims=True)
                pc_scr[jp, rows, :] = jnp.exp2(sc - m).astype(BF16)
                for t, tl in enumerate(tiles):
                    p_scr[jp, rows, t * LANES:(t + 1) * LANES] = jnp.exp2(tl - m).astype(BF16)

    def weighted_values(jp):
        vc = jnp.concatenate([vc_ref[0], jnp.ones((CTX_LEN, LANES), BF16)], axis=1)
        oc = jnp.dot(pc_scr[jp], vc, preferred_element_type=F32)
        for i in range(2):
            _, _, koff = window(jp, i)
            rows = slice(i * 2 * GRID_W, (i + 1) * 2 * GRID_W)
            vw = jnp.concatenate([v_ref[0, pl.ds(koff, ROW_KEYS), :], jnp.ones((ROW_KEYS, LANES), BF16)],
                                 axis=1)
            o = jnp.dot(p_scr[jp, rows, :], vw, preferred_element_type=F32) + oc[rows]
            oa, ob = o[:GRID_W], o[GRID_W:]
            out = jnp.where(first1, oa[:, :LANES] / oa[:, LANES:], ob[:, :LANES] / ob[:, LANES:])
            o_ref[0, (2 * jp + i) * GRID_W:(2 * jp + i + 1) * GRID_W, :] = out.astype(BF16)

    scores(0)
    scores(1)
    softmax(0)
    for jp in range(ATT_PAIRS):
        if jp + 2 < ATT_PAIRS:
            scores(jp + 2)
        if jp + 1 < ATT_PAIRS:
            softmax(jp + 1)
        weighted_values(jp)


def bias_tables(rpb):
    dc, valid = _bias_index()
    n_dc = 2 * NA_COLS - 1
    onehot = (dc.reshape(1, -1) == np.arange(n_dc).reshape(-1, 1)).astype(np.float32)
    t = jnp.dot(rpb.reshape(-1, n_dc), jnp.asarray(onehot), precision=lax.Precision.HIGHEST)
    t = t.reshape(N_HEADS, 2 * NA_ROWS - 1, GRID_W, GRID_W)
    tb = pl.pallas_call(
        _bias_pairs_kernel,
        grid=(N_HEADS,),
        in_specs=[pl.BlockSpec((GRID_W, GRID_W), lambda h: (0, 0)),
                  pl.BlockSpec((1, N_DR + 1, GRID_W, GRID_W), lambda h: (h, 0, 0, 0))],
        out_specs=pl.BlockSpec((1, N_DR, GRID_W, LANES), lambda h: (h, 0, 0, 0)),
        out_shape=jax.ShapeDtypeStruct((N_HEADS, N_DR, GRID_W, LANES), F32),
        compiler_params=_cparams(("arbitrary",)),
        name="bias_pairs",
    )(jnp.asarray(valid.astype(np.float32)), t)
    return tb.reshape(N_PAIRS, 2, N_DR, GRID_W, LANES)


def _bias_pairs_kernel(valid_ref, t_ref, o_ref):
    inside = valid_ref[...] > 0.5
    rows = [jnp.where(inside, LOG2E * t_ref[0, d], NEG) for d in range(N_DR + 1)]
    for d in range(N_DR):
        o_ref[0, d] = jnp.concatenate([rows[d], rows[d + 1]], axis=1)


def attention(qr, qp, kr, v, kc, vc, tb):
    b, n, _ = qr.shape
    tq = ATT_ROWS * GRID_W
    qspec = pl.BlockSpec((1, tq, LANES), lambda bi, hp, i: (bi, i, hp))
    kspec = pl.BlockSpec((1, n, LANES), lambda bi, hp, i: (bi, 0, hp))
    cspec = pl.BlockSpec((1, CTX_LEN, LANES), lambda bi, hp, i: (bi, 0, hp))
    return pl.pallas_call(
        _attention_kernel,
        grid=(b, N_PAIRS, GRID_H // ATT_ROWS),
        in_specs=[qspec, qspec, kspec, kspec, cspec, cspec,
                  pl.BlockSpec((1, 2, N_DR, GRID_W, LANES), lambda bi, hp, i: (hp, 0, 0, 0, 0))],
        out_specs=qspec,
        out_shape=jax.ShapeDtypeStruct((b, n, D_NA), BF16),
        scratch_shapes=[pltpu.VMEM((ATT_PAIRS, 4 * GRID_W, CTX_LEN), F32),
                        pltpu.VMEM((ATT_PAIRS, 4 * GRID_W, CTX_LEN), BF16),
                        pltpu.VMEM((ATT_PAIRS, 4 * GRID_W, ROW_KEYS), F32),
                        pltpu.VMEM((ATT_PAIRS, 4 * GRID_W, ROW_KEYS), BF16)],
        compiler_params=_cparams(("arbitrary", "arbitrary", "arbitrary")),
        name="attention",
    )(qr, qp, kr, v, kc, vc, tb)


def _rms(x, g):
    ms = jnp.mean(x * x, axis=-1, keepdims=True)
    return x * lax.rsqrt(ms + EPS) * g


TILE_ROWS = D_MODEL // LANES


def _store_token_tiles(ref, val, start=0):
    rows = val.shape[0]
    for j in range(TILE_ROWS):
        ref[pl.ds(start + j, rows, stride=TILE_ROWS), :] = val[:, LANES * j:LANES * (j + 1)]


def _load_token_tiles(ref, start, rows):
    return jnp.concatenate(
        [ref[pl.ds(start + j, rows, stride=TILE_ROWS), :] for j in range(TILE_ROWS)], axis=1)


def _out_proj_kernel(fr_ref, na_ref, x_ref, mod_ref, wf_ref, wo_ref, go_ref, g2_ref, wr_ref, tri_ref,
                     x1_ref, h2_ref, meta_ref, metat_ref, cnt_ref, run_scr):
    @pl.when((pl.program_id(0) == 0) & (pl.program_id(1) == 0))
    def _():
        run_scr[...] = jnp.zeros_like(run_scr)

    m = mod_ref[0]
    go = go_ref[...]
    rows_c = OUT_CHUNK
    lane = lax.broadcasted_iota(jnp.int32, (rows_c, LANES), 1).astype(F32)
    ninf = jnp.float32(-jnp.inf)

    def argmax_first(vals):
        mx = jnp.max(vals, axis=1, keepdims=True)
        idx = jnp.min(jnp.where(vals == mx, lane, float(LANES)), axis=1, keepdims=True)
        return mx, idx

    run = run_scr[...]
    for c in range(x_ref.shape[1] // rows_c):
        rows = slice(c * rows_c, (c + 1) * rows_c)
        fo = jnp.dot(fr_ref[0, rows, :], wf_ref[...], preferred_element_type=F32)
        fn = _rms(fo, go[:, :D_FOURIER]).astype(BF16)
        nn = _rms(na_ref[0, rows, :].astype(F32), go[:, D_FOURIER:]).astype(BF16)
        y = (jnp.dot(fn, wo_ref[0:D_FOURIER, :], preferred_element_type=F32)
             + jnp.dot(nn, wo_ref[D_FOURIER:, :], preferred_element_type=F32))
        x1 = x_ref[0, rows, :] + m[2:3] * y
        x1_ref[0, rows, :] = x1
        h2 = _norm_mod(x1, g2_ref[...], m[3:4], m[4:5])
        _store_token_tiles(h2_ref, h2, c * rows_c * TILE_ROWS)
        logits = jnp.dot(h2.astype(BF16), wr_ref[...], preferred_element_type=F32)

        lg = jnp.where(lane < N_GROUPS, logits, ninf)
        gmax, gidx = argmax_first(lg)
        pg = 1.0 / jnp.sum(jnp.exp(lg - gmax), axis=1, keepdims=True)
        lo = N_GROUPS + EXPERTS_PER_GROUP * gidx
        le = jnp.where((lane >= lo) & (lane < lo + EXPERTS_PER_GROUP), logits, ninf)
        e1, i1 = argmax_first(le)
        e2, i2 = argmax_first(jnp.where(lane == i1, ninf, le))
        dd = jnp.exp(e2 - e1)
        gate1 = pg / (1.0 + dd)
        gate2 = pg * dd / (1.0 + dd)

        hot1 = lane == i1
        hot2 = lane == i2
        onehot = jnp.where(hot1 | hot2, 1.0, 0.0)
        cnt = jnp.dot(tri_ref[...], onehot.astype(BF16), preferred_element_type=F32) + run
        rank1 = jnp.sum(jnp.where(hot1, cnt, 0.0), axis=1, keepdims=True)
        rank2 = jnp.sum(jnp.where(hot2, cnt, 0.0), axis=1, keepdims=True)
        run = run + jnp.sum(onehot, axis=0, keepdims=True)

        meta = jnp.where(lane == 0, i1 - N_GROUPS,
               jnp.where(lane == 1, i2 - N_GROUPS,
               jnp.where(lane == 2, rank1,
               jnp.where(lane == 3, rank2,
               jnp.where(lane == 4, gate1,
               jnp.where(lane == 5, gate2, 0.0))))))
        meta_ref[rows, :] = meta
        metat_ref[:, rows] = jnp.transpose(meta)[0:8, :]
    run_scr[...] = run
    cnt_ref[...] = jnp.broadcast_to(run, cnt_ref.shape)


def out_proj(fr, na, x, mods, w_fmix_bf16, w_out_bf16, g_out, g2, w_router_bf16):
    b, n, d = x.shape
    tm = TM_PROJ
    steps = n // tm
    tok = lambda bi, i: (bi, i, 0)
    const2 = lambda bi, i: (0, 0)
    flat = lambda bi, i: (bi * steps + i, 0)
    tri = _mxu_const(_strict_lower(OUT_CHUNK))
    return pl.pallas_call(
        _out_proj_kernel,
        grid=(b, steps),
        in_specs=[pl.BlockSpec((1, tm, D_FOURIER), tok),
                  pl.BlockSpec((1, tm, D_NA), tok),
                  pl.BlockSpec((1, tm, d), tok),
                  pl.BlockSpec((1, 8, d), lambda bi, i: (bi, 0, 0)),
                  pl.BlockSpec((D_FOURIER, D_FOURIER), const2),
                  pl.BlockSpec((d, d), const2),
                  pl.BlockSpec((1, d), const2),
                  pl.BlockSpec((1, d), const2),
                  pl.BlockSpec((d, LANES), const2),
                  pl.BlockSpec((OUT_CHUNK, OUT_CHUNK), const2)],
        out_specs=[pl.BlockSpec((1, tm, d), tok),
                   pl.BlockSpec((tm * TILE_ROWS, LANES), flat),
                   pl.BlockSpec((tm, LANES), flat),
                   pl.BlockSpec((8, tm), lambda bi, i: (0, bi * steps + i)),
                   pl.BlockSpec((8, LANES), const2)],
        out_shape=[jax.ShapeDtypeStruct((b, n, d), F32),
                   jax.ShapeDtypeStruct((b * n * TILE_ROWS, LANES), F32),
                   jax.ShapeDtypeStruct((b * n, LANES), F32),
                   jax.ShapeDtypeStruct((8, b * n), F32),
                   jax.ShapeDtypeStruct((8, LANES), F32)],
        scratch_shapes=[pltpu.VMEM((1, LANES), F32)],
        compiler_params=_cparams(("arbitrary", "arbitrary")),
        name="out_proj",
    )(fr, na, x, mods, w_fmix_bf16, w_out_bf16, g_out.reshape(1, d), g2.reshape(1, d), w_router_bf16, tri)


def _dispatch_kernel(dest_ref, pstart_ref, count_ref, h2_ref, xs_hbm, zero_scr, sem, pad_sem):
    i = pl.program_id(0)
    nt = dest_ref.shape[0] // 2

    @pl.when(i == 0)
    def _():
        zero_scr[...] = jnp.zeros_like(zero_scr)

        def per_expert(e, npad):
            lo = pstart_ref[e] + count_ref[e]
            mid = pstart_ref[e] + ((count_ref[e] + MOE_CHUNK - 1) // MOE_CHUNK) * MOE_CHUNK
            hi = pstart_ref[e] + ((count_ref[e] + MOE_BLK - 1) // MOE_BLK) * MOE_BLK

            def fill(s, carry):
                pltpu.make_async_copy(zero_scr.at[pl.ds(0, TILE_ROWS)],
                                      xs_hbm.at[pl.ds(s * TILE_ROWS, TILE_ROWS)], pad_sem).start()
                return carry

            def fill_chunk(s, carry):
                pltpu.make_async_copy(zero_scr.at[pl.ds(0, MOE_CHUNK * TILE_ROWS)],
                                      xs_hbm.at[pl.ds(mid * TILE_ROWS + s * MOE_CHUNK * TILE_ROWS,
                                                      MOE_CHUNK * TILE_ROWS)], pad_sem).start()
                return carry

            lax.fori_loop(lo, mid, fill, 0)
            lax.fori_loop(0, (hi - mid) // MOE_CHUNK, fill_chunk, 0)
            return npad + (hi - lo)

        npad = lax.fori_loop(0, N_EXPERTS, per_expert, 0)

        blk_rows = MOE_BLK * TILE_ROWS
        first_free = (pstart_ref[N_EXPERTS - 1] + count_ref[N_EXPERTS - 1] + MOE_BLK - 1) // MOE_BLK
        n_blocks = xs_hbm.shape[0] // blk_rows

        def fill_block(bk, carry):
            pltpu.make_async_copy(zero_scr, xs_hbm.at[pl.ds(bk * blk_rows, blk_rows)], pad_sem).start()
            return carry

        lax.fori_loop(first_free, n_blocks, fill_block, 0)
        rows = npad * TILE_ROWS + (n_blocks - first_free) * blk_rows

        @pl.when(rows > 0)
        def _():
            pltpu.make_async_copy(xs_hbm.at[pl.ds(0, rows)], xs_hbm.at[pl.ds(0, rows)], pad_sem).wait()

    def issue(c, carry):
        t0 = c * DMA_CHUNK
        for u in range(DMA_CHUNK):
            for k in range(2):
                d = dest_ref[k * nt + i * TD_DISP + t0 + u]
                pltpu.make_async_copy(h2_ref.at[pl.ds((t0 + u) * TILE_ROWS, TILE_ROWS)],
                                      xs_hbm.at[pl.ds(d * TILE_ROWS, TILE_ROWS)], sem).start(priority=k)
        return carry

    lax.fori_loop(0, TD_DISP // DMA_CHUNK, issue, 0)
    rows = 2 * TD_DISP * TILE_ROWS
    pltpu.make_async_copy(xs_hbm.at[pl.ds(0, rows)], xs_hbm.at[pl.ds(0, rows)], sem).wait()


def dispatch(h2_tiles, dest, pstarts, counts, n_slots):
    nt = dest.shape[0] // 2
    grid_spec = pltpu.PrefetchScalarGridSpec(
        num_scalar_prefetch=3,
        grid=(nt // TD_DISP,),
        in_specs=[pl.BlockSpec((TD_DISP * TILE_ROWS, LANES), lambda i, ds, ps, ct: (i, 0))],
        out_specs=pl.BlockSpec(memory_space=pl.ANY),
        scratch_shapes=[pltpu.VMEM((MOE_BLK * TILE_ROWS, LANES), F32),
                        pltpu.SemaphoreType.DMA(()),
                        pltpu.SemaphoreType.DMA(())],
    )
    return pl.pallas_call(
        _dispatch_kernel,
        grid_spec=grid_spec,
        out_shape=jax.ShapeDtypeStruct((n_slots * TILE_ROWS, LANES), F32),
        compiler_params=_cparams(("arbitrary",)),
        name="dispatch",
    )(dest, pstarts, counts, h2_tiles)


def _experts_kernel(be_ref, nused_ref, valid_ref, xs_ref, wg_ref, wu_ref, wd_ref, ys_ref,
                    wg_scr, wu_scr, wd_scr):
    i = pl.program_id(0)
    valid = valid_ref[i]
    changed = (i == 0) | (be_ref[i] != be_ref[jnp.maximum(i - 1, 0)])
    chunk_rows = MOE_CHUNK * TILE_ROWS

    @pl.when(changed & (valid > 0))
    def _():
        wg_scr[...] = wg_ref[0].astype(BF16)
        wu_scr[...] = wu_ref[0].astype(BF16)
        wd_scr[...] = wd_ref[0].astype(BF16)

    def run(n_chunks):
        hmids = []
        for h in range(n_chunks):
            x = _load_token_tiles(xs_ref, h * chunk_rows, MOE_CHUNK).astype(BF16)
            g = jnp.dot(x, wg_scr[...], preferred_element_type=F32)
            u = jnp.dot(x, wu_scr[...], preferred_element_type=F32)
            hmids.append((g * jax.nn.sigmoid(g) * u).astype(BF16))
        for h, hmid in enumerate(hmids):
            _store_token_tiles(ys_ref, jnp.dot(hmid, wd_scr[...], preferred_element_type=F32), h * chunk_rows)
        if n_chunks * chunk_rows < ys_ref.shape[0]:
            ys_ref[n_chunks * chunk_rows:, :] = jnp.zeros((ys_ref.shape[0] - n_chunks * chunk_rows, LANES), F32)

    for n_chunks in range(MOE_BLK // MOE_CHUNK + 1):
        lo, hi = (n_chunks - 1) * MOE_CHUNK, n_chunks * MOE_CHUNK
        pl.when((valid > lo) & (valid <= hi))(functools.partial(run, n_chunks))


def experts(xs_tiles, block_expert, nused, block_valid, w_gate, w_up, w_down):
    d = D_MODEL
    nblk = block_expert.shape[0]
    blk_rows = MOE_BLK * TILE_ROWS
    wmap = lambda i, be, nu, bv: (be[i], 0, 0)
    grid_spec = pltpu.PrefetchScalarGridSpec(
        num_scalar_prefetch=3,
        grid=(nblk,),
        in_specs=[pl.BlockSpec((blk_rows, LANES), lambda i, be, nu, bv: (jnp.minimum(i, nu[0] - 1), 0)),
                  pl.BlockSpec((1, d, D_EXPERT), wmap),
                  pl.BlockSpec((1, d, D_EXPERT), wmap),
                  pl.BlockSpec((1, D_EXPERT, d), wmap)],
        out_specs=pl.BlockSpec((blk_rows, LANES), lambda i, be, nu, bv: (i, 0)),
        scratch_shapes=[pltpu.VMEM((d, D_EXPERT), BF16),
                        pltpu.VMEM((d, D_EXPERT), BF16),
                        pltpu.VMEM((D_EXPERT, d), BF16)],
    )
    return pl.pallas_call(
        _experts_kernel,
        grid_spec=grid_spec,
        out_shape=jax.ShapeDtypeStruct((nblk * blk_rows, LANES), F32),
        compiler_params=_cparams(("arbitrary",)),
        name="experts",
    )(block_expert, nused, block_valid, xs_tiles, w_gate, w_up, w_down)


def _combine_kernel(dest_ref, ys_hbm, x1_ref, meta_ref, mod_ref, gf_ref, o_ref, ybuf, sem):
    i = pl.program_id(0)
    nstep = pl.num_programs(0)
    tc = TC_COMB
    nt = dest_ref.shape[0] // 2
    half_rows = tc * TILE_ROWS
    buf_rows = 2 * half_rows

    def gather(step, slot):
        def issue(c, carry):
            t0 = c * DMA_CHUNK
            for u in range(DMA_CHUNK):
                for k in range(2):
                    d = dest_ref[k * nt + step * tc + t0 + u]
                    pltpu.make_async_copy(
                        ys_hbm.at[pl.ds(d * TILE_ROWS, TILE_ROWS)],
                        ybuf.at[pl.ds(slot * buf_rows + k * half_rows + (t0 + u) * TILE_ROWS, TILE_ROWS)],
                        sem.at[slot]).start(priority=k)
            return carry

        lax.fori_loop(0, tc // DMA_CHUNK, issue, 0)

    @pl.when(i == 0)
    def _():
        gather(0, 0)

    @pl.when(i + 1 < nstep)
    def _():
        gather(i + 1, (i + 1) % 2)

    slot = i % 2
    start = pl.multiple_of(slot * buf_rows, buf_rows)
    pltpu.make_async_copy(ys_hbm.at[pl.ds(0, buf_rows)], ybuf.at[pl.ds(start, buf_rows)], sem.at[slot]).wait()
    meta = meta_ref[...]
    y0 = _load_token_tiles(ybuf, start, tc)
    y1 = _load_token_tiles(ybuf, start + half_rows, tc)
    moe = y0 * meta[:, 4:5] + y1 * meta[:, 5:6]
    x2 = x1_ref[...] + mod_ref[0][5:6] * moe
    o_ref[...] = _rms(x2, gf_ref[...])


def combine(dest_flat, ys, x1_flat, meta, mods, g_final, n_per_batch):
    nt, d = x1_flat.shape
    tc = TC_COMB
    per_b = n_per_batch // tc
    grid_spec = pltpu.PrefetchScalarGridSpec(
        num_scalar_prefetch=1,
        grid=(nt // tc,),
        in_specs=[pl.BlockSpec(memory_space=pl.ANY),
                  pl.BlockSpec((tc, d), lambda i, ds: (i, 0)),
                  pl.BlockSpec((tc, LANES), lambda i, ds: (i, 0)),
                  pl.BlockSpec((1, 8, d), lambda i, ds: (i // per_b, 0, 0)),
                  pl.BlockSpec((1, d), lambda i, ds: (0, 0))],
        out_specs=pl.BlockSpec((tc, d), lambda i, ds: (i, 0)),
        scratch_shapes=[pltpu.VMEM((2 * 2 * tc * TILE_ROWS, LANES), F32),
                        pltpu.SemaphoreType.DMA((2,))],
    )
    return pl.pallas_call(
        _combine_kernel,
        grid_spec=grid_spec,
        out_shape=jax.ShapeDtypeStruct((nt, d), F32),
        compiler_params=_cparams(("arbitrary",)),
        name="combine",
    )(dest_flat, ys, x1_flat, meta, mods, g_final.reshape(1, d))


def _slots_kernel(pstart_ref, metat_ref, dest_ref):
    eid = metat_ref[0:2, :]
    slot = metat_ref[2:4, :]
    for e in range(N_EXPERTS):
        slot = slot + jnp.where(eid == float(e), pstart_ref[e].astype(F32), 0.0)
    dest_ref[...] = slot.astype(jnp.int32)


def _dispatch_plan(metat, counts_row, nt):
    counts = counts_row[N_GROUPS:N_GROUPS + N_EXPERTS].astype(jnp.int32)
    pcounts = ((counts + MOE_BLK - 1) // MOE_BLK) * MOE_BLK
    pends = jnp.cumsum(pcounts)
    pstarts = pends - pcounts
    dest = pl.pallas_call(
        _slots_kernel,
        grid_spec=pltpu.PrefetchScalarGridSpec(
            num_scalar_prefetch=1, grid=(1,),
            in_specs=[pl.BlockSpec(metat.shape, lambda i, ps: (0, 0))],
            out_specs=pl.BlockSpec((2, nt), lambda i, ps: (0, 0))),
        out_shape=jax.ShapeDtypeStruct((2, nt), jnp.int32),
        compiler_params=_cparams(("arbitrary",)),
        name="slots",
    )(pstarts.astype(jnp.int32), metat).reshape(-1)
    nblk = (nt * 2) // MOE_BLK + N_EXPERTS
    first_slot = jnp.arange(nblk, dtype=jnp.int32) * MOE_BLK
    block_expert = jnp.minimum(
        jnp.sum((pends[None, :] <= first_slot[:, None]).astype(jnp.int32), axis=1), N_EXPERTS - 1)
    nused = (pends[-1] // MOE_BLK).astype(jnp.int32).reshape(1)
    seg_end = jnp.sum(jnp.where(block_expert[:, None] == jnp.arange(N_EXPERTS)[None, :],
                                (pstarts + counts)[None, :], 0), axis=1)
    block_valid = jnp.where(first_slot < pends[-1], jnp.clip(seg_end - first_slot, 0, MOE_BLK), 0)
    return (dest, pstarts.astype(jnp.int32), counts, block_expert.astype(jnp.int32), nused,
            block_valid.astype(jnp.int32))


def kernel(x, c, ctx, c_ctx, w_ada, b_ada, g_norm1, w_in, w_fmix, rpb, g_out, w_out, g_norm2,
           w_router_group, w_router_expert, w_gate, w_up, w_down, g_final):
    b, n, d = x.shape
    assert (b, n, d) == (c.shape[0], SEQ, D_MODEL) and w_ada.shape[0] == 1
    nt = b * n

    cond8 = jnp.zeros((8, d), F32).at[0:b].set(c).at[b].set(c_ctx)
    mod = adaln(cond8, w_ada[0], b_ada[0])
    mods = jnp.pad(mod[0:b].reshape(b, N_MOD, d), ((0, 0), (0, 2), (0, 0)))
    mod_ctx = jnp.pad(mod[b].reshape(N_MOD, d), ((0, 2), (0, 0)))

    w_in_b = w_in[0].astype(BF16)
    qr, qp, kr, v, a = in_proj(x, mods, g_norm1[0], w_in_b)
    kc, vc = ctx_proj(ctx, mod_ctx, g_norm1[0], w_in_b[:, D_FOURIER + D_NA:])

    fr = dft_cols(dft_rows(a))
    na = attention(qr, qp, kr, v, kc, vc, bias_tables(rpb[0]))

    w_router = jnp.concatenate(
        [w_router_group[0], w_router_expert[0],
         jnp.zeros((d, LANES - N_GROUPS - N_EXPERTS), F32)], axis=1).astype(BF16)
    x1, h2_tiles, meta, metat, cnt = out_proj(fr, na, x, mods, w_fmix[0].astype(BF16),
                                              w_out[0].astype(BF16), g_out[0], g_norm2[0], w_router)

    dest, pstarts, counts, block_expert, nused, block_valid = _dispatch_plan(metat, cnt[0], nt)
    xs_tiles = dispatch(h2_tiles, dest, pstarts, counts, block_expert.shape[0] * MOE_BLK)
    ys_tiles = experts(xs_tiles, block_expert, nused, block_valid, w_gate[0], w_up[0], w_down[0])
    out = combine(dest, ys_tiles, x1.reshape(nt, d), meta, mods, g_final, n)
    return out.reshape(b, n, d)
```

```python
import functools
import math

import numpy as np
import jax
import jax.numpy as jnp
from jax import lax
from jax.experimental import pallas as pl
from jax.experimental.pallas import tpu as pltpu

F32 = jnp.float32
BF16 = jnp.bfloat16

D_MODEL = 1024
GRID_W = 64
GRID_H = 128
SEQ = GRID_W * GRID_H
CTX_LEN = 256
D_FOURIER = 256
FOURIER_GROUP = 64
HEAD_DIM = 64
N_HEADS = 12
D_NA = N_HEADS * HEAD_DIM
N_PAIRS = N_HEADS // 2
NA_ROWS = 8
NA_COLS = 16
ROPE_THETA = 10000.0
ROPE_CHUNK = HEAD_DIM // 2
ROPE_HALF = ROPE_CHUNK // 2
N_GROUPS = 4
EXPERTS_PER_GROUP = 8
N_EXPERTS = N_GROUPS * EXPERTS_PER_GROUP
D_EXPERT = 512
N_MOD = 6
D_IN_PROJ = D_FOURIER + 3 * D_NA
EPS = 1e-6
LANES = 128
NEG = -1e30
LOG2E = math.log2(math.e)

ADALN_TN = 1536
TM_IN = 1024
ROW_PITCH = 72
TM_PROJ = 1024
OUT_CHUNK = 512
ATT_ROWS = 64
DFT_TW = 16
DFT_TK1 = 16
MOE_BLK = 512
MOE_CHUNK = 256
TD_DISP = 2048
TC_COMB = 256
DMA_CHUNK = 8
VMEM_LIMIT = 56 * 1024 * 1024


def _cparams(sem):
    return pltpu.CompilerParams(dimension_semantics=sem, vmem_limit_bytes=VMEM_LIMIT)


def _mxu_const(table):
    return jnp.asarray(table, F32).astype(BF16)


@functools.lru_cache(maxsize=None)
def _rope_tables():
    t = np.arange(SEQ)
    row, col = t // GRID_W, t % GRID_W
    lane = np.arange(LANES)
    d = lane % HEAD_DIM
    chunk = d // ROPE_CHUNK
    e = d % ROPE_CHUNK
    j = e % ROPE_HALF
    inv = ROPE_THETA ** (-(j.astype(np.float64)) / ROPE_HALF)
    pos = np.where(chunk[None, :] == 0, row[:, None], col[:, None]).astype(np.float64)
    ang = pos * inv[None, :]
    cos = np.cos(ang)
    sin = np.sin(ang)
    first = (e < ROPE_HALF)[None, :]
    s_first = np.where(first, -sin, 0.0)
    s_second = np.where(first, 0.0, sin)
    return (cos.astype(np.float32), s_first.astype(np.float32), s_second.astype(np.float32))


@functools.lru_cache(maxsize=None)
def _chan_dft():
    c = np.arange(FOURIER_GROUP)
    ang = 2.0 * np.pi * ((c[:, None] * c[None, :]) % FOURIER_GROUP) / FOURIER_GROUP
    eye = np.eye(D_FOURIER // FOURIER_GROUP)
    re = np.kron(eye, np.cos(ang))
    im = np.kron(eye, -np.sin(ang))
    return np.concatenate([re, im], axis=1).astype(np.float32)


@functools.lru_cache(maxsize=None)
def _row_dft():
    k1 = np.arange(GRID_H)[:, None]
    r = np.arange(GRID_H)[None, :]
    out = np.zeros((GRID_W, 2 * GRID_H, 2 * GRID_H), np.float32)
    for w in range(GRID_W):
        m = (k1 * (GRID_W * r + w)) % SEQ
        ang = 2.0 * np.pi * m / SEQ
        c, s = np.cos(ang), np.sin(ang)
        out[w] = np.block([[c, s], [-s, c]])
    return out


@functools.lru_cache(maxsize=None)
def _col_dft():
    k2 = np.arange(GRID_W)
    ang = 2.0 * np.pi * ((k2[:, None] * k2[None, :]) % GRID_W) / GRID_W
    scale = 1.0 / math.sqrt(SEQ * FOURIER_GROUP)
    return (np.concatenate([np.cos(ang), np.sin(ang)], axis=1) * scale).astype(np.float32)


@functools.lru_cache(maxsize=None)
def _bias_index():
    c = np.arange(GRID_W)
    start = np.clip(c - NA_COLS // 2, 0, GRID_W - NA_COLS)
    valid = (c[None, :] >= start[:, None]) & (c[None, :] < start[:, None] + NA_COLS)
    dc = np.clip(c[None, :] - c[:, None] + (NA_COLS - 1), 0, 2 * NA_COLS - 2)
    return dc.astype(np.int32), valid


@functools.lru_cache(maxsize=None)
def _strict_lower(n):
    return np.tril(np.ones((n, n), np.float32), k=-1)


def _adaln_kernel(c_ref, w_ref, b_ref, o_ref):
    c = c_ref[...]
    s = c * jax.nn.sigmoid(c)
    o_ref[...] = jnp.dot(s.astype(BF16), w_ref[...].astype(BF16), preferred_element_type=F32) + b_ref[...]


def adaln(cond8, w, b):
    n = w.shape[1]
    tn = ADALN_TN
    return pl.pallas_call(
        _adaln_kernel,
        grid=(n // tn,),
        in_specs=[pl.BlockSpec((8, D_MODEL), lambda j: (0, 0)),
                  pl.BlockSpec((D_MODEL, tn), lambda j: (0, j)),
                  pl.BlockSpec((1, tn), lambda j: (0, j))],
        out_specs=pl.BlockSpec((8, tn), lambda j: (0, j)),
        out_shape=jax.ShapeDtypeStruct((8, n), F32),
        compiler_params=_cparams(("arbitrary",)),
        name="adaln",
    )(cond8, w, b.reshape(1, n))


def _norm_mod(x, g, shift, scale):
    ms = jnp.mean(x * x, axis=-1, keepdims=True)
    return (x * lax.rsqrt(ms + EPS) * g) * (1.0 + scale) + shift


def _in_proj_kernel(x_ref, mod_ref, g_ref, w_ref, cs_ref, cos_ref, s1_ref, s2_ref,
                    qr_ref, qp_ref, kr_ref, v_ref, a_ref, h_scr, a_scr):
    m = mod_ref[0]
    h_scr[...] = _norm_mod(x_ref[0], g_ref[...], m[0:1], m[1:2]).astype(BF16)
    cos, s1, s2 = cos_ref[...], s1_ref[...], s2_ref[...]

    def rope(t):
        return (t * cos + pltpu.roll(t, LANES - ROPE_HALF, axis=1) * s1
                + pltpu.roll(t, ROPE_HALF, axis=1) * s2)

    f = jnp.dot(h_scr[...], w_ref[:, 0:D_FOURIER], preferred_element_type=F32)
    a = jnp.dot(f.astype(BF16), cs_ref[...], preferred_element_type=F32)
    n_planes = 2 * D_FOURIER // LANES
    tile_rows = a.shape[0] // GRID_W
    for p in range(n_planes):
        for r in range(tile_rows):
            a_scr[p, r * ROW_PITCH:r * ROW_PITCH + GRID_W, :] = a[r * GRID_W:(r + 1) * GRID_W,
                                                                LANES * p:LANES * (p + 1)]
    for w in range(GRID_W):
        for p in range(n_planes):
            slab = a_scr[p, pl.ds(w, tile_rows, stride=ROW_PITCH), :]
            lo = w * D_FOURIER + (p % 2) * LANES
            a_ref[0, p // 2, :, lo:lo + LANES] = slab.astype(BF16)

    scale = HEAD_DIM ** -0.5 * LOG2E
    wide = 2 * LANES
    for c in range(D_NA // wide):
        lo = D_FOURIER + wide * c
        q = jnp.dot(h_scr[...], w_ref[:, lo:lo + wide], preferred_element_type=F32)
        k = jnp.dot(h_scr[...], w_ref[:, lo + D_NA:lo + D_NA + wide], preferred_element_type=F32)
        v = jnp.dot(h_scr[...], w_ref[:, lo + 2 * D_NA:lo + 2 * D_NA + wide], preferred_element_type=F32)
        v_ref[0, :, wide * c:wide * (c + 1)] = v.astype(BF16)
        for s in range(2):
            sl = slice(LANES * s, LANES * (s + 1))
            ol = slice(wide * c + LANES * s, wide * c + LANES * (s + 1))
            qs, ks = q[:, sl], k[:, sl]
            qp_ref[0, :, ol] = (qs * scale).astype(BF16)
            qr_ref[0, :, ol] = (rope(qs) * scale).astype(BF16)
            kr_ref[0, :, ol] = rope(ks).astype(BF16)


def in_proj(x, mods, g1, w_in_bf16):
    b, n, d = x.shape
    tm = TM_IN
    cos, s1, s2 = _rope_tables()
    cs = _mxu_const(_chan_dft())
    tok = lambda bi, i: (bi, i, 0)
    const2 = lambda bi, i: (0, 0)
    tab = pl.BlockSpec((tm, LANES), lambda bi, i: (i, 0))
    qkv_shape = jax.ShapeDtypeStruct((b, n, D_NA), BF16)
    qkv_spec = pl.BlockSpec((1, tm, D_NA), tok)
    return pl.pallas_call(
        _in_proj_kernel,
        grid=(b, n // tm),
        in_specs=[pl.BlockSpec((1, tm, d), tok),
                  pl.BlockSpec((1, 8, d), lambda bi, i: (bi, 0, 0)),
                  pl.BlockSpec((1, d), const2),
                  pl.BlockSpec((d, D_IN_PROJ), const2),
                  pl.BlockSpec((D_FOURIER, 2 * D_FOURIER), const2),
                  tab, tab, tab],
        out_specs=[qkv_spec, qkv_spec, qkv_spec, qkv_spec,
                   pl.BlockSpec((1, 2, tm // GRID_W, GRID_W * D_FOURIER), lambda bi, i: (bi, 0, i, 0))],
        out_shape=[qkv_shape, qkv_shape, qkv_shape, qkv_shape,
                   jax.ShapeDtypeStruct((b, 2, n // GRID_W, GRID_W * D_FOURIER), BF16)],
        scratch_shapes=[pltpu.VMEM((tm, d), BF16),
                        pltpu.VMEM((2 * D_FOURIER // LANES, (tm // GRID_W) * ROW_PITCH, LANES), F32)],
        compiler_params=_cparams(("arbitrary", "arbitrary")),
        name="in_proj",
    )(x, mods, g1.reshape(1, d), w_in_bf16, cs, jnp.asarray(cos), jnp.asarray(s1), jnp.asarray(s2))


def _ctx_proj_kernel(x_ref, mod_ref, g_ref, w_ref, k_ref, v_ref):
    m = mod_ref[...]
    h = _norm_mod(x_ref[0], g_ref[...], m[0:1], m[1:2]).astype(BF16)
    k_ref[0] = jnp.dot(h, w_ref[:, 0:D_NA], preferred_element_type=F32).astype(BF16)
    v_ref[0] = jnp.dot(h, w_ref[:, D_NA:2 * D_NA], preferred_element_type=F32).astype(BF16)


def ctx_proj(ctx, mod_ctx, g1, w_kv_bf16):
    b, l, d = ctx.shape
    shape = jax.ShapeDtypeStruct((b, l, D_NA), BF16)
    spec = pl.BlockSpec((1, l, D_NA), lambda bi: (bi, 0, 0))
    return pl.pallas_call(
        _ctx_proj_kernel,
        grid=(b,),
        in_specs=[pl.BlockSpec((1, l, d), lambda bi: (bi, 0, 0)),
                  pl.BlockSpec((8, d), lambda bi: (0, 0)),
                  pl.BlockSpec((1, d), lambda bi: (0, 0)),
                  pl.BlockSpec((d, 2 * D_NA), lambda bi: (0, 0))],
        out_specs=[spec, spec],
        out_shape=[shape, shape],
        compiler_params=_cparams(("arbitrary",)),
        name="ctx_proj",
    )(ctx, mod_ctx, g1.reshape(1, d), w_kv_bf16)


N_PLANES = D_FOURIER // LANES
K1_PITCH = GRID_H + 8
K2_PITCH = GRID_W + 8


def _dft_rows_kernel(a_ref, g_ref, z_ref, z_scr):
    for j in range(DFT_TW):
        sl = slice(D_FOURIER * j, D_FOURIER * (j + 1))
        rhs = jnp.concatenate([a_ref[0, 0, :, sl], a_ref[0, 1, :, sl]], axis=0)
        z = jnp.dot(g_ref[j], rhs, preferred_element_type=F32)
        for c in range(2):
            for p in range(N_PLANES):
                z_scr[c * N_PLANES + p, j * K1_PITCH:j * K1_PITCH + GRID_H, :] = (
                    z[c * GRID_H:(c + 1) * GRID_H, LANES * p:LANES * (p + 1)])
    for k1 in range(GRID_H):
        for c in range(2):
            for p in range(N_PLANES):
                slab = z_scr[c * N_PLANES + p, pl.ds(k1, DFT_TW, stride=K1_PITCH), :]
                lo = k1 * D_FOURIER + p * LANES
                z_ref[0, c, :, lo:lo + LANES] = slab.astype(BF16)


def dft_rows(a):
    b = a.shape[0]
    g = _mxu_const(_row_dft())
    return pl.pallas_call(
        _dft_rows_kernel,
        grid=(GRID_W // DFT_TW, b),
        in_specs=[pl.BlockSpec((1, 2, GRID_H, DFT_TW * D_FOURIER), lambda j, bi: (bi, 0, 0, j)),
                  pl.BlockSpec((DFT_TW, 2 * GRID_H, 2 * GRID_H), lambda j, bi: (j, 0, 0))],
        out_specs=pl.BlockSpec((1, 2, DFT_TW, GRID_H * D_FOURIER), lambda j, bi: (bi, 0, j, 0)),
        out_shape=jax.ShapeDtypeStruct((b, 2, GRID_W, GRID_H * D_FOURIER), BF16),
        scratch_shapes=[pltpu.VMEM((2 * N_PLANES, DFT_TW * K1_PITCH, LANES), F32)],
        compiler_params=_cparams(("arbitrary", "arbitrary")),
        name="dft_rows",
    )(a, g)


def _dft_cols_kernel(z_ref, cs_ref, o_ref, o_scr):
    for j in range(DFT_TK1):
        sl = slice(D_FOURIER * j, D_FOURIER * (j + 1))
        rhs = jnp.concatenate([z_ref[0, 0, :, sl], z_ref[0, 1, :, sl]], axis=0)
        out = jnp.dot(cs_ref[...], rhs, preferred_element_type=F32)
        for p in range(N_PLANES):
            o_scr[p, j * K2_PITCH:j * K2_PITCH + GRID_W, :] = out[:, LANES * p:LANES * (p + 1)]
    for k2 in range(GRID_W):
        for p in range(N_PLANES):
            slab = o_scr[p, pl.ds(k2, DFT_TK1, stride=K2_PITCH), :]
            o_ref[0, k2, :, LANES * p:LANES * (p + 1)] = slab.astype(BF16)


def dft_cols(z):
    b = z.shape[0]
    cs = _mxu_const(_col_dft())
    out = pl.pallas_call(
        _dft_cols_kernel,
        grid=(b, GRID_H // DFT_TK1),
        in_specs=[pl.BlockSpec((1, 2, GRID_W, DFT_TK1 * D_FOURIER), lambda bi, j: (bi, 0, 0, j)),
                  pl.BlockSpec((GRID_W, 2 * GRID_W), lambda bi, j: (0, 0))],
        out_specs=pl.BlockSpec((1, GRID_W, DFT_TK1, D_FOURIER), lambda bi, j: (bi, 0, j, 0)),
        out_shape=jax.ShapeDtypeStruct((b, GRID_W, GRID_H, D_FOURIER), BF16),
        scratch_shapes=[pltpu.VMEM((N_PLANES, DFT_TK1 * K2_PITCH, LANES), F32)],
        compiler_params=_cparams(("arbitrary", "arbitrary")),
        name="dft_cols",
    )(z, cs)
    return out.reshape(b, SEQ, D_FOURIER)


ROW_KEYS = NA_ROWS * GRID_W
ROW_TILES = ROW_KEYS // LANES
N_DR = 2 * NA_ROWS - 2
SOFT_ROWS = 32
ATT_PAIRS = ATT_ROWS // 2


def _attention_kernel(qr_ref, qp_ref, k_ref, v_ref, kc_ref, vc_ref, tb_ref, o_ref,
                      sc_scr, pc_scr, s_scr, p_scr):
    rb = pl.program_id(2)
    first1 = lax.broadcasted_iota(jnp.int32, (GRID_W, LANES), 1) < HEAD_DIM
    nt = (((1,), (1,)), ((), ()))

    def window(jp, i):
        r = rb * ATT_ROWS + 2 * jp + i
        rs = jnp.clip(r - NA_ROWS // 2, 0, GRID_H - NA_ROWS)
        return r, rs, pl.multiple_of(rs * GRID_W, GRID_W)

    def split_heads(q1):
        z1 = jnp.zeros_like(q1)
        return jnp.concatenate([jnp.where(first1, q1, z1), jnp.where(first1, z1, q1)], axis=0)

    def scores(jp):
        qp_rows = []
        for i in range(2):
            _, _, koff = window(jp, i)
            qs = slice((2 * jp + i) * GRID_W, (2 * jp + i + 1) * GRID_W)
            kw = k_ref[0, pl.ds(koff, ROW_KEYS), :]
            s_scr[jp, i * 2 * GRID_W:(i + 1) * 2 * GRID_W, :] = lax.dot_general(
                split_heads(qr_ref[0, qs, :]), kw, nt, preferred_element_type=F32)
            qp_rows.append(split_heads(qp_ref[0, qs, :]))
        sc_scr[jp] = lax.dot_general(jnp.concatenate(qp_rows, axis=0), kc_ref[0], nt,
                                     preferred_element_type=F32)

    def softmax(jp):
        for c in range(4):
            i, hh = c // 2, c % 2
            r, rs, _ = window(jp, i)
            d0 = rs - r + NA_ROWS - 1
            for h in range(GRID_W // SOFT_ROWS):
                lo = h * SOFT_ROWS
                rows = slice(c * GRID_W + lo, c * GRID_W + lo + SOFT_ROWS)
                tiles = [s_scr[jp, rows, t * LANES:(t + 1) * LANES]
                         + tb_ref[0, hh, d0 + 2 * t, lo:lo + SOFT_ROWS, :] for t in range(ROW_TILES)]
                sc = sc_scr[jp, rows, :]
                mt = jnp.maximum(sc[:, :LANES], sc[:, LANES:])
                for tl in tiles:
                    mt = jnp.maximum(mt, tl)
                m = jnp.max(mt, axis=1, keepdims=True)
                pc_scr[jp, rows, :] = jnp.exp2(sc - m).astype(BF16)
                for t, tl in enumerate(tiles):
                    p_scr[jp, rows, t * LANES:(t + 1) * LANES] = jnp.exp2(tl - m).astype(BF16)

    def weighted_values(jp):
        vc = jnp.concatenate([vc_ref[0], jnp.ones((CTX_LEN, LANES), BF16)], axis=1)
        oc = jnp.dot(pc_scr[jp], vc, preferred_element_type=F32)
        for i in range(2):
            _, _, koff = window(jp, i)
            rows = slice(i * 2 * GRID_W, (i + 1) * 2 * GRID_W)
            vw = jnp.concatenate([v_ref[0, pl.ds(koff, ROW_KEYS), :], jnp.ones((ROW_KEYS, LANES), BF16)],
                                 axis=1)
            o = jnp.dot(p_scr[jp, rows, :], vw, preferred_element_type=F32) + oc[rows]
            oa, ob = o[:GRID_W], o[GRID_W:]
            out = jnp.where(first1, oa[:, :LANES] / oa[:, LANES:], ob[:, :LANES] / ob[:, LANES:])
            o_ref[0, (2 * jp + i) * GRID_W:(2 * jp + i + 1) * GRID_W, :] = out.astype(BF16)

    scores(0)
    scores(1)
    softmax(0)
    for jp in range(ATT_PAIRS):
        if jp + 2 < ATT_PAIRS:
            scores(jp + 2)
        if jp + 1 < ATT_PAIRS:
            softmax(jp + 1)
        weighted_values(jp)


def bias_tables(rpb):
    dc, valid = _bias_index()
    n_dc = 2 * NA_COLS - 1
    onehot = (dc[None] == np.arange(n_dc).reshape(-1, 1, 1)).astype(np.float32)
    t = jnp.einsum("rj,jqk->rqk", rpb.reshape(-1, n_dc), jnp.asarray(onehot), precision=lax.Precision.HIGHEST)
    t = t.reshape(N_HEADS, 2 * NA_ROWS - 1, GRID_W, GRID_W)
    tb = pl.pallas_call(
        _bias_pairs_kernel,
        grid=(N_HEADS,),
        in_specs=[pl.BlockSpec((GRID_W, GRID_W), lambda h: (0, 0)),
                  pl.BlockSpec((1, N_DR + 1, GRID_W, GRID_W), lambda h: (h, 0, 0, 0))],
        out_specs=pl.BlockSpec((1, N_DR, GRID_W, LANES), lambda h: (h, 0, 0, 0)),
        out_shape=jax.ShapeDtypeStruct((N_HEADS, N_DR, GRID_W, LANES), F32),
        compiler_params=_cparams(("arbitrary",)),
        name="bias_pairs",
    )(jnp.asarray(valid.astype(np.float32)), t)
    return tb.reshape(N_PAIRS, 2, N_DR, GRID_W, LANES)


def _bias_pairs_kernel(valid_ref, t_ref, o_ref):
    inside = valid_ref[...] > 0.5
    rows = [jnp.where(inside, LOG2E * t_ref[0, d], NEG) for d in range(N_DR + 1)]
    for d in range(N_DR):
        o_ref[0, d] = jnp.concatenate([rows[d], rows[d + 1]], axis=1)


def attention(qr, qp, kr, v, kc, vc, tb):
    b, n, _ = qr.shape
    tq = ATT_ROWS * GRID_W
    qspec = pl.BlockSpec((1, tq, LANES), lambda bi, hp, i: (bi, i, hp))
    kspec = pl.BlockSpec((1, n, LANES), lambda bi, hp, i: (bi, 0, hp))
    cspec = pl.BlockSpec((1, CTX_LEN, LANES), lambda bi, hp, i: (bi, 0, hp))
    return pl.pallas_call(
        _attention_kernel,
        grid=(b, N_PAIRS, GRID_H // ATT_ROWS),
        in_specs=[qspec, qspec, kspec, kspec, cspec, cspec,
                  pl.BlockSpec((1, 2, N_DR, GRID_W, LANES), lambda bi, hp, i: (hp, 0, 0, 0, 0))],
        out_specs=qspec,
        out_shape=jax.ShapeDtypeStruct((b, n, D_NA), BF16),
        scratch_shapes=[pltpu.VMEM((ATT_PAIRS, 4 * GRID_W, CTX_LEN), F32),
                        pltpu.VMEM((ATT_PAIRS, 4 * GRID_W, CTX_LEN), BF16),
                        pltpu.VMEM((ATT_PAIRS, 4 * GRID_W, ROW_KEYS), F32),
                        pltpu.VMEM((ATT_PAIRS, 4 * GRID_W, ROW_KEYS), BF16)],
        compiler_params=_cparams(("arbitrary", "arbitrary", "arbitrary")),
        name="attention",
    )(qr, qp, kr, v, kc, vc, tb)


def _rms(x, g):
    ms = jnp.mean(x * x, axis=-1, keepdims=True)
    return x * lax.rsqrt(ms + EPS) * g


TILE_ROWS = D_MODEL // LANES


def _store_token_tiles(ref, val, start=0):
    rows = val.shape[0]
    for j in range(TILE_ROWS):
        ref[pl.ds(start + j, rows, stride=TILE_ROWS), :] = val[:, LANES * j:LANES * (j + 1)]


def _load_token_tiles(ref, start, rows):
    return jnp.concatenate(
        [ref[pl.ds(start + j, rows, stride=TILE_ROWS), :] for j in range(TILE_ROWS)], axis=1)


def _out_proj_kernel(fr_ref, na_ref, x_ref, mod_ref, wf_ref, wo_ref, go_ref, g2_ref, wr_ref, tri_ref,
                     x1_ref, h2_ref, meta_ref, metat_ref, cnt_ref, run_scr):
    @pl.when((pl.program_id(0) == 0) & (pl.program_id(1) == 0))
    def _():
        run_scr[...] = jnp.zeros_like(run_scr)

    m = mod_ref[0]
    go = go_ref[...]
    rows_c = OUT_CHUNK
    lane = lax.broadcasted_iota(jnp.int32, (rows_c, LANES), 1).astype(F32)
    ninf = jnp.float32(-jnp.inf)

    def argmax_first(vals):
        mx = jnp.max(vals, axis=1, keepdims=True)
        idx = jnp.min(jnp.where(vals == mx, lane, float(LANES)), axis=1, keepdims=True)
        return mx, idx

    run = run_scr[...]
    for c in range(x_ref.shape[1] // rows_c):
        rows = slice(c * rows_c, (c + 1) * rows_c)
        fo = jnp.dot(fr_ref[0, rows, :], wf_ref[...], preferred_element_type=F32)
        fn = _rms(fo, go[:, :D_FOURIER]).astype(BF16)
        nn = _rms(na_ref[0, rows, :].astype(F32), go[:, D_FOURIER:]).astype(BF16)
        y = (jnp.dot(fn, wo_ref[0:D_FOURIER, :], preferred_element_type=F32)
             + jnp.dot(nn, wo_ref[D_FOURIER:, :], preferred_element_type=F32))
        x1 = x_ref[0, rows, :] + m[2:3] * y
        x1_ref[0, rows, :] = x1
        h2 = _norm_mod(x1, g2_ref[...], m[3:4], m[4:5])
        _store_token_tiles(h2_ref, h2, c * rows_c * TILE_ROWS)
        logits = jnp.dot(h2.astype(BF16), wr_ref[...], preferred_element_type=F32)

        lg = jnp.where(lane < N_GROUPS, logits, ninf)
        gmax, gidx = argmax_first(lg)
        pg = 1.0 / jnp.sum(jnp.exp(lg - gmax), axis=1, keepdims=True)
        lo = N_GROUPS + EXPERTS_PER_GROUP * gidx
        le = jnp.where((lane >= lo) & (lane < lo + EXPERTS_PER_GROUP), logits, ninf)
        e1, i1 = argmax_first(le)
        e2, i2 = argmax_first(jnp.where(lane == i1, ninf, le))
        dd = jnp.exp(e2 - e1)
        gate1 = pg / (1.0 + dd)
        gate2 = pg * dd / (1.0 + dd)

        hot1 = lane == i1
        hot2 = lane == i2
        onehot = jnp.where(hot1 | hot2, 1.0, 0.0)
        cnt = jnp.dot(tri_ref[...], onehot.astype(BF16), preferred_element_type=F32) + run
        rank1 = jnp.sum(jnp.where(hot1, cnt, 0.0), axis=1, keepdims=True)
        rank2 = jnp.sum(jnp.where(hot2, cnt, 0.0), axis=1, keepdims=True)
        run = run + jnp.sum(onehot, axis=0, keepdims=True)

        meta = jnp.where(lane == 0, i1 - N_GROUPS,
               jnp.where(lane == 1, i2 - N_GROUPS,
               jnp.where(lane == 2, rank1,
               jnp.where(lane == 3, rank2,
               jnp.where(lane == 4, gate1,
               jnp.where(lane == 5, gate2, 0.0))))))
        meta_ref[rows, :] = meta
        metat_ref[:, rows] = jnp.transpose(meta)[0:8, :]
    run_scr[...] = run
    cnt_ref[...] = jnp.broadcast_to(run, cnt_ref.shape)


def out_proj(fr, na, x, mods, w_fmix_bf16, w_out_bf16, g_out, g2, w_router_bf16):
    b, n, d = x.shape
    tm = TM_PROJ
    steps = n // tm
    tok = lambda bi, i: (bi, i, 0)
    const2 = lambda bi, i: (0, 0)
    flat = lambda bi, i: (bi * steps + i, 0)
    tri = _mxu_const(_strict_lower(OUT_CHUNK))
    return pl.pallas_call(
        _out_proj_kernel,
        grid=(b, steps),
        in_specs=[pl.BlockSpec((1, tm, D_FOURIER), tok),
                  pl.BlockSpec((1, tm, D_NA), tok),
                  pl.BlockSpec((1, tm, d), tok),
                  pl.BlockSpec((1, 8, d), lambda bi, i: (bi, 0, 0)),
                  pl.BlockSpec((D_FOURIER, D_FOURIER), const2),
                  pl.BlockSpec((d, d), const2),
                  pl.BlockSpec((1, d), const2),
                  pl.BlockSpec((1, d), const2),
                  pl.BlockSpec((d, LANES), const2),
                  pl.BlockSpec((OUT_CHUNK, OUT_CHUNK), const2)],
        out_specs=[pl.BlockSpec((1, tm, d), tok),
                   pl.BlockSpec((tm * TILE_ROWS, LANES), flat),
                   pl.BlockSpec((tm, LANES), flat),
                   pl.BlockSpec((8, tm), lambda bi, i: (0, bi * steps + i)),
                   pl.BlockSpec((8, LANES), const2)],
        out_shape=[jax.ShapeDtypeStruct((b, n, d), F32),
                   jax.ShapeDtypeStruct((b * n * TILE_ROWS, LANES), F32),
                   jax.ShapeDtypeStruct((b * n, LANES), F32),
                   jax.ShapeDtypeStruct((8, b * n), F32),
                   jax.ShapeDtypeStruct((8, LANES), F32)],
        scratch_shapes=[pltpu.VMEM((1, LANES), F32)],
        compiler_params=_cparams(("arbitrary", "arbitrary")),
        name="out_proj",
    )(fr, na, x, mods, w_fmix_bf16, w_out_bf16, g_out.reshape(1, d), g2.reshape(1, d), w_router_bf16, tri)


def _dispatch_kernel(dest_ref, pstart_ref, count_ref, h2_ref, xs_hbm, zero_scr, sem, pad_sem):
    i = pl.program_id(0)
    nt = dest_ref.shape[0] // 2

    @pl.when(i == 0)
    def _():
        zero_scr[...] = jnp.zeros_like(zero_scr)

        def per_expert(e, npad):
            lo = pstart_ref[e] + count_ref[e]
            mid = pstart_ref[e] + ((count_ref[e] + MOE_CHUNK - 1) // MOE_CHUNK) * MOE_CHUNK
            hi = pstart_ref[e] + ((count_ref[e] + MOE_BLK - 1) // MOE_BLK) * MOE_BLK

            def fill(s, carry):
                pltpu.make_async_copy(zero_scr.at[pl.ds(0, TILE_ROWS)],
                                      xs_hbm.at[pl.ds(s * TILE_ROWS, TILE_ROWS)], pad_sem).start()
                return carry

            def fill_chunk(s, carry):
                pltpu.make_async_copy(zero_scr.at[pl.ds(0, MOE_CHUNK * TILE_ROWS)],
                                      xs_hbm.at[pl.ds(mid * TILE_ROWS + s * MOE_CHUNK * TILE_ROWS,
                                                      MOE_CHUNK * TILE_ROWS)], pad_sem).start()
                return carry

            lax.fori_loop(lo, mid, fill, 0)
            lax.fori_loop(0, (hi - mid) // MOE_CHUNK, fill_chunk, 0)
            return npad + (hi - lo)

        npad = lax.fori_loop(0, N_EXPERTS, per_expert, 0)

        blk_rows = MOE_BLK * TILE_ROWS
        first_free = (pstart_ref[N_EXPERTS - 1] + count_ref[N_EXPERTS - 1] + MOE_BLK - 1) // MOE_BLK
        n_blocks = xs_hbm.shape[0] // blk_rows

        def fill_block(bk, carry):
            pltpu.make_async_copy(zero_scr, xs_hbm.at[pl.ds(bk * blk_rows, blk_rows)], pad_sem).start()
            return carry

        lax.fori_loop(first_free, n_blocks, fill_block, 0)
        rows = npad * TILE_ROWS + (n_blocks - first_free) * blk_rows

        @pl.when(rows > 0)
        def _():
            pltpu.make_async_copy(xs_hbm.at[pl.ds(0, rows)], xs_hbm.at[pl.ds(0, rows)], pad_sem).wait()

    def issue(c, carry):
        t0 = c * DMA_CHUNK
        for u in range(DMA_CHUNK):
            for k in range(2):
                d = dest_ref[k * nt + i * TD_DISP + t0 + u]
                pltpu.make_async_copy(h2_ref.at[pl.ds((t0 + u) * TILE_ROWS, TILE_ROWS)],
                                      xs_hbm.at[pl.ds(d * TILE_ROWS, TILE_ROWS)], sem).start(priority=k)
        return carry

    lax.fori_loop(0, TD_DISP // DMA_CHUNK, issue, 0)
    rows = 2 * TD_DISP * TILE_ROWS
    pltpu.make_async_copy(xs_hbm.at[pl.ds(0, rows)], xs_hbm.at[pl.ds(0, rows)], sem).wait()


def dispatch(h2_tiles, dest, pstarts, counts, n_slots):
    nt = dest.shape[0] // 2
    grid_spec = pltpu.PrefetchScalarGridSpec(
        num_scalar_prefetch=3,
        grid=(nt // TD_DISP,),
        in_specs=[pl.BlockSpec((TD_DISP * TILE_ROWS, LANES), lambda i, ds, ps, ct: (i, 0))],
        out_specs=pl.BlockSpec(memory_space=pl.ANY),
        scratch_shapes=[pltpu.VMEM((MOE_BLK * TILE_ROWS, LANES), F32),
                        pltpu.SemaphoreType.DMA(()),
                        pltpu.SemaphoreType.DMA(())],
    )
    return pl.pallas_call(
        _dispatch_kernel,
        grid_spec=grid_spec,
        out_shape=jax.ShapeDtypeStruct((n_slots * TILE_ROWS, LANES), F32),
        compiler_params=_cparams(("arbitrary",)),
        name="dispatch",
    )(dest, pstarts, counts, h2_tiles)


def _experts_kernel(be_ref, nused_ref, valid_ref, xs_ref, wg_ref, wu_ref, wd_ref, ys_ref,
                    wg_scr, wu_scr, wd_scr):
    i = pl.program_id(0)
    valid = valid_ref[i]
    changed = (i == 0) | (be_ref[i] != be_ref[jnp.maximum(i - 1, 0)])
    chunk_rows = MOE_CHUNK * TILE_ROWS

    @pl.when(changed & (valid > 0))
    def _():
        wg_scr[...] = wg_ref[0].astype(BF16)
        wu_scr[...] = wu_ref[0].astype(BF16)
        wd_scr[...] = wd_ref[0].astype(BF16)

    def run(n_chunks):
        hmids = []
        for h in range(n_chunks):
            x = _load_token_tiles(xs_ref, h * chunk_rows, MOE_CHUNK).astype(BF16)
            g = jnp.dot(x, wg_scr[...], preferred_element_type=F32)
            u = jnp.dot(x, wu_scr[...], preferred_element_type=F32)
            hmids.append((g * jax.nn.sigmoid(g) * u).astype(BF16))
        for h, hmid in enumerate(hmids):
            _store_token_tiles(ys_ref, jnp.dot(hmid, wd_scr[...], preferred_element_type=F32), h * chunk_rows)
        if n_chunks * chunk_rows < ys_ref.shape[0]:
            ys_ref[n_chunks * chunk_rows:, :] = jnp.zeros((ys_ref.shape[0] - n_chunks * chunk_rows, LANES), F32)

    for n_chunks in range(MOE_BLK // MOE_CHUNK + 1):
        lo, hi = (n_chunks - 1) * MOE_CHUNK, n_chunks * MOE_CHUNK
        pl.when((valid > lo) & (valid <= hi))(functools.partial(run, n_chunks))


def experts(xs_tiles, block_expert, nused, block_valid, w_gate, w_up, w_down):
    d = D_MODEL
    nblk = block_expert.shape[0]
    blk_rows = MOE_BLK * TILE_ROWS
    wmap = lambda i, be, nu, bv: (be[i], 0, 0)
    grid_spec = pltpu.PrefetchScalarGridSpec(
        num_scalar_prefetch=3,
        grid=(nblk,),
        in_specs=[pl.BlockSpec((blk_rows, LANES), lambda i, be, nu, bv: (jnp.minimum(i, nu[0] - 1), 0)),
                  pl.BlockSpec((1, d, D_EXPERT), wmap),
                  pl.BlockSpec((1, d, D_EXPERT), wmap),
                  pl.BlockSpec((1, D_EXPERT, d), wmap)],
        out_specs=pl.BlockSpec((blk_rows, LANES), lambda i, be, nu, bv: (i, 0)),
        scratch_shapes=[pltpu.VMEM((d, D_EXPERT), BF16),
                        pltpu.VMEM((d, D_EXPERT), BF16),
                        pltpu.VMEM((D_EXPERT, d), BF16)],
    )
    return pl.pallas_call(
        _experts_kernel,
        grid_spec=grid_spec,
        out_shape=jax.ShapeDtypeStruct((nblk * blk_rows, LANES), F32),
        compiler_params=_cparams(("arbitrary",)),
        name="experts",
    )(block_expert, nused, block_valid, xs_tiles, w_gate, w_up, w_down)


def _combine_kernel(dest_ref, ys_hbm, x1_ref, meta_ref, mod_ref, gf_ref, o_ref, ybuf, sem):
    i = pl.program_id(0)
    nstep = pl.num_programs(0)
    tc = TC_COMB
    nt = dest_ref.shape[0] // 2
    half_rows = tc * TILE_ROWS
    buf_rows = 2 * half_rows

    def gather(step, slot):
        def issue(c, carry):
            t0 = c * DMA_CHUNK
            for u in range(DMA_CHUNK):
                for k in range(2):
                    d = dest_ref[k * nt + step * tc + t0 + u]
                    pltpu.make_async_copy(
                        ys_hbm.at[pl.ds(d * TILE_ROWS, TILE_ROWS)],
                        ybuf.at[pl.ds(slot * buf_rows + k * half_rows + (t0 + u) * TILE_ROWS, TILE_ROWS)],
                        sem.at[slot]).start(priority=k)
            return carry

        lax.fori_loop(0, tc // DMA_CHUNK, issue, 0)

    @pl.when(i == 0)
    def _():
        gather(0, 0)

    @pl.when(i + 1 < nstep)
    def _():
        gather(i + 1, (i + 1) % 2)

    slot = i % 2
    start = pl.multiple_of(slot * buf_rows, buf_rows)
    pltpu.make_async_copy(ys_hbm.at[pl.ds(0, buf_rows)], ybuf.at[pl.ds(start, buf_rows)], sem.at[slot]).wait()
    meta = meta_ref[...]
    y0 = _load_token_tiles(ybuf, start, tc)
    y1 = _load_token_tiles(ybuf, start + half_rows, tc)
    moe = y0 * meta[:, 4:5] + y1 * meta[:, 5:6]
    x2 = x1_ref[...] + mod_ref[0][5:6] * moe
    o_ref[...] = _rms(x2, gf_ref[...])


def combine(dest_flat, ys, x1_flat, meta, mods, g_final, n_per_batch):
    nt, d = x1_flat.shape
    tc = TC_COMB
    per_b = n_per_batch // tc
    grid_spec = pltpu.PrefetchScalarGridSpec(
        num_scalar_prefetch=1,
        grid=(nt // tc,),
        in_specs=[pl.BlockSpec(memory_space=pl.ANY),
                  pl.BlockSpec((tc, d), lambda i, ds: (i, 0)),
                  pl.BlockSpec((tc, LANES), lambda i, ds: (i, 0)),
                  pl.BlockSpec((1, 8, d), lambda i, ds: (i // per_b, 0, 0)),
                  pl.BlockSpec((1, d), lambda i, ds: (0, 0))],
        out_specs=pl.BlockSpec((tc, d), lambda i, ds: (i, 0)),
        scratch_shapes=[pltpu.VMEM((2 * 2 * tc * TILE_ROWS, LANES), F32),
                        pltpu.SemaphoreType.DMA((2,))],
    )
    return pl.pallas_call(
        _combine_kernel,
        grid_spec=grid_spec,
        out_shape=jax.ShapeDtypeStruct((nt, d), F32),
        compiler_params=_cparams(("arbitrary",)),
        name="combine",
    )(dest_flat, ys, x1_flat, meta, mods, g_final.reshape(1, d))


def _slots_kernel(pstart_ref, metat_ref, dest_ref):
    eid = metat_ref[0:2, :]
    slot = metat_ref[2:4, :]
    for e in range(N_EXPERTS):
        slot = slot + jnp.where(eid == float(e), pstart_ref[e].astype(F32), 0.0)
    dest_ref[...] = slot.astype(jnp.int32)


def _dispatch_plan(metat, counts_row, nt):
    counts = counts_row[N_GROUPS:N_GROUPS + N_EXPERTS].astype(jnp.int32)
    pcounts = ((counts + MOE_BLK - 1) // MOE_BLK) * MOE_BLK
    pends = jnp.cumsum(pcounts)
    pstarts = pends - pcounts
    dest = pl.pallas_call(
        _slots_kernel,
        grid_spec=pltpu.PrefetchScalarGridSpec(
            num_scalar_prefetch=1, grid=(1,),
            in_specs=[pl.BlockSpec(metat.shape, lambda i, ps: (0, 0))],
            out_specs=pl.BlockSpec((2, nt), lambda i, ps: (0, 0))),
        out_shape=jax.ShapeDtypeStruct((2, nt), jnp.int32),
        compiler_params=_cparams(("arbitrary",)),
        name="slots",
    )(pstarts.astype(jnp.int32), metat).reshape(-1)
    nblk = (nt * 2) // MOE_BLK + N_EXPERTS
    first_slot = jnp.arange(nblk, dtype=jnp.int32) * MOE_BLK
    block_expert = jnp.minimum(
        jnp.sum((pends[None, :] <= first_slot[:, None]).astype(jnp.int32), axis=1), N_EXPERTS - 1)
    nused = (pends[-1] // MOE_BLK).astype(jnp.int32).reshape(1)
    seg_end = jnp.sum(jnp.where(block_expert[:, None] == jnp.arange(N_EXPERTS)[None, :],
                                (pstarts + counts)[None, :], 0), axis=1)
    block_valid = jnp.where(first_slot < pends[-1], jnp.clip(seg_end - first_slot, 0, MOE_BLK), 0)
    return (dest, pstarts.astype(jnp.int32), counts, block_expert.astype(jnp.int32), nused,
            block_valid.astype(jnp.int32))


def kernel(x, c, ctx, c_ctx, w_ada, b_ada, g_norm1, w_in, w_fmix, rpb, g_out, w_out, g_norm2,
           w_router_group, w_router_expert, w_gate, w_up, w_down, g_final):
    b, n, d = x.shape
    assert (b, n, d) == (c.shape[0], SEQ, D_MODEL) and w_ada.shape[0] == 1
    nt = b * n

    cond8 = jnp.zeros((8, d), F32).at[0:b].set(c).at[b].set(c_ctx)
    mod = adaln(cond8, w_ada[0], b_ada[0])
    mods = jnp.pad(mod[0:b].reshape(b, N_MOD, d), ((0, 0), (0, 2), (0, 0)))
    mod_ctx = jnp.pad(mod[b].reshape(N_MOD, d), ((0, 2), (0, 0)))

    w_in_b = w_in[0].astype(BF16)
    qr, qp, kr, v, a = in_proj(x, mods, g_norm1[0], w_in_b)
    kc, vc = ctx_proj(ctx, mod_ctx, g_norm1[0], w_in_b[:, D_FOURIER + D_NA:])

    fr = dft_cols(dft_rows(a))
    na = attention(qr, qp, kr, v, kc, vc, bias_tables(rpb[0]))

    w_router = jnp.concatenate(
        [w_router_group[0], w_router_expert[0],
         jnp.zeros((d, LANES - N_GROUPS - N_EXPERTS), F32)], axis=1).astype(BF16)
    x1, h2_tiles, meta, metat, cnt = out_proj(fr, na, x, mods, w_fmix[0].astype(BF16),
                                              w_out[0].astype(BF16), g_out[0], g_norm2[0], w_router)

    dest, pstarts, counts, block_expert, nused, block_valid = _dispatch_plan(metat, cnt[0], nt)
    xs_tiles = dispatch(h2_tiles, dest, pstarts, counts, block_expert.shape[0] * MOE_BLK)
    ys_tiles = experts(xs_tiles, block_expert, nused, block_valid, w_gate[0], w_up[0], w_down[0])
    out = combine(dest, ys_tiles, x1.reshape(nt, d), meta, mods, g_final, n)
    return out.reshape(b, n, d)
```

```python
import functools
import math

import numpy as np
import jax
import jax.numpy as jnp
from jax import lax
from jax.experimental import pallas as pl
from jax.experimental.pallas import tpu as pltpu

F32 = jnp.float32
BF16 = jnp.bfloat16

D_MODEL = 1024
GRID_W = 64
GRID_H = 128
SEQ = GRID_W * GRID_H
CTX_LEN = 256
D_FOURIER = 256
FOURIER_GROUP = 64
HEAD_DIM = 64
N_HEADS = 12
D_NA = N_HEADS * HEAD_DIM
N_PAIRS = N_HEADS // 2
NA_ROWS = 8
NA_COLS = 16
ROPE_THETA = 10000.0
ROPE_CHUNK = HEAD_DIM // 2
ROPE_HALF = ROPE_CHUNK // 2
N_GROUPS = 4
EXPERTS_PER_GROUP = 8
N_EXPERTS = N_GROUPS * EXPERTS_PER_GROUP
D_EXPERT = 512
N_MOD = 6
D_IN_PROJ = D_FOURIER + 3 * D_NA
EPS = 1e-6
LANES = 128
NEG = -1e30
LOG2E = math.log2(math.e)

ADALN_TN = 1536
TM_IN = 1024
ROW_PITCH = 72
TM_PROJ = 1024
OUT_CHUNK = 512
ATT_ROWS = 64
DFT_TW = 32
DFT_TK1 = 32
MOE_BLK = 512
MOE_CHUNK = 256
TD_DISP = 2048
TC_COMB = 256
DMA_CHUNK = 8
VMEM_LIMIT = 56 * 1024 * 1024


def _cparams(sem):
    return pltpu.CompilerParams(dimension_semantics=sem, vmem_limit_bytes=VMEM_LIMIT)


def _mxu_const(table):
    return jnp.asarray(table, F32).astype(BF16)


@functools.lru_cache(maxsize=None)
def _rope_tables():
    t = np.arange(SEQ)
    row, col = t // GRID_W, t % GRID_W
    lane = np.arange(LANES)
    d = lane % HEAD_DIM
    chunk = d // ROPE_CHUNK
    e = d % ROPE_CHUNK
    j = e % ROPE_HALF
    inv = ROPE_THETA ** (-(j.astype(np.float64)) / ROPE_HALF)
    pos = np.where(chunk[None, :] == 0, row[:, None], col[:, None]).astype(np.float64)
    ang = pos * inv[None, :]
    cos = np.cos(ang)
    sin = np.sin(ang)
    first = (e < ROPE_HALF)[None, :]
    s_first = np.where(first, -sin, 0.0)
    s_second = np.where(first, 0.0, sin)
    return (cos.astype(np.float32), s_first.astype(np.float32), s_second.astype(np.float32))


@functools.lru_cache(maxsize=None)
def _chan_dft():
    c = np.arange(FOURIER_GROUP)
    ang = 2.0 * np.pi * ((c[:, None] * c[None, :]) % FOURIER_GROUP) / FOURIER_GROUP
    eye = np.eye(D_FOURIER // FOURIER_GROUP)
    re = np.kron(eye, np.cos(ang))
    im = np.kron(eye, -np.sin(ang))
    return np.concatenate([re, im], axis=1).astype(np.float32)


@functools.lru_cache(maxsize=None)
def _row_dft():
    k1 = np.arange(GRID_H)[:, None]
    r = np.arange(GRID_H)[None, :]
    out = np.zeros((GRID_W, 2 * GRID_H, 2 * GRID_H), np.float32)
    for w in range(GRID_W):
        m = (k1 * (GRID_W * r + w)) % SEQ
        ang = 2.0 * np.pi * m / SEQ
        c, s = np.cos(ang), np.sin(ang)
        out[w] = np.block([[c, s], [-s, c]])
    return out


@functools.lru_cache(maxsize=None)
def _col_dft():
    k2 = np.arange(GRID_W)
    ang = 2.0 * np.pi * ((k2[:, None] * k2[None, :]) % GRID_W) / GRID_W
    scale = 1.0 / math.sqrt(SEQ * FOURIER_GROUP)
    return (np.concatenate([np.cos(ang), np.sin(ang)], axis=1) * scale).astype(np.float32)


@functools.lru_cache(maxsize=None)
def _bias_index():
    c = np.arange(GRID_W)
    start = np.clip(c - NA_COLS // 2, 0, GRID_W - NA_COLS)
    valid = (c[None, :] >= start[:, None]) & (c[None, :] < start[:, None] + NA_COLS)
    dc = np.clip(c[None, :] - c[:, None] + (NA_COLS - 1), 0, 2 * NA_COLS - 2)
    return dc.astype(np.int32), valid


@functools.lru_cache(maxsize=None)
def _strict_lower(n):
    return np.tril(np.ones((n, n), np.float32), k=-1)


def _adaln_kernel(c_ref, w_ref, b_ref, o_ref):
    c = c_ref[...]
    s = c * jax.nn.sigmoid(c)
    o_ref[...] = jnp.dot(s.astype(BF16), w_ref[...].astype(BF16), preferred_element_type=F32) + b_ref[...]


def adaln(cond8, w, b):
    n = w.shape[1]
    tn = ADALN_TN
    return pl.pallas_call(
        _adaln_kernel,
        grid=(n // tn,),
        in_specs=[pl.BlockSpec((8, D_MODEL), lambda j: (0, 0)),
                  pl.BlockSpec((D_MODEL, tn), lambda j: (0, j)),
                  pl.BlockSpec((1, tn), lambda j: (0, j))],
        out_specs=pl.BlockSpec((8, tn), lambda j: (0, j)),
        out_shape=jax.ShapeDtypeStruct((8, n), F32),
        compiler_params=_cparams(("arbitrary",)),
        name="adaln",
    )(cond8, w, b.reshape(1, n))


def _norm_mod(x, g, shift, scale):
    ms = jnp.mean(x * x, axis=-1, keepdims=True)
    return (x * lax.rsqrt(ms + EPS) * g) * (1.0 + scale) + shift


def _in_proj_kernel(x_ref, mod_ref, g_ref, w_ref, cs_ref, cos_ref, s1_ref, s2_ref,
                    qr_ref, qp_ref, kr_ref, v_ref, a_ref, h_scr, a_scr):
    m = mod_ref[0]
    h_scr[...] = _norm_mod(x_ref[0], g_ref[...], m[0:1], m[1:2]).astype(BF16)
    cos, s1, s2 = cos_ref[...], s1_ref[...], s2_ref[...]

    def rope(t):
        return (t * cos + pltpu.roll(t, LANES - ROPE_HALF, axis=1) * s1
                + pltpu.roll(t, ROPE_HALF, axis=1) * s2)

    f = jnp.dot(h_scr[...], w_ref[:, 0:D_FOURIER], preferred_element_type=F32)
    a = jnp.dot(f.astype(BF16), cs_ref[...], preferred_element_type=F32)
    n_planes = 2 * D_FOURIER // LANES
    tile_rows = a.shape[0] // GRID_W
    for p in range(n_planes):
        for r in range(tile_rows):
            a_scr[p, r * ROW_PITCH:r * ROW_PITCH + GRID_W, :] = a[r * GRID_W:(r + 1) * GRID_W,
                                                                LANES * p:LANES * (p + 1)]
    for w in range(GRID_W):
        for p in range(n_planes):
            slab = a_scr[p, pl.ds(w, tile_rows, stride=ROW_PITCH), :]
            lo = w * D_FOURIER + (p % 2) * LANES
            a_ref[0, p // 2, :, lo:lo + LANES] = slab.astype(BF16)

    scale = HEAD_DIM ** -0.5 * LOG2E
    wide = 2 * LANES
    for c in range(D_NA // wide):
        lo = D_FOURIER + wide * c
        q = jnp.dot(h_scr[...], w_ref[:, lo:lo + wide], preferred_element_type=F32)
        k = jnp.dot(h_scr[...], w_ref[:, lo + D_NA:lo + D_NA + wide], preferred_element_type=F32)
        v = jnp.dot(h_scr[...], w_ref[:, lo + 2 * D_NA:lo + 2 * D_NA + wide], preferred_element_type=F32)
        v_ref[0, :, wide * c:wide * (c + 1)] = v.astype(BF16)
        for s in range(2):
            sl = slice(LANES * s, LANES * (s + 1))
            ol = slice(wide * c + LANES * s, wide * c + LANES * (s + 1))
            qs, ks = q[:, sl], k[:, sl]
            qp_ref[0, :, ol] = (qs * scale).astype(BF16)
            qr_ref[0, :, ol] = (rope(qs) * scale).astype(BF16)
            kr_ref[0, :, ol] = rope(ks).astype(BF16)


def in_proj(x, mods, g1, w_in_bf16):
    b, n, d = x.shape
    tm = TM_IN
    cos, s1, s2 = _rope_tables()
    cs = _mxu_const(_chan_dft())
    tok = lambda bi, i: (bi, i, 0)
    const2 = lambda bi, i: (0, 0)
    tab = pl.BlockSpec((tm, LANES), lambda bi, i: (i, 0))
    qkv_shape = jax.ShapeDtypeStruct((b, n, D_NA), BF16)
    qkv_spec = pl.BlockSpec((1, tm, D_NA), tok)
    return pl.pallas_call(
        _in_proj_kernel,
        grid=(b, n // tm),
        in_specs=[pl.BlockSpec((1, tm, d), tok),
                  pl.BlockSpec((1, 8, d), lambda bi, i: (bi, 0, 0)),
                  pl.BlockSpec((1, d), const2),
                  pl.BlockSpec((d, D_IN_PROJ), const2),
                  pl.BlockSpec((D_FOURIER, 2 * D_FOURIER), const2),
                  tab, tab, tab],
        out_specs=[qkv_spec, qkv_spec, qkv_spec, qkv_spec,
                   pl.BlockSpec((1, 2, tm // GRID_W, GRID_W * D_FOURIER), lambda bi, i: (bi, 0, i, 0))],
        out_shape=[qkv_shape, qkv_shape, qkv_shape, qkv_shape,
                   jax.ShapeDtypeStruct((b, 2, n // GRID_W, GRID_W * D_FOURIER), BF16)],
        scratch_shapes=[pltpu.VMEM((tm, d), BF16),
                        pltpu.VMEM((2 * D_FOURIER // LANES, (tm // GRID_W) * ROW_PITCH, LANES), F32)],
        compiler_params=_cparams(("arbitrary", "arbitrary")),
        name="in_proj",
    )(x, mods, g1.reshape(1, d), w_in_bf16, cs, jnp.asarray(cos), jnp.asarray(s1), jnp.asarray(s2))


def _ctx_proj_kernel(x_ref, mod_ref, g_ref, w_ref, k_ref, v_ref):
    m = mod_ref[...]
    h = _norm_mod(x_ref[0], g_ref[...], m[0:1], m[1:2]).astype(BF16)
    k_ref[0] = jnp.dot(h, w_ref[:, 0:D_NA], preferred_element_type=F32).astype(BF16)
    v_ref[0] = jnp.dot(h, w_ref[:, D_NA:2 * D_NA], preferred_element_type=F32).astype(BF16)


def ctx_proj(ctx, mod_ctx, g1, w_kv_bf16):
    b, l, d = ctx.shape
    shape = jax.ShapeDtypeStruct((b, l, D_NA), BF16)
    spec = pl.BlockSpec((1, l, D_NA), lambda bi: (bi, 0, 0))
    return pl.pallas_call(
        _ctx_proj_kernel,
        grid=(b,),
        in_specs=[pl.BlockSpec((1, l, d), lambda bi: (bi, 0, 0)),
                  pl.BlockSpec((8, d), lambda bi: (0, 0)),
                  pl.BlockSpec((1, d), lambda bi: (0, 0)),
                  pl.BlockSpec((d, 2 * D_NA), lambda bi: (0, 0))],
        out_specs=[spec, spec],
        out_shape=[shape, shape],
        compiler_params=_cparams(("arbitrary",)),
        name="ctx_proj",
    )(ctx, mod_ctx, g1.reshape(1, d), w_kv_bf16)


N_PLANES = D_FOURIER // LANES
K1_PITCH = GRID_H + 8
K2_PITCH = GRID_W + 8


def _dft_rows_kernel(a_ref, g_ref, z_ref, z_scr):
    for j in range(DFT_TW):
        sl = slice(D_FOURIER * j, D_FOURIER * (j + 1))
        rhs = jnp.concatenate([a_ref[0, 0, :, sl], a_ref[0, 1, :, sl]], axis=0)
        z = jnp.dot(g_ref[j], rhs, preferred_element_type=F32)
        for c in range(2):
            for p in range(N_PLANES):
                z_scr[c * N_PLANES + p, j * K1_PITCH:j * K1_PITCH + GRID_H, :] = (
                    z[c * GRID_H:(c + 1) * GRID_H, LANES * p:LANES * (p + 1)])
    for k1 in range(GRID_H):
        for c in range(2):
            for p in range(N_PLANES):
                slab = z_scr[c * N_PLANES + p, pl.ds(k1, DFT_TW, stride=K1_PITCH), :]
                lo = k1 * D_FOURIER + p * LANES
                z_ref[0, c, :, lo:lo + LANES] = slab.astype(BF16)


def dft_rows(a):
    b = a.shape[0]
    g = _mxu_const(_row_dft())
    return pl.pallas_call(
        _dft_rows_kernel,
        grid=(GRID_W // DFT_TW, b),
        in_specs=[pl.BlockSpec((1, 2, GRID_H, DFT_TW * D_FOURIER), lambda j, bi: (bi, 0, 0, j)),
                  pl.BlockSpec((DFT_TW, 2 * GRID_H, 2 * GRID_H), lambda j, bi: (j, 0, 0))],
        out_specs=pl.BlockSpec((1, 2, DFT_TW, GRID_H * D_FOURIER), lambda j, bi: (bi, 0, j, 0)),
        out_shape=jax.ShapeDtypeStruct((b, 2, GRID_W, GRID_H * D_FOURIER), BF16),
        scratch_shapes=[pltpu.VMEM((2 * N_PLANES, DFT_TW * K1_PITCH, LANES), F32)],
        compiler_params=_cparams(("arbitrary", "arbitrary")),
        name="dft_rows",
    )(a, g)


def _dft_cols_kernel(z_ref, cs_ref, o_ref, o_scr):
    for j in range(DFT_TK1):
        sl = slice(D_FOURIER * j, D_FOURIER * (j + 1))
        rhs = jnp.concatenate([z_ref[0, 0, :, sl], z_ref[0, 1, :, sl]], axis=0)
        out = jnp.dot(cs_ref[...], rhs, preferred_element_type=F32)
        for p in range(N_PLANES):
            o_scr[p, j * K2_PITCH:j * K2_PITCH + GRID_W, :] = out[:, LANES * p:LANES * (p + 1)]
    for k2 in range(GRID_W):
        for p in range(N_PLANES):
            slab = o_scr[p, pl.ds(k2, DFT_TK1, stride=K2_PITCH), :]
            o_ref[0, k2, :, LANES * p:LANES * (p + 1)] = slab.astype(BF16)


def dft_cols(z):
    b = z.shape[0]
    cs = _mxu_const(_col_dft())
    out = pl.pallas_call(
        _dft_cols_kernel,
        grid=(b, GRID_H // DFT_TK1),
        in_specs=[pl.BlockSpec((1, 2, GRID_W, DFT_TK1 * D_FOURIER), lambda bi, j: (bi, 0, 0, j)),
                  pl.BlockSpec((GRID_W, 2 * GRID_W), lambda bi, j: (0, 0))],
        out_specs=pl.BlockSpec((1, GRID_W, DFT_TK1, D_FOURIER), lambda bi, j: (bi, 0, j, 0)),
        out_shape=jax.ShapeDtypeStruct((b, GRID_W, GRID_H, D_FOURIER), BF16),
        scratch_shapes=[pltpu.VMEM((N_PLANES, DFT_TK1 * K2_PITCH, LANES), F32)],
        compiler_params=_cparams(("arbitrary", "arbitrary")),
        name="dft_cols",
    )(z, cs)
    return out.reshape(b, SEQ, D_FOURIER)


ROW_KEYS = NA_ROWS * GRID_W
ROW_TILES = ROW_KEYS // LANES
N_DR = 2 * NA_ROWS - 2
SOFT_ROWS = 32
ATT_PAIRS = ATT_ROWS // 2


def _attention_kernel(qr_ref, qp_ref, k_ref, v_ref, kc_ref, vc_ref, tb_ref, o_ref,
                      sc_scr, pc_scr, s_scr, p_scr):
    rb = pl.program_id(2)
    first1 = lax.broadcasted_iota(jnp.int32, (GRID_W, LANES), 1) < HEAD_DIM
    nt = (((1,), (1,)), ((), ()))

    def window(jp, i):
        r = rb * ATT_ROWS + 2 * jp + i
        rs = jnp.clip(r - NA_ROWS // 2, 0, GRID_H - NA_ROWS)
        return r, rs, pl.multiple_of(rs * GRID_W, GRID_W)

    def split_heads(q1):
        z1 = jnp.zeros_like(q1)
        return jnp.concatenate([jnp.where(first1, q1, z1), jnp.where(first1, z1, q1)], axis=0)

    def scores(jp):
        qp_rows = []
        for i in range(2):
            _, _, koff = window(jp, i)
            qs = slice((2 * jp + i) * GRID_W, (2 * jp + i + 1) * GRID_W)
            kw = k_ref[0, pl.ds(koff, ROW_KEYS), :]
            s_scr[jp, i * 2 * GRID_W:(i + 1) * 2 * GRID_W, :] = lax.dot_general(
                split_heads(qr_ref[0, qs, :]), kw, nt, preferred_element_type=F32)
            qp_rows.append(split_heads(qp_ref[0, qs, :]))
        sc_scr[jp] = lax.dot_general(jnp.concatenate(qp_rows, axis=0), kc_ref[0], nt,
                                     preferred_element_type=F32)

    def softmax(jp):
        for c in range(4):
            i, hh = c // 2, c % 2
            r, rs, _ = window(jp, i)
            d0 = rs - r + NA_ROWS - 1
            for h in range(GRID_W // SOFT_ROWS):
                lo = h * SOFT_ROWS
                rows = slice(c * GRID_W + lo, c * GRID_W + lo + SOFT_ROWS)
                tiles = [s_scr[jp, rows, t * LANES:(t + 1) * LANES]
                         + tb_ref[0, hh, d0 + 2 * t, lo:lo + SOFT_ROWS, :] for t in range(ROW_TILES)]
                sc = sc_scr[jp, rows, :]
                mt = jnp.maximum(sc[:, :LANES], sc[:, LANES:])
                for tl in tiles:
                    mt = jnp.maximum(mt, tl)
                m = jnp.max(mt, axis=1, keepdims=True)
                pc_scr[jp, rows, :] = jnp.exp2(sc - m).astype(BF16)
                for t, tl in enumerate(tiles):
                    p_scr[jp, rows, t * LANES:(t + 1) * LANES] = jnp.exp2(tl - m).astype(BF16)

    def weighted_values(jp):
        vc = jnp.concatenate([vc_ref[0], jnp.ones((CTX_LEN, LANES), BF16)], axis=1)
        oc = jnp.dot(pc_scr[jp], vc, preferred_element_type=F32)
        for i in range(2):
            _, _, koff = window(jp, i)
            rows = slice(i * 2 * GRID_W, (i + 1) * 2 * GRID_W)
            vw = jnp.concatenate([v_ref[0, pl.ds(koff, ROW_KEYS), :], jnp.ones((ROW_KEYS, LANES), BF16)],
                                 axis=1)
            o = jnp.dot(p_scr[jp, rows, :], vw, preferred_element_type=F32) + oc[rows]
            oa, ob = o[:GRID_W], o[GRID_W:]
            out = jnp.where(first1, oa[:, :LANES] / oa[:, LANES:], ob[:, :LANES] / ob[:, LANES:])
            o_ref[0, (2 * jp + i) * GRID_W:(2 * jp + i + 1) * GRID_W, :] = out.astype(BF16)

    scores(0)
    scores(1)
    softmax(0)
    for jp in range(ATT_PAIRS):
        if jp + 2 < ATT_PAIRS:
            scores(jp + 2)
        if jp + 1 < ATT_PAIRS:
            softmax(jp + 1)
        weighted_values(jp)


def bias_tables(rpb):
    dc, valid = _bias_index()
    n_dc = 2 * NA_COLS - 1
    onehot = (dc.reshape(1, -1) == np.arange(n_dc).reshape(-1, 1)).astype(np.float32)
    t = jnp.dot(rpb.reshape(-1, n_dc), jnp.asarray(onehot), precision=lax.Precision.HIGHEST)
    t = t.reshape(N_HEADS, 2 * NA_ROWS - 1, GRID_W, GRID_W)
    tb = pl.pallas_call(
        _bias_pairs_kernel,
        grid=(N_HEADS,),
        in_specs=[pl.BlockSpec((GRID_W, GRID_W), lambda h: (0, 0)),
                  pl.BlockSpec((1, N_DR + 1, GRID_W, GRID_W), lambda h: (h, 0, 0, 0))],
        out_specs=pl.BlockSpec((1, N_DR, GRID_W, LANES), lambda h: (h, 0, 0, 0)),
        out_shape=jax.ShapeDtypeStruct((N_HEADS, N_DR, GRID_W, LANES), F32),
        compiler_params=_cparams(("arbitrary",)),
        name="bias_pairs",
    )(jnp.asarray(valid.astype(np.float32)), t)
    return tb.reshape(N_PAIRS, 2, N_DR, GRID_W, LANES)


def _bias_pairs_kernel(valid_ref, t_ref, o_ref):
    inside = valid_ref[...] > 0.5
    rows = [jnp.where(inside, LOG2E * t_ref[0, d], NEG) for d in range(N_DR + 1)]
    for d in range(N_DR):
        o_ref[0, d] = jnp.concatenate([rows[d], rows[d + 1]], axis=1)


def attention(qr, qp, kr, v, kc, vc, tb):
    b, n, _ = qr.shape
    tq = ATT_ROWS * GRID_W
    qspec = pl.BlockSpec((1, tq, LANES), lambda bi, hp, i: (bi, i, hp))
    kspec = pl.BlockSpec((1, n, LANES), lambda bi, hp, i: (bi, 0, hp))
    cspec = pl.BlockSpec((1, CTX_LEN, LANES), lambda bi, hp, i: (bi, 0, hp))
    return pl.pallas_call(
        _attention_kernel,
        grid=(b, N_PAIRS, GRID_H // ATT_ROWS),
        in_specs=[qspec, qspec, kspec, kspec, cspec, cspec,
                  pl.BlockSpec((1, 2, N_DR, GRID_W, LANES), lambda bi, hp, i: (hp, 0, 0, 0, 0))],
        out_specs=qspec,
        out_shape=jax.ShapeDtypeStruct((b, n, D_NA), BF16),
        scratch_shapes=[pltpu.VMEM((ATT_PAIRS, 4 * GRID_W, CTX_LEN), F32),
                        pltpu.VMEM((ATT_PAIRS, 4 * GRID_W, CTX_LEN), BF16),
                        pltpu.VMEM((ATT_PAIRS, 4 * GRID_W, ROW_KEYS), F32),
                        pltpu.VMEM((ATT_PAIRS, 4 * GRID_W, ROW_KEYS), BF16)],
        compiler_params=_cparams(("arbitrary", "arbitrary", "arbitrary")),
        name="attention",
    )(qr, qp, kr, v, kc, vc, tb)


def _rms(x, g):
    ms = jnp.mean(x * x, axis=-1, keepdims=True)
    return x * lax.rsqrt(ms + EPS) * g


TILE_ROWS = D_MODEL // LANES


def _store_token_tiles(ref, val, start=0):
    rows = val.shape[0]
    for j in range(TILE_ROWS):
        ref[pl.ds(start + j, rows, stride=TILE_ROWS), :] = val[:, LANES * j:LANES * (j + 1)]


def _load_token_tiles(ref, start, rows):
    return jnp.concatenate(
        [ref[pl.ds(start + j, rows, stride=TILE_ROWS), :] for j in range(TILE_ROWS)], axis=1)


def _out_proj_kernel(fr_ref, na_ref, x_ref, mod_ref, wf_ref, wo_ref, go_ref, g2_ref, wr_ref, tri_ref,
                     x1_ref, h2_ref, meta_ref, metat_ref, cnt_ref, run_scr):
    @pl.when((pl.program_id(0) == 0) & (pl.program_id(1) == 0))
    def _():
        run_scr[...] = jnp.zeros_like(run_scr)

    m = mod_ref[0]
    go = go_ref[...]
    rows_c = OUT_CHUNK
    lane = lax.broadcasted_iota(jnp.int32, (rows_c, LANES), 1).astype(F32)
    ninf = jnp.float32(-jnp.inf)

    def argmax_first(vals):
        mx = jnp.max(vals, axis=1, keepdims=True)
        idx = jnp.min(jnp.where(vals == mx, lane, float(LANES)), axis=1, keepdims=True)
        return mx, idx

    run = run_scr[...]
    for c in range(x_ref.shape[1] // rows_c):
        rows = slice(c * rows_c, (c + 1) * rows_c)
        fo = jnp.dot(fr_ref[0, rows, :], wf_ref[...], preferred_element_type=F32)
        fn = _rms(fo, go[:, :D_FOURIER]).astype(BF16)
        nn = _rms(na_ref[0, rows, :].astype(F32), go[:, D_FOURIER:]).astype(BF16)
        y = (jnp.dot(fn, wo_ref[0:D_FOURIER, :], preferred_element_type=F32)
             + jnp.dot(nn, wo_ref[D_FOURIER:, :], preferred_element_type=F32))
        x1 = x_ref[0, rows, :] + m[2:3] * y
        x1_ref[0, rows, :] = x1
        h2 = _norm_mod(x1, g2_ref[...], m[3:4], m[4:5])
        _store_token_tiles(h2_ref, h2, c * rows_c * TILE_ROWS)
        logits = jnp.dot(h2.astype(BF16), wr_ref[...], preferred_element_type=F32)

        lg = jnp.where(lane < N_GROUPS, logits, ninf)
        gmax, gidx = argmax_first(lg)
        pg = 1.0 / jnp.sum(jnp.exp(lg - gmax), axis=1, keepdims=True)
        lo = N_GROUPS + EXPERTS_PER_GROUP * gidx
        le = jnp.where((lane >= lo) & (lane < lo + EXPERTS_PER_GROUP), logits, ninf)
        e1, i1 = argmax_first(le)
        e2, i2 = argmax_first(jnp.where(lane == i1, ninf, le))
        dd = jnp.exp(e2 - e1)
        gate1 = pg / (1.0 + dd)
        gate2 = pg * dd / (1.0 + dd)

        hot1 = lane == i1
        hot2 = lane == i2
        onehot = jnp.where(hot1 | hot2, 1.0, 0.0)
        cnt = jnp.dot(tri_ref[...], onehot.astype(BF16), preferred_element_type=F32) + run
        rank1 = jnp.sum(jnp.where(hot1, cnt, 0.0), axis=1, keepdims=True)
        rank2 = jnp.sum(jnp.where(hot2, cnt, 0.0), axis=1, keepdims=True)
        run = run + jnp.sum(onehot, axis=0, keepdims=True)

        meta = jnp.where(lane == 0, i1 - N_GROUPS,
               jnp.where(lane == 1, i2 - N_GROUPS,
               jnp.where(lane == 2, rank1,
               jnp.where(lane == 3, rank2,
               jnp.where(lane == 4, gate1,
               jnp.where(lane == 5, gate2, 0.0))))))
        meta_ref[rows, :] = meta
        metat_ref[:, rows] = jnp.transpose(meta)[0:8, :]
    run_scr[...] = run
    cnt_ref[...] = jnp.broadcast_to(run, cnt_ref.shape)


def out_proj(fr, na, x, mods, w_fmix_bf16, w_out_bf16, g_out, g2, w_router_bf16):
    b, n, d = x.shape
    tm = TM_PROJ
    steps = n // tm
    tok = lambda bi, i: (bi, i, 0)
    const2 = lambda bi, i: (0, 0)
    flat = lambda bi, i: (bi * steps + i, 0)
    tri = _mxu_const(_strict_lower(OUT_CHUNK))
    return pl.pallas_call(
        _out_proj_kernel,
        grid=(b, steps),
        in_specs=[pl.BlockSpec((1, tm, D_FOURIER), tok),
                  pl.BlockSpec((1, tm, D_NA), tok),
                  pl.BlockSpec((1, tm, d), tok),
                  pl.BlockSpec((1, 8, d), lambda bi, i: (bi, 0, 0)),
                  pl.BlockSpec((D_FOURIER, D_FOURIER), const2),
                  pl.BlockSpec((d, d), const2),
                  pl.BlockSpec((1, d), const2),
                  pl.BlockSpec((1, d), const2),
                  pl.BlockSpec((d, LANES), const2),
                  pl.BlockSpec((OUT_CHUNK, OUT_CHUNK), const2)],
        out_specs=[pl.BlockSpec((1, tm, d), tok),
                   pl.BlockSpec((tm * TILE_ROWS, LANES), flat),
                   pl.BlockSpec((tm, LANES), flat),
                   pl.BlockSpec((8, tm), lambda bi, i: (0, bi * steps + i)),
                   pl.BlockSpec((8, LANES), const2)],
        out_shape=[jax.ShapeDtypeStruct((b, n, d), F32),
                   jax.ShapeDtypeStruct((b * n * TILE_ROWS, LANES), F32),
                   jax.ShapeDtypeStruct((b * n, LANES), F32),
                   jax.ShapeDtypeStruct((8, b * n), F32),
                   jax.ShapeDtypeStruct((8, LANES), F32)],
        scratch_shapes=[pltpu.VMEM((1, LANES), F32)],
        compiler_params=_cparams(("arbitrary", "arbitrary")),
        name="out_proj",
    )(fr, na, x, mods, w_fmix_bf16, w_out_bf16, g_out.reshape(1, d), g2.reshape(1, d), w_router_bf16, tri)


def _dispatch_kernel(dest_ref, pstart_ref, count_ref, h2_ref, xs_hbm, zero_scr, sem, pad_sem):
    i = pl.program_id(0)
    nt = dest_ref.shape[0] // 2

    @pl.when(i == 0)
    def _():
        zero_scr[...] = jnp.zeros_like(zero_scr)

        def per_expert(e, npad):
            lo = pstart_ref[e] + count_ref[e]
            mid = pstart_ref[e] + ((count_ref[e] + MOE_CHUNK - 1) // MOE_CHUNK) * MOE_CHUNK
            hi = pstart_ref[e] + ((count_ref[e] + MOE_BLK - 1) // MOE_BLK) * MOE_BLK

            def fill(s, carry):
                pltpu.make_async_copy(zero_scr.at[pl.ds(0, TILE_ROWS)],
                                      xs_hbm.at[pl.ds(s * TILE_ROWS, TILE_ROWS)], pad_sem).start()
                return carry

            def fill_chunk(s, carry):
                pltpu.make_async_copy(zero_scr.at[pl.ds(0, MOE_CHUNK * TILE_ROWS)],
                                      xs_hbm.at[pl.ds(mid * TILE_ROWS + s * MOE_CHUNK * TILE_ROWS,
                                                      MOE_CHUNK * TILE_ROWS)], pad_sem).start()
                return carry

            lax.fori_loop(lo, mid, fill, 0)
            lax.fori_loop(0, (hi - mid) // MOE_CHUNK, fill_chunk, 0)
            return npad + (hi - lo)

        npad = lax.fori_loop(0, N_EXPERTS, per_expert, 0)

        blk_rows = MOE_BLK * TILE_ROWS
        first_free = (pstart_ref[N_EXPERTS - 1] + count_ref[N_EXPERTS - 1] + MOE_BLK - 1) // MOE_BLK
        n_blocks = xs_hbm.shape[0] // blk_rows

        def fill_block(bk, carry):
            pltpu.make_async_copy(zero_scr, xs_hbm.at[pl.ds(bk * blk_rows, blk_rows)], pad_sem).start()
            return carry

        lax.fori_loop(first_free, n_blocks, fill_block, 0)
        rows = npad * TILE_ROWS + (n_blocks - first_free) * blk_rows

        @pl.when(rows > 0)
        def _():
            pltpu.make_async_copy(xs_hbm.at[pl.ds(0, rows)], xs_hbm.at[pl.ds(0, rows)], pad_sem).wait()

    def issue(c, carry):
        t0 = c * DMA_CHUNK
        for u in range(DMA_CHUNK):
            for k in range(2):
                d = dest_ref[k * nt + i * TD_DISP + t0 + u]
                pltpu.make_async_copy(h2_ref.at[pl.ds((t0 + u) * TILE_ROWS, TILE_ROWS)],
                                      xs_hbm.at[pl.ds(d * TILE_ROWS, TILE_ROWS)], sem).start(priority=k)
        return carry

    lax.fori_loop(0, TD_DISP // DMA_CHUNK, issue, 0)
    rows = 2 * TD_DISP * TILE_ROWS
    pltpu.make_async_copy(xs_hbm.at[pl.ds(0, rows)], xs_hbm.at[pl.ds(0, rows)], sem).wait()


def dispatch(h2_tiles, dest, pstarts, counts, n_slots):
    nt = dest.shape[0] // 2
    grid_spec = pltpu.PrefetchScalarGridSpec(
        num_scalar_prefetch=3,
        grid=(nt // TD_DISP,),
        in_specs=[pl.BlockSpec((TD_DISP * TILE_ROWS, LANES), lambda i, ds, ps, ct: (i, 0))],
        out_specs=pl.BlockSpec(memory_space=pl.ANY),
        scratch_shapes=[pltpu.VMEM((MOE_BLK * TILE_ROWS, LANES), F32),
                        pltpu.SemaphoreType.DMA(()),
                        pltpu.SemaphoreType.DMA(())],
    )
    return pl.pallas_call(
        _dispatch_kernel,
        grid_spec=grid_spec,
        out_shape=jax.ShapeDtypeStruct((n_slots * TILE_ROWS, LANES), F32),
        compiler_params=_cparams(("arbitrary",)),
        name="dispatch",
    )(dest, pstarts, counts, h2_tiles)


def _experts_kernel(be_ref, nused_ref, valid_ref, xs_ref, wg_ref, wu_ref, wd_ref, ys_ref,
                    wg_scr, wu_scr, wd_scr):
    i = pl.program_id(0)
    valid = valid_ref[i]
    changed = (i == 0) | (be_ref[i] != be_ref[jnp.maximum(i - 1, 0)])
    chunk_rows = MOE_CHUNK * TILE_ROWS

    @pl.when(changed & (valid > 0))
    def _():
        wg_scr[...] = wg_ref[0].astype(BF16)
        wu_scr[...] = wu_ref[0].astype(BF16)
        wd_scr[...] = wd_ref[0].astype(BF16)

    def run(n_chunks):
        hmids = []
        for h in range(n_chunks):
            x = _load_token_tiles(xs_ref, h * chunk_rows, MOE_CHUNK).astype(BF16)
            g = jnp.dot(x, wg_scr[...], preferred_element_type=F32)
            u = jnp.dot(x, wu_scr[...], preferred_element_type=F32)
            hmids.append((g * jax.nn.sigmoid(g) * u).astype(BF16))
        for h, hmid in enumerate(hmids):
            _store_token_tiles(ys_ref, jnp.dot(hmid, wd_scr[...], preferred_element_type=F32), h * chunk_rows)
        if n_chunks * chunk_rows < ys_ref.shape[0]:
            ys_ref[n_chunks * chunk_rows:, :] = jnp.zeros((ys_ref.shape[0] - n_chunks * chunk_rows, LANES), F32)

    for n_chunks in range(MOE_BLK // MOE_CHUNK + 1):
        lo, hi = (n_chunks - 1) * MOE_CHUNK, n_chunks * MOE_CHUNK
        pl.when((valid > lo) & (valid <= hi))(functools.partial(run, n_chunks))


def experts(xs_tiles, block_expert, nused, block_valid, w_gate, w_up, w_down):
    d = D_MODEL
    nblk = block_expert.shape[0]
    blk_rows = MOE_BLK * TILE_ROWS
    wmap = lambda i, be, nu, bv: (be[i], 0, 0)
    grid_spec = pltpu.PrefetchScalarGridSpec(
        num_scalar_prefetch=3,
        grid=(nblk,),
        in_specs=[pl.BlockSpec((blk_rows, LANES), lambda i, be, nu, bv: (jnp.minimum(i, nu[0] - 1), 0)),
                  pl.BlockSpec((1, d, D_EXPERT), wmap),
                  pl.BlockSpec((1, d, D_EXPERT), wmap),
                  pl.BlockSpec((1, D_EXPERT, d), wmap)],
        out_specs=pl.BlockSpec((blk_rows, LANES), lambda i, be, nu, bv: (i, 0)),
        scratch_shapes=[pltpu.VMEM((d, D_EXPERT), BF16),
                        pltpu.VMEM((d, D_EXPERT), BF16),
                        pltpu.VMEM((D_EXPERT, d), BF16)],
    )
    return pl.pallas_call(
        _experts_kernel,
        grid_spec=grid_spec,
        out_shape=jax.ShapeDtypeStruct((nblk * blk_rows, LANES), F32),
        compiler_params=_cparams(("arbitrary",)),
        name="experts",
    )(block_expert, nused, block_valid, xs_tiles, w_gate, w_up, w_down)


def _combine_kernel(dest_ref, ys_hbm, x1_ref, meta_ref, mod_ref, gf_ref, o_ref, ybuf, sem):
    i = pl.program_id(0)
    nstep = pl.num_programs(0)
    tc = TC_COMB
    nt = dest_ref.shape[0] // 2
    half_rows = tc * TILE_ROWS
    buf_rows = 2 * half_rows

    def gather(step, slot):
        def issue(c, carry):
            t0 = c * DMA_CHUNK
            for u in range(DMA_CHUNK):
                for k in range(2):
                    d = dest_ref[k * nt + step * tc + t0 + u]
                    pltpu.make_async_copy(
                        ys_hbm.at[pl.ds(d * TILE_ROWS, TILE_ROWS)],
                        ybuf.at[pl.ds(slot * buf_rows + k * half_rows + (t0 + u) * TILE_ROWS, TILE_ROWS)],
                        sem.at[slot]).start(priority=k)
            return carry

        lax.fori_loop(0, tc // DMA_CHUNK, issue, 0)

    @pl.when(i == 0)
    def _():
        gather(0, 0)

    @pl.when(i + 1 < nstep)
    def _():
        gather(i + 1, (i + 1) % 2)

    slot = i % 2
    start = pl.multiple_of(slot * buf_rows, buf_rows)
    pltpu.make_async_copy(ys_hbm.at[pl.ds(0, buf_rows)], ybuf.at[pl.ds(start, buf_rows)], sem.at[slot]).wait()
    meta = meta_ref[...]
    y0 = _load_token_tiles(ybuf, start, tc)
    y1 = _load_token_tiles(ybuf, start + half_rows, tc)
    moe = y0 * meta[:, 4:5] + y1 * meta[:, 5:6]
    x2 = x1_ref[...] + mod_ref[0][5:6] * moe
    o_ref[...] = _rms(x2, gf_ref[...])


def combine(dest_flat, ys, x1_flat, meta, mods, g_final, n_per_batch):
    nt, d = x1_flat.shape
    tc = TC_COMB
    per_b = n_per_batch // tc
    grid_spec = pltpu.PrefetchScalarGridSpec(
        num_scalar_prefetch=1,
        grid=(nt // tc,),
        in_specs=[pl.BlockSpec(memory_space=pl.ANY),
                  pl.BlockSpec((tc, d), lambda i, ds: (i, 0)),
                  pl.BlockSpec((tc, LANES), lambda i, ds: (i, 0)),
                  pl.BlockSpec((1, 8, d), lambda i, ds: (i // per_b, 0, 0)),
                  pl.BlockSpec((1, d), lambda i, ds: (0, 0))],
        out_specs=pl.BlockSpec((tc, d), lambda i, ds: (i, 0)),
        scratch_shapes=[pltpu.VMEM((2 * 2 * tc * TILE_ROWS, LANES), F32),
                        pltpu.SemaphoreType.DMA((2,))],
    )
    return pl.pallas_call(
        _combine_kernel,
        grid_spec=grid_spec,
        out_shape=jax.ShapeDtypeStruct((nt, d), F32),
        compiler_params=_cparams(("arbitrary",)),
        name="combine",
    )(dest_flat, ys, x1_flat, meta, mods, g_final.reshape(1, d))


def _slots_kernel(pstart_ref, metat_ref, dest_ref):
    eid = metat_ref[0:2, :]
    slot = metat_ref[2:4, :]
    for e in range(N_EXPERTS):
        slot = slot + jnp.where(eid == float(e), pstart_ref[e].astype(F32), 0.0)
    dest_ref[...] = slot.astype(jnp.int32)


def _dispatch_plan(metat, counts_row, nt):
    counts = counts_row[N_GROUPS:N_GROUPS + N_EXPERTS].astype(jnp.int32)
    pcounts = ((counts + MOE_BLK - 1) // MOE_BLK) * MOE_BLK
    pends = jnp.cumsum(pcounts)
    pstarts = pends - pcounts
    dest = pl.pallas_call(
        _slots_kernel,
        grid_spec=pltpu.PrefetchScalarGridSpec(
            num_scalar_prefetch=1, grid=(1,),
            in_specs=[pl.BlockSpec(metat.shape, lambda i, ps: (0, 0))],
            out_specs=pl.BlockSpec((2, nt), lambda i, ps: (0, 0))),
        out_shape=jax.ShapeDtypeStruct((2, nt), jnp.int32),
        compiler_params=_cparams(("arbitrary",)),
        name="slots",
    )(pstarts.astype(jnp.int32), metat).reshape(-1)
    nblk = (nt * 2) // MOE_BLK + N_EXPERTS
    first_slot = jnp.arange(nblk, dtype=jnp.int32) * MOE_BLK
    block_expert = jnp.minimum(
        jnp.sum((pends[None, :] <= first_slot[:, None]).astype(jnp.int32), axis=1), N_EXPERTS - 1)
    nused = (pends[-1] // MOE_BLK).astype(jnp.int32).reshape(1)
    seg_end = jnp.sum(jnp.where(block_expert[:, None] == jnp.arange(N_EXPERTS)[None, :],
                                (pstarts + counts)[None, :], 0), axis=1)
    block_valid = jnp.where(first_slot < pends[-1], jnp.clip(seg_end - first_slot, 0, MOE_BLK), 0)
    return (dest, pstarts.astype(jnp.int32), counts, block_expert.astype(jnp.int32), nused,
            block_valid.astype(jnp.int32))


def kernel(x, c, ctx, c_ctx, w_ada, b_ada, g_norm1, w_in, w_fmix, rpb, g_out, w_out, g_norm2,
           w_router_group, w_router_expert, w_gate, w_up, w_down, g_final):
    b, n, d = x.shape
    assert (b, n, d) == (c.shape[0], SEQ, D_MODEL) and w_ada.shape[0] == 1
    nt = b * n

    cond8 = jnp.zeros((8, d), F32).at[0:b].set(c).at[b].set(c_ctx)
    mod = adaln(cond8, w_ada[0], b_ada[0])
    mods = jnp.pad(mod[0:b].reshape(b, N_MOD, d), ((0, 0), (0, 2), (0, 0)))
    mod_ctx = jnp.pad(mod[b].reshape(N_MOD, d), ((0, 2), (0, 0)))

    w_in_b = w_in[0].astype(BF16)
    qr, qp, kr, v, a = in_proj(x, mods, g_norm1[0], w_in_b)
    kc, vc = ctx_proj(ctx, mod_ctx, g_norm1[0], w_in_b[:, D_FOURIER + D_NA:])

    fr = dft_cols(dft_rows(a))
    na = attention(qr, qp, kr, v, kc, vc, bias_tables(rpb[0]))

    w_router = jnp.concatenate(
        [w_router_group[0], w_router_expert[0],
         jnp.zeros((d, LANES - N_GROUPS - N_EXPERTS), F32)], axis=1).astype(BF16)
    x1, h2_tiles, meta, metat, cnt = out_proj(fr, na, x, mods, w_fmix[0].astype(BF16),
                                              w_out[0].astype(BF16), g_out[0], g_norm2[0], w_router)

    dest, pstarts, counts, block_expert, nused, block_valid = _dispatch_plan(metat, cnt[0], nt)
    xs_tiles = dispatch(h2_tiles, dest, pstarts, counts, block_expert.shape[0] * MOE_BLK)
    ys_tiles = experts(xs_tiles, block_expert, nused, block_valid, w_gate[0], w_up[0], w_down[0])
    out = combine(dest, ys_tiles, x1.reshape(nt, d), meta, mods, g_final, n)
    return out.reshape(b, n, d)
```

```python
import functools
import math

import numpy as np
import jax
import jax.numpy as jnp
from jax import lax
from jax.experimental import pallas as pl
from jax.experimental.pallas import tpu as pltpu

F32 = jnp.float32
BF16 = jnp.bfloat16

D_MODEL = 1024
GRID_W = 64
GRID_H = 128
SEQ = GRID_W * GRID_H
CTX_LEN = 256
D_FOURIER = 256
FOURIER_GROUP = 64
HEAD_DIM = 64
N_HEADS = 12
D_NA = N_HEADS * HEAD_DIM
N_PAIRS = N_HEADS // 2
NA_ROWS = 8
NA_COLS = 16
ROPE_THETA = 10000.0
ROPE_CHUNK = HEAD_DIM // 2
ROPE_HALF = ROPE_CHUNK // 2
N_GROUPS = 4
EXPERTS_PER_GROUP = 8
N_EXPERTS = N_GROUPS * EXPERTS_PER_GROUP
D_EXPERT = 512
N_MOD = 6
D_IN_PROJ = D_FOURIER + 3 * D_NA
EPS = 1e-6
LANES = 128
NEG = -1e30
LOG2E = math.log2(math.e)

ADALN_TN = 1536
TM_IN = 1024
ROW_PITCH = 72
TM_PROJ = 1024
OUT_CHUNK = 512
ATT_ROWS = 64
DFT_TW = 32
DFT_TK1 = 32
MOE_BLK = 512
MOE_CHUNK = 256
TD_DISP = 2048
TC_COMB = 256
DMA_CHUNK = 8
VMEM_LIMIT = 56 * 1024 * 1024


def _cparams(sem):
    return pltpu.CompilerParams(dimension_semantics=sem, vmem_limit_bytes=VMEM_LIMIT)


def _mxu_const(table):
    return jnp.asarray(table, F32).astype(BF16)


@functools.lru_cache(maxsize=None)
def _rope_tables():
    t = np.arange(SEQ)
    row, col = t // GRID_W, t % GRID_W
    lane = np.arange(LANES)
    d = lane % HEAD_DIM
    chunk = d // ROPE_CHUNK
    e = d % ROPE_CHUNK
    j = e % ROPE_HALF
    inv = ROPE_THETA ** (-(j.astype(np.float64)) / ROPE_HALF)
    pos = np.where(chunk[None, :] == 0, row[:, None], col[:, None]).astype(np.float64)
    ang = pos * inv[None, :]
    cos = np.cos(ang)
    sin = np.sin(ang)
    first = (e < ROPE_HALF)[None, :]
    s_first = np.where(first, -sin, 0.0)
    s_second = np.where(first, 0.0, sin)
    return (cos.astype(np.float32), s_first.astype(np.float32), s_second.astype(np.float32))


@functools.lru_cache(maxsize=None)
def _chan_dft():
    c = np.arange(FOURIER_GROUP)
    ang = 2.0 * np.pi * ((c[:, None] * c[None, :]) % FOURIER_GROUP) / FOURIER_GROUP
    eye = np.eye(D_FOURIER // FOURIER_GROUP)
    re = np.kron(eye, np.cos(ang))
    im = np.kron(eye, -np.sin(ang))
    return np.concatenate([re, im], axis=1).astype(np.float32)


@functools.lru_cache(maxsize=None)
def _row_dft():
    k1 = np.arange(GRID_H)[:, None]
    r = np.arange(GRID_H)[None, :]
    out = np.zeros((GRID_W, 2 * GRID_H, 2 * GRID_H), np.float32)
    for w in range(GRID_W):
        m = (k1 * (GRID_W * r + w)) % SEQ
        ang = 2.0 * np.pi * m / SEQ
        c, s = np.cos(ang), np.sin(ang)
        out[w] = np.block([[c, s], [-s, c]])
    return out


@functools.lru_cache(maxsize=None)
def _col_dft():
    k2 = np.arange(GRID_W)
    ang = 2.0 * np.pi * ((k2[:, None] * k2[None, :]) % GRID_W) / GRID_W
    scale = 1.0 / math.sqrt(SEQ * FOURIER_GROUP)
    return (np.concatenate([np.cos(ang), np.sin(ang)], axis=1) * scale).astype(np.float32)


@functools.lru_cache(maxsize=None)
def _bias_index():
    c = np.arange(GRID_W)
    start = np.clip(c - NA_COLS // 2, 0, GRID_W - NA_COLS)
    valid = (c[None, :] >= start[:, None]) & (c[None, :] < start[:, None] + NA_COLS)
    dc = np.clip(c[None, :] - c[:, None] + (NA_COLS - 1), 0, 2 * NA_COLS - 2)
    return dc.astype(np.int32), valid


@functools.lru_cache(maxsize=None)
def _strict_lower(n):
    return np.tril(np.ones((n, n), np.float32), k=-1)


def _adaln_kernel(c_ref, w_ref, b_ref, o_ref):
    c = c_ref[...]
    s = c * jax.nn.sigmoid(c)
    o_ref[...] = jnp.dot(s.astype(BF16), w_ref[...].astype(BF16), preferred_element_type=F32) + b_ref[...]


def adaln(cond8, w, b):
    n = w.shape[1]
    tn = ADALN_TN
    return pl.pallas_call(
        _adaln_kernel,
        grid=(n // tn,),
        in_specs=[pl.BlockSpec((8, D_MODEL), lambda j: (0, 0)),
                  pl.BlockSpec((D_MODEL, tn), lambda j: (0, j)),
                  pl.BlockSpec((1, tn), lambda j: (0, j))],
        out_specs=pl.BlockSpec((8, tn), lambda j: (0, j)),
        out_shape=jax.ShapeDtypeStruct((8, n), F32),
        compiler_params=_cparams(("arbitrary",)),
        name="adaln",
    )(cond8, w, b.reshape(1, n))


def _norm_mod(x, g, shift, scale):
    ms = jnp.mean(x * x, axis=-1, keepdims=True)
    return (x * lax.rsqrt(ms + EPS) * g) * (1.0 + scale) + shift


def _in_proj_kernel(x_ref, mod_ref, g_ref, w_ref, cs_ref, cos_ref, s1_ref, s2_ref,
                    qr_ref, qp_ref, kr_ref, v_ref, a_ref, h_scr, a_scr):
    m = mod_ref[0]
    h_scr[...] = _norm_mod(x_ref[0], g_ref[...], m[0:1], m[1:2]).astype(BF16)
    cos, s1, s2 = cos_ref[...], s1_ref[...], s2_ref[...]

    def rope(t):
        return (t * cos + pltpu.roll(t, LANES - ROPE_HALF, axis=1) * s1
                + pltpu.roll(t, ROPE_HALF, axis=1) * s2)

    f = jnp.dot(h_scr[...], w_ref[:, 0:D_FOURIER], preferred_element_type=F32)
    a = jnp.dot(f.astype(BF16), cs_ref[...], preferred_element_type=F32)
    n_planes = 2 * D_FOURIER // LANES
    tile_rows = a.shape[0] // GRID_W
    for p in range(n_planes):
        for r in range(tile_rows):
            a_scr[p, r * ROW_PITCH:r * ROW_PITCH + GRID_W, :] = a[r * GRID_W:(r + 1) * GRID_W,
                                                                LANES * p:LANES * (p + 1)]
    for w in range(GRID_W):
        for p in range(n_planes):
            slab = a_scr[p, pl.ds(w, tile_rows, stride=ROW_PITCH), :]
            lo = w * D_FOURIER + (p % 2) * LANES
            a_ref[0, p // 2, :, lo:lo + LANES] = slab.astype(BF16)

    scale = HEAD_DIM ** -0.5 * LOG2E
    wide = 2 * LANES
    for c in range(D_NA // wide):
        lo = D_FOURIER + wide * c
        q = jnp.dot(h_scr[...], w_ref[:, lo:lo + wide], preferred_element_type=F32)
        k = jnp.dot(h_scr[...], w_ref[:, lo + D_NA:lo + D_NA + wide], preferred_element_type=F32)
        v = jnp.dot(h_scr[...], w_ref[:, lo + 2 * D_NA:lo + 2 * D_NA + wide], preferred_element_type=F32)
        v_ref[0, :, wide * c:wide * (c + 1)] = v.astype(BF16)
        for s in range(2):
            sl = slice(LANES * s, LANES * (s + 1))
            ol = slice(wide * c + LANES * s, wide * c + LANES * (s + 1))
            qs, ks = q[:, sl], k[:, sl]
            qp_ref[0, :, ol] = (qs * scale).astype(BF16)
            qr_ref[0, :, ol] = (rope(qs) * scale).astype(BF16)
            kr_ref[0, :, ol] = rope(ks).astype(BF16)


def in_proj(x, mods, g1, w_in_bf16):
    b, n, d = x.shape
    tm = TM_IN
    cos, s1, s2 = _rope_tables()
    cs = _mxu_const(_chan_dft())
    tok = lambda bi, i: (bi, i, 0)
    const2 = lambda bi, i: (0, 0)
    tab = pl.BlockSpec((tm, LANES), lambda bi, i: (i, 0))
    qkv_shape = jax.ShapeDtypeStruct((b, n, D_NA), BF16)
    qkv_spec = pl.BlockSpec((1, tm, D_NA), tok)
    return pl.pallas_call(
        _in_proj_kernel,
        grid=(b, n // tm),
        in_specs=[pl.BlockSpec((1, tm, d), tok),
                  pl.BlockSpec((1, 8, d), lambda bi, i: (bi, 0, 0)),
                  pl.BlockSpec((1, d), const2),
                  pl.BlockSpec((d, D_IN_PROJ), const2),
                  pl.BlockSpec((D_FOURIER, 2 * D_FOURIER), const2),
                  tab, tab, tab],
        out_specs=[qkv_spec, qkv_spec, qkv_spec, qkv_spec,
                   pl.BlockSpec((1, 2, tm // GRID_W, GRID_W * D_FOURIER), lambda bi, i: (bi, 0, i, 0))],
        out_shape=[qkv_shape, qkv_shape, qkv_shape, qkv_shape,
                   jax.ShapeDtypeStruct((b, 2, n // GRID_W, GRID_W * D_FOURIER), BF16)],
        scratch_shapes=[pltpu.VMEM((tm, d), BF16),
                        pltpu.VMEM((2 * D_FOURIER // LANES, (tm // GRID_W) * ROW_PITCH, LANES), F32)],
        compiler_params=_cparams(("arbitrary", "arbitrary")),
        name="in_proj",
    )(x, mods, g1.reshape(1, d), w_in_bf16, cs, jnp.asarray(cos), jnp.asarray(s1), jnp.asarray(s2))


def _ctx_proj_kernel(x_ref, mod_ref, g_ref, w_ref, k_ref, v_ref):
    m = mod_ref[...]
    h = _norm_mod(x_ref[0], g_ref[...], m[0:1], m[1:2]).astype(BF16)
    k_ref[0] = jnp.dot(h, w_ref[:, 0:D_NA], preferred_element_type=F32).astype(BF16)
    v_ref[0] = jnp.dot(h, w_ref[:, D_NA:2 * D_NA], preferred_element_type=F32).astype(BF16)


def ctx_proj(ctx, mod_ctx, g1, w_kv_bf16):
    b, l, d = ctx.shape
    shape = jax.ShapeDtypeStruct((b, l, D_NA), BF16)
    spec = pl.BlockSpec((1, l, D_NA), lambda bi: (bi, 0, 0))
    return pl.pallas_call(
        _ctx_proj_kernel,
        grid=(b,),
        in_specs=[pl.BlockSpec((1, l, d), lambda bi: (bi, 0, 0)),
                  pl.BlockSpec((8, d), lambda bi: (0, 0)),
                  pl.BlockSpec((1, d), lambda bi: (0, 0)),
                  pl.BlockSpec((d, 2 * D_NA), lambda bi: (0, 0))],
        out_specs=[spec, spec],
        out_shape=[shape, shape],
        compiler_params=_cparams(("arbitrary",)),
        name="ctx_proj",
    )(ctx, mod_ctx, g1.reshape(1, d), w_kv_bf16)


N_PLANES = D_FOURIER // LANES
K1_PITCH = GRID_H + 8
K2_PITCH = GRID_W + 8


def _dft_rows_kernel(a_ref, g_ref, z_ref, z_scr):
    for j in range(DFT_TW):
        sl = slice(D_FOURIER * j, D_FOURIER * (j + 1))
        rhs = jnp.concatenate([a_ref[0, 0, :, sl], a_ref[0, 1, :, sl]], axis=0)
        z = jnp.dot(g_ref[j], rhs, preferred_element_type=F32)
        for c in range(2):
            for p in range(N_PLANES):
                z_scr[c * N_PLANES + p, j * K1_PITCH:j * K1_PITCH + GRID_H, :] = (
                    z[c * GRID_H:(c + 1) * GRID_H, LANES * p:LANES * (p + 1)])
    for k1 in range(GRID_H):
        for c in range(2):
            for p in range(N_PLANES):
                slab = z_scr[c * N_PLANES + p, pl.ds(k1, DFT_TW, stride=K1_PITCH), :]
                lo = k1 * D_FOURIER + p * LANES
                z_ref[0, c, :, lo:lo + LANES] = slab.astype(BF16)


def dft_rows(a):
    b = a.shape[0]
    g = _mxu_const(_row_dft())
    return pl.pallas_call(
        _dft_rows_kernel,
        grid=(GRID_W // DFT_TW, b),
        in_specs=[pl.BlockSpec((1, 2, GRID_H, DFT_TW * D_FOURIER), lambda j, bi: (bi, 0, 0, j)),
                  pl.BlockSpec((DFT_TW, 2 * GRID_H, 2 * GRID_H), lambda j, bi: (j, 0, 0))],
        out_specs=pl.BlockSpec((1, 2, DFT_TW, GRID_H * D_FOURIER), lambda j, bi: (bi, 0, j, 0)),
        out_shape=jax.ShapeDtypeStruct((b, 2, GRID_W, GRID_H * D_FOURIER), BF16),
        scratch_shapes=[pltpu.VMEM((2 * N_PLANES, DFT_TW * K1_PITCH, LANES), F32)],
        compiler_params=_cparams(("arbitrary", "arbitrary")),
        name="dft_rows",
    )(a, g)


def _dft_cols_kernel(z_ref, cs_ref, o_ref, o_scr):
    for j in range(DFT_TK1):
        sl = slice(D_FOURIER * j, D_FOURIER * (j + 1))
        rhs = jnp.concatenate([z_ref[0, 0, :, sl], z_ref[0, 1, :, sl]], axis=0)
        out = jnp.dot(cs_ref[...], rhs, preferred_element_type=F32)
        for p in range(N_PLANES):
            o_scr[p, j * K2_PITCH:j * K2_PITCH + GRID_W, :] = out[:, LANES * p:LANES * (p + 1)]
    for k2 in range(GRID_W):
        for p in range(N_PLANES):
            slab = o_scr[p, pl.ds(k2, DFT_TK1, stride=K2_PITCH), :]
            o_ref[0, k2, :, LANES * p:LANES * (p + 1)] = slab.astype(BF16)


def dft_cols(z):
    b = z.shape[0]
    cs = _mxu_const(_col_dft())
    out = pl.pallas_call(
        _dft_cols_kernel,
        grid=(b, GRID_H // DFT_TK1),
        in_specs=[pl.BlockSpec((1, 2, GRID_W, DFT_TK1 * D_FOURIER), lambda bi, j: (bi, 0, 0, j)),
                  pl.BlockSpec((GRID_W, 2 * GRID_W), lambda bi, j: (0, 0))],
        out_specs=pl.BlockSpec((1, GRID_W, DFT_TK1, D_FOURIER), lambda bi, j: (bi, 0, j, 0)),
        out_shape=jax.ShapeDtypeStruct((b, GRID_W, GRID_H, D_FOURIER), BF16),
        scratch_shapes=[pltpu.VMEM((N_PLANES, DFT_TK1 * K2_PITCH, LANES), F32)],
        compiler_params=_cparams(("arbitrary", "arbitrary")),
        name="dft_cols",
    )(z, cs)
    return out.reshape(b, SEQ, D_FOURIER)


ROW_KEYS = NA_ROWS * GRID_W
ROW_TILES = ROW_KEYS // LANES
N_DR = 2 * NA_ROWS - 2
SOFT_ROWS = 32
ATT_PAIRS = ATT_ROWS // 2


def _attention_kernel(qr_ref, qp_ref, k_ref, v_ref, kc_ref, vc_ref, tb_ref, o_ref,
                      sc_scr, pc_scr, s_scr, p_scr):
    rb = pl.program_id(2)
    first1 = lax.broadcasted_iota(jnp.int32, (GRID_W, LANES), 1) < HEAD_DIM
    nt = (((1,), (1,)), ((), ()))

    def window(jp, i):
        r = rb * ATT_ROWS + 2 * jp + i
        rs = jnp.clip(r - NA_ROWS // 2, 0, GRID_H - NA_ROWS)
        return r, rs, pl.multiple_of(rs * GRID_W, GRID_W)

    def split_heads(q1):
        z1 = jnp.zeros_like(q1)
        return jnp.concatenate([jnp.where(first1, q1, z1), jnp.where(first1, z1, q1)], axis=0)

    def scores(jp):
        qp_rows = []
        for i in range(2):
            _, _, koff = window(jp, i)
            qs = slice((2 * jp + i) * GRID_W, (2 * jp + i + 1) * GRID_W)
            kw = k_ref[0, pl.ds(koff, ROW_KEYS), :]
            s_scr[jp, i * 2 * GRID_W:(i + 1) * 2 * GRID_W, :] = lax.dot_general(
                split_heads(qr_ref[0, qs, :]), kw, nt, preferred_element_type=F32)
            qp_rows.append(split_heads(qp_ref[0, qs, :]))
        sc_scr[jp] = lax.dot_general(jnp.concatenate(qp_rows, axis=0), kc_ref[0], nt,
                                     preferred_element_type=F32)

    def softmax(jp):
        for c in range(4):
            i, hh = c // 2, c % 2
            r, rs, _ = window(jp, i)
            d0 = rs - r + NA_ROWS - 1
            for h in range(GRID_W // SOFT_ROWS):
                lo = h * SOFT_ROWS
                rows = slice(c * GRID_W + lo, c * GRID_W + lo + SOFT_ROWS)
                tiles = [s_scr[jp, rows, t * LANES:(t + 1) * LANES]
                         + tb_ref[0, hh, d0 + 2 * t, lo:lo + SOFT_ROWS, :] for t in range(ROW_TILES)]
                sc = sc_scr[jp, rows, :]
                mt = jnp.maximum(sc[:, :LANES], sc[:, LANES:])
                for tl in tiles:
                    mt = jnp.maximum(mt, tl)
                m = jnp.max(mt, axis=1, keepdims=True)
                pc_scr[jp, rows, :] = jnp.exp2(sc - m).astype(BF16)
                for t, tl in enumerate(tiles):
                    p_scr[jp, rows, t * LANES:(t + 1) * LANES] = jnp.exp2(tl - m).astype(BF16)

    def weighted_values(jp):
        vc = jnp.concatenate([vc_ref[0], jnp.ones((CTX_LEN, LANES), BF16)], axis=1)
        oc = jnp.dot(pc_scr[jp], vc, preferred_element_type=F32)
        for i in range(2):
            _, _, koff = window(jp, i)
            rows = slice(i * 2 * GRID_W, (i + 1) * 2 * GRID_W)
            vw = jnp.concatenate([v_ref[0, pl.ds(koff, ROW_KEYS), :], jnp.ones((ROW_KEYS, LANES), BF16)],
                                 axis=1)
            o = jnp.dot(p_scr[jp, rows, :], vw, preferred_element_type=F32) + oc[rows]
            oa, ob = o[:GRID_W], o[GRID_W:]
            out = jnp.where(first1, oa[:, :LANES] / oa[:, LANES:], ob[:, :LANES] / ob[:, LANES:])
            o_ref[0, (2 * jp + i) * GRID_W:(2 * jp + i + 1) * GRID_W, :] = out.astype(BF16)

    scores(0)
    scores(1)
    softmax(0)
    for jp in range(ATT_PAIRS):
        if jp + 2 < ATT_PAIRS:
            scores(jp + 2)
        if jp + 1 < ATT_PAIRS:
            softmax(jp + 1)
        weighted_values(jp)


def bias_tables(rpb):
    dc, valid = _bias_index()
    n_dc = 2 * NA_COLS - 1
    onehot = (dc.reshape(1, -1) == np.arange(n_dc).reshape(-1, 1)).astype(np.float32)
    t = jnp.dot(rpb.reshape(-1, n_dc), jnp.asarray(onehot), precision=lax.Precision.HIGHEST)
    t = t.reshape(N_HEADS, 2 * NA_ROWS - 1, GRID_W, GRID_W)
    tb = pl.pallas_call(
        _bias_pairs_kernel,
        grid=(N_HEADS,),
        in_specs=[pl.BlockSpec((GRID_W, GRID_W), lambda h: (0, 0)),
                  pl.BlockSpec((1, N_DR + 1, GRID_W, GRID_W), lambda h: (h, 0, 0, 0))],
        out_specs=pl.BlockSpec((1, N_DR, GRID_W, LANES), lambda h: (h, 0, 0, 0)),
        out_shape=jax.ShapeDtypeStruct((N_HEADS, N_DR, GRID_W, LANES), F32),
        compiler_params=_cparams(("arbitrary",)),
        name="bias_pairs",
    )(jnp.asarray(valid.astype(np.float32)), t)
    return tb.reshape(N_PAIRS, 2, N_DR, GRID_W, LANES)


def _bias_pairs_kernel(valid_ref, t_ref, o_ref):
    inside = valid_ref[...] > 0.5
    rows = [jnp.where(inside, LOG2E * t_ref[0, d], NEG) for d in range(N_DR + 1)]
    for d in range(N_DR):
        o_ref[0, d] = jnp.concatenate([rows[d], rows[d + 1]], axis=1)


def attention(qr, qp, kr, v, kc, vc, tb):
    b, n, _ = qr.shape
    tq = ATT_ROWS * GRID_W
    qspec = pl.BlockSpec((1, tq, LANES), lambda bi, hp, i: (bi, i, hp))
    kspec = pl.BlockSpec((1, n, LANES), lambda bi, hp, i: (bi, 0, hp))
    cspec = pl.BlockSpec((1, CTX_LEN, LANES), lambda bi, hp, i: (bi, 0, hp))
    return pl.pallas_call(
        _attention_kernel,
        grid=(b, N_PAIRS, GRID_H // ATT_ROWS),
        in_specs=[qspec, qspec, kspec, kspec, cspec, cspec,
                  pl.BlockSpec((1, 2, N_DR, GRID_W, LANES), lambda bi, hp, i: (hp, 0, 0, 0, 0))],
        out_specs=qspec,
        out_shape=jax.ShapeDtypeStruct((b, n, D_NA), BF16),
        scratch_shapes=[pltpu.VMEM((ATT_PAIRS, 4 * GRID_W, CTX_LEN), F32),
                        pltpu.VMEM((ATT_PAIRS, 4 * GRID_W, CTX_LEN), BF16),
                        pltpu.VMEM((ATT_PAIRS, 4 * GRID_W, ROW_KEYS), F32),
                        pltpu.VMEM((ATT_PAIRS, 4 * GRID_W, ROW_KEYS), BF16)],
        compiler_params=_cparams(("arbitrary", "arbitrary", "arbitrary")),
        name="attention",
    )(qr, qp, kr, v, kc, vc, tb)


def _rms(x, g):
    ms = jnp.mean(x * x, axis=-1, keepdims=True)
    return x * lax.rsqrt(ms + EPS) * g


TILE_ROWS = D_MODEL // LANES


def _store_token_tiles(ref, val, start=0):
    rows = val.shape[0]
    for j in range(TILE_ROWS):
        ref[pl.ds(start + j, rows, stride=TILE_ROWS), :] = val[:, LANES * j:LANES * (j + 1)]


def _load_token_tiles(ref, start, rows):
    return jnp.concatenate(
        [ref[pl.ds(start + j, rows, stride=TILE_ROWS), :] for j in range(TILE_ROWS)], axis=1)


def _out_proj_kernel(fr_ref, na_ref, x_ref, mod_ref, wf_ref, wo_ref, go_ref, g2_ref, wr_ref, tri_ref,
                     x1_ref, h2_ref, meta_ref, metat_ref, cnt_ref, run_scr):
    @pl.when((pl.program_id(0) == 0) & (pl.program_id(1) == 0))
    def _():
        run_scr[...] = jnp.zeros_like(run_scr)

    m = mod_ref[0]
    go = go_ref[...]
    rows_c = OUT_CHUNK
    lane = lax.broadcasted_iota(jnp.int32, (rows_c, LANES), 1).astype(F32)
    ninf = jnp.float32(-jnp.inf)

    def argmax_first(vals):
        mx = jnp.max(vals, axis=1, keepdims=True)
        idx = jnp.min(jnp.where(vals == mx, lane, float(LANES)), axis=1, keepdims=True)
        return mx, idx

    run = run_scr[...]
    for c in range(x_ref.shape[1] // rows_c):
        rows = slice(c * rows_c, (c + 1) * rows_c)
        fo = jnp.dot(fr_ref[0, rows, :], wf_ref[...], preferred_element_type=F32)
        fn = _rms(fo, go[:, :D_FOURIER]).astype(BF16)
        nn = _rms(na_ref[0, rows, :].astype(F32), go[:, D_FOURIER:]).astype(BF16)
        y = (jnp.dot(fn, wo_ref[0:D_FOURIER, :], preferred_element_type=F32)
             + jnp.dot(nn, wo_ref[D_FOURIER:, :], preferred_element_type=F32))
        x1 = x_ref[0, rows, :] + m[2:3] * y
        x1_ref[0, rows, :] = x1
        h2 = _norm_mod(x1, g2_ref[...], m[3:4], m[4:5])
        _store_token_tiles(h2_ref, h2, c * rows_c * TILE_ROWS)
        logits = jnp.dot(h2.astype(BF16), wr_ref[...], preferred_element_type=F32)

        lg = jnp.where(lane < N_GROUPS, logits, ninf)
        gmax, gidx = argmax_first(lg)
        pg = 1.0 / jnp.sum(jnp.exp(lg - gmax), axis=1, keepdims=True)
        lo = N_GROUPS + EXPERTS_PER_GROUP * gidx
        le = jnp.where((lane >= lo) & (lane < lo + EXPERTS_PER_GROUP), logits, ninf)
        e1, i1 = argmax_first(le)
        e2, i2 = argmax_first(jnp.where(lane == i1, ninf, le))
        dd = jnp.exp(e2 - e1)
        gate1 = pg / (1.0 + dd)
        gate2 = pg * dd / (1.0 + dd)

        hot1 = lane == i1
        hot2 = lane == i2
        onehot = jnp.where(hot1 | hot2, 1.0, 0.0)
        cnt = jnp.dot(tri_ref[...], onehot.astype(BF16), preferred_element_type=F32) + run
        rank1 = jnp.sum(jnp.where(hot1, cnt, 0.0), axis=1, keepdims=True)
        rank2 = jnp.sum(jnp.where(hot2, cnt, 0.0), axis=1, keepdims=True)
        run = run + jnp.sum(onehot, axis=0, keepdims=True)

        meta = jnp.where(lane == 0, i1 - N_GROUPS,
               jnp.where(lane == 1, i2 - N_GROUPS,
               jnp.where(lane == 2, rank1,
               jnp.where(lane == 3, rank2,
               jnp.where(lane == 4, gate1,
               jnp.where(lane == 5, gate2, 0.0))))))
        meta_ref[rows, :] = meta
        metat_ref[:, rows] = jnp.transpose(meta)[0:8, :]
    run_scr[...] = run
    cnt_ref[...] = jnp.broadcast_to(run, cnt_ref.shape)


def out_proj(fr, na, x, mods, w_fmix_bf16, w_out_bf16, g_out, g2, w_router_bf16):
    b, n, d = x.shape
    tm = TM_PROJ
    steps = n // tm
    tok = lambda bi, i: (bi, i, 0)
    const2 = lambda bi, i: (0, 0)
    flat = lambda bi, i: (bi * steps + i, 0)
    tri = _mxu_const(_strict_lower(OUT_CHUNK))
    return pl.pallas_call(
        _out_proj_kernel,
        grid=(b, steps),
        in_specs=[pl.BlockSpec((1, tm, D_FOURIER), tok),
                  pl.BlockSpec((1, tm, D_NA), tok),
                  pl.BlockSpec((1, tm, d), tok),
                  pl.BlockSpec((1, 8, d), lambda bi, i: (bi, 0, 0)),
                  pl.BlockSpec((D_FOURIER, D_FOURIER), const2),
                  pl.BlockSpec((d, d), const2),
                  pl.BlockSpec((1, d), const2),
                  pl.BlockSpec((1, d), const2),
                  pl.BlockSpec((d, LANES), const2),
                  pl.BlockSpec((OUT_CHUNK, OUT_CHUNK), const2)],
        out_specs=[pl.BlockSpec((1, tm, d), tok),
                   pl.BlockSpec((tm * TILE_ROWS, LANES), flat),
                   pl.BlockSpec((tm, LANES), flat),
                   pl.BlockSpec((8, tm), lambda bi, i: (0, bi * steps + i)),
                   pl.BlockSpec((8, LANES), const2)],
        out_shape=[jax.ShapeDtypeStruct((b, n, d), F32),
                   jax.ShapeDtypeStruct((b * n * TILE_ROWS, LANES), F32),
                   jax.ShapeDtypeStruct((b * n, LANES), F32),
                   jax.ShapeDtypeStruct((8, b * n), F32),
                   jax.ShapeDtypeStruct((8, LANES), F32)],
        scratch_shapes=[pltpu.VMEM((1, LANES), F32)],
        compiler_params=_cparams(("arbitrary", "arbitrary")),
        name="out_proj",
    )(fr, na, x, mods, w_fmix_bf16, w_out_bf16, g_out.reshape(1, d), g2.reshape(1, d), w_router_bf16, tri)


def _dispatch_kernel(dest_ref, pstart_ref, count_ref, h2_ref, xs_hbm, zero_scr, pad_rows, sem, pad_sem):
    i = pl.program_id(0)
    nt = dest_ref.shape[0] // 2

    @pl.when(i == 0)
    def _():
        zero_scr[...] = jnp.zeros_like(zero_scr)

        def per_expert(e, npad):
            lo = pstart_ref[e] + count_ref[e]
            mid = pstart_ref[e] + ((count_ref[e] + MOE_CHUNK - 1) // MOE_CHUNK) * MOE_CHUNK
            hi = pstart_ref[e] + ((count_ref[e] + MOE_BLK - 1) // MOE_BLK) * MOE_BLK

            def fill(s, carry):
                pltpu.make_async_copy(zero_scr.at[pl.ds(0, TILE_ROWS)],
                                      xs_hbm.at[pl.ds(s * TILE_ROWS, TILE_ROWS)], pad_sem).start()
                return carry

            def fill_chunk(s, carry):
                pltpu.make_async_copy(zero_scr.at[pl.ds(0, MOE_CHUNK * TILE_ROWS)],
                                      xs_hbm.at[pl.ds(mid * TILE_ROWS + s * MOE_CHUNK * TILE_ROWS,
                                                      MOE_CHUNK * TILE_ROWS)], pad_sem).start()
                return carry

            lax.fori_loop(lo, mid, fill, 0)
            lax.fori_loop(0, (hi - mid) // MOE_CHUNK, fill_chunk, 0)
            return npad + (hi - lo)

        npad = lax.fori_loop(0, N_EXPERTS, per_expert, 0)

        blk_rows = MOE_BLK * TILE_ROWS
        first_free = (pstart_ref[N_EXPERTS - 1] + count_ref[N_EXPERTS - 1] + MOE_BLK - 1) // MOE_BLK
        n_blocks = xs_hbm.shape[0] // blk_rows

        def fill_block(bk, carry):
            pltpu.make_async_copy(zero_scr, xs_hbm.at[pl.ds(bk * blk_rows, blk_rows)], pad_sem).start()
            return carry

        lax.fori_loop(first_free, n_blocks, fill_block, 0)
        pad_rows[0] = npad * TILE_ROWS + (n_blocks - first_free) * blk_rows

    def issue(c, carry):
        t0 = c * DMA_CHUNK
        for u in range(DMA_CHUNK):
            for k in range(2):
                d = dest_ref[k * nt + i * TD_DISP + t0 + u]
                pltpu.make_async_copy(h2_ref.at[pl.ds((t0 + u) * TILE_ROWS, TILE_ROWS)],
                                      xs_hbm.at[pl.ds(d * TILE_ROWS, TILE_ROWS)], sem).start(priority=k)
        return carry

    lax.fori_loop(0, TD_DISP // DMA_CHUNK, issue, 0)
    rows = 2 * TD_DISP * TILE_ROWS
    pltpu.make_async_copy(xs_hbm.at[pl.ds(0, rows)], xs_hbm.at[pl.ds(0, rows)], sem).wait()

    @pl.when((i == pl.num_programs(0) - 1) & (pad_rows[0] > 0))
    def _():
        n = pad_rows[0]
        pltpu.make_async_copy(xs_hbm.at[pl.ds(0, n)], xs_hbm.at[pl.ds(0, n)], pad_sem).wait()


def dispatch(h2_tiles, dest, pstarts, counts, n_slots):
    nt = dest.shape[0] // 2
    grid_spec = pltpu.PrefetchScalarGridSpec(
        num_scalar_prefetch=3,
        grid=(nt // TD_DISP,),
        in_specs=[pl.BlockSpec((TD_DISP * TILE_ROWS, LANES), lambda i, ds, ps, ct: (i, 0))],
        out_specs=pl.BlockSpec(memory_space=pl.ANY),
        scratch_shapes=[pltpu.VMEM((MOE_BLK * TILE_ROWS, LANES), F32),
                        pltpu.SMEM((1,), jnp.int32),
                        pltpu.SemaphoreType.DMA(()),
                        pltpu.SemaphoreType.DMA(())],
    )
    return pl.pallas_call(
        _dispatch_kernel,
        grid_spec=grid_spec,
        out_shape=jax.ShapeDtypeStruct((n_slots * TILE_ROWS, LANES), F32),
        compiler_params=_cparams(("arbitrary",)),
        name="dispatch",
    )(dest, pstarts, counts, h2_tiles)


def _experts_kernel(be_ref, nused_ref, valid_ref, xs_ref, wg_ref, wu_ref, wd_ref, ys_ref,
                    wg_scr, wu_scr, wd_scr):
    i = pl.program_id(0)
    valid = valid_ref[i]
    changed = (i == 0) | (be_ref[i] != be_ref[jnp.maximum(i - 1, 0)])
    chunk_rows = MOE_CHUNK * TILE_ROWS

    @pl.when(changed & (valid > 0))
    def _():
        wg_scr[...] = wg_ref[0].astype(BF16)
        wu_scr[...] = wu_ref[0].astype(BF16)
        wd_scr[...] = wd_ref[0].astype(BF16)

    def run(n_chunks):
        hmids = []
        for h in range(n_chunks):
            x = _load_token_tiles(xs_ref, h * chunk_rows, MOE_CHUNK).astype(BF16)
            g = jnp.dot(x, wg_scr[...], preferred_element_type=F32)
            u = jnp.dot(x, wu_scr[...], preferred_element_type=F32)
            hmids.append((g * jax.nn.sigmoid(g) * u).astype(BF16))
        for h, hmid in enumerate(hmids):
            _store_token_tiles(ys_ref, jnp.dot(hmid, wd_scr[...], preferred_element_type=F32), h * chunk_rows)
        if n_chunks * chunk_rows < ys_ref.shape[0]:
            ys_ref[n_chunks * chunk_rows:, :] = jnp.zeros((ys_ref.shape[0] - n_chunks * chunk_rows, LANES), F32)

    for n_chunks in range(MOE_BLK // MOE_CHUNK + 1):
        lo, hi = (n_chunks - 1) * MOE_CHUNK, n_chunks * MOE_CHUNK
        pl.when((valid > lo) & (valid <= hi))(functools.partial(run, n_chunks))


def experts(xs_tiles, block_expert, nused, block_valid, w_gate, w_up, w_down):
    d = D_MODEL
    nblk = block_expert.shape[0]
    blk_rows = MOE_BLK * TILE_ROWS
    wmap = lambda i, be, nu, bv: (be[i], 0, 0)
    grid_spec = pltpu.PrefetchScalarGridSpec(
        num_scalar_prefetch=3,
        grid=(nblk,),
        in_specs=[pl.BlockSpec((blk_rows, LANES), lambda i, be, nu, bv: (jnp.minimum(i, nu[0] - 1), 0)),
                  pl.BlockSpec((1, d, D_EXPERT), wmap),
                  pl.BlockSpec((1, d, D_EXPERT), wmap),
                  pl.BlockSpec((1, D_EXPERT, d), wmap)],
        out_specs=pl.BlockSpec((blk_rows, LANES), lambda i, be, nu, bv: (i, 0)),
        scratch_shapes=[pltpu.VMEM((d, D_EXPERT), BF16),
                        pltpu.VMEM((d, D_EXPERT), BF16),
                        pltpu.VMEM((D_EXPERT, d), BF16)],
    )
    return pl.pallas_call(
        _experts_kernel,
        grid_spec=grid_spec,
        out_shape=jax.ShapeDtypeStruct((nblk * blk_rows, LANES), F32),
        compiler_params=_cparams(("arbitrary",)),
        name="experts",
    )(block_expert, nused, block_valid, xs_tiles, w_gate, w_up, w_down)


def _combine_kernel(dest_ref, ys_hbm, x1_ref, meta_ref, mod_ref, gf_ref, o_ref, ybuf, sem):
    i = pl.program_id(0)
    nstep = pl.num_programs(0)
    tc = TC_COMB
    nt = dest_ref.shape[0] // 2
    half_rows = tc * TILE_ROWS
    buf_rows = 2 * half_rows

    def gather(step, slot):
        def issue(c, carry):
            t0 = c * DMA_CHUNK
            for u in range(DMA_CHUNK):
                for k in range(2):
                    d = dest_ref[k * nt + step * tc + t0 + u]
                    pltpu.make_async_copy(
                        ys_hbm.at[pl.ds(d * TILE_ROWS, TILE_ROWS)],
                        ybuf.at[pl.ds(slot * buf_rows + k * half_rows + (t0 + u) * TILE_ROWS, TILE_ROWS)],
                        sem.at[slot]).start(priority=k)
            return carry

        lax.fori_loop(0, tc // DMA_CHUNK, issue, 0)

    @pl.when(i == 0)
    def _():
        gather(0, 0)

    @pl.when(i + 1 < nstep)
    def _():
        gather(i + 1, (i + 1) % 2)

    slot = i % 2
    start = pl.multiple_of(slot * buf_rows, buf_rows)
    pltpu.make_async_copy(ys_hbm.at[pl.ds(0, buf_rows)], ybuf.at[pl.ds(start, buf_rows)], sem.at[slot]).wait()
    meta = meta_ref[...]
    y0 = _load_token_tiles(ybuf, start, tc)
    y1 = _load_token_tiles(ybuf, start + half_rows, tc)
    moe = y0 * meta[:, 4:5] + y1 * meta[:, 5:6]
    x2 = x1_ref[...] + mod_ref[0][5:6] * moe
    o_ref[...] = _rms(x2, gf_ref[...])


def combine(dest_flat, ys, x1_flat, meta, mods, g_final, n_per_batch):
    nt, d = x1_flat.shape
    tc = TC_COMB
    per_b = n_per_batch // tc
    grid_spec = pltpu.PrefetchScalarGridSpec(
        num_scalar_prefetch=1,
        grid=(nt // tc,),
        in_specs=[pl.BlockSpec(memory_space=pl.ANY),
                  pl.BlockSpec((tc, d), lambda i, ds: (i, 0)),
                  pl.BlockSpec((tc, LANES), lambda i, ds: (i, 0)),
                  pl.BlockSpec((1, 8, d), lambda i, ds: (i // per_b, 0, 0)),
                  pl.BlockSpec((1, d), lambda i, ds: (0, 0))],
        out_specs=pl.BlockSpec((tc, d), lambda i, ds: (i, 0)),
        scratch_shapes=[pltpu.VMEM((2 * 2 * tc * TILE_ROWS, LANES), F32),
                        pltpu.SemaphoreType.DMA((2,))],
    )
    return pl.pallas_call(
        _combine_kernel,
        grid_spec=grid_spec,
        out_shape=jax.ShapeDtypeStruct((nt, d), F32),
        compiler_params=_cparams(("arbitrary",)),
        name="combine",
    )(dest_flat, ys, x1_flat, meta, mods, g_final.reshape(1, d))


def _slots_kernel(pstart_ref, metat_ref, dest_ref):
    eid = metat_ref[0:2, :]
    slot = metat_ref[2:4, :]
    for e in range(N_EXPERTS):
        slot = slot + jnp.where(eid == float(e), pstart_ref[e].astype(F32), 0.0)
    dest_ref[...] = slot.astype(jnp.int32)


def _dispatch_plan(metat, counts_row, nt):
    counts = counts_row[N_GROUPS:N_GROUPS + N_EXPERTS].astype(jnp.int32)
    pcounts = ((counts + MOE_BLK - 1) // MOE_BLK) * MOE_BLK
    pends = jnp.cumsum(pcounts)
    pstarts = pends - pcounts
    dest = pl.pallas_call(
        _slots_kernel,
        grid_spec=pltpu.PrefetchScalarGridSpec(
            num_scalar_prefetch=1, grid=(1,),
            in_specs=[pl.BlockSpec(metat.shape, lambda i, ps: (0, 0))],
            out_specs=pl.BlockSpec((2, nt), lambda i, ps: (0, 0))),
        out_shape=jax.ShapeDtypeStruct((2, nt), jnp.int32),
        compiler_params=_cparams(("arbitrary",)),
        name="slots",
    )(pstarts.astype(jnp.int32), metat).reshape(-1)
    nblk = (nt * 2) // MOE_BLK + N_EXPERTS
    first_slot = jnp.arange(nblk, dtype=jnp.int32) * MOE_BLK
    block_expert = jnp.minimum(
        jnp.sum((pends[None, :] <= first_slot[:, None]).astype(jnp.int32), axis=1), N_EXPERTS - 1)
    nused = (pends[-1] // MOE_BLK).astype(jnp.int32).reshape(1)
    seg_end = jnp.sum(jnp.where(block_expert[:, None] == jnp.arange(N_EXPERTS)[None, :],
                                (pstarts + counts)[None, :], 0), axis=1)
    block_valid = jnp.where(first_slot < pends[-1], jnp.clip(seg_end - first_slot, 0, MOE_BLK), 0)
    return (dest, pstarts.astype(jnp.int32), counts, block_expert.astype(jnp.int32), nused,
            block_valid.astype(jnp.int32))


def kernel(x, c, ctx, c_ctx, w_ada, b_ada, g_norm1, w_in, w_fmix, rpb, g_out, w_out, g_norm2,
           w_router_group, w_router_expert, w_gate, w_up, w_down, g_final):
    b, n, d = x.shape
    assert (b, n, d) == (c.shape[0], SEQ, D_MODEL) and w_ada.shape[0] == 1
    nt = b * n

    cond8 = jnp.zeros((8, d), F32).at[0:b].set(c).at[b].set(c_ctx)
    mod = adaln(cond8, w_ada[0], b_ada[0])
    mods = jnp.pad(mod[0:b].reshape(b, N_MOD, d), ((0, 0), (0, 2), (0, 0)))
    mod_ctx = jnp.pad(mod[b].reshape(N_MOD, d), ((0, 2), (0, 0)))

    w_in_b = w_in[0].astype(BF16)
    qr, qp, kr, v, a = in_proj(x, mods, g_norm1[0], w_in_b)
    kc, vc = ctx_proj(ctx, mod_ctx, g_norm1[0], w_in_b[:, D_FOURIER + D_NA:])

    fr = dft_cols(dft_rows(a))
    na = attention(qr, qp, kr, v, kc, vc, bias_tables(rpb[0]))

    w_router = jnp.concatenate(
        [w_router_group[0], w_router_expert[0],
         jnp.zeros((d, LANES - N_GROUPS - N_EXPERTS), F32)], axis=1).astype(BF16)
    x1, h2_tiles, meta, metat, cnt = out_proj(fr, na, x, mods, w_fmix[0].astype(BF16),
                                              w_out[0].astype(BF16), g_out[0], g_norm2[0], w_router)

    dest, pstarts, counts, block_expert, nused, block_valid = _dispatch_plan(metat, cnt[0], nt)
    xs_tiles = dispatch(h2_tiles, dest, pstarts, counts, block_expert.shape[0] * MOE_BLK)
    ys_tiles = experts(xs_tiles, block_expert, nused, block_valid, w_gate[0], w_up[0], w_down[0])
    out = combine(dest, ys_tiles, x1.reshape(nt, d), meta, mods, g_final, n)
    return out.reshape(b, n, d)
```

```python
import functools
import math

import numpy as np
import jax
import jax.numpy as jnp
from jax import lax
from jax.experimental import pallas as pl
from jax.experimental.pallas import tpu as pltpu

F32 = jnp.float32
BF16 = jnp.bfloat16

D_MODEL = 1024
GRID_W = 64
GRID_H = 128
SEQ = GRID_W * GRID_H
CTX_LEN = 256
D_FOURIER = 256
FOURIER_GROUP = 64
HEAD_DIM = 64
N_HEADS = 12
D_NA = N_HEADS * HEAD_DIM
N_PAIRS = N_HEADS // 2
NA_ROWS = 8
NA_COLS = 16
ROPE_THETA = 10000.0
ROPE_CHUNK = HEAD_DIM // 2
ROPE_HALF = ROPE_CHUNK // 2
N_GROUPS = 4
EXPERTS_PER_GROUP = 8
N_EXPERTS = N_GROUPS * EXPERTS_PER_GROUP
D_EXPERT = 512
N_MOD = 6
D_IN_PROJ = D_FOURIER + 3 * D_NA
EPS = 1e-6
LANES = 128
NEG = -1e30
LOG2E = math.log2(math.e)

ADALN_TN = 1536
TM_IN = 1024
ROW_PITCH = 72
TM_PROJ = 1024
OUT_CHUNK = 512
ATT_ROWS = 64
DFT_TW = 32
DFT_TK1 = 32
MOE_BLK = 512
MOE_CHUNK = 256
TD_DISP = 4096
TC_COMB = 256
DMA_CHUNK = 8
VMEM_LIMIT = 56 * 1024 * 1024


def _cparams(sem):
    return pltpu.CompilerParams(dimension_semantics=sem, vmem_limit_bytes=VMEM_LIMIT)


def _mxu_const(table):
    return jnp.asarray(table, F32).astype(BF16)


@functools.lru_cache(maxsize=None)
def _rope_tables():
    t = np.arange(SEQ)
    row, col = t // GRID_W, t % GRID_W
    lane = np.arange(LANES)
    d = lane % HEAD_DIM
    chunk = d // ROPE_CHUNK
    e = d % ROPE_CHUNK
    j = e % ROPE_HALF
    inv = ROPE_THETA ** (-(j.astype(np.float64)) / ROPE_HALF)
    pos = np.where(chunk[None, :] == 0, row[:, None], col[:, None]).astype(np.float64)
    ang = pos * inv[None, :]
    cos = np.cos(ang)
    sin = np.sin(ang)
    first = (e < ROPE_HALF)[None, :]
    s_first = np.where(first, -sin, 0.0)
    s_second = np.where(first, 0.0, sin)
    return (cos.astype(np.float32), s_first.astype(np.float32), s_second.astype(np.float32))


@functools.lru_cache(maxsize=None)
def _chan_dft():
    c = np.arange(FOURIER_GROUP)
    ang = 2.0 * np.pi * ((c[:, None] * c[None, :]) % FOURIER_GROUP) / FOURIER_GROUP
    eye = np.eye(D_FOURIER // FOURIER_GROUP)
    re = np.kron(eye, np.cos(ang))
    im = np.kron(eye, -np.sin(ang))
    return np.concatenate([re, im], axis=1).astype(np.float32)


@functools.lru_cache(maxsize=None)
def _row_dft():
    k1 = np.arange(GRID_H)[:, None]
    r = np.arange(GRID_H)[None, :]
    out = np.zeros((GRID_W, 2 * GRID_H, 2 * GRID_H), np.float32)
    for w in range(GRID_W):
        m = (k1 * (GRID_W * r + w)) % SEQ
        ang = 2.0 * np.pi * m / SEQ
        c, s = np.cos(ang), np.sin(ang)
        out[w] = np.block([[c, s], [-s, c]])
    return out


@functools.lru_cache(maxsize=None)
def _col_dft():
    k2 = np.arange(GRID_W)
    ang = 2.0 * np.pi * ((k2[:, None] * k2[None, :]) % GRID_W) / GRID_W
    scale = 1.0 / math.sqrt(SEQ * FOURIER_GROUP)
    return (np.concatenate([np.cos(ang), np.sin(ang)], axis=1) * scale).astype(np.float32)


@functools.lru_cache(maxsize=None)
def _bias_index():
    c = np.arange(GRID_W)
    start = np.clip(c - NA_COLS // 2, 0, GRID_W - NA_COLS)
    valid = (c[None, :] >= start[:, None]) & (c[None, :] < start[:, None] + NA_COLS)
    dc = np.clip(c[None, :] - c[:, None] + (NA_COLS - 1), 0, 2 * NA_COLS - 2)
    return dc.astype(np.int32), valid


@functools.lru_cache(maxsize=None)
def _strict_lower(n):
    return np.tril(np.ones((n, n), np.float32), k=-1)


def _adaln_kernel(c_ref, w_ref, b_ref, o_ref):
    c = c_ref[...]
    s = c * jax.nn.sigmoid(c)
    o_ref[...] = jnp.dot(s.astype(BF16), w_ref[...].astype(BF16), preferred_element_type=F32) + b_ref[...]


def adaln(cond8, w, b):
    n = w.shape[1]
    tn = ADALN_TN
    return pl.pallas_call(
        _adaln_kernel,
        grid=(n // tn,),
        in_specs=[pl.BlockSpec((8, D_MODEL), lambda j: (0, 0)),
                  pl.BlockSpec((D_MODEL, tn), lambda j: (0, j)),
                  pl.BlockSpec((1, tn), lambda j: (0, j))],
        out_specs=pl.BlockSpec((8, tn), lambda j: (0, j)),
        out_shape=jax.ShapeDtypeStruct((8, n), F32),
        compiler_params=_cparams(("arbitrary",)),
        name="adaln",
    )(cond8, w, b.reshape(1, n))


def _norm_mod(x, g, shift, scale):
    ms = jnp.mean(x * x, axis=-1, keepdims=True)
    return (x * lax.rsqrt(ms + EPS) * g) * (1.0 + scale) + shift


def _in_proj_kernel(x_ref, mod_ref, g_ref, w_ref, cs_ref, cos_ref, s1_ref, s2_ref,
                    qr_ref, qp_ref, kr_ref, v_ref, a_ref, h_scr, a_scr):
    m = mod_ref[0]
    h_scr[...] = _norm_mod(x_ref[0], g_ref[...], m[0:1], m[1:2]).astype(BF16)
    cos, s1, s2 = cos_ref[...], s1_ref[...], s2_ref[...]

    def rope(t):
        return (t * cos + pltpu.roll(t, LANES - ROPE_HALF, axis=1) * s1
                + pltpu.roll(t, ROPE_HALF, axis=1) * s2)

    f = jnp.dot(h_scr[...], w_ref[:, 0:D_FOURIER], preferred_element_type=F32)
    a = jnp.dot(f.astype(BF16), cs_ref[...], preferred_element_type=F32)
    n_planes = 2 * D_FOURIER // LANES
    tile_rows = a.shape[0] // GRID_W
    for p in range(n_planes):
        for r in range(tile_rows):
            a_scr[p, r * ROW_PITCH:r * ROW_PITCH + GRID_W, :] = a[r * GRID_W:(r + 1) * GRID_W,
                                                                LANES * p:LANES * (p + 1)]
    for w in range(GRID_W):
        for p in range(n_planes):
            slab = a_scr[p, pl.ds(w, tile_rows, stride=ROW_PITCH), :]
            lo = w * D_FOURIER + (p % 2) * LANES
            a_ref[0, p // 2, :, lo:lo + LANES] = slab.astype(BF16)

    scale = HEAD_DIM ** -0.5 * LOG2E
    wide = 2 * LANES
    for c in range(D_NA // wide):
        lo = D_FOURIER + wide * c
        q = jnp.dot(h_scr[...], w_ref[:, lo:lo + wide], preferred_element_type=F32)
        k = jnp.dot(h_scr[...], w_ref[:, lo + D_NA:lo + D_NA + wide], preferred_element_type=F32)
        v = jnp.dot(h_scr[...], w_ref[:, lo + 2 * D_NA:lo + 2 * D_NA + wide], preferred_element_type=F32)
        v_ref[0, :, wide * c:wide * (c + 1)] = v.astype(BF16)
        for s in range(2):
            sl = slice(LANES * s, LANES * (s + 1))
            ol = slice(wide * c + LANES * s, wide * c + LANES * (s + 1))
            qs, ks = q[:, sl], k[:, sl]
            qp_ref[0, :, ol] = (qs * scale).astype(BF16)
            qr_ref[0, :, ol] = (rope(qs) * scale).astype(BF16)
            kr_ref[0, :, ol] = rope(ks).astype(BF16)


def in_proj(x, mods, g1, w_in_bf16):
    b, n, d = x.shape
    tm = TM_IN
    cos, s1, s2 = _rope_tables()
    cs = _mxu_const(_chan_dft())
    tok = lambda bi, i: (bi, i, 0)
    const2 = lambda bi, i: (0, 0)
    tab = pl.BlockSpec((tm, LANES), lambda bi, i: (i, 0))
    qkv_shape = jax.ShapeDtypeStruct((b, n, D_NA), BF16)
    qkv_spec = pl.BlockSpec((1, tm, D_NA), tok)
    return pl.pallas_call(
        _in_proj_kernel,
        grid=(b, n // tm),
        in_specs=[pl.BlockSpec((1, tm, d), tok),
                  pl.BlockSpec((1, 8, d), lambda bi, i: (bi, 0, 0)),
                  pl.BlockSpec((1, d), const2),
                  pl.BlockSpec((d, D_IN_PROJ), const2),
                  pl.BlockSpec((D_FOURIER, 2 * D_FOURIER), const2),
                  tab, tab, tab],
        out_specs=[qkv_spec, qkv_spec, qkv_spec, qkv_spec,
                   pl.BlockSpec((1, 2, tm // GRID_W, GRID_W * D_FOURIER), lambda bi, i: (bi, 0, i, 0))],
        out_shape=[qkv_shape, qkv_shape, qkv_shape, qkv_shape,
                   jax.ShapeDtypeStruct((b, 2, n // GRID_W, GRID_W * D_FOURIER), BF16)],
        scratch_shapes=[pltpu.VMEM((tm, d), BF16),
                        pltpu.VMEM((2 * D_FOURIER // LANES, (tm // GRID_W) * ROW_PITCH, LANES), F32)],
        compiler_params=_cparams(("arbitrary", "arbitrary")),
        name="in_proj",
    )(x, mods, g1.reshape(1, d), w_in_bf16, cs, jnp.asarray(cos), jnp.asarray(s1), jnp.asarray(s2))


def _ctx_proj_kernel(x_ref, mod_ref, g_ref, w_ref, k_ref, v_ref):
    m = mod_ref[...]
    h = _norm_mod(x_ref[0], g_ref[...], m[0:1], m[1:2]).astype(BF16)
    k_ref[0] = jnp.dot(h, w_ref[:, 0:D_NA], preferred_element_type=F32).astype(BF16)
    v_ref[0] = jnp.dot(h, w_ref[:, D_NA:2 * D_NA], preferred_element_type=F32).astype(BF16)


def ctx_proj(ctx, mod_ctx, g1, w_kv_bf16):
    b, l, d = ctx.shape
    shape = jax.ShapeDtypeStruct((b, l, D_NA), BF16)
    spec = pl.BlockSpec((1, l, D_NA), lambda bi: (bi, 0, 0))
    return pl.pallas_call(
        _ctx_proj_kernel,
        grid=(b,),
        in_specs=[pl.BlockSpec((1, l, d), lambda bi: (bi, 0, 0)),
                  pl.BlockSpec((8, d), lambda bi: (0, 0)),
                  pl.BlockSpec((1, d), lambda bi: (0, 0)),
                  pl.BlockSpec((d, 2 * D_NA), lambda bi: (0, 0))],
        out_specs=[spec, spec],
        out_shape=[shape, shape],
        compiler_params=_cparams(("arbitrary",)),
        name="ctx_proj",
    )(ctx, mod_ctx, g1.reshape(1, d), w_kv_bf16)


N_PLANES = D_FOURIER // LANES
K1_PITCH = GRID_H + 8
K2_PITCH = GRID_W + 8


def _dft_rows_kernel(a_ref, g_ref, z_ref, z_scr):
    for j in range(DFT_TW):
        sl = slice(D_FOURIER * j, D_FOURIER * (j + 1))
        rhs = jnp.concatenate([a_ref[0, 0, :, sl], a_ref[0, 1, :, sl]], axis=0)
        z = jnp.dot(g_ref[j], rhs, preferred_element_type=F32)
        for c in range(2):
            for p in range(N_PLANES):
                z_scr[c * N_PLANES + p, j * K1_PITCH:j * K1_PITCH + GRID_H, :] = (
                    z[c * GRID_H:(c + 1) * GRID_H, LANES * p:LANES * (p + 1)])
    for k1 in range(GRID_H):
        for c in range(2):
            for p in range(N_PLANES):
                slab = z_scr[c * N_PLANES + p, pl.ds(k1, DFT_TW, stride=K1_PITCH), :]
                lo = k1 * D_FOURIER + p * LANES
                z_ref[0, c, :, lo:lo + LANES] = slab.astype(BF16)


def dft_rows(a):
    b = a.shape[0]
    g = _mxu_const(_row_dft())
    return pl.pallas_call(
        _dft_rows_kernel,
        grid=(GRID_W // DFT_TW, b),
        in_specs=[pl.BlockSpec((1, 2, GRID_H, DFT_TW * D_FOURIER), lambda j, bi: (bi, 0, 0, j)),
                  pl.BlockSpec((DFT_TW, 2 * GRID_H, 2 * GRID_H), lambda j, bi: (j, 0, 0))],
        out_specs=pl.BlockSpec((1, 2, DFT_TW, GRID_H * D_FOURIER), lambda j, bi: (bi, 0, j, 0)),
        out_shape=jax.ShapeDtypeStruct((b, 2, GRID_W, GRID_H * D_FOURIER), BF16),
        scratch_shapes=[pltpu.VMEM((2 * N_PLANES, DFT_TW * K1_PITCH, LANES), F32)],
        compiler_params=_cparams(("arbitrary", "arbitrary")),
        name="dft_rows",
    )(a, g)


def _dft_cols_kernel(z_ref, cs_ref, o_ref, o_scr):
    for j in range(DFT_TK1):
        sl = slice(D_FOURIER * j, D_FOURIER * (j + 1))
        rhs = jnp.concatenate([z_ref[0, 0, :, sl], z_ref[0, 1, :, sl]], axis=0)
        out = jnp.dot(cs_ref[...], rhs, preferred_element_type=F32)
        for p in range(N_PLANES):
            o_scr[p, j * K2_PITCH:j * K2_PITCH + GRID_W, :] = out[:, LANES * p:LANES * (p + 1)]
    for k2 in range(GRID_W):
        for p in range(N_PLANES):
            slab = o_scr[p, pl.ds(k2, DFT_TK1, stride=K2_PITCH), :]
            o_ref[0, k2, :, LANES * p:LANES * (p + 1)] = slab.astype(BF16)


def dft_cols(z):
    b = z.shape[0]
    cs = _mxu_const(_col_dft())
    out = pl.pallas_call(
        _dft_cols_kernel,
        grid=(b, GRID_H // DFT_TK1),
        in_specs=[pl.BlockSpec((1, 2, GRID_W, DFT_TK1 * D_FOURIER), lambda bi, j: (bi, 0, 0, j)),
                  pl.BlockSpec((GRID_W, 2 * GRID_W), lambda bi, j: (0, 0))],
        out_specs=pl.BlockSpec((1, GRID_W, DFT_TK1, D_FOURIER), lambda bi, j: (bi, 0, j, 0)),
        out_shape=jax.ShapeDtypeStruct((b, GRID_W, GRID_H, D_FOURIER), BF16),
        scratch_shapes=[pltpu.VMEM((N_PLANES, DFT_TK1 * K2_PITCH, LANES), F32)],
        compiler_params=_cparams(("arbitrary", "arbitrary")),
        name="dft_cols",
    )(z, cs)
    return out.reshape(b, SEQ, D_FOURIER)


ROW_KEYS = NA_ROWS * GRID_W
ROW_TILES = ROW_KEYS // LANES
N_DR = 2 * NA_ROWS - 2
SOFT_ROWS = 32
ATT_PAIRS = ATT_ROWS // 2


def _attention_kernel(qr_ref, qp_ref, k_ref, v_ref, kc_ref, vc_ref, tb_ref, o_ref,
                      sc_scr, pc_scr, s_scr, p_scr):
    rb = pl.program_id(2)
    first1 = lax.broadcasted_iota(jnp.int32, (GRID_W, LANES), 1) < HEAD_DIM
    nt = (((1,), (1,)), ((), ()))

    def window(jp, i):
        r = rb * ATT_ROWS + 2 * jp + i
        rs = jnp.clip(r - NA_ROWS // 2, 0, GRID_H - NA_ROWS)
        return r, rs, pl.multiple_of(rs * GRID_W, GRID_W)

    def split_heads(q1):
        z1 = jnp.zeros_like(q1)
        return jnp.concatenate([jnp.where(first1, q1, z1), jnp.where(first1, z1, q1)], axis=0)

    def scores(jp):
        qp_rows = []
        for i in range(2):
            _, _, koff = window(jp, i)
            qs = slice((2 * jp + i) * GRID_W, (2 * jp + i + 1) * GRID_W)
            kw = k_ref[0, pl.ds(koff, ROW_KEYS), :]
            s_scr[jp, i * 2 * GRID_W:(i + 1) * 2 * GRID_W, :] = lax.dot_general(
                split_heads(qr_ref[0, qs, :]), kw, nt, preferred_element_type=F32)
            qp_rows.append(split_heads(qp_ref[0, qs, :]))
        sc_scr[jp] = lax.dot_general(jnp.concatenate(qp_rows, axis=0), kc_ref[0], nt,
                                     preferred_element_type=F32)

    def softmax(jp):
        for c in range(4):
            i, hh = c // 2, c % 2
            r, rs, _ = window(jp, i)
            d0 = rs - r + NA_ROWS - 1
            for h in range(GRID_W // SOFT_ROWS):
                lo = h * SOFT_ROWS
                rows = slice(c * GRID_W + lo, c * GRID_W + lo + SOFT_ROWS)
                tiles = [s_scr[jp, rows, t * LANES:(t + 1) * LANES]
                         + tb_ref[0, hh, d0 + 2 * t, lo:lo + SOFT_ROWS, :] for t in range(ROW_TILES)]
                sc = sc_scr[jp, rows, :]
                mt = jnp.maximum(sc[:, :LANES], sc[:, LANES:])
                for tl in tiles:
                    mt = jnp.maximum(mt, tl)
                m = jnp.max(mt, axis=1, keepdims=True)
                pc_scr[jp, rows, :] = jnp.exp2(sc - m).astype(BF16)
                for t, tl in enumerate(tiles):
                    p_scr[jp, rows, t * LANES:(t + 1) * LANES] = jnp.exp2(tl - m).astype(BF16)

    def weighted_values(jp):
        vc = jnp.concatenate([vc_ref[0], jnp.ones((CTX_LEN, LANES), BF16)], axis=1)
        oc = jnp.dot(pc_scr[jp], vc, preferred_element_type=F32)
        for i in range(2):
            _, _, koff = window(jp, i)
            rows = slice(i * 2 * GRID_W, (i + 1) * 2 * GRID_W)
            vw = jnp.concatenate([v_ref[0, pl.ds(koff, ROW_KEYS), :], jnp.ones((ROW_KEYS, LANES), BF16)],
                                 axis=1)
            o = jnp.dot(p_scr[jp, rows, :], vw, preferred_element_type=F32) + oc[rows]
            oa, ob = o[:GRID_W], o[GRID_W:]
            out = jnp.where(first1, oa[:, :LANES] / oa[:, LANES:], ob[:, :LANES] / ob[:, LANES:])
            o_ref[0, (2 * jp + i) * GRID_W:(2 * jp + i + 1) * GRID_W, :] = out.astype(BF16)

    scores(0)
    scores(1)
    softmax(0)
    for jp in range(ATT_PAIRS):
        if jp + 2 < ATT_PAIRS:
            scores(jp + 2)
        if jp + 1 < ATT_PAIRS:
            softmax(jp + 1)
        weighted_values(jp)


def bias_tables(rpb):
    dc, valid = _bias_index()
    n_dc = 2 * NA_COLS - 1
    onehot = (dc.reshape(1, -1) == np.arange(n_dc).reshape(-1, 1)).astype(np.float32)
    t = jnp.dot(rpb.reshape(-1, n_dc), jnp.asarray(onehot), precision=lax.Precision.HIGHEST)
    t = t.reshape(N_HEADS, 2 * NA_ROWS - 1, GRID_W, GRID_W)
    tb = pl.pallas_call(
        _bias_pairs_kernel,
        grid=(N_HEADS,),
        in_specs=[pl.BlockSpec((GRID_W, GRID_W), lambda h: (0, 0)),
                  pl.BlockSpec((1, N_DR + 1, GRID_W, GRID_W), lambda h: (h, 0, 0, 0))],
        out_specs=pl.BlockSpec((1, N_DR, GRID_W, LANES), lambda h: (h, 0, 0, 0)),
        out_shape=jax.ShapeDtypeStruct((N_HEADS, N_DR, GRID_W, LANES), F32),
        compiler_params=_cparams(("arbitrary",)),
        name="bias_pairs",
    )(jnp.asarray(valid.astype(np.float32)), t)
    return tb.reshape(N_PAIRS, 2, N_DR, GRID_W, LANES)


def _bias_pairs_kernel(valid_ref, t_ref, o_ref):
    inside = valid_ref[...] > 0.5
    rows = [jnp.where(inside, LOG2E * t_ref[0, d], NEG) for d in range(N_DR + 1)]
    for d in range(N_DR):
        o_ref[0, d] = jnp.concatenate([rows[d], rows[d + 1]], axis=1)


def attention(qr, qp, kr, v, kc, vc, tb):
    b, n, _ = qr.shape
    tq = ATT_ROWS * GRID_W
    qspec = pl.BlockSpec((1, tq, LANES), lambda bi, hp, i: (bi, i, hp))
    kspec = pl.BlockSpec((1, n, LANES), lambda bi, hp, i: (bi, 0, hp))
    cspec = pl.BlockSpec((1, CTX_LEN, LANES), lambda bi, hp, i: (bi, 0, hp))
    return pl.pallas_call(
        _attention_kernel,
        grid=(b, N_PAIRS, GRID_H // ATT_ROWS),
        in_specs=[qspec, qspec, kspec, kspec, cspec, cspec,
                  pl.BlockSpec((1, 2, N_DR, GRID_W, LANES), lambda bi, hp, i: (hp, 0, 0, 0, 0))],
        out_specs=qspec,
        out_shape=jax.ShapeDtypeStruct((b, n, D_NA), BF16),
        scratch_shapes=[pltpu.VMEM((ATT_PAIRS, 4 * GRID_W, CTX_LEN), F32),
                        pltpu.VMEM((ATT_PAIRS, 4 * GRID_W, CTX_LEN), BF16),
                        pltpu.VMEM((ATT_PAIRS, 4 * GRID_W, ROW_KEYS), F32),
                        pltpu.VMEM((ATT_PAIRS, 4 * GRID_W, ROW_KEYS), BF16)],
        compiler_params=_cparams(("arbitrary", "arbitrary", "arbitrary")),
        name="attention",
    )(qr, qp, kr, v, kc, vc, tb)


def _rms(x, g):
    ms = jnp.mean(x * x, axis=-1, keepdims=True)
    return x * lax.rsqrt(ms + EPS) * g


TILE_ROWS = D_MODEL // LANES


def _store_token_tiles(ref, val, start=0):
    rows = val.shape[0]
    for j in range(TILE_ROWS):
        ref[pl.ds(start + j, rows, stride=TILE_ROWS), :] = val[:, LANES * j:LANES * (j + 1)]


def _load_token_tiles(ref, start, rows):
    return jnp.concatenate(
        [ref[pl.ds(start + j, rows, stride=TILE_ROWS), :] for j in range(TILE_ROWS)], axis=1)


def _out_proj_kernel(fr_ref, na_ref, x_ref, mod_ref, wf_ref, wo_ref, go_ref, g2_ref, wr_ref, tri_ref,
                     x1_ref, h2_ref, meta_ref, metat_ref, cnt_ref, run_scr):
    @pl.when((pl.program_id(0) == 0) & (pl.program_id(1) == 0))
    def _():
        run_scr[...] = jnp.zeros_like(run_scr)

    m = mod_ref[0]
    go = go_ref[...]
    rows_c = OUT_CHUNK
    lane = lax.broadcasted_iota(jnp.int32, (rows_c, LANES), 1).astype(F32)
    ninf = jnp.float32(-jnp.inf)

    def argmax_first(vals):
        mx = jnp.max(vals, axis=1, keepdims=True)
        idx = jnp.min(jnp.where(vals == mx, lane, float(LANES)), axis=1, keepdims=True)
        return mx, idx

    run = run_scr[...]
    for c in range(x_ref.shape[1] // rows_c):
        rows = slice(c * rows_c, (c + 1) * rows_c)
        fo = jnp.dot(fr_ref[0, rows, :], wf_ref[...], preferred_element_type=F32)
        fn = _rms(fo, go[:, :D_FOURIER]).astype(BF16)
        nn = _rms(na_ref[0, rows, :].astype(F32), go[:, D_FOURIER:]).astype(BF16)
        y = (jnp.dot(fn, wo_ref[0:D_FOURIER, :], preferred_element_type=F32)
             + jnp.dot(nn, wo_ref[D_FOURIER:, :], preferred_element_type=F32))
        x1 = x_ref[0, rows, :] + m[2:3] * y
        x1_ref[0, rows, :] = x1
        h2 = _norm_mod(x1, g2_ref[...], m[3:4], m[4:5])
        _store_token_tiles(h2_ref, h2, c * rows_c * TILE_ROWS)
        logits = jnp.dot(h2.astype(BF16), wr_ref[...], preferred_element_type=F32)

        lg = jnp.where(lane < N_GROUPS, logits, ninf)
        gmax, gidx = argmax_first(lg)
        pg = 1.0 / jnp.sum(jnp.exp(lg - gmax), axis=1, keepdims=True)
        lo = N_GROUPS + EXPERTS_PER_GROUP * gidx
        le = jnp.where((lane >= lo) & (lane < lo + EXPERTS_PER_GROUP), logits, ninf)
        e1, i1 = argmax_first(le)
        e2, i2 = argmax_first(jnp.where(lane == i1, ninf, le))
        dd = jnp.exp(e2 - e1)
        gate1 = pg / (1.0 + dd)
        gate2 = pg * dd / (1.0 + dd)

        hot1 = lane == i1
        hot2 = lane == i2
        onehot = jnp.where(hot1 | hot2, 1.0, 0.0)
        cnt = jnp.dot(tri_ref[...], onehot.astype(BF16), preferred_element_type=F32) + run
        rank1 = jnp.sum(jnp.where(hot1, cnt, 0.0), axis=1, keepdims=True)
        rank2 = jnp.sum(jnp.where(hot2, cnt, 0.0), axis=1, keepdims=True)
        run = run + jnp.sum(onehot, axis=0, keepdims=True)

        meta = jnp.where(lane == 0, i1 - N_GROUPS,
               jnp.where(lane == 1, i2 - N_GROUPS,
               jnp.where(lane == 2, rank1,
               jnp.where(lane == 3, rank2,
               jnp.where(lane == 4, gate1,
               jnp.where(lane == 5, gate2, 0.0))))))
        meta_ref[rows, :] = meta
        metat_ref[:, rows] = jnp.transpose(meta)[0:8, :]
    run_scr[...] = run
    cnt_ref[...] = jnp.broadcast_to(run, cnt_ref.shape)


def out_proj(fr, na, x, mods, w_fmix_bf16, w_out_bf16, g_out, g2, w_router_bf16):
    b, n, d = x.shape
    tm = TM_PROJ
    steps = n // tm
    tok = lambda bi, i: (bi, i, 0)
    const2 = lambda bi, i: (0, 0)
    flat = lambda bi, i: (bi * steps + i, 0)
    tri = _mxu_const(_strict_lower(OUT_CHUNK))
    return pl.pallas_call(
        _out_proj_kernel,
        grid=(b, steps),
        in_specs=[pl.BlockSpec((1, tm, D_FOURIER), tok),
                  pl.BlockSpec((1, tm, D_NA), tok),
                  pl.BlockSpec((1, tm, d), tok),
                  pl.BlockSpec((1, 8, d), lambda bi, i: (bi, 0, 0)),
                  pl.BlockSpec((D_FOURIER, D_FOURIER), const2),
                  pl.BlockSpec((d, d), const2),
                  pl.BlockSpec((1, d), const2),
                  pl.BlockSpec((1, d), const2),
                  pl.BlockSpec((d, LANES), const2),
                  pl.BlockSpec((OUT_CHUNK, OUT_CHUNK), const2)],
        out_specs=[pl.BlockSpec((1, tm, d), tok),
                   pl.BlockSpec((tm * TILE_ROWS, LANES), flat),
                   pl.BlockSpec((tm, LANES), flat),
                   pl.BlockSpec((8, tm), lambda bi, i: (0, bi * steps + i)),
                   pl.BlockSpec((8, LANES), const2)],
        out_shape=[jax.ShapeDtypeStruct((b, n, d), F32),
                   jax.ShapeDtypeStruct((b * n * TILE_ROWS, LANES), F32),
                   jax.ShapeDtypeStruct((b * n, LANES), F32),
                   jax.ShapeDtypeStruct((8, b * n), F32),
                   jax.ShapeDtypeStruct((8, LANES), F32)],
        scratch_shapes=[pltpu.VMEM((1, LANES), F32)],
        compiler_params=_cparams(("arbitrary", "arbitrary")),
        name="out_proj",
    )(fr, na, x, mods, w_fmix_bf16, w_out_bf16, g_out.reshape(1, d), g2.reshape(1, d), w_router_bf16, tri)


def _dispatch_kernel(dest_ref, pstart_ref, count_ref, h2_ref, xs_hbm, zero_scr, pad_rows, sem, pad_sem):
    i = pl.program_id(0)
    nt = dest_ref.shape[0] // 2

    @pl.when(i == 0)
    def _():
        zero_scr[...] = jnp.zeros_like(zero_scr)

        def per_expert(e, npad):
            lo = pstart_ref[e] + count_ref[e]
            mid = pstart_ref[e] + ((count_ref[e] + MOE_CHUNK - 1) // MOE_CHUNK) * MOE_CHUNK
            hi = pstart_ref[e] + ((count_ref[e] + MOE_BLK - 1) // MOE_BLK) * MOE_BLK

            def fill(s, carry):
                pltpu.make_async_copy(zero_scr.at[pl.ds(0, TILE_ROWS)],
                                      xs_hbm.at[pl.ds(s * TILE_ROWS, TILE_ROWS)], pad_sem).start()
                return carry

            def fill_chunk(s, carry):
                pltpu.make_async_copy(zero_scr.at[pl.ds(0, MOE_CHUNK * TILE_ROWS)],
                                      xs_hbm.at[pl.ds(mid * TILE_ROWS + s * MOE_CHUNK * TILE_ROWS,
                                                      MOE_CHUNK * TILE_ROWS)], pad_sem).start()
                return carry

            lax.fori_loop(lo, mid, fill, 0)
            lax.fori_loop(0, (hi - mid) // MOE_CHUNK, fill_chunk, 0)
            return npad + (hi - lo)

        npad = lax.fori_loop(0, N_EXPERTS, per_expert, 0)

        blk_rows = MOE_BLK * TILE_ROWS
        first_free = (pstart_ref[N_EXPERTS - 1] + count_ref[N_EXPERTS - 1] + MOE_BLK - 1) // MOE_BLK
        n_blocks = xs_hbm.shape[0] // blk_rows

        def fill_block(bk, carry):
            pltpu.make_async_copy(zero_scr, xs_hbm.at[pl.ds(bk * blk_rows, blk_rows)], pad_sem).start()
            return carry

        lax.fori_loop(first_free, n_blocks, fill_block, 0)
        pad_rows[0] = npad * TILE_ROWS + (n_blocks - first_free) * blk_rows

    def issue(c, carry):
        t0 = c * DMA_CHUNK
        for u in range(DMA_CHUNK):
            for k in range(2):
                d = dest_ref[k * nt + i * TD_DISP + t0 + u]
                pltpu.make_async_copy(h2_ref.at[pl.ds((t0 + u) * TILE_ROWS, TILE_ROWS)],
                                      xs_hbm.at[pl.ds(d * TILE_ROWS, TILE_ROWS)], sem).start(priority=k)
        return carry

    lax.fori_loop(0, TD_DISP // DMA_CHUNK, issue, 0)
    rows = 2 * TD_DISP * TILE_ROWS
    pltpu.make_async_copy(xs_hbm.at[pl.ds(0, rows)], xs_hbm.at[pl.ds(0, rows)], sem).wait()

    @pl.when((i == pl.num_programs(0) - 1) & (pad_rows[0] > 0))
    def _():
        n = pad_rows[0]
        pltpu.make_async_copy(xs_hbm.at[pl.ds(0, n)], xs_hbm.at[pl.ds(0, n)], pad_sem).wait()


def dispatch(h2_tiles, dest, pstarts, counts, n_slots):
    nt = dest.shape[0] // 2
    grid_spec = pltpu.PrefetchScalarGridSpec(
        num_scalar_prefetch=3,
        grid=(nt // TD_DISP,),
        in_specs=[pl.BlockSpec((TD_DISP * TILE_ROWS, LANES), lambda i, ds, ps, ct: (i, 0))],
        out_specs=pl.BlockSpec(memory_space=pl.ANY),
        scratch_shapes=[pltpu.VMEM((MOE_BLK * TILE_ROWS, LANES), F32),
                        pltpu.SMEM((1,), jnp.int32),
                        pltpu.SemaphoreType.DMA(()),
                        pltpu.SemaphoreType.DMA(())],
    )
    return pl.pallas_call(
        _dispatch_kernel,
        grid_spec=grid_spec,
        out_shape=jax.ShapeDtypeStruct((n_slots * TILE_ROWS, LANES), F32),
        compiler_params=_cparams(("arbitrary",)),
        name="dispatch",
    )(dest, pstarts, counts, h2_tiles)


def _experts_kernel(be_ref, nused_ref, valid_ref, xs_ref, wg_ref, wu_ref, wd_ref, ys_ref,
                    wg_scr, wu_scr, wd_scr):
    i = pl.program_id(0)
    valid = valid_ref[i]
    changed = (i == 0) | (be_ref[i] != be_ref[jnp.maximum(i - 1, 0)])
    chunk_rows = MOE_CHUNK * TILE_ROWS

    @pl.when(changed & (valid > 0))
    def _():
        wg_scr[...] = wg_ref[0].astype(BF16)
        wu_scr[...] = wu_ref[0].astype(BF16)
        wd_scr[...] = wd_ref[0].astype(BF16)

    def run(n_chunks):
        hmids = []
        for h in range(n_chunks):
            x = _load_token_tiles(xs_ref, h * chunk_rows, MOE_CHUNK).astype(BF16)
            g = jnp.dot(x, wg_scr[...], preferred_element_type=F32)
            u = jnp.dot(x, wu_scr[...], preferred_element_type=F32)
            hmids.append((g * jax.nn.sigmoid(g) * u).astype(BF16))
        for h, hmid in enumerate(hmids):
            _store_token_tiles(ys_ref, jnp.dot(hmid, wd_scr[...], preferred_element_type=F32), h * chunk_rows)
        if n_chunks * chunk_rows < ys_ref.shape[0]:
            ys_ref[n_chunks * chunk_rows:, :] = jnp.zeros((ys_ref.shape[0] - n_chunks * chunk_rows, LANES), F32)

    for n_chunks in range(MOE_BLK // MOE_CHUNK + 1):
        lo, hi = (n_chunks - 1) * MOE_CHUNK, n_chunks * MOE_CHUNK
        pl.when((valid > lo) & (valid <= hi))(functools.partial(run, n_chunks))


def experts(xs_tiles, block_expert, nused, block_valid, w_gate, w_up, w_down):
    d = D_MODEL
    nblk = block_expert.shape[0]
    blk_rows = MOE_BLK * TILE_ROWS
    wmap = lambda i, be, nu, bv: (be[i], 0, 0)
    grid_spec = pltpu.PrefetchScalarGridSpec(
        num_scalar_prefetch=3,
        grid=(nblk,),
        in_specs=[pl.BlockSpec((blk_rows, LANES), lambda i, be, nu, bv: (jnp.minimum(i, nu[0] - 1), 0)),
                  pl.BlockSpec((1, d, D_EXPERT), wmap),
                  pl.BlockSpec((1, d, D_EXPERT), wmap),
                  pl.BlockSpec((1, D_EXPERT, d), wmap)],
        out_specs=pl.BlockSpec((blk_rows, LANES), lambda i, be, nu, bv: (i, 0)),
        scratch_shapes=[pltpu.VMEM((d, D_EXPERT), BF16),
                        pltpu.VMEM((d, D_EXPERT), BF16),
                        pltpu.VMEM((D_EXPERT, d), BF16)],
    )
    return pl.pallas_call(
        _experts_kernel,
        grid_spec=grid_spec,
        out_shape=jax.ShapeDtypeStruct((nblk * blk_rows, LANES), F32),
        compiler_params=_cparams(("arbitrary",)),
        name="experts",
    )(block_expert, nused, block_valid, xs_tiles, w_gate, w_up, w_down)


def _combine_kernel(dest_ref, ys_hbm, x1_ref, meta_ref, mod_ref, gf_ref, o_ref, ybuf, sem):
    i = pl.program_id(0)
    nstep = pl.num_programs(0)
    tc = TC_COMB
    nt = dest_ref.shape[0] // 2
    half_rows = tc * TILE_ROWS
    buf_rows = 2 * half_rows

    def gather(step, slot):
        def issue(c, carry):
            t0 = c * DMA_CHUNK
            for u in range(DMA_CHUNK):
                for k in range(2):
                    d = dest_ref[k * nt + step * tc + t0 + u]
                    pltpu.make_async_copy(
                        ys_hbm.at[pl.ds(d * TILE_ROWS, TILE_ROWS)],
                        ybuf.at[pl.ds(slot * buf_rows + k * half_rows + (t0 + u) * TILE_ROWS, TILE_ROWS)],
                        sem.at[slot]).start(priority=k)
            return carry

        lax.fori_loop(0, tc // DMA_CHUNK, issue, 0)

    @pl.when(i == 0)
    def _():
        gather(0, 0)

    @pl.when(i + 1 < nstep)
    def _():
        gather(i + 1, (i + 1) % 2)

    slot = i % 2
    start = pl.multiple_of(slot * buf_rows, buf_rows)
    pltpu.make_async_copy(ys_hbm.at[pl.ds(0, buf_rows)], ybuf.at[pl.ds(start, buf_rows)], sem.at[slot]).wait()
    meta = meta_ref[...]
    y0 = _load_token_tiles(ybuf, start, tc)
    y1 = _load_token_tiles(ybuf, start + half_rows, tc)
    moe = y0 * meta[:, 4:5] + y1 * meta[:, 5:6]
    x2 = x1_ref[...] + mod_ref[0][5:6] * moe
    o_ref[...] = _rms(x2, gf_ref[...])


def combine(dest_flat, ys, x1_flat, meta, mods, g_final, n_per_batch):
    nt, d = x1_flat.shape
    tc = TC_COMB
    per_b = n_per_batch // tc
    grid_spec = pltpu.PrefetchScalarGridSpec(
        num_scalar_prefetch=1,
        grid=(nt // tc,),
        in_specs=[pl.BlockSpec(memory_space=pl.ANY),
                  pl.BlockSpec((tc, d), lambda i, ds: (i, 0)),
                  pl.BlockSpec((tc, LANES), lambda i, ds: (i, 0)),
                  pl.BlockSpec((1, 8, d), lambda i, ds: (i // per_b, 0, 0)),
                  pl.BlockSpec((1, d), lambda i, ds: (0, 0))],
        out_specs=pl.BlockSpec((tc, d), lambda i, ds: (i, 0)),
        scratch_shapes=[pltpu.VMEM((2 * 2 * tc * TILE_ROWS, LANES), F32),
                        pltpu.SemaphoreType.DMA((2,))],
    )
    return pl.pallas_call(
        _combine_kernel,
        grid_spec=grid_spec,
        out_shape=jax.ShapeDtypeStruct((nt, d), F32),
        compiler_params=_cparams(("arbitrary",)),
        name="combine",
    )(dest_flat, ys, x1_flat, meta, mods, g_final.reshape(1, d))


def _slots_kernel(pstart_ref, metat_ref, dest_ref):
    eid = metat_ref[0:2, :]
    slot = metat_ref[2:4, :]
    for e in range(N_EXPERTS):
        slot = slot + jnp.where(eid == float(e), pstart_ref[e].astype(F32), 0.0)
    dest_ref[...] = slot.astype(jnp.int32)


def _dispatch_plan(metat, counts_row, nt):
    counts = counts_row[N_GROUPS:N_GROUPS + N_EXPERTS].astype(jnp.int32)
    pcounts = ((counts + MOE_BLK - 1) // MOE_BLK) * MOE_BLK
    pends = jnp.cumsum(pcounts)
    pstarts = pends - pcounts
    dest = pl.pallas_call(
        _slots_kernel,
        grid_spec=pltpu.PrefetchScalarGridSpec(
            num_scalar_prefetch=1, grid=(1,),
            in_specs=[pl.BlockSpec(metat.shape, lambda i, ps: (0, 0))],
            out_specs=pl.BlockSpec((2, nt), lambda i, ps: (0, 0))),
        out_shape=jax.ShapeDtypeStruct((2, nt), jnp.int32),
        compiler_params=_cparams(("arbitrary",)),
        name="slots",
    )(pstarts.astype(jnp.int32), metat).reshape(-1)
    nblk = (nt * 2) // MOE_BLK + N_EXPERTS
    first_slot = jnp.arange(nblk, dtype=jnp.int32) * MOE_BLK
    block_expert = jnp.minimum(
        jnp.sum((pends[None, :] <= first_slot[:, None]).astype(jnp.int32), axis=1), N_EXPERTS - 1)
    nused = (pends[-1] // MOE_BLK).astype(jnp.int32).reshape(1)
    seg_end = jnp.sum(jnp.where(block_expert[:, None] == jnp.arange(N_EXPERTS)[None, :],
                                (pstarts + counts)[None, :], 0), axis=1)
    block_valid = jnp.where(first_slot < pends[-1], jnp.clip(seg_end - first_slot, 0, MOE_BLK), 0)
    return (dest, pstarts.astype(jnp.int32), counts, block_expert.astype(jnp.int32), nused,
            block_valid.astype(jnp.int32))


def kernel(x, c, ctx, c_ctx, w_ada, b_ada, g_norm1, w_in, w_fmix, rpb, g_out, w_out, g_norm2,
           w_router_group, w_router_expert, w_gate, w_up, w_down, g_final):
    b, n, d = x.shape
    assert (b, n, d) == (c.shape[0], SEQ, D_MODEL) and w_ada.shape[0] == 1
    nt = b * n

    cond8 = jnp.zeros((8, d), F32).at[0:b].set(c).at[b].set(c_ctx)
    mod = adaln(cond8, w_ada[0], b_ada[0])
    mods = jnp.pad(mod[0:b].reshape(b, N_MOD, d), ((0, 0), (0, 2), (0, 0)))
    mod_ctx = jnp.pad(mod[b].reshape(N_MOD, d), ((0, 2), (0, 0)))

    w_in_b = w_in[0].astype(BF16)
    qr, qp, kr, v, a = in_proj(x, mods, g_norm1[0], w_in_b)
    kc, vc = ctx_proj(ctx, mod_ctx, g_norm1[0], w_in_b[:, D_FOURIER + D_NA:])

    fr = dft_cols(dft_rows(a))
    na = attention(qr, qp, kr, v, kc, vc, bias_tables(rpb[0]))

    w_router = jnp.concatenate(
        [w_router_group[0], w_router_expert[0],
         jnp.zeros((d, LANES - N_GROUPS - N_EXPERTS), F32)], axis=1).astype(BF16)
    x1, h2_tiles, meta, metat, cnt = out_proj(fr, na, x, mods, w_fmix[0].astype(BF16),
                                              w_out[0].astype(BF16), g_out[0], g_norm2[0], w_router)

    dest, pstarts, counts, block_expert, nused, block_valid = _dispatch_plan(metat, cnt[0], nt)
    xs_tiles = dispatch(h2_tiles, dest, pstarts, counts, block_expert.shape[0] * MOE_BLK)
    ys_tiles = experts(xs_tiles, block_expert, nused, block_valid, w_gate[0], w_up[0], w_down[0])
    out = combine(dest, ys_tiles, x1.reshape(nt, d), meta, mods, g_final, n)
    return out.reshape(b, n, d)
```
